```python
import jax, jax.numpy as jnp
from jax import lax
import numpy as np

D_MODEL = 1024
BATCH = 8
SEQ = 4096
DEPTH = 4

EPS = 1e-6
N_BRANCH = 4
BRANCH_WIDTH = D_MODEL // 2
GM_HEADS = 4
GM_CHUNK = 128
POOL_WINDOWS = (2, 4, 8, 16)
POOL_GROUP = BRANCH_WIDTH // len(POOL_WINDOWS)
ATT_HEADS = 4
ATT_HEAD_DIM = BRANCH_WIDTH // ATT_HEADS
DIL_PATTERNS = ((128, 1), (512, 4), (2048, 16))
N_DIL = len(DIL_PATTERNS)
ATT_BLOCK = 128
MEM_LEN = 256
MEM_HEADS = 4
MEM_HEAD_DIM = BRANCH_WIDTH // MEM_HEADS
NEG = -1e30

IN_SIZES = (2 * BRANCH_WIDTH, BRANCH_WIDTH,
            BRANCH_WIDTH, BRANCH_WIDTH,
            N_DIL * BRANCH_WIDTH, BRANCH_WIDTH, BRANCH_WIDTH, BRANCH_WIDTH,
            BRANCH_WIDTH, BRANCH_WIDTH,
            N_BRANCH * D_MODEL)
D_IN = sum(IN_SIZES)

kernel_name = "hybrid_gmlp_pool_dilated_attn_mem"


def rms_norm(x, g):
    xf = x.astype(jnp.float32)
    y = xf * lax.rsqrt(jnp.mean(xf * xf, axis=-1, keepdims=True) + EPS)
    return (y * g.astype(jnp.float32)).astype(x.dtype)


def layer_norm(x, g, b):
    xf = x.astype(jnp.float32)
    mu = jnp.mean(xf, axis=-1, keepdims=True)
    var = jnp.mean(jnp.square(xf - mu), axis=-1, keepdims=True)
    y = (xf - mu) * lax.rsqrt(var + EPS)
    return (y * g.astype(jnp.float32) + b.astype(jnp.float32)).astype(x.dtype)


def split_cols(h):
    idx = np.cumsum(np.array(IN_SIZES))[:-1].tolist()
    return jnp.split(h, idx, axis=-1)


def gmlp_spatial_gating(uv, ln_g, ln_b, w_s, b_s):
    u, v = jnp.split(jax.nn.gelu(uv, approximate=False), 2, axis=-1)
    v = layer_norm(v, ln_g, ln_b)
    B_, S_, W = v.shape
    nc = S_ // GM_CHUNK
    vh = v.reshape(B_, nc, GM_CHUNK, GM_HEADS, W // GM_HEADS)
    causal = jnp.tril(jnp.ones((GM_CHUNK, GM_CHUNK), dtype=bool))
    w = jnp.where(causal[None], w_s, jnp.zeros_like(w_s)).astype(v.dtype)
    mixed = jnp.einsum('hts,bcshe->bcthe', w, vh) + b_s.T.astype(v.dtype)[None, None, :, :, None]
    return u * mixed.reshape(B_, S_, W)


def multiscale_pool(p, pool_w, pool_scale):
    B_, S_, W = p.shape
    pf = p.astype(jnp.float32)
    cs = jnp.cumsum(pf, axis=1)
    count = jnp.arange(1, S_ + 1, dtype=jnp.float32)
    outs = []
    for g, win in enumerate(POOL_WINDOWS):
        c = cs[..., g * POOL_GROUP:(g + 1) * POOL_GROUP]
        prev = jnp.pad(c, ((0, 0), (win, 0), (0, 0)))[:, :S_]
        mean = (c - prev) / jnp.minimum(count, float(win))[None, :, None]
        outs.append(mean - pf[..., g * POOL_GROUP:(g + 1) * POOL_GROUP])
    d = jnp.stack(outs, axis=2)
    y = jnp.einsum('bsgi,gio->bsgo', d, pool_w.astype(jnp.float32)).reshape(B_, S_, W)
    return (y * pool_scale.astype(jnp.float32)).astype(p.dtype)


def dilated_window_attention(q, k, v, window, dilation):
    B_, S_, H, E = q.shape
    n_back = window // dilation
    L = S_ // dilation
    nb = -(-L // ATT_BLOCK)
    Lp = nb * ATT_BLOCK

    def to_blocks(t):
        t = t.reshape(B_, L, dilation, H, E).transpose(0, 2, 1, 3, 4)
        t = jnp.pad(t, ((0, 0), (0, 0), (0, Lp - L), (0, 0), (0, 0)))
        return t.reshape(B_, dilation, nb, ATT_BLOCK, H, E)

    def with_prev(t):
        prev = jnp.pad(t, ((0, 0), (0, 0), (1, 0), (0, 0), (0, 0), (0, 0)))[:, :, :nb]
        return jnp.concatenate([prev, t], axis=3)

    qb = to_blocks(q).astype(jnp.float32)
    kk = with_prev(to_blocks(k)).astype(jnp.float32)
    vv = with_prev(to_blocks(v)).astype(jnp.float32)
    s = jnp.einsum('brnqhe,brnkhe->brnhqk', qb, kk) * (E ** -0.5)
    i = jnp.arange(ATT_BLOCK)[:, None]
    j = jnp.arange(2 * ATT_BLOCK)[None, :]
    dist = ATT_BLOCK + i - j
    band = (dist >= 0) & (dist <= n_back)
    key_exists = (jnp.arange(nb)[:, None] > 0) | (jnp.arange(2 * ATT_BLOCK)[None, :] >= ATT_BLOCK)
    valid = band[None] & key_exists[:, None, :]
    s = jnp.where(valid[None, None, :, None], s, NEG)
    m = jnp.max(s, axis=-1, keepdims=True)
    e = jnp.exp(s - m)
    den = jnp.sum(e, axis=-1, keepdims=True)
    o = jnp.einsum('brnhqk,brnkhe->brnqhe', e / den, vv)
    lse = (m + jnp.log(den))[..., 0]
    o = o.reshape(B_, dilation, Lp, H, E)[:, :, :L].transpose(0, 2, 1, 3, 4).reshape(B_, S_, H, E)
    lse = lse.transpose(0, 1, 2, 4, 3).reshape(B_, dilation, Lp, H)[:, :, :L]
    lse = lse.transpose(0, 2, 1, 3).reshape(B_, S_, H)
    return o, lse


def dilated_mixture(c_q, c_k, c_v):
    B_, S_, _ = c_q.shape
    qg = c_q.reshape(B_, S_, N_DIL, ATT_HEADS, ATT_HEAD_DIM)
    k = c_k.reshape(B_, S_, ATT_HEADS, ATT_HEAD_DIM)
    v = c_v.reshape(B_, S_, ATT_HEADS, ATT_HEAD_DIM)
    outs, lses = [], []
    for g, (win, dil) in enumerate(DIL_PATTERNS):
        o, l = dilated_window_attention(qg[:, :, g], k, v, win, dil)
        outs.append(o)
        lses.append(l)
    alpha = jax.nn.softmax(jnp.stack(lses, axis=0), axis=0)
    o = jnp.sum(alpha[..., None] * jnp.stack(outs, axis=0), axis=0)
    return o.reshape(B_, S_, BRANCH_WIDTH).astype(c_q.dtype)


def memory_attention(m_q, mem_n, w_kv):
    B_, S_, _ = m_q.shape
    q = m_q.reshape(B_, S_, MEM_HEADS, MEM_HEAD_DIM).astype(jnp.float32)
    k, v = jnp.split(mem_n @ w_kv, 2, axis=-1)
    k = k.reshape(B_, -1, MEM_HEADS, MEM_HEAD_DIM).astype(jnp.float32)
    v = v.reshape(B_, -1, MEM_HEADS, MEM_HEAD_DIM).astype(jnp.float32)
    s = jnp.einsum('bshe,bmhe->bhsm', q, k) * (MEM_HEAD_DIM ** -0.5)
    p = jax.nn.softmax(s, axis=-1)
    o = jnp.einsum('bhsm,bmhe->bshe', p, v)
    return o.reshape(B_, S_, BRANCH_WIDTH).astype(m_q.dtype)


def _fwd_setup_inputs(seed: int = 0) -> dict:
    key = jax.random.key(seed)
    ks = jax.random.split(key, 20)
    f32 = jnp.float32
    nrm = lambda k, shape, s: jax.random.normal(k, shape, f32) * s
    return {
        "x": nrm(ks[0], (BATCH, SEQ, D_MODEL), 1.0),
        "mem": nrm(ks[1], (BATCH, MEM_LEN, D_MODEL), 1.0),
        "norm_g": 1.0 + nrm(ks[2], (DEPTH, D_MODEL), 0.02),
        "w_in": nrm(ks[3], (DEPTH, D_MODEL, D_IN), D_MODEL ** -0.5),
        "gm_ln_g": 1.0 + nrm(ks[4], (DEPTH, BRANCH_WIDTH), 0.02),
        "gm_ln_b": nrm(ks[5], (DEPTH, BRANCH_WIDTH), 0.02),
        "gm_ws": nrm(ks[6], (DEPTH, GM_HEADS, GM_CHUNK, GM_CHUNK), GM_CHUNK ** -0.5),
        "gm_bs": 1.0 + nrm(ks[7], (DEPTH, GM_HEADS, GM_CHUNK), 0.1),
        "pool_w": nrm(ks[8], (DEPTH, len(POOL_WINDOWS), POOL_GROUP, POOL_GROUP), POOL_GROUP ** -0.5),
        "pool_scale": 1.0 + nrm(ks[9], (DEPTH, BRANCH_WIDTH), 0.1),
        "mem_norm_g": 1.0 + nrm(ks[10], (DEPTH, D_MODEL), 0.02),
        "w_mem_kv": nrm(ks[11], (DEPTH, D_MODEL, 2 * BRANCH_WIDTH), D_MODEL ** -0.5),
        "w_branch": nrm(ks[12], (DEPTH, N_BRANCH, BRANCH_WIDTH, D_MODEL), BRANCH_WIDTH ** -0.5),
        "w_out": nrm(ks[13], (DEPTH, D_MODEL, D_MODEL), 0.5 * D_MODEL ** -0.5),
        "final_norm_g": 1.0 + nrm(ks[14], (D_MODEL,), 0.02),
    }


def _fwd_reference(x, mem, norm_g, w_in, gm_ln_g, gm_ln_b, gm_ws, gm_bs, pool_w, pool_scale,
              mem_norm_g, w_mem_kv, w_branch, w_out, final_norm_g):
    B_, S_, D = x.shape
    for l in range(DEPTH):
        h = rms_norm(x, norm_g[l])
        proj = h @ w_in[l]
        (a_uv, a_gate, p_in, p_gate, c_q, c_k, c_v, c_gate,
         m_q, m_gate, g_merge) = split_cols(proj)
        y_a = gmlp_spatial_gating(a_uv, gm_ln_g[l], gm_ln_b[l], gm_ws[l], gm_bs[l]) * jax.nn.silu(a_gate)
        y_p = multiscale_pool(p_in, pool_w[l], pool_scale[l]) * jax.nn.silu(p_gate)
        y_c = dilated_mixture(c_q, c_k, c_v) * jax.nn.silu(c_gate)
        mem_n = rms_norm(mem, mem_norm_g[l])
        y_m = memory_attention(m_q, mem_n, w_mem_kv[l]) * jax.nn.silu(m_gate)
        gates = jax.nn.sigmoid(g_merge.reshape(B_, S_, N_BRANCH, D))
        z = (gates[:, :, 0] * (y_a @ w_branch[l, 0])
             + gates[:, :, 1] * (y_p @ w_branch[l, 1])
             + gates[:, :, 2] * (y_c @ w_branch[l, 2])
             + gates[:, :, 3] * (y_m @ w_branch[l, 3]))
        x = x + z @ w_out[l]
    return rms_norm(x, final_norm_g)


import jax as _jax
import jax.numpy as _jnp

TWIN_FORMAT = 'train_step'
FWD_PARAMS = ['x', 'mem', 'norm_g', 'w_in', 'gm_ln_g', 'gm_ln_b', 'gm_ws', 'gm_bs', 'pool_w', 'pool_scale', 'mem_norm_g', 'w_mem_kv', 'w_branch', 'w_out', 'final_norm_g']
TWIN_WEIGHTS = ['norm_g', 'w_in', 'gm_ln_g', 'gm_ln_b', 'gm_ws', 'gm_bs', 'pool_w', 'pool_scale', 'mem_norm_g', 'w_mem_kv', 'w_branch', 'w_out', 'final_norm_g']
TWIN_DIFF_INPUT = 'x'
TWIN_INPUTS = ['x', 'mem', 'norm_g', 'w_in', 'gm_ln_g', 'gm_ln_b', 'gm_ws', 'gm_bs', 'pool_w', 'pool_scale', 'mem_norm_g', 'w_mem_kv', 'w_branch', 'w_out', 'final_norm_g', 'loss_target', 'm_norm_g', 'm_w_in', 'm_gm_ln_g', 'm_gm_ln_b', 'm_gm_ws', 'm_gm_bs', 'm_pool_w', 'm_pool_scale', 'm_mem_norm_g', 'm_w_mem_kv', 'm_w_branch', 'm_w_out', 'm_final_norm_g', 'v_norm_g', 'v_w_in', 'v_gm_ln_g', 'v_gm_ln_b', 'v_gm_ws', 'v_gm_bs', 'v_pool_w', 'v_pool_scale', 'v_mem_norm_g', 'v_w_mem_kv', 'v_w_branch', 'v_w_out', 'v_final_norm_g']
TWIN_OUTPUTS = ['loss', 'grad_x', 'grad_norm_g', 'grad_w_in', 'grad_gm_ln_g', 'grad_gm_ln_b', 'grad_gm_ws', 'grad_gm_bs', 'grad_pool_w', 'grad_pool_scale', 'grad_mem_norm_g', 'grad_w_mem_kv', 'grad_w_branch', 'grad_w_out', 'grad_final_norm_g', 'delta_norm_g', 'delta_w_in', 'delta_gm_ln_g', 'delta_gm_ln_b', 'delta_gm_ws', 'delta_gm_bs', 'delta_pool_w', 'delta_pool_scale', 'delta_mem_norm_g', 'delta_w_mem_kv', 'delta_w_branch', 'delta_w_out', 'delta_final_norm_g', 'new_m_norm_g', 'new_m_w_in', 'new_m_gm_ln_g', 'new_m_gm_ln_b', 'new_m_gm_ws', 'new_m_gm_bs', 'new_m_pool_w', 'new_m_pool_scale', 'new_m_mem_norm_g', 'new_m_w_mem_kv', 'new_m_w_branch', 'new_m_w_out', 'new_m_final_norm_g', 'new_v_norm_g', 'new_v_w_in', 'new_v_gm_ln_g', 'new_v_gm_ln_b', 'new_v_gm_ws', 'new_v_gm_bs', 'new_v_pool_w', 'new_v_pool_scale', 'new_v_mem_norm_g', 'new_v_w_mem_kv', 'new_v_w_branch', 'new_v_w_out', 'new_v_final_norm_g']
TWIN_LEAF_KINDS = {'loss': 'loss', 'grad_x': 'grad_x', 'grad_norm_g': 'grad_w', 'grad_w_in': 'grad_w', 'grad_gm_ln_g': 'grad_w', 'grad_gm_ln_b': 'grad_w', 'grad_gm_ws': 'grad_w', 'grad_gm_bs': 'grad_w', 'grad_pool_w': 'grad_w', 'grad_pool_scale': 'grad_w', 'grad_mem_norm_g': 'grad_w', 'grad_w_mem_kv': 'grad_w', 'grad_w_branch': 'grad_w', 'grad_w_out': 'grad_w', 'grad_final_norm_g': 'grad_w', 'delta_norm_g': 'delta_w', 'delta_w_in': 'delta_w', 'delta_gm_ln_g': 'delta_w', 'delta_gm_ln_b': 'delta_w', 'delta_gm_ws': 'delta_w', 'delta_gm_bs': 'delta_w', 'delta_pool_w': 'delta_w', 'delta_pool_scale': 'delta_w', 'delta_mem_norm_g': 'delta_w', 'delta_w_mem_kv': 'delta_w', 'delta_w_branch': 'delta_w', 'delta_w_out': 'delta_w', 'delta_final_norm_g': 'delta_w', 'new_m_norm_g': 'new_m', 'new_m_w_in': 'new_m', 'new_m_gm_ln_g': 'new_m', 'new_m_gm_ln_b': 'new_m', 'new_m_gm_ws': 'new_m', 'new_m_gm_bs': 'new_m', 'new_m_pool_w': 'new_m', 'new_m_pool_scale': 'new_m', 'new_m_mem_norm_g': 'new_m', 'new_m_w_mem_kv': 'new_m', 'new_m_w_branch': 'new_m', 'new_m_w_out': 'new_m', 'new_m_final_norm_g': 'new_m', 'new_v_norm_g': 'new_v', 'new_v_w_in': 'new_v', 'new_v_gm_ln_g': 'new_v', 'new_v_gm_ln_b': 'new_v', 'new_v_gm_ws': 'new_v', 'new_v_gm_bs': 'new_v', 'new_v_pool_w': 'new_v', 'new_v_pool_scale': 'new_v', 'new_v_mem_norm_g': 'new_v', 'new_v_w_mem_kv': 'new_v', 'new_v_w_branch': 'new_v', 'new_v_w_out': 'new_v', 'new_v_final_norm_g': 'new_v'}


def _forward(args):
    return _fwd_reference(*[args[k] for k in FWD_PARAMS])


def _output_shape():
    def fwd():
        inp = _fwd_setup_inputs(0)
        return _fwd_reference(*[inp[k] for k in FWD_PARAMS])
    out = _jax.eval_shape(fwd)
    return out.shape, out.dtype

N_MICROBATCH = 1
ADAM_LR = 0.001
ADAM_B1 = 0.9
ADAM_B2 = 0.999
ADAM_EPS = 1e-08
ADAM_WD = 0.01
ADAM_STEP = 10
PER_EXAMPLE_BATCH_AXIS = {'x': 0, 'mem': 0, 'loss_target': 0}
SHARED_INPUTS = []
_WEIGHT_DTYPES = {'norm_g': _jnp.float32, 'w_in': _jnp.float32, 'gm_ln_g': _jnp.float32, 'gm_ln_b': _jnp.float32, 'gm_ws': _jnp.float32, 'gm_bs': _jnp.float32, 'pool_w': _jnp.float32, 'pool_scale': _jnp.float32, 'mem_norm_g': _jnp.float32, 'w_mem_kv': _jnp.float32, 'w_branch': _jnp.float32, 'w_out': _jnp.float32, 'final_norm_g': _jnp.float32}
MOMENT_SCALE = {'norm_g': 5.411046e-02, 'w_in': 1.672314e-02, 'gm_ln_g': 1.855134e-02, 'gm_ln_b': 1.894680e-02, 'gm_ws': 1.868963e-02, 'gm_bs': 2.639605e-02, 'pool_w': 3.532342e-02, 'pool_scale': 3.527986e-02, 'mem_norm_g': 4.392786e-03, 'w_mem_kv': 4.244768e-03, 'w_branch': 1.718844e-02, 'w_out': 6.867106e-02, 'final_norm_g': 3.200878e+01}


def _to_microbatches(a, axis):
    t = _jnp.moveaxis(a, axis, 0)
    t = t.reshape((N_MICROBATCH, t.shape[0] // N_MICROBATCH) + t.shape[1:])
    return _jnp.moveaxis(t, 1, axis + 1)


def setup_inputs(seed: int = 0) -> dict:
    inp = _fwd_setup_inputs(seed)
    key = _jax.random.fold_in(_jax.random.key(seed), 7919)
    shape, _ = _output_shape()
    out = dict(inp)
    out["loss_target"] = _jax.random.normal(_jax.random.fold_in(key, 0), shape, _jnp.float32)
    for i, name in enumerate(TWIN_WEIGHTS):
        w = inp[name].astype(_jnp.float32)
        if MOMENT_SCALE is None:
            s = _jnp.sqrt(_jnp.mean(_jnp.square(w)) + 1e-30)
        else:
            s = MOMENT_SCALE[name]
        km, kv = _jax.random.split(_jax.random.fold_in(key, i + 1))
        out[name] = w
        out["m_" + name] = s * _jax.random.normal(km, w.shape, _jnp.float32)
        out["v_" + name] = (s * s) * _jax.random.uniform(kv, w.shape, _jnp.float32, 0.5, 1.5)
    if N_MICROBATCH > 1:
        for name, axis in PER_EXAMPLE_BATCH_AXIS.items():
            out[name] = _to_microbatches(out[name], axis)
    return {'x': out['x'], 'mem': out['mem'], 'norm_g': out['norm_g'], 'w_in': out['w_in'], 'gm_ln_g': out['gm_ln_g'], 'gm_ln_b': out['gm_ln_b'], 'gm_ws': out['gm_ws'], 'gm_bs': out['gm_bs'], 'pool_w': out['pool_w'], 'pool_scale': out['pool_scale'], 'mem_norm_g': out['mem_norm_g'], 'w_mem_kv': out['w_mem_kv'], 'w_branch': out['w_branch'], 'w_out': out['w_out'], 'final_norm_g': out['final_norm_g'], 'loss_target': out['loss_target'], 'm_norm_g': out['m_norm_g'], 'm_w_in': out['m_w_in'], 'm_gm_ln_g': out['m_gm_ln_g'], 'm_gm_ln_b': out['m_gm_ln_b'], 'm_gm_ws': out['m_gm_ws'], 'm_gm_bs': out['m_gm_bs'], 'm_pool_w': out['m_pool_w'], 'm_pool_scale': out['m_pool_scale'], 'm_mem_norm_g': out['m_mem_norm_g'], 'm_w_mem_kv': out['m_w_mem_kv'], 'm_w_branch': out['m_w_branch'], 'm_w_out': out['m_w_out'], 'm_final_norm_g': out['m_final_norm_g'], 'v_norm_g': out['v_norm_g'], 'v_w_in': out['v_w_in'], 'v_gm_ln_g': out['v_gm_ln_g'], 'v_gm_ln_b': out['v_gm_ln_b'], 'v_gm_ws': out['v_gm_ws'], 'v_gm_bs': out['v_gm_bs'], 'v_pool_w': out['v_pool_w'], 'v_pool_scale': out['v_pool_scale'], 'v_mem_norm_g': out['v_mem_norm_g'], 'v_w_mem_kv': out['v_w_mem_kv'], 'v_w_branch': out['v_w_branch'], 'v_w_out': out['v_w_out'], 'v_final_norm_g': out['v_final_norm_g']}


def _loss(weights, diff, rest, loss_target):
    with _jax.named_scope("forward"):
        args = {**rest, TWIN_DIFF_INPUT: diff, **{k: w.astype(_WEIGHT_DTYPES[k]) for k, w in weights.items()}}
        y = _forward(args)
    with _jax.named_scope("loss_head"):
        err = _jnp.square(y.astype(_jnp.float32) - loss_target)
        return 0.5 * _jnp.sum(_jnp.mean(err, axis=-1)) if err.ndim else 0.5 * err


def _adamw(w, g, m, v):
    m = ADAM_B1 * m + (1.0 - ADAM_B1) * g
    v = ADAM_B2 * v + (1.0 - ADAM_B2) * _jnp.square(g)
    m_hat = m / (1.0 - ADAM_B1 ** ADAM_STEP)
    v_hat = v / (1.0 - ADAM_B2 ** ADAM_STEP)
    delta = -ADAM_LR * (m_hat / (_jnp.sqrt(v_hat) + ADAM_EPS) + ADAM_WD * w)
    return delta, m, v


def reference(x, mem, norm_g, w_in, gm_ln_g, gm_ln_b, gm_ws, gm_bs, pool_w, pool_scale, mem_norm_g, w_mem_kv, w_branch, w_out, final_norm_g, loss_target, m_norm_g, m_w_in, m_gm_ln_g, m_gm_ln_b, m_gm_ws, m_gm_bs, m_pool_w, m_pool_scale, m_mem_norm_g, m_w_mem_kv, m_w_branch, m_w_out, m_final_norm_g, v_norm_g, v_w_in, v_gm_ln_g, v_gm_ln_b, v_gm_ws, v_gm_bs, v_pool_w, v_pool_scale, v_mem_norm_g, v_w_mem_kv, v_w_branch, v_w_out, v_final_norm_g):
    given = dict(x=x, mem=mem, norm_g=norm_g, w_in=w_in, gm_ln_g=gm_ln_g, gm_ln_b=gm_ln_b, gm_ws=gm_ws, gm_bs=gm_bs, pool_w=pool_w, pool_scale=pool_scale, mem_norm_g=mem_norm_g, w_mem_kv=w_mem_kv, w_branch=w_branch, w_out=w_out, final_norm_g=final_norm_g, loss_target=loss_target, m_norm_g=m_norm_g, m_w_in=m_w_in, m_gm_ln_g=m_gm_ln_g, m_gm_ln_b=m_gm_ln_b, m_gm_ws=m_gm_ws, m_gm_bs=m_gm_bs, m_pool_w=m_pool_w, m_pool_scale=m_pool_scale, m_mem_norm_g=m_mem_norm_g, m_w_mem_kv=m_w_mem_kv, m_w_branch=m_w_branch, m_w_out=m_w_out, m_final_norm_g=m_final_norm_g, v_norm_g=v_norm_g, v_w_in=v_w_in, v_gm_ln_g=v_gm_ln_g, v_gm_ln_b=v_gm_ln_b, v_gm_ws=v_gm_ws, v_gm_bs=v_gm_bs, v_pool_w=v_pool_w, v_pool_scale=v_pool_scale, v_mem_norm_g=v_mem_norm_g, v_w_mem_kv=v_w_mem_kv, v_w_branch=v_w_branch, v_w_out=v_w_out, v_final_norm_g=v_final_norm_g)
    weights = {n: given[n] for n in TWIN_WEIGHTS}
    shared = {n: given[n] for n in SHARED_INPUTS}
    per_example = {n: given[n] for n in ['x', 'mem']}
    grad_fn = _jax.value_and_grad(_loss, argnums=(0, 1))

    def one_microbatch(ex, loss_target):
        ex = dict(ex)
        diff = ex.pop(TWIN_DIFF_INPUT)
        return grad_fn(weights, diff, {**shared, **ex}, loss_target)

    if N_MICROBATCH == 1:
        loss, (grad_w, grad_x) = one_microbatch(per_example, given["loss_target"])
    else:
        def body(carry, xs):
            loss_sum, grad_sum = carry
            l_k, (gw_k, gx_k) = one_microbatch(xs[0], xs[1])
            with _jax.named_scope("update"):
                return (loss_sum + l_k, _jax.tree.map(_jnp.add, grad_sum, gw_k)), gx_k

        init = (_jnp.zeros((), _jnp.float32), _jax.tree.map(_jnp.zeros_like, weights))
        (loss, grad_w), grad_x = _jax.lax.scan(body, init, (per_example, given["loss_target"]))
    with _jax.named_scope("update"):
        delta_w, new_m, new_v = {}, {}, {}
        for n in TWIN_WEIGHTS:
            delta_w[n], new_m[n], new_v[n] = _adamw(weights[n], grad_w[n], given["m_" + n], given["v_" + n])
    return (loss, grad_x, *[grad_w[n] for n in TWIN_WEIGHTS], *[delta_w[n] for n in TWIN_WEIGHTS],
            *[new_m[n] for n in TWIN_WEIGHTS], *[new_v[n] for n in TWIN_WEIGHTS])
```

```python
import functools
import math

import jax
import jax.numpy as jnp
from jax import lax
from jax.experimental import pallas as pl
from jax.experimental.pallas import tpu as pltpu

F32 = jnp.float32
BF16 = jnp.bfloat16

S = 4096
D = 1024
W = 512
DIN = 10752
NL = 4
NCHIP = 4
NDEV = 8
CW = DIN // NCHIP
TN_IN = 896
NJ = CW // TN_IN
HD = 128
EPS = 1e-6
NEG = -1e30
SCALE = HD ** -0.5
INV_SQRT2 = 1.0 / math.sqrt(2.0)
INV_SQRT2PI = 1.0 / math.sqrt(2.0 * math.pi)
POOL_WINDOWS = (2, 4, 8, 16)
DILATIONS = (1, 4, 16)
HALO = 16
NPIECE = DIN // W
P_AGATE, P_PIN, P_PGATE, P_CQ, P_CK, P_CV, P_CGATE, P_MQ, P_MGATE, P_GM = 2, 3, 4, 5, 8, 9, 10, 11, 12, 13
VMEM_LIMIT = 56 * 1024 * 1024

ADAM_LR, ADAM_B1, ADAM_B2, ADAM_EPS, ADAM_WD, ADAM_STEP = 0.001, 0.9, 0.999, 1e-08, 0.01, 10

MESH = pl.DeviceIdType.MESH
ANY = pl.BlockSpec(memory_space=pl.ANY)


def _cp(*sem):
    return pltpu.CompilerParams(dimension_semantics=sem or None, vmem_limit_bytes=VMEM_LIMIT)


def _sds(shape, dtype):
    return jax.ShapeDtypeStruct(shape, dtype)


def _sigmoid(v):
    return 1.0 / (1.0 + jnp.exp(-v))


def _dot(a, b):
    return jnp.dot(a, b, preferred_element_type=F32)


def _dot_nt(a, b):
    return lax.dot_general(a, b, (((1,), (1,)), ((), ())), preferred_element_type=F32)


def _dot_tn(a, b):
    return lax.dot_general(a, b, (((0,), (0,)), ((), ())), preferred_element_type=F32)


def _tile_put(buf, sem, dst_of, step, nsteps, fill):
    slot = step % 2

    def copy(s, st):
        return pltpu.make_async_copy(buf.at[s], dst_of(st), sem.at[s])

    @pl.when(step >= 2)
    def _():
        copy(slot, step).wait()

    fill(buf.at[slot])
    copy(slot, step).start()

    @pl.when(step == nsteps - 1)
    def _():
        if nsteps >= 2:
            copy(1 - slot, step).wait()
        copy(slot, step).wait()


def _my_pos():
    return lax.axis_index("x"), lax.axis_index("y"), lax.axis_index("c")


_CHIP_REL = ((1, 0), (0, 1), (1, 1))
_DEV_REL = tuple((dx, dy, dc) for dx in (0, 1) for dy in (0, 1) for dc in (0, 1))[1:]


def _gather_weights(win_s, wkv_s, wbr_s, wo_s):
    shards = (win_s, wkv_s, wbr_s, wo_s)
    nk = len(shards)
    half = tuple(s.shape[1] // 2 for s in shards)
    out_shape = []
    for s in shards:
        out_shape += [_sds((NCHIP,) + s.shape[1:], BF16) for _ in range(NL)]
    n_pair = 3 * nk * NL

    def body(*refs):
        ins = refs[:nk]
        outs = refs[nk:nk + nk * NL]
        send1, recv1, send2, recv2, lsem = refs[nk + nk * NL:]
        x, y, c = _my_pos()
        me = 2 * x + y
        sibling = (x, y, 1 - c)
        chips = [(x ^ dx, y ^ dy) for dx, dy in _CHIP_REL]

        def out(k, l):
            return outs[k * NL + l]

        def piece(k, l, chip, hf):
            return out(k, l).at[chip, pl.ds(hf * half[k], half[k])]

        def mine(k, l, hf):
            return ins[k].at[l, pl.ds(hf * half[k], half[k])]

        local = []
        for k in range(nk):
            for l in range(NL):
                cp = pltpu.make_async_copy(ins[k].at[l], out(k, l).at[me], lsem.at[k * NL + l])
                cp.start()
                local.append(cp)

        def first(r, k, l):
            cx, cy = chips[r]
            i = (r * nk + k) * NL + l
            return pltpu.make_async_remote_copy(
                src_ref=mine(k, l, c), dst_ref=piece(k, l, me, c), send_sem=send1.at[i], recv_sem=recv1.at[i],
                device_id=(cx, cy, c), device_id_type=MESH)

        def landed(r, k, l):
            cx, cy = chips[r]
            i = (r * nk + k) * NL + l
            return pltpu.make_async_remote_copy(
                src_ref=mine(k, l, c), dst_ref=piece(k, l, 2 * cx + cy, c), send_sem=send1.at[i], recv_sem=recv1.at[i],
                device_id=(cx, cy, c), device_id_type=MESH)

        def passed(r, k, l, hf):
            cx, cy = chips[r]
            i = (r * nk + k) * NL + l
            ref = piece(k, l, 2 * cx + cy, hf)
            return pltpu.make_async_remote_copy(
                src_ref=ref, dst_ref=ref, send_sem=send2.at[i], recv_sem=recv2.at[i],
                device_id=sibling, device_id_type=MESH)

        idx = [(r, k, l) for r in range(3) for k in range(nk) for l in range(NL)]
        for r, k, l in idx:
            first(r, k, l).start()
        for r, k, l in idx:
            landed(r, k, l).wait_recv()
            passed(r, k, l, c).start()
        for r, k, l in idx:
            passed(r, k, l, 1 - c).wait_recv()
        for r, k, l in idx:
            first(r, k, l).wait_send()
            passed(r, k, l, c).wait_send()
        for cp in local:
            cp.wait()

    res = pl.pallas_call(
        body, out_shape=out_shape, in_specs=[ANY] * nk, out_specs=[ANY] * (nk * NL),
        scratch_shapes=[pltpu.SemaphoreType.DMA((n_pair,))] * 4 + [pltpu.SemaphoreType.DMA((nk * NL,))],
        name="gather_weights")(*shards)
    return [res[k * NL:(k + 1) * NL] for k in range(nk)]


def _exchange_grads(grads):
    nk = len(grads)
    flat = [g for gk in grads for g in gk]
    out_shape = [_sds((NL, NDEV) + gk[0].shape[2:], BF16) for gk in grads]
    n_pair = 7 * nk * NL

    def body(*refs):
        ins = refs[:nk * NL]
        outs = refs[nk * NL:nk * NL + nk]
        send, recv, lsem = refs[nk * NL + nk:]
        x, y, c = _my_pos()
        me_chip = 2 * x + y
        me_dev = 4 * x + 2 * y + c
        local = []
        for k in range(nk):
            for l in range(NL):
                cp = pltpu.make_async_copy(ins[k * NL + l].at[me_chip, c], outs[k].at[l, me_dev], lsem.at[k * NL + l])
                cp.start()
                local.append(cp)

        def copy(r, k, l):
            dx, dy, dc = _DEV_REL[r]
            px, py, pc = x ^ dx, y ^ dy, c ^ dc
            i = (r * nk + k) * NL + l
            return pltpu.make_async_remote_copy(
                src_ref=ins[k * NL + l].at[2 * px + py, pc], dst_ref=outs[k].at[l, me_dev],
                send_sem=send.at[i], recv_sem=recv.at[i], device_id=(px, py, pc), device_id_type=MESH)

        def arrival(r, k, l):
            dx, dy, dc = _DEV_REL[r]
            px, py, pc = x ^ dx, y ^ dy, c ^ dc
            i = (r * nk + k) * NL + l
            return pltpu.make_async_remote_copy(
                src_ref=ins[k * NL + l].at[me_chip, c], dst_ref=outs[k].at[l, 4 * px + 2 * py + pc],
                send_sem=send.at[i], recv_sem=recv.at[i], device_id=(px, py, pc), device_id_type=MESH)

        idx = [(r, k, l) for r in range(7) for k in range(nk) for l in range(NL)]
        for r, k, l in idx:
            copy(r, k, l).start()
        for r, k, l in idx:
            arrival(r, k, l).wait_recv()
        for r, k, l in idx:
            copy(r, k, l).wait_send()
        for cp in local:
            cp.wait()

    return pl.pallas_call(
        body, out_shape=out_shape, in_specs=[ANY] * (nk * NL), out_specs=[ANY] * nk,
        scratch_shapes=[pltpu.SemaphoreType.DMA((n_pair,))] * 2 + [pltpu.SemaphoreType.DMA((nk * NL,))],
        name="exchange_grads")(*flat)


def _exchange_small(packed):
    def body(in_ref, out_ref, send, recv, lsem):
        x, y, c = _my_pos()
        me_dev = 4 * x + 2 * y + c
        loc = pltpu.make_async_copy(in_ref, out_ref.at[me_dev], lsem)
        loc.start()

        def copy(r, slot):
            dx, dy, dc = _DEV_REL[r]
            px, py, pc = x ^ dx, y ^ dy, c ^ dc
            return pltpu.make_async_remote_copy(
                src_ref=in_ref, dst_ref=out_ref.at[slot(px, py, pc)], send_sem=send.at[r], recv_sem=recv.at[r],
                device_id=(px, py, pc), device_id_type=MESH)

        for r in range(7):
            copy(r, lambda px, py, pc: me_dev).start()
        for r in range(7):
            copy(r, lambda px, py, pc: 4 * px + 2 * py + pc).wait_recv()
        for r in range(7):
            copy(r, lambda px, py, pc: me_dev).wait_send()
        loc.wait()

    return pl.pallas_call(
        body, out_shape=_sds((NDEV,) + packed.shape, F32), in_specs=[ANY], out_specs=ANY,
        scratch_shapes=[pltpu.SemaphoreType.DMA((7,)), pltpu.SemaphoreType.DMA((7,)), pltpu.SemaphoreType.DMA(())],
        name="exchange_small")(packed)


def _share_halves(halves):
    nk = len(halves)

    def body(*refs):
        ins = refs[:nk]
        outs = refs[nk:2 * nk]
        send, recv, lsem = refs[2 * nk:]
        x, y, c = _my_pos()
        local = []
        for k in range(nk):
            cp = pltpu.make_async_copy(ins[k], outs[k].at[:, c], lsem.at[k])
            cp.start()
            local.append(cp)

        def copy(k, hf):
            return pltpu.make_async_remote_copy(
                src_ref=ins[k], dst_ref=outs[k].at[:, hf], send_sem=send.at[k], recv_sem=recv.at[k],
                device_id=(x, y, 1 - c), device_id_type=MESH)

        for k in range(nk):
            copy(k, c).start()
        for k in range(nk):
            copy(k, 1 - c).wait_recv()
        for k in range(nk):
            copy(k, c).wait_send()
        for cp in local:
            cp.wait()

    out_shape = [_sds((NL, 2) + h.shape[1:], F32) for h in halves]
    return pl.pallas_call(
        body, out_shape=out_shape, in_specs=[ANY] * nk, out_specs=[ANY] * nk,
        scratch_shapes=[pltpu.SemaphoreType.DMA((nk,))] * 3, name="share_halves")(*halves)


def _sum_slots(recv, tr):
    nl, _, r, ccols = recv.shape

    def body(r_ref, o_ref):
        acc = r_ref[0].astype(F32)
        for s in range(1, NDEV):
            acc = acc + r_ref[s].astype(F32)
        o_ref[...] = acc

    return pl.pallas_call(
        body, grid=(nl, r // tr),
        in_specs=[pl.BlockSpec((None, NDEV, tr, ccols), lambda l, i: (l, 0, i, 0))],
        out_specs=pl.BlockSpec((None, tr, ccols), lambda l, i: (l, i, 0)),
        out_shape=_sds((nl, r, ccols), F32), compiler_params=_cp("parallel", "parallel"), name="sum_slots")(recv)


def _rms_fwd(x, g, tm):
    n = x.shape[0]

    def body(x_ref, g_ref, h_ref):
        xv = x_ref[...]
        r = lax.rsqrt(jnp.mean(xv * xv, axis=-1, keepdims=True) + EPS)
        h_ref[...] = (xv * r * g_ref[...]).astype(BF16)

    return pl.pallas_call(
        body, grid=(n // tm,),
        in_specs=[pl.BlockSpec((tm, D), lambda i: (i, 0)), pl.BlockSpec((1, D), lambda i: (0, 0))],
        out_specs=pl.BlockSpec((tm, D), lambda i: (i, 0)), out_shape=_sds((n, D), BF16),
        compiler_params=_cp("parallel"), name="rms_fwd")(x, g)


def _rms_bwd(dh, x, g, dres, tm=512):
    n = x.shape[0]

    def body(dh_ref, x_ref, g_ref, dr_ref, dx_ref, dg_ref):
        i = pl.program_id(0)
        xv = x_ref[...]
        r = lax.rsqrt(jnp.mean(xv * xv, axis=-1, keepdims=True) + EPS)
        xh = xv * r
        dhv = dh_ref[...]
        dxh = dhv * g_ref[...]
        dx_ref[...] = dr_ref[...] + r * (dxh - xh * jnp.mean(dxh * xh, axis=-1, keepdims=True))
        part = jnp.sum(dhv * xh, axis=0, keepdims=True)

        @pl.when(i == 0)
        def _():
            dg_ref[...] = part

        @pl.when(i > 0)
        def _():
            dg_ref[...] += part

    row = pl.BlockSpec((tm, D), lambda i: (i, 0))
    vec = pl.BlockSpec((1, D), lambda i: (0, 0))
    return pl.pallas_call(
        body, grid=(n // tm,), in_specs=[row, row, vec, row], out_specs=[row, vec],
        out_shape=[_sds((n, D), F32), _sds((1, D), F32)], compiler_params=_cp("arbitrary"), name="rms_bwd")(dh, x, g, dres)


def _loss_head(x, tgt, g, tm=512):
    def body(x_ref, t_ref, g_ref, dx_ref, dg_ref, ls_ref):
        i = pl.program_id(0)
        xv = x_ref[...]
        r = lax.rsqrt(jnp.mean(xv * xv, axis=-1, keepdims=True) + EPS)
        xh = xv * r
        gv = g_ref[...]
        diff = xh * gv - t_ref[...]
        dy = diff * (1.0 / D)
        dxh = dy * gv
        dx_ref[...] = r * (dxh - xh * jnp.mean(dxh * xh, axis=-1, keepdims=True))
        part_g = jnp.sum(dy * xh, axis=0, keepdims=True)
        part_l = jnp.sum(diff * diff, axis=0, keepdims=True)

        @pl.when(i == 0)
        def _():
            dg_ref[...] = part_g
            ls_ref[...] = part_l

        @pl.when(i > 0)
        def _():
            dg_ref[...] += part_g
            ls_ref[...] += part_l

        @pl.when(i == pl.num_programs(0) - 1)
        def _():
            tot = jnp.sum(ls_ref[...], axis=-1, keepdims=True) * (0.5 / D)
            ls_ref[...] = jnp.broadcast_to(tot, (1, D))

    row = pl.BlockSpec((tm, D), lambda i: (i, 0))
    vec = pl.BlockSpec((1, D), lambda i: (0, 0))
    return pl.pallas_call(
        body, grid=(S // tm,), in_specs=[row, row, vec], out_specs=[row, vec, vec],
        out_shape=[_sds((S, D), F32), _sds((1, D), F32), _sds((1, D), F32)],
        compiler_params=_cp("arbitrary"), name="loss_head")(x, tgt, g)


def _adamw(w, g, m, v):
    shape = w.shape
    cols = shape[-1] if w.ndim > 1 else shape[0]
    rows = w.size // cols
    w2, g2, m2, v2 = (t.reshape(rows, cols) for t in (w, g, m, v))
    tr = rows
    while tr * cols * 4 > (1 << 20) and tr % 16 == 0:
        tr //= 2
    c1 = 1.0 - ADAM_B1 ** ADAM_STEP
    c2 = 1.0 - ADAM_B2 ** ADAM_STEP

    def body(w_ref, g_ref, m_ref, v_ref, d_ref, nm_ref, nv_ref):
        gv = g_ref[...]
        mn = ADAM_B1 * m_ref[...] + (1.0 - ADAM_B1) * gv
        vn = ADAM_B2 * v_ref[...] + (1.0 - ADAM_B2) * (gv * gv)
        d_ref[...] = -ADAM_LR * ((mn / c1) / (jnp.sqrt(vn / c2) + ADAM_EPS) + ADAM_WD * w_ref[...])
        nm_ref[...] = mn
        nv_ref[...] = vn

    blk = pl.BlockSpec((tr, cols), lambda i: (i, 0))
    outs = pl.pallas_call(
        body, grid=(rows // tr,), in_specs=[blk] * 4, out_specs=[blk] * 3,
        out_shape=[_sds((rows, cols), F32)] * 3, compiler_params=_cp("parallel"), name="adamw")(w2, g2, m2, v2)
    return tuple(o.reshape(shape) for o in outs)


def _proj_fwd(h, wg, tm=1024):
    def body(h_ref, w_ref, o_ref):
        o_ref[...] = _dot(h_ref[...], w_ref[...])

    return pl.pallas_call(
        body, grid=(NCHIP, NJ, S // tm),
        in_specs=[pl.BlockSpec((tm, D), lambda c, j, i: (i, 0)), pl.BlockSpec((None, D, TN_IN), lambda c, j, i: (c, 0, j))],
        out_specs=pl.BlockSpec((tm, TN_IN), lambda c, j, i: (i, c * NJ + j)), out_shape=_sds((S, DIN), F32),
        compiler_params=_cp("parallel", "parallel", "parallel"), name="proj_fwd")(h, wg)


def _proj_bwd_x(dproj, wg, tm=1024):
    nk = NCHIP * NJ

    def body(d_ref, w_ref, o_ref):
        k = pl.program_id(1)
        part = _dot_nt(d_ref[...], w_ref[...])

        @pl.when(k == 0)
        def _():
            o_ref[...] = part

        @pl.when(k > 0)
        def _():
            o_ref[...] += part

    return pl.pallas_call(
        body, grid=(S // tm, nk),
        in_specs=[pl.BlockSpec((tm, TN_IN), lambda i, k: (i, k)),
                  pl.BlockSpec((None, D, TN_IN), lambda i, k: (k // NJ, 0, k % NJ))],
        out_specs=pl.BlockSpec((tm, D), lambda i, k: (i, 0)), out_shape=_sds((S, D), F32),
        compiler_params=_cp("parallel", "arbitrary"), name="proj_bwd_x")(dproj, wg)


def _proj_bwd_w(h, dproj, tk=1024):
    nk = S // tk

    def body(h_ref, d_ref, o_ref, acc):
        k = pl.program_id(2)
        part = _dot_tn(h_ref[...], d_ref[...])

        @pl.when(k == 0)
        def _():
            acc[...] = part

        @pl.when(k > 0)
        def _():
            acc[...] += part

        @pl.when(k == nk - 1)
        def _():
            o_ref[...] = acc[...].astype(BF16)

    return pl.pallas_call(
        body, grid=(NCHIP, NJ, nk),
        in_specs=[pl.BlockSpec((tk, D), lambda c, j, k: (k, 0)), pl.BlockSpec((tk, TN_IN), lambda c, j, k: (k, c * NJ + j))],
        out_specs=pl.BlockSpec((None, D, TN_IN), lambda c, j, k: (c, 0, j)), out_shape=_sds((NCHIP, D, CW), BF16),
        scratch_shapes=[pltpu.VMEM((D, TN_IN), F32)],
        compiler_params=_cp("parallel", "parallel", "arbitrary"), name="proj_bwd_w")(h, dproj)


def _merge_fwd(y_all, wbr, proj, tm=512):
    cb = D // NCHIP

    def body(y0, y1, y2, y3, w_ref, g0, g1, g2, g3, z_ref):
        acc = None
        for b, (y_ref, g_ref) in enumerate(zip((y0, y1, y2, y3), (g0, g1, g2, g3))):
            t = _dot(y_ref[...], w_ref[b]) * _sigmoid(g_ref[...])
            acc = t if acc is None else acc + t
        z_ref[...] = acc.astype(BF16)

    y_specs = [pl.BlockSpec((None, tm, W), functools.partial(lambda b, c, i: (b, i, 0), b)) for b in range(4)]
    g_specs = [pl.BlockSpec((tm, cb), functools.partial(lambda b, c, i: (i, (P_GM * W + b * D) // cb + c), b)) for b in range(4)]
    return pl.pallas_call(
        body, grid=(NCHIP, S // tm),
        in_specs=y_specs + [pl.BlockSpec((None, 4, W, cb), lambda c, i: (c, 0, 0, 0))] + g_specs,
        out_specs=pl.BlockSpec((tm, cb), lambda c, i: (i, c)), out_shape=_sds((S, D), BF16),
        compiler_params=_cp("parallel", "parallel"), name="merge_fwd")(y_all, y_all, y_all, y_all, wbr, proj, proj, proj, proj)


def _merge_bwd(dz, y_all, wbr, proj, dproj, tm=512):
    cb = D // NCHIP
    ni = S // tm

    def body(dz_ref, y_ref, w_ref, ga_ref, gb_ref, dp_in, dp_ref, dy_ref, dw_ref, acc, obuf, osem):
        b = pl.program_id(0)
        i = pl.program_id(1)
        yv = y_ref[...]
        dys = []

        def fill(slot):
            dy = None
            for c in range(NCHIP):
                wv = w_ref[c]
                t = _dot(yv, wv)
                g_ref = ga_ref if c < 2 else gb_ref
                g = _sigmoid(g_ref[:, (c % 2) * cb:(c % 2 + 1) * cb])
                dzc = dz_ref[:, c * cb:(c + 1) * cb]
                slot[:, c * cb:(c + 1) * cb] = (dzc * t * g * (1.0 - g)).astype(BF16)
                dt = (dzc * g).astype(BF16)
                part = _dot_nt(dt, wv)
                dy = part if dy is None else dy + part
                dwp = _dot_tn(yv, dt)

                @pl.when(i == 0)
                def _():
                    acc[c] = dwp

                @pl.when(i > 0)
                def _():
                    acc[c] += dwp
            dys.append(dy)

        _tile_put(obuf, osem, lambda st: dp_ref.at[pl.ds((st % ni) * tm, tm), pl.ds(P_GM * W + (st // ni) * D, D)],
                  b * ni + i, 4 * ni, fill)
        dy_ref[...] = dys[0]

        @pl.when(i == ni - 1)
        def _():
            dw_ref[...] = acc[...].astype(BF16)

    return pl.pallas_call(
        body, grid=(4, ni),
        in_specs=[pl.BlockSpec((tm, D), lambda b, i: (i, 0)), pl.BlockSpec((None, tm, W), lambda b, i: (b, i, 0)),
                  pl.BlockSpec((NCHIP, None, W, cb), lambda b, i: (0, b, 0, 0)),
                  pl.BlockSpec((tm, W), lambda b, i: (i, P_GM + 2 * b)), pl.BlockSpec((tm, W), lambda b, i: (i, P_GM + 2 * b + 1)), ANY],
        out_specs=[ANY, pl.BlockSpec((None, tm, W), lambda b, i: (b, i, 0)), pl.BlockSpec((NCHIP, None, W, cb), lambda b, i: (0, b, 0, 0))],
        out_shape=[_sds((S, DIN), BF16), _sds((4, S, W), F32), _sds((NCHIP, 4, W, cb), BF16)],
        scratch_shapes=[pltpu.VMEM((NCHIP, W, cb), F32), pltpu.VMEM((2, tm, D), BF16), pltpu.SemaphoreType.DMA((2,))],
        input_output_aliases={5: 0}, compiler_params=_cp("arbitrary", "arbitrary"), name="merge_bwd")(dz, y_all, wbr, proj, proj, dproj)


def _out_fwd(z, wo, x, tm=512):
    def body(z_ref, w_ref, x_ref, o_ref):
        o_ref[...] = x_ref[...] + _dot(z_ref[...], w_ref[...])

    row = pl.BlockSpec((tm, D), lambda i: (i, 0))
    return pl.pallas_call(
        body, grid=(S // tm,), in_specs=[row, pl.BlockSpec((D, D), lambda i: (0, 0)), row], out_specs=row,
        out_shape=_sds((S, D), F32), compiler_params=_cp("parallel"), name="out_fwd")(z, wo, x)


def _out_bwd(dx, z, wo, tm=512):
    ni = S // tm

    def body(dx_ref, z_ref, w_ref, dz_ref, dw_ref, acc):
        i = pl.program_id(0)
        dxb = dx_ref[...].astype(BF16)
        dz_ref[...] = _dot_nt(dxb, w_ref[...])
        part = _dot_tn(z_ref[...], dxb)

        @pl.when(i == 0)
        def _():
            acc[...] = part

        @pl.when(i > 0)
        def _():
            acc[...] += part

        @pl.when(i == ni - 1)
        def _():
            dw_ref[...] = acc[...].astype(BF16)

    row = pl.BlockSpec((tm, D), lambda i: (i, 0))
    full = pl.BlockSpec((D, D), lambda i: (0, 0))
    return pl.pallas_call(
        body, grid=(ni,), in_specs=[row, row, full], out_specs=[row, full],
        out_shape=[_sds((S, D), F32), _sds((D, D), BF16)], scratch_shapes=[pltpu.VMEM((D, D), F32)],
        compiler_params=_cp("arbitrary"), name="out_bwd")(dx, z, wo)


def _gelu_parts(a):
    cdf = 0.5 * (1.0 + lax.erf(a * INV_SQRT2))
    return a * cdf, cdf


def _ln_parts(v):
    mu = jnp.mean(v, axis=-1, keepdims=True)
    vc = v - mu
    rs = lax.rsqrt(jnp.mean(vc * vc, axis=-1, keepdims=True) + EPS)
    return vc * rs, rs


def _causal_mask():
    return lax.broadcasted_iota(jnp.int32, (HD, HD), 0) >= lax.broadcasted_iota(jnp.int32, (HD, HD), 1)


def _gmlp_fwd(proj, lg, lb, ws, bias, y_all, tm=512):
    def body(uv_ref, gt_ref, lg_ref, lb_ref, ws_ref, b_ref, y_in, y_ref):
        act, _ = _gelu_parts(uv_ref[...])
        u = act[:, :W]
        xh, _ = _ln_parts(act[:, W:])
        vn = (xh * lg_ref[...] + lb_ref[...]).astype(BF16)
        gt = gt_ref[...]
        us = u * (gt * _sigmoid(gt))
        mask = _causal_mask()
        for h in range(4):
            wm = jnp.where(mask, ws_ref[h], 0.0).astype(BF16)
            cs = slice(h * HD, (h + 1) * HD)
            for c in range(tm // HD):
                rs_ = slice(c * HD, (c + 1) * HD)
                mixed = _dot(wm, vn[rs_, cs]) + b_ref[h]
                y_ref[rs_, cs] = (us[rs_, cs] * mixed).astype(BF16)

    vec = pl.BlockSpec((1, W), lambda i: (0, 0))
    mats = pl.BlockSpec((4, HD, HD), lambda i: (0, 0, 0))
    return pl.pallas_call(
        body, grid=(S // tm,),
        in_specs=[pl.BlockSpec((tm, 2 * W), lambda i: (i, 0)), pl.BlockSpec((tm, W), lambda i: (i, P_AGATE)), vec, vec, mats, mats, ANY],
        out_specs=pl.BlockSpec((None, tm, W), lambda i: (0, i, 0)), out_shape=_sds((4, S, W), BF16),
        input_output_aliases={6: 0}, compiler_params=_cp("parallel"), name="gmlp_fwd")(proj, proj, lg, lb, ws, bias, y_all)


def _gmlp_bwd(proj, dy_all, lg, lb, ws, bias, dproj, tm=256):
    ni = S // tm

    def body(uv_ref, gt_ref, dy_ref, lg_ref, lb_ref, ws_ref, b_ref, dp_in, dp_ref, dws_ref, dbs_ref, dlg_ref, dlb_ref, mix_s, dvn_s):
        i = pl.program_id(0)
        a0 = uv_ref[...]
        act, cdf = _gelu_parts(a0)
        u = act[:, :W]
        xh, rs = _ln_parts(act[:, W:])
        lgv = lg_ref[...]
        vn = (xh * lgv + lb_ref[...]).astype(BF16)
        mask = _causal_mask()
        wms = [jnp.where(mask, ws_ref[h], 0.0).astype(BF16) for h in range(4)]
        for h in range(4):
            cs = slice(h * HD, (h + 1) * HD)
            for c in range(tm // HD):
                rs_ = slice(c * HD, (c + 1) * HD)
                mix_s[rs_, cs] = _dot(wms[h], vn[rs_, cs]) + b_ref[h]
        mixed = mix_s[...]
        gt = gt_ref[...]
        sg = _sigmoid(gt)
        sl = gt * sg
        dyv = dy_ref[...]
        dum = dyv * sl
        dgate = dyv * (u * mixed) * (sg * (1.0 + gt * (1.0 - sg)))
        du = dum * mixed
        dmix = dum * u
        dmb = dmix.astype(BF16)
        for h in range(4):
            cs = slice(h * HD, (h + 1) * HD)
            dw = None
            db = None
            for c in range(tm // HD):
                rs_ = slice(c * HD, (c + 1) * HD)
                dvn_s[rs_, cs] = _dot_tn(wms[h], dmb[rs_, cs])
                pw = _dot_nt(dmb[rs_, cs], vn[rs_, cs])
                dw = pw if dw is None else dw + pw
                db = dmix[rs_, cs] if db is None else db + dmix[rs_, cs]

            @pl.when(i == 0)
            def _():
                dws_ref[h] = dw
                dbs_ref[h] = db

            @pl.when(i > 0)
            def _():
                dws_ref[h] += dw
                dbs_ref[h] += db

            @pl.when(i == ni - 1)
            def _():
                dws_ref[h] = jnp.where(mask, dws_ref[h], 0.0)
                dbs_ref[h] = jnp.broadcast_to(jnp.sum(dbs_ref[h], axis=1, keepdims=True), (HD, HD))
        dvn = dvn_s[...]
        plg = jnp.sum(dvn * xh, axis=0, keepdims=True)
        plb = jnp.sum(dvn, axis=0, keepdims=True)

        @pl.when(i == 0)
        def _():
            dlg_ref[...] = plg
            dlb_ref[...] = plb

        @pl.when(i > 0)
        def _():
            dlg_ref[...] += plg
            dlb_ref[...] += plb

        dxh = dvn * lgv
        dv = rs * (dxh - jnp.mean(dxh, axis=-1, keepdims=True) - xh * jnp.mean(dxh * xh, axis=-1, keepdims=True))
        gp = cdf + a0 * (jnp.exp(-0.5 * a0 * a0) * INV_SQRT2PI)
        dp_ref[:, :W] = (du * gp[:, :W]).astype(BF16)
        dp_ref[:, W:2 * W] = (dv * gp[:, W:]).astype(BF16)
        dp_ref[:, 2 * W:] = dgate.astype(BF16)

    vec = pl.BlockSpec((1, W), lambda i: (0, 0))
    mats = pl.BlockSpec((4, HD, HD), lambda i: (0, 0, 0))
    return pl.pallas_call(
        body, grid=(ni,),
        in_specs=[pl.BlockSpec((tm, 2 * W), lambda i: (i, 0)), pl.BlockSpec((tm, W), lambda i: (i, P_AGATE)),
                  pl.BlockSpec((None, tm, W), lambda i: (0, i, 0)), vec, vec, mats, mats, ANY],
        out_specs=[pl.BlockSpec((tm, 3 * W), lambda i: (i, 0)), mats, mats, vec, vec],
        out_shape=[_sds((S, DIN), BF16), _sds((4, HD, HD), F32), _sds((4, HD, HD), F32), _sds((1, W), F32), _sds((1, W), F32)],
        scratch_shapes=[pltpu.VMEM((tm, W), F32), pltpu.VMEM((tm, W), F32)],
        input_output_aliases={7: 0}, compiler_params=_cp("arbitrary"), name="gmlp_bwd")(proj, proj, dy_all, lg, lb, ws, bias, dproj)


def _pool_diff(p, halo, row0, tm):
    xx = jnp.concatenate([halo, p], axis=0)
    t1 = (row0 + 1 + lax.broadcasted_iota(jnp.int32, (tm, 1), 0)).astype(F32)
    out = []
    for g, win in enumerate(POOL_WINDOWS):
        s = xx[:, g * HD:(g + 1) * HD]
        sh = 1
        while sh < win:
            s = s + pltpu.roll(s, sh, 0)
            sh *= 2
        out.append(s[HALO:] / jnp.minimum(t1, float(win)) - p[:, g * HD:(g + 1) * HD])
    return out


def _pool_fwd(proj, pw, sc, y_all, tm=512):
    rb = tm // HALO

    def body(p_ref, h_ref, gt_ref, pw_ref, sc_ref, y_in, y_ref):
        i = pl.program_id(0)
        halo = jnp.where(i > 0, h_ref[...], 0.0)
        ds = _pool_diff(p_ref[...], halo, i * tm, tm)
        gt = gt_ref[...]
        sl = gt * _sigmoid(gt)
        for g in range(4):
            cs = slice(g * HD, (g + 1) * HD)
            lin = _dot(ds[g].astype(BF16), pw_ref[g].astype(BF16))
            y_ref[:, cs] = (lin * sc_ref[:, cs] * sl[:, cs]).astype(BF16)

    return pl.pallas_call(
        body, grid=(S // tm,),
        in_specs=[pl.BlockSpec((tm, W), lambda i: (i, P_PIN)),
                  pl.BlockSpec((HALO, W), lambda i: (jnp.maximum(i * rb - 1, 0), P_PIN)),
                  pl.BlockSpec((tm, W), lambda i: (i, P_PGATE)),
                  pl.BlockSpec((4, HD, HD), lambda i: (0, 0, 0)), pl.BlockSpec((1, W), lambda i: (0, 0)), ANY],
        out_specs=pl.BlockSpec((None, tm, W), lambda i: (1, i, 0)), out_shape=_sds((4, S, W), BF16),
        input_output_aliases={5: 0}, compiler_params=_cp("parallel"), name="pool_fwd")(proj, proj, proj, pw, sc, y_all)


def _pool_bwd(proj, dy_all, pw, sc, dproj, tm=256):
    ni = S // tm
    rb = tm // HALO
    last_rb = S // HALO - 1
    rx = tm + HALO

    def body(p_ref, h_ref, gt_ref, gh_ref, dy_ref, dyh_ref, pw_ref, sc_ref, dp_in, dp_ref, dpw_ref, dsc_ref, obuf, osem):
        i = pl.program_id(0)
        halo = jnp.where(i > 0, h_ref[...], 0.0)
        ds = _pool_diff(p_ref[...], halo, i * tm, tm)
        nxt = i < ni - 1
        gx = jnp.concatenate([gt_ref[...], gh_ref[...]], axis=0)
        dyx = jnp.concatenate([dy_ref[...], jnp.where(nxt, dyh_ref[...], 0.0)], axis=0)
        sgx = _sigmoid(gx)
        slx = gx * sgx
        scv = sc_ref[...]
        dlinx = dyx * slx * scv
        t1 = (i * tm + 1 + lax.broadcasted_iota(jnp.int32, (rx, 1), 0)).astype(F32)
        gt, sg, sl, dyv = gx[:tm], sgx[:tm], slx[:tm], dyx[:tm]
        dsl = sg * (1.0 + gt * (1.0 - sg))

        def fill(slot):
            for g, win in enumerate(POOL_WINDOWS):
                cs = slice(g * HD, (g + 1) * HD)
                wv = pw_ref[g].astype(BF16)
                dlb = dlinx[:, cs].astype(BF16)
                ddx = _dot_nt(dlb, wv)
                f = ddx / jnp.minimum(t1, float(win))
                sh = 1
                while sh < win:
                    f = f + pltpu.roll(f, rx - sh, 0)
                    sh *= 2
                slot[:, cs] = (f[:tm] - ddx[:tm]).astype(BF16)
                db = ds[g].astype(BF16)
                lin = _dot(db, wv)
                slot[:, W + g * HD:W + (g + 1) * HD] = (dyv[:, cs] * lin * scv[:, cs] * dsl[:, cs]).astype(BF16)
                psc = jnp.sum(dyv[:, cs] * sl[:, cs] * lin, axis=0, keepdims=True)
                pwg = _dot_tn(db, dlb[:tm])

                @pl.when(i == 0)
                def _():
                    dpw_ref[g] = pwg
                    dsc_ref[:, cs] = psc

                @pl.when(i > 0)
                def _():
                    dpw_ref[g] += pwg
                    dsc_ref[:, cs] += psc

        _tile_put(obuf, osem, lambda st: dp_ref.at[pl.ds(st * tm, tm), pl.ds(P_PIN * W, 2 * W)], i, ni, fill)

    mats = pl.BlockSpec((4, HD, HD), lambda i: (0, 0, 0))
    vec = pl.BlockSpec((1, W), lambda i: (0, 0))
    return pl.pallas_call(
        body, grid=(ni,),
        in_specs=[pl.BlockSpec((tm, W), lambda i: (i, P_PIN)),
                  pl.BlockSpec((HALO, W), lambda i: (jnp.maximum(i * rb - 1, 0), P_PIN)),
                  pl.BlockSpec((tm, W), lambda i: (i, P_PGATE)),
                  pl.BlockSpec((HALO, W), lambda i: (jnp.minimum((i + 1) * rb, last_rb), P_PGATE)),
                  pl.BlockSpec((None, tm, W), lambda i: (1, i, 0)),
                  pl.BlockSpec((None, HALO, W), lambda i: (1, jnp.minimum((i + 1) * rb, last_rb), 0)),
                  mats, vec, ANY],
        out_specs=[ANY, mats, vec],
        out_shape=[_sds((S, DIN), BF16), _sds((4, HD, HD), F32), _sds((1, W), F32)],
        scratch_shapes=[pltpu.VMEM((2, tm, 2 * W), BF16), pltpu.SemaphoreType.DMA((2,))],
        input_output_aliases={8: 0}, compiler_params=_cp("arbitrary"), name="pool_bwd")(proj, proj, proj, proj, dy_all, dy_all, pw, sc, dproj)


def _att_masks():
    qi = lax.broadcasted_iota(jnp.int32, (HD, HD), 0)
    kj = lax.broadcasted_iota(jnp.int32, (HD, HD), 1)
    return kj >= qi, kj <= qi


def _att_fwd(proj, g):
    d = DILATIONS[g]
    ln = S // d
    nb = ln // HD
    pv = proj.reshape(ln, d * DIN)

    def body(q_ref, kp_ref, ko_ref, vp_ref, vo_ref, o_ref, l_ref):
        jb = pl.program_id(1)
        m_prev, m_own = _att_masks()
        no_prev = jnp.where(jb > 0, 0.0, NEG)
        for h in range(4):
            cs = slice(h * HD, (h + 1) * HD)
            q = q_ref[:, cs].astype(BF16)
            sp = jnp.where(m_prev, _dot_nt(q, kp_ref[:, cs].astype(BF16)) * SCALE, NEG) + no_prev
            so = jnp.where(m_own, _dot_nt(q, ko_ref[:, cs].astype(BF16)) * SCALE, NEG)
            m = jnp.maximum(jnp.max(sp, axis=-1, keepdims=True), jnp.max(so, axis=-1, keepdims=True))
            ep = jnp.exp(sp - m)
            eo = jnp.exp(so - m)
            den = jnp.sum(ep, axis=-1, keepdims=True) + jnp.sum(eo, axis=-1, keepdims=True)
            inv = 1.0 / den
            o_ref[:, cs] = (_dot((ep * inv).astype(BF16), vp_ref[:, cs].astype(BF16))
                            + _dot((eo * inv).astype(BF16), vo_ref[:, cs].astype(BF16)))
            l_ref[:, cs] = jnp.broadcast_to(m + jnp.log(den), (HD, HD))

    def col(piece):
        return lambda r, jb: (jb, r * NPIECE + piece)

    def col_prev(piece):
        return lambda r, jb: (jnp.maximum(jb - 1, 0), r * NPIECE + piece)

    blk = (HD, W)
    out = pl.BlockSpec(blk, lambda r, jb: (jb, r))
    o, lse = pl.pallas_call(
        body, grid=(d, nb),
        in_specs=[pl.BlockSpec(blk, col(P_CQ + g)), pl.BlockSpec(blk, col_prev(P_CK)), pl.BlockSpec(blk, col(P_CK)),
                  pl.BlockSpec(blk, col_prev(P_CV)), pl.BlockSpec(blk, col(P_CV))],
        out_specs=[out, out], out_shape=[_sds((ln, d * W), F32)] * 2,
        compiler_params=_cp("parallel", "parallel"), name="att_fwd")(pv, pv, pv, pv, pv)
    return o.reshape(S, W), lse.reshape(S, W)


def _att_mix(os_, ls_, proj, y_all, tm=512):
    def body(o0, o1, o2, l0, l1, l2, gt_ref, y_in, y_ref, om_ref, lt_ref):
        a, b, c = l0[...], l1[...], l2[...]
        m = jnp.maximum(jnp.maximum(a, b), c)
        ea, eb, ec = jnp.exp(a - m), jnp.exp(b - m), jnp.exp(c - m)
        z = ea + eb + ec
        inv = 1.0 / z
        o = (ea * inv) * o0[...] + (eb * inv) * o1[...] + (ec * inv) * o2[...]
        gt = gt_ref[...]
        om_ref[...] = o
        lt_ref[...] = m + jnp.log(z)
        y_ref[...] = (o * (gt * _sigmoid(gt))).astype(BF16)

    row = pl.BlockSpec((tm, W), lambda i: (i, 0))
    return pl.pallas_call(
        body, grid=(S // tm,),
        in_specs=[row] * 6 + [pl.BlockSpec((tm, W), lambda i: (i, P_CGATE)), ANY],
        out_specs=[pl.BlockSpec((None, tm, W), lambda i: (2, i, 0)), row, row],
        out_shape=[_sds((4, S, W), BF16), _sds((S, W), F32), _sds((S, W), F32)],
        input_output_aliases={7: 0}, compiler_params=_cp("parallel"), name="att_mix")(*os_, *ls_, proj, y_all)


def _att_bwd_pre(dy_all, proj, om, dproj, tm=512):
    def body(dy_ref, gt_ref, om_ref, dp_in, do_ref, dl_ref, dp_ref):
        gt = gt_ref[...]
        sg = _sigmoid(gt)
        dyv = dy_ref[...]
        o = om_ref[...]
        do = dyv * (gt * sg)
        dp_ref[...] = (dyv * o * (sg * (1.0 + gt * (1.0 - sg)))).astype(BF16)
        do_ref[...] = do.astype(BF16)
        pr = do * o
        for h in range(4):
            cs = slice(h * HD, (h + 1) * HD)
            dl_ref[:, cs] = jnp.broadcast_to(jnp.sum(pr[:, cs], axis=-1, keepdims=True), (tm, HD))

    row = pl.BlockSpec((tm, W), lambda i: (i, 0))
    gcol = pl.BlockSpec((tm, W), lambda i: (i, P_CGATE))
    return pl.pallas_call(
        body, grid=(S // tm,),
        in_specs=[pl.BlockSpec((None, tm, W), lambda i: (2, i, 0)), gcol, row, ANY],
        out_specs=[row, row, gcol], out_shape=[_sds((S, W), BF16), _sds((S, W), F32), _sds((S, DIN), BF16)],
        input_output_aliases={3: 2}, compiler_params=_cp("parallel"), name="att_bwd_pre")(dy_all, proj, om, dproj)


def _att_bwd(proj, do, lse, delta, dproj, g):
    d = DILATIONS[g]
    ln = S // d
    nb = ln // HD
    pv = proj.reshape(ln, d * DIN)
    dpv = dproj.reshape(ln, d * DIN)
    dov, lsv, dlv = (t.reshape(ln, d * W) for t in (do, lse, delta))

    def body(qa_ref, qb_ref, kp_ref, ko_ref, vp_ref, vo_ref, doa_ref, dob_ref, la_ref, lb_ref, da_ref, db_ref, dp_in,
             dq_ref, dk_ref, dv_ref):
        jb = pl.program_id(1)
        m_prev, m_own = _att_masks()
        has_prev = jnp.where(jb > 0, 1.0, 0.0)
        has_next = jnp.where(jb < nb - 1, 1.0, 0.0)

        def pair(q, k, v, dov_, lsev, dlt, mask, flag=None):
            p = jnp.where(mask, jnp.exp(_dot_nt(q, k) * SCALE - lsev), 0.0)
            if flag is not None:
                p = p * flag
            dsb = (p * (_dot_nt(dov_, v) - dlt) * SCALE).astype(BF16)
            return p.astype(BF16), dsb

        for h in range(4):
            cs = slice(h * HD, (h + 1) * HD)
            qa, qb = qa_ref[:, cs].astype(BF16), qb_ref[:, cs].astype(BF16)
            kp, ko = kp_ref[:, cs].astype(BF16), ko_ref[:, cs].astype(BF16)
            vp, vo = vp_ref[:, cs].astype(BF16), vo_ref[:, cs].astype(BF16)
            doa, dob = doa_ref[:, cs], dob_ref[:, cs]
            p, dsb = pair(qa, ko, vo, doa, la_ref[:, cs], da_ref[:, cs], m_own)
            dq = _dot(dsb, ko)
            dk = _dot_tn(dsb, qa)
            dv = _dot_tn(p, doa)
            p, dsb = pair(qa, kp, vp, doa, la_ref[:, cs], da_ref[:, cs], m_prev, has_prev)
            dq = dq + _dot(dsb, kp)
            p, dsb = pair(qb, ko, vo, dob, lb_ref[:, cs], db_ref[:, cs], m_prev, has_next)
            dk = dk + _dot_tn(dsb, qb)
            dv = dv + _dot_tn(p, dob)
            dq_ref[:, cs] = dq.astype(BF16)
            dk_ref[:, cs] = dk
            dv_ref[:, cs] = dv

    def col(piece):
        return lambda r, jb: (jb, r * NPIECE + piece)

    def col_prev(piece):
        return lambda r, jb: (jnp.maximum(jb - 1, 0), r * NPIECE + piece)

    def col_next(piece):
        return lambda r, jb: (jnp.minimum(jb + 1, nb - 1), r * NPIECE + piece)

    blk = (HD, W)
    own = pl.BlockSpec(blk, lambda r, jb: (jb, r))
    nxt = pl.BlockSpec(blk, lambda r, jb: (jnp.minimum(jb + 1, nb - 1), r))
    dpn, dk, dv = pl.pallas_call(
        body, grid=(d, nb),
        in_specs=[pl.BlockSpec(blk, col(P_CQ + g)), pl.BlockSpec(blk, col_next(P_CQ + g)),
                  pl.BlockSpec(blk, col_prev(P_CK)), pl.BlockSpec(blk, col(P_CK)),
                  pl.BlockSpec(blk, col_prev(P_CV)), pl.BlockSpec(blk, col(P_CV)),
                  own, nxt, own, nxt, own, nxt, ANY],
        out_specs=[pl.BlockSpec(blk, col(P_CQ + g)), own, own],
        out_shape=[_sds((ln, d * DIN), BF16), _sds((ln, d * W), F32), _sds((ln, d * W), F32)],
        input_output_aliases={12: 0}, compiler_params=_cp("parallel", "parallel"), name="att_bwd")(
            pv, pv, pv, pv, pv, pv, dov, dov, lsv, lsv, dlv, dlv, dpv)
    return dpn.reshape(S, DIN), dk.reshape(S, W), dv.reshape(S, W)


def _att_sum_kv(dks, dvs, dproj, tm=512):
    def body(k0, k1, k2, v0, v1, v2, dp_in, dp_ref):
        dp_ref[:, :W] = (k0[...] + k1[...] + k2[...]).astype(BF16)
        dp_ref[:, W:] = (v0[...] + v1[...] + v2[...]).astype(BF16)

    row = pl.BlockSpec((tm, W), lambda i: (i, 0))
    return pl.pallas_call(
        body, grid=(S // tm,), in_specs=[row] * 6 + [ANY],
        out_specs=pl.BlockSpec((tm, 2 * W), lambda i: (i, P_CK // 2)), out_shape=_sds((S, DIN), BF16),
        input_output_aliases={6: 0}, compiler_params=_cp("parallel"), name="att_sum_kv")(*dks, *dvs, dproj)


def _mem_kv_fwd(mem_n, wkv):
    m = mem_n.shape[0]

    def body(a_ref, w_ref, o_ref):
        o_ref[...] = _dot(a_ref[...], w_ref[...])

    return pl.pallas_call(body, out_shape=_sds((m, 2 * W), F32), compiler_params=_cp(), name="mem_kv_fwd")(mem_n, wkv)


def _mem_softmax(q, k):
    s = _dot_nt(q, k) * SCALE
    e = jnp.exp(s - jnp.max(s, axis=-1, keepdims=True))
    return e * (1.0 / jnp.sum(e, axis=-1, keepdims=True))


def _mem_fwd(proj, kv, y_all, tm=512):
    m = kv.shape[0]

    def body(q_ref, gt_ref, kv_ref, y_in, y_ref):
        gt = gt_ref[...]
        sl = gt * _sigmoid(gt)
        for h in range(4):
            cs = slice(h * HD, (h + 1) * HD)
            p = _mem_softmax(q_ref[:, cs].astype(BF16), kv_ref[:, cs].astype(BF16))
            o = _dot(p.astype(BF16), kv_ref[:, W + h * HD:W + (h + 1) * HD].astype(BF16))
            y_ref[:, cs] = (o * sl[:, cs]).astype(BF16)

    return pl.pallas_call(
        body, grid=(S // tm,),
        in_specs=[pl.BlockSpec((tm, W), lambda i: (i, P_MQ)), pl.BlockSpec((tm, W), lambda i: (i, P_MGATE)),
                  pl.BlockSpec((m, 2 * W), lambda i: (0, 0)), ANY],
        out_specs=pl.BlockSpec((None, tm, W), lambda i: (3, i, 0)), out_shape=_sds((4, S, W), BF16),
        input_output_aliases={3: 0}, compiler_params=_cp("parallel"), name="mem_fwd")(proj, proj, kv, y_all)


def _mem_bwd(proj, kv, dy_all, dproj, tm=512):
    m = kv.shape[0]
    ni = S // tm

    def body(q_ref, gt_ref, kv_ref, dy_ref, dp_in, dp_ref, dkv_ref, obuf, osem):
        i = pl.program_id(0)
        gt = gt_ref[...]
        sg = _sigmoid(gt)
        sl = gt * sg
        dsl = sg * (1.0 + gt * (1.0 - sg))
        dyv = dy_ref[...]

        def fill(slot):
            for h in range(4):
                cs = slice(h * HD, (h + 1) * HD)
                vs = slice(W + h * HD, W + (h + 1) * HD)
                q = q_ref[:, cs].astype(BF16)
                k = kv_ref[:, cs].astype(BF16)
                v = kv_ref[:, vs].astype(BF16)
                p = _mem_softmax(q, k)
                pb = p.astype(BF16)
                o = _dot(pb, v)
                do = dyv[:, cs] * sl[:, cs]
                dob = do.astype(BF16)
                dp = _dot_nt(dob, v)
                dsb = (p * (dp - jnp.sum(dp * p, axis=-1, keepdims=True)) * SCALE).astype(BF16)
                slot[:, cs] = _dot(dsb, k).astype(BF16)
                slot[:, vs] = (dyv[:, cs] * o * dsl[:, cs]).astype(BF16)
                dk = _dot_tn(dsb, q)
                dv = _dot_tn(pb, dob)

                @pl.when(i == 0)
                def _():
                    dkv_ref[:, cs] = dk
                    dkv_ref[:, vs] = dv

                @pl.when(i > 0)
                def _():
                    dkv_ref[:, cs] += dk
                    dkv_ref[:, vs] += dv

        _tile_put(obuf, osem, lambda st: dp_ref.at[pl.ds(st * tm, tm), pl.ds(P_MQ * W, 2 * W)], i, ni, fill)

    return pl.pallas_call(
        body, grid=(ni,),
        in_specs=[pl.BlockSpec((tm, W), lambda i: (i, P_MQ)), pl.BlockSpec((tm, W), lambda i: (i, P_MGATE)),
                  pl.BlockSpec((m, 2 * W), lambda i: (0, 0)), pl.BlockSpec((None, tm, W), lambda i: (3, i, 0)), ANY],
        out_specs=[ANY, pl.BlockSpec((m, 2 * W), lambda i: (0, 0))],
        out_shape=[_sds((S, DIN), BF16), _sds((m, 2 * W), F32)],
        scratch_shapes=[pltpu.VMEM((2, tm, 2 * W), BF16), pltpu.SemaphoreType.DMA((2,))],
        input_output_aliases={4: 0}, compiler_params=_cp("arbitrary"), name="mem_bwd")(proj, proj, kv, dy_all, dproj)


def _mem_kv_bwd(mem, g, mem_n, wkv, dkv):
    m = mem.shape[0]

    def body(x_ref, g_ref, a_ref, w_ref, d_ref, dw_ref, dg_ref):
        db = d_ref[...].astype(BF16)
        dw_ref[...] = _dot_tn(a_ref[...], db).astype(BF16)
        dn = _dot_nt(db, w_ref[...])
        xv = x_ref[...]
        xh = xv * lax.rsqrt(jnp.mean(xv * xv, axis=-1, keepdims=True) + EPS)
        dg_ref[...] = jnp.sum(dn * xh, axis=0, keepdims=True)

    return pl.pallas_call(
        body, out_shape=[_sds((D, 2 * W), BF16), _sds((1, D), F32)], compiler_params=_cp(), name="mem_kv_bwd")(mem, g, mem_n, wkv, dkv)


def _layer_fwd(x, mem, p, wg):
    win, wkv, wbr, wo = wg
    h = _rms_fwd(x, p["norm_g"], 512)
    proj = _proj_fwd(h, win)
    y_all = lax.empty((4, S, W), BF16)
    y_all = _gmlp_fwd(proj, p["gm_ln_g"], p["gm_ln_b"], p["gm_ws"], p["gm_bias"], y_all)
    y_all = _pool_fwd(proj, p["pool_w"], p["pool_scale"], y_all)
    os_, ls_ = zip(*[_att_fwd(proj, g) for g in range(3)])
    y_all, om, lse = _att_mix(os_, ls_, proj, y_all)
    mem_n = _rms_fwd(mem, p["mem_norm_g"], mem.shape[0])
    kv = _mem_kv_fwd(mem_n, wkv)
    y_all = _mem_fwd(proj, kv, y_all)
    z = _merge_fwd(y_all, wbr, proj)
    x_new = _out_fwd(z, wo, x)
    return x_new, dict(x=x, h=h, proj=proj, y_all=y_all, om=om, lse=lse, mem_n=mem_n, kv=kv, z=z)


def _layer_bwd(dx, mem, p, wg, sv):
    win, wkv, wbr, wo = wg
    proj = sv["proj"]
    dz, d_wo = _out_bwd(dx, sv["z"], wo)
    dproj = lax.empty((S, DIN), BF16)
    dproj, dy_all, d_wbr = _merge_bwd(dz, sv["y_all"], wbr, proj, dproj)
    dproj, d_ws, d_bs, d_lg, d_lb = _gmlp_bwd(proj, dy_all, p["gm_ln_g"], p["gm_ln_b"], p["gm_ws"], p["gm_bias"], dproj)
    dproj, d_pw, d_sc = _pool_bwd(proj, dy_all, p["pool_w"], p["pool_scale"], dproj)
    do, delta, dproj = _att_bwd_pre(dy_all, proj, sv["om"], dproj)
    dks, dvs = [], []
    for g in range(3):
        dproj, dk, dv = _att_bwd(proj, do, sv["lse"], delta, dproj, g)
        dks.append(dk)
        dvs.append(dv)
    dproj = _att_sum_kv(dks, dvs, dproj)
    dproj, dkv = _mem_bwd(proj, sv["kv"], dy_all, dproj)
    d_wkv, d_mg = _mem_kv_bwd(mem, p["mem_norm_g"], sv["mem_n"], wkv, dkv)
    dh = _proj_bwd_x(dproj, win)
    d_win = _proj_bwd_w(sv["h"], dproj)
    dx_in, d_ng = _rms_bwd(dh, sv["x"], p["norm_g"], dx)
    big = (d_win, d_wkv, d_wbr, d_wo)
    small = dict(norm_g=d_ng, gm_ln_g=d_lg, gm_ln_b=d_lb, gm_ws=d_ws, gm_bs=d_bs[:, :, 0], pool_w=d_pw, pool_scale=d_sc, mem_norm_g=d_mg)
    return dx_in, big, small


_SMALL = ("norm_g", "gm_ln_g", "gm_ln_b", "gm_ws", "gm_bs", "pool_w", "pool_scale", "mem_norm_g")


def _layer_params(l, norm_g, gm_ln_g, gm_ln_b, gm_ws, gm_bs, pool_w, pool_scale, mem_norm_g):
    return dict(norm_g=norm_g[l][None], gm_ln_g=gm_ln_g[l][None], gm_ln_b=gm_ln_b[l][None], gm_ws=gm_ws[l],
                gm_bias=jnp.broadcast_to(gm_bs[l][:, :, None], (4, HD, HD)), pool_w=pool_w[l],
                pool_scale=pool_scale[l][None], mem_norm_g=mem_norm_g[l][None])


def kernel(x, mem, norm_g, w_in, gm_ln_g, gm_ln_b, gm_ws, gm_bs, pool_w, pool_scale, mem_norm_g, w_mem_kv, w_branch, w_out, final_norm_g, loss_target, m_norm_g, m_w_in, m_gm_ln_g, m_gm_ln_b, m_gm_ws, m_gm_bs, m_pool_w, m_pool_scale, m_mem_norm_g, m_w_mem_kv, m_w_branch, m_w_out, m_final_norm_g, v_norm_g, v_w_in, v_gm_ln_g, v_gm_ln_b, v_gm_ws, v_gm_bs, v_pool_w, v_pool_scale, v_mem_norm_g, v_w_mem_kv, v_w_branch, v_w_out, v_final_norm_g):
    xs, memv, tgt = x[0], mem[0], loss_target[0]
    params = [_layer_params(l, norm_g, gm_ln_g, gm_ln_b, gm_ws, gm_bs, pool_w, pool_scale, mem_norm_g) for l in range(NL)]

    win_g, wkv_g, wbr_g, wo_g = _gather_weights(w_in.astype(BF16), w_mem_kv.astype(BF16), w_branch.astype(BF16), w_out.astype(BF16))
    wgs = [(win_g[l], wkv_g[l].reshape(D, 2 * W), wbr_g[l], wo_g[l].reshape(D, D)) for l in range(NL)]

    saved = []
    for l in range(NL):
        xs, sv = _layer_fwd(xs, memv, params[l], wgs[l])
        saved.append(sv)
    dx, d_fg, ls = _loss_head(xs, tgt, final_norm_g[None])
    loss = lax.psum(ls[0, 0], ("x", "y", "c"))

    big, small = [None] * NL, [None] * NL
    for l in reversed(range(NL)):
        dx, big[l], small[l] = _layer_bwd(dx, memv, params[l], wgs[l], saved[l])
    grad_x = dx[None]

    shapes2 = ((2, D // 2, CW), (2, D // 8, 2 * W), (2, 2 * W, D // NCHIP), (2, D // 8, D))
    parts = [[big[l][k].reshape((NCHIP,) + shapes2[k]) for l in range(NL)] for k in range(4)]
    recv = _exchange_grads(parts)
    halves = [_sum_slots(recv[k], tr) for k, tr in zip(range(4), (64, 128, 256, 128))]
    full = _share_halves(halves)
    g_w_in = full[0].reshape(NL, D, CW)
    g_w_kv = full[1].reshape(NL, D // NCHIP, 2 * W)
    g_w_br = full[2].reshape(NL, 4, W, D // NCHIP)
    g_w_out = full[3].reshape(NL, D // NCHIP, D)

    leaves = [jnp.stack([small[l][n] for l in range(NL)]) for n in _SMALL] + [d_fg]
    sizes = [t.size for t in leaves]
    packed = jnp.concatenate([t.reshape(-1, 128) for t in leaves], axis=0)
    rows = packed.shape[0]
    tr = max(t for t in range(8, 513, 8) if rows % t == 0)
    tot = _sum_slots(_exchange_small(packed)[None], tr)[0]
    offs = [0]
    for sz in sizes:
        offs.append(offs[-1] + sz // 128)
    small_shapes = [norm_g.shape, gm_ln_g.shape, gm_ln_b.shape, gm_ws.shape, gm_bs.shape, pool_w.shape, pool_scale.shape,
                    mem_norm_g.shape, final_norm_g.shape]
    gs = [tot[offs[i]:offs[i + 1]].reshape(small_shapes[i]) for i in range(len(leaves))]
    g_norm_g, g_ln_g, g_ln_b, g_ws, g_bs, g_pw, g_sc, g_mg, g_fg = gs

    grads = [g_norm_g, g_w_in, g_ln_g, g_ln_b, g_ws, g_bs, g_pw, g_sc, g_mg, g_w_kv, g_w_br, g_w_out, g_fg]
    ws = [norm_g, w_in, gm_ln_g, gm_ln_b, gm_ws, gm_bs, pool_w, pool_scale, mem_norm_g, w_mem_kv, w_branch, w_out, final_norm_g]
    ms = [m_norm_g, m_w_in, m_gm_ln_g, m_gm_ln_b, m_gm_ws, m_gm_bs, m_pool_w, m_pool_scale, m_mem_norm_g, m_w_mem_kv, m_w_branch, m_w_out, m_final_norm_g]
    vs = [v_norm_g, v_w_in, v_gm_ln_g, v_gm_ln_b, v_gm_ws, v_gm_bs, v_pool_w, v_pool_scale, v_mem_norm_g, v_w_mem_kv, v_w_branch, v_w_out, v_final_norm_g]
    deltas, new_m, new_v = zip(*[_adamw(w, g, m, v) for w, g, m, v in zip(ws, grads, ms, vs)])
    return (loss, grad_x, *grads, *deltas, *new_m, *new_v)
```

```python
import functools
import math

import jax
import jax.numpy as jnp
from jax import lax
from jax.experimental import pallas as pl
from jax.experimental.pallas import tpu as pltpu

F32 = jnp.float32
BF16 = jnp.bfloat16

S = 4096
D = 1024
W = 512
DIN = 10752
NL = 4
NCHIP = 4
NDEV = 8
CW = DIN // NCHIP
TN_IN = 896
NJ = CW // TN_IN
HD = 128
EPS = 1e-6
NEG = -1e30
SCALE = HD ** -0.5
INV_SQRT2 = 1.0 / math.sqrt(2.0)
INV_SQRT2PI = 1.0 / math.sqrt(2.0 * math.pi)
POOL_WINDOWS = (2, 4, 8, 16)
DILATIONS = (1, 4, 16)
HALO = 16
NPIECE = DIN // W
P_AGATE, P_PIN, P_PGATE, P_CQ, P_CK, P_CV, P_CGATE, P_MQ, P_MGATE, P_GM = 2, 3, 4, 5, 8, 9, 10, 11, 12, 13
VMEM_LIMIT = 56 * 1024 * 1024

ADAM_LR, ADAM_B1, ADAM_B2, ADAM_EPS, ADAM_WD, ADAM_STEP = 0.001, 0.9, 0.999, 1e-08, 0.01, 10

MESH = pl.DeviceIdType.MESH
ANY = pl.BlockSpec(memory_space=pl.ANY)


def _cp(*sem):
    return pltpu.CompilerParams(dimension_semantics=sem or None, vmem_limit_bytes=VMEM_LIMIT)


def _sds(shape, dtype):
    return jax.ShapeDtypeStruct(shape, dtype)


def _sigmoid(v):
    return 1.0 / (1.0 + jnp.exp(-v))


def _dot(a, b):
    return jnp.dot(a, b, preferred_element_type=F32)


def _dot_nt(a, b):
    return lax.dot_general(a, b, (((1,), (1,)), ((), ())), preferred_element_type=F32)


def _dot_tn(a, b):
    return lax.dot_general(a, b, (((0,), (0,)), ((), ())), preferred_element_type=F32)


def _tile_put(buf, sem, dst_of, step, nsteps, fill):
    slot = step % 2

    def copy(s, st):
        return pltpu.make_async_copy(buf.at[s], dst_of(st), sem.at[s])

    @pl.when(step >= 2)
    def _():
        copy(slot, step).wait()

    fill(buf.at[slot])
    copy(slot, step).start()

    @pl.when(step == nsteps - 1)
    def _():
        if nsteps >= 2:
            copy(1 - slot, step).wait()
        copy(slot, step).wait()


def _my_pos():
    return lax.axis_index("x"), lax.axis_index("y"), lax.axis_index("c")


_CHIP_REL = ((1, 0), (0, 1), (1, 1))
_DEV_REL = tuple((dx, dy, dc) for dx in (0, 1) for dy in (0, 1) for dc in (0, 1))[1:]


def _gather_weights(win_s, wkv_s, wbr_s, wo_s):
    shards = (win_s, wkv_s, wbr_s, wo_s)
    nk = len(shards)
    half = tuple(s.shape[1] // 2 for s in shards)
    out_shape = []
    for s in shards:
        out_shape += [_sds((NCHIP,) + s.shape[1:], BF16) for _ in range(NL)]
    n_pair = 3 * nk * NL

    def body(*refs):
        ins = refs[:nk]
        outs = refs[nk:nk + nk * NL]
        send1, recv1, send2, recv2, lsem = refs[nk + nk * NL:]
        x, y, c = _my_pos()
        me = 2 * x + y
        sibling = (x, y, 1 - c)
        chips = [(x ^ dx, y ^ dy) for dx, dy in _CHIP_REL]

        def out(k, l):
            return outs[k * NL + l]

        def piece(k, l, chip, hf):
            return out(k, l).at[chip, pl.ds(hf * half[k], half[k])]

        def mine(k, l, hf):
            return ins[k].at[l, pl.ds(hf * half[k], half[k])]

        local = []
        for k in range(nk):
            for l in range(NL):
                cp = pltpu.make_async_copy(ins[k].at[l], out(k, l).at[me], lsem.at[k * NL + l])
                cp.start()
                local.append(cp)

        def first(r, k, l):
            cx, cy = chips[r]
            i = (r * nk + k) * NL + l
            return pltpu.make_async_remote_copy(
                src_ref=mine(k, l, c), dst_ref=piece(k, l, me, c), send_sem=send1.at[i], recv_sem=recv1.at[i],
                device_id=(cx, cy, c), device_id_type=MESH)

        def landed(r, k, l):
            cx, cy = chips[r]
            i = (r * nk + k) * NL + l
            return pltpu.make_async_remote_copy(
                src_ref=mine(k, l, c), dst_ref=piece(k, l, 2 * cx + cy, c), send_sem=send1.at[i], recv_sem=recv1.at[i],
                device_id=(cx, cy, c), device_id_type=MESH)

        def passed(r, k, l, hf):
            cx, cy = chips[r]
            i = (r * nk + k) * NL + l
            ref = piece(k, l, 2 * cx + cy, hf)
            return pltpu.make_async_remote_copy(
                src_ref=ref, dst_ref=ref, send_sem=send2.at[i], recv_sem=recv2.at[i],
                device_id=sibling, device_id_type=MESH)

        idx = [(r, k, l) for r in range(3) for k in range(nk) for l in range(NL)]
        for r, k, l in idx:
            first(r, k, l).start()
        for r, k, l in idx:
            landed(r, k, l).wait_recv()
            passed(r, k, l, c).start()
        for r, k, l in idx:
            passed(r, k, l, 1 - c).wait_recv()
        for r, k, l in idx:
            first(r, k, l).wait_send()
            passed(r, k, l, c).wait_send()
        for cp in local:
            cp.wait()

    res = pl.pallas_call(
        body, out_shape=out_shape, in_specs=[ANY] * nk, out_specs=[ANY] * (nk * NL),
        scratch_shapes=[pltpu.SemaphoreType.DMA((n_pair,))] * 4 + [pltpu.SemaphoreType.DMA((nk * NL,))],
        name="gather_weights")(*shards)
    return [res[k * NL:(k + 1) * NL] for k in range(nk)]


def _exchange_grads(grads):
    nk = len(grads)
    flat = [g for gk in grads for g in gk]
    out_shape = [_sds((NL, NDEV) + gk[0].shape[2:], BF16) for gk in grads]
    n_pair = 7 * nk * NL

    def body(*refs):
        ins = refs[:nk * NL]
        outs = refs[nk * NL:nk * NL + nk]
        send, recv, lsem = refs[nk * NL + nk:]
        x, y, c = _my_pos()
        me_chip = 2 * x + y
        me_dev = 4 * x + 2 * y + c
        local = []
        for k in range(nk):
            for l in range(NL):
                cp = pltpu.make_async_copy(ins[k * NL + l].at[me_chip, c], outs[k].at[l, me_dev], lsem.at[k * NL + l])
                cp.start()
                local.append(cp)

        def copy(r, k, l):
            dx, dy, dc = _DEV_REL[r]
            px, py, pc = x ^ dx, y ^ dy, c ^ dc
            i = (r * nk + k) * NL + l
            return pltpu.make_async_remote_copy(
                src_ref=ins[k * NL + l].at[2 * px + py, pc], dst_ref=outs[k].at[l, me_dev],
                send_sem=send.at[i], recv_sem=recv.at[i], device_id=(px, py, pc), device_id_type=MESH)

        def arrival(r, k, l):
            dx, dy, dc = _DEV_REL[r]
            px, py, pc = x ^ dx, y ^ dy, c ^ dc
            i = (r * nk + k) * NL + l
            return pltpu.make_async_remote_copy(
                src_ref=ins[k * NL + l].at[me_chip, c], dst_ref=outs[k].at[l, 4 * px + 2 * py + pc],
                send_sem=send.at[i], recv_sem=recv.at[i], device_id=(px, py, pc), device_id_type=MESH)

        idx = [(r, k, l) for r in range(7) for k in range(nk) for l in range(NL)]
        for r, k, l in idx:
            copy(r, k, l).start()
        for r, k, l in idx:
            arrival(r, k, l).wait_recv()
        for r, k, l in idx:
            copy(r, k, l).wait_send()
        for cp in local:
            cp.wait()

    return pl.pallas_call(
        body, out_shape=out_shape, in_specs=[ANY] * (nk * NL), out_specs=[ANY] * nk,
        scratch_shapes=[pltpu.SemaphoreType.DMA((n_pair,))] * 2 + [pltpu.SemaphoreType.DMA((nk * NL,))],
        name="exchange_grads")(*flat)


def _exchange_small(packed):
    def body(in_ref, out_ref, send, recv, lsem):
        x, y, c = _my_pos()
        me_dev = 4 * x + 2 * y + c
        loc = pltpu.make_async_copy(in_ref, out_ref.at[me_dev], lsem)
        loc.start()

        def copy(r, slot):
            dx, dy, dc = _DEV_REL[r]
            px, py, pc = x ^ dx, y ^ dy, c ^ dc
            return pltpu.make_async_remote_copy(
                src_ref=in_ref, dst_ref=out_ref.at[slot(px, py, pc)], send_sem=send.at[r], recv_sem=recv.at[r],
                device_id=(px, py, pc), device_id_type=MESH)

        for r in range(7):
            copy(r, lambda px, py, pc: me_dev).start()
        for r in range(7):
            copy(r, lambda px, py, pc: 4 * px + 2 * py + pc).wait_recv()
        for r in range(7):
            copy(r, lambda px, py, pc: me_dev).wait_send()
        loc.wait()

    return pl.pallas_call(
        body, out_shape=_sds((NDEV,) + packed.shape, F32), in_specs=[ANY], out_specs=ANY,
        scratch_shapes=[pltpu.SemaphoreType.DMA((7,)), pltpu.SemaphoreType.DMA((7,)), pltpu.SemaphoreType.DMA(())],
        name="exchange_small")(packed)


def _share_halves(halves):
    nk = len(halves)

    def body(*refs):
        ins = refs[:nk]
        outs = refs[nk:2 * nk]
        send, recv, lsem = refs[2 * nk:]
        x, y, c = _my_pos()
        local = []
        for k in range(nk):
            cp = pltpu.make_async_copy(ins[k], outs[k].at[:, c], lsem.at[k])
            cp.start()
            local.append(cp)

        def copy(k, hf):
            return pltpu.make_async_remote_copy(
                src_ref=ins[k], dst_ref=outs[k].at[:, hf], send_sem=send.at[k], recv_sem=recv.at[k],
                device_id=(x, y, 1 - c), device_id_type=MESH)

        for k in range(nk):
            copy(k, c).start()
        for k in range(nk):
            copy(k, 1 - c).wait_recv()
        for k in range(nk):
            copy(k, c).wait_send()
        for cp in local:
            cp.wait()

    out_shape = [_sds((NL, 2) + h.shape[1:], F32) for h in halves]
    return pl.pallas_call(
        body, out_shape=out_shape, in_specs=[ANY] * nk, out_specs=[ANY] * nk,
        scratch_shapes=[pltpu.SemaphoreType.DMA((nk,))] * 3, name="share_halves")(*halves)


def _sum_slots(recv, tr):
    nl, _, r, ccols = recv.shape

    def body(r_ref, o_ref):
        acc = r_ref[0].astype(F32)
        for s in range(1, NDEV):
            acc = acc + r_ref[s].astype(F32)
        o_ref[...] = acc

    return pl.pallas_call(
        body, grid=(nl, r // tr),
        in_specs=[pl.BlockSpec((None, NDEV, tr, ccols), lambda l, i: (l, 0, i, 0))],
        out_specs=pl.BlockSpec((None, tr, ccols), lambda l, i: (l, i, 0)),
        out_shape=_sds((nl, r, ccols), F32), compiler_params=_cp("parallel", "parallel"), name="sum_slots")(recv)


def _rms_fwd(x, g, tm):
    n = x.shape[0]

    def body(x_ref, g_ref, h_ref):
        xv = x_ref[...]
        r = lax.rsqrt(jnp.mean(xv * xv, axis=-1, keepdims=True) + EPS)
        h_ref[...] = (xv * r * g_ref[...]).astype(BF16)

    return pl.pallas_call(
        body, grid=(n // tm,),
        in_specs=[pl.BlockSpec((tm, D), lambda i: (i, 0)), pl.BlockSpec((1, D), lambda i: (0, 0))],
        out_specs=pl.BlockSpec((tm, D), lambda i: (i, 0)), out_shape=_sds((n, D), BF16),
        compiler_params=_cp("parallel"), name="rms_fwd")(x, g)


def _rms_bwd(dh, x, g, dres, tm=512):
    n = x.shape[0]

    def body(dh_ref, x_ref, g_ref, dr_ref, dx_ref, dg_ref):
        i = pl.program_id(0)
        xv = x_ref[...]
        r = lax.rsqrt(jnp.mean(xv * xv, axis=-1, keepdims=True) + EPS)
        xh = xv * r
        dhv = dh_ref[...]
        dxh = dhv * g_ref[...]
        dx_ref[...] = dr_ref[...] + r * (dxh - xh * jnp.mean(dxh * xh, axis=-1, keepdims=True))
        part = jnp.sum(dhv * xh, axis=0, keepdims=True)

        @pl.when(i == 0)
        def _():
            dg_ref[...] = part

        @pl.when(i > 0)
        def _():
            dg_ref[...] += part

    row = pl.BlockSpec((tm, D), lambda i: (i, 0))
    vec = pl.BlockSpec((1, D), lambda i: (0, 0))
    return pl.pallas_call(
        body, grid=(n // tm,), in_specs=[row, row, vec, row], out_specs=[row, vec],
        out_shape=[_sds((n, D), F32), _sds((1, D), F32)], compiler_params=_cp("arbitrary"), name="rms_bwd")(dh, x, g, dres)


def _loss_head(x, tgt, g, tm=512):
    def body(x_ref, t_ref, g_ref, dx_ref, dg_ref, ls_ref):
        i = pl.program_id(0)
        xv = x_ref[...]
        r = lax.rsqrt(jnp.mean(xv * xv, axis=-1, keepdims=True) + EPS)
        xh = xv * r
        gv = g_ref[...]
        diff = xh * gv - t_ref[...]
        dy = diff * (1.0 / D)
        dxh = dy * gv
        dx_ref[...] = r * (dxh - xh * jnp.mean(dxh * xh, axis=-1, keepdims=True))
        part_g = jnp.sum(dy * xh, axis=0, keepdims=True)
        part_l = jnp.sum(diff * diff, axis=0, keepdims=True)

        @pl.when(i == 0)
        def _():
            dg_ref[...] = part_g
            ls_ref[...] = part_l

        @pl.when(i > 0)
        def _():
            dg_ref[...] += part_g
            ls_ref[...] += part_l

        @pl.when(i == pl.num_programs(0) - 1)
        def _():
            tot = jnp.sum(ls_ref[...], axis=-1, keepdims=True) * (0.5 / D)
            ls_ref[...] = jnp.broadcast_to(tot, (1, D))

    row = pl.BlockSpec((tm, D), lambda i: (i, 0))
    vec = pl.BlockSpec((1, D), lambda i: (0, 0))
    return pl.pallas_call(
        body, grid=(S // tm,), in_specs=[row, row, vec], out_specs=[row, vec, vec],
        out_shape=[_sds((S, D), F32), _sds((1, D), F32), _sds((1, D), F32)],
        compiler_params=_cp("arbitrary"), name="loss_head")(x, tgt, g)


def _adamw(w, g, m, v):
    shape = w.shape
    cols = shape[-1] if w.ndim > 1 else shape[0]
    rows = w.size // cols
    w2, g2, m2, v2 = (t.reshape(rows, cols) for t in (w, g, m, v))
    tr = rows
    while tr * cols * 4 > (1 << 20) and tr % 16 == 0:
        tr //= 2
    c1 = 1.0 - ADAM_B1 ** ADAM_STEP
    c2 = 1.0 - ADAM_B2 ** ADAM_STEP

    def body(w_ref, g_ref, m_ref, v_ref, d_ref, nm_ref, nv_ref):
        gv = g_ref[...]
        mn = ADAM_B1 * m_ref[...] + (1.0 - ADAM_B1) * gv
        vn = ADAM_B2 * v_ref[...] + (1.0 - ADAM_B2) * (gv * gv)
        d_ref[...] = -ADAM_LR * ((mn / c1) / (jnp.sqrt(vn / c2) + ADAM_EPS) + ADAM_WD * w_ref[...])
        nm_ref[...] = mn
        nv_ref[...] = vn

    blk = pl.BlockSpec((tr, cols), lambda i: (i, 0))
    outs = pl.pallas_call(
        body, grid=(rows // tr,), in_specs=[blk] * 4, out_specs=[blk] * 3,
        out_shape=[_sds((rows, cols), F32)] * 3, compiler_params=_cp("parallel"), name="adamw")(w2, g2, m2, v2)
    return tuple(o.reshape(shape) for o in outs)


def _proj_fwd(h, wg, tm=1024):
    def body(h_ref, w_ref, o_ref):
        o_ref[...] = _dot(h_ref[...], w_ref[...])

    return pl.pallas_call(
        body, grid=(NCHIP, NJ, S // tm),
        in_specs=[pl.BlockSpec((tm, D), lambda c, j, i: (i, 0)), pl.BlockSpec((None, D, TN_IN), lambda c, j, i: (c, 0, j))],
        out_specs=pl.BlockSpec((tm, TN_IN), lambda c, j, i: (i, c * NJ + j)), out_shape=_sds((S, DIN), F32),
        compiler_params=_cp("parallel", "parallel", "parallel"), name="proj_fwd")(h, wg)


def _proj_bwd_x(dproj, wg, tm=1024):
    nk = NCHIP * NJ

    def body(d_ref, w_ref, o_ref):
        k = pl.program_id(1)
        part = _dot_nt(d_ref[...], w_ref[...])

        @pl.when(k == 0)
        def _():
            o_ref[...] = part

        @pl.when(k > 0)
        def _():
            o_ref[...] += part

    return pl.pallas_call(
        body, grid=(S // tm, nk),
        in_specs=[pl.BlockSpec((tm, TN_IN), lambda i, k: (i, k)),
                  pl.BlockSpec((None, D, TN_IN), lambda i, k: (k // NJ, 0, k % NJ))],
        out_specs=pl.BlockSpec((tm, D), lambda i, k: (i, 0)), out_shape=_sds((S, D), F32),
        compiler_params=_cp("parallel", "arbitrary"), name="proj_bwd_x")(dproj, wg)


def _proj_bwd_w(h, dproj, tk=1024):
    nk = S // tk

    def body(h_ref, d_ref, o_ref, acc):
        k = pl.program_id(2)
        part = _dot_tn(h_ref[...], d_ref[...])

        @pl.when(k == 0)
        def _():
            acc[...] = part

        @pl.when(k > 0)
        def _():
            acc[...] += part

        @pl.when(k == nk - 1)
        def _():
            o_ref[...] = acc[...].astype(BF16)

    return pl.pallas_call(
        body, grid=(NCHIP, NJ, nk),
        in_specs=[pl.BlockSpec((tk, D), lambda c, j, k: (k, 0)), pl.BlockSpec((tk, TN_IN), lambda c, j, k: (k, c * NJ + j))],
        out_specs=pl.BlockSpec((None, D, TN_IN), lambda c, j, k: (c, 0, j)), out_shape=_sds((NCHIP, D, CW), BF16),
        scratch_shapes=[pltpu.VMEM((D, TN_IN), F32)],
        compiler_params=_cp("parallel", "parallel", "arbitrary"), name="proj_bwd_w")(h, dproj)


def _merge_fwd(y_all, wbr, proj, tm=512):
    cb = D // NCHIP

    def body(y0, y1, y2, y3, w_ref, g0, g1, g2, g3, z_ref):
        acc = None
        for b, (y_ref, g_ref) in enumerate(zip((y0, y1, y2, y3), (g0, g1, g2, g3))):
            t = _dot(y_ref[...], w_ref[b]) * _sigmoid(g_ref[...])
            acc = t if acc is None else acc + t
        z_ref[...] = acc.astype(BF16)

    y_specs = [pl.BlockSpec((None, tm, W), functools.partial(lambda b, c, i: (b, i, 0), b)) for b in range(4)]
    g_specs = [pl.BlockSpec((tm, cb), functools.partial(lambda b, c, i: (i, (P_GM * W + b * D) // cb + c), b)) for b in range(4)]
    return pl.pallas_call(
        body, grid=(NCHIP, S // tm),
        in_specs=y_specs + [pl.BlockSpec((None, 4, W, cb), lambda c, i: (c, 0, 0, 0))] + g_specs,
        out_specs=pl.BlockSpec((tm, cb), lambda c, i: (i, c)), out_shape=_sds((S, D), BF16),
        compiler_params=_cp("parallel", "parallel"), name="merge_fwd")(y_all, y_all, y_all, y_all, wbr, proj, proj, proj, proj)


def _merge_bwd(dz, y_all, wbr, proj, dproj, tm=512):
    cb = D // NCHIP
    ni = S // tm

    def body(dz_ref, y_ref, w_ref, ga_ref, gb_ref, dp_in, dp_ref, dy_ref, dw_ref, acc, obuf, osem):
        b = pl.program_id(0)
        i = pl.program_id(1)
        yv = y_ref[...]
        dys = []

        def fill(slot):
            dy = None
            for c in range(NCHIP):
                wv = w_ref[c]
                t = _dot(yv, wv)
                g_ref = ga_ref if c < 2 else gb_ref
                g = _sigmoid(g_ref[:, (c % 2) * cb:(c % 2 + 1) * cb])
                dzc = dz_ref[:, c * cb:(c + 1) * cb]
                slot[:, c * cb:(c + 1) * cb] = (dzc * t * g * (1.0 - g)).astype(BF16)
                dt = (dzc * g).astype(BF16)
                part = _dot_nt(dt, wv)
                dy = part if dy is None else dy + part
                dwp = _dot_tn(yv, dt)

                @pl.when(i == 0)
                def _():
                    acc[c] = dwp

                @pl.when(i > 0)
                def _():
                    acc[c] += dwp
            dys.append(dy)

        _tile_put(obuf, osem, lambda st: dp_ref.at[pl.ds((st % ni) * tm, tm), pl.ds(P_GM * W + (st // ni) * D, D)],
                  b * ni + i, 4 * ni, fill)
        dy_ref[...] = dys[0]

        @pl.when(i == ni - 1)
        def _():
            dw_ref[...] = acc[...].astype(BF16)

    return pl.pallas_call(
        body, grid=(4, ni),
        in_specs=[pl.BlockSpec((tm, D), lambda b, i: (i, 0)), pl.BlockSpec((None, tm, W), lambda b, i: (b, i, 0)),
                  pl.BlockSpec((NCHIP, None, W, cb), lambda b, i: (0, b, 0, 0)),
                  pl.BlockSpec((tm, W), lambda b, i: (i, P_GM + 2 * b)), pl.BlockSpec((tm, W), lambda b, i: (i, P_GM + 2 * b + 1)), ANY],
        out_specs=[ANY, pl.BlockSpec((None, tm, W), lambda b, i: (b, i, 0)), pl.BlockSpec((NCHIP, None, W, cb), lambda b, i: (0, b, 0, 0))],
        out_shape=[_sds((S, DIN), BF16), _sds((4, S, W), F32), _sds((NCHIP, 4, W, cb), BF16)],
        scratch_shapes=[pltpu.VMEM((NCHIP, W, cb), F32), pltpu.VMEM((2, tm, D), BF16), pltpu.SemaphoreType.DMA((2,))],
        input_output_aliases={5: 0}, compiler_params=_cp("arbitrary", "arbitrary"), name="merge_bwd")(dz, y_all, wbr, proj, proj, dproj)


def _out_fwd(z, wo, x, tm=512):
    def body(z_ref, w_ref, x_ref, o_ref):
        o_ref[...] = x_ref[...] + _dot(z_ref[...], w_ref[...])

    row = pl.BlockSpec((tm, D), lambda i: (i, 0))
    return pl.pallas_call(
        body, grid=(S // tm,), in_specs=[row, pl.BlockSpec((D, D), lambda i: (0, 0)), row], out_specs=row,
        out_shape=_sds((S, D), F32), compiler_params=_cp("parallel"), name="out_fwd")(z, wo, x)


def _out_bwd(dx, z, wo, tm=512):
    ni = S // tm

    def body(dx_ref, z_ref, w_ref, dz_ref, dw_ref, acc):
        i = pl.program_id(0)
        dxb = dx_ref[...].astype(BF16)
        dz_ref[...] = _dot_nt(dxb, w_ref[...])
        part = _dot_tn(z_ref[...], dxb)

        @pl.when(i == 0)
        def _():
            acc[...] = part

        @pl.when(i > 0)
        def _():
            acc[...] += part

        @pl.when(i == ni - 1)
        def _():
            dw_ref[...] = acc[...].astype(BF16)

    row = pl.BlockSpec((tm, D), lambda i: (i, 0))
    full = pl.BlockSpec((D, D), lambda i: (0, 0))
    return pl.pallas_call(
        body, grid=(ni,), in_specs=[row, row, full], out_specs=[row, full],
        out_shape=[_sds((S, D), F32), _sds((D, D), BF16)], scratch_shapes=[pltpu.VMEM((D, D), F32)],
        compiler_params=_cp("arbitrary"), name="out_bwd")(dx, z, wo)


def _gelu_parts(a):
    cdf = 0.5 * (1.0 + lax.erf(a * INV_SQRT2))
    return a * cdf, cdf


def _ln_parts(v):
    mu = jnp.mean(v, axis=-1, keepdims=True)
    vc = v - mu
    rs = lax.rsqrt(jnp.mean(vc * vc, axis=-1, keepdims=True) + EPS)
    return vc * rs, rs


def _causal_mask():
    return lax.broadcasted_iota(jnp.int32, (HD, HD), 0) >= lax.broadcasted_iota(jnp.int32, (HD, HD), 1)


def _gmlp_fwd(proj, lg, lb, ws, bias, y_all, tm=512):
    def body(uv_ref, gt_ref, lg_ref, lb_ref, ws_ref, b_ref, y_in, y_ref):
        act, _ = _gelu_parts(uv_ref[...])
        u = act[:, :W]
        xh, _ = _ln_parts(act[:, W:])
        vn = (xh * lg_ref[...] + lb_ref[...]).astype(BF16)
        gt = gt_ref[...]
        us = u * (gt * _sigmoid(gt))
        mask = _causal_mask()
        for h in range(4):
            wm = jnp.where(mask, ws_ref[h], 0.0).astype(BF16)
            cs = slice(h * HD, (h + 1) * HD)
            for c in range(tm // HD):
                rs_ = slice(c * HD, (c + 1) * HD)
                mixed = _dot(wm, vn[rs_, cs]) + b_ref[h]
                y_ref[rs_, cs] = (us[rs_, cs] * mixed).astype(BF16)

    vec = pl.BlockSpec((1, W), lambda i: (0, 0))
    mats = pl.BlockSpec((4, HD, HD), lambda i: (0, 0, 0))
    return pl.pallas_call(
        body, grid=(S // tm,),
        in_specs=[pl.BlockSpec((tm, 2 * W), lambda i: (i, 0)), pl.BlockSpec((tm, W), lambda i: (i, P_AGATE)), vec, vec, mats, mats, ANY],
        out_specs=pl.BlockSpec((None, tm, W), lambda i: (0, i, 0)), out_shape=_sds((4, S, W), BF16),
        input_output_aliases={6: 0}, compiler_params=_cp("parallel"), name="gmlp_fwd")(proj, proj, lg, lb, ws, bias, y_all)


def _gmlp_bwd(proj, dy_all, lg, lb, ws, bias, dproj, tm=256):
    ni = S // tm

    def body(uv_ref, gt_ref, dy_ref, lg_ref, lb_ref, ws_ref, b_ref, dp_in, dp_ref, dws_ref, dbs_ref, dlg_ref, dlb_ref, mix_s, dvn_s):
        i = pl.program_id(0)
        a0 = uv_ref[...]
        act, cdf = _gelu_parts(a0)
        u = act[:, :W]
        xh, rs = _ln_parts(act[:, W:])
        lgv = lg_ref[...]
        vn = (xh * lgv + lb_ref[...]).astype(BF16)
        mask = _causal_mask()
        wms = [jnp.where(mask, ws_ref[h], 0.0).astype(BF16) for h in range(4)]
        for h in range(4):
            cs = slice(h * HD, (h + 1) * HD)
            for c in range(tm // HD):
                rs_ = slice(c * HD, (c + 1) * HD)
                mix_s[rs_, cs] = _dot(wms[h], vn[rs_, cs]) + b_ref[h]
        mixed = mix_s[...]
        gt = gt_ref[...]
        sg = _sigmoid(gt)
        sl = gt * sg
        dyv = dy_ref[...]
        dum = dyv * sl
        dgate = dyv * (u * mixed) * (sg * (1.0 + gt * (1.0 - sg)))
        du = dum * mixed
        dmix = dum * u
        dmb = dmix.astype(BF16)
        for h in range(4):
            cs = slice(h * HD, (h + 1) * HD)
            dw = None
            db = None
            for c in range(tm // HD):
                rs_ = slice(c * HD, (c + 1) * HD)
                dvn_s[rs_, cs] = _dot_tn(wms[h], dmb[rs_, cs])
                pw = _dot_nt(dmb[rs_, cs], vn[rs_, cs])
                dw = pw if dw is None else dw + pw
                db = dmix[rs_, cs] if db is None else db + dmix[rs_, cs]

            @pl.when(i == 0)
            def _():
                dws_ref[h] = dw
                dbs_ref[h] = db

            @pl.when(i > 0)
            def _():
                dws_ref[h] += dw
                dbs_ref[h] += db

            @pl.when(i == ni - 1)
            def _():
                dws_ref[h] = jnp.where(mask, dws_ref[h], 0.0)
                dbs_ref[h] = jnp.broadcast_to(jnp.sum(dbs_ref[h], axis=1, keepdims=True), (HD, HD))
        dvn = dvn_s[...]
        plg = jnp.sum(dvn * xh, axis=0, keepdims=True)
        plb = jnp.sum(dvn, axis=0, keepdims=True)

        @pl.when(i == 0)
        def _():
            dlg_ref[...] = plg
            dlb_ref[...] = plb

        @pl.when(i > 0)
        def _():
            dlg_ref[...] += plg
            dlb_ref[...] += plb

        dxh = dvn * lgv
        dv = rs * (dxh - jnp.mean(dxh, axis=-1, keepdims=True) - xh * jnp.mean(dxh * xh, axis=-1, keepdims=True))
        gp = cdf + a0 * (jnp.exp(-0.5 * a0 * a0) * INV_SQRT2PI)
        dp_ref[:, :W] = (du * gp[:, :W]).astype(BF16)
        dp_ref[:, W:2 * W] = (dv * gp[:, W:]).astype(BF16)
        dp_ref[:, 2 * W:] = dgate.astype(BF16)

    vec = pl.BlockSpec((1, W), lambda i: (0, 0))
    mats = pl.BlockSpec((4, HD, HD), lambda i: (0, 0, 0))
    return pl.pallas_call(
        body, grid=(ni,),
        in_specs=[pl.BlockSpec((tm, 2 * W), lambda i: (i, 0)), pl.BlockSpec((tm, W), lambda i: (i, P_AGATE)),
                  pl.BlockSpec((None, tm, W), lambda i: (0, i, 0)), vec, vec, mats, mats, ANY],
        out_specs=[pl.BlockSpec((tm, 3 * W), lambda i: (i, 0)), mats, mats, vec, vec],
        out_shape=[_sds((S, DIN), BF16), _sds((4, HD, HD), F32), _sds((4, HD, HD), F32), _sds((1, W), F32), _sds((1, W), F32)],
        scratch_shapes=[pltpu.VMEM((tm, W), F32), pltpu.VMEM((tm, W), F32)],
        input_output_aliases={7: 0}, compiler_params=_cp("arbitrary"), name="gmlp_bwd")(proj, proj, dy_all, lg, lb, ws, bias, dproj)


def _pool_diff(p, halo, row0, tm):
    xx = jnp.concatenate([halo, p], axis=0)
    t1 = (row0 + 1 + lax.broadcasted_iota(jnp.int32, (tm, 1), 0)).astype(F32)
    out = []
    for g, win in enumerate(POOL_WINDOWS):
        s = xx[:, g * HD:(g + 1) * HD]
        sh = 1
        while sh < win:
            s = s + pltpu.roll(s, sh, 0)
            sh *= 2
        out.append(s[HALO:] / jnp.minimum(t1, float(win)) - p[:, g * HD:(g + 1) * HD])
    return out


def _pool_fwd(proj, pw, sc, y_all, tm=512):
    rb = tm // HALO

    def body(p_ref, h_ref, gt_ref, pw_ref, sc_ref, y_in, y_ref):
        i = pl.program_id(0)
        halo = jnp.where(i > 0, h_ref[...], 0.0)
        ds = _pool_diff(p_ref[...], halo, i * tm, tm)
        gt = gt_ref[...]
        sl = gt * _sigmoid(gt)
        for g in range(4):
            cs = slice(g * HD, (g + 1) * HD)
            lin = _dot(ds[g].astype(BF16), pw_ref[g].astype(BF16))
            y_ref[:, cs] = (lin * sc_ref[:, cs] * sl[:, cs]).astype(BF16)

    return pl.pallas_call(
        body, grid=(S // tm,),
        in_specs=[pl.BlockSpec((tm, W), lambda i: (i, P_PIN)),
                  pl.BlockSpec((HALO, W), lambda i: (jnp.maximum(i * rb - 1, 0), P_PIN)),
                  pl.BlockSpec((tm, W), lambda i: (i, P_PGATE)),
                  pl.BlockSpec((4, HD, HD), lambda i: (0, 0, 0)), pl.BlockSpec((1, W), lambda i: (0, 0)), ANY],
        out_specs=pl.BlockSpec((None, tm, W), lambda i: (1, i, 0)), out_shape=_sds((4, S, W), BF16),
        input_output_aliases={5: 0}, compiler_params=_cp("parallel"), name="pool_fwd")(proj, proj, proj, pw, sc, y_all)


def _pool_bwd(proj, dy_all, pw, sc, dproj, tm=256):
    ni = S // tm
    rb = tm // HALO
    last_rb = S // HALO - 1
    rx = tm + HALO

    def body(p_ref, h_ref, gt_ref, gh_ref, dy_ref, dyh_ref, pw_ref, sc_ref, dp_in, dp_ref, dpw_ref, dsc_ref, obuf, osem):
        i = pl.program_id(0)
        halo = jnp.where(i > 0, h_ref[...], 0.0)
        ds = _pool_diff(p_ref[...], halo, i * tm, tm)
        nxt = i < ni - 1
        gx = jnp.concatenate([gt_ref[...], gh_ref[...]], axis=0)
        dyx = jnp.concatenate([dy_ref[...], jnp.where(nxt, dyh_ref[...], 0.0)], axis=0)
        sgx = _sigmoid(gx)
        slx = gx * sgx
        scv = sc_ref[...]
        dlinx = dyx * slx * scv
        t1 = (i * tm + 1 + lax.broadcasted_iota(jnp.int32, (rx, 1), 0)).astype(F32)
        gt, sg, sl, dyv = gx[:tm], sgx[:tm], slx[:tm], dyx[:tm]
        dsl = sg * (1.0 + gt * (1.0 - sg))

        def fill(slot):
            for g, win in enumerate(POOL_WINDOWS):
                cs = slice(g * HD, (g + 1) * HD)
                wv = pw_ref[g].astype(BF16)
                dlb = dlinx[:, cs].astype(BF16)
                ddx = _dot_nt(dlb, wv)
                f = ddx / jnp.minimum(t1, float(win))
                sh = 1
                while sh < win:
                    f = f + pltpu.roll(f, rx - sh, 0)
                    sh *= 2
                slot[:, cs] = (f[:tm] - ddx[:tm]).astype(BF16)
                db = ds[g].astype(BF16)
                lin = _dot(db, wv)
                slot[:, W + g * HD:W + (g + 1) * HD] = (dyv[:, cs] * lin * scv[:, cs] * dsl[:, cs]).astype(BF16)
                psc = jnp.sum(dyv[:, cs] * sl[:, cs] * lin, axis=0, keepdims=True)
                pwg = _dot_tn(db, dlb[:tm])

                @pl.when(i == 0)
                def _():
                    dpw_ref[g] = pwg
                    dsc_ref[:, cs] = psc

                @pl.when(i > 0)
                def _():
                    dpw_ref[g] += pwg
                    dsc_ref[:, cs] += psc

        _tile_put(obuf, osem, lambda st: dp_ref.at[pl.ds(st * tm, tm), pl.ds(P_PIN * W, 2 * W)], i, ni, fill)

    mats = pl.BlockSpec((4, HD, HD), lambda i: (0, 0, 0))
    vec = pl.BlockSpec((1, W), lambda i: (0, 0))
    return pl.pallas_call(
        body, grid=(ni,),
        in_specs=[pl.BlockSpec((tm, W), lambda i: (i, P_PIN)),
                  pl.BlockSpec((HALO, W), lambda i: (jnp.maximum(i * rb - 1, 0), P_PIN)),
                  pl.BlockSpec((tm, W), lambda i: (i, P_PGATE)),
                  pl.BlockSpec((HALO, W), lambda i: (jnp.minimum((i + 1) * rb, last_rb), P_PGATE)),
                  pl.BlockSpec((None, tm, W), lambda i: (1, i, 0)),
                  pl.BlockSpec((None, HALO, W), lambda i: (1, jnp.minimum((i + 1) * rb, last_rb), 0)),
                  mats, vec, ANY],
        out_specs=[ANY, mats, vec],
        out_shape=[_sds((S, DIN), BF16), _sds((4, HD, HD), F32), _sds((1, W), F32)],
        scratch_shapes=[pltpu.VMEM((2, tm, 2 * W), BF16), pltpu.SemaphoreType.DMA((2,))],
        input_output_aliases={8: 0}, compiler_params=_cp("arbitrary"), name="pool_bwd")(proj, proj, proj, proj, dy_all, dy_all, pw, sc, dproj)


ATT_STEP = ((1, 4), (4, 1), (4, 1))


def _att_masks():
    qi = lax.broadcasted_iota(jnp.int32, (HD, HD), 0)
    kj = lax.broadcasted_iota(jnp.int32, (HD, HD), 1)
    return kj >= qi, kj <= qi


def _dilate(src, dst, d, rows, cast=None):
    for r in range(d):
        for h in range(4):
            v = src.at[h][pl.ds(r, rows // d, stride=d), :] if d > 1 else src[h]
            dst[r * 4 + h] = v if cast is None else v.astype(cast)


def _undilate(src, dst, d, rows):
    for r in range(d):
        for h in range(4):
            if d > 1:
                dst.at[h][pl.ds(r, rows // d, stride=d), :] = src[r * 4 + h].astype(F32)
            else:
                dst[h] = src[h].astype(F32)


def _dil_spec(d, tm):
    return pl.BlockSpec((4 * d, tm // d, HD), lambda i: (0, i, 0))


def _att_prep(proj, tm=512):
    def body(q0, q1, q2, k_ref, v_ref, *rest):
        outs, scr = rest[:9], rest[9]
        for j, (src, dsts) in enumerate(((q0, ((0, outs[0]),)), (q1, ((1, outs[1]),)), (q2, ((2, outs[2]),)),
                                         (k_ref, tuple((g, outs[3 + g]) for g in range(3))),
                                         (v_ref, tuple((g, outs[6 + g]) for g in range(3))))):
            for h in range(4):
                scr[j, h] = src[:, h * HD:(h + 1) * HD]
            for g, dst in dsts:
                _dilate(scr.at[j], dst, DILATIONS[g], tm, BF16)

    def piece(p):
        return pl.BlockSpec((tm, W), lambda i: (i, p))

    shapes = [_sds((4 * d, S // d, HD), BF16) for d in DILATIONS]
    res = pl.pallas_call(
        body, grid=(S // tm,),
        in_specs=[piece(P_CQ), piece(P_CQ + 1), piece(P_CQ + 2), piece(P_CK), piece(P_CV)],
        out_specs=[_dil_spec(d, tm) for d in DILATIONS] * 3, out_shape=shapes * 3,
        scratch_shapes=[pltpu.VMEM((5, 4, tm, HD), F32)],
        compiler_params=_cp("parallel"), name="att_prep")(proj, proj, proj, proj, proj)
    return res[0:3], res[3:6], res[6:9]


def _att_specs(g):
    d = DILATIONS[g]
    nres, njb = ATT_STEP[g]
    nb = S // d // HD
    own = pl.BlockSpec((4 * nres, njb * HD, HD), lambda r, j: (r, j, 0))
    prev = pl.BlockSpec((4 * nres, HD, HD), lambda r, j: (r, jnp.maximum(j * njb - 1, 0), 0))
    nxt = pl.BlockSpec((4 * nres, HD, HD), lambda r, j: (r, jnp.minimum((j + 1) * njb, nb - 1), 0))
    return (d // nres, nb // njb), own, prev, nxt


def _att_fwd(q, k, v, g):
    d = DILATIONS[g]
    nres, njb = ATT_STEP[g]
    grid, own, prev, _ = _att_specs(g)

    def body(q_ref, kp_ref, ko_ref, vp_ref, vo_ref, o_ref, l_ref):
        jb = pl.program_id(1)
        m_prev, m_own = _att_masks()
        no_prev = jnp.where(jb > 0, 0.0, NEG)
        for a in range(4 * nres):
            for jj in range(njb):
                rs_ = slice(jj * HD, (jj + 1) * HD)
                qb = q_ref[a, rs_, :]
                ko, vo = ko_ref[a, rs_, :], vo_ref[a, rs_, :]
                if jj == 0:
                    kp, vp, miss = kp_ref[a], vp_ref[a], no_prev
                else:
                    kp, vp, miss = ko_ref[a, (jj - 1) * HD:jj * HD, :], vo_ref[a, (jj - 1) * HD:jj * HD, :], 0.0
                sp = jnp.where(m_prev, _dot_nt(qb, kp) * SCALE, NEG) + miss
                so = jnp.where(m_own, _dot_nt(qb, ko) * SCALE, NEG)
                m = jnp.maximum(jnp.max(sp, axis=-1, keepdims=True), jnp.max(so, axis=-1, keepdims=True))
                ep = jnp.exp(sp - m)
                eo = jnp.exp(so - m)
                den = jnp.sum(ep, axis=-1, keepdims=True) + jnp.sum(eo, axis=-1, keepdims=True)
                inv = 1.0 / den
                o_ref[a, rs_, :] = _dot((ep * inv).astype(BF16), vp) + _dot((eo * inv).astype(BF16), vo)
                l_ref[a, rs_, :] = jnp.broadcast_to(m + jnp.log(den), (HD, HD))

    return pl.pallas_call(
        body, grid=grid, in_specs=[own, prev, own, prev, own], out_specs=[own, own],
        out_shape=[_sds((4 * d, S // d, HD), F32)] * 2,
        compiler_params=_cp("parallel", "parallel"), name="att_fwd")(q, k, k, v, v)


def _att_mix(os_, ls_, proj, y_all, tm=512):
    def body(o0, o1, o2, l0, l1, l2, gt_ref, y_in, y_ref, om_ref, lt_ref, so1, so2, sl1, sl2):
        _undilate(o1, so1, DILATIONS[1], tm)
        _undilate(o2, so2, DILATIONS[2], tm)
        _undilate(l1, sl1, DILATIONS[1], tm)
        _undilate(l2, sl2, DILATIONS[2], tm)
        for h in range(4):
            a, b, c = l0[h], sl1[h], sl2[h]
            m = jnp.maximum(jnp.maximum(a, b), c)
            ea, eb, ec = jnp.exp(a - m), jnp.exp(b - m), jnp.exp(c - m)
            z = ea + eb + ec
            inv = 1.0 / z
            o = (ea * inv) * o0[h] + (eb * inv) * so1[h] + (ec * inv) * so2[h]
            gt = gt_ref[:, h * HD:(h + 1) * HD]
            om_ref[h] = o
            lt_ref[h] = m + jnp.log(z)
            y_ref[:, h * HD:(h + 1) * HD] = (o * (gt * _sigmoid(gt))).astype(BF16)

    dil = [_dil_spec(d, tm) for d in DILATIONS]
    return pl.pallas_call(
        body, grid=(S // tm,),
        in_specs=dil * 2 + [pl.BlockSpec((tm, W), lambda i: (i, P_CGATE)), ANY],
        out_specs=[pl.BlockSpec((None, tm, W), lambda i: (2, i, 0)), dil[0], dil[0]],
        out_shape=[_sds((4, S, W), BF16), _sds((4, S, HD), F32), _sds((4, S, HD), F32)],
        scratch_shapes=[pltpu.VMEM((4, tm, HD), F32)] * 4,
        input_output_aliases={7: 0}, compiler_params=_cp("parallel"), name="att_mix")(*os_, *ls_, proj, y_all)


def _att_bwd_pre(dy_all, proj, om, lse, dproj, tm=512):
    def body(dy_ref, gt_ref, om_ref, ls_ref, dp_in, *rest):
        dos, dls, lss, dp_ref, sdo, sdl = rest[0:3], rest[3:6], rest[6:8], rest[8], rest[9], rest[10]
        for h in range(4):
            cs = slice(h * HD, (h + 1) * HD)
            gt = gt_ref[:, cs]
            sg = _sigmoid(gt)
            dyv = dy_ref[:, cs]
            o = om_ref[h]
            do = dyv * (gt * sg)
            dp_ref[:, cs] = (dyv * o * (sg * (1.0 + gt * (1.0 - sg)))).astype(BF16)
            sdo[h] = do
            sdl[h] = jnp.broadcast_to(jnp.sum(do * o, axis=-1, keepdims=True), (tm, HD))
        for g, d in enumerate(DILATIONS):
            _dilate(sdo, dos[g], d, tm, BF16)
            _dilate(sdl, dls[g], d, tm)
            if g > 0:
                _dilate(ls_ref, lss[g - 1], d, tm)

    dil = [_dil_spec(d, tm) for d in DILATIONS]
    gcol = pl.BlockSpec((tm, W), lambda i: (i, P_CGATE))
    res = pl.pallas_call(
        body, grid=(S // tm,),
        in_specs=[pl.BlockSpec((None, tm, W), lambda i: (2, i, 0)), gcol, dil[0], dil[0], ANY],
        out_specs=dil + dil + dil[1:] + [gcol],
        out_shape=([_sds((4 * d, S // d, HD), BF16) for d in DILATIONS] + [_sds((4 * d, S // d, HD), F32) for d in DILATIONS]
                   + [_sds((4 * d, S // d, HD), F32) for d in DILATIONS[1:]] + [_sds((S, DIN), BF16)]),
        scratch_shapes=[pltpu.VMEM((4, tm, HD), F32)] * 2,
        input_output_aliases={4: 8}, compiler_params=_cp("parallel"), name="att_bwd_pre")(dy_all, proj, om, lse, dproj)
    return res[0:3], res[3:6], [lse] + list(res[6:8]), res[8]


def _att_bwd(q, k, v, do, lse, delta, g):
    d = DILATIONS[g]
    nres, njb = ATT_STEP[g]
    grid, own, prev, nxt = _att_specs(g)

    def body(qa_ref, qn_ref, kp_ref, ko_ref, vp_ref, vo_ref, doa_ref, don_ref, la_ref, ln_ref, da_ref, dn_ref,
             dq_ref, dk_ref, dv_ref):
        jb = pl.program_id(1)
        m_prev, m_own = _att_masks()
        has_prev = jnp.where(jb > 0, 1.0, 0.0)
        has_next = jnp.where(jb < grid[1] - 1, 1.0, 0.0)

        def pair(q, k, v, dov_, lsev, dlt, mask, flag=None):
            p = jnp.where(mask, jnp.exp(_dot_nt(q, k) * SCALE - lsev), 0.0)
            if flag is not None:
                p = p * flag
            dsb = (p * (_dot_nt(dov_, v) - dlt) * SCALE).astype(BF16)
            return p.astype(BF16), dsb

        for a in range(4 * nres):
            for jj in range(njb):
                rs_ = slice(jj * HD, (jj + 1) * HD)
                qa, ko, vo = qa_ref[a, rs_, :], ko_ref[a, rs_, :], vo_ref[a, rs_, :]
                doa, lsa, dla = doa_ref[a, rs_, :], la_ref[a, rs_, :], da_ref[a, rs_, :]
                if jj == 0:
                    kp, vp, fp = kp_ref[a], vp_ref[a], has_prev
                else:
                    kp, vp, fp = ko_ref[a, (jj - 1) * HD:jj * HD, :], vo_ref[a, (jj - 1) * HD:jj * HD, :], None
                if jj == njb - 1:
                    qn, don, lsn, dln, fn = qn_ref[a], don_ref[a], ln_ref[a], dn_ref[a], has_next
                else:
                    ns = slice((jj + 1) * HD, (jj + 2) * HD)
                    qn, don, lsn, dln, fn = qa_ref[a, ns, :], doa_ref[a, ns, :], la_ref[a, ns, :], da_ref[a, ns, :], None
                p, dsb = pair(qa, ko, vo, doa, lsa, dla, m_own)
                dq = _dot(dsb, ko)
                dk = _dot_tn(dsb, qa)
                dv = _dot_tn(p, doa)
                p, dsb = pair(qa, kp, vp, doa, lsa, dla, m_prev, fp)
                dq = dq + _dot(dsb, kp)
                p, dsb = pair(qn, ko, vo, don, lsn, dln, m_prev, fn)
                dk = dk + _dot_tn(dsb, qn)
                dv = dv + _dot_tn(p, don)
                dq_ref[a, rs_, :] = dq
                dk_ref[a, rs_, :] = dk
                dv_ref[a, rs_, :] = dv

    return pl.pallas_call(
        body, grid=grid, in_specs=[own, nxt, prev, own, prev, own, own, nxt, own, nxt, own, nxt],
        out_specs=[own, own, own], out_shape=[_sds((4 * d, S // d, HD), F32)] * 3,
        compiler_params=_cp("parallel", "parallel"), name="att_bwd")(q, q, k, k, v, v, do, do, lse, lse, delta, delta)


def _att_bwd_post(dqs, dks, dvs, dproj, tm=512):
    def body(*refs):
        dq, dk, dv, dp_ref, scr = refs[0:3], refs[3:6], refs[6:9], refs[10], refs[11]
        for g in range(3):
            _undilate(dq[g], scr, DILATIONS[g], tm)
            for h in range(4):
                dp_ref[:, g * W + h * HD:g * W + (h + 1) * HD] = scr[h].astype(BF16)
        for j, parts in enumerate((dk, dv)):
            acc = None
            for g in range(3):
                _undilate(parts[g], scr, DILATIONS[g], tm)
                vals = [scr[h] for h in range(4)]
                acc = vals if acc is None else [x + y for x, y in zip(acc, vals)]
            for h in range(4):
                dp_ref[:, (3 + j) * W + h * HD:(3 + j) * W + (h + 1) * HD] = acc[h].astype(BF16)

    dil = [_dil_spec(d, tm) for d in DILATIONS]
    return pl.pallas_call(
        body, grid=(S // tm,), in_specs=dil * 3 + [ANY],
        out_specs=pl.BlockSpec((tm, 5 * W), lambda i: (i, 1)), out_shape=_sds((S, DIN), BF16),
        scratch_shapes=[pltpu.VMEM((4, tm, HD), F32)],
        input_output_aliases={9: 0}, compiler_params=_cp("parallel"), name="att_bwd_post")(*dqs, *dks, *dvs, dproj)


def _mem_kv_fwd(mem_n, wkv):
    m = mem_n.shape[0]

    def body(a_ref, w_ref, o_ref):
        o_ref[...] = _dot(a_ref[...], w_ref[...])

    return pl.pallas_call(body, out_shape=_sds((m, 2 * W), F32), compiler_params=_cp(), name="mem_kv_fwd")(mem_n, wkv)


def _mem_softmax(q, k):
    s = _dot_nt(q, k) * SCALE
    e = jnp.exp(s - jnp.max(s, axis=-1, keepdims=True))
    return e * (1.0 / jnp.sum(e, axis=-1, keepdims=True))


def _mem_fwd(proj, kv, y_all, tm=512):
    m = kv.shape[0]

    def body(q_ref, gt_ref, kv_ref, y_in, y_ref):
        gt = gt_ref[...]
        sl = gt * _sigmoid(gt)
        for h in range(4):
            cs = slice(h * HD, (h + 1) * HD)
            p = _mem_softmax(q_ref[:, cs].astype(BF16), kv_ref[:, cs].astype(BF16))
            o = _dot(p.astype(BF16), kv_ref[:, W + h * HD:W + (h + 1) * HD].astype(BF16))
            y_ref[:, cs] = (o * sl[:, cs]).astype(BF16)

    return pl.pallas_call(
        body, grid=(S // tm,),
        in_specs=[pl.BlockSpec((tm, W), lambda i: (i, P_MQ)), pl.BlockSpec((tm, W), lambda i: (i, P_MGATE)),
                  pl.BlockSpec((m, 2 * W), lambda i: (0, 0)), ANY],
        out_specs=pl.BlockSpec((None, tm, W), lambda i: (3, i, 0)), out_shape=_sds((4, S, W), BF16),
        input_output_aliases={3: 0}, compiler_params=_cp("parallel"), name="mem_fwd")(proj, proj, kv, y_all)


def _mem_bwd(proj, kv, dy_all, dproj, tm=512):
    m = kv.shape[0]
    ni = S // tm

    def body(q_ref, gt_ref, kv_ref, dy_ref, dp_in, dp_ref, dkv_ref, obuf, osem):
        i = pl.program_id(0)
        gt = gt_ref[...]
        sg = _sigmoid(gt)
        sl = gt * sg
        dsl = sg * (1.0 + gt * (1.0 - sg))
        dyv = dy_ref[...]

        def fill(slot):
            for h in range(4):
                cs = slice(h * HD, (h + 1) * HD)
                vs = slice(W + h * HD, W + (h + 1) * HD)
                q = q_ref[:, cs].astype(BF16)
                k = kv_ref[:, cs].astype(BF16)
                v = kv_ref[:, vs].astype(BF16)
                p = _mem_softmax(q, k)
                pb = p.astype(BF16)
                o = _dot(pb, v)
                do = dyv[:, cs] * sl[:, cs]
                dob = do.astype(BF16)
                dp = _dot_nt(dob, v)
                dsb = (p * (dp - jnp.sum(dp * p, axis=-1, keepdims=True)) * SCALE).astype(BF16)
                slot[:, cs] = _dot(dsb, k).astype(BF16)
                slot[:, vs] = (dyv[:, cs] * o * dsl[:, cs]).astype(BF16)
                dk = _dot_tn(dsb, q)
                dv = _dot_tn(pb, dob)

                @pl.when(i == 0)
                def _():
                    dkv_ref[:, cs] = dk
                    dkv_ref[:, vs] = dv

                @pl.when(i > 0)
                def _():
                    dkv_ref[:, cs] += dk
                    dkv_ref[:, vs] += dv

        _tile_put(obuf, osem, lambda st: dp_ref.at[pl.ds(st * tm, tm), pl.ds(P_MQ * W, 2 * W)], i, ni, fill)

    return pl.pallas_call(
        body, grid=(ni,),
        in_specs=[pl.BlockSpec((tm, W), lambda i: (i, P_MQ)), pl.BlockSpec((tm, W), lambda i: (i, P_MGATE)),
                  pl.BlockSpec((m, 2 * W), lambda i: (0, 0)), pl.BlockSpec((None, tm, W), lambda i: (3, i, 0)), ANY],
        out_specs=[ANY, pl.BlockSpec((m, 2 * W), lambda i: (0, 0))],
        out_shape=[_sds((S, DIN), BF16), _sds((m, 2 * W), F32)],
        scratch_shapes=[pltpu.VMEM((2, tm, 2 * W), BF16), pltpu.SemaphoreType.DMA((2,))],
        input_output_aliases={4: 0}, compiler_params=_cp("arbitrary"), name="mem_bwd")(proj, proj, kv, dy_all, dproj)


def _mem_kv_bwd(mem, g, mem_n, wkv, dkv):
    m = mem.shape[0]

    def body(x_ref, g_ref, a_ref, w_ref, d_ref, dw_ref, dg_ref):
        db = d_ref[...].astype(BF16)
        dw_ref[...] = _dot_tn(a_ref[...], db).astype(BF16)
        dn = _dot_nt(db, w_ref[...])
        xv = x_ref[...]
        xh = xv * lax.rsqrt(jnp.mean(xv * xv, axis=-1, keepdims=True) + EPS)
        dg_ref[...] = jnp.sum(dn * xh, axis=0, keepdims=True)

    return pl.pallas_call(
        body, out_shape=[_sds((D, 2 * W), BF16), _sds((1, D), F32)], compiler_params=_cp(), name="mem_kv_bwd")(mem, g, mem_n, wkv, dkv)


def _layer_fwd(x, mem, p, wg):
    win, wkv, wbr, wo = wg
    h = _rms_fwd(x, p["norm_g"], 512)
    proj = _proj_fwd(h, win)
    y_all = lax.empty((4, S, W), BF16)
    y_all = _gmlp_fwd(proj, p["gm_ln_g"], p["gm_ln_b"], p["gm_ws"], p["gm_bias"], y_all)
    y_all = _pool_fwd(proj, p["pool_w"], p["pool_scale"], y_all)
    qs, ks, vs = _att_prep(proj)
    os_, ls_ = zip(*[_att_fwd(qs[g], ks[g], vs[g], g) for g in range(3)])
    y_all, om, lse = _att_mix(os_, ls_, proj, y_all)
    mem_n = _rms_fwd(mem, p["mem_norm_g"], mem.shape[0])
    kv = _mem_kv_fwd(mem_n, wkv)
    y_all = _mem_fwd(proj, kv, y_all)
    z = _merge_fwd(y_all, wbr, proj)
    x_new = _out_fwd(z, wo, x)
    return x_new, dict(x=x, h=h, proj=proj, y_all=y_all, om=om, lse=lse, mem_n=mem_n, kv=kv, z=z, qkv=(qs, ks, vs))


def _layer_bwd(dx, mem, p, wg, sv):
    win, wkv, wbr, wo = wg
    proj = sv["proj"]
    dz, d_wo = _out_bwd(dx, sv["z"], wo)
    dproj = lax.empty((S, DIN), BF16)
    dproj, dy_all, d_wbr = _merge_bwd(dz, sv["y_all"], wbr, proj, dproj)
    dproj, d_ws, d_bs, d_lg, d_lb = _gmlp_bwd(proj, dy_all, p["gm_ln_g"], p["gm_ln_b"], p["gm_ws"], p["gm_bias"], dproj)
    dproj, d_pw, d_sc = _pool_bwd(proj, dy_all, p["pool_w"], p["pool_scale"], dproj)
    dos, dls, lss, dproj = _att_bwd_pre(dy_all, proj, sv["om"], sv["lse"], dproj)
    qs, ks, vs = sv["qkv"]
    dqs, dks, dvs = zip(*[_att_bwd(qs[g], ks[g], vs[g], dos[g], lss[g], dls[g], g) for g in range(3)])
    dproj = _att_bwd_post(dqs, dks, dvs, dproj)
    dproj, dkv = _mem_bwd(proj, sv["kv"], dy_all, dproj)
    d_wkv, d_mg = _mem_kv_bwd(mem, p["mem_norm_g"], sv["mem_n"], wkv, dkv)
    dh = _proj_bwd_x(dproj, win)
    d_win = _proj_bwd_w(sv["h"], dproj)
    dx_in, d_ng = _rms_bwd(dh, sv["x"], p["norm_g"], dx)
    big = (d_win, d_wkv, d_wbr, d_wo)
    small = dict(norm_g=d_ng, gm_ln_g=d_lg, gm_ln_b=d_lb, gm_ws=d_ws, gm_bs=d_bs[:, :, 0], pool_w=d_pw, pool_scale=d_sc, mem_norm_g=d_mg)
    return dx_in, big, small


_SMALL = ("norm_g", "gm_ln_g", "gm_ln_b", "gm_ws", "gm_bs", "pool_w", "pool_scale", "mem_norm_g")


def _layer_params(l, norm_g, gm_ln_g, gm_ln_b, gm_ws, gm_bs, pool_w, pool_scale, mem_norm_g):
    return dict(norm_g=norm_g[l][None], gm_ln_g=gm_ln_g[l][None], gm_ln_b=gm_ln_b[l][None], gm_ws=gm_ws[l],
                gm_bias=jnp.broadcast_to(gm_bs[l][:, :, None], (4, HD, HD)), pool_w=pool_w[l],
                pool_scale=pool_scale[l][None], mem_norm_g=mem_norm_g[l][None])


def kernel(x, mem, norm_g, w_in, gm_ln_g, gm_ln_b, gm_ws, gm_bs, pool_w, pool_scale, mem_norm_g, w_mem_kv, w_branch, w_out, final_norm_g, loss_target, m_norm_g, m_w_in, m_gm_ln_g, m_gm_ln_b, m_gm_ws, m_gm_bs, m_pool_w, m_pool_scale, m_mem_norm_g, m_w_mem_kv, m_w_branch, m_w_out, m_final_norm_g, v_norm_g, v_w_in, v_gm_ln_g, v_gm_ln_b, v_gm_ws, v_gm_bs, v_pool_w, v_pool_scale, v_mem_norm_g, v_w_mem_kv, v_w_branch, v_w_out, v_final_norm_g):
    xs, memv, tgt = x[0], mem[0], loss_target[0]
    params = [_layer_params(l, norm_g, gm_ln_g, gm_ln_b, gm_ws, gm_bs, pool_w, pool_scale, mem_norm_g) for l in range(NL)]

    win_g, wkv_g, wbr_g, wo_g = _gather_weights(w_in.astype(BF16), w_mem_kv.astype(BF16), w_branch.astype(BF16), w_out.astype(BF16))
    wgs = [(win_g[l], wkv_g[l].reshape(D, 2 * W), wbr_g[l], wo_g[l].reshape(D, D)) for l in range(NL)]

    saved = []
    for l in range(NL):
        xs, sv = _layer_fwd(xs, memv, params[l], wgs[l])
        saved.append(sv)
    dx, d_fg, ls = _loss_head(xs, tgt, final_norm_g[None])
    loss = lax.psum(ls[0, 0], ("x", "y", "c"))

    big, small = [None] * NL, [None] * NL
    for l in reversed(range(NL)):
        dx, big[l], small[l] = _layer_bwd(dx, memv, params[l], wgs[l], saved[l])
    grad_x = dx[None]

    shapes2 = ((2, D // 2, CW), (2, D // 8, 2 * W), (2, 2 * W, D // NCHIP), (2, D // 8, D))
    parts = [[big[l][k].reshape((NCHIP,) + shapes2[k]) for l in range(NL)] for k in range(4)]
    recv = _exchange_grads(parts)
    halves = [_sum_slots(recv[k], tr) for k, tr in zip(range(4), (64, 128, 256, 128))]
    full = _share_halves(halves)
    g_w_in = full[0].reshape(NL, D, CW)
    g_w_kv = full[1].reshape(NL, D // NCHIP, 2 * W)
    g_w_br = full[2].reshape(NL, 4, W, D // NCHIP)
    g_w_out = full[3].reshape(NL, D // NCHIP, D)

    leaves = [jnp.stack([small[l][n] for l in range(NL)]) for n in _SMALL] + [d_fg]
    sizes = [t.size for t in leaves]
    packed = jnp.concatenate([t.reshape(-1, 128) for t in leaves], axis=0)
    rows = packed.shape[0]
    tr = max(t for t in range(8, 513, 8) if rows % t == 0)
    tot = _sum_slots(_exchange_small(packed)[None], tr)[0]
    offs = [0]
    for sz in sizes:
        offs.append(offs[-1] + sz // 128)
    small_shapes = [norm_g.shape, gm_ln_g.shape, gm_ln_b.shape, gm_ws.shape, gm_bs.shape, pool_w.shape, pool_scale.shape,
                    mem_norm_g.shape, final_norm_g.shape]
    gs = [tot[offs[i]:offs[i + 1]].reshape(small_shapes[i]) for i in range(len(leaves))]
    g_norm_g, g_ln_g, g_ln_b, g_ws, g_bs, g_pw, g_sc, g_mg, g_fg = gs

    grads = [g_norm_g, g_w_in, g_ln_g, g_ln_b, g_ws, g_bs, g_pw, g_sc, g_mg, g_w_kv, g_w_br, g_w_out, g_fg]
    ws = [norm_g, w_in, gm_ln_g, gm_ln_b, gm_ws, gm_bs, pool_w, pool_scale, mem_norm_g, w_mem_kv, w_branch, w_out, final_norm_g]
    ms = [m_norm_g, m_w_in, m_gm_ln_g, m_gm_ln_b, m_gm_ws, m_gm_bs, m_pool_w, m_pool_scale, m_mem_norm_g, m_w_mem_kv, m_w_branch, m_w_out, m_final_norm_g]
    vs = [v_norm_g, v_w_in, v_gm_ln_g, v_gm_ln_b, v_gm_ws, v_gm_bs, v_pool_w, v_pool_scale, v_mem_norm_g, v_w_mem_kv, v_w_branch, v_w_out, v_final_norm_g]
    deltas, new_m, new_v = zip(*[_adamw(w, g, m, v) for w, g, m, v in zip(ws, grads, ms, vs)])
    return (loss, grad_x, *grads, *deltas, *new_m, *new_v)
```

```python
import functools
import math

import jax
import jax.numpy as jnp
from jax import lax
from jax.experimental import pallas as pl
from jax.experimental.pallas import tpu as pltpu

F32 = jnp.float32
BF16 = jnp.bfloat16

S = 4096
D = 1024
W = 512
DIN = 10752
NL = 4
NCHIP = 4
NDEV = 8
CW = DIN // NCHIP
TN_IN = 896
NJ = CW // TN_IN
HD = 128
EPS = 1e-6
NEG = -1e30
SCALE = HD ** -0.5
INV_SQRT2 = 1.0 / math.sqrt(2.0)
INV_SQRT2PI = 1.0 / math.sqrt(2.0 * math.pi)
POOL_WINDOWS = (2, 4, 8, 16)
DILATIONS = (1, 4, 16)
HALO = 16
NPIECE = DIN // W
P_AGATE, P_PIN, P_PGATE, P_CQ, P_CK, P_CV, P_CGATE, P_MQ, P_MGATE, P_GM = 2, 3, 4, 5, 8, 9, 10, 11, 12, 13
VMEM_LIMIT = 56 * 1024 * 1024

ADAM_LR, ADAM_B1, ADAM_B2, ADAM_EPS, ADAM_WD, ADAM_STEP = 0.001, 0.9, 0.999, 1e-08, 0.01, 10

MESH = pl.DeviceIdType.MESH
ANY = pl.BlockSpec(memory_space=pl.ANY)


def _cp(*sem):
    return pltpu.CompilerParams(dimension_semantics=sem or None, vmem_limit_bytes=VMEM_LIMIT)


def _sds(shape, dtype):
    return jax.ShapeDtypeStruct(shape, dtype)


def _sigmoid(v):
    return 1.0 / (1.0 + jnp.exp(-v))


def _dot(a, b):
    return jnp.dot(a, b, preferred_element_type=F32)


def _dot_nt(a, b):
    return lax.dot_general(a, b, (((1,), (1,)), ((), ())), preferred_element_type=F32)


def _dot_tn(a, b):
    return lax.dot_general(a, b, (((0,), (0,)), ((), ())), preferred_element_type=F32)


def _tile_put(buf, sem, dst_of, step, nsteps, fill):
    slot = step % 2

    def copy(s, st):
        return pltpu.make_async_copy(buf.at[s], dst_of(st), sem.at[s])

    @pl.when(step >= 2)
    def _():
        copy(slot, step).wait()

    fill(buf.at[slot])
    copy(slot, step).start()

    @pl.when(step == nsteps - 1)
    def _():
        if nsteps >= 2:
            copy(1 - slot, step).wait()
        copy(slot, step).wait()


def _my_pos():
    return lax.axis_index("x"), lax.axis_index("y"), lax.axis_index("c")


_CHIP_REL = ((1, 0), (0, 1), (1, 1))
_DEV_REL = tuple((dx, dy, dc) for dx in (0, 1) for dy in (0, 1) for dc in (0, 1))[1:]


def _gather_weights(win_s, wkv_s, wbr_s, wo_s):
    shards = (win_s, wkv_s, wbr_s, wo_s)
    nk = len(shards)
    half = tuple(s.shape[1] // 2 for s in shards)
    out_shape = []
    for s in shards:
        out_shape += [_sds((NCHIP,) + s.shape[1:], BF16) for _ in range(NL)]
    n_pair = 3 * nk * NL

    def body(*refs):
        ins = refs[:nk]
        outs = refs[nk:nk + nk * NL]
        send1, recv1, send2, recv2, lsem = refs[nk + nk * NL:]
        x, y, c = _my_pos()
        me = 2 * x + y
        sibling = (x, y, 1 - c)
        chips = [(x ^ dx, y ^ dy) for dx, dy in _CHIP_REL]

        def out(k, l):
            return outs[k * NL + l]

        def piece(k, l, chip, hf):
            return out(k, l).at[chip, pl.ds(hf * half[k], half[k])]

        def mine(k, l, hf):
            return ins[k].at[l, pl.ds(hf * half[k], half[k])]

        local = []
        for k in range(nk):
            for l in range(NL):
                cp = pltpu.make_async_copy(ins[k].at[l], out(k, l).at[me], lsem.at[k * NL + l])
                cp.start()
                local.append(cp)

        def first(r, k, l):
            cx, cy = chips[r]
            i = (r * nk + k) * NL + l
            return pltpu.make_async_remote_copy(
                src_ref=mine(k, l, c), dst_ref=piece(k, l, me, c), send_sem=send1.at[i], recv_sem=recv1.at[i],
                device_id=(cx, cy, c), device_id_type=MESH)

        def landed(r, k, l):
            cx, cy = chips[r]
            i = (r * nk + k) * NL + l
            return pltpu.make_async_remote_copy(
                src_ref=mine(k, l, c), dst_ref=piece(k, l, 2 * cx + cy, c), send_sem=send1.at[i], recv_sem=recv1.at[i],
                device_id=(cx, cy, c), device_id_type=MESH)

        def passed(r, k, l, hf):
            cx, cy = chips[r]
            i = (r * nk + k) * NL + l
            ref = piece(k, l, 2 * cx + cy, hf)
            return pltpu.make_async_remote_copy(
                src_ref=ref, dst_ref=ref, send_sem=send2.at[i], recv_sem=recv2.at[i],
                device_id=sibling, device_id_type=MESH)

        idx = [(r, k, l) for r in range(3) for k in range(nk) for l in range(NL)]
        for r, k, l in idx:
            first(r, k, l).start()
        for r, k, l in idx:
            landed(r, k, l).wait_recv()
            passed(r, k, l, c).start()
        for r, k, l in idx:
            passed(r, k, l, 1 - c).wait_recv()
        for r, k, l in idx:
            first(r, k, l).wait_send()
            passed(r, k, l, c).wait_send()
        for cp in local:
            cp.wait()

    res = pl.pallas_call(
        body, out_shape=out_shape, in_specs=[ANY] * nk, out_specs=[ANY] * (nk * NL),
        scratch_shapes=[pltpu.SemaphoreType.DMA((n_pair,))] * 4 + [pltpu.SemaphoreType.DMA((nk * NL,))],
        name="gather_weights")(*shards)
    return [res[k * NL:(k + 1) * NL] for k in range(nk)]


def _exchange_grads(grads):
    nk = len(grads)
    flat = [g for gk in grads for g in gk]
    out_shape = [_sds((NL, NDEV) + gk[0].shape[2:], BF16) for gk in grads]
    n_pair = 7 * nk * NL

    def body(*refs):
        ins = refs[:nk * NL]
        outs = refs[nk * NL:nk * NL + nk]
        send, recv, lsem = refs[nk * NL + nk:]
        x, y, c = _my_pos()
        me_chip = 2 * x + y
        me_dev = 4 * x + 2 * y + c
        local = []
        for k in range(nk):
            for l in range(NL):
                cp = pltpu.make_async_copy(ins[k * NL + l].at[me_chip, c], outs[k].at[l, me_dev], lsem.at[k * NL + l])
                cp.start()
                local.append(cp)

        def copy(r, k, l):
            dx, dy, dc = _DEV_REL[r]
            px, py, pc = x ^ dx, y ^ dy, c ^ dc
            i = (r * nk + k) * NL + l
            return pltpu.make_async_remote_copy(
                src_ref=ins[k * NL + l].at[2 * px + py, pc], dst_ref=outs[k].at[l, me_dev],
                send_sem=send.at[i], recv_sem=recv.at[i], device_id=(px, py, pc), device_id_type=MESH)

        def arrival(r, k, l):
            dx, dy, dc = _DEV_REL[r]
            px, py, pc = x ^ dx, y ^ dy, c ^ dc
            i = (r * nk + k) * NL + l
            return pltpu.make_async_remote_copy(
                src_ref=ins[k * NL + l].at[me_chip, c], dst_ref=outs[k].at[l, 4 * px + 2 * py + pc],
                send_sem=send.at[i], recv_sem=recv.at[i], device_id=(px, py, pc), device_id_type=MESH)

        idx = [(r, k, l) for r in range(7) for k in range(nk) for l in range(NL)]
        for r, k, l in idx:
            copy(r, k, l).start()
        for r, k, l in idx:
            arrival(r, k, l).wait_recv()
        for r, k, l in idx:
            copy(r, k, l).wait_send()
        for cp in local:
            cp.wait()

    return pl.pallas_call(
        body, out_shape=out_shape, in_specs=[ANY] * (nk * NL), out_specs=[ANY] * nk,
        scratch_shapes=[pltpu.SemaphoreType.DMA((n_pair,))] * 2 + [pltpu.SemaphoreType.DMA((nk * NL,))],
        name="exchange_grads")(*flat)


def _exchange_small(packed):
    def body(in_ref, out_ref, send, recv, lsem):
        x, y, c = _my_pos()
        me_dev = 4 * x + 2 * y + c
        loc = pltpu.make_async_copy(in_ref, out_ref.at[me_dev], lsem)
        loc.start()

        def copy(r, slot):
            dx, dy, dc = _DEV_REL[r]
            px, py, pc = x ^ dx, y ^ dy, c ^ dc
            return pltpu.make_async_remote_copy(
                src_ref=in_ref, dst_ref=out_ref.at[slot(px, py, pc)], send_sem=send.at[r], recv_sem=recv.at[r],
                device_id=(px, py, pc), device_id_type=MESH)

        for r in range(7):
            copy(r, lambda px, py, pc: me_dev).start()
        for r in range(7):
            copy(r, lambda px, py, pc: 4 * px + 2 * py + pc).wait_recv()
        for r in range(7):
            copy(r, lambda px, py, pc: me_dev).wait_send()
        loc.wait()

    return pl.pallas_call(
        body, out_shape=_sds((NDEV,) + packed.shape, F32), in_specs=[ANY], out_specs=ANY,
        scratch_shapes=[pltpu.SemaphoreType.DMA((7,)), pltpu.SemaphoreType.DMA((7,)), pltpu.SemaphoreType.DMA(())],
        name="exchange_small")(packed)


def _share_halves(halves):
    nk = len(halves)

    def body(*refs):
        ins = refs[:nk]
        outs = refs[nk:2 * nk]
        send, recv, lsem = refs[2 * nk:]
        x, y, c = _my_pos()
        local = []
        for k in range(nk):
            cp = pltpu.make_async_copy(ins[k], outs[k].at[:, c], lsem.at[k])
            cp.start()
            local.append(cp)

        def copy(k, hf):
            return pltpu.make_async_remote_copy(
                src_ref=ins[k], dst_ref=outs[k].at[:, hf], send_sem=send.at[k], recv_sem=recv.at[k],
                device_id=(x, y, 1 - c), device_id_type=MESH)

        for k in range(nk):
            copy(k, c).start()
        for k in range(nk):
            copy(k, 1 - c).wait_recv()
        for k in range(nk):
            copy(k, c).wait_send()
        for cp in local:
            cp.wait()

    out_shape = [_sds((NL, 2) + h.shape[1:], F32) for h in halves]
    return pl.pallas_call(
        body, out_shape=out_shape, in_specs=[ANY] * nk, out_specs=[ANY] * nk,
        scratch_shapes=[pltpu.SemaphoreType.DMA((nk,))] * 3, name="share_halves")(*halves)


def _sum_slots(recv, tr):
    nl, _, r, ccols = recv.shape

    def body(r_ref, o_ref):
        acc = r_ref[0].astype(F32)
        for s in range(1, NDEV):
            acc = acc + r_ref[s].astype(F32)
        o_ref[...] = acc

    return pl.pallas_call(
        body, grid=(nl, r // tr),
        in_specs=[pl.BlockSpec((None, NDEV, tr, ccols), lambda l, i: (l, 0, i, 0))],
        out_specs=pl.BlockSpec((None, tr, ccols), lambda l, i: (l, i, 0)),
        out_shape=_sds((nl, r, ccols), F32), compiler_params=_cp("parallel", "parallel"), name="sum_slots")(recv)


HBM = pl.BlockSpec(memory_space=pltpu.HBM)
SEM = pl.BlockSpec(memory_space=pltpu.SEMAPHORE)
EFFECT = pltpu.SideEffectType.DATAFLOW_SIDE_EFFECTING
N_GATHER = 3 * 4
N_EXCH = 7 * 4


def _in_hbm(t):
    return pltpu.with_memory_space_constraint(t, pltpu.HBM)


def _gather_start(shards):
    nk = len(shards)
    lands = [pltpu.HBM((NCHIP,) + s.shape[1:], BF16) for s in shards for _ in range(NL)]

    def body(*refs):
        ins, outs = refs[:nk], refs[nk:nk + nk * NL]
        sems = refs[nk + nk * NL:nk + nk * NL + 2 * NL]
        token = refs[-1]
        x, y, c = _my_pos()
        me = 2 * x + y
        for l in range(NL):
            for r, (dx, dy) in enumerate(_CHIP_REL):
                for k in range(nk):
                    pltpu.make_async_remote_copy(
                        src_ref=ins[k].at[l], dst_ref=outs[k * NL + l].at[me], send_sem=sems[2 * l].at[r * nk + k],
                        recv_sem=sems[2 * l + 1].at[r * nk + k], device_id=(x ^ dx, y ^ dy, c), device_id_type=MESH).start()
        token[...] = jnp.zeros_like(token)

    res = pl.pallas_call(
        body, name="gather_start",
        out_shape=lands + [pltpu.SemaphoreType.DMA((N_GATHER,))] * (2 * NL) + [_sds((8, 128), F32)],
        in_specs=[HBM] * nk, out_specs=[HBM] * (nk * NL) + [SEM] * (2 * NL) + [pl.BlockSpec(memory_space=pltpu.VMEM)],
        compiler_params=pltpu.CompilerParams(has_side_effects=EFFECT))(*[_in_hbm(s) for s in shards])
    lands = [[res[k * NL + l] for k in range(nk)] for l in range(NL)]
    sems = [(res[nk * NL + 2 * l], res[nk * NL + 2 * l + 1]) for l in range(NL)]
    return lands, sems, res[-1]


def _gather_wait(l, shards, lands, sems, after):
    nk = len(shards)

    def body(*refs):
        ins, land = refs[:nk], refs[nk:2 * nk]
        send, recv = refs[2 * nk], refs[2 * nk + 1]
        x, y, c = _my_pos()
        for r, (dx, dy) in enumerate(_CHIP_REL):
            cx, cy = x ^ dx, y ^ dy
            for k in range(nk):
                cp = pltpu.make_async_remote_copy(
                    src_ref=ins[k].at[l], dst_ref=land[k].at[2 * cx + cy], send_sem=send.at[r * nk + k],
                    recv_sem=recv.at[r * nk + k], device_id=(cx, cy, c), device_id_type=MESH)
                cp.wait_send()
                cp.wait_recv()

    return pl.pallas_call(
        body, name=f"gather_wait_{l}", out_shape=[pltpu.HBM(t.shape, t.dtype) for t in lands],
        in_specs=[ANY] * nk + [HBM] * nk + [SEM, SEM, ANY], out_specs=[HBM] * nk,
        input_output_aliases={nk + k: k for k in range(nk)},
        compiler_params=pltpu.CompilerParams(has_side_effects=EFFECT))(*shards, *lands, *sems, after)


def _place_own(land, shard, l, tr):
    _, rows, cols = shard.shape[0], shard.shape[-2], shard.shape[-1]
    lead = shard.shape[1:-2]
    nlead = math.prod(lead)
    sh = shard.reshape((NL, nlead, rows, cols))
    ld = land.reshape((NCHIP, nlead, rows, cols))
    me = (2 * lax.axis_index("x") + lax.axis_index("y")).astype(jnp.int32).reshape(1)

    def body(me_ref, s_ref, l_in, o_ref):
        o_ref[...] = s_ref[...]

    out = pl.pallas_call(
        body,
        grid_spec=pltpu.PrefetchScalarGridSpec(
            num_scalar_prefetch=1, grid=(nlead, rows // tr),
            in_specs=[pl.BlockSpec((None, None, tr, cols), lambda b, i, me_ref: (l, b, i, 0)), ANY],
            out_specs=pl.BlockSpec((None, None, tr, cols), lambda b, i, me_ref: (me_ref[0], b, i, 0))),
        out_shape=_sds(ld.shape, BF16), input_output_aliases={2: 0},
        compiler_params=_cp("parallel", "parallel"), name="place_own")(me, sh, ld)
    return out.reshape(land.shape)


def _exch_start(l, parts):
    nk = len(parts)

    def body(*refs):
        ins, outs = refs[:nk], refs[nk:2 * nk]
        send, recv, token = refs[2 * nk:]
        x, y, c = _my_pos()
        for r, (dx, dy, dc) in enumerate(_DEV_REL):
            px, py, pc = x ^ dx, y ^ dy, c ^ dc
            for k in range(nk):
                pltpu.make_async_remote_copy(
                    src_ref=ins[k].at[2 * px + py, pc], dst_ref=outs[k].at[r], send_sem=send.at[r * nk + k],
                    recv_sem=recv.at[r * nk + k], device_id=(px, py, pc), device_id_type=MESH).start()
        token[...] = jnp.zeros_like(token)

    res = pl.pallas_call(
        body, name=f"exch_start_{l}",
        out_shape=[pltpu.HBM((7,) + p.shape[2:], BF16) for p in parts] + [pltpu.SemaphoreType.DMA((N_EXCH,))] * 2 + [_sds((8, 128), F32)],
        in_specs=[HBM] * nk, out_specs=[HBM] * nk + [SEM, SEM, pl.BlockSpec(memory_space=pltpu.VMEM)],
        compiler_params=pltpu.CompilerParams(has_side_effects=EFFECT))(*[_in_hbm(p) for p in parts])
    return res[:nk], (res[nk], res[nk + 1]), res[-1]


def _exch_wait(l, parts, lands, sems, after):
    nk = len(parts)

    def body(*refs):
        ins, land = refs[:nk], refs[nk:2 * nk]
        send, recv = refs[2 * nk], refs[2 * nk + 1]
        x, y, c = _my_pos()
        for r, (dx, dy, dc) in enumerate(_DEV_REL):
            px, py, pc = x ^ dx, y ^ dy, c ^ dc
            for k in range(nk):
                cp = pltpu.make_async_remote_copy(
                    src_ref=ins[k].at[2 * px + py, pc], dst_ref=land[k].at[r], send_sem=send.at[r * nk + k],
                    recv_sem=recv.at[r * nk + k], device_id=(px, py, pc), device_id_type=MESH)
                cp.wait_send()
                cp.wait_recv()

    return pl.pallas_call(
        body, name=f"exch_wait_{l}", out_shape=[pltpu.HBM(t.shape, t.dtype) for t in lands],
        in_specs=[ANY] * nk + [HBM] * nk + [SEM, SEM, ANY], out_specs=[HBM] * nk,
        input_output_aliases={nk + k: k for k in range(nk)},
        compiler_params=pltpu.CompilerParams(has_side_effects=EFFECT))(*parts, *lands, *sems, after)


def _chip_half():
    x, y, c = _my_pos()
    return jnp.stack([2 * x + y, c]).astype(jnp.int32)


def _sum_into(full, part, land, l, tr):
    _, _, r2, cols = full.shape

    def body(pos_ref, p_ref, r_ref, f_in, o_ref):
        acc = p_ref[...].astype(F32)
        for r in range(7):
            acc = acc + r_ref[r].astype(F32)
        o_ref[...] = acc

    return pl.pallas_call(
        body,
        grid_spec=pltpu.PrefetchScalarGridSpec(
            num_scalar_prefetch=1, grid=(r2 // tr,),
            in_specs=[pl.BlockSpec((None, None, tr, cols), lambda i, pos: (pos[0], pos[1], i, 0)),
                      pl.BlockSpec((7, tr, cols), lambda i, pos: (0, i, 0)), ANY],
            out_specs=pl.BlockSpec((None, None, tr, cols), lambda i, pos: (l, pos[1], i, 0))),
        out_shape=_sds(full.shape, F32), input_output_aliases={3: 0},
        compiler_params=_cp("parallel"), name="sum_into")(_chip_half(), part, land, full)


def _share_full(fulls):
    nk = len(fulls)

    def body(*refs):
        ins, outs = refs[:nk], refs[nk:2 * nk]
        send, recv = refs[2 * nk:]
        x, y, c = _my_pos()

        def copy(k, hf):
            return pltpu.make_async_remote_copy(
                src_ref=ins[k].at[:, hf], dst_ref=outs[k].at[:, hf], send_sem=send.at[k], recv_sem=recv.at[k],
                device_id=(x, y, 1 - c), device_id_type=MESH)

        for k in range(nk):
            copy(k, c).start()
        for k in range(nk):
            copy(k, 1 - c).wait_recv()
        for k in range(nk):
            copy(k, c).wait_send()

    return pl.pallas_call(
        body, out_shape=[_sds(f.shape, F32) for f in fulls], in_specs=[ANY] * nk, out_specs=[ANY] * nk,
        scratch_shapes=[pltpu.SemaphoreType.DMA((nk,))] * 2, input_output_aliases={k: k for k in range(nk)},
        name="share_full")(*fulls)


def _allreduce_small(packed, tr):
    rows = packed.shape[0]

    def xbody(in_ref, out_ref, send, recv):
        x, y, c = _my_pos()

        def copy(r):
            dx, dy, dc = _DEV_REL[r]
            return pltpu.make_async_remote_copy(
                src_ref=in_ref, dst_ref=out_ref.at[r], send_sem=send.at[r], recv_sem=recv.at[r],
                device_id=(x ^ dx, y ^ dy, c ^ dc), device_id_type=MESH)

        for r in range(7):
            copy(r).start()
        for r in range(7):
            copy(r).wait_recv()
        for r in range(7):
            copy(r).wait_send()

    land = pl.pallas_call(
        xbody, out_shape=_sds((7, rows, 128), F32), in_specs=[ANY], out_specs=ANY,
        scratch_shapes=[pltpu.SemaphoreType.DMA((7,))] * 2, name="exchange_small")(packed)
    x, y, c = _my_pos()
    me = (4 * x + 2 * y + c).astype(jnp.int32).reshape(1)

    def sbody(me_ref, p_ref, r_ref, o_ref):
        me_dev = me_ref[0]
        own = p_ref[...]
        acc = None
        for s in range(NDEV):
            rel = s ^ me_dev
            v = jnp.where(rel == 0, own, r_ref[jnp.maximum(rel - 1, 0)])
            acc = v if acc is None else acc + v
        o_ref[...] = acc

    return pl.pallas_call(
        sbody,
        grid_spec=pltpu.PrefetchScalarGridSpec(
            num_scalar_prefetch=1, grid=(rows // tr,),
            in_specs=[pl.BlockSpec((tr, 128), lambda i, me_ref: (i, 0)), pl.BlockSpec((7, tr, 128), lambda i, me_ref: (0, i, 0))],
            out_specs=pl.BlockSpec((tr, 128), lambda i, me_ref: (i, 0))),
        out_shape=_sds((rows, 128), F32), compiler_params=_cp("parallel"), name="sum_small")(me, packed, land)


def _rms_fwd(x, g, tm):
    n = x.shape[0]

    def body(x_ref, g_ref, h_ref):
        xv = x_ref[...]
        r = lax.rsqrt(jnp.mean(xv * xv, axis=-1, keepdims=True) + EPS)
        h_ref[...] = (xv * r * g_ref[...]).astype(BF16)

    return pl.pallas_call(
        body, grid=(n // tm,),
        in_specs=[pl.BlockSpec((tm, D), lambda i: (i, 0)), pl.BlockSpec((1, D), lambda i: (0, 0))],
        out_specs=pl.BlockSpec((tm, D), lambda i: (i, 0)), out_shape=_sds((n, D), BF16),
        compiler_params=_cp("parallel"), name="rms_fwd")(x, g)


def _rms_bwd(dh, x, g, dres, tm=512):
    n = x.shape[0]

    def body(dh_ref, x_ref, g_ref, dr_ref, dx_ref, dg_ref):
        i = pl.program_id(0)
        xv = x_ref[...]
        r = lax.rsqrt(jnp.mean(xv * xv, axis=-1, keepdims=True) + EPS)
        xh = xv * r
        dhv = dh_ref[...]
        dxh = dhv * g_ref[...]
        dx_ref[...] = dr_ref[...] + r * (dxh - xh * jnp.mean(dxh * xh, axis=-1, keepdims=True))
        part = jnp.sum(dhv * xh, axis=0, keepdims=True)

        @pl.when(i == 0)
        def _():
            dg_ref[...] = part

        @pl.when(i > 0)
        def _():
            dg_ref[...] += part

    row = pl.BlockSpec((tm, D), lambda i: (i, 0))
    vec = pl.BlockSpec((1, D), lambda i: (0, 0))
    return pl.pallas_call(
        body, grid=(n // tm,), in_specs=[row, row, vec, row], out_specs=[row, vec],
        out_shape=[_sds((n, D), F32), _sds((1, D), F32)], compiler_params=_cp("arbitrary"), name="rms_bwd")(dh, x, g, dres)


def _loss_head(x, tgt, g, tm=512):
    def body(x_ref, t_ref, g_ref, dx_ref, dg_ref, ls_ref):
        i = pl.program_id(0)
        xv = x_ref[...]
        r = lax.rsqrt(jnp.mean(xv * xv, axis=-1, keepdims=True) + EPS)
        xh = xv * r
        gv = g_ref[...]
        diff = xh * gv - t_ref[...]
        dy = diff * (1.0 / D)
        dxh = dy * gv
        dx_ref[...] = r * (dxh - xh * jnp.mean(dxh * xh, axis=-1, keepdims=True))
        part_g = jnp.sum(dy * xh, axis=0, keepdims=True)
        part_l = jnp.sum(diff * diff, axis=0, keepdims=True)

        @pl.when(i == 0)
        def _():
            dg_ref[...] = part_g
            ls_ref[...] = part_l

        @pl.when(i > 0)
        def _():
            dg_ref[...] += part_g
            ls_ref[...] += part_l

        @pl.when(i == pl.num_programs(0) - 1)
        def _():
            tot = jnp.sum(ls_ref[...], axis=-1, keepdims=True) * (0.5 / D)
            ls_ref[...] = jnp.broadcast_to(tot, (1, D))

    row = pl.BlockSpec((tm, D), lambda i: (i, 0))
    vec = pl.BlockSpec((1, D), lambda i: (0, 0))
    return pl.pallas_call(
        body, grid=(S // tm,), in_specs=[row, row, vec], out_specs=[row, vec, vec],
        out_shape=[_sds((S, D), F32), _sds((1, D), F32), _sds((1, D), F32)],
        compiler_params=_cp("arbitrary"), name="loss_head")(x, tgt, g)


def _adamw(w, g, m, v):
    shape = w.shape
    cols = shape[-1] if w.ndim > 1 else shape[0]
    rows = w.size // cols
    w2, g2, m2, v2 = (t.reshape(rows, cols) for t in (w, g, m, v))
    tr = rows
    while tr * cols * 4 > (1 << 20) and tr % 16 == 0:
        tr //= 2
    c1 = 1.0 - ADAM_B1 ** ADAM_STEP
    c2 = 1.0 - ADAM_B2 ** ADAM_STEP

    def body(w_ref, g_ref, m_ref, v_ref, d_ref, nm_ref, nv_ref):
        gv = g_ref[...]
        mn = ADAM_B1 * m_ref[...] + (1.0 - ADAM_B1) * gv
        vn = ADAM_B2 * v_ref[...] + (1.0 - ADAM_B2) * (gv * gv)
        d_ref[...] = -ADAM_LR * ((mn / c1) / (jnp.sqrt(vn / c2) + ADAM_EPS) + ADAM_WD * w_ref[...])
        nm_ref[...] = mn
        nv_ref[...] = vn

    blk = pl.BlockSpec((tr, cols), lambda i: (i, 0))
    outs = pl.pallas_call(
        body, grid=(rows // tr,), in_specs=[blk] * 4, out_specs=[blk] * 3,
        out_shape=[_sds((rows, cols), F32)] * 3, compiler_params=_cp("parallel"), name="adamw")(w2, g2, m2, v2)
    return tuple(o.reshape(shape) for o in outs)


def _proj_fwd(h, wg, tm=1024):
    def body(h_ref, w_ref, o_ref):
        o_ref[...] = _dot(h_ref[...], w_ref[...])

    return pl.pallas_call(
        body, grid=(NCHIP, NJ, S // tm),
        in_specs=[pl.BlockSpec((tm, D), lambda c, j, i: (i, 0)), pl.BlockSpec((None, D, TN_IN), lambda c, j, i: (c, 0, j))],
        out_specs=pl.BlockSpec((tm, TN_IN), lambda c, j, i: (i, c * NJ + j)), out_shape=_sds((S, DIN), F32),
        compiler_params=_cp("parallel", "parallel", "parallel"), name="proj_fwd")(h, wg)


def _proj_bwd_x(dproj, wg, dep, tm=1024):
    nk = NCHIP * NJ

    def body(d_ref, w_ref, dep_ref, o_ref):
        k = pl.program_id(1)
        part = _dot_nt(d_ref[...], w_ref[...])

        @pl.when(k == 0)
        def _():
            o_ref[...] = part

        @pl.when(k > 0)
        def _():
            o_ref[...] += part

    return pl.pallas_call(
        body, grid=(S // tm, nk),
        in_specs=[pl.BlockSpec((tm, TN_IN), lambda i, k: (i, k)),
                  pl.BlockSpec((None, D, TN_IN), lambda i, k: (k // NJ, 0, k % NJ)), ANY],
        out_specs=pl.BlockSpec((tm, D), lambda i, k: (i, 0)), out_shape=_sds((S, D), F32),
        compiler_params=_cp("parallel", "arbitrary"), name="proj_bwd_x")(dproj, wg, dep)


def _proj_bwd_w(h, dproj, tk=1024):
    nk = S // tk

    def body(h_ref, d_ref, o_ref, acc):
        k = pl.program_id(2)
        part = _dot_tn(h_ref[...], d_ref[...])

        @pl.when(k == 0)
        def _():
            acc[...] = part

        @pl.when(k > 0)
        def _():
            acc[...] += part

        @pl.when(k == nk - 1)
        def _():
            o_ref[...] = acc[...].astype(BF16)

    return pl.pallas_call(
        body, grid=(NCHIP, NJ, nk),
        in_specs=[pl.BlockSpec((tk, D), lambda c, j, k: (k, 0)), pl.BlockSpec((tk, TN_IN), lambda c, j, k: (k, c * NJ + j))],
        out_specs=pl.BlockSpec((None, D, TN_IN), lambda c, j, k: (c, 0, j)), out_shape=_sds((NCHIP, D, CW), BF16),
        scratch_shapes=[pltpu.VMEM((D, TN_IN), F32)],
        compiler_params=_cp("parallel", "parallel", "arbitrary"), name="proj_bwd_w")(h, dproj)


def _merge_fwd(y_all, wbr, proj, tm=512):
    cb = D // NCHIP

    def body(y0, y1, y2, y3, w_ref, g0, g1, g2, g3, z_ref):
        acc = None
        for b, (y_ref, g_ref) in enumerate(zip((y0, y1, y2, y3), (g0, g1, g2, g3))):
            t = _dot(y_ref[...], w_ref[b]) * _sigmoid(g_ref[...])
            acc = t if acc is None else acc + t
        z_ref[...] = acc.astype(BF16)

    y_specs = [pl.BlockSpec((None, tm, W), functools.partial(lambda b, c, i: (b, i, 0), b)) for b in range(4)]
    g_specs = [pl.BlockSpec((tm, cb), functools.partial(lambda b, c, i: (i, (P_GM * W + b * D) // cb + c), b)) for b in range(4)]
    return pl.pallas_call(
        body, grid=(NCHIP, S // tm),
        in_specs=y_specs + [pl.BlockSpec((None, 4, W, cb), lambda c, i: (c, 0, 0, 0))] + g_specs,
        out_specs=pl.BlockSpec((tm, cb), lambda c, i: (i, c)), out_shape=_sds((S, D), BF16),
        compiler_params=_cp("parallel", "parallel"), name="merge_fwd")(y_all, y_all, y_all, y_all, wbr, proj, proj, proj, proj)


def _merge_bwd(dz, y_all, wbr, proj, dproj, tm=512):
    cb = D // NCHIP
    ni = S // tm

    def body(dz_ref, y_ref, w_ref, ga_ref, gb_ref, dp_in, dp_ref, dy_ref, dw_ref, acc, obuf, osem):
        b = pl.program_id(0)
        i = pl.program_id(1)
        yv = y_ref[...]
        dys = []

        def fill(slot):
            dy = None
            for c in range(NCHIP):
                wv = w_ref[c]
                t = _dot(yv, wv)
                g_ref = ga_ref if c < 2 else gb_ref
                g = _sigmoid(g_ref[:, (c % 2) * cb:(c % 2 + 1) * cb])
                dzc = dz_ref[:, c * cb:(c + 1) * cb]
                slot[:, c * cb:(c + 1) * cb] = (dzc * t * g * (1.0 - g)).astype(BF16)
                dt = (dzc * g).astype(BF16)
                part = _dot_nt(dt, wv)
                dy = part if dy is None else dy + part
                dwp = _dot_tn(yv, dt)

                @pl.when(i == 0)
                def _():
                    acc[c] = dwp

                @pl.when(i > 0)
                def _():
                    acc[c] += dwp
            dys.append(dy)

        _tile_put(obuf, osem, lambda st: dp_ref.at[pl.ds((st % ni) * tm, tm), pl.ds(P_GM * W + (st // ni) * D, D)],
                  b * ni + i, 4 * ni, fill)
        dy_ref[...] = dys[0]

        @pl.when(i == ni - 1)
        def _():
            dw_ref[...] = acc[...].astype(BF16)

    return pl.pallas_call(
        body, grid=(4, ni),
        in_specs=[pl.BlockSpec((tm, D), lambda b, i: (i, 0)), pl.BlockSpec((None, tm, W), lambda b, i: (b, i, 0)),
                  pl.BlockSpec((NCHIP, None, W, cb), lambda b, i: (0, b, 0, 0)),
                  pl.BlockSpec((tm, W), lambda b, i: (i, P_GM + 2 * b)), pl.BlockSpec((tm, W), lambda b, i: (i, P_GM + 2 * b + 1)), ANY],
        out_specs=[ANY, pl.BlockSpec((None, tm, W), lambda b, i: (b, i, 0)), pl.BlockSpec((NCHIP, None, W, cb), lambda b, i: (0, b, 0, 0))],
        out_shape=[_sds((S, DIN), BF16), _sds((4, S, W), F32), _sds((NCHIP, 4, W, cb), BF16)],
        scratch_shapes=[pltpu.VMEM((NCHIP, W, cb), F32), pltpu.VMEM((2, tm, D), BF16), pltpu.SemaphoreType.DMA((2,))],
        input_output_aliases={5: 0}, compiler_params=_cp("arbitrary", "arbitrary"), name="merge_bwd")(dz, y_all, wbr, proj, proj, dproj)


def _out_fwd(z, wo, x, tm=512):
    def body(z_ref, w_ref, x_ref, o_ref):
        o_ref[...] = x_ref[...] + _dot(z_ref[...], w_ref[...])

    row = pl.BlockSpec((tm, D), lambda i: (i, 0))
    return pl.pallas_call(
        body, grid=(S // tm,), in_specs=[row, pl.BlockSpec((D, D), lambda i: (0, 0)), row], out_specs=row,
        out_shape=_sds((S, D), F32), compiler_params=_cp("parallel"), name="out_fwd")(z, wo, x)


def _out_bwd(dx, z, wo, tm=512):
    ni = S // tm

    def body(dx_ref, z_ref, w_ref, dz_ref, dw_ref, acc):
        i = pl.program_id(0)
        dxb = dx_ref[...].astype(BF16)
        dz_ref[...] = _dot_nt(dxb, w_ref[...])
        part = _dot_tn(z_ref[...], dxb)

        @pl.when(i == 0)
        def _():
            acc[...] = part

        @pl.when(i > 0)
        def _():
            acc[...] += part

        @pl.when(i == ni - 1)
        def _():
            dw_ref[...] = acc[...].astype(BF16)

    row = pl.BlockSpec((tm, D), lambda i: (i, 0))
    full = pl.BlockSpec((D, D), lambda i: (0, 0))
    return pl.pallas_call(
        body, grid=(ni,), in_specs=[row, row, full], out_specs=[row, full],
        out_shape=[_sds((S, D), F32), _sds((D, D), BF16)], scratch_shapes=[pltpu.VMEM((D, D), F32)],
        compiler_params=_cp("arbitrary"), name="out_bwd")(dx, z, wo)


def _gelu_parts(a):
    cdf = 0.5 * (1.0 + lax.erf(a * INV_SQRT2))
    return a * cdf, cdf


def _ln_parts(v):
    mu = jnp.mean(v, axis=-1, keepdims=True)
    vc = v - mu
    rs = lax.rsqrt(jnp.mean(vc * vc, axis=-1, keepdims=True) + EPS)
    return vc * rs, rs


def _causal_mask():
    return lax.broadcasted_iota(jnp.int32, (HD, HD), 0) >= lax.broadcasted_iota(jnp.int32, (HD, HD), 1)


def _gmlp_fwd(proj, lg, lb, ws, bias, y_all, tm=512):
    def body(uv_ref, gt_ref, lg_ref, lb_ref, ws_ref, b_ref, y_in, y_ref):
        act, _ = _gelu_parts(uv_ref[...])
        u = act[:, :W]
        xh, _ = _ln_parts(act[:, W:])
        vn = (xh * lg_ref[...] + lb_ref[...]).astype(BF16)
        gt = gt_ref[...]
        us = u * (gt * _sigmoid(gt))
        mask = _causal_mask()
        for h in range(4):
            wm = jnp.where(mask, ws_ref[h], 0.0).astype(BF16)
            cs = slice(h * HD, (h + 1) * HD)
            for c in range(tm // HD):
                rs_ = slice(c * HD, (c + 1) * HD)
                mixed = _dot(wm, vn[rs_, cs]) + b_ref[h]
                y_ref[rs_, cs] = (us[rs_, cs] * mixed).astype(BF16)

    vec = pl.BlockSpec((1, W), lambda i: (0, 0))
    mats = pl.BlockSpec((4, HD, HD), lambda i: (0, 0, 0))
    return pl.pallas_call(
        body, grid=(S // tm,),
        in_specs=[pl.BlockSpec((tm, 2 * W), lambda i: (i, 0)), pl.BlockSpec((tm, W), lambda i: (i, P_AGATE)), vec, vec, mats, mats, ANY],
        out_specs=pl.BlockSpec((None, tm, W), lambda i: (0, i, 0)), out_shape=_sds((4, S, W), BF16),
        input_output_aliases={6: 0}, compiler_params=_cp("parallel"), name="gmlp_fwd")(proj, proj, lg, lb, ws, bias, y_all)


def _gmlp_bwd(proj, dy_all, lg, lb, ws, bias, dproj, tm=256):
    ni = S // tm

    def body(uv_ref, gt_ref, dy_ref, lg_ref, lb_ref, ws_ref, b_ref, dp_in, dp_ref, dws_ref, dbs_ref, dlg_ref, dlb_ref, mix_s, dvn_s):
        i = pl.program_id(0)
        a0 = uv_ref[...]
        act, cdf = _gelu_parts(a0)
        u = act[:, :W]
        xh, rs = _ln_parts(act[:, W:])
        lgv = lg_ref[...]
        vn = (xh * lgv + lb_ref[...]).astype(BF16)
        mask = _causal_mask()
        wms = [jnp.where(mask, ws_ref[h], 0.0).astype(BF16) for h in range(4)]
        for h in range(4):
            cs = slice(h * HD, (h + 1) * HD)
            for c in range(tm // HD):
                rs_ = slice(c * HD, (c + 1) * HD)
                mix_s[rs_, cs] = _dot(wms[h], vn[rs_, cs]) + b_ref[h]
        mixed = mix_s[...]
        gt = gt_ref[...]
        sg = _sigmoid(gt)
        sl = gt * sg
        dyv = dy_ref[...]
        dum = dyv * sl
        dgate = dyv * (u * mixed) * (sg * (1.0 + gt * (1.0 - sg)))
        du = dum * mixed
        dmix = dum * u
        dmb = dmix.astype(BF16)
        for h in range(4):
            cs = slice(h * HD, (h + 1) * HD)
            dw = None
            db = None
            for c in range(tm // HD):
                rs_ = slice(c * HD, (c + 1) * HD)
                dvn_s[rs_, cs] = _dot_tn(wms[h], dmb[rs_, cs])
                pw = _dot_nt(dmb[rs_, cs], vn[rs_, cs])
                dw = pw if dw is None else dw + pw
                db = dmix[rs_, cs] if db is None else db + dmix[rs_, cs]

            @pl.when(i == 0)
            def _():
                dws_ref[h] = dw
                dbs_ref[h] = db

            @pl.when(i > 0)
            def _():
                dws_ref[h] += dw
                dbs_ref[h] += db

            @pl.when(i == ni - 1)
            def _():
                dws_ref[h] = jnp.where(mask, dws_ref[h], 0.0)
                dbs_ref[h] = jnp.broadcast_to(jnp.sum(dbs_ref[h], axis=1, keepdims=True), (HD, HD))
        dvn = dvn_s[...]
        plg = jnp.sum(dvn * xh, axis=0, keepdims=True)
        plb = jnp.sum(dvn, axis=0, keepdims=True)

        @pl.when(i == 0)
        def _():
            dlg_ref[...] = plg
            dlb_ref[...] = plb

        @pl.when(i > 0)
        def _():
            dlg_ref[...] += plg
            dlb_ref[...] += plb

        dxh = dvn * lgv
        dv = rs * (dxh - jnp.mean(dxh, axis=-1, keepdims=True) - xh * jnp.mean(dxh * xh, axis=-1, keepdims=True))
        gp = cdf + a0 * (jnp.exp(-0.5 * a0 * a0) * INV_SQRT2PI)
        dp_ref[:, :W] = (du * gp[:, :W]).astype(BF16)
        dp_ref[:, W:2 * W] = (dv * gp[:, W:]).astype(BF16)
        dp_ref[:, 2 * W:] = dgate.astype(BF16)

    vec = pl.BlockSpec((1, W), lambda i: (0, 0))
    mats = pl.BlockSpec((4, HD, HD), lambda i: (0, 0, 0))
    return pl.pallas_call(
        body, grid=(ni,),
        in_specs=[pl.BlockSpec((tm, 2 * W), lambda i: (i, 0)), pl.BlockSpec((tm, W), lambda i: (i, P_AGATE)),
                  pl.BlockSpec((None, tm, W), lambda i: (0, i, 0)), vec, vec, mats, mats, ANY],
        out_specs=[pl.BlockSpec((tm, 3 * W), lambda i: (i, 0)), mats, mats, vec, vec],
        out_shape=[_sds((S, DIN), BF16), _sds((4, HD, HD), F32), _sds((4, HD, HD), F32), _sds((1, W), F32), _sds((1, W), F32)],
        scratch_shapes=[pltpu.VMEM((tm, W), F32), pltpu.VMEM((tm, W), F32)],
        input_output_aliases={7: 0}, compiler_params=_cp("arbitrary"), name="gmlp_bwd")(proj, proj, dy_all, lg, lb, ws, bias, dproj)


def _pool_diff(p, halo, row0, tm):
    xx = jnp.concatenate([halo, p], axis=0)
    t1 = (row0 + 1 + lax.broadcasted_iota(jnp.int32, (tm, 1), 0)).astype(F32)
    out = []
    for g, win in enumerate(POOL_WINDOWS):
        s = xx[:, g * HD:(g + 1) * HD]
        sh = 1
        while sh < win:
            s = s + pltpu.roll(s, sh, 0)
            sh *= 2
        out.append(s[HALO:] / jnp.minimum(t1, float(win)) - p[:, g * HD:(g + 1) * HD])
    return out


def _pool_fwd(proj, pw, sc, y_all, tm=512):
    rb = tm // HALO

    def body(p_ref, h_ref, gt_ref, pw_ref, sc_ref, y_in, y_ref):
        i = pl.program_id(0)
        halo = jnp.where(i > 0, h_ref[...], 0.0)
        ds = _pool_diff(p_ref[...], halo, i * tm, tm)
        gt = gt_ref[...]
        sl = gt * _sigmoid(gt)
        for g in range(4):
            cs = slice(g * HD, (g + 1) * HD)
            lin = _dot(ds[g].astype(BF16), pw_ref[g].astype(BF16))
            y_ref[:, cs] = (lin * sc_ref[:, cs] * sl[:, cs]).astype(BF16)

    return pl.pallas_call(
        body, grid=(S // tm,),
        in_specs=[pl.BlockSpec((tm, W), lambda i: (i, P_PIN)),
                  pl.BlockSpec((HALO, W), lambda i: (jnp.maximum(i * rb - 1, 0), P_PIN)),
                  pl.BlockSpec((tm, W), lambda i: (i, P_PGATE)),
                  pl.BlockSpec((4, HD, HD), lambda i: (0, 0, 0)), pl.BlockSpec((1, W), lambda i: (0, 0)), ANY],
        out_specs=pl.BlockSpec((None, tm, W), lambda i: (1, i, 0)), out_shape=_sds((4, S, W), BF16),
        input_output_aliases={5: 0}, compiler_params=_cp("parallel"), name="pool_fwd")(proj, proj, proj, pw, sc, y_all)


def _pool_bwd(proj, dy_all, pw, sc, dproj, tm=256):
    ni = S // tm
    rb = tm // HALO
    last_rb = S // HALO - 1
    rx = tm + HALO

    def body(p_ref, h_ref, gt_ref, gh_ref, dy_ref, dyh_ref, pw_ref, sc_ref, dp_in, dp_ref, dpw_ref, dsc_ref, obuf, osem):
        i = pl.program_id(0)
        halo = jnp.where(i > 0, h_ref[...], 0.0)
        ds = _pool_diff(p_ref[...], halo, i * tm, tm)
        nxt = i < ni - 1
        gx = jnp.concatenate([gt_ref[...], gh_ref[...]], axis=0)
        dyx = jnp.concatenate([dy_ref[...], jnp.where(nxt, dyh_ref[...], 0.0)], axis=0)
        sgx = _sigmoid(gx)
        slx = gx * sgx
        scv = sc_ref[...]
        dlinx = dyx * slx * scv
        t1 = (i * tm + 1 + lax.broadcasted_iota(jnp.int32, (rx, 1), 0)).astype(F32)
        gt, sg, sl, dyv = gx[:tm], sgx[:tm], slx[:tm], dyx[:tm]
        dsl = sg * (1.0 + gt * (1.0 - sg))

        def fill(slot):
            for g, win in enumerate(POOL_WINDOWS):
                cs = slice(g * HD, (g + 1) * HD)
                wv = pw_ref[g].astype(BF16)
                dlb = dlinx[:, cs].astype(BF16)
                ddx = _dot_nt(dlb, wv)
                f = ddx / jnp.minimum(t1, float(win))
                sh = 1
                while sh < win:
                    f = f + pltpu.roll(f, rx - sh, 0)
                    sh *= 2
                slot[:, cs] = (f[:tm] - ddx[:tm]).astype(BF16)
                db = ds[g].astype(BF16)
                lin = _dot(db, wv)
                slot[:, W + g * HD:W + (g + 1) * HD] = (dyv[:, cs] * lin * scv[:, cs] * dsl[:, cs]).astype(BF16)
                psc = jnp.sum(dyv[:, cs] * sl[:, cs] * lin, axis=0, keepdims=True)
                pwg = _dot_tn(db, dlb[:tm])

                @pl.when(i == 0)
                def _():
                    dpw_ref[g] = pwg
                    dsc_ref[:, cs] = psc

                @pl.when(i > 0)
                def _():
                    dpw_ref[g] += pwg
                    dsc_ref[:, cs] += psc

        _tile_put(obuf, osem, lambda st: dp_ref.at[pl.ds(st * tm, tm), pl.ds(P_PIN * W, 2 * W)], i, ni, fill)

    mats = pl.BlockSpec((4, HD, HD), lambda i: (0, 0, 0))
    vec = pl.BlockSpec((1, W), lambda i: (0, 0))
    return pl.pallas_call(
        body, grid=(ni,),
        in_specs=[pl.BlockSpec((tm, W), lambda i: (i, P_PIN)),
                  pl.BlockSpec((HALO, W), lambda i: (jnp.maximum(i * rb - 1, 0), P_PIN)),
                  pl.BlockSpec((tm, W), lambda i: (i, P_PGATE)),
                  pl.BlockSpec((HALO, W), lambda i: (jnp.minimum((i + 1) * rb, last_rb), P_PGATE)),
                  pl.BlockSpec((None, tm, W), lambda i: (1, i, 0)),
                  pl.BlockSpec((None, HALO, W), lambda i: (1, jnp.minimum((i + 1) * rb, last_rb), 0)),
                  mats, vec, ANY],
        out_specs=[ANY, mats, vec],
        out_shape=[_sds((S, DIN), BF16), _sds((4, HD, HD), F32), _sds((1, W), F32)],
        scratch_shapes=[pltpu.VMEM((2, tm, 2 * W), BF16), pltpu.SemaphoreType.DMA((2,))],
        input_output_aliases={8: 0}, compiler_params=_cp("arbitrary"), name="pool_bwd")(proj, proj, proj, proj, dy_all, dy_all, pw, sc, dproj)


ATT_STEP = ((1, 4), (4, 1), (4, 1))


def _att_masks():
    qi = lax.broadcasted_iota(jnp.int32, (HD, HD), 0)
    kj = lax.broadcasted_iota(jnp.int32, (HD, HD), 1)
    return kj >= qi, kj <= qi


def _dilate(src, dst, d, rows, cast=None):
    for r in range(d):
        for h in range(4):
            v = src.at[h][pl.ds(r, rows // d, stride=d), :] if d > 1 else src[h]
            dst[r * 4 + h] = v if cast is None else v.astype(cast)


def _undilate(src, dst, d, rows):
    for r in range(d):
        for h in range(4):
            if d > 1:
                dst.at[h][pl.ds(r, rows // d, stride=d), :] = src[r * 4 + h].astype(F32)
            else:
                dst[h] = src[h].astype(F32)


def _dil_spec(d, tm):
    return pl.BlockSpec((4 * d, tm // d, HD), lambda i: (0, i, 0))


def _att_prep(proj, tm=512):
    def body(q0, q1, q2, k_ref, v_ref, *rest):
        outs, scr = rest[:9], rest[9]
        for j, (src, dsts) in enumerate(((q0, ((0, outs[0]),)), (q1, ((1, outs[1]),)), (q2, ((2, outs[2]),)),
                                         (k_ref, tuple((g, outs[3 + g]) for g in range(3))),
                                         (v_ref, tuple((g, outs[6 + g]) for g in range(3))))):
            for h in range(4):
                scr[j, h] = src[:, h * HD:(h + 1) * HD]
            for g, dst in dsts:
                _dilate(scr.at[j], dst, DILATIONS[g], tm, BF16)

    def piece(p):
        return pl.BlockSpec((tm, W), lambda i: (i, p))

    shapes = [_sds((4 * d, S // d, HD), BF16) for d in DILATIONS]
    res = pl.pallas_call(
        body, grid=(S // tm,),
        in_specs=[piece(P_CQ), piece(P_CQ + 1), piece(P_CQ + 2), piece(P_CK), piece(P_CV)],
        out_specs=[_dil_spec(d, tm) for d in DILATIONS] * 3, out_shape=shapes * 3,
        scratch_shapes=[pltpu.VMEM((5, 4, tm, HD), F32)],
        compiler_params=_cp("parallel"), name="att_prep")(proj, proj, proj, proj, proj)
    return res[0:3], res[3:6], res[6:9]


def _att_specs(g):
    d = DILATIONS[g]
    nres, njb = ATT_STEP[g]
    nb = S // d // HD
    own = pl.BlockSpec((4 * nres, njb * HD, HD), lambda r, j: (r, j, 0))
    prev = pl.BlockSpec((4 * nres, HD, HD), lambda r, j: (r, jnp.maximum(j * njb - 1, 0), 0))
    nxt = pl.BlockSpec((4 * nres, HD, HD), lambda r, j: (r, jnp.minimum((j + 1) * njb, nb - 1), 0))
    return (d // nres, nb // njb), own, prev, nxt


def _att_fwd(q, k, v, g):
    d = DILATIONS[g]
    nres, njb = ATT_STEP[g]
    grid, own, prev, _ = _att_specs(g)

    def body(q_ref, kp_ref, ko_ref, vp_ref, vo_ref, o_ref, l_ref):
        jb = pl.program_id(1)
        m_prev, m_own = _att_masks()
        no_prev = jnp.where(jb > 0, 0.0, NEG)
        for a in range(4 * nres):
            for jj in range(njb):
                rs_ = slice(jj * HD, (jj + 1) * HD)
                qb = q_ref[a, rs_, :]
                ko, vo = ko_ref[a, rs_, :], vo_ref[a, rs_, :]
                if jj == 0:
                    kp, vp, miss = kp_ref[a], vp_ref[a], no_prev
                else:
                    kp, vp, miss = ko_ref[a, (jj - 1) * HD:jj * HD, :], vo_ref[a, (jj - 1) * HD:jj * HD, :], 0.0
                sp = jnp.where(m_prev, _dot_nt(qb, kp) * SCALE, NEG) + miss
                so = jnp.where(m_own, _dot_nt(qb, ko) * SCALE, NEG)
                m = jnp.maximum(jnp.max(sp, axis=-1, keepdims=True), jnp.max(so, axis=-1, keepdims=True))
                ep = jnp.exp(sp - m)
                eo = jnp.exp(so - m)
                den = jnp.sum(ep, axis=-1, keepdims=True) + jnp.sum(eo, axis=-1, keepdims=True)
                inv = 1.0 / den
                o_ref[a, rs_, :] = _dot((ep * inv).astype(BF16), vp) + _dot((eo * inv).astype(BF16), vo)
                l_ref[a, rs_, :] = jnp.broadcast_to(m + jnp.log(den), (HD, HD))

    return pl.pallas_call(
        body, grid=grid, in_specs=[own, prev, own, prev, own], out_specs=[own, own],
        out_shape=[_sds((4 * d, S // d, HD), F32)] * 2,
        compiler_params=_cp("parallel", "parallel"), name="att_fwd")(q, k, k, v, v)


def _att_mix(os_, ls_, proj, y_all, tm=512):
    def body(o0, o1, o2, l0, l1, l2, gt_ref, y_in, y_ref, om_ref, lt_ref, so1, so2, sl1, sl2):
        _undilate(o1, so1, DILATIONS[1], tm)
        _undilate(o2, so2, DILATIONS[2], tm)
        _undilate(l1, sl1, DILATIONS[1], tm)
        _undilate(l2, sl2, DILATIONS[2], tm)
        for h in range(4):
            a, b, c = l0[h], sl1[h], sl2[h]
            m = jnp.maximum(jnp.maximum(a, b), c)
            ea, eb, ec = jnp.exp(a - m), jnp.exp(b - m), jnp.exp(c - m)
            z = ea + eb + ec
            inv = 1.0 / z
            o = (ea * inv) * o0[h] + (eb * inv) * so1[h] + (ec * inv) * so2[h]
            gt = gt_ref[:, h * HD:(h + 1) * HD]
            om_ref[h] = o
            lt_ref[h] = m + jnp.log(z)
            y_ref[:, h * HD:(h + 1) * HD] = (o * (gt * _sigmoid(gt))).astype(BF16)

    dil = [_dil_spec(d, tm) for d in DILATIONS]
    return pl.pallas_call(
        body, grid=(S // tm,),
        in_specs=dil * 2 + [pl.BlockSpec((tm, W), lambda i: (i, P_CGATE)), ANY],
        out_specs=[pl.BlockSpec((None, tm, W), lambda i: (2, i, 0)), dil[0], dil[0]],
        out_shape=[_sds((4, S, W), BF16), _sds((4, S, HD), F32), _sds((4, S, HD), F32)],
        scratch_shapes=[pltpu.VMEM((4, tm, HD), F32)] * 4,
        input_output_aliases={7: 0}, compiler_params=_cp("parallel"), name="att_mix")(*os_, *ls_, proj, y_all)


def _att_bwd_pre(dy_all, proj, om, lse, dproj, tm=512):
    def body(dy_ref, gt_ref, om_ref, ls_ref, dp_in, *rest):
        dos, dls, lss, dp_ref, sdo, sdl = rest[0:3], rest[3:6], rest[6:8], rest[8], rest[9], rest[10]
        for h in range(4):
            cs = slice(h * HD, (h + 1) * HD)
            gt = gt_ref[:, cs]
            sg = _sigmoid(gt)
            dyv = dy_ref[:, cs]
            o = om_ref[h]
            do = dyv * (gt * sg)
            dp_ref[:, cs] = (dyv * o * (sg * (1.0 + gt * (1.0 - sg)))).astype(BF16)
            sdo[h] = do
            sdl[h] = jnp.broadcast_to(jnp.sum(do * o, axis=-1, keepdims=True), (tm, HD))
        for g, d in enumerate(DILATIONS):
            _dilate(sdo, dos[g], d, tm, BF16)
            _dilate(sdl, dls[g], d, tm)
            if g > 0:
                _dilate(ls_ref, lss[g - 1], d, tm)

    dil = [_dil_spec(d, tm) for d in DILATIONS]
    gcol = pl.BlockSpec((tm, W), lambda i: (i, P_CGATE))
    res = pl.pallas_call(
        body, grid=(S // tm,),
        in_specs=[pl.BlockSpec((None, tm, W), lambda i: (2, i, 0)), gcol, dil[0], dil[0], ANY],
        out_specs=dil + dil + dil[1:] + [gcol],
        out_shape=([_sds((4 * d, S // d, HD), BF16) for d in DILATIONS] + [_sds((4 * d, S // d, HD), F32) for d in DILATIONS]
                   + [_sds((4 * d, S // d, HD), F32) for d in DILATIONS[1:]] + [_sds((S, DIN), BF16)]),
        scratch_shapes=[pltpu.VMEM((4, tm, HD), F32)] * 2,
        input_output_aliases={4: 8}, compiler_params=_cp("parallel"), name="att_bwd_pre")(dy_all, proj, om, lse, dproj)
    return res[0:3], res[3:6], [lse] + list(res[6:8]), res[8]


def _att_bwd(q, k, v, do, lse, delta, g):
    d = DILATIONS[g]
    nres, njb = ATT_STEP[g]
    grid, own, prev, nxt = _att_specs(g)

    def body(qa_ref, qn_ref, kp_ref, ko_ref, vp_ref, vo_ref, doa_ref, don_ref, la_ref, ln_ref, da_ref, dn_ref,
             dq_ref, dk_ref, dv_ref):
        jb = pl.program_id(1)
        m_prev, m_own = _att_masks()
        has_prev = jnp.where(jb > 0, 1.0, 0.0)
        has_next = jnp.where(jb < grid[1] - 1, 1.0, 0.0)

        def pair(q, k, v, dov_, lsev, dlt, mask, flag=None):
            p = jnp.where(mask, jnp.exp(_dot_nt(q, k) * SCALE - lsev), 0.0)
            if flag is not None:
                p = p * flag
            dsb = (p * (_dot_nt(dov_, v) - dlt) * SCALE).astype(BF16)
            return p.astype(BF16), dsb

        for a in range(4 * nres):
            for jj in range(njb):
                rs_ = slice(jj * HD, (jj + 1) * HD)
                qa, ko, vo = qa_ref[a, rs_, :], ko_ref[a, rs_, :], vo_ref[a, rs_, :]
                doa, lsa, dla = doa_ref[a, rs_, :], la_ref[a, rs_, :], da_ref[a, rs_, :]
                if jj == 0:
                    kp, vp, fp = kp_ref[a], vp_ref[a], has_prev
                else:
                    kp, vp, fp = ko_ref[a, (jj - 1) * HD:jj * HD, :], vo_ref[a, (jj - 1) * HD:jj * HD, :], None
                if jj == njb - 1:
                    qn, don, lsn, dln, fn = qn_ref[a], don_ref[a], ln_ref[a], dn_ref[a], has_next
                else:
                    ns = slice((jj + 1) * HD, (jj + 2) * HD)
                    qn, don, lsn, dln, fn = qa_ref[a, ns, :], doa_ref[a, ns, :], la_ref[a, ns, :], da_ref[a, ns, :], None
                p, dsb = pair(qa, ko, vo, doa, lsa, dla, m_own)
                dq = _dot(dsb, ko)
                dk = _dot_tn(dsb, qa)
                dv = _dot_tn(p, doa)
                p, dsb = pair(qa, kp, vp, doa, lsa, dla, m_prev, fp)
                dq = dq + _dot(dsb, kp)
                p, dsb = pair(qn, ko, vo, don, lsn, dln, m_prev, fn)
                dk = dk + _dot_tn(dsb, qn)
                dv = dv + _dot_tn(p, don)
                dq_ref[a, rs_, :] = dq
                dk_ref[a, rs_, :] = dk
                dv_ref[a, rs_, :] = dv

    return pl.pallas_call(
        body, grid=grid, in_specs=[own, nxt, prev, own, prev, own, own, nxt, own, nxt, own, nxt],
        out_specs=[own, own, own], out_shape=[_sds((4 * d, S // d, HD), F32)] * 3,
        compiler_params=_cp("parallel", "parallel"), name="att_bwd")(q, q, k, k, v, v, do, do, lse, lse, delta, delta)


def _att_bwd_post(dqs, dks, dvs, dproj, tm=512):
    def body(*refs):
        dq, dk, dv, dp_ref, scr = refs[0:3], refs[3:6], refs[6:9], refs[10], refs[11]
        for g in range(3):
            _undilate(dq[g], scr, DILATIONS[g], tm)
            for h in range(4):
                dp_ref[:, g * W + h * HD:g * W + (h + 1) * HD] = scr[h].astype(BF16)
        for j, parts in enumerate((dk, dv)):
            acc = None
            for g in range(3):
                _undilate(parts[g], scr, DILATIONS[g], tm)
                vals = [scr[h] for h in range(4)]
                acc = vals if acc is None else [x + y for x, y in zip(acc, vals)]
            for h in range(4):
                dp_ref[:, (3 + j) * W + h * HD:(3 + j) * W + (h + 1) * HD] = acc[h].astype(BF16)

    dil = [_dil_spec(d, tm) for d in DILATIONS]
    return pl.pallas_call(
        body, grid=(S // tm,), in_specs=dil * 3 + [ANY],
        out_specs=pl.BlockSpec((tm, 5 * W), lambda i: (i, 1)), out_shape=_sds((S, DIN), BF16),
        scratch_shapes=[pltpu.VMEM((4, tm, HD), F32)],
        input_output_aliases={9: 0}, compiler_params=_cp("parallel"), name="att_bwd_post")(*dqs, *dks, *dvs, dproj)


def _mem_kv_fwd(mem_n, wkv):
    m = mem_n.shape[0]

    def body(a_ref, w_ref, o_ref):
        o_ref[...] = _dot(a_ref[...], w_ref[...])

    return pl.pallas_call(body, out_shape=_sds((m, 2 * W), F32), compiler_params=_cp(), name="mem_kv_fwd")(mem_n, wkv)


def _mem_softmax(q, k):
    s = _dot_nt(q, k) * SCALE
    e = jnp.exp(s - jnp.max(s, axis=-1, keepdims=True))
    return e * (1.0 / jnp.sum(e, axis=-1, keepdims=True))


def _mem_fwd(proj, kv, y_all, tm=512):
    m = kv.shape[0]

    def body(q_ref, gt_ref, kv_ref, y_in, y_ref):
        gt = gt_ref[...]
        sl = gt * _sigmoid(gt)
        for h in range(4):
            cs = slice(h * HD, (h + 1) * HD)
            p = _mem_softmax(q_ref[:, cs].astype(BF16), kv_ref[:, cs].astype(BF16))
            o = _dot(p.astype(BF16), kv_ref[:, W + h * HD:W + (h + 1) * HD].astype(BF16))
            y_ref[:, cs] = (o * sl[:, cs]).astype(BF16)

    return pl.pallas_call(
        body, grid=(S // tm,),
        in_specs=[pl.BlockSpec((tm, W), lambda i: (i, P_MQ)), pl.BlockSpec((tm, W), lambda i: (i, P_MGATE)),
                  pl.BlockSpec((m, 2 * W), lambda i: (0, 0)), ANY],
        out_specs=pl.BlockSpec((None, tm, W), lambda i: (3, i, 0)), out_shape=_sds((4, S, W), BF16),
        input_output_aliases={3: 0}, compiler_params=_cp("parallel"), name="mem_fwd")(proj, proj, kv, y_all)


def _mem_bwd(proj, kv, dy_all, dproj, tm=512):
    m = kv.shape[0]
    ni = S // tm

    def body(q_ref, gt_ref, kv_ref, dy_ref, dp_in, dp_ref, dkv_ref, obuf, osem):
        i = pl.program_id(0)
        gt = gt_ref[...]
        sg = _sigmoid(gt)
        sl = gt * sg
        dsl = sg * (1.0 + gt * (1.0 - sg))
        dyv = dy_ref[...]

        def fill(slot):
            for h in range(4):
                cs = slice(h * HD, (h + 1) * HD)
                vs = slice(W + h * HD, W + (h + 1) * HD)
                q = q_ref[:, cs].astype(BF16)
                k = kv_ref[:, cs].astype(BF16)
                v = kv_ref[:, vs].astype(BF16)
                p = _mem_softmax(q, k)
                pb = p.astype(BF16)
                o = _dot(pb, v)
                do = dyv[:, cs] * sl[:, cs]
                dob = do.astype(BF16)
                dp = _dot_nt(dob, v)
                dsb = (p * (dp - jnp.sum(dp * p, axis=-1, keepdims=True)) * SCALE).astype(BF16)
                slot[:, cs] = _dot(dsb, k).astype(BF16)
                slot[:, vs] = (dyv[:, cs] * o * dsl[:, cs]).astype(BF16)
                dk = _dot_tn(dsb, q)
                dv = _dot_tn(pb, dob)

                @pl.when(i == 0)
                def _():
                    dkv_ref[:, cs] = dk
                    dkv_ref[:, vs] = dv

                @pl.when(i > 0)
                def _():
                    dkv_ref[:, cs] += dk
                    dkv_ref[:, vs] += dv

        _tile_put(obuf, osem, lambda st: dp_ref.at[pl.ds(st * tm, tm), pl.ds(P_MQ * W, 2 * W)], i, ni, fill)

    return pl.pallas_call(
        body, grid=(ni,),
        in_specs=[pl.BlockSpec((tm, W), lambda i: (i, P_MQ)), pl.BlockSpec((tm, W), lambda i: (i, P_MGATE)),
                  pl.BlockSpec((m, 2 * W), lambda i: (0, 0)), pl.BlockSpec((None, tm, W), lambda i: (3, i, 0)), ANY],
        out_specs=[ANY, pl.BlockSpec((m, 2 * W), lambda i: (0, 0))],
        out_shape=[_sds((S, DIN), BF16), _sds((m, 2 * W), F32)],
        scratch_shapes=[pltpu.VMEM((2, tm, 2 * W), BF16), pltpu.SemaphoreType.DMA((2,))],
        input_output_aliases={4: 0}, compiler_params=_cp("arbitrary"), name="mem_bwd")(proj, proj, kv, dy_all, dproj)


def _mem_kv_bwd(mem, g, mem_n, wkv, dkv):
    m = mem.shape[0]

    def body(x_ref, g_ref, a_ref, w_ref, d_ref, dw_ref, dg_ref):
        db = d_ref[...].astype(BF16)
        dw_ref[...] = _dot_tn(a_ref[...], db).astype(BF16)
        dn = _dot_nt(db, w_ref[...])
        xv = x_ref[...]
        xh = xv * lax.rsqrt(jnp.mean(xv * xv, axis=-1, keepdims=True) + EPS)
        dg_ref[...] = jnp.sum(dn * xh, axis=0, keepdims=True)

    return pl.pallas_call(
        body, out_shape=[_sds((D, 2 * W), BF16), _sds((1, D), F32)], compiler_params=_cp(), name="mem_kv_bwd")(mem, g, mem_n, wkv, dkv)


def _layer_fwd(x, mem, p, wg):
    win, wkv, wbr, wo = wg
    h = _rms_fwd(x, p["norm_g"], 512)
    proj = _proj_fwd(h, win)
    y_all = lax.empty((4, S, W), BF16)
    y_all = _gmlp_fwd(proj, p["gm_ln_g"], p["gm_ln_b"], p["gm_ws"], p["gm_bias"], y_all)
    y_all = _pool_fwd(proj, p["pool_w"], p["pool_scale"], y_all)
    qs, ks, vs = _att_prep(proj)
    os_, ls_ = zip(*[_att_fwd(qs[g], ks[g], vs[g], g) for g in range(3)])
    y_all, om, lse = _att_mix(os_, ls_, proj, y_all)
    mem_n = _rms_fwd(mem, p["mem_norm_g"], mem.shape[0])
    kv = _mem_kv_fwd(mem_n, wkv)
    y_all = _mem_fwd(proj, kv, y_all)
    z = _merge_fwd(y_all, wbr, proj)
    x_new = _out_fwd(z, wo, x)
    return x_new, dict(x=x, h=h, proj=proj, y_all=y_all, om=om, lse=lse, mem_n=mem_n, kv=kv, z=z, qkv=(qs, ks, vs))


GRAD_PARTS = ((2, D // 2, CW), (2, D // 8, 2 * W), (2, 2 * W, D // NCHIP), (2, D // 8, D))
SUM_TILE = (64, 128, 256, 128)
PLACE_TILE = (256, 256, 512, 256)


def _layer_bwd(dx, mem, p, wg, sv, exchange):
    win, wkv, wbr, wo = wg
    proj = sv["proj"]
    dz, d_wo = _out_bwd(dx, sv["z"], wo)
    dproj = lax.empty((S, DIN), BF16)
    dproj, dy_all, d_wbr = _merge_bwd(dz, sv["y_all"], wbr, proj, dproj)
    dproj, d_ws, d_bs, d_lg, d_lb = _gmlp_bwd(proj, dy_all, p["gm_ln_g"], p["gm_ln_b"], p["gm_ws"], p["gm_bias"], dproj)
    dproj, d_pw, d_sc = _pool_bwd(proj, dy_all, p["pool_w"], p["pool_scale"], dproj)
    dos, dls, lss, dproj = _att_bwd_pre(dy_all, proj, sv["om"], sv["lse"], dproj)
    qs, ks, vs = sv["qkv"]
    dqs, dks, dvs = zip(*[_att_bwd(qs[g], ks[g], vs[g], dos[g], lss[g], dls[g], g) for g in range(3)])
    dproj = _att_bwd_post(dqs, dks, dvs, dproj)
    dproj, dkv = _mem_bwd(proj, sv["kv"], dy_all, dproj)
    d_wkv, d_mg = _mem_kv_bwd(mem, p["mem_norm_g"], sv["mem_n"], wkv, dkv)
    d_win = _proj_bwd_w(sv["h"], dproj)
    big = tuple(t.reshape((NCHIP,) + s) for t, s in zip((d_win, d_wkv, d_wbr, d_wo), GRAD_PARTS))
    inflight = exchange(big)
    dh = _proj_bwd_x(dproj, win, inflight[-1])
    dx_in, d_ng = _rms_bwd(dh, sv["x"], p["norm_g"], dx)
    small = dict(norm_g=d_ng, gm_ln_g=d_lg, gm_ln_b=d_lb, gm_ws=d_ws, gm_bs=d_bs[:, :, 0], pool_w=d_pw, pool_scale=d_sc, mem_norm_g=d_mg)
    return dx_in, (big,) + inflight, small


_SMALL = ("norm_g", "gm_ln_g", "gm_ln_b", "gm_ws", "gm_bs", "pool_w", "pool_scale", "mem_norm_g")


def _layer_params(l, norm_g, gm_ln_g, gm_ln_b, gm_ws, gm_bs, pool_w, pool_scale, mem_norm_g):
    return dict(norm_g=norm_g[l][None], gm_ln_g=gm_ln_g[l][None], gm_ln_b=gm_ln_b[l][None], gm_ws=gm_ws[l],
                gm_bias=jnp.broadcast_to(gm_bs[l][:, :, None], (4, HD, HD)), pool_w=pool_w[l],
                pool_scale=pool_scale[l][None], mem_norm_g=mem_norm_g[l][None])


def kernel(x, mem, norm_g, w_in, gm_ln_g, gm_ln_b, gm_ws, gm_bs, pool_w, pool_scale, mem_norm_g, w_mem_kv, w_branch, w_out, final_norm_g, loss_target, m_norm_g, m_w_in, m_gm_ln_g, m_gm_ln_b, m_gm_ws, m_gm_bs, m_pool_w, m_pool_scale, m_mem_norm_g, m_w_mem_kv, m_w_branch, m_w_out, m_final_norm_g, v_norm_g, v_w_in, v_gm_ln_g, v_gm_ln_b, v_gm_ws, v_gm_bs, v_pool_w, v_pool_scale, v_mem_norm_g, v_w_mem_kv, v_w_branch, v_w_out, v_final_norm_g):
    xs, memv, tgt = x[0], mem[0], loss_target[0]
    params = [_layer_params(l, norm_g, gm_ln_g, gm_ln_b, gm_ws, gm_bs, pool_w, pool_scale, mem_norm_g) for l in range(NL)]

    shards = (w_in.astype(BF16), w_mem_kv.astype(BF16), w_branch.astype(BF16), w_out.astype(BF16))
    lands, gsems, after = _gather_start(shards)

    saved, wgs = [], []
    for l in range(NL):
        got = _gather_wait(l, shards, lands[l], gsems[l], after)
        got = [_place_own(got[k], shards[k], l, PLACE_TILE[k]) for k in range(4)]
        wgs.append((got[0], got[1].reshape(D, 2 * W), got[2], got[3].reshape(D, D)))
        xs, sv = _layer_fwd(xs, memv, params[l], wgs[l])
        saved.append(sv)
        after = xs
    dx, d_fg, ls = _loss_head(xs, tgt, final_norm_g[None])
    loss = lax.psum(ls[0, 0], ("x", "y", "c"))

    flight, small = [None] * NL, [None] * NL
    for l in reversed(range(NL)):
        dx, flight[l], small[l] = _layer_bwd(dx, memv, params[l], wgs[l], saved[l], functools.partial(_exch_start, l))
    grad_x = dx[None]

    leaves = [jnp.stack([small[l][n] for l in range(NL)]) for n in _SMALL] + [d_fg]
    sizes = [t.size for t in leaves]
    packed = jnp.concatenate([t.reshape(-1, 128) for t in leaves], axis=0)
    rows = packed.shape[0]
    tot = _allreduce_small(packed, max(t for t in range(8, 513, 8) if rows % t == 0))

    full = [lax.empty((NL,) + s, F32) for s in GRAD_PARTS]
    for l in reversed(range(NL)):
        parts, zones, sems, _ = flight[l]
        zones = _exch_wait(l, parts, zones, sems, tot)
        for k in range(4):
            full[k] = _sum_into(full[k], parts[k], zones[k], l, SUM_TILE[k])
    full = _share_full(full)
    g_w_in = full[0].reshape(NL, D, CW)
    g_w_kv = full[1].reshape(NL, D // NCHIP, 2 * W)
    g_w_br = full[2].reshape(NL, 4, W, D // NCHIP)
    g_w_out = full[3].reshape(NL, D // NCHIP, D)

    offs = [0]
    for sz in sizes:
        offs.append(offs[-1] + sz // 128)
    small_shapes = [norm_g.shape, gm_ln_g.shape, gm_ln_b.shape, gm_ws.shape, gm_bs.shape, pool_w.shape, pool_scale.shape,
                    mem_norm_g.shape, final_norm_g.shape]
    gs = [tot[offs[i]:offs[i + 1]].reshape(small_shapes[i]) for i in range(len(leaves))]
    g_norm_g, g_ln_g, g_ln_b, g_ws, g_bs, g_pw, g_sc, g_mg, g_fg = gs

    grads = [g_norm_g, g_w_in, g_ln_g, g_ln_b, g_ws, g_bs, g_pw, g_sc, g_mg, g_w_kv, g_w_br, g_w_out, g_fg]
    ws = [norm_g, w_in, gm_ln_g, gm_ln_b, gm_ws, gm_bs, pool_w, pool_scale, mem_norm_g, w_mem_kv, w_branch, w_out, final_norm_g]
    ms = [m_norm_g, m_w_in, m_gm_ln_g, m_gm_ln_b, m_gm_ws, m_gm_bs, m_pool_w, m_pool_scale, m_mem_norm_g, m_w_mem_kv, m_w_branch, m_w_out, m_final_norm_g]
    vs = [v_norm_g, v_w_in, v_gm_ln_g, v_gm_ln_b, v_gm_ws, v_gm_bs, v_pool_w, v_pool_scale, v_mem_norm_g, v_w_mem_kv, v_w_branch, v_w_out, v_final_norm_g]
    deltas, new_m, new_v = zip(*[_adamw(w, g, m, v) for w, g, m, v in zip(ws, grads, ms, vs)])
    return (loss, grad_x, *grads, *deltas, *new_m, *new_v)
```

```python
import functools
import math

import jax
import jax.numpy as jnp
from jax import lax
from jax.experimental import pallas as pl
from jax.experimental.pallas import tpu as pltpu

F32 = jnp.float32
BF16 = jnp.bfloat16

S = 4096
D = 1024
W = 512
DIN = 10752
NL = 4
NCHIP = 4
NDEV = 8
CW = DIN // NCHIP
TN_IN = 896
NJ = CW // TN_IN
HD = 128
EPS = 1e-6
NEG = -1e30
SCALE = HD ** -0.5
INV_SQRT2 = 1.0 / math.sqrt(2.0)
INV_SQRT2PI = 1.0 / math.sqrt(2.0 * math.pi)
POOL_WINDOWS = (2, 4, 8, 16)
DILATIONS = (1, 4, 16)
HALO = 16
NPIECE = DIN // W
P_AGATE, P_PIN, P_PGATE, P_CQ, P_CK, P_CV, P_CGATE, P_MQ, P_MGATE, P_GM = 2, 3, 4, 5, 8, 9, 10, 11, 12, 13
VMEM_LIMIT = 56 * 1024 * 1024

ADAM_LR, ADAM_B1, ADAM_B2, ADAM_EPS, ADAM_WD, ADAM_STEP = 0.001, 0.9, 0.999, 1e-08, 0.01, 10

MESH = pl.DeviceIdType.MESH
ANY = pl.BlockSpec(memory_space=pl.ANY)


def _cp(*sem):
    return pltpu.CompilerParams(dimension_semantics=sem or None, vmem_limit_bytes=VMEM_LIMIT)


def _sds(shape, dtype):
    return jax.ShapeDtypeStruct(shape, dtype)


def _sigmoid(v):
    return 1.0 / (1.0 + jnp.exp(-v))


def _dot(a, b):
    return jnp.dot(a, b, preferred_element_type=F32)


def _dot_nt(a, b):
    return lax.dot_general(a, b, (((1,), (1,)), ((), ())), preferred_element_type=F32)


def _dot_tn(a, b):
    return lax.dot_general(a, b, (((0,), (0,)), ((), ())), preferred_element_type=F32)


def _tile_put(buf, sem, dst_of, step, nsteps, fill):
    slot = step % 2

    def copy(s, st):
        return pltpu.make_async_copy(buf.at[s], dst_of(st), sem.at[s])

    @pl.when(step >= 2)
    def _():
        copy(slot, step).wait()

    fill(buf.at[slot])
    copy(slot, step).start()

    @pl.when(step == nsteps - 1)
    def _():
        if nsteps >= 2:
            copy(1 - slot, step).wait()
        copy(slot, step).wait()


def _my_pos():
    return lax.axis_index("x"), lax.axis_index("y"), lax.axis_index("c")


_CHIP_REL = ((1, 0), (0, 1), (1, 1))
_DEV_REL = tuple((dx, dy, dc) for dx in (0, 1) for dy in (0, 1) for dc in (0, 1))[1:]


HBM = pl.BlockSpec(memory_space=pltpu.HBM)
SEM = pl.BlockSpec(memory_space=pltpu.SEMAPHORE)
EFFECT = pltpu.SideEffectType.DATAFLOW_SIDE_EFFECTING
N_GATHER = 3 * 4
N_EXCH = 7 * 4


def _in_hbm(t):
    return pltpu.with_memory_space_constraint(t, pltpu.HBM)


def _gather_start(shards):
    nk = len(shards)
    lands = [pltpu.HBM((NCHIP,) + s.shape[1:], BF16) for s in shards for _ in range(NL)]

    def body(*refs):
        ins, outs = refs[:nk], refs[nk:nk + nk * NL]
        sems = refs[nk + nk * NL:nk + nk * NL + 2 * NL]
        token = refs[-1]
        x, y, c = _my_pos()
        me = 2 * x + y
        for l in range(NL):
            for r, (dx, dy) in enumerate(_CHIP_REL):
                for k in range(nk):
                    src, dst = ins[k].at[l], outs[k * NL + l].at[me]
                    if l == 0:
                        hf = pl.ds(c * (shards[k].shape[1] // 2), shards[k].shape[1] // 2)
                        src, dst = src.at[hf], dst.at[hf]
                    pltpu.make_async_remote_copy(
                        src_ref=src, dst_ref=dst, send_sem=sems[2 * l].at[r * nk + k],
                        recv_sem=sems[2 * l + 1].at[r * nk + k], device_id=(x ^ dx, y ^ dy, c), device_id_type=MESH).start()
        token[...] = jnp.zeros_like(token)

    res = pl.pallas_call(
        body, name="gather_start",
        out_shape=lands + [pltpu.SemaphoreType.DMA((N_GATHER,))] * (2 * NL) + [_sds((8, 128), F32)],
        in_specs=[HBM] * nk, out_specs=[HBM] * (nk * NL) + [SEM] * (2 * NL) + [pl.BlockSpec(memory_space=pltpu.VMEM)],
        compiler_params=pltpu.CompilerParams(has_side_effects=EFFECT))(*[_in_hbm(s) for s in shards])
    lands = [[res[k * NL + l] for k in range(nk)] for l in range(NL)]
    sems = [(res[nk * NL + 2 * l], res[nk * NL + 2 * l + 1]) for l in range(NL)]
    return lands, sems, res[-1]


def _gather_relay(shards, lands, sems, after):
    nk = len(shards)
    half = [s.shape[1] // 2 for s in shards]

    def body(*refs):
        ins, land = refs[:nk], refs[nk:2 * nk]
        send, recv = refs[2 * nk], refs[2 * nk + 1]
        send2, recv2 = refs[3 * nk + 3], refs[3 * nk + 4]
        x, y, c = _my_pos()
        for r, (dx, dy) in enumerate(_CHIP_REL):
            cx, cy = x ^ dx, y ^ dy
            for k in range(nk):
                hf = pl.ds(c * half[k], half[k])
                got = land[k].at[2 * cx + cy].at[hf]
                cp = pltpu.make_async_remote_copy(
                    src_ref=ins[k].at[0].at[hf], dst_ref=got, send_sem=send.at[r * nk + k],
                    recv_sem=recv.at[r * nk + k], device_id=(cx, cy, c), device_id_type=MESH)
                cp.wait_send()
                cp.wait_recv()
                pltpu.make_async_remote_copy(
                    src_ref=got, dst_ref=got, send_sem=send2.at[r * nk + k], recv_sem=recv2.at[r * nk + k],
                    device_id=(x, y, 1 - c), device_id_type=MESH).start()

    res = pl.pallas_call(
        body, name="gather_relay",
        out_shape=[pltpu.HBM(t.shape, t.dtype) for t in lands] + [pltpu.SemaphoreType.DMA((N_GATHER,))] * 2,
        in_specs=[ANY] * nk + [HBM] * nk + [SEM, SEM, ANY], out_specs=[HBM] * nk + [SEM, SEM],
        input_output_aliases={nk + k: k for k in range(nk)},
        compiler_params=pltpu.CompilerParams(has_side_effects=EFFECT))(*shards, *lands, *sems, after)
    return res[:nk], (res[nk], res[nk + 1])


def _gather_wait_relay(lands, sems):
    nk = len(lands)
    half = [t.shape[1] // 2 for t in lands]

    def body(*refs):
        land = refs[:nk]
        send, recv = refs[nk], refs[nk + 1]
        x, y, c = _my_pos()
        for r, (dx, dy) in enumerate(_CHIP_REL):
            chip = 2 * (x ^ dx) + (y ^ dy)
            for k in range(nk):
                mine = land[k].at[chip].at[pl.ds(c * half[k], half[k])]
                theirs = land[k].at[chip].at[pl.ds((1 - c) * half[k], half[k])]
                cp = pltpu.make_async_remote_copy(
                    src_ref=mine, dst_ref=theirs, send_sem=send.at[r * nk + k], recv_sem=recv.at[r * nk + k],
                    device_id=(x, y, 1 - c), device_id_type=MESH)
                cp.wait_send()
                cp.wait_recv()

    return pl.pallas_call(
        body, name="gather_wait_relay", out_shape=[pltpu.HBM(t.shape, t.dtype) for t in lands],
        in_specs=[HBM] * nk + [SEM, SEM], out_specs=[HBM] * nk, input_output_aliases={k: k for k in range(nk)},
        compiler_params=pltpu.CompilerParams(has_side_effects=EFFECT))(*lands, *sems)


def _gather_wait(l, shards, lands, sems, after):
    nk = len(shards)

    def body(*refs):
        ins, land = refs[:nk], refs[nk:2 * nk]
        send, recv = refs[2 * nk], refs[2 * nk + 1]
        x, y, c = _my_pos()
        for r, (dx, dy) in enumerate(_CHIP_REL):
            cx, cy = x ^ dx, y ^ dy
            for k in range(nk):
                cp = pltpu.make_async_remote_copy(
                    src_ref=ins[k].at[l], dst_ref=land[k].at[2 * cx + cy], send_sem=send.at[r * nk + k],
                    recv_sem=recv.at[r * nk + k], device_id=(cx, cy, c), device_id_type=MESH)
                cp.wait_send()
                cp.wait_recv()

    return pl.pallas_call(
        body, name=f"gather_wait_{l}", out_shape=[pltpu.HBM(t.shape, t.dtype) for t in lands],
        in_specs=[ANY] * nk + [HBM] * nk + [SEM, SEM, ANY], out_specs=[HBM] * nk,
        input_output_aliases={nk + k: k for k in range(nk)},
        compiler_params=pltpu.CompilerParams(has_side_effects=EFFECT))(*shards, *lands, *sems, after)


def _place_own(land, shard, l, tr):
    _, rows, cols = shard.shape[0], shard.shape[-2], shard.shape[-1]
    lead = shard.shape[1:-2]
    nlead = math.prod(lead)
    sh = shard.reshape((NL, nlead, rows, cols))
    ld = land.reshape((NCHIP, nlead, rows, cols))
    me = (2 * lax.axis_index("x") + lax.axis_index("y")).astype(jnp.int32).reshape(1)

    def body(me_ref, s_ref, l_in, o_ref):
        o_ref[...] = s_ref[...]

    out = pl.pallas_call(
        body,
        grid_spec=pltpu.PrefetchScalarGridSpec(
            num_scalar_prefetch=1, grid=(nlead, rows // tr),
            in_specs=[pl.BlockSpec((None, None, tr, cols), lambda b, i, me_ref: (l, b, i, 0)), ANY],
            out_specs=pl.BlockSpec((None, None, tr, cols), lambda b, i, me_ref: (me_ref[0], b, i, 0))),
        out_shape=_sds(ld.shape, BF16), input_output_aliases={2: 0},
        compiler_params=_cp("parallel", "parallel"), name="place_own")(me, sh, ld)
    return out.reshape(land.shape)


def _exch_start(l, parts):
    nk = len(parts)

    def body(*refs):
        ins, outs = refs[:nk], refs[nk:2 * nk]
        send, recv, token = refs[2 * nk:]
        x, y, c = _my_pos()
        for r, (dx, dy, dc) in enumerate(_DEV_REL):
            px, py, pc = x ^ dx, y ^ dy, c ^ dc
            for k in range(nk):
                pltpu.make_async_remote_copy(
                    src_ref=ins[k].at[2 * px + py, pc], dst_ref=outs[k].at[r], send_sem=send.at[r * nk + k],
                    recv_sem=recv.at[r * nk + k], device_id=(px, py, pc), device_id_type=MESH).start()
        token[...] = jnp.zeros_like(token)

    res = pl.pallas_call(
        body, name=f"exch_start_{l}",
        out_shape=[pltpu.HBM((7,) + p.shape[2:], BF16) for p in parts] + [pltpu.SemaphoreType.DMA((N_EXCH,))] * 2 + [_sds((8, 128), F32)],
        in_specs=[HBM] * nk, out_specs=[HBM] * nk + [SEM, SEM, pl.BlockSpec(memory_space=pltpu.VMEM)],
        compiler_params=pltpu.CompilerParams(has_side_effects=EFFECT))(*[_in_hbm(p) for p in parts])
    return res[:nk], (res[nk], res[nk + 1]), res[-1]


def _exch_wait(l, parts, lands, sems, after):
    nk = len(parts)

    def body(*refs):
        ins, land = refs[:nk], refs[nk:2 * nk]
        send, recv = refs[2 * nk], refs[2 * nk + 1]
        x, y, c = _my_pos()
        for r, (dx, dy, dc) in enumerate(_DEV_REL):
            px, py, pc = x ^ dx, y ^ dy, c ^ dc
            for k in range(nk):
                cp = pltpu.make_async_remote_copy(
                    src_ref=ins[k].at[2 * px + py, pc], dst_ref=land[k].at[r], send_sem=send.at[r * nk + k],
                    recv_sem=recv.at[r * nk + k], device_id=(px, py, pc), device_id_type=MESH)
                cp.wait_send()
                cp.wait_recv()

    return pl.pallas_call(
        body, name=f"exch_wait_{l}", out_shape=[pltpu.HBM(t.shape, t.dtype) for t in lands],
        in_specs=[ANY] * nk + [HBM] * nk + [SEM, SEM, ANY], out_specs=[HBM] * nk,
        input_output_aliases={nk + k: k for k in range(nk)},
        compiler_params=pltpu.CompilerParams(has_side_effects=EFFECT))(*parts, *lands, *sems, after)


def _chip_half():
    x, y, c = _my_pos()
    return jnp.stack([2 * x + y, c]).astype(jnp.int32)


def _sum_into(full, part, land, l, tr):
    _, _, r2, cols = full.shape

    def body(pos_ref, p_ref, r_ref, f_in, o_ref):
        acc = p_ref[...].astype(F32)
        for r in range(7):
            acc = acc + r_ref[r].astype(F32)
        o_ref[...] = acc

    return pl.pallas_call(
        body,
        grid_spec=pltpu.PrefetchScalarGridSpec(
            num_scalar_prefetch=1, grid=(r2 // tr,),
            in_specs=[pl.BlockSpec((None, None, tr, cols), lambda i, pos: (pos[0], pos[1], i, 0)),
                      pl.BlockSpec((7, tr, cols), lambda i, pos: (0, i, 0)), ANY],
            out_specs=pl.BlockSpec((None, None, tr, cols), lambda i, pos: (l, pos[1], i, 0))),
        out_shape=_sds(full.shape, F32), input_output_aliases={3: 0},
        compiler_params=_cp("parallel"), name="sum_into")(_chip_half(), part, land, full)


def _share_full(fulls):
    nk = len(fulls)

    def body(*refs):
        ins, outs = refs[:nk], refs[nk:2 * nk]
        send, recv = refs[2 * nk:]
        x, y, c = _my_pos()

        def copy(k, hf):
            return pltpu.make_async_remote_copy(
                src_ref=ins[k].at[:, hf], dst_ref=outs[k].at[:, hf], send_sem=send.at[k], recv_sem=recv.at[k],
                device_id=(x, y, 1 - c), device_id_type=MESH)

        for k in range(nk):
            copy(k, c).start()
        for k in range(nk):
            copy(k, 1 - c).wait_recv()
        for k in range(nk):
            copy(k, c).wait_send()

    return pl.pallas_call(
        body, out_shape=[_sds(f.shape, F32) for f in fulls], in_specs=[ANY] * nk, out_specs=[ANY] * nk,
        scratch_shapes=[pltpu.SemaphoreType.DMA((nk,))] * 2, input_output_aliases={k: k for k in range(nk)},
        name="share_full")(*fulls)


def _small_start(packed):
    def body(in_ref, out_ref, send, recv, token):
        x, y, c = _my_pos()
        for r, (dx, dy, dc) in enumerate(_DEV_REL):
            pltpu.make_async_remote_copy(
                src_ref=in_ref, dst_ref=out_ref.at[r], send_sem=send.at[r], recv_sem=recv.at[r],
                device_id=(x ^ dx, y ^ dy, c ^ dc), device_id_type=MESH).start()
        token[...] = jnp.zeros_like(token)

    res = pl.pallas_call(
        body, name="small_start",
        out_shape=[pltpu.HBM((7,) + packed.shape, F32)] + [pltpu.SemaphoreType.DMA((7,))] * 2 + [_sds((8, 128), F32)],
        in_specs=[HBM], out_specs=[HBM, SEM, SEM, pl.BlockSpec(memory_space=pltpu.VMEM)],
        compiler_params=pltpu.CompilerParams(has_side_effects=EFFECT))(_in_hbm(packed))
    return res[0], (res[1], res[2]), res[3]


def _small_wait(packed, land, sems, after):
    def body(in_ref, land_ref, send, recv, after_ref, out_ref):
        x, y, c = _my_pos()
        for r, (dx, dy, dc) in enumerate(_DEV_REL):
            cp = pltpu.make_async_remote_copy(
                src_ref=in_ref, dst_ref=land_ref.at[r], send_sem=send.at[r], recv_sem=recv.at[r],
                device_id=(x ^ dx, y ^ dy, c ^ dc), device_id_type=MESH)
            cp.wait_send()
            cp.wait_recv()

    return pl.pallas_call(
        body, name="small_wait", out_shape=pltpu.HBM(land.shape, land.dtype),
        in_specs=[ANY, HBM, SEM, SEM, ANY], out_specs=HBM, input_output_aliases={1: 0},
        compiler_params=pltpu.CompilerParams(has_side_effects=EFFECT))(packed, land, *sems, after)


def _sum_small(packed, land, tr):
    rows = packed.shape[0]
    x, y, c = _my_pos()
    me = (4 * x + 2 * y + c).astype(jnp.int32).reshape(1)

    def sbody(me_ref, p_ref, r_ref, o_ref):
        me_dev = me_ref[0]
        own = p_ref[...]
        acc = None
        for s in range(NDEV):
            rel = s ^ me_dev
            v = jnp.where(rel == 0, own, r_ref[jnp.maximum(rel - 1, 0)])
            acc = v if acc is None else acc + v
        o_ref[...] = acc

    return pl.pallas_call(
        sbody,
        grid_spec=pltpu.PrefetchScalarGridSpec(
            num_scalar_prefetch=1, grid=(rows // tr,),
            in_specs=[pl.BlockSpec((tr, 128), lambda i, me_ref: (i, 0)), pl.BlockSpec((7, tr, 128), lambda i, me_ref: (0, i, 0))],
            out_specs=pl.BlockSpec((tr, 128), lambda i, me_ref: (i, 0))),
        out_shape=_sds((rows, 128), F32), compiler_params=_cp("parallel"), name="sum_small")(me, packed, land)


def _rms_fwd(x, g, tm):
    n = x.shape[0]

    def body(x_ref, g_ref, h_ref):
        xv = x_ref[...]
        r = lax.rsqrt(jnp.mean(xv * xv, axis=-1, keepdims=True) + EPS)
        h_ref[...] = (xv * r * g_ref[...]).astype(BF16)

    return pl.pallas_call(
        body, grid=(n // tm,),
        in_specs=[pl.BlockSpec((tm, D), lambda i: (i, 0)), pl.BlockSpec((1, D), lambda i: (0, 0))],
        out_specs=pl.BlockSpec((tm, D), lambda i: (i, 0)), out_shape=_sds((n, D), BF16),
        compiler_params=_cp("parallel"), name="rms_fwd")(x, g)


def _rms_bwd(dh, x, g, dres, tm=512):
    n = x.shape[0]

    def body(dh_ref, x_ref, g_ref, dr_ref, dx_ref, dg_ref):
        i = pl.program_id(0)
        xv = x_ref[...]
        r = lax.rsqrt(jnp.mean(xv * xv, axis=-1, keepdims=True) + EPS)
        xh = xv * r
        dhv = dh_ref[...]
        dxh = dhv * g_ref[...]
        dx_ref[...] = dr_ref[...] + r * (dxh - xh * jnp.mean(dxh * xh, axis=-1, keepdims=True))
        part = jnp.sum(dhv * xh, axis=0, keepdims=True)

        @pl.when(i == 0)
        def _():
            dg_ref[...] = part

        @pl.when(i > 0)
        def _():
            dg_ref[...] += part

    row = pl.BlockSpec((tm, D), lambda i: (i, 0))
    vec = pl.BlockSpec((1, D), lambda i: (0, 0))
    return pl.pallas_call(
        body, grid=(n // tm,), in_specs=[row, row, vec, row], out_specs=[row, vec],
        out_shape=[_sds((n, D), F32), _sds((1, D), F32)], compiler_params=_cp("arbitrary"), name="rms_bwd")(dh, x, g, dres)


def _loss_head(x, tgt, g, tm=512):
    def body(x_ref, t_ref, g_ref, dx_ref, dg_ref, ls_ref):
        i = pl.program_id(0)
        xv = x_ref[...]
        r = lax.rsqrt(jnp.mean(xv * xv, axis=-1, keepdims=True) + EPS)
        xh = xv * r
        gv = g_ref[...]
        diff = xh * gv - t_ref[...]
        dy = diff * (1.0 / D)
        dxh = dy * gv
        dx_ref[...] = r * (dxh - xh * jnp.mean(dxh * xh, axis=-1, keepdims=True))
        part_g = jnp.sum(dy * xh, axis=0, keepdims=True)
        part_l = jnp.sum(diff * diff, axis=0, keepdims=True)

        @pl.when(i == 0)
        def _():
            dg_ref[...] = part_g
            ls_ref[...] = part_l

        @pl.when(i > 0)
        def _():
            dg_ref[...] += part_g
            ls_ref[...] += part_l

        @pl.when(i == pl.num_programs(0) - 1)
        def _():
            tot = jnp.sum(ls_ref[...], axis=-1, keepdims=True) * (0.5 / D)
            ls_ref[...] = jnp.broadcast_to(tot, (1, D))

    row = pl.BlockSpec((tm, D), lambda i: (i, 0))
    vec = pl.BlockSpec((1, D), lambda i: (0, 0))
    return pl.pallas_call(
        body, grid=(S // tm,), in_specs=[row, row, vec], out_specs=[row, vec, vec],
        out_shape=[_sds((S, D), F32), _sds((1, D), F32), _sds((1, D), F32)],
        compiler_params=_cp("arbitrary"), name="loss_head")(x, tgt, g)


def _adamw(w, g, m, v):
    shape = w.shape
    cols = shape[-1] if w.ndim > 1 else shape[0]
    rows = w.size // cols
    w2, g2, m2, v2 = (t.reshape(rows, cols) for t in (w, g, m, v))
    tr = rows
    while tr * cols * 4 > (1 << 20) and tr % 16 == 0:
        tr //= 2
    c1 = 1.0 - ADAM_B1 ** ADAM_STEP
    c2 = 1.0 - ADAM_B2 ** ADAM_STEP

    def body(w_ref, g_ref, m_ref, v_ref, d_ref, nm_ref, nv_ref):
        gv = g_ref[...]
        mn = ADAM_B1 * m_ref[...] + (1.0 - ADAM_B1) * gv
        vn = ADAM_B2 * v_ref[...] + (1.0 - ADAM_B2) * (gv * gv)
        d_ref[...] = -ADAM_LR * ((mn / c1) / (jnp.sqrt(vn / c2) + ADAM_EPS) + ADAM_WD * w_ref[...])
        nm_ref[...] = mn
        nv_ref[...] = vn

    blk = pl.BlockSpec((tr, cols), lambda i: (i, 0))
    outs = pl.pallas_call(
        body, grid=(rows // tr,), in_specs=[blk] * 4, out_specs=[blk] * 3,
        out_shape=[_sds((rows, cols), F32)] * 3, compiler_params=_cp("parallel"), name="adamw")(w2, g2, m2, v2)
    return tuple(o.reshape(shape) for o in outs)


def _proj_fwd(h, wg, tm=512):
    def body(h_ref, w_ref, o_ref):
        o_ref[...] = _dot(h_ref[...], w_ref[...])

    return pl.pallas_call(
        body, grid=(NCHIP, S // tm),
        in_specs=[pl.BlockSpec((tm, D), lambda c, i: (i, 0)), pl.BlockSpec((None, D, CW), lambda c, i: (c, 0, 0))],
        out_specs=pl.BlockSpec((tm, CW), lambda c, i: (i, c)), out_shape=_sds((S, DIN), F32),
        compiler_params=_cp("parallel", "parallel"), name="proj_fwd")(h, wg)


def _proj_bwd_x(dproj, wg, dep, tm=2048):
    nk = NCHIP * NJ

    def body(d_ref, w_ref, dep_ref, o_ref):
        k = pl.program_id(1)
        part = _dot_nt(d_ref[...], w_ref[...])

        @pl.when(k == 0)
        def _():
            o_ref[...] = part

        @pl.when(k > 0)
        def _():
            o_ref[...] += part

    return pl.pallas_call(
        body, grid=(S // tm, nk),
        in_specs=[pl.BlockSpec((tm, TN_IN), lambda i, k: (i, k)),
                  pl.BlockSpec((None, D, TN_IN), lambda i, k: (k // NJ, 0, k % NJ)), ANY],
        out_specs=pl.BlockSpec((tm, D), lambda i, k: (i, 0)), out_shape=_sds((S, D), F32),
        compiler_params=_cp("parallel", "arbitrary"), name="proj_bwd_x")(dproj, wg, dep)


def _proj_bwd_w(h, dproj, tk=1024):
    nk = S // tk

    def body(h_ref, d_ref, o_ref, acc):
        k = pl.program_id(1)
        part = _dot_tn(h_ref[...], d_ref[...])

        @pl.when(k == 0)
        def _():
            acc[...] = part

        @pl.when(k > 0)
        def _():
            acc[...] += part

        @pl.when(k == nk - 1)
        def _():
            o_ref[...] = acc[...].astype(BF16)

    return pl.pallas_call(
        body, grid=(NCHIP, nk),
        in_specs=[pl.BlockSpec((tk, D), lambda c, k: (k, 0)), pl.BlockSpec((tk, CW), lambda c, k: (k, c))],
        out_specs=pl.BlockSpec((None, D, CW), lambda c, k: (c, 0, 0)), out_shape=_sds((NCHIP, D, CW), BF16),
        scratch_shapes=[pltpu.VMEM((D, CW), F32)],
        compiler_params=_cp("parallel", "arbitrary"), name="proj_bwd_w")(h, dproj)


def _merge_fwd(y_all, wbr, proj, tm=512):
    cb = D // NCHIP

    def body(y0, y1, y2, y3, w_ref, g0, g1, g2, g3, z_ref):
        acc = None
        for b, (y_ref, g_ref) in enumerate(zip((y0, y1, y2, y3), (g0, g1, g2, g3))):
            t = _dot(y_ref[...], w_ref[b]) * _sigmoid(g_ref[...])
            acc = t if acc is None else acc + t
        z_ref[...] = acc.astype(BF16)

    y_specs = [pl.BlockSpec((None, tm, W), functools.partial(lambda b, c, i: (b, i, 0), b)) for b in range(4)]
    g_specs = [pl.BlockSpec((tm, cb), functools.partial(lambda b, c, i: (i, (P_GM * W + b * D) // cb + c), b)) for b in range(4)]
    return pl.pallas_call(
        body, grid=(NCHIP, S // tm),
        in_specs=y_specs + [pl.BlockSpec((None, 4, W, cb), lambda c, i: (c, 0, 0, 0))] + g_specs,
        out_specs=pl.BlockSpec((tm, cb), lambda c, i: (i, c)), out_shape=_sds((S, D), BF16),
        compiler_params=_cp("parallel", "parallel"), name="merge_fwd")(y_all, y_all, y_all, y_all, wbr, proj, proj, proj, proj)


def _merge_bwd(dz, y_all, wbr, proj, dproj, tm=512):
    cb = D // NCHIP
    ni = S // tm

    def body(dz_ref, y_ref, w_ref, ga_ref, gb_ref, dp_in, dp_ref, dy_ref, dw_ref, acc, obuf, osem):
        b = pl.program_id(0)
        i = pl.program_id(1)
        yv = y_ref[...]
        dys = []

        def fill(slot):
            dy = None
            for c in range(NCHIP):
                wv = w_ref[c]
                t = _dot(yv, wv)
                g_ref = ga_ref if c < 2 else gb_ref
                g = _sigmoid(g_ref[:, (c % 2) * cb:(c % 2 + 1) * cb])
                dzc = dz_ref[:, c * cb:(c + 1) * cb]
                slot[:, c * cb:(c + 1) * cb] = (dzc * t * g * (1.0 - g)).astype(BF16)
                dt = (dzc * g).astype(BF16)
                part = _dot_nt(dt, wv)
                dy = part if dy is None else dy + part
                dwp = _dot_tn(yv, dt)

                @pl.when(i == 0)
                def _():
                    acc[c] = dwp

                @pl.when(i > 0)
                def _():
                    acc[c] += dwp
            dys.append(dy)

        _tile_put(obuf, osem, lambda st: dp_ref.at[pl.ds((st % ni) * tm, tm), pl.ds(P_GM * W + (st // ni) * D, D)],
                  b * ni + i, 4 * ni, fill)
        dy_ref[...] = dys[0]

        @pl.when(i == ni - 1)
        def _():
            dw_ref[...] = acc[...].astype(BF16)

    return pl.pallas_call(
        body, grid=(4, ni),
        in_specs=[pl.BlockSpec((tm, D), lambda b, i: (i, 0)), pl.BlockSpec((None, tm, W), lambda b, i: (b, i, 0)),
                  pl.BlockSpec((NCHIP, None, W, cb), lambda b, i: (0, b, 0, 0)),
                  pl.BlockSpec((tm, W), lambda b, i: (i, P_GM + 2 * b)), pl.BlockSpec((tm, W), lambda b, i: (i, P_GM + 2 * b + 1)), ANY],
        out_specs=[ANY, pl.BlockSpec((None, tm, W), lambda b, i: (b, i, 0)), pl.BlockSpec((NCHIP, None, W, cb), lambda b, i: (0, b, 0, 0))],
        out_shape=[_sds((S, DIN), BF16), _sds((4, S, W), F32), _sds((NCHIP, 4, W, cb), BF16)],
        scratch_shapes=[pltpu.VMEM((NCHIP, W, cb), F32), pltpu.VMEM((2, tm, D), BF16), pltpu.SemaphoreType.DMA((2,))],
        input_output_aliases={5: 0}, compiler_params=_cp("arbitrary", "arbitrary"), name="merge_bwd")(dz, y_all, wbr, proj, proj, dproj)


def _out_fwd(z, wo, x, tm=512):
    def body(z_ref, w_ref, x_ref, o_ref):
        o_ref[...] = x_ref[...] + _dot(z_ref[...], w_ref[...])

    row = pl.BlockSpec((tm, D), lambda i: (i, 0))
    return pl.pallas_call(
        body, grid=(S // tm,), in_specs=[row, pl.BlockSpec((D, D), lambda i: (0, 0)), row], out_specs=row,
        out_shape=_sds((S, D), F32), compiler_params=_cp("parallel"), name="out_fwd")(z, wo, x)


def _out_bwd(dx, z, wo, tm=512):
    ni = S // tm

    def body(dx_ref, z_ref, w_ref, dz_ref, dw_ref, acc):
        i = pl.program_id(0)
        dxb = dx_ref[...].astype(BF16)
        dz_ref[...] = _dot_nt(dxb, w_ref[...])
        part = _dot_tn(z_ref[...], dxb)

        @pl.when(i == 0)
        def _():
            acc[...] = part

        @pl.when(i > 0)
        def _():
            acc[...] += part

        @pl.when(i == ni - 1)
        def _():
            dw_ref[...] = acc[...].astype(BF16)

    row = pl.BlockSpec((tm, D), lambda i: (i, 0))
    full = pl.BlockSpec((D, D), lambda i: (0, 0))
    return pl.pallas_call(
        body, grid=(ni,), in_specs=[row, row, full], out_specs=[row, full],
        out_shape=[_sds((S, D), F32), _sds((D, D), BF16)], scratch_shapes=[pltpu.VMEM((D, D), F32)],
        compiler_params=_cp("arbitrary"), name="out_bwd")(dx, z, wo)


def _gelu_parts(a):
    cdf = 0.5 * (1.0 + lax.erf(a * INV_SQRT2))
    return a * cdf, cdf


def _ln_parts(v):
    mu = jnp.mean(v, axis=-1, keepdims=True)
    vc = v - mu
    rs = lax.rsqrt(jnp.mean(vc * vc, axis=-1, keepdims=True) + EPS)
    return vc * rs, rs


def _causal_mask():
    return lax.broadcasted_iota(jnp.int32, (HD, HD), 0) >= lax.broadcasted_iota(jnp.int32, (HD, HD), 1)


def _gmlp_fwd(proj, lg, lb, ws, bias, y_all, tm=512):
    def body(uv_ref, gt_ref, lg_ref, lb_ref, ws_ref, b_ref, y_in, y_ref):
        act, _ = _gelu_parts(uv_ref[...])
        u = act[:, :W]
        xh, _ = _ln_parts(act[:, W:])
        vn = (xh * lg_ref[...] + lb_ref[...]).astype(BF16)
        gt = gt_ref[...]
        us = u * (gt * _sigmoid(gt))
        mask = _causal_mask()
        for h in range(4):
            wm = jnp.where(mask, ws_ref[h], 0.0).astype(BF16)
            cs = slice(h * HD, (h + 1) * HD)
            for c in range(tm // HD):
                rs_ = slice(c * HD, (c + 1) * HD)
                mixed = _dot(wm, vn[rs_, cs]) + b_ref[h]
                y_ref[rs_, cs] = (us[rs_, cs] * mixed).astype(BF16)

    vec = pl.BlockSpec((1, W), lambda i: (0, 0))
    mats = pl.BlockSpec((4, HD, HD), lambda i: (0, 0, 0))
    return pl.pallas_call(
        body, grid=(S // tm,),
        in_specs=[pl.BlockSpec((tm, 2 * W), lambda i: (i, 0)), pl.BlockSpec((tm, W), lambda i: (i, P_AGATE)), vec, vec, mats, mats, ANY],
        out_specs=pl.BlockSpec((None, tm, W), lambda i: (0, i, 0)), out_shape=_sds((4, S, W), BF16),
        input_output_aliases={6: 0}, compiler_params=_cp("parallel"), name="gmlp_fwd")(proj, proj, lg, lb, ws, bias, y_all)


def _gmlp_bwd(proj, dy_all, lg, lb, ws, bias, dproj, tm=256):
    ni = S // tm

    def body(uv_ref, gt_ref, dy_ref, lg_ref, lb_ref, ws_ref, b_ref, dp_in, dp_ref, dws_ref, dbs_ref, dlg_ref, dlb_ref, mix_s, dvn_s):
        i = pl.program_id(0)
        a0 = uv_ref[...]
        act, cdf = _gelu_parts(a0)
        u = act[:, :W]
        xh, rs = _ln_parts(act[:, W:])
        lgv = lg_ref[...]
        vn = (xh * lgv + lb_ref[...]).astype(BF16)
        mask = _causal_mask()
        wms = [jnp.where(mask, ws_ref[h], 0.0).astype(BF16) for h in range(4)]
        for h in range(4):
            cs = slice(h * HD, (h + 1) * HD)
            for c in range(tm // HD):
                rs_ = slice(c * HD, (c + 1) * HD)
                mix_s[rs_, cs] = _dot(wms[h], vn[rs_, cs]) + b_ref[h]
        mixed = mix_s[...]
        gt = gt_ref[...]
        sg = _sigmoid(gt)
        sl = gt * sg
        dyv = dy_ref[...]
        dum = dyv * sl
        dgate = dyv * (u * mixed) * (sg * (1.0 + gt * (1.0 - sg)))
        du = dum * mixed
        dmix = dum * u
        dmb = dmix.astype(BF16)
        for h in range(4):
            cs = slice(h * HD, (h + 1) * HD)
            dw = None
            db = None
            for c in range(tm // HD):
                rs_ = slice(c * HD, (c + 1) * HD)
                dvn_s[rs_, cs] = _dot_tn(wms[h], dmb[rs_, cs])
                pw = _dot_nt(dmb[rs_, cs], vn[rs_, cs])
                dw = pw if dw is None else dw + pw
                db = dmix[rs_, cs] if db is None else db + dmix[rs_, cs]

            @pl.when(i == 0)
            def _():
                dws_ref[h] = dw
                dbs_ref[h] = db

            @pl.when(i > 0)
            def _():
                dws_ref[h] += dw
                dbs_ref[h] += db

            @pl.when(i == ni - 1)
            def _():
                dws_ref[h] = jnp.where(mask, dws_ref[h], 0.0)
                dbs_ref[h] = jnp.broadcast_to(jnp.sum(dbs_ref[h], axis=1, keepdims=True), (HD, HD))
        dvn = dvn_s[...]
        plg = jnp.sum(dvn * xh, axis=0, keepdims=True)
        plb = jnp.sum(dvn, axis=0, keepdims=True)

        @pl.when(i == 0)
        def _():
            dlg_ref[...] = plg
            dlb_ref[...] = plb

        @pl.when(i > 0)
        def _():
            dlg_ref[...] += plg
            dlb_ref[...] += plb

        dxh = dvn * lgv
        dv = rs * (dxh - jnp.mean(dxh, axis=-1, keepdims=True) - xh * jnp.mean(dxh * xh, axis=-1, keepdims=True))
        gp = cdf + a0 * (jnp.exp(-0.5 * a0 * a0) * INV_SQRT2PI)
        dp_ref[:, :W] = (du * gp[:, :W]).astype(BF16)
        dp_ref[:, W:2 * W] = (dv * gp[:, W:]).astype(BF16)
        dp_ref[:, 2 * W:] = dgate.astype(BF16)

    vec = pl.BlockSpec((1, W), lambda i: (0, 0))
    mats = pl.BlockSpec((4, HD, HD), lambda i: (0, 0, 0))
    return pl.pallas_call(
        body, grid=(ni,),
        in_specs=[pl.BlockSpec((tm, 2 * W), lambda i: (i, 0)), pl.BlockSpec((tm, W), lambda i: (i, P_AGATE)),
                  pl.BlockSpec((None, tm, W), lambda i: (0, i, 0)), vec, vec, mats, mats, ANY],
        out_specs=[pl.BlockSpec((tm, 3 * W), lambda i: (i, 0)), mats, mats, vec, vec],
        out_shape=[_sds((S, DIN), BF16), _sds((4, HD, HD), F32), _sds((4, HD, HD), F32), _sds((1, W), F32), _sds((1, W), F32)],
        scratch_shapes=[pltpu.VMEM((tm, W), F32), pltpu.VMEM((tm, W), F32)],
        input_output_aliases={7: 0}, compiler_params=_cp("arbitrary"), name="gmlp_bwd")(proj, proj, dy_all, lg, lb, ws, bias, dproj)


def _pool_diff(p, halo, row0, tm):
    xx = jnp.concatenate([halo, p], axis=0)
    t1 = (row0 + 1 + lax.broadcasted_iota(jnp.int32, (tm, 1), 0)).astype(F32)
    out = []
    for g, win in enumerate(POOL_WINDOWS):
        s = xx[:, g * HD:(g + 1) * HD]
        sh = 1
        while sh < win:
            s = s + pltpu.roll(s, sh, 0)
            sh *= 2
        out.append(s[HALO:] / jnp.minimum(t1, float(win)) - p[:, g * HD:(g + 1) * HD])
    return out


def _pool_fwd(proj, pw, sc, y_all, tm=512):
    rb = tm // HALO

    def body(p_ref, h_ref, gt_ref, pw_ref, sc_ref, y_in, y_ref):
        i = pl.program_id(0)
        halo = jnp.where(i > 0, h_ref[...], 0.0)
        ds = _pool_diff(p_ref[...], halo, i * tm, tm)
        gt = gt_ref[...]
        sl = gt * _sigmoid(gt)
        for g in range(4):
            cs = slice(g * HD, (g + 1) * HD)
            lin = _dot(ds[g].astype(BF16), pw_ref[g].astype(BF16))
            y_ref[:, cs] = (lin * sc_ref[:, cs] * sl[:, cs]).astype(BF16)

    return pl.pallas_call(
        body, grid=(S // tm,),
        in_specs=[pl.BlockSpec((tm, W), lambda i: (i, P_PIN)),
                  pl.BlockSpec((HALO, W), lambda i: (jnp.maximum(i * rb - 1, 0), P_PIN)),
                  pl.BlockSpec((tm, W), lambda i: (i, P_PGATE)),
                  pl.BlockSpec((4, HD, HD), lambda i: (0, 0, 0)), pl.BlockSpec((1, W), lambda i: (0, 0)), ANY],
        out_specs=pl.BlockSpec((None, tm, W), lambda i: (1, i, 0)), out_shape=_sds((4, S, W), BF16),
        input_output_aliases={5: 0}, compiler_params=_cp("parallel"), name="pool_fwd")(proj, proj, proj, pw, sc, y_all)


def _pool_bwd(proj, dy_all, pw, sc, dproj, tm=256):
    ni = S // tm
    rb = tm // HALO
    last_rb = S // HALO - 1
    rx = tm + HALO

    def body(p_ref, h_ref, gt_ref, gh_ref, dy_ref, dyh_ref, pw_ref, sc_ref, dp_in, dp_ref, dpw_ref, dsc_ref, obuf, osem):
        i = pl.program_id(0)
        halo = jnp.where(i > 0, h_ref[...], 0.0)
        ds = _pool_diff(p_ref[...], halo, i * tm, tm)
        nxt = i < ni - 1
        gx = jnp.concatenate([gt_ref[...], gh_ref[...]], axis=0)
        dyx = jnp.concatenate([dy_ref[...], jnp.where(nxt, dyh_ref[...], 0.0)], axis=0)
        sgx = _sigmoid(gx)
        slx = gx * sgx
        scv = sc_ref[...]
        dlinx = dyx * slx * scv
        t1 = (i * tm + 1 + lax.broadcasted_iota(jnp.int32, (rx, 1), 0)).astype(F32)
        gt, sg, sl, dyv = gx[:tm], sgx[:tm], slx[:tm], dyx[:tm]
        dsl = sg * (1.0 + gt * (1.0 - sg))

        def fill(slot):
            for g, win in enumerate(POOL_WINDOWS):
                cs = slice(g * HD, (g + 1) * HD)
                wv = pw_ref[g].astype(BF16)
                dlb = dlinx[:, cs].astype(BF16)
                ddx = _dot_nt(dlb, wv)
                f = ddx / jnp.minimum(t1, float(win))
                sh = 1
                while sh < win:
                    f = f + pltpu.roll(f, rx - sh, 0)
                    sh *= 2
                slot[:, cs] = (f[:tm] - ddx[:tm]).astype(BF16)
                db = ds[g].astype(BF16)
                lin = _dot(db, wv)
                slot[:, W + g * HD:W + (g + 1) * HD] = (dyv[:, cs] * lin * scv[:, cs] * dsl[:, cs]).astype(BF16)
                psc = jnp.sum(dyv[:, cs] * sl[:, cs] * lin, axis=0, keepdims=True)
                pwg = _dot_tn(db, dlb[:tm])

                @pl.when(i == 0)
                def _():
                    dpw_ref[g] = pwg
                    dsc_ref[:, cs] = psc

                @pl.when(i > 0)
                def _():
                    dpw_ref[g] += pwg
                    dsc_ref[:, cs] += psc

        _tile_put(obuf, osem, lambda st: dp_ref.at[pl.ds(st * tm, tm), pl.ds(P_PIN * W, 2 * W)], i, ni, fill)

    mats = pl.BlockSpec((4, HD, HD), lambda i: (0, 0, 0))
    vec = pl.BlockSpec((1, W), lambda i: (0, 0))
    return pl.pallas_call(
        body, grid=(ni,),
        in_specs=[pl.BlockSpec((tm, W), lambda i: (i, P_PIN)),
                  pl.BlockSpec((HALO, W), lambda i: (jnp.maximum(i * rb - 1, 0), P_PIN)),
                  pl.BlockSpec((tm, W), lambda i: (i, P_PGATE)),
                  pl.BlockSpec((HALO, W), lambda i: (jnp.minimum((i + 1) * rb, last_rb), P_PGATE)),
                  pl.BlockSpec((None, tm, W), lambda i: (1, i, 0)),
                  pl.BlockSpec((None, HALO, W), lambda i: (1, jnp.minimum((i + 1) * rb, last_rb), 0)),
                  mats, vec, ANY],
        out_specs=[ANY, mats, vec],
        out_shape=[_sds((S, DIN), BF16), _sds((4, HD, HD), F32), _sds((1, W), F32)],
        scratch_shapes=[pltpu.VMEM((2, tm, 2 * W), BF16), pltpu.SemaphoreType.DMA((2,))],
        input_output_aliases={8: 0}, compiler_params=_cp("arbitrary"), name="pool_bwd")(proj, proj, proj, proj, dy_all, dy_all, pw, sc, dproj)


ATT_STEP = ((1, 4), (4, 1), (4, 1))
ATT_GROUP = 4
ATT_GROUP_BWD = 2


def _att_masks():
    qi = lax.broadcasted_iota(jnp.int32, (HD, HD), 0)
    kj = lax.broadcasted_iota(jnp.int32, (HD, HD), 1)
    return kj >= qi, kj <= qi


def _dilate(src, dst, d, rows, cast=None):
    for r in range(d):
        for h in range(4):
            v = src.at[h][pl.ds(r, rows // d, stride=d), :] if d > 1 else src[h]
            dst[r * 4 + h] = v if cast is None else v.astype(cast)


def _undilate(src, dst, d, rows):
    for r in range(d):
        for h in range(4):
            if d > 1:
                dst.at[h][pl.ds(r, rows // d, stride=d), :] = src[r * 4 + h].astype(F32)
            else:
                dst[h] = src[h].astype(F32)


def _dil_spec(d, tm):
    return pl.BlockSpec((4 * d, tm // d, HD), lambda i: (0, i, 0))


def _att_prep(proj, tm=512):
    def body(q0, q1, q2, k_ref, v_ref, *rest):
        outs, scr = rest[:9], rest[9]
        for j, (src, dsts) in enumerate(((q0, ((0, outs[0]),)), (q1, ((1, outs[1]),)), (q2, ((2, outs[2]),)),
                                         (k_ref, tuple((g, outs[3 + g]) for g in range(3))),
                                         (v_ref, tuple((g, outs[6 + g]) for g in range(3))))):
            for h in range(4):
                scr[j, h] = src[:, h * HD:(h + 1) * HD]
            for g, dst in dsts:
                _dilate(scr.at[j], dst, DILATIONS[g], tm, BF16)

    def piece(p):
        return pl.BlockSpec((tm, W), lambda i: (i, p))

    shapes = [_sds((4 * d, S // d, HD), BF16) for d in DILATIONS]
    res = pl.pallas_call(
        body, grid=(S // tm,),
        in_specs=[piece(P_CQ), piece(P_CQ + 1), piece(P_CQ + 2), piece(P_CK), piece(P_CV)],
        out_specs=[_dil_spec(d, tm) for d in DILATIONS] * 3, out_shape=shapes * 3,
        scratch_shapes=[pltpu.VMEM((5, 4, tm, HD), F32)],
        compiler_params=_cp("parallel"), name="att_prep")(proj, proj, proj, proj, proj)
    return res[0:3], res[3:6], res[6:9]


def _att_specs(g):
    d = DILATIONS[g]
    nres, njb = ATT_STEP[g]
    nb = S // d // HD
    own = pl.BlockSpec((4 * nres, njb * HD, HD), lambda r, j: (r, j, 0))
    prev = pl.BlockSpec((4 * nres, HD, HD), lambda r, j: (r, jnp.maximum(j * njb - 1, 0), 0))
    nxt = pl.BlockSpec((4 * nres, HD, HD), lambda r, j: (r, jnp.minimum((j + 1) * njb, nb - 1), 0))
    return (d // nres, nb // njb), own, prev, nxt


def _att_fwd(q, k, v, g):
    d = DILATIONS[g]
    nres, njb = ATT_STEP[g]
    grid, own, prev, _ = _att_specs(g)

    def body(q_ref, kp_ref, ko_ref, vp_ref, vo_ref, o_ref, l_ref):
        jb = pl.program_id(1)
        m_prev, m_own = _att_masks()
        no_prev = jnp.where(jb > 0, 0.0, NEG)
        blocks = [(a, jj) for jj in range(njb) for a in range(4 * nres)]
        for g0 in range(0, len(blocks), ATT_GROUP):
            grp = blocks[g0:g0 + ATT_GROUP]
            sp, so, vp, vo = [], [], [], []
            for a, jj in grp:
                rs_ = slice(jj * HD, (jj + 1) * HD)
                qb = q_ref[a, rs_, :]
                if jj == 0:
                    kp, vpv, miss = kp_ref[a], vp_ref[a], no_prev
                else:
                    kp, vpv, miss = ko_ref[a, (jj - 1) * HD:jj * HD, :], vo_ref[a, (jj - 1) * HD:jj * HD, :], 0.0
                sp.append(jnp.where(m_prev, _dot_nt(qb, kp) * SCALE, NEG) + miss)
                so.append(jnp.where(m_own, _dot_nt(qb, ko_ref[a, rs_, :]) * SCALE, NEG))
                vp.append(vpv)
                vo.append(vo_ref[a, rs_, :])
            m = [jnp.max(jnp.maximum(p_, o_), axis=-1, keepdims=True) for p_, o_ in zip(sp, so)]
            ep = [jnp.exp(p_ - m_) for p_, m_ in zip(sp, m)]
            eo = [jnp.exp(o_ - m_) for o_, m_ in zip(so, m)]
            den = [jnp.sum(p_ + o_, axis=-1, keepdims=True) for p_, o_ in zip(ep, eo)]
            inv = [1.0 / d_ for d_ in den]
            for i, (a, jj) in enumerate(grp):
                rs_ = slice(jj * HD, (jj + 1) * HD)
                o_ref[a, rs_, :] = (_dot((ep[i] * inv[i]).astype(BF16), vp[i]) + _dot((eo[i] * inv[i]).astype(BF16), vo[i]))
                l_ref[a, rs_, :] = jnp.broadcast_to(m[i] + jnp.log(den[i]), (HD, HD))

    return pl.pallas_call(
        body, grid=grid, in_specs=[own, prev, own, prev, own], out_specs=[own, own],
        out_shape=[_sds((4 * d, S // d, HD), F32)] * 2,
        compiler_params=_cp("parallel", "parallel"), name="att_fwd")(q, k, k, v, v)


def _att_mix(os_, ls_, proj, y_all, tm=512):
    def body(o0, o1, o2, l0, l1, l2, gt_ref, y_in, y_ref, om_ref, lt_ref, so1, so2, sl1, sl2):
        _undilate(o1, so1, DILATIONS[1], tm)
        _undilate(o2, so2, DILATIONS[2], tm)
        _undilate(l1, sl1, DILATIONS[1], tm)
        _undilate(l2, sl2, DILATIONS[2], tm)
        for h in range(4):
            a, b, c = l0[h], sl1[h], sl2[h]
            m = jnp.maximum(jnp.maximum(a, b), c)
            ea, eb, ec = jnp.exp(a - m), jnp.exp(b - m), jnp.exp(c - m)
            z = ea + eb + ec
            inv = 1.0 / z
            o = (ea * inv) * o0[h] + (eb * inv) * so1[h] + (ec * inv) * so2[h]
            gt = gt_ref[:, h * HD:(h + 1) * HD]
            om_ref[h] = o
            lt_ref[h] = m + jnp.log(z)
            y_ref[:, h * HD:(h + 1) * HD] = (o * (gt * _sigmoid(gt))).astype(BF16)

    dil = [_dil_spec(d, tm) for d in DILATIONS]
    return pl.pallas_call(
        body, grid=(S // tm,),
        in_specs=dil * 2 + [pl.BlockSpec((tm, W), lambda i: (i, P_CGATE)), ANY],
        out_specs=[pl.BlockSpec((None, tm, W), lambda i: (2, i, 0)), dil[0], dil[0]],
        out_shape=[_sds((4, S, W), BF16), _sds((4, S, HD), F32), _sds((4, S, HD), F32)],
        scratch_shapes=[pltpu.VMEM((4, tm, HD), F32)] * 4,
        input_output_aliases={7: 0}, compiler_params=_cp("parallel"), name="att_mix")(*os_, *ls_, proj, y_all)


def _att_bwd_pre(dy_all, proj, om, lse, dproj, tm=512):
    def body(dy_ref, gt_ref, om_ref, ls_ref, dp_in, *rest):
        dos, dls, lss, dp_ref, sdo, sdl = rest[0:3], rest[3:6], rest[6:8], rest[8], rest[9], rest[10]
        for h in range(4):
            cs = slice(h * HD, (h + 1) * HD)
            gt = gt_ref[:, cs]
            sg = _sigmoid(gt)
            dyv = dy_ref[:, cs]
            o = om_ref[h]
            do = dyv * (gt * sg)
            dp_ref[:, cs] = (dyv * o * (sg * (1.0 + gt * (1.0 - sg)))).astype(BF16)
            sdo[h] = do
            sdl[h] = jnp.broadcast_to(jnp.sum(do * o, axis=-1, keepdims=True), (tm, HD))
        for g, d in enumerate(DILATIONS):
            _dilate(sdo, dos[g], d, tm, BF16)
            _dilate(sdl, dls[g], d, tm)
            if g > 0:
                _dilate(ls_ref, lss[g - 1], d, tm)

    dil = [_dil_spec(d, tm) for d in DILATIONS]
    gcol = pl.BlockSpec((tm, W), lambda i: (i, P_CGATE))
    res = pl.pallas_call(
        body, grid=(S // tm,),
        in_specs=[pl.BlockSpec((None, tm, W), lambda i: (2, i, 0)), gcol, dil[0], dil[0], ANY],
        out_specs=dil + dil + dil[1:] + [gcol],
        out_shape=([_sds((4 * d, S // d, HD), BF16) for d in DILATIONS] + [_sds((4 * d, S // d, HD), F32) for d in DILATIONS]
                   + [_sds((4 * d, S // d, HD), F32) for d in DILATIONS[1:]] + [_sds((S, DIN), BF16)]),
        scratch_shapes=[pltpu.VMEM((4, tm, HD), F32)] * 2,
        input_output_aliases={4: 8}, compiler_params=_cp("parallel"), name="att_bwd_pre")(dy_all, proj, om, lse, dproj)
    return res[0:3], res[3:6], [lse] + list(res[6:8]), res[8]


def _att_bwd(q, k, v, do, lse, delta, g):
    d = DILATIONS[g]
    nres, njb = ATT_STEP[g]
    grid, own, prev, nxt = _att_specs(g)

    def body(qa_ref, qn_ref, kp_ref, ko_ref, vp_ref, vo_ref, doa_ref, don_ref, la_ref, ln_ref, da_ref, dn_ref,
             dq_ref, dk_ref, dv_ref):
        jb = pl.program_id(1)
        m_prev, m_own = _att_masks()
        has_prev = jnp.where(jb > 0, 1.0, 0.0)
        has_next = jnp.where(jb < grid[1] - 1, 1.0, 0.0)

        blocks = [(a, jj) for jj in range(njb) for a in range(4 * nres)]
        for g0 in range(0, len(blocks), ATT_GROUP_BWD):
            grp = blocks[g0:g0 + ATT_GROUP_BWD]
            pairs = []
            for a, jj in grp:
                rs_ = slice(jj * HD, (jj + 1) * HD)
                qa, ko, vo = qa_ref[a, rs_, :], ko_ref[a, rs_, :], vo_ref[a, rs_, :]
                doa, lsa, dla = doa_ref[a, rs_, :], la_ref[a, rs_, :], da_ref[a, rs_, :]
                if jj == 0:
                    kp, vp, fp = kp_ref[a], vp_ref[a], has_prev
                else:
                    kp, vp, fp = ko_ref[a, (jj - 1) * HD:jj * HD, :], vo_ref[a, (jj - 1) * HD:jj * HD, :], None
                if jj == njb - 1:
                    qn, don, lsn, dln, fn = qn_ref[a], don_ref[a], ln_ref[a], dn_ref[a], has_next
                else:
                    ns = slice((jj + 1) * HD, (jj + 2) * HD)
                    qn, don, lsn, dln, fn = qa_ref[a, ns, :], doa_ref[a, ns, :], la_ref[a, ns, :], da_ref[a, ns, :], None
                pairs += [(qa, ko, vo, doa, lsa, dla, m_own, None), (qa, kp, vp, doa, lsa, dla, m_prev, fp),
                          (qn, ko, vo, don, lsn, dln, m_prev, fn)]
            s = [_dot_nt(q_, k_) for q_, k_, *_ in pairs]
            dp = [_dot_nt(do_, v_) for _, _, v_, do_, *_ in pairs]
            p = []
            for s_, (_, _, _, _, ls_, _, mask, flag) in zip(s, pairs):
                p_ = jnp.where(mask, jnp.exp(s_ * SCALE - ls_), 0.0)
                p.append(p_ if flag is None else p_ * flag)
            ds = [(p_ * (dp_ - pr[5]) * SCALE).astype(BF16) for p_, dp_, pr in zip(p, dp, pairs)]
            pb = [p_.astype(BF16) for p_ in p]
            for i, (a, jj) in enumerate(grp):
                rs_ = slice(jj * HD, (jj + 1) * HD)
                own, prv, nx = pairs[3 * i], pairs[3 * i + 1], pairs[3 * i + 2]
                dq_ref[a, rs_, :] = _dot(ds[3 * i], own[1]) + _dot(ds[3 * i + 1], prv[1])
                dk_ref[a, rs_, :] = _dot_tn(ds[3 * i], own[0]) + _dot_tn(ds[3 * i + 2], nx[0])
                dv_ref[a, rs_, :] = _dot_tn(pb[3 * i], own[3]) + _dot_tn(pb[3 * i + 2], nx[3])

    return pl.pallas_call(
        body, grid=grid, in_specs=[own, nxt, prev, own, prev, own, own, nxt, own, nxt, own, nxt],
        out_specs=[own, own, own], out_shape=[_sds((4 * d, S // d, HD), F32)] * 3,
        compiler_params=_cp("parallel", "parallel"), name="att_bwd")(q, q, k, k, v, v, do, do, lse, lse, delta, delta)


def _att_bwd_post(dqs, dks, dvs, dproj, tm=512):
    def body(*refs):
        dq, dk, dv, dp_ref, scr = refs[0:3], refs[3:6], refs[6:9], refs[10], refs[11]
        for g in range(3):
            _undilate(dq[g], scr, DILATIONS[g], tm)
            for h in range(4):
                dp_ref[:, g * W + h * HD:g * W + (h + 1) * HD] = scr[h].astype(BF16)
        for j, parts in enumerate((dk, dv)):
            acc = None
            for g in range(3):
                _undilate(parts[g], scr, DILATIONS[g], tm)
                vals = [scr[h] for h in range(4)]
                acc = vals if acc is None else [x + y for x, y in zip(acc, vals)]
            for h in range(4):
                dp_ref[:, (3 + j) * W + h * HD:(3 + j) * W + (h + 1) * HD] = acc[h].astype(BF16)

    dil = [_dil_spec(d, tm) for d in DILATIONS]
    return pl.pallas_call(
        body, grid=(S // tm,), in_specs=dil * 3 + [ANY],
        out_specs=pl.BlockSpec((tm, 5 * W), lambda i: (i, 1)), out_shape=_sds((S, DIN), BF16),
        scratch_shapes=[pltpu.VMEM((4, tm, HD), F32)],
        input_output_aliases={9: 0}, compiler_params=_cp("parallel"), name="att_bwd_post")(*dqs, *dks, *dvs, dproj)


def _mem_kv_fwd(mem_n, wkv):
    m = mem_n.shape[0]

    def body(a_ref, w_ref, o_ref):
        o_ref[...] = _dot(a_ref[...], w_ref[...])

    return pl.pallas_call(body, out_shape=_sds((m, 2 * W), F32), compiler_params=_cp(), name="mem_kv_fwd")(mem_n, wkv)


def _mem_softmax(q, k):
    s = _dot_nt(q, k) * SCALE
    e = jnp.exp(s - jnp.max(s, axis=-1, keepdims=True))
    return e * (1.0 / jnp.sum(e, axis=-1, keepdims=True))


def _mem_fwd(proj, kv, y_all, tm=512):
    m = kv.shape[0]

    def body(q_ref, gt_ref, kv_ref, y_in, y_ref):
        gt = gt_ref[...]
        sl = gt * _sigmoid(gt)
        for h in range(4):
            cs = slice(h * HD, (h + 1) * HD)
            p = _mem_softmax(q_ref[:, cs].astype(BF16), kv_ref[:, cs].astype(BF16))
            o = _dot(p.astype(BF16), kv_ref[:, W + h * HD:W + (h + 1) * HD].astype(BF16))
            y_ref[:, cs] = (o * sl[:, cs]).astype(BF16)

    return pl.pallas_call(
        body, grid=(S // tm,),
        in_specs=[pl.BlockSpec((tm, W), lambda i: (i, P_MQ)), pl.BlockSpec((tm, W), lambda i: (i, P_MGATE)),
                  pl.BlockSpec((m, 2 * W), lambda i: (0, 0)), ANY],
        out_specs=pl.BlockSpec((None, tm, W), lambda i: (3, i, 0)), out_shape=_sds((4, S, W), BF16),
        input_output_aliases={3: 0}, compiler_params=_cp("parallel"), name="mem_fwd")(proj, proj, kv, y_all)


def _mem_bwd(proj, kv, dy_all, dproj, tm=512):
    m = kv.shape[0]
    ni = S // tm

    def body(q_ref, gt_ref, kv_ref, dy_ref, dp_in, dp_ref, dkv_ref, obuf, osem):
        i = pl.program_id(0)
        gt = gt_ref[...]
        sg = _sigmoid(gt)
        sl = gt * sg
        dsl = sg * (1.0 + gt * (1.0 - sg))
        dyv = dy_ref[...]

        def fill(slot):
            for h in range(4):
                cs = slice(h * HD, (h + 1) * HD)
                vs = slice(W + h * HD, W + (h + 1) * HD)
                q = q_ref[:, cs].astype(BF16)
                k = kv_ref[:, cs].astype(BF16)
                v = kv_ref[:, vs].astype(BF16)
                p = _mem_softmax(q, k)
                pb = p.astype(BF16)
                o = _dot(pb, v)
                do = dyv[:, cs] * sl[:, cs]
                dob = do.astype(BF16)
                dp = _dot_nt(dob, v)
                dsb = (p * (dp - jnp.sum(dp * p, axis=-1, keepdims=True)) * SCALE).astype(BF16)
                slot[:, cs] = _dot(dsb, k).astype(BF16)
                slot[:, vs] = (dyv[:, cs] * o * dsl[:, cs]).astype(BF16)
                dk = _dot_tn(dsb, q)
                dv = _dot_tn(pb, dob)

                @pl.when(i == 0)
                def _():
                    dkv_ref[:, cs] = dk
                    dkv_ref[:, vs] = dv

                @pl.when(i > 0)
                def _():
                    dkv_ref[:, cs] += dk
                    dkv_ref[:, vs] += dv

        _tile_put(obuf, osem, lambda st: dp_ref.at[pl.ds(st * tm, tm), pl.ds(P_MQ * W, 2 * W)], i, ni, fill)

    return pl.pallas_call(
        body, grid=(ni,),
        in_specs=[pl.BlockSpec((tm, W), lambda i: (i, P_MQ)), pl.BlockSpec((tm, W), lambda i: (i, P_MGATE)),
                  pl.BlockSpec((m, 2 * W), lambda i: (0, 0)), pl.BlockSpec((None, tm, W), lambda i: (3, i, 0)), ANY],
        out_specs=[ANY, pl.BlockSpec((m, 2 * W), lambda i: (0, 0))],
        out_shape=[_sds((S, DIN), BF16), _sds((m, 2 * W), F32)],
        scratch_shapes=[pltpu.VMEM((2, tm, 2 * W), BF16), pltpu.SemaphoreType.DMA((2,))],
        input_output_aliases={4: 0}, compiler_params=_cp("arbitrary"), name="mem_bwd")(proj, proj, kv, dy_all, dproj)


def _mem_kv_bwd(mem, g, mem_n, wkv, dkv):
    m = mem.shape[0]

    def body(x_ref, g_ref, a_ref, w_ref, d_ref, dw_ref, dg_ref):
        db = d_ref[...].astype(BF16)
        dw_ref[...] = _dot_tn(a_ref[...], db).astype(BF16)
        dn = _dot_nt(db, w_ref[...])
        xv = x_ref[...]
        xh = xv * lax.rsqrt(jnp.mean(xv * xv, axis=-1, keepdims=True) + EPS)
        dg_ref[...] = jnp.sum(dn * xh, axis=0, keepdims=True)

    return pl.pallas_call(
        body, out_shape=[_sds((D, 2 * W), BF16), _sds((1, D), F32)], compiler_params=_cp(), name="mem_kv_bwd")(mem, g, mem_n, wkv, dkv)


def _layer_fwd(x, mem, p, wg):
    win, wkv, wbr, wo = wg
    h = _rms_fwd(x, p["norm_g"], 512)
    proj = _proj_fwd(h, win)
    y_all = lax.empty((4, S, W), BF16)
    y_all = _gmlp_fwd(proj, p["gm_ln_g"], p["gm_ln_b"], p["gm_ws"], p["gm_bias"], y_all)
    y_all = _pool_fwd(proj, p["pool_w"], p["pool_scale"], y_all)
    qs, ks, vs = _att_prep(proj)
    os_, ls_ = zip(*[_att_fwd(qs[g], ks[g], vs[g], g) for g in range(3)])
    y_all, om, lse = _att_mix(os_, ls_, proj, y_all)
    mem_n = _rms_fwd(mem, p["mem_norm_g"], mem.shape[0])
    kv = _mem_kv_fwd(mem_n, wkv)
    y_all = _mem_fwd(proj, kv, y_all)
    z = _merge_fwd(y_all, wbr, proj)
    x_new = _out_fwd(z, wo, x)
    return x_new, dict(x=x, h=h, proj=proj, y_all=y_all, om=om, lse=lse, mem_n=mem_n, kv=kv, z=z, qkv=(qs, ks, vs))


GRAD_PARTS = ((2, D // 2, CW), (2, D // 8, 2 * W), (2, 2 * W, D // NCHIP), (2, D // 8, D))
SUM_TILE = (64, 128, 256, 128)
PLACE_TILE = (256, 256, 512, 256)


def _layer_bwd(dx, mem, p, wg, sv, exchange):
    win, wkv, wbr, wo = wg
    proj = sv["proj"]
    dz, d_wo = _out_bwd(dx, sv["z"], wo)
    dproj = lax.empty((S, DIN), BF16)
    dproj, dy_all, d_wbr = _merge_bwd(dz, sv["y_all"], wbr, proj, dproj)
    dproj, d_ws, d_bs, d_lg, d_lb = _gmlp_bwd(proj, dy_all, p["gm_ln_g"], p["gm_ln_b"], p["gm_ws"], p["gm_bias"], dproj)
    dproj, d_pw, d_sc = _pool_bwd(proj, dy_all, p["pool_w"], p["pool_scale"], dproj)
    dos, dls, lss, dproj = _att_bwd_pre(dy_all, proj, sv["om"], sv["lse"], dproj)
    qs, ks, vs = sv["qkv"]
    dqs, dks, dvs = zip(*[_att_bwd(qs[g], ks[g], vs[g], dos[g], lss[g], dls[g], g) for g in range(3)])
    dproj = _att_bwd_post(dqs, dks, dvs, dproj)
    dproj, dkv = _mem_bwd(proj, sv["kv"], dy_all, dproj)
    d_wkv, d_mg = _mem_kv_bwd(mem, p["mem_norm_g"], sv["mem_n"], wkv, dkv)
    d_win = _proj_bwd_w(sv["h"], dproj)
    big = tuple(t.reshape((NCHIP,) + s) for t, s in zip((d_win, d_wkv, d_wbr, d_wo), GRAD_PARTS))
    inflight = exchange(big)
    dh = _proj_bwd_x(dproj, win, inflight[-1])
    dx_in, d_ng = _rms_bwd(dh, sv["x"], p["norm_g"], dx)
    small = dict(norm_g=d_ng, gm_ln_g=d_lg, gm_ln_b=d_lb, gm_ws=d_ws, gm_bs=d_bs[:, :, 0], pool_w=d_pw, pool_scale=d_sc, mem_norm_g=d_mg)
    return dx_in, (big,) + inflight, small


_SMALL = ("norm_g", "gm_ln_g", "gm_ln_b", "gm_ws", "gm_bs", "pool_w", "pool_scale", "mem_norm_g")


def _layer_params(l, norm_g, gm_ln_g, gm_ln_b, gm_ws, gm_bs, pool_w, pool_scale, mem_norm_g):
    return dict(norm_g=norm_g[l][None], gm_ln_g=gm_ln_g[l][None], gm_ln_b=gm_ln_b[l][None], gm_ws=gm_ws[l],
                gm_bias=jnp.broadcast_to(gm_bs[l][:, :, None], (4, HD, HD)), pool_w=pool_w[l],
                pool_scale=pool_scale[l][None], mem_norm_g=mem_norm_g[l][None])


def kernel(x, mem, norm_g, w_in, gm_ln_g, gm_ln_b, gm_ws, gm_bs, pool_w, pool_scale, mem_norm_g, w_mem_kv, w_branch, w_out, final_norm_g, loss_target, m_norm_g, m_w_in, m_gm_ln_g, m_gm_ln_b, m_gm_ws, m_gm_bs, m_pool_w, m_pool_scale, m_mem_norm_g, m_w_mem_kv, m_w_branch, m_w_out, m_final_norm_g, v_norm_g, v_w_in, v_gm_ln_g, v_gm_ln_b, v_gm_ws, v_gm_bs, v_pool_w, v_pool_scale, v_mem_norm_g, v_w_mem_kv, v_w_branch, v_w_out, v_final_norm_g):
    xs, memv, tgt = x[0], mem[0], loss_target[0]
    params = [_layer_params(l, norm_g, gm_ln_g, gm_ln_b, gm_ws, gm_bs, pool_w, pool_scale, mem_norm_g) for l in range(NL)]

    shards = (w_in.astype(BF16), w_mem_kv.astype(BF16), w_branch.astype(BF16), w_out.astype(BF16))
    lands, gsems, after = _gather_start(shards)

    saved, wgs = [], []
    for l in range(NL):
        if l == 0:
            got, relay_sems = _gather_relay(shards, lands[0], gsems[0], after)
            got = _gather_wait_relay(got, relay_sems)
        else:
            got = _gather_wait(l, shards, lands[l], gsems[l], after)
        got = [_place_own(got[k], shards[k], l, PLACE_TILE[k]) for k in range(4)]
        wgs.append((got[0], got[1].reshape(D, 2 * W), got[2], got[3].reshape(D, D)))
        xs, sv = _layer_fwd(xs, memv, params[l], wgs[l])
        saved.append(sv)
        after = xs
    dx, d_fg, ls = _loss_head(xs, tgt, final_norm_g[None])
    loss = lax.psum(ls[0, 0], ("x", "y", "c"))

    flight, small = [None] * NL, [None] * NL
    for l in reversed(range(NL)):
        dx, flight[l], small[l] = _layer_bwd(dx, memv, params[l], wgs[l], saved[l], functools.partial(_exch_start, l))
    grad_x = dx[None]

    leaves = [jnp.stack([small[l][n] for l in range(NL)]) for n in _SMALL] + [d_fg]
    sizes = [t.size for t in leaves]
    packed = jnp.concatenate([t.reshape(-1, 128) for t in leaves], axis=0)
    rows = packed.shape[0]
    small_zone, small_sems, after = _small_start(packed)

    full = [lax.empty((NL,) + s, F32) for s in GRAD_PARTS]
    for l in reversed(range(NL)):
        parts, zones, sems, _ = flight[l]
        zones = _exch_wait(l, parts, zones, sems, after)
        for k in range(4):
            full[k] = _sum_into(full[k], parts[k], zones[k], l, SUM_TILE[k])
    full = _share_full(full)
    ws = dict(norm_g=norm_g, w_in=w_in, gm_ln_g=gm_ln_g, gm_ln_b=gm_ln_b, gm_ws=gm_ws, gm_bs=gm_bs, pool_w=pool_w,
              pool_scale=pool_scale, mem_norm_g=mem_norm_g, w_mem_kv=w_mem_kv, w_branch=w_branch, w_out=w_out,
              final_norm_g=final_norm_g)
    ms = dict(norm_g=m_norm_g, w_in=m_w_in, gm_ln_g=m_gm_ln_g, gm_ln_b=m_gm_ln_b, gm_ws=m_gm_ws, gm_bs=m_gm_bs,
              pool_w=m_pool_w, pool_scale=m_pool_scale, mem_norm_g=m_mem_norm_g, w_mem_kv=m_w_mem_kv,
              w_branch=m_w_branch, w_out=m_w_out, final_norm_g=m_final_norm_g)
    vs = dict(norm_g=v_norm_g, w_in=v_w_in, gm_ln_g=v_gm_ln_g, gm_ln_b=v_gm_ln_b, gm_ws=v_gm_ws, gm_bs=v_gm_bs,
              pool_w=v_pool_w, pool_scale=v_pool_scale, mem_norm_g=v_mem_norm_g, w_mem_kv=v_w_mem_kv,
              w_branch=v_w_branch, w_out=v_w_out, final_norm_g=v_final_norm_g)
    grads = dict(w_in=full[0].reshape(NL, D, CW), w_mem_kv=full[1].reshape(NL, D // NCHIP, 2 * W),
                 w_branch=full[2].reshape(NL, 4, W, D // NCHIP), w_out=full[3].reshape(NL, D // NCHIP, D))
    upd = {n: _adamw(ws[n], grads[n], ms[n], vs[n]) for n in ("w_in", "w_mem_kv", "w_branch", "w_out")}

    small_zone = _small_wait(packed, small_zone, small_sems, upd["w_in"][0])
    tot = _sum_small(packed, small_zone, max(t for t in range(8, 513, 8) if rows % t == 0))
    offs = [0]
    for sz in sizes:
        offs.append(offs[-1] + sz // 128)
    for i, n in enumerate(_SMALL + ("final_norm_g",)):
        grads[n] = tot[offs[i]:offs[i + 1]].reshape(ws[n].shape)
        upd[n] = _adamw(ws[n], grads[n], ms[n], vs[n])
    order = ("norm_g", "w_in", "gm_ln_g", "gm_ln_b", "gm_ws", "gm_bs", "pool_w", "pool_scale", "mem_norm_g", "w_mem_kv",
             "w_branch", "w_out", "final_norm_g")
    return (loss, grad_x, *[grads[n] for n in order], *[upd[n][0] for n in order], *[upd[n][1] for n in order],
            *[upd[n][2] for n in order])
```

```python
import functools
import math

import jax
import jax.numpy as jnp
from jax import lax
from jax.experimental import pallas as pl
from jax.experimental.pallas import tpu as pltpu

F32 = jnp.float32
BF16 = jnp.bfloat16

S = 4096
D = 1024
W = 512
DIN = 10752
NL = 4
NCHIP = 4
NDEV = 8
CW = DIN // NCHIP
TN_IN = 896
NJ = CW // TN_IN
HD = 128
EPS = 1e-6
NEG = -1e30
SCALE = HD ** -0.5
INV_SQRT2 = 1.0 / math.sqrt(2.0)
INV_SQRT2PI = 1.0 / math.sqrt(2.0 * math.pi)
POOL_WINDOWS = (2, 4, 8, 16)
DILATIONS = (1, 4, 16)
HALO = 16
NPIECE = DIN // W
P_AGATE, P_PIN, P_PGATE, P_CQ, P_CK, P_CV, P_CGATE, P_MQ, P_MGATE, P_GM = 2, 3, 4, 5, 8, 9, 10, 11, 12, 13
VMEM_LIMIT = 56 * 1024 * 1024

ADAM_LR, ADAM_B1, ADAM_B2, ADAM_EPS, ADAM_WD, ADAM_STEP = 0.001, 0.9, 0.999, 1e-08, 0.01, 10

MESH = pl.DeviceIdType.MESH
ANY = pl.BlockSpec(memory_space=pl.ANY)


def _cp(*sem):
    return pltpu.CompilerParams(dimension_semantics=sem or None, vmem_limit_bytes=VMEM_LIMIT)


def _sds(shape, dtype):
    return jax.ShapeDtypeStruct(shape, dtype)


def _sigmoid(v):
    return 1.0 / (1.0 + jnp.exp(-v))


def _dot(a, b):
    return jnp.dot(a, b, preferred_element_type=F32)


def _dot_nt(a, b):
    return lax.dot_general(a, b, (((1,), (1,)), ((), ())), preferred_element_type=F32)


def _dot_tn(a, b):
    return lax.dot_general(a, b, (((0,), (0,)), ((), ())), preferred_element_type=F32)


def _tile_put(buf, sem, dst_of, step, nsteps, fill):
    slot = step % 2

    def copy(s, st):
        return pltpu.make_async_copy(buf.at[s], dst_of(st), sem.at[s])

    @pl.when(step >= 2)
    def _():
        copy(slot, step).wait()

    fill(buf.at[slot])
    copy(slot, step).start()

    @pl.when(step == nsteps - 1)
    def _():
        if nsteps >= 2:
            copy(1 - slot, step).wait()
        copy(slot, step).wait()


def _my_pos():
    return lax.axis_index("x"), lax.axis_index("y"), lax.axis_index("c")


_CHIP_REL = ((1, 0), (0, 1), (1, 1))
_DEV_REL = tuple((dx, dy, dc) for dx in (0, 1) for dy in (0, 1) for dc in (0, 1))[1:]


HBM = pl.BlockSpec(memory_space=pltpu.HBM)
SEM = pl.BlockSpec(memory_space=pltpu.SEMAPHORE)
EFFECT = pltpu.SideEffectType.DATAFLOW_SIDE_EFFECTING
N_GATHER = 3 * 4
N_EXCH = 7 * 4


def _in_hbm(t):
    return pltpu.with_memory_space_constraint(t, pltpu.HBM)


def _gather_start(shards):
    nk = len(shards)
    lands = [pltpu.HBM((NCHIP,) + s.shape[1:], BF16) for s in shards for _ in range(NL)]

    def body(*refs):
        ins, outs = refs[:nk], refs[nk:nk + nk * NL]
        sems = refs[nk + nk * NL:nk + nk * NL + 2 * NL]
        token = refs[-1]
        x, y, c = _my_pos()
        me = 2 * x + y
        for l in range(NL):
            for r, (dx, dy) in enumerate(_CHIP_REL):
                for k in range(nk):
                    src, dst = ins[k].at[l], outs[k * NL + l].at[me]
                    if l == 0:
                        hf = pl.ds(c * (shards[k].shape[1] // 2), shards[k].shape[1] // 2)
                        src, dst = src.at[hf], dst.at[hf]
                    pltpu.make_async_remote_copy(
                        src_ref=src, dst_ref=dst, send_sem=sems[2 * l].at[r * nk + k],
                        recv_sem=sems[2 * l + 1].at[r * nk + k], device_id=(x ^ dx, y ^ dy, c), device_id_type=MESH).start()
        token[...] = jnp.zeros_like(token)

    res = pl.pallas_call(
        body, name="gather_start",
        out_shape=lands + [pltpu.SemaphoreType.DMA((N_GATHER,))] * (2 * NL) + [_sds((8, 128), F32)],
        in_specs=[HBM] * nk, out_specs=[HBM] * (nk * NL) + [SEM] * (2 * NL) + [pl.BlockSpec(memory_space=pltpu.VMEM)],
        compiler_params=pltpu.CompilerParams(has_side_effects=EFFECT))(*[_in_hbm(s) for s in shards])
    lands = [[res[k * NL + l] for k in range(nk)] for l in range(NL)]
    sems = [(res[nk * NL + 2 * l], res[nk * NL + 2 * l + 1]) for l in range(NL)]
    return lands, sems, res[-1]


def _gather_relay(shards, lands, sems, after):
    nk = len(shards)
    half = [s.shape[1] // 2 for s in shards]

    def body(*refs):
        ins, land = refs[:nk], refs[nk:2 * nk]
        send, recv = refs[2 * nk], refs[2 * nk + 1]
        send2, recv2 = refs[3 * nk + 3], refs[3 * nk + 4]
        x, y, c = _my_pos()
        for r, (dx, dy) in enumerate(_CHIP_REL):
            cx, cy = x ^ dx, y ^ dy
            for k in range(nk):
                hf = pl.ds(c * half[k], half[k])
                got = land[k].at[2 * cx + cy].at[hf]
                cp = pltpu.make_async_remote_copy(
                    src_ref=ins[k].at[0].at[hf], dst_ref=got, send_sem=send.at[r * nk + k],
                    recv_sem=recv.at[r * nk + k], device_id=(cx, cy, c), device_id_type=MESH)
                cp.wait_send()
                cp.wait_recv()
                pltpu.make_async_remote_copy(
                    src_ref=got, dst_ref=got, send_sem=send2.at[r * nk + k], recv_sem=recv2.at[r * nk + k],
                    device_id=(x, y, 1 - c), device_id_type=MESH).start()

    res = pl.pallas_call(
        body, name="gather_relay",
        out_shape=[pltpu.HBM(t.shape, t.dtype) for t in lands] + [pltpu.SemaphoreType.DMA((N_GATHER,))] * 2,
        in_specs=[ANY] * nk + [HBM] * nk + [SEM, SEM, ANY], out_specs=[HBM] * nk + [SEM, SEM],
        input_output_aliases={nk + k: k for k in range(nk)},
        compiler_params=pltpu.CompilerParams(has_side_effects=EFFECT))(*shards, *lands, *sems, after)
    return res[:nk], (res[nk], res[nk + 1])


def _gather_wait_relay(lands, sems):
    nk = len(lands)
    half = [t.shape[1] // 2 for t in lands]

    def body(*refs):
        land = refs[:nk]
        send, recv = refs[nk], refs[nk + 1]
        x, y, c = _my_pos()
        for r, (dx, dy) in enumerate(_CHIP_REL):
            chip = 2 * (x ^ dx) + (y ^ dy)
            for k in range(nk):
                mine = land[k].at[chip].at[pl.ds(c * half[k], half[k])]
                theirs = land[k].at[chip].at[pl.ds((1 - c) * half[k], half[k])]
                cp = pltpu.make_async_remote_copy(
                    src_ref=mine, dst_ref=theirs, send_sem=send.at[r * nk + k], recv_sem=recv.at[r * nk + k],
                    device_id=(x, y, 1 - c), device_id_type=MESH)
                cp.wait_send()
                cp.wait_recv()

    return pl.pallas_call(
        body, name="gather_wait_relay", out_shape=[pltpu.HBM(t.shape, t.dtype) for t in lands],
        in_specs=[HBM] * nk + [SEM, SEM], out_specs=[HBM] * nk, input_output_aliases={k: k for k in range(nk)},
        compiler_params=pltpu.CompilerParams(has_side_effects=EFFECT))(*lands, *sems)


def _gather_wait(l, shards, lands, sems, after):
    nk = len(shards)

    def body(*refs):
        ins, land = refs[:nk], refs[nk:2 * nk]
        send, recv = refs[2 * nk], refs[2 * nk + 1]
        x, y, c = _my_pos()
        for r, (dx, dy) in enumerate(_CHIP_REL):
            cx, cy = x ^ dx, y ^ dy
            for k in range(nk):
                cp = pltpu.make_async_remote_copy(
                    src_ref=ins[k].at[l], dst_ref=land[k].at[2 * cx + cy], send_sem=send.at[r * nk + k],
                    recv_sem=recv.at[r * nk + k], device_id=(cx, cy, c), device_id_type=MESH)
                cp.wait_send()
                cp.wait_recv()

    return pl.pallas_call(
        body, name=f"gather_wait_{l}", out_shape=[pltpu.HBM(t.shape, t.dtype) for t in lands],
        in_specs=[ANY] * nk + [HBM] * nk + [SEM, SEM, ANY], out_specs=[HBM] * nk,
        input_output_aliases={nk + k: k for k in range(nk)},
        compiler_params=pltpu.CompilerParams(has_side_effects=EFFECT))(*shards, *lands, *sems, after)


def _place_own(land, shard, l, tr):
    _, rows, cols = shard.shape[0], shard.shape[-2], shard.shape[-1]
    lead = shard.shape[1:-2]
    nlead = math.prod(lead)
    sh = shard.reshape((NL, nlead, rows, cols))
    ld = land.reshape((NCHIP, nlead, rows, cols))
    me = (2 * lax.axis_index("x") + lax.axis_index("y")).astype(jnp.int32).reshape(1)

    def body(me_ref, s_ref, l_in, o_ref):
        o_ref[...] = s_ref[...]

    out = pl.pallas_call(
        body,
        grid_spec=pltpu.PrefetchScalarGridSpec(
            num_scalar_prefetch=1, grid=(nlead, rows // tr),
            in_specs=[pl.BlockSpec((None, None, tr, cols), lambda b, i, me_ref: (l, b, i, 0)), ANY],
            out_specs=pl.BlockSpec((None, None, tr, cols), lambda b, i, me_ref: (me_ref[0], b, i, 0))),
        out_shape=_sds(ld.shape, BF16), input_output_aliases={2: 0},
        compiler_params=_cp("parallel", "parallel"), name="place_own")(me, sh, ld)
    return out.reshape(land.shape)


def _exch_start(l, parts):
    nk = len(parts)

    def body(*refs):
        ins, outs = refs[:nk], refs[nk:2 * nk]
        send, recv, token = refs[2 * nk:]
        x, y, c = _my_pos()
        for r, (dx, dy, dc) in enumerate(_DEV_REL):
            px, py, pc = x ^ dx, y ^ dy, c ^ dc
            for k in range(nk):
                pltpu.make_async_remote_copy(
                    src_ref=ins[k].at[2 * px + py, pc], dst_ref=outs[k].at[r], send_sem=send.at[r * nk + k],
                    recv_sem=recv.at[r * nk + k], device_id=(px, py, pc), device_id_type=MESH).start()
        token[...] = jnp.zeros_like(token)

    res = pl.pallas_call(
        body, name=f"exch_start_{l}",
        out_shape=[pltpu.HBM((7,) + p.shape[2:], BF16) for p in parts] + [pltpu.SemaphoreType.DMA((N_EXCH,))] * 2 + [_sds((8, 128), F32)],
        in_specs=[HBM] * nk, out_specs=[HBM] * nk + [SEM, SEM, pl.BlockSpec(memory_space=pltpu.VMEM)],
        compiler_params=pltpu.CompilerParams(has_side_effects=EFFECT))(*[_in_hbm(p) for p in parts])
    return res[:nk], (res[nk], res[nk + 1]), res[-1]


def _exch_wait(l, parts, lands, sems, after):
    nk = len(parts)

    def body(*refs):
        ins, land = refs[:nk], refs[nk:2 * nk]
        send, recv = refs[2 * nk], refs[2 * nk + 1]
        x, y, c = _my_pos()
        for r, (dx, dy, dc) in enumerate(_DEV_REL):
            px, py, pc = x ^ dx, y ^ dy, c ^ dc
            for k in range(nk):
                cp = pltpu.make_async_remote_copy(
                    src_ref=ins[k].at[2 * px + py, pc], dst_ref=land[k].at[r], send_sem=send.at[r * nk + k],
                    recv_sem=recv.at[r * nk + k], device_id=(px, py, pc), device_id_type=MESH)
                cp.wait_send()
                cp.wait_recv()

    return pl.pallas_call(
        body, name=f"exch_wait_{l}", out_shape=[pltpu.HBM(t.shape, t.dtype) for t in lands],
        in_specs=[ANY] * nk + [HBM] * nk + [SEM, SEM, ANY], out_specs=[HBM] * nk,
        input_output_aliases={nk + k: k for k in range(nk)},
        compiler_params=pltpu.CompilerParams(has_side_effects=EFFECT))(*parts, *lands, *sems, after)


def _chip_half():
    x, y, c = _my_pos()
    return jnp.stack([2 * x + y, c]).astype(jnp.int32)


def _sum_half(part, land, tr):
    _, _, r2, cols = part.shape

    def body(pos_ref, p_ref, r_ref, o_ref):
        acc = p_ref[...].astype(F32)
        for r in range(7):
            acc = acc + r_ref[r].astype(F32)
        o_ref[...] = acc

    return pl.pallas_call(
        body,
        grid_spec=pltpu.PrefetchScalarGridSpec(
            num_scalar_prefetch=1, grid=(r2 // tr,),
            in_specs=[pl.BlockSpec((None, None, tr, cols), lambda i, pos: (pos[0], pos[1], i, 0)),
                      pl.BlockSpec((7, tr, cols), lambda i, pos: (0, i, 0))],
            out_specs=pl.BlockSpec((None, tr, cols), lambda i, pos: (pos[1], i, 0))),
        out_shape=_sds((2, r2, cols), F32), compiler_params=_cp("parallel"), name="sum_half")(_chip_half(), part, land)


def _share_full(fulls):
    nk = len(fulls)

    def body(*refs):
        ins, outs = refs[:nk], refs[nk:2 * nk]
        send, recv = refs[2 * nk:]
        x, y, c = _my_pos()

        def copy(k, hf):
            return pltpu.make_async_remote_copy(
                src_ref=ins[k].at[hf], dst_ref=outs[k].at[hf], send_sem=send.at[k], recv_sem=recv.at[k],
                device_id=(x, y, 1 - c), device_id_type=MESH)

        for k in range(nk):
            copy(k, c).start()
        for k in range(nk):
            copy(k, 1 - c).wait_recv()
        for k in range(nk):
            copy(k, c).wait_send()

    return pl.pallas_call(
        body, out_shape=[_sds(f.shape, F32) for f in fulls], in_specs=[ANY] * nk, out_specs=[ANY] * nk,
        scratch_shapes=[pltpu.SemaphoreType.DMA((nk,))] * 2, input_output_aliases={k: k for k in range(nk)},
        name="share_full")(*fulls)


def _small_start(packed):
    def body(in_ref, out_ref, send, recv, token):
        x, y, c = _my_pos()
        for r, (dx, dy, dc) in enumerate(_DEV_REL):
            pltpu.make_async_remote_copy(
                src_ref=in_ref, dst_ref=out_ref.at[r], send_sem=send.at[r], recv_sem=recv.at[r],
                device_id=(x ^ dx, y ^ dy, c ^ dc), device_id_type=MESH).start()
        token[...] = jnp.zeros_like(token)

    res = pl.pallas_call(
        body, name="small_start",
        out_shape=[pltpu.HBM((7,) + packed.shape, F32)] + [pltpu.SemaphoreType.DMA((7,))] * 2 + [_sds((8, 128), F32)],
        in_specs=[HBM], out_specs=[HBM, SEM, SEM, pl.BlockSpec(memory_space=pltpu.VMEM)],
        compiler_params=pltpu.CompilerParams(has_side_effects=EFFECT))(_in_hbm(packed))
    return res[0], (res[1], res[2]), res[3]


def _small_wait(packed, land, sems, after):
    def body(in_ref, land_ref, send, recv, after_ref, out_ref):
        x, y, c = _my_pos()
        for r, (dx, dy, dc) in enumerate(_DEV_REL):
            cp = pltpu.make_async_remote_copy(
                src_ref=in_ref, dst_ref=land_ref.at[r], send_sem=send.at[r], recv_sem=recv.at[r],
                device_id=(x ^ dx, y ^ dy, c ^ dc), device_id_type=MESH)
            cp.wait_send()
            cp.wait_recv()

    return pl.pallas_call(
        body, name="small_wait", out_shape=pltpu.HBM(land.shape, land.dtype),
        in_specs=[ANY, HBM, SEM, SEM, ANY], out_specs=HBM, input_output_aliases={1: 0},
        compiler_params=pltpu.CompilerParams(has_side_effects=EFFECT))(packed, land, *sems, after)


def _sum_small(packed, land, tr):
    rows = packed.shape[0]
    x, y, c = _my_pos()
    me = (4 * x + 2 * y + c).astype(jnp.int32).reshape(1)

    def sbody(me_ref, p_ref, r_ref, o_ref):
        me_dev = me_ref[0]
        own = p_ref[...]
        acc = None
        for s in range(NDEV):
            rel = s ^ me_dev
            v = jnp.where(rel == 0, own, r_ref[jnp.maximum(rel - 1, 0)])
            acc = v if acc is None else acc + v
        o_ref[...] = acc

    return pl.pallas_call(
        sbody,
        grid_spec=pltpu.PrefetchScalarGridSpec(
            num_scalar_prefetch=1, grid=(rows // tr,),
            in_specs=[pl.BlockSpec((tr, 128), lambda i, me_ref: (i, 0)), pl.BlockSpec((7, tr, 128), lambda i, me_ref: (0, i, 0))],
            out_specs=pl.BlockSpec((tr, 128), lambda i, me_ref: (i, 0))),
        out_shape=_sds((rows, 128), F32), compiler_params=_cp("parallel"), name="sum_small")(me, packed, land)


def _rms_fwd(x, g, tm):
    n = x.shape[0]

    def body(x_ref, g_ref, h_ref):
        xv = x_ref[...]
        r = lax.rsqrt(jnp.mean(xv * xv, axis=-1, keepdims=True) + EPS)
        h_ref[...] = (xv * r * g_ref[...]).astype(BF16)

    return pl.pallas_call(
        body, grid=(n // tm,),
        in_specs=[pl.BlockSpec((tm, D), lambda i: (i, 0)), pl.BlockSpec((1, D), lambda i: (0, 0))],
        out_specs=pl.BlockSpec((tm, D), lambda i: (i, 0)), out_shape=_sds((n, D), BF16),
        compiler_params=_cp("parallel"), name="rms_fwd")(x, g)


def _rms_bwd(dh, x, g, dres, tm=512):
    n = x.shape[0]

    def body(dh_ref, x_ref, g_ref, dr_ref, dx_ref, dg_ref):
        i = pl.program_id(0)
        xv = x_ref[...]
        r = lax.rsqrt(jnp.mean(xv * xv, axis=-1, keepdims=True) + EPS)
        xh = xv * r
        dhv = dh_ref[...]
        dxh = dhv * g_ref[...]
        dx_ref[...] = dr_ref[...] + r * (dxh - xh * jnp.mean(dxh * xh, axis=-1, keepdims=True))
        part = jnp.sum(dhv * xh, axis=0, keepdims=True)

        @pl.when(i == 0)
        def _():
            dg_ref[...] = part

        @pl.when(i > 0)
        def _():
            dg_ref[...] += part

    row = pl.BlockSpec((tm, D), lambda i: (i, 0))
    vec = pl.BlockSpec((1, D), lambda i: (0, 0))
    return pl.pallas_call(
        body, grid=(n // tm,), in_specs=[row, row, vec, row], out_specs=[row, vec],
        out_shape=[_sds((n, D), F32), _sds((1, D), F32)], compiler_params=_cp("arbitrary"), name="rms_bwd")(dh, x, g, dres)


def _loss_head(x, tgt, g, tm=512):
    def body(x_ref, t_ref, g_ref, dx_ref, dg_ref, ls_ref):
        i = pl.program_id(0)
        xv = x_ref[...]
        r = lax.rsqrt(jnp.mean(xv * xv, axis=-1, keepdims=True) + EPS)
        xh = xv * r
        gv = g_ref[...]
        diff = xh * gv - t_ref[...]
        dy = diff * (1.0 / D)
        dxh = dy * gv
        dx_ref[...] = r * (dxh - xh * jnp.mean(dxh * xh, axis=-1, keepdims=True))
        part_g = jnp.sum(dy * xh, axis=0, keepdims=True)
        part_l = jnp.sum(diff * diff, axis=0, keepdims=True)

        @pl.when(i == 0)
        def _():
            dg_ref[...] = part_g
            ls_ref[...] = part_l

        @pl.when(i > 0)
        def _():
            dg_ref[...] += part_g
            ls_ref[...] += part_l

        @pl.when(i == pl.num_programs(0) - 1)
        def _():
            tot = jnp.sum(ls_ref[...], axis=-1, keepdims=True) * (0.5 / D)
            ls_ref[...] = jnp.broadcast_to(tot, (1, D))

    row = pl.BlockSpec((tm, D), lambda i: (i, 0))
    vec = pl.BlockSpec((1, D), lambda i: (0, 0))
    return pl.pallas_call(
        body, grid=(S // tm,), in_specs=[row, row, vec], out_specs=[row, vec, vec],
        out_shape=[_sds((S, D), F32), _sds((1, D), F32), _sds((1, D), F32)],
        compiler_params=_cp("arbitrary"), name="loss_head")(x, tgt, g)


def _adamw(w, g, m, v):
    shape = w.shape
    cols = shape[-1] if w.ndim > 1 else shape[0]
    rows = w.size // cols
    w2, g2, m2, v2 = (t.reshape(rows, cols) for t in (w, g, m, v))
    tr = rows
    while tr * cols * 4 > (1 << 20) and tr % 16 == 0:
        tr //= 2
    c1 = 1.0 - ADAM_B1 ** ADAM_STEP
    c2 = 1.0 - ADAM_B2 ** ADAM_STEP

    def body(w_ref, g_ref, m_ref, v_ref, d_ref, nm_ref, nv_ref):
        gv = g_ref[...]
        mn = ADAM_B1 * m_ref[...] + (1.0 - ADAM_B1) * gv
        vn = ADAM_B2 * v_ref[...] + (1.0 - ADAM_B2) * (gv * gv)
        d_ref[...] = -ADAM_LR * ((mn / c1) / (jnp.sqrt(vn / c2) + ADAM_EPS) + ADAM_WD * w_ref[...])
        nm_ref[...] = mn
        nv_ref[...] = vn

    blk = pl.BlockSpec((tr, cols), lambda i: (i, 0))
    outs = pl.pallas_call(
        body, grid=(rows // tr,), in_specs=[blk] * 4, out_specs=[blk] * 3,
        out_shape=[_sds((rows, cols), F32)] * 3, compiler_params=_cp("parallel"), name="adamw")(w2, g2, m2, v2)
    return tuple(o.reshape(shape) for o in outs)


def _adamw_layer(l, w, g, m, v, outs, tr):
    cols = w.shape[-1]
    rows = w.size // (NL * cols)
    nb = rows // tr
    w2, m2, v2 = (t.reshape(NL * rows, cols) for t in (w, m, v))
    g2 = g.reshape(rows, cols)
    c1 = 1.0 - ADAM_B1 ** ADAM_STEP
    c2 = 1.0 - ADAM_B2 ** ADAM_STEP

    def body(w_ref, g_ref, m_ref, v_ref, d_in, nm_in, nv_in, go_in, d_ref, nm_ref, nv_ref, go_ref):
        gv = g_ref[...]
        mn = ADAM_B1 * m_ref[...] + (1.0 - ADAM_B1) * gv
        vn = ADAM_B2 * v_ref[...] + (1.0 - ADAM_B2) * (gv * gv)
        d_ref[...] = -ADAM_LR * ((mn / c1) / (jnp.sqrt(vn / c2) + ADAM_EPS) + ADAM_WD * w_ref[...])
        nm_ref[...] = mn
        nv_ref[...] = vn
        go_ref[...] = gv

    lay = pl.BlockSpec((tr, cols), lambda i: (l * nb + i, 0))
    return pl.pallas_call(
        body, grid=(nb,), in_specs=[lay, pl.BlockSpec((tr, cols), lambda i: (i, 0)), lay, lay] + [ANY] * 4,
        out_specs=[lay] * 4, out_shape=[_sds((NL * rows, cols), F32)] * 4,
        input_output_aliases={4: 0, 5: 1, 6: 2, 7: 3}, compiler_params=_cp("parallel"), name="adamw_layer")(w2, g2, m2, v2, *outs)


def _proj_fwd(h, wg, tm=512):
    def body(h_ref, w_ref, o_ref):
        o_ref[...] = _dot(h_ref[...], w_ref[...])

    return pl.pallas_call(
        body, grid=(NCHIP, S // tm),
        in_specs=[pl.BlockSpec((tm, D), lambda c, i: (i, 0)), pl.BlockSpec((None, D, CW), lambda c, i: (c, 0, 0))],
        out_specs=pl.BlockSpec((tm, CW), lambda c, i: (i, c)), out_shape=_sds((S, DIN), F32),
        compiler_params=_cp("parallel", "parallel"), name="proj_fwd")(h, wg)


def _proj_bwd_x(dproj, wg, dep, tm=2048):
    nk = NCHIP * NJ

    def body(d_ref, w_ref, dep_ref, o_ref):
        k = pl.program_id(1)
        part = _dot_nt(d_ref[...], w_ref[...])

        @pl.when(k == 0)
        def _():
            o_ref[...] = part

        @pl.when(k > 0)
        def _():
            o_ref[...] += part

    return pl.pallas_call(
        body, grid=(S // tm, nk),
        in_specs=[pl.BlockSpec((tm, TN_IN), lambda i, k: (i, k)),
                  pl.BlockSpec((None, D, TN_IN), lambda i, k: (k // NJ, 0, k % NJ)), ANY],
        out_specs=pl.BlockSpec((tm, D), lambda i, k: (i, 0)), out_shape=_sds((S, D), F32),
        compiler_params=_cp("parallel", "arbitrary"), name="proj_bwd_x")(dproj, wg, dep)


def _proj_bwd_w(h, dproj, tk=1024):
    nk = S // tk

    def body(h_ref, d_ref, o_ref, acc):
        k = pl.program_id(1)
        part = _dot_tn(h_ref[...], d_ref[...])

        @pl.when(k == 0)
        def _():
            acc[...] = part

        @pl.when(k > 0)
        def _():
            acc[...] += part

        @pl.when(k == nk - 1)
        def _():
            o_ref[...] = acc[...].astype(BF16)

    return pl.pallas_call(
        body, grid=(NCHIP, nk),
        in_specs=[pl.BlockSpec((tk, D), lambda c, k: (k, 0)), pl.BlockSpec((tk, CW), lambda c, k: (k, c))],
        out_specs=pl.BlockSpec((None, D, CW), lambda c, k: (c, 0, 0)), out_shape=_sds((NCHIP, D, CW), BF16),
        scratch_shapes=[pltpu.VMEM((D, CW), F32)],
        compiler_params=_cp("parallel", "arbitrary"), name="proj_bwd_w")(h, dproj)


def _merge_fwd(y_all, wbr, proj, tm=512):
    cb = D // NCHIP

    def body(y0, y1, y2, y3, w_ref, g0, g1, g2, g3, z_ref):
        acc = None
        for b, (y_ref, g_ref) in enumerate(zip((y0, y1, y2, y3), (g0, g1, g2, g3))):
            t = _dot(y_ref[...], w_ref[b]) * _sigmoid(g_ref[...])
            acc = t if acc is None else acc + t
        z_ref[...] = acc.astype(BF16)

    y_specs = [pl.BlockSpec((None, tm, W), functools.partial(lambda b, c, i: (b, i, 0), b)) for b in range(4)]
    g_specs = [pl.BlockSpec((tm, cb), functools.partial(lambda b, c, i: (i, (P_GM * W + b * D) // cb + c), b)) for b in range(4)]
    return pl.pallas_call(
        body, grid=(NCHIP, S // tm),
        in_specs=y_specs + [pl.BlockSpec((None, 4, W, cb), lambda c, i: (c, 0, 0, 0))] + g_specs,
        out_specs=pl.BlockSpec((tm, cb), lambda c, i: (i, c)), out_shape=_sds((S, D), BF16),
        compiler_params=_cp("parallel", "parallel"), name="merge_fwd")(y_all, y_all, y_all, y_all, wbr, proj, proj, proj, proj)


def _merge_bwd(dz, y_all, wbr, proj, dproj, tm=512):
    cb = D // NCHIP
    ni = S // tm

    def body(dz_ref, y_ref, w_ref, ga_ref, gb_ref, dp_in, dp_ref, dy_ref, dw_ref, acc, obuf, osem):
        b = pl.program_id(0)
        i = pl.program_id(1)
        yv = y_ref[...]
        dys = []

        def fill(slot):
            dy = None
            for c in range(NCHIP):
                wv = w_ref[c]
                t = _dot(yv, wv)
                g_ref = ga_ref if c < 2 else gb_ref
                g = _sigmoid(g_ref[:, (c % 2) * cb:(c % 2 + 1) * cb])
                dzc = dz_ref[:, c * cb:(c + 1) * cb]
                slot[:, c * cb:(c + 1) * cb] = (dzc * t * g * (1.0 - g)).astype(BF16)
                dt = (dzc * g).astype(BF16)
                part = _dot_nt(dt, wv)
                dy = part if dy is None else dy + part
                dwp = _dot_tn(yv, dt)

                @pl.when(i == 0)
                def _():
                    acc[c] = dwp

                @pl.when(i > 0)
                def _():
                    acc[c] += dwp
            dys.append(dy)

        _tile_put(obuf, osem, lambda st: dp_ref.at[pl.ds((st % ni) * tm, tm), pl.ds(P_GM * W + (st // ni) * D, D)],
                  b * ni + i, 4 * ni, fill)
        dy_ref[...] = dys[0]

        @pl.when(i == ni - 1)
        def _():
            dw_ref[...] = acc[...].astype(BF16)

    return pl.pallas_call(
        body, grid=(4, ni),
        in_specs=[pl.BlockSpec((tm, D), lambda b, i: (i, 0)), pl.BlockSpec((None, tm, W), lambda b, i: (b, i, 0)),
                  pl.BlockSpec((NCHIP, None, W, cb), lambda b, i: (0, b, 0, 0)),
                  pl.BlockSpec((tm, W), lambda b, i: (i, P_GM + 2 * b)), pl.BlockSpec((tm, W), lambda b, i: (i, P_GM + 2 * b + 1)), ANY],
        out_specs=[ANY, pl.BlockSpec((None, tm, W), lambda b, i: (b, i, 0)), pl.BlockSpec((NCHIP, None, W, cb), lambda b, i: (0, b, 0, 0))],
        out_shape=[_sds((S, DIN), BF16), _sds((4, S, W), F32), _sds((NCHIP, 4, W, cb), BF16)],
        scratch_shapes=[pltpu.VMEM((NCHIP, W, cb), F32), pltpu.VMEM((2, tm, D), BF16), pltpu.SemaphoreType.DMA((2,))],
        input_output_aliases={5: 0}, compiler_params=_cp("arbitrary", "arbitrary"), name="merge_bwd")(dz, y_all, wbr, proj, proj, dproj)


def _out_fwd(z, wo, x, tm=512):
    def body(z_ref, w_ref, x_ref, o_ref):
        o_ref[...] = x_ref[...] + _dot(z_ref[...], w_ref[...])

    row = pl.BlockSpec((tm, D), lambda i: (i, 0))
    return pl.pallas_call(
        body, grid=(S // tm,), in_specs=[row, pl.BlockSpec((D, D), lambda i: (0, 0)), row], out_specs=row,
        out_shape=_sds((S, D), F32), compiler_params=_cp("parallel"), name="out_fwd")(z, wo, x)


def _out_bwd(dx, z, wo, tm=512):
    ni = S // tm

    def body(dx_ref, z_ref, w_ref, dz_ref, dw_ref, acc):
        i = pl.program_id(0)
        dxb = dx_ref[...].astype(BF16)
        dz_ref[...] = _dot_nt(dxb, w_ref[...])
        part = _dot_tn(z_ref[...], dxb)

        @pl.when(i == 0)
        def _():
            acc[...] = part

        @pl.when(i > 0)
        def _():
            acc[...] += part

        @pl.when(i == ni - 1)
        def _():
            dw_ref[...] = acc[...].astype(BF16)

    row = pl.BlockSpec((tm, D), lambda i: (i, 0))
    full = pl.BlockSpec((D, D), lambda i: (0, 0))
    return pl.pallas_call(
        body, grid=(ni,), in_specs=[row, row, full], out_specs=[row, full],
        out_shape=[_sds((S, D), F32), _sds((D, D), BF16)], scratch_shapes=[pltpu.VMEM((D, D), F32)],
        compiler_params=_cp("arbitrary"), name="out_bwd")(dx, z, wo)


def _gelu_parts(a):
    cdf = 0.5 * (1.0 + lax.erf(a * INV_SQRT2))
    return a * cdf, cdf


def _ln_parts(v):
    mu = jnp.mean(v, axis=-1, keepdims=True)
    vc = v - mu
    rs = lax.rsqrt(jnp.mean(vc * vc, axis=-1, keepdims=True) + EPS)
    return vc * rs, rs


def _causal_mask():
    return lax.broadcasted_iota(jnp.int32, (HD, HD), 0) >= lax.broadcasted_iota(jnp.int32, (HD, HD), 1)


def _gmlp_fwd(proj, lg, lb, ws, bias, y_all, tm=512):
    def body(uv_ref, gt_ref, lg_ref, lb_ref, ws_ref, b_ref, y_in, y_ref):
        act, _ = _gelu_parts(uv_ref[...])
        u = act[:, :W]
        xh, _ = _ln_parts(act[:, W:])
        vn = (xh * lg_ref[...] + lb_ref[...]).astype(BF16)
        gt = gt_ref[...]
        us = u * (gt * _sigmoid(gt))
        mask = _causal_mask()
        for h in range(4):
            wm = jnp.where(mask, ws_ref[h], 0.0).astype(BF16)
            cs = slice(h * HD, (h + 1) * HD)
            for c in range(tm // HD):
                rs_ = slice(c * HD, (c + 1) * HD)
                mixed = _dot(wm, vn[rs_, cs]) + b_ref[h]
                y_ref[rs_, cs] = (us[rs_, cs] * mixed).astype(BF16)

    vec = pl.BlockSpec((1, W), lambda i: (0, 0))
    mats = pl.BlockSpec((4, HD, HD), lambda i: (0, 0, 0))
    return pl.pallas_call(
        body, grid=(S // tm,),
        in_specs=[pl.BlockSpec((tm, 2 * W), lambda i: (i, 0)), pl.BlockSpec((tm, W), lambda i: (i, P_AGATE)), vec, vec, mats, mats, ANY],
        out_specs=pl.BlockSpec((None, tm, W), lambda i: (0, i, 0)), out_shape=_sds((4, S, W), BF16),
        input_output_aliases={6: 0}, compiler_params=_cp("parallel"), name="gmlp_fwd")(proj, proj, lg, lb, ws, bias, y_all)


def _gmlp_bwd(proj, dy_all, lg, lb, ws, bias, dproj, tm=256):
    ni = S // tm

    def body(uv_ref, gt_ref, dy_ref, lg_ref, lb_ref, ws_ref, b_ref, dp_in, dp_ref, dws_ref, dbs_ref, dlg_ref, dlb_ref, mix_s, dvn_s):
        i = pl.program_id(0)
        a0 = uv_ref[...]
        act, cdf = _gelu_parts(a0)
        u = act[:, :W]
        xh, rs = _ln_parts(act[:, W:])
        lgv = lg_ref[...]
        vn = (xh * lgv + lb_ref[...]).astype(BF16)
        mask = _causal_mask()
        wms = [jnp.where(mask, ws_ref[h], 0.0).astype(BF16) for h in range(4)]
        for h in range(4):
            cs = slice(h * HD, (h + 1) * HD)
            for c in range(tm // HD):
                rs_ = slice(c * HD, (c + 1) * HD)
                mix_s[rs_, cs] = _dot(wms[h], vn[rs_, cs]) + b_ref[h]
        mixed = mix_s[...]
        gt = gt_ref[...]
        sg = _sigmoid(gt)
        sl = gt * sg
        dyv = dy_ref[...]
        dum = dyv * sl
        dgate = dyv * (u * mixed) * (sg * (1.0 + gt * (1.0 - sg)))
        du = dum * mixed
        dmix = dum * u
        dmb = dmix.astype(BF16)
        for h in range(4):
            cs = slice(h * HD, (h + 1) * HD)
            dw = None
            db = None
            for c in range(tm // HD):
                rs_ = slice(c * HD, (c + 1) * HD)
                dvn_s[rs_, cs] = _dot_tn(wms[h], dmb[rs_, cs])
                pw = _dot_nt(dmb[rs_, cs], vn[rs_, cs])
                dw = pw if dw is None else dw + pw
                db = dmix[rs_, cs] if db is None else db + dmix[rs_, cs]

            @pl.when(i == 0)
            def _():
                dws_ref[h] = dw
                dbs_ref[h] = db

            @pl.when(i > 0)
            def _():
                dws_ref[h] += dw
                dbs_ref[h] += db

            @pl.when(i == ni - 1)
            def _():
                dws_ref[h] = jnp.where(mask, dws_ref[h], 0.0)
                dbs_ref[h] = jnp.broadcast_to(jnp.sum(dbs_ref[h], axis=1, keepdims=True), (HD, HD))
        dvn = dvn_s[...]
        plg = jnp.sum(dvn * xh, axis=0, keepdims=True)
        plb = jnp.sum(dvn, axis=0, keepdims=True)

        @pl.when(i == 0)
        def _():
            dlg_ref[...] = plg
            dlb_ref[...] = plb

        @pl.when(i > 0)
        def _():
            dlg_ref[...] += plg
            dlb_ref[...] += plb

        dxh = dvn * lgv
        dv = rs * (dxh - jnp.mean(dxh, axis=-1, keepdims=True) - xh * jnp.mean(dxh * xh, axis=-1, keepdims=True))
        gp = cdf + a0 * (jnp.exp(-0.5 * a0 * a0) * INV_SQRT2PI)
        dp_ref[:, :W] = (du * gp[:, :W]).astype(BF16)
        dp_ref[:, W:2 * W] = (dv * gp[:, W:]).astype(BF16)
        dp_ref[:, 2 * W:] = dgate.astype(BF16)

    vec = pl.BlockSpec((1, W), lambda i: (0, 0))
    mats = pl.BlockSpec((4, HD, HD), lambda i: (0, 0, 0))
    return pl.pallas_call(
        body, grid=(ni,),
        in_specs=[pl.BlockSpec((tm, 2 * W), lambda i: (i, 0)), pl.BlockSpec((tm, W), lambda i: (i, P_AGATE)),
                  pl.BlockSpec((None, tm, W), lambda i: (0, i, 0)), vec, vec, mats, mats, ANY],
        out_specs=[pl.BlockSpec((tm, 3 * W), lambda i: (i, 0)), mats, mats, vec, vec],
        out_shape=[_sds((S, DIN), BF16), _sds((4, HD, HD), F32), _sds((4, HD, HD), F32), _sds((1, W), F32), _sds((1, W), F32)],
        scratch_shapes=[pltpu.VMEM((tm, W), F32), pltpu.VMEM((tm, W), F32)],
        input_output_aliases={7: 0}, compiler_params=_cp("arbitrary"), name="gmlp_bwd")(proj, proj, dy_all, lg, lb, ws, bias, dproj)


def _pool_diff(p, halo, row0, tm):
    xx = jnp.concatenate([halo, p], axis=0)
    t1 = (row0 + 1 + lax.broadcasted_iota(jnp.int32, (tm, 1), 0)).astype(F32)
    out = []
    for g, win in enumerate(POOL_WINDOWS):
        s = xx[:, g * HD:(g + 1) * HD]
        sh = 1
        while sh < win:
            s = s + pltpu.roll(s, sh, 0)
            sh *= 2
        out.append(s[HALO:] / jnp.minimum(t1, float(win)) - p[:, g * HD:(g + 1) * HD])
    return out


def _pool_fwd(proj, pw, sc, y_all, tm=512):
    rb = tm // HALO

    def body(p_ref, h_ref, gt_ref, pw_ref, sc_ref, y_in, y_ref):
        i = pl.program_id(0)
        halo = jnp.where(i > 0, h_ref[...], 0.0)
        ds = _pool_diff(p_ref[...], halo, i * tm, tm)
        gt = gt_ref[...]
        sl = gt * _sigmoid(gt)
        for g in range(4):
            cs = slice(g * HD, (g + 1) * HD)
            lin = _dot(ds[g].astype(BF16), pw_ref[g].astype(BF16))
            y_ref[:, cs] = (lin * sc_ref[:, cs] * sl[:, cs]).astype(BF16)

    return pl.pallas_call(
        body, grid=(S // tm,),
        in_specs=[pl.BlockSpec((tm, W), lambda i: (i, P_PIN)),
                  pl.BlockSpec((HALO, W), lambda i: (jnp.maximum(i * rb - 1, 0), P_PIN)),
                  pl.BlockSpec((tm, W), lambda i: (i, P_PGATE)),
                  pl.BlockSpec((4, HD, HD), lambda i: (0, 0, 0)), pl.BlockSpec((1, W), lambda i: (0, 0)), ANY],
        out_specs=pl.BlockSpec((None, tm, W), lambda i: (1, i, 0)), out_shape=_sds((4, S, W), BF16),
        input_output_aliases={5: 0}, compiler_params=_cp("parallel"), name="pool_fwd")(proj, proj, proj, pw, sc, y_all)


def _pool_bwd(proj, dy_all, pw, sc, dproj, tm=256):
    ni = S // tm
    rb = tm // HALO
    last_rb = S // HALO - 1
    rx = tm + HALO

    def body(p_ref, h_ref, gt_ref, gh_ref, dy_ref, dyh_ref, pw_ref, sc_ref, dp_in, dp_ref, dpw_ref, dsc_ref, obuf, osem):
        i = pl.program_id(0)
        halo = jnp.where(i > 0, h_ref[...], 0.0)
        ds = _pool_diff(p_ref[...], halo, i * tm, tm)
        nxt = i < ni - 1
        gx = jnp.concatenate([gt_ref[...], gh_ref[...]], axis=0)
        dyx = jnp.concatenate([dy_ref[...], jnp.where(nxt, dyh_ref[...], 0.0)], axis=0)
        sgx = _sigmoid(gx)
        slx = gx * sgx
        scv = sc_ref[...]
        dlinx = dyx * slx * scv
        t1 = (i * tm + 1 + lax.broadcasted_iota(jnp.int32, (rx, 1), 0)).astype(F32)
        gt, sg, sl, dyv = gx[:tm], sgx[:tm], slx[:tm], dyx[:tm]
        dsl = sg * (1.0 + gt * (1.0 - sg))

        def fill(slot):
            for g, win in enumerate(POOL_WINDOWS):
                cs = slice(g * HD, (g + 1) * HD)
                wv = pw_ref[g].astype(BF16)
                dlb = dlinx[:, cs].astype(BF16)
                ddx = _dot_nt(dlb, wv)
                f = ddx / jnp.minimum(t1, float(win))
                sh = 1
                while sh < win:
                    f = f + pltpu.roll(f, rx - sh, 0)
                    sh *= 2
                slot[:, cs] = (f[:tm] - ddx[:tm]).astype(BF16)
                db = ds[g].astype(BF16)
                lin = _dot(db, wv)
                slot[:, W + g * HD:W + (g + 1) * HD] = (dyv[:, cs] * lin * scv[:, cs] * dsl[:, cs]).astype(BF16)
                psc = jnp.sum(dyv[:, cs] * sl[:, cs] * lin, axis=0, keepdims=True)
                pwg = _dot_tn(db, dlb[:tm])

                @pl.when(i == 0)
                def _():
                    dpw_ref[g] = pwg
                    dsc_ref[:, cs] = psc

                @pl.when(i > 0)
                def _():
                    dpw_ref[g] += pwg
                    dsc_ref[:, cs] += psc

        _tile_put(obuf, osem, lambda st: dp_ref.at[pl.ds(st * tm, tm), pl.ds(P_PIN * W, 2 * W)], i, ni, fill)

    mats = pl.BlockSpec((4, HD, HD), lambda i: (0, 0, 0))
    vec = pl.BlockSpec((1, W), lambda i: (0, 0))
    return pl.pallas_call(
        body, grid=(ni,),
        in_specs=[pl.BlockSpec((tm, W), lambda i: (i, P_PIN)),
                  pl.BlockSpec((HALO, W), lambda i: (jnp.maximum(i * rb - 1, 0), P_PIN)),
                  pl.BlockSpec((tm, W), lambda i: (i, P_PGATE)),
                  pl.BlockSpec((HALO, W), lambda i: (jnp.minimum((i + 1) * rb, last_rb), P_PGATE)),
                  pl.BlockSpec((None, tm, W), lambda i: (1, i, 0)),
                  pl.BlockSpec((None, HALO, W), lambda i: (1, jnp.minimum((i + 1) * rb, last_rb), 0)),
                  mats, vec, ANY],
        out_specs=[ANY, mats, vec],
        out_shape=[_sds((S, DIN), BF16), _sds((4, HD, HD), F32), _sds((1, W), F32)],
        scratch_shapes=[pltpu.VMEM((2, tm, 2 * W), BF16), pltpu.SemaphoreType.DMA((2,))],
        input_output_aliases={8: 0}, compiler_params=_cp("arbitrary"), name="pool_bwd")(proj, proj, proj, proj, dy_all, dy_all, pw, sc, dproj)


ATT_STEP = ((1, 4), (4, 1), (4, 1))
ATT_GROUP = 16
ATT_GROUP_BWD = 8


def _att_band():
    qi = lax.broadcasted_iota(jnp.int32, (HD, 2 * HD), 0)
    kj = lax.broadcasted_iota(jnp.int32, (HD, 2 * HD), 1)
    return jnp.logical_and(kj >= qi, kj <= qi + HD), kj < HD


def _att_keys(kp_ref, ko_ref, vp_ref, vo_ref, a, jj):
    if jj == 0:
        return (jnp.concatenate([kp_ref[a], ko_ref[a, :HD, :]], axis=0), jnp.concatenate([vp_ref[a], vo_ref[a, :HD, :]], axis=0))
    return ko_ref[a, (jj - 1) * HD:(jj + 1) * HD, :], vo_ref[a, (jj - 1) * HD:(jj + 1) * HD, :]


def _dilate(src, dst, d, rows, cast=None):
    for r in range(d):
        for h in range(4):
            v = src.at[h][pl.ds(r, rows // d, stride=d), :] if d > 1 else src[h]
            dst[r * 4 + h] = v if cast is None else v.astype(cast)


def _undilate(src, dst, d, rows):
    for r in range(d):
        for h in range(4):
            if d > 1:
                dst.at[h][pl.ds(r, rows // d, stride=d), :] = src[r * 4 + h].astype(F32)
            else:
                dst[h] = src[h].astype(F32)


def _dil_spec(d, tm):
    return pl.BlockSpec((4 * d, tm // d, HD), lambda i: (0, i, 0))


def _att_prep(proj, tm=512):
    def body(q0, q1, q2, k_ref, v_ref, *rest):
        outs, scr = rest[:9], rest[9]
        for j, (src, dsts) in enumerate(((q0, ((0, outs[0]),)), (q1, ((1, outs[1]),)), (q2, ((2, outs[2]),)),
                                         (k_ref, tuple((g, outs[3 + g]) for g in range(3))),
                                         (v_ref, tuple((g, outs[6 + g]) for g in range(3))))):
            for h in range(4):
                scr[j, h] = src[:, h * HD:(h + 1) * HD]
            for g, dst in dsts:
                _dilate(scr.at[j], dst, DILATIONS[g], tm, BF16)

    def piece(p):
        return pl.BlockSpec((tm, W), lambda i: (i, p))

    shapes = [_sds((4 * d, S // d, HD), BF16) for d in DILATIONS]
    res = pl.pallas_call(
        body, grid=(S // tm,),
        in_specs=[piece(P_CQ), piece(P_CQ + 1), piece(P_CQ + 2), piece(P_CK), piece(P_CV)],
        out_specs=[_dil_spec(d, tm) for d in DILATIONS] * 3, out_shape=shapes * 3,
        scratch_shapes=[pltpu.VMEM((5, 4, tm, HD), F32)],
        compiler_params=_cp("parallel"), name="att_prep")(proj, proj, proj, proj, proj)
    return res[0:3], res[3:6], res[6:9]


def _att_specs(g):
    d = DILATIONS[g]
    nres, njb = ATT_STEP[g]
    nb = S // d // HD
    own = pl.BlockSpec((4 * nres, njb * HD, HD), lambda r, j: (r, j, 0))
    prev = pl.BlockSpec((4 * nres, HD, HD), lambda r, j: (r, jnp.maximum(j * njb - 1, 0), 0))
    nxt = pl.BlockSpec((4 * nres, HD, HD), lambda r, j: (r, jnp.minimum((j + 1) * njb, nb - 1), 0))
    return (d // nres, nb // njb), own, prev, nxt


def _att_fwd(q, k, v, g):
    d = DILATIONS[g]
    nres, njb = ATT_STEP[g]
    grid, own, prev, _ = _att_specs(g)

    def body(q_ref, kp_ref, ko_ref, vp_ref, vo_ref, o_ref, l_ref):
        jb = pl.program_id(1)
        band, is_prev = _att_band()
        no_prev = jnp.where(is_prev, jnp.where(jb > 0, 0.0, NEG), 0.0)
        blocks = [(a, jj) for jj in range(njb) for a in range(4 * nres)]
        for g0 in range(0, len(blocks), ATT_GROUP):
            grp = blocks[g0:g0 + ATT_GROUP]
            s, v2 = [], []
            for a, jj in grp:
                k2_, v2_ = _att_keys(kp_ref, ko_ref, vp_ref, vo_ref, a, jj)
                s_ = jnp.where(band, _dot_nt(q_ref[a, jj * HD:(jj + 1) * HD, :], k2_) * SCALE, NEG)
                s.append(s_ + no_prev if jj == 0 else s_)
                v2.append(v2_)
            m = [jnp.max(s_, axis=-1, keepdims=True) for s_ in s]
            e = [jnp.exp(s_ - m_) for s_, m_ in zip(s, m)]
            den = [jnp.sum(e_, axis=-1, keepdims=True) for e_ in e]
            inv = [1.0 / d_ for d_ in den]
            for i, (a, jj) in enumerate(grp):
                rs_ = slice(jj * HD, (jj + 1) * HD)
                o_ref[a, rs_, :] = _dot((e[i] * inv[i]).astype(BF16), v2[i])
                l_ref[a, rs_, :] = jnp.broadcast_to(m[i] + jnp.log(den[i]), (HD, HD))

    return pl.pallas_call(
        body, grid=grid, in_specs=[own, prev, own, prev, own], out_specs=[own, own],
        out_shape=[_sds((4 * d, S // d, HD), F32)] * 2,
        compiler_params=_cp("parallel", "parallel"), name="att_fwd")(q, k, k, v, v)


def _att_mix(os_, ls_, proj, y_all, tm=512):
    def body(o0, o1, o2, l0, l1, l2, gt_ref, y_in, y_ref, om_ref, lt_ref, so1, so2, sl1, sl2):
        _undilate(o1, so1, DILATIONS[1], tm)
        _undilate(o2, so2, DILATIONS[2], tm)
        _undilate(l1, sl1, DILATIONS[1], tm)
        _undilate(l2, sl2, DILATIONS[2], tm)
        for h in range(4):
            a, b, c = l0[h], sl1[h], sl2[h]
            m = jnp.maximum(jnp.maximum(a, b), c)
            ea, eb, ec = jnp.exp(a - m), jnp.exp(b - m), jnp.exp(c - m)
            z = ea + eb + ec
            inv = 1.0 / z
            o = (ea * inv) * o0[h] + (eb * inv) * so1[h] + (ec * inv) * so2[h]
            gt = gt_ref[:, h * HD:(h + 1) * HD]
            om_ref[h] = o
            lt_ref[h] = m + jnp.log(z)
            y_ref[:, h * HD:(h + 1) * HD] = (o * (gt * _sigmoid(gt))).astype(BF16)

    dil = [_dil_spec(d, tm) for d in DILATIONS]
    return pl.pallas_call(
        body, grid=(S // tm,),
        in_specs=dil * 2 + [pl.BlockSpec((tm, W), lambda i: (i, P_CGATE)), ANY],
        out_specs=[pl.BlockSpec((None, tm, W), lambda i: (2, i, 0)), dil[0], dil[0]],
        out_shape=[_sds((4, S, W), BF16), _sds((4, S, HD), F32), _sds((4, S, HD), F32)],
        scratch_shapes=[pltpu.VMEM((4, tm, HD), F32)] * 4,
        input_output_aliases={7: 0}, compiler_params=_cp("parallel"), name="att_mix")(*os_, *ls_, proj, y_all)


def _att_bwd_pre(dy_all, proj, om, lse, dproj, tm=512):
    def body(dy_ref, gt_ref, om_ref, ls_ref, dp_in, *rest):
        dos, dls, lss, dp_ref, sdo, sdl = rest[0:3], rest[3:6], rest[6:8], rest[8], rest[9], rest[10]
        for h in range(4):
            cs = slice(h * HD, (h + 1) * HD)
            gt = gt_ref[:, cs]
            sg = _sigmoid(gt)
            dyv = dy_ref[:, cs]
            o = om_ref[h]
            do = dyv * (gt * sg)
            dp_ref[:, cs] = (dyv * o * (sg * (1.0 + gt * (1.0 - sg)))).astype(BF16)
            sdo[h] = do
            sdl[h] = jnp.broadcast_to(jnp.sum(do * o, axis=-1, keepdims=True), (tm, HD))
        for g, d in enumerate(DILATIONS):
            _dilate(sdo, dos[g], d, tm, BF16)
            _dilate(sdl, dls[g], d, tm)
            if g > 0:
                _dilate(ls_ref, lss[g - 1], d, tm)

    dil = [_dil_spec(d, tm) for d in DILATIONS]
    gcol = pl.BlockSpec((tm, W), lambda i: (i, P_CGATE))
    res = pl.pallas_call(
        body, grid=(S // tm,),
        in_specs=[pl.BlockSpec((None, tm, W), lambda i: (2, i, 0)), gcol, dil[0], dil[0], ANY],
        out_specs=dil + dil + dil[1:] + [gcol],
        out_shape=([_sds((4 * d, S // d, HD), BF16) for d in DILATIONS] + [_sds((4 * d, S // d, HD), F32) for d in DILATIONS]
                   + [_sds((4 * d, S // d, HD), F32) for d in DILATIONS[1:]] + [_sds((S, DIN), BF16)]),
        scratch_shapes=[pltpu.VMEM((4, tm, HD), F32)] * 2,
        input_output_aliases={4: 8}, compiler_params=_cp("parallel"), name="att_bwd_pre")(dy_all, proj, om, lse, dproj)
    return res[0:3], res[3:6], [lse] + list(res[6:8]), res[8]


def _att_bwd(q, k, v, do, lse, delta, g):
    d = DILATIONS[g]
    nres, njb = ATT_STEP[g]
    grid, own, prev, nxt = _att_specs(g)

    def body(qa_ref, qn_ref, kp_ref, ko_ref, vp_ref, vo_ref, doa_ref, don_ref, la_ref, ln_ref, da_ref, dn_ref,
             dq_ref, dk_ref, dv_ref):
        jb = pl.program_id(1)
        band, is_prev = _att_band()
        m_next = lax.broadcasted_iota(jnp.int32, (HD, HD), 1) >= lax.broadcasted_iota(jnp.int32, (HD, HD), 0)
        has_prev = jnp.where(is_prev, jnp.where(jb > 0, 1.0, 0.0), 1.0)
        has_next = jnp.where(jb < grid[1] - 1, 1.0, 0.0)

        def wide(t):
            return jnp.concatenate([t, t], axis=1)

        blocks = [(a, jj) for jj in range(njb) for a in range(4 * nres)]
        for g0 in range(0, len(blocks), ATT_GROUP_BWD):
            grp = blocks[g0:g0 + ATT_GROUP_BWD]
            ops = []
            for a, jj in grp:
                rs_ = slice(jj * HD, (jj + 1) * HD)
                k2, v2 = _att_keys(kp_ref, ko_ref, vp_ref, vo_ref, a, jj)
                if jj == njb - 1:
                    qn, don, lsn, dln, fn = qn_ref[a], don_ref[a], ln_ref[a], dn_ref[a], has_next
                else:
                    ns = slice((jj + 1) * HD, (jj + 2) * HD)
                    qn, don, lsn, dln, fn = qa_ref[a, ns, :], doa_ref[a, ns, :], la_ref[a, ns, :], da_ref[a, ns, :], None
                ops.append(dict(qa=qa_ref[a, rs_, :], doa=doa_ref[a, rs_, :], lsa=wide(la_ref[a, rs_, :]),
                                dla=wide(da_ref[a, rs_, :]), k2=k2, v2=v2, ko=ko_ref[a, rs_, :], vo=vo_ref[a, rs_, :],
                                qn=qn, don=don, lsn=lsn, dln=dln, fn=fn, first=jj == 0))
            sa = [_dot_nt(o["qa"], o["k2"]) for o in ops]
            dpa = [_dot_nt(o["doa"], o["v2"]) for o in ops]
            sn = [_dot_nt(o["qn"], o["ko"]) for o in ops]
            dpn = [_dot_nt(o["don"], o["vo"]) for o in ops]
            pa, pn = [], []
            for o, sa_, sn_ in zip(ops, sa, sn):
                p_ = jnp.where(band, jnp.exp(sa_ * SCALE - o["lsa"]), 0.0)
                pa.append(p_ * has_prev if o["first"] else p_)
                p_ = jnp.where(m_next, jnp.exp(sn_ * SCALE - o["lsn"]), 0.0)
                pn.append(p_ if o["fn"] is None else p_ * o["fn"])
            dsa = [(p_ * (dp_ - o["dla"]) * SCALE).astype(BF16) for p_, dp_, o in zip(pa, dpa, ops)]
            dsn = [(p_ * (dp_ - o["dln"]) * SCALE).astype(BF16) for p_, dp_, o in zip(pn, dpn, ops)]
            for i, (a, jj) in enumerate(grp):
                rs_ = slice(jj * HD, (jj + 1) * HD)
                o = ops[i]
                dq_ref[a, rs_, :] = _dot(dsa[i], o["k2"])
                q2 = jnp.concatenate([o["qa"], o["qn"]], axis=0)
                do2 = jnp.concatenate([o["doa"], o["don"]], axis=0)
                dk_ref[a, rs_, :] = _dot_tn(jnp.concatenate([dsa[i][:, HD:], dsn[i]], axis=0), q2)
                dv_ref[a, rs_, :] = _dot_tn(jnp.concatenate([pa[i][:, HD:].astype(BF16), pn[i].astype(BF16)], axis=0), do2)

    return pl.pallas_call(
        body, grid=grid, in_specs=[own, nxt, prev, own, prev, own, own, nxt, own, nxt, own, nxt],
        out_specs=[own, own, own], out_shape=[_sds((4 * d, S // d, HD), F32)] * 3,
        compiler_params=_cp("parallel", "parallel"), name="att_bwd")(q, q, k, k, v, v, do, do, lse, lse, delta, delta)


def _att_bwd_post(dqs, dks, dvs, dproj, tm=512):
    def body(*refs):
        dq, dk, dv, dp_ref, scr = refs[0:3], refs[3:6], refs[6:9], refs[10], refs[11]
        for g in range(3):
            _undilate(dq[g], scr, DILATIONS[g], tm)
            for h in range(4):
                dp_ref[:, g * W + h * HD:g * W + (h + 1) * HD] = scr[h].astype(BF16)
        for j, parts in enumerate((dk, dv)):
            acc = None
            for g in range(3):
                _undilate(parts[g], scr, DILATIONS[g], tm)
                vals = [scr[h] for h in range(4)]
                acc = vals if acc is None else [x + y for x, y in zip(acc, vals)]
            for h in range(4):
                dp_ref[:, (3 + j) * W + h * HD:(3 + j) * W + (h + 1) * HD] = acc[h].astype(BF16)

    dil = [_dil_spec(d, tm) for d in DILATIONS]
    return pl.pallas_call(
        body, grid=(S // tm,), in_specs=dil * 3 + [ANY],
        out_specs=pl.BlockSpec((tm, 5 * W), lambda i: (i, 1)), out_shape=_sds((S, DIN), BF16),
        scratch_shapes=[pltpu.VMEM((4, tm, HD), F32)],
        input_output_aliases={9: 0}, compiler_params=_cp("parallel"), name="att_bwd_post")(*dqs, *dks, *dvs, dproj)


def _mem_kv_fwd(mem_n, wkv):
    m = mem_n.shape[0]

    def body(a_ref, w_ref, o_ref):
        o_ref[...] = _dot(a_ref[...], w_ref[...])

    return pl.pallas_call(body, out_shape=_sds((m, 2 * W), F32), compiler_params=_cp(), name="mem_kv_fwd")(mem_n, wkv)


def _mem_softmax(q, k):
    s = _dot_nt(q, k) * SCALE
    e = jnp.exp(s - jnp.max(s, axis=-1, keepdims=True))
    return e * (1.0 / jnp.sum(e, axis=-1, keepdims=True))


def _mem_fwd(proj, kv, y_all, tm=512):
    m = kv.shape[0]

    def body(q_ref, gt_ref, kv_ref, y_in, y_ref):
        gt = gt_ref[...]
        sl = gt * _sigmoid(gt)
        for h in range(4):
            cs = slice(h * HD, (h + 1) * HD)
            p = _mem_softmax(q_ref[:, cs].astype(BF16), kv_ref[:, cs].astype(BF16))
            o = _dot(p.astype(BF16), kv_ref[:, W + h * HD:W + (h + 1) * HD].astype(BF16))
            y_ref[:, cs] = (o * sl[:, cs]).astype(BF16)

    return pl.pallas_call(
        body, grid=(S // tm,),
        in_specs=[pl.BlockSpec((tm, W), lambda i: (i, P_MQ)), pl.BlockSpec((tm, W), lambda i: (i, P_MGATE)),
                  pl.BlockSpec((m, 2 * W), lambda i: (0, 0)), ANY],
        out_specs=pl.BlockSpec((None, tm, W), lambda i: (3, i, 0)), out_shape=_sds((4, S, W), BF16),
        input_output_aliases={3: 0}, compiler_params=_cp("parallel"), name="mem_fwd")(proj, proj, kv, y_all)


def _mem_bwd(proj, kv, dy_all, dproj, tm=512):
    m = kv.shape[0]
    ni = S // tm

    def body(q_ref, gt_ref, kv_ref, dy_ref, dp_in, dp_ref, dkv_ref, obuf, osem):
        i = pl.program_id(0)
        gt = gt_ref[...]
        sg = _sigmoid(gt)
        sl = gt * sg
        dsl = sg * (1.0 + gt * (1.0 - sg))
        dyv = dy_ref[...]

        def fill(slot):
            for h in range(4):
                cs = slice(h * HD, (h + 1) * HD)
                vs = slice(W + h * HD, W + (h + 1) * HD)
                q = q_ref[:, cs].astype(BF16)
                k = kv_ref[:, cs].astype(BF16)
                v = kv_ref[:, vs].astype(BF16)
                p = _mem_softmax(q, k)
                pb = p.astype(BF16)
                o = _dot(pb, v)
                do = dyv[:, cs] * sl[:, cs]
                dob = do.astype(BF16)
                dp = _dot_nt(dob, v)
                dsb = (p * (dp - jnp.sum(dp * p, axis=-1, keepdims=True)) * SCALE).astype(BF16)
                slot[:, cs] = _dot(dsb, k).astype(BF16)
                slot[:, vs] = (dyv[:, cs] * o * dsl[:, cs]).astype(BF16)
                dk = _dot_tn(dsb, q)
                dv = _dot_tn(pb, dob)

                @pl.when(i == 0)
                def _():
                    dkv_ref[:, cs] = dk
                    dkv_ref[:, vs] = dv

                @pl.when(i > 0)
                def _():
                    dkv_ref[:, cs] += dk
                    dkv_ref[:, vs] += dv

        _tile_put(obuf, osem, lambda st: dp_ref.at[pl.ds(st * tm, tm), pl.ds(P_MQ * W, 2 * W)], i, ni, fill)

    return pl.pallas_call(
        body, grid=(ni,),
        in_specs=[pl.BlockSpec((tm, W), lambda i: (i, P_MQ)), pl.BlockSpec((tm, W), lambda i: (i, P_MGATE)),
                  pl.BlockSpec((m, 2 * W), lambda i: (0, 0)), pl.BlockSpec((None, tm, W), lambda i: (3, i, 0)), ANY],
        out_specs=[ANY, pl.BlockSpec((m, 2 * W), lambda i: (0, 0))],
        out_shape=[_sds((S, DIN), BF16), _sds((m, 2 * W), F32)],
        scratch_shapes=[pltpu.VMEM((2, tm, 2 * W), BF16), pltpu.SemaphoreType.DMA((2,))],
        input_output_aliases={4: 0}, compiler_params=_cp("arbitrary"), name="mem_bwd")(proj, proj, kv, dy_all, dproj)


def _mem_kv_bwd(mem, g, mem_n, wkv, dkv):
    m = mem.shape[0]

    def body(x_ref, g_ref, a_ref, w_ref, d_ref, dw_ref, dg_ref):
        db = d_ref[...].astype(BF16)
        dw_ref[...] = _dot_tn(a_ref[...], db).astype(BF16)
        dn = _dot_nt(db, w_ref[...])
        xv = x_ref[...]
        xh = xv * lax.rsqrt(jnp.mean(xv * xv, axis=-1, keepdims=True) + EPS)
        dg_ref[...] = jnp.sum(dn * xh, axis=0, keepdims=True)

    return pl.pallas_call(
        body, out_shape=[_sds((D, 2 * W), BF16), _sds((1, D), F32)], compiler_params=_cp(), name="mem_kv_bwd")(mem, g, mem_n, wkv, dkv)


def _layer_fwd(x, mem, p, wg):
    win, wkv, wbr, wo = wg
    h = _rms_fwd(x, p["norm_g"], 512)
    proj = _proj_fwd(h, win)
    y_all = lax.empty((4, S, W), BF16)
    y_all = _gmlp_fwd(proj, p["gm_ln_g"], p["gm_ln_b"], p["gm_ws"], p["gm_bias"], y_all)
    y_all = _pool_fwd(proj, p["pool_w"], p["pool_scale"], y_all)
    qs, ks, vs = _att_prep(proj)
    os_, ls_ = zip(*[_att_fwd(qs[g], ks[g], vs[g], g) for g in range(3)])
    y_all, om, lse = _att_mix(os_, ls_, proj, y_all)
    mem_n = _rms_fwd(mem, p["mem_norm_g"], mem.shape[0])
    kv = _mem_kv_fwd(mem_n, wkv)
    y_all = _mem_fwd(proj, kv, y_all)
    z = _merge_fwd(y_all, wbr, proj)
    x_new = _out_fwd(z, wo, x)
    return x_new, dict(x=x, h=h, proj=proj, y_all=y_all, om=om, lse=lse, mem_n=mem_n, kv=kv, z=z, qkv=(qs, ks, vs))


GRAD_PARTS = ((2, D // 2, CW), (2, D // 8, 2 * W), (2, 2 * W, D // NCHIP), (2, D // 8, D))
SUM_TILE = (64, 128, 256, 128)
ADAM_TILE = (128, 256, 2048, 256)
PLACE_TILE = (256, 256, 512, 256)


def _layer_bwd(dx, mem, p, wg, sv, exchange):
    win, wkv, wbr, wo = wg
    proj = sv["proj"]
    dz, d_wo = _out_bwd(dx, sv["z"], wo)
    dproj = lax.empty((S, DIN), BF16)
    dproj, dy_all, d_wbr = _merge_bwd(dz, sv["y_all"], wbr, proj, dproj)
    dproj, d_ws, d_bs, d_lg, d_lb = _gmlp_bwd(proj, dy_all, p["gm_ln_g"], p["gm_ln_b"], p["gm_ws"], p["gm_bias"], dproj)
    dproj, d_pw, d_sc = _pool_bwd(proj, dy_all, p["pool_w"], p["pool_scale"], dproj)
    dos, dls, lss, dproj = _att_bwd_pre(dy_all, proj, sv["om"], sv["lse"], dproj)
    qs, ks, vs = sv["qkv"]
    dqs, dks, dvs = zip(*[_att_bwd(qs[g], ks[g], vs[g], dos[g], lss[g], dls[g], g) for g in range(3)])
    dproj = _att_bwd_post(dqs, dks, dvs, dproj)
    dproj, dkv = _mem_bwd(proj, sv["kv"], dy_all, dproj)
    d_wkv, d_mg = _mem_kv_bwd(mem, p["mem_norm_g"], sv["mem_n"], wkv, dkv)
    d_win = _proj_bwd_w(sv["h"], dproj)
    big = tuple(t.reshape((NCHIP,) + s) for t, s in zip((d_win, d_wkv, d_wbr, d_wo), GRAD_PARTS))
    inflight = exchange(big)
    dh = _proj_bwd_x(dproj, win, inflight[-1])
    dx_in, d_ng = _rms_bwd(dh, sv["x"], p["norm_g"], dx)
    small = dict(norm_g=d_ng, gm_ln_g=d_lg, gm_ln_b=d_lb, gm_ws=d_ws, gm_bs=d_bs[:, :, 0], pool_w=d_pw, pool_scale=d_sc, mem_norm_g=d_mg)
    return dx_in, (big,) + inflight, small


_SMALL = ("norm_g", "gm_ln_g", "gm_ln_b", "gm_ws", "gm_bs", "pool_w", "pool_scale", "mem_norm_g")


def _layer_params(l, norm_g, gm_ln_g, gm_ln_b, gm_ws, gm_bs, pool_w, pool_scale, mem_norm_g):
    return dict(norm_g=norm_g[l][None], gm_ln_g=gm_ln_g[l][None], gm_ln_b=gm_ln_b[l][None], gm_ws=gm_ws[l],
                gm_bias=jnp.broadcast_to(gm_bs[l][:, :, None], (4, HD, HD)), pool_w=pool_w[l],
                pool_scale=pool_scale[l][None], mem_norm_g=mem_norm_g[l][None])


def kernel(x, mem, norm_g, w_in, gm_ln_g, gm_ln_b, gm_ws, gm_bs, pool_w, pool_scale, mem_norm_g, w_mem_kv, w_branch, w_out, final_norm_g, loss_target, m_norm_g, m_w_in, m_gm_ln_g, m_gm_ln_b, m_gm_ws, m_gm_bs, m_pool_w, m_pool_scale, m_mem_norm_g, m_w_mem_kv, m_w_branch, m_w_out, m_final_norm_g, v_norm_g, v_w_in, v_gm_ln_g, v_gm_ln_b, v_gm_ws, v_gm_bs, v_pool_w, v_pool_scale, v_mem_norm_g, v_w_mem_kv, v_w_branch, v_w_out, v_final_norm_g):
    xs, memv, tgt = x[0], mem[0], loss_target[0]
    params = [_layer_params(l, norm_g, gm_ln_g, gm_ln_b, gm_ws, gm_bs, pool_w, pool_scale, mem_norm_g) for l in range(NL)]

    shards = (w_in.astype(BF16), w_mem_kv.astype(BF16), w_branch.astype(BF16), w_out.astype(BF16))
    lands, gsems, after = _gather_start(shards)

    saved, wgs = [], []
    for l in range(NL):
        if l == 0:
            got, relay_sems = _gather_relay(shards, lands[0], gsems[0], after)
            got = _gather_wait_relay(got, relay_sems)
        else:
            got = _gather_wait(l, shards, lands[l], gsems[l], after)
        got = [_place_own(got[k], shards[k], l, PLACE_TILE[k]) for k in range(4)]
        wgs.append((got[0], got[1].reshape(D, 2 * W), got[2], got[3].reshape(D, D)))
        xs, sv = _layer_fwd(xs, memv, params[l], wgs[l])
        saved.append(sv)
        after = xs
    dx, d_fg, ls = _loss_head(xs, tgt, final_norm_g[None])
    loss = lax.psum(ls[0, 0], ("x", "y", "c"))

    flight, small = [None] * NL, [None] * NL
    for l in reversed(range(NL)):
        dx, flight[l], small[l] = _layer_bwd(dx, memv, params[l], wgs[l], saved[l], functools.partial(_exch_start, l))
    grad_x = dx[None]

    leaves = [jnp.stack([small[l][n] for l in range(NL)]) for n in _SMALL] + [d_fg]
    sizes = [t.size for t in leaves]
    packed = jnp.concatenate([t.reshape(-1, 128) for t in leaves], axis=0)
    rows = packed.shape[0]
    small_zone, small_sems, after = _small_start(packed)

    ws = dict(norm_g=norm_g, w_in=w_in, gm_ln_g=gm_ln_g, gm_ln_b=gm_ln_b, gm_ws=gm_ws, gm_bs=gm_bs, pool_w=pool_w,
              pool_scale=pool_scale, mem_norm_g=mem_norm_g, w_mem_kv=w_mem_kv, w_branch=w_branch, w_out=w_out,
              final_norm_g=final_norm_g)
    ms = dict(norm_g=m_norm_g, w_in=m_w_in, gm_ln_g=m_gm_ln_g, gm_ln_b=m_gm_ln_b, gm_ws=m_gm_ws, gm_bs=m_gm_bs,
              pool_w=m_pool_w, pool_scale=m_pool_scale, mem_norm_g=m_mem_norm_g, w_mem_kv=m_w_mem_kv,
              w_branch=m_w_branch, w_out=m_w_out, final_norm_g=m_final_norm_g)
    vs = dict(norm_g=v_norm_g, w_in=v_w_in, gm_ln_g=v_gm_ln_g, gm_ln_b=v_gm_ln_b, gm_ws=v_gm_ws, gm_bs=v_gm_bs,
              pool_w=v_pool_w, pool_scale=v_pool_scale, mem_norm_g=v_mem_norm_g, w_mem_kv=v_w_mem_kv,
              w_branch=v_w_branch, w_out=v_w_out, final_norm_g=v_final_norm_g)

    big = ("w_in", "w_mem_kv", "w_branch", "w_out")
    acc = {n: [lax.empty((ws[n].size // ws[n].shape[-1], ws[n].shape[-1]), F32) for _ in range(4)] for n in big}
    for l in reversed(range(NL)):
        parts, zones, sems, _ = flight[l]
        zones = _exch_wait(l, parts, zones, sems, after)
        full = _share_full([_sum_half(parts[k], zones[k], SUM_TILE[k]) for k in range(4)])
        for k, n in enumerate(big):
            acc[n] = _adamw_layer(l, ws[n], full[k], ms[n], vs[n], acc[n], ADAM_TILE[k])
        after = acc["w_in"][0]
    grads, upd = {}, {}
    for n in big:
        d_, m_, v_, g_ = (t.reshape(ws[n].shape) for t in acc[n])
        grads[n], upd[n] = g_, (d_, m_, v_)

    small_zone = _small_wait(packed, small_zone, small_sems, after)
    tot = _sum_small(packed, small_zone, max(t for t in range(8, 513, 8) if rows % t == 0))
    offs = [0]
    for sz in sizes:
        offs.append(offs[-1] + sz // 128)
    for i, n in enumerate(_SMALL + ("final_norm_g",)):
        grads[n] = tot[offs[i]:offs[i + 1]].reshape(ws[n].shape)
        upd[n] = _adamw(ws[n], grads[n], ms[n], vs[n])
    order = ("norm_g", "w_in", "gm_ln_g", "gm_ln_b", "gm_ws", "gm_bs", "pool_w", "pool_scale", "mem_norm_g", "w_mem_kv",
             "w_branch", "w_out", "final_norm_g")
    return (loss, grad_x, *[grads[n] for n in order], *[upd[n][0] for n in order], *[upd[n][1] for n in order],
            *[upd[n][2] for n in order])
```

```python
import functools
import math

import jax
import jax.numpy as jnp
from jax import lax
from jax.experimental import pallas as pl
from jax.experimental.pallas import tpu as pltpu

F32 = jnp.float32
BF16 = jnp.bfloat16

S = 4096
D = 1024
W = 512
DIN = 10752
NL = 4
NCHIP = 4
NDEV = 8
CW = DIN // NCHIP
TN_IN = 896
NJ = CW // TN_IN
HD = 128
EPS = 1e-6
NEG = -1e30
SCALE = HD ** -0.5
INV_SQRT2 = 1.0 / math.sqrt(2.0)
INV_SQRT2PI = 1.0 / math.sqrt(2.0 * math.pi)
POOL_WINDOWS = (2, 4, 8, 16)
DILATIONS = (1, 4, 16)
HALO = 16
NPIECE = DIN // W
P_AGATE, P_PIN, P_PGATE, P_CQ, P_CK, P_CV, P_CGATE, P_MQ, P_MGATE, P_GM = 2, 3, 4, 5, 8, 9, 10, 11, 12, 13
VMEM_LIMIT = 56 * 1024 * 1024

ADAM_LR, ADAM_B1, ADAM_B2, ADAM_EPS, ADAM_WD, ADAM_STEP = 0.001, 0.9, 0.999, 1e-08, 0.01, 10

MESH = pl.DeviceIdType.MESH
ANY = pl.BlockSpec(memory_space=pl.ANY)


def _cp(*sem):
    return pltpu.CompilerParams(dimension_semantics=sem or None, vmem_limit_bytes=VMEM_LIMIT)


def _sds(shape, dtype):
    return jax.ShapeDtypeStruct(shape, dtype)


def _sigmoid(v):
    return 1.0 / (1.0 + jnp.exp(-v))


def _dot(a, b):
    return jnp.dot(a, b, preferred_element_type=F32)


def _dot_nt(a, b):
    return lax.dot_general(a, b, (((1,), (1,)), ((), ())), preferred_element_type=F32)


def _dot_tn(a, b):
    return lax.dot_general(a, b, (((0,), (0,)), ((), ())), preferred_element_type=F32)


def _tile_put(buf, sem, dst_of, step, nsteps, fill):
    slot = step % 2

    def copy(s, st):
        return pltpu.make_async_copy(buf.at[s], dst_of(st), sem.at[s])

    @pl.when(step >= 2)
    def _():
        copy(slot, step).wait()

    fill(buf.at[slot])
    copy(slot, step).start()

    @pl.when(step == nsteps - 1)
    def _():
        if nsteps >= 2:
            copy(1 - slot, step).wait()
        copy(slot, step).wait()


def _my_pos():
    return lax.axis_index("x"), lax.axis_index("y"), lax.axis_index("c")


_CHIP_REL = ((1, 0), (0, 1), (1, 1))
_DEV_REL = tuple((dx, dy, dc) for dx in (0, 1) for dy in (0, 1) for dc in (0, 1))[1:]


HBM = pl.BlockSpec(memory_space=pltpu.HBM)
SEM = pl.BlockSpec(memory_space=pltpu.SEMAPHORE)
EFFECT = pltpu.SideEffectType.DATAFLOW_SIDE_EFFECTING
N_GATHER = 3 * 4
N_EXCH = 7 * 4


def _in_hbm(t):
    return pltpu.with_memory_space_constraint(t, pltpu.HBM)


def _gather_start(shards):
    nk = len(shards)
    lands = [pltpu.HBM((NCHIP,) + s.shape[1:], BF16) for s in shards for _ in range(NL)]

    def body(*refs):
        ins, outs = refs[:nk], refs[nk:nk + nk * NL]
        sems = refs[nk + nk * NL:nk + nk * NL + 2 * NL]
        token = refs[-1]
        x, y, c = _my_pos()
        me = 2 * x + y
        for l in range(NL):
            for r, (dx, dy) in enumerate(_CHIP_REL):
                for k in range(nk):
                    src, dst = ins[k].at[l], outs[k * NL + l].at[me]
                    if l == 0:
                        hf = pl.ds(c * (shards[k].shape[1] // 2), shards[k].shape[1] // 2)
                        src, dst = src.at[hf], dst.at[hf]
                    pltpu.make_async_remote_copy(
                        src_ref=src, dst_ref=dst, send_sem=sems[2 * l].at[r * nk + k],
                        recv_sem=sems[2 * l + 1].at[r * nk + k], device_id=(x ^ dx, y ^ dy, c), device_id_type=MESH).start()
        token[...] = jnp.zeros_like(token)

    res = pl.pallas_call(
        body, name="gather_start",
        out_shape=lands + [pltpu.SemaphoreType.DMA((N_GATHER,))] * (2 * NL) + [_sds((8, 128), F32)],
        in_specs=[HBM] * nk, out_specs=[HBM] * (nk * NL) + [SEM] * (2 * NL) + [pl.BlockSpec(memory_space=pltpu.VMEM)],
        compiler_params=pltpu.CompilerParams(has_side_effects=EFFECT))(*[_in_hbm(s) for s in shards])
    lands = [[res[k * NL + l] for k in range(nk)] for l in range(NL)]
    sems = [(res[nk * NL + 2 * l], res[nk * NL + 2 * l + 1]) for l in range(NL)]
    return lands, sems, res[-1]


def _gather_relay(shards, lands, sems, after):
    nk = len(shards)
    half = [s.shape[1] // 2 for s in shards]

    na = len(after)

    def body(*refs):
        ins, land = refs[:nk], refs[nk:2 * nk]
        send, recv = refs[2 * nk], refs[2 * nk + 1]
        send2, recv2 = refs[3 * nk + 2 + na], refs[3 * nk + 3 + na]
        x, y, c = _my_pos()
        for r, (dx, dy) in enumerate(_CHIP_REL):
            cx, cy = x ^ dx, y ^ dy
            for k in range(nk):
                hf = pl.ds(c * half[k], half[k])
                got = land[k].at[2 * cx + cy].at[hf]
                cp = pltpu.make_async_remote_copy(
                    src_ref=ins[k].at[0].at[hf], dst_ref=got, send_sem=send.at[r * nk + k],
                    recv_sem=recv.at[r * nk + k], device_id=(cx, cy, c), device_id_type=MESH)
                cp.wait_send()
                cp.wait_recv()
                pltpu.make_async_remote_copy(
                    src_ref=got, dst_ref=got, send_sem=send2.at[r * nk + k], recv_sem=recv2.at[r * nk + k],
                    device_id=(x, y, 1 - c), device_id_type=MESH).start()

    res = pl.pallas_call(
        body, name="gather_relay",
        out_shape=[pltpu.HBM(t.shape, t.dtype) for t in lands] + [pltpu.SemaphoreType.DMA((N_GATHER,))] * 2,
        in_specs=[ANY] * nk + [HBM] * nk + [SEM, SEM] + [ANY] * na, out_specs=[HBM] * nk + [SEM, SEM],
        input_output_aliases={nk + k: k for k in range(nk)},
        compiler_params=pltpu.CompilerParams(has_side_effects=EFFECT))(*shards, *lands, *sems, *after)
    return res[:nk], (res[nk], res[nk + 1])


def _gather_wait_relay(lands, sems):
    nk = len(lands)
    half = [t.shape[1] // 2 for t in lands]

    def body(*refs):
        land = refs[:nk]
        send, recv = refs[nk], refs[nk + 1]
        x, y, c = _my_pos()
        for r, (dx, dy) in enumerate(_CHIP_REL):
            chip = 2 * (x ^ dx) + (y ^ dy)
            for k in range(nk):
                mine = land[k].at[chip].at[pl.ds(c * half[k], half[k])]
                theirs = land[k].at[chip].at[pl.ds((1 - c) * half[k], half[k])]
                cp = pltpu.make_async_remote_copy(
                    src_ref=mine, dst_ref=theirs, send_sem=send.at[r * nk + k], recv_sem=recv.at[r * nk + k],
                    device_id=(x, y, 1 - c), device_id_type=MESH)
                cp.wait_send()
                cp.wait_recv()

    return pl.pallas_call(
        body, name="gather_wait_relay", out_shape=[pltpu.HBM(t.shape, t.dtype) for t in lands],
        in_specs=[HBM] * nk + [SEM, SEM], out_specs=[HBM] * nk, input_output_aliases={k: k for k in range(nk)},
        compiler_params=pltpu.CompilerParams(has_side_effects=EFFECT))(*lands, *sems)


def _gather_wait(l, shards, lands, sems, after):
    nk = len(shards)

    def body(*refs):
        ins, land = refs[:nk], refs[nk:2 * nk]
        send, recv = refs[2 * nk], refs[2 * nk + 1]
        x, y, c = _my_pos()
        for r, (dx, dy) in enumerate(_CHIP_REL):
            cx, cy = x ^ dx, y ^ dy
            for k in range(nk):
                cp = pltpu.make_async_remote_copy(
                    src_ref=ins[k].at[l], dst_ref=land[k].at[2 * cx + cy], send_sem=send.at[r * nk + k],
                    recv_sem=recv.at[r * nk + k], device_id=(cx, cy, c), device_id_type=MESH)
                cp.wait_send()
                cp.wait_recv()

    return pl.pallas_call(
        body, name=f"gather_wait_{l}", out_shape=[pltpu.HBM(t.shape, t.dtype) for t in lands],
        in_specs=[ANY] * nk + [HBM] * nk + [SEM, SEM, ANY], out_specs=[HBM] * nk,
        input_output_aliases={nk + k: k for k in range(nk)},
        compiler_params=pltpu.CompilerParams(has_side_effects=EFFECT))(*shards, *lands, *sems, after)


def _place_own(land, shard, l, tr):
    _, rows, cols = shard.shape[0], shard.shape[-2], shard.shape[-1]
    lead = shard.shape[1:-2]
    nlead = math.prod(lead)
    sh = shard.reshape((NL, nlead, rows, cols))
    ld = land.reshape((NCHIP, nlead, rows, cols))
    me = (2 * lax.axis_index("x") + lax.axis_index("y")).astype(jnp.int32).reshape(1)

    def body(me_ref, s_ref, l_in, o_ref):
        o_ref[...] = s_ref[...]

    out = pl.pallas_call(
        body,
        grid_spec=pltpu.PrefetchScalarGridSpec(
            num_scalar_prefetch=1, grid=(nlead, rows // tr),
            in_specs=[pl.BlockSpec((None, None, tr, cols), lambda b, i, me_ref: (l, b, i, 0)), ANY],
            out_specs=pl.BlockSpec((None, None, tr, cols), lambda b, i, me_ref: (me_ref[0], b, i, 0))),
        out_shape=_sds(ld.shape, BF16), input_output_aliases={2: 0},
        compiler_params=_cp("parallel", "parallel"), name="place_own")(me, sh, ld)
    return out.reshape(land.shape)


def _exch_start(l, parts):
    nk = len(parts)

    def body(*refs):
        ins, outs = refs[:nk], refs[nk:2 * nk]
        send, recv, token = refs[2 * nk:]
        x, y, c = _my_pos()
        for r, (dx, dy, dc) in enumerate(_DEV_REL):
            px, py, pc = x ^ dx, y ^ dy, c ^ dc
            for k in range(nk):
                pltpu.make_async_remote_copy(
                    src_ref=ins[k].at[2 * px + py, pc], dst_ref=outs[k].at[r], send_sem=send.at[r * nk + k],
                    recv_sem=recv.at[r * nk + k], device_id=(px, py, pc), device_id_type=MESH).start()
        token[...] = jnp.zeros_like(token)

    res = pl.pallas_call(
        body, name=f"exch_start_{l}",
        out_shape=[pltpu.HBM((7,) + p.shape[2:], BF16) for p in parts] + [pltpu.SemaphoreType.DMA((N_EXCH,))] * 2 + [_sds((8, 128), F32)],
        in_specs=[HBM] * nk, out_specs=[HBM] * nk + [SEM, SEM, pl.BlockSpec(memory_space=pltpu.VMEM)],
        compiler_params=pltpu.CompilerParams(has_side_effects=EFFECT))(*[_in_hbm(p) for p in parts])
    return res[:nk], (res[nk], res[nk + 1]), res[-1]


def _exch_wait(l, parts, lands, sems, after):
    nk = len(parts)

    def body(*refs):
        ins, land = refs[:nk], refs[nk:2 * nk]
        send, recv = refs[2 * nk], refs[2 * nk + 1]
        x, y, c = _my_pos()
        for r, (dx, dy, dc) in enumerate(_DEV_REL):
            px, py, pc = x ^ dx, y ^ dy, c ^ dc
            for k in range(nk):
                cp = pltpu.make_async_remote_copy(
                    src_ref=ins[k].at[2 * px + py, pc], dst_ref=land[k].at[r], send_sem=send.at[r * nk + k],
                    recv_sem=recv.at[r * nk + k], device_id=(px, py, pc), device_id_type=MESH)
                cp.wait_send()
                cp.wait_recv()

    return pl.pallas_call(
        body, name=f"exch_wait_{l}", out_shape=[pltpu.HBM(t.shape, t.dtype) for t in lands],
        in_specs=[ANY] * nk + [HBM] * nk + [SEM, SEM, ANY], out_specs=[HBM] * nk,
        input_output_aliases={nk + k: k for k in range(nk)},
        compiler_params=pltpu.CompilerParams(has_side_effects=EFFECT))(*parts, *lands, *sems, after)


def _chip_half():
    x, y, c = _my_pos()
    return jnp.stack([2 * x + y, c]).astype(jnp.int32)


def _sum_half(part, land, tr):
    _, _, r2, cols = part.shape

    def body(pos_ref, p_ref, r_ref, o_ref):
        acc = p_ref[...].astype(F32)
        for r in range(7):
            acc = acc + r_ref[r].astype(F32)
        o_ref[...] = acc

    return pl.pallas_call(
        body,
        grid_spec=pltpu.PrefetchScalarGridSpec(
            num_scalar_prefetch=1, grid=(r2 // tr,),
            in_specs=[pl.BlockSpec((None, None, tr, cols), lambda i, pos: (pos[0], pos[1], i, 0)),
                      pl.BlockSpec((7, tr, cols), lambda i, pos: (0, i, 0))],
            out_specs=pl.BlockSpec((None, tr, cols), lambda i, pos: (pos[1], i, 0))),
        out_shape=_sds((2, r2, cols), F32), compiler_params=_cp("parallel"), name="sum_half")(_chip_half(), part, land)


def _share_full(fulls):
    nk = len(fulls)

    def body(*refs):
        ins, outs = refs[:nk], refs[nk:2 * nk]
        send, recv = refs[2 * nk:]
        x, y, c = _my_pos()

        def copy(k, hf):
            return pltpu.make_async_remote_copy(
                src_ref=ins[k].at[hf], dst_ref=outs[k].at[hf], send_sem=send.at[k], recv_sem=recv.at[k],
                device_id=(x, y, 1 - c), device_id_type=MESH)

        for k in range(nk):
            copy(k, c).start()
        for k in range(nk):
            copy(k, 1 - c).wait_recv()
        for k in range(nk):
            copy(k, c).wait_send()

    return pl.pallas_call(
        body, out_shape=[_sds(f.shape, F32) for f in fulls], in_specs=[ANY] * nk, out_specs=[ANY] * nk,
        scratch_shapes=[pltpu.SemaphoreType.DMA((nk,))] * 2, input_output_aliases={k: k for k in range(nk)},
        name="share_full")(*fulls)


def _small_start(packed):
    def body(in_ref, out_ref, send, recv, token):
        x, y, c = _my_pos()
        for r, (dx, dy, dc) in enumerate(_DEV_REL):
            pltpu.make_async_remote_copy(
                src_ref=in_ref, dst_ref=out_ref.at[r], send_sem=send.at[r], recv_sem=recv.at[r],
                device_id=(x ^ dx, y ^ dy, c ^ dc), device_id_type=MESH).start()
        token[...] = jnp.zeros_like(token)

    res = pl.pallas_call(
        body, name="small_start",
        out_shape=[pltpu.HBM((7,) + packed.shape, F32)] + [pltpu.SemaphoreType.DMA((7,))] * 2 + [_sds((8, 128), F32)],
        in_specs=[HBM], out_specs=[HBM, SEM, SEM, pl.BlockSpec(memory_space=pltpu.VMEM)],
        compiler_params=pltpu.CompilerParams(has_side_effects=EFFECT))(_in_hbm(packed))
    return res[0], (res[1], res[2]), res[3]


def _small_wait(packed, land, sems, after):
    def body(in_ref, land_ref, send, recv, after_ref, out_ref):
        x, y, c = _my_pos()
        for r, (dx, dy, dc) in enumerate(_DEV_REL):
            cp = pltpu.make_async_remote_copy(
                src_ref=in_ref, dst_ref=land_ref.at[r], send_sem=send.at[r], recv_sem=recv.at[r],
                device_id=(x ^ dx, y ^ dy, c ^ dc), device_id_type=MESH)
            cp.wait_send()
            cp.wait_recv()

    return pl.pallas_call(
        body, name="small_wait", out_shape=pltpu.HBM(land.shape, land.dtype),
        in_specs=[ANY, HBM, SEM, SEM, ANY], out_specs=HBM, input_output_aliases={1: 0},
        compiler_params=pltpu.CompilerParams(has_side_effects=EFFECT))(packed, land, *sems, after)


def _sum_small(packed, land, tr):
    rows = packed.shape[0]
    x, y, c = _my_pos()
    me = (4 * x + 2 * y + c).astype(jnp.int32).reshape(1)

    def sbody(me_ref, p_ref, r_ref, o_ref):
        me_dev = me_ref[0]
        own = p_ref[...]
        acc = None
        for s in range(NDEV):
            rel = s ^ me_dev
            v = jnp.where(rel == 0, own, r_ref[jnp.maximum(rel - 1, 0)])
            acc = v if acc is None else acc + v
        o_ref[...] = acc

    return pl.pallas_call(
        sbody,
        grid_spec=pltpu.PrefetchScalarGridSpec(
            num_scalar_prefetch=1, grid=(rows // tr,),
            in_specs=[pl.BlockSpec((tr, 128), lambda i, me_ref: (i, 0)), pl.BlockSpec((7, tr, 128), lambda i, me_ref: (0, i, 0))],
            out_specs=pl.BlockSpec((tr, 128), lambda i, me_ref: (i, 0))),
        out_shape=_sds((rows, 128), F32), compiler_params=_cp("parallel"), name="sum_small")(me, packed, land)


def _rms_fwd(x, g, tm):
    n = x.shape[0]

    def body(x_ref, g_ref, h_ref):
        xv = x_ref[...]
        r = lax.rsqrt(jnp.mean(xv * xv, axis=-1, keepdims=True) + EPS)
        h_ref[...] = (xv * r * g_ref[...]).astype(BF16)

    return pl.pallas_call(
        body, grid=(n // tm,),
        in_specs=[pl.BlockSpec((tm, D), lambda i: (i, 0)), pl.BlockSpec((1, D), lambda i: (0, 0))],
        out_specs=pl.BlockSpec((tm, D), lambda i: (i, 0)), out_shape=_sds((n, D), BF16),
        compiler_params=_cp("parallel"), name="rms_fwd")(x, g)


def _rms_bwd(dh, x, g, dres, tm=512):
    n = x.shape[0]

    def body(dh_ref, x_ref, g_ref, dr_ref, dx_ref, dg_ref):
        i = pl.program_id(0)
        xv = x_ref[...]
        r = lax.rsqrt(jnp.mean(xv * xv, axis=-1, keepdims=True) + EPS)
        xh = xv * r
        dhv = dh_ref[...]
        dxh = dhv * g_ref[...]
        dx_ref[...] = dr_ref[...] + r * (dxh - xh * jnp.mean(dxh * xh, axis=-1, keepdims=True))
        part = jnp.sum(dhv * xh, axis=0, keepdims=True)

        @pl.when(i == 0)
        def _():
            dg_ref[...] = part

        @pl.when(i > 0)
        def _():
            dg_ref[...] += part

    row = pl.BlockSpec((tm, D), lambda i: (i, 0))
    vec = pl.BlockSpec((1, D), lambda i: (0, 0))
    return pl.pallas_call(
        body, grid=(n // tm,), in_specs=[row, row, vec, row], out_specs=[row, vec],
        out_shape=[_sds((n, D), F32), _sds((1, D), F32)], compiler_params=_cp("arbitrary"), name="rms_bwd")(dh, x, g, dres)


def _loss_head(x, tgt, g, tm=512):
    def body(x_ref, t_ref, g_ref, dx_ref, dg_ref, ls_ref):
        i = pl.program_id(0)
        xv = x_ref[...]
        r = lax.rsqrt(jnp.mean(xv * xv, axis=-1, keepdims=True) + EPS)
        xh = xv * r
        gv = g_ref[...]
        diff = xh * gv - t_ref[...]
        dy = diff * (1.0 / D)
        dxh = dy * gv
        dx_ref[...] = r * (dxh - xh * jnp.mean(dxh * xh, axis=-1, keepdims=True))
        part_g = jnp.sum(dy * xh, axis=0, keepdims=True)
        part_l = jnp.sum(diff * diff, axis=0, keepdims=True)

        @pl.when(i == 0)
        def _():
            dg_ref[...] = part_g
            ls_ref[...] = part_l

        @pl.when(i > 0)
        def _():
            dg_ref[...] += part_g
            ls_ref[...] += part_l

        @pl.when(i == pl.num_programs(0) - 1)
        def _():
            tot = jnp.sum(ls_ref[...], axis=-1, keepdims=True) * (0.5 / D)
            ls_ref[...] = jnp.broadcast_to(tot, (1, D))

    row = pl.BlockSpec((tm, D), lambda i: (i, 0))
    vec = pl.BlockSpec((1, D), lambda i: (0, 0))
    return pl.pallas_call(
        body, grid=(S // tm,), in_specs=[row, row, vec], out_specs=[row, vec, vec],
        out_shape=[_sds((S, D), F32), _sds((1, D), F32), _sds((1, D), F32)],
        compiler_params=_cp("arbitrary"), name="loss_head")(x, tgt, g)


def _adamw(w, g, m, v):
    shape = w.shape
    cols = shape[-1] if w.ndim > 1 else shape[0]
    rows = w.size // cols
    w2, g2, m2, v2 = (t.reshape(rows, cols) for t in (w, g, m, v))
    tr = rows
    while tr * cols * 4 > (1 << 20) and tr % 16 == 0:
        tr //= 2
    c1 = 1.0 - ADAM_B1 ** ADAM_STEP
    c2 = 1.0 - ADAM_B2 ** ADAM_STEP

    def body(w_ref, g_ref, m_ref, v_ref, d_ref, nm_ref, nv_ref):
        gv = g_ref[...]
        mn = ADAM_B1 * m_ref[...] + (1.0 - ADAM_B1) * gv
        vn = ADAM_B2 * v_ref[...] + (1.0 - ADAM_B2) * (gv * gv)
        d_ref[...] = -ADAM_LR * ((mn / c1) / (jnp.sqrt(vn / c2) + ADAM_EPS) + ADAM_WD * w_ref[...])
        nm_ref[...] = mn
        nv_ref[...] = vn

    blk = pl.BlockSpec((tr, cols), lambda i: (i, 0))
    outs = pl.pallas_call(
        body, grid=(rows // tr,), in_specs=[blk] * 4, out_specs=[blk] * 3,
        out_shape=[_sds((rows, cols), F32)] * 3, compiler_params=_cp("parallel"), name="adamw")(w2, g2, m2, v2)
    return tuple(o.reshape(shape) for o in outs)


def _adamw_layer(l, w, g, m, v, outs, tr):
    cols = w.shape[-1]
    rows = w.size // (NL * cols)
    nb = rows // tr
    w2, m2, v2 = (t.reshape(NL * rows, cols) for t in (w, m, v))
    g2 = g.reshape(rows, cols)
    c1 = 1.0 - ADAM_B1 ** ADAM_STEP
    c2 = 1.0 - ADAM_B2 ** ADAM_STEP

    def body(w_ref, g_ref, m_ref, v_ref, d_in, nm_in, nv_in, go_in, d_ref, nm_ref, nv_ref, go_ref):
        gv = g_ref[...]
        mn = ADAM_B1 * m_ref[...] + (1.0 - ADAM_B1) * gv
        vn = ADAM_B2 * v_ref[...] + (1.0 - ADAM_B2) * (gv * gv)
        d_ref[...] = -ADAM_LR * ((mn / c1) / (jnp.sqrt(vn / c2) + ADAM_EPS) + ADAM_WD * w_ref[...])
        nm_ref[...] = mn
        nv_ref[...] = vn
        go_ref[...] = gv

    lay = pl.BlockSpec((tr, cols), lambda i: (l * nb + i, 0))
    return pl.pallas_call(
        body, grid=(nb,), in_specs=[lay, pl.BlockSpec((tr, cols), lambda i: (i, 0)), lay, lay] + [ANY] * 4,
        out_specs=[lay] * 4, out_shape=[_sds((NL * rows, cols), F32)] * 4,
        input_output_aliases={4: 0, 5: 1, 6: 2, 7: 3}, compiler_params=_cp("parallel"), name="adamw_layer")(w2, g2, m2, v2, *outs)


def _proj_fwd(h, wg, tm=512):
    def body(h_ref, w_ref, o_ref):
        o_ref[...] = _dot(h_ref[...], w_ref[...])

    return pl.pallas_call(
        body, grid=(NCHIP, S // tm),
        in_specs=[pl.BlockSpec((tm, D), lambda c, i: (i, 0)), pl.BlockSpec((None, D, CW), lambda c, i: (c, 0, 0))],
        out_specs=pl.BlockSpec((tm, CW), lambda c, i: (i, c)), out_shape=_sds((S, DIN), F32),
        compiler_params=_cp("parallel", "parallel"), name="proj_fwd")(h, wg)


def _proj_bwd_x(dproj, wg, dep, tm=1024):
    def body(d_ref, w_ref, dep_ref, o_ref):
        k = pl.program_id(1)
        part = _dot_nt(d_ref[...], w_ref[...])

        @pl.when(k == 0)
        def _():
            o_ref[...] = part

        @pl.when(k > 0)
        def _():
            o_ref[...] += part

    return pl.pallas_call(
        body, grid=(S // tm, NCHIP),
        in_specs=[pl.BlockSpec((tm, CW), lambda i, k: (i, k)), pl.BlockSpec((None, D, CW), lambda i, k: (k, 0, 0)), ANY],
        out_specs=pl.BlockSpec((tm, D), lambda i, k: (i, 0)), out_shape=_sds((S, D), F32),
        compiler_params=_cp("parallel", "arbitrary"), name="proj_bwd_x")(dproj, wg, dep)


def _proj_bwd_w(h, dproj, tk=1024):
    nk = S // tk

    def body(h_ref, d_ref, o_ref, acc):
        k = pl.program_id(1)
        part = _dot_tn(h_ref[...], d_ref[...])

        @pl.when(k == 0)
        def _():
            acc[...] = part

        @pl.when(k > 0)
        def _():
            acc[...] += part

        @pl.when(k == nk - 1)
        def _():
            o_ref[...] = acc[...].astype(BF16)

    return pl.pallas_call(
        body, grid=(NCHIP, nk),
        in_specs=[pl.BlockSpec((tk, D), lambda c, k: (k, 0)), pl.BlockSpec((tk, CW), lambda c, k: (k, c))],
        out_specs=pl.BlockSpec((None, D, CW), lambda c, k: (c, 0, 0)), out_shape=_sds((NCHIP, D, CW), BF16),
        scratch_shapes=[pltpu.VMEM((D, CW), F32)],
        compiler_params=_cp("parallel", "arbitrary"), name="proj_bwd_w")(h, dproj)


def _merge_fwd(y_all, wbr, proj, tm=256):
    cb = D // NCHIP

    def body(y_ref, w_ref, *rest):
        g_refs, z_ref = rest[:8], rest[8]
        for c in range(NCHIP):
            acc = None
            for b in range(4):
                g = g_refs[2 * b + c // 2][:, (c % 2) * cb:(c % 2 + 1) * cb]
                t = _dot(y_ref[b], w_ref[c, b]) * _sigmoid(g)
                acc = t if acc is None else acc + t
            z_ref[:, c * cb:(c + 1) * cb] = acc.astype(BF16)

    g_specs = [pl.BlockSpec((tm, W), functools.partial(lambda j, i: (i, P_GM + j), j)) for j in range(8)]
    return pl.pallas_call(
        body, grid=(S // tm,),
        in_specs=[pl.BlockSpec((4, tm, W), lambda i: (0, i, 0)), pl.BlockSpec((NCHIP, 4, W, cb), lambda i: (0, 0, 0, 0))] + g_specs,
        out_specs=pl.BlockSpec((tm, D), lambda i: (i, 0)), out_shape=_sds((S, D), BF16),
        compiler_params=_cp("parallel"), name="merge_fwd")(y_all, wbr, *([proj] * 8))


def _merge_bwd(dz, y_all, wbr, proj, dproj, tm=512):
    cb = D // NCHIP
    ni = S // tm

    def body(dz_ref, y_ref, w_ref, ga_ref, gb_ref, dp_in, dp_ref, dy_ref, dw_ref, acc, obuf, osem):
        b = pl.program_id(0)
        i = pl.program_id(1)
        yv = y_ref[...]
        dys = []

        def fill(slot):
            ws_ = [w_ref[c] for c in range(NCHIP)]
            t = [_dot(yv, wv) for wv in ws_]
            dts = []
            for c in range(NCHIP):
                g_ref = ga_ref if c < 2 else gb_ref
                g = _sigmoid(g_ref[:, (c % 2) * cb:(c % 2 + 1) * cb])
                dzc = dz_ref[:, c * cb:(c + 1) * cb]
                slot[:, c * cb:(c + 1) * cb] = (dzc * t[c] * g * (1.0 - g)).astype(BF16)
                dts.append((dzc * g).astype(BF16))
            dy = None
            for c in range(NCHIP):
                part = _dot_nt(dts[c], ws_[c])
                dy = part if dy is None else dy + part
            dwp = [_dot_tn(yv, dt) for dt in dts]
            for c in range(NCHIP):
                @pl.when(i == 0)
                def _():
                    acc[c] = dwp[c]

                @pl.when(i > 0)
                def _():
                    acc[c] += dwp[c]
            dys.append(dy)

        _tile_put(obuf, osem, lambda st: dp_ref.at[pl.ds((st % ni) * tm, tm), pl.ds(P_GM * W + (st // ni) * D, D)],
                  b * ni + i, 4 * ni, fill)
        dy_ref[...] = dys[0]

        @pl.when(i == ni - 1)
        def _():
            dw_ref[...] = acc[...].astype(BF16)

    return pl.pallas_call(
        body, grid=(4, ni),
        in_specs=[pl.BlockSpec((tm, D), lambda b, i: (i, 0)), pl.BlockSpec((None, tm, W), lambda b, i: (b, i, 0)),
                  pl.BlockSpec((NCHIP, None, W, cb), lambda b, i: (0, b, 0, 0)),
                  pl.BlockSpec((tm, W), lambda b, i: (i, P_GM + 2 * b)), pl.BlockSpec((tm, W), lambda b, i: (i, P_GM + 2 * b + 1)), ANY],
        out_specs=[ANY, pl.BlockSpec((None, tm, W), lambda b, i: (b, i, 0)), pl.BlockSpec((NCHIP, None, W, cb), lambda b, i: (0, b, 0, 0))],
        out_shape=[_sds((S, DIN), BF16), _sds((4, S, W), F32), _sds((NCHIP, 4, W, cb), BF16)],
        scratch_shapes=[pltpu.VMEM((NCHIP, W, cb), F32), pltpu.VMEM((2, tm, D), BF16), pltpu.SemaphoreType.DMA((2,))],
        input_output_aliases={5: 0}, compiler_params=_cp("arbitrary", "arbitrary"), name="merge_bwd")(dz, y_all, wbr, proj, proj, dproj)


def _out_fwd(z, wo, x, tm=512):
    def body(z_ref, w_ref, x_ref, o_ref):
        o_ref[...] = x_ref[...] + _dot(z_ref[...], w_ref[...])

    row = pl.BlockSpec((tm, D), lambda i: (i, 0))
    return pl.pallas_call(
        body, grid=(S // tm,), in_specs=[row, pl.BlockSpec((D, D), lambda i: (0, 0)), row], out_specs=row,
        out_shape=_sds((S, D), F32), compiler_params=_cp("parallel"), name="out_fwd")(z, wo, x)


def _out_bwd(dx, z, wo, tm=512):
    ni = S // tm

    def body(dx_ref, z_ref, w_ref, dz_ref, dw_ref, acc):
        i = pl.program_id(0)
        dxb = dx_ref[...].astype(BF16)
        dz_ref[...] = _dot_nt(dxb, w_ref[...])
        part = _dot_tn(z_ref[...], dxb)

        @pl.when(i == 0)
        def _():
            acc[...] = part

        @pl.when(i > 0)
        def _():
            acc[...] += part

        @pl.when(i == ni - 1)
        def _():
            dw_ref[...] = acc[...].astype(BF16)

    row = pl.BlockSpec((tm, D), lambda i: (i, 0))
    full = pl.BlockSpec((D, D), lambda i: (0, 0))
    return pl.pallas_call(
        body, grid=(ni,), in_specs=[row, row, full], out_specs=[row, full],
        out_shape=[_sds((S, D), F32), _sds((D, D), BF16)], scratch_shapes=[pltpu.VMEM((D, D), F32)],
        compiler_params=_cp("arbitrary"), name="out_bwd")(dx, z, wo)


def _gelu_parts(a):
    cdf = 0.5 * (1.0 + lax.erf(a * INV_SQRT2))
    return a * cdf, cdf


def _ln_parts(v):
    mu = jnp.mean(v, axis=-1, keepdims=True)
    vc = v - mu
    rs = lax.rsqrt(jnp.mean(vc * vc, axis=-1, keepdims=True) + EPS)
    return vc * rs, rs


def _causal_mask():
    return lax.broadcasted_iota(jnp.int32, (HD, HD), 0) >= lax.broadcasted_iota(jnp.int32, (HD, HD), 1)


def _gmlp_fwd(proj, lg, lb, ws, bias, y_all, tm=512):
    def body(uv_ref, gt_ref, lg_ref, lb_ref, ws_ref, b_ref, y_in, y_ref):
        act, _ = _gelu_parts(uv_ref[...])
        u = act[:, :W]
        xh, _ = _ln_parts(act[:, W:])
        vn = (xh * lg_ref[...] + lb_ref[...]).astype(BF16)
        gt = gt_ref[...]
        us = u * (gt * _sigmoid(gt))
        mask = _causal_mask()
        for h in range(4):
            wm = jnp.where(mask, ws_ref[h], 0.0).astype(BF16)
            cs = slice(h * HD, (h + 1) * HD)
            for c in range(tm // HD):
                rs_ = slice(c * HD, (c + 1) * HD)
                mixed = _dot(wm, vn[rs_, cs]) + b_ref[h]
                y_ref[rs_, cs] = (us[rs_, cs] * mixed).astype(BF16)

    vec = pl.BlockSpec((1, W), lambda i: (0, 0))
    mats = pl.BlockSpec((4, HD, HD), lambda i: (0, 0, 0))
    return pl.pallas_call(
        body, grid=(S // tm,),
        in_specs=[pl.BlockSpec((tm, 2 * W), lambda i: (i, 0)), pl.BlockSpec((tm, W), lambda i: (i, P_AGATE)), vec, vec, mats, mats, ANY],
        out_specs=pl.BlockSpec((None, tm, W), lambda i: (0, i, 0)), out_shape=_sds((4, S, W), BF16),
        input_output_aliases={6: 0}, compiler_params=_cp("parallel"), name="gmlp_fwd")(proj, proj, lg, lb, ws, bias, y_all)


def _gmlp_bwd(proj, dy_all, lg, lb, ws, bias, dproj, tm=256):
    ni = S // tm

    def body(uv_ref, gt_ref, dy_ref, lg_ref, lb_ref, ws_ref, b_ref, dp_in, dp_ref, dws_ref, dbs_ref, dlg_ref, dlb_ref, mix_s, dvn_s):
        i = pl.program_id(0)
        a0 = uv_ref[...]
        act, cdf = _gelu_parts(a0)
        u = act[:, :W]
        xh, rs = _ln_parts(act[:, W:])
        lgv = lg_ref[...]
        vn = (xh * lgv + lb_ref[...]).astype(BF16)
        mask = _causal_mask()
        wms = [jnp.where(mask, ws_ref[h], 0.0).astype(BF16) for h in range(4)]
        for h in range(4):
            cs = slice(h * HD, (h + 1) * HD)
            for c in range(tm // HD):
                rs_ = slice(c * HD, (c + 1) * HD)
                mix_s[rs_, cs] = _dot(wms[h], vn[rs_, cs]) + b_ref[h]
        mixed = mix_s[...]
        gt = gt_ref[...]
        sg = _sigmoid(gt)
        sl = gt * sg
        dyv = dy_ref[...]
        dum = dyv * sl
        dgate = dyv * (u * mixed) * (sg * (1.0 + gt * (1.0 - sg)))
        du = dum * mixed
        dmix = dum * u
        dmb = dmix.astype(BF16)
        for h in range(4):
            cs = slice(h * HD, (h + 1) * HD)
            dw = None
            db = None
            for c in range(tm // HD):
                rs_ = slice(c * HD, (c + 1) * HD)
                dvn_s[rs_, cs] = _dot_tn(wms[h], dmb[rs_, cs])
                pw = _dot_nt(dmb[rs_, cs], vn[rs_, cs])
                dw = pw if dw is None else dw + pw
                db = dmix[rs_, cs] if db is None else db + dmix[rs_, cs]

            @pl.when(i == 0)
            def _():
                dws_ref[h] = dw
                dbs_ref[h] = db

            @pl.when(i > 0)
            def _():
                dws_ref[h] += dw
                dbs_ref[h] += db

            @pl.when(i == ni - 1)
            def _():
                dws_ref[h] = jnp.where(mask, dws_ref[h], 0.0)
                dbs_ref[h] = jnp.broadcast_to(jnp.sum(dbs_ref[h], axis=1, keepdims=True), (HD, HD))
        dvn = dvn_s[...]
        plg = jnp.sum(dvn * xh, axis=0, keepdims=True)
        plb = jnp.sum(dvn, axis=0, keepdims=True)

        @pl.when(i == 0)
        def _():
            dlg_ref[...] = plg
            dlb_ref[...] = plb

        @pl.when(i > 0)
        def _():
            dlg_ref[...] += plg
            dlb_ref[...] += plb

        dxh = dvn * lgv
        dv = rs * (dxh - jnp.mean(dxh, axis=-1, keepdims=True) - xh * jnp.mean(dxh * xh, axis=-1, keepdims=True))
        gp = cdf + a0 * (jnp.exp(-0.5 * a0 * a0) * INV_SQRT2PI)
        dp_ref[:, :W] = (du * gp[:, :W]).astype(BF16)
        dp_ref[:, W:2 * W] = (dv * gp[:, W:]).astype(BF16)
        dp_ref[:, 2 * W:] = dgate.astype(BF16)

    vec = pl.BlockSpec((1, W), lambda i: (0, 0))
    mats = pl.BlockSpec((4, HD, HD), lambda i: (0, 0, 0))
    return pl.pallas_call(
        body, grid=(ni,),
        in_specs=[pl.BlockSpec((tm, 2 * W), lambda i: (i, 0)), pl.BlockSpec((tm, W), lambda i: (i, P_AGATE)),
                  pl.BlockSpec((None, tm, W), lambda i: (0, i, 0)), vec, vec, mats, mats, ANY],
        out_specs=[pl.BlockSpec((tm, 3 * W), lambda i: (i, 0)), mats, mats, vec, vec],
        out_shape=[_sds((S, DIN), BF16), _sds((4, HD, HD), F32), _sds((4, HD, HD), F32), _sds((1, W), F32), _sds((1, W), F32)],
        scratch_shapes=[pltpu.VMEM((tm, W), F32), pltpu.VMEM((tm, W), F32)],
        input_output_aliases={7: 0}, compiler_params=_cp("arbitrary"), name="gmlp_bwd")(proj, proj, dy_all, lg, lb, ws, bias, dproj)


def _pool_diff(p, halo, row0, tm):
    xx = jnp.concatenate([halo, p], axis=0)
    t1 = (row0 + 1 + lax.broadcasted_iota(jnp.int32, (tm, 1), 0)).astype(F32)
    out = []
    for g, win in enumerate(POOL_WINDOWS):
        s = xx[:, g * HD:(g + 1) * HD]
        sh = 1
        while sh < win:
            s = s + pltpu.roll(s, sh, 0)
            sh *= 2
        out.append(s[HALO:] / jnp.minimum(t1, float(win)) - p[:, g * HD:(g + 1) * HD])
    return out


def _pool_fwd(proj, pw, sc, y_all, tm=512):
    rb = tm // HALO

    def body(p_ref, h_ref, gt_ref, pw_ref, sc_ref, y_in, y_ref):
        i = pl.program_id(0)
        halo = jnp.where(i > 0, h_ref[...], 0.0)
        ds = _pool_diff(p_ref[...], halo, i * tm, tm)
        gt = gt_ref[...]
        sl = gt * _sigmoid(gt)
        for g in range(4):
            cs = slice(g * HD, (g + 1) * HD)
            lin = _dot(ds[g].astype(BF16), pw_ref[g].astype(BF16))
            y_ref[:, cs] = (lin * sc_ref[:, cs] * sl[:, cs]).astype(BF16)

    return pl.pallas_call(
        body, grid=(S // tm,),
        in_specs=[pl.BlockSpec((tm, W), lambda i: (i, P_PIN)),
                  pl.BlockSpec((HALO, W), lambda i: (jnp.maximum(i * rb - 1, 0), P_PIN)),
                  pl.BlockSpec((tm, W), lambda i: (i, P_PGATE)),
                  pl.BlockSpec((4, HD, HD), lambda i: (0, 0, 0)), pl.BlockSpec((1, W), lambda i: (0, 0)), ANY],
        out_specs=pl.BlockSpec((None, tm, W), lambda i: (1, i, 0)), out_shape=_sds((4, S, W), BF16),
        input_output_aliases={5: 0}, compiler_params=_cp("parallel"), name="pool_fwd")(proj, proj, proj, pw, sc, y_all)


def _pool_bwd(proj, dy_all, pw, sc, dproj, tm=256):
    ni = S // tm
    rb = tm // HALO
    last_rb = S // HALO - 1
    rx = tm + HALO

    def body(p_ref, h_ref, gt_ref, gh_ref, dy_ref, dyh_ref, pw_ref, sc_ref, dp_in, dp_ref, dpw_ref, dsc_ref, obuf, osem):
        i = pl.program_id(0)
        halo = jnp.where(i > 0, h_ref[...], 0.0)
        ds = _pool_diff(p_ref[...], halo, i * tm, tm)
        nxt = i < ni - 1
        gx = jnp.concatenate([gt_ref[...], gh_ref[...]], axis=0)
        dyx = jnp.concatenate([dy_ref[...], jnp.where(nxt, dyh_ref[...], 0.0)], axis=0)
        sgx = _sigmoid(gx)
        slx = gx * sgx
        scv = sc_ref[...]
        dlinx = dyx * slx * scv
        t1 = (i * tm + 1 + lax.broadcasted_iota(jnp.int32, (rx, 1), 0)).astype(F32)
        gt, sg, sl, dyv = gx[:tm], sgx[:tm], slx[:tm], dyx[:tm]
        dsl = sg * (1.0 + gt * (1.0 - sg))

        def fill(slot):
            for g, win in enumerate(POOL_WINDOWS):
                cs = slice(g * HD, (g + 1) * HD)
                wv = pw_ref[g].astype(BF16)
                dlb = dlinx[:, cs].astype(BF16)
                ddx = _dot_nt(dlb, wv)
                f = ddx / jnp.minimum(t1, float(win))
                sh = 1
                while sh < win:
                    f = f + pltpu.roll(f, rx - sh, 0)
                    sh *= 2
                slot[:, cs] = (f[:tm] - ddx[:tm]).astype(BF16)
                db = ds[g].astype(BF16)
                lin = _dot(db, wv)
                slot[:, W + g * HD:W + (g + 1) * HD] = (dyv[:, cs] * lin * scv[:, cs] * dsl[:, cs]).astype(BF16)
                psc = jnp.sum(dyv[:, cs] * sl[:, cs] * lin, axis=0, keepdims=True)
                pwg = _dot_tn(db, dlb[:tm])

                @pl.when(i == 0)
                def _():
                    dpw_ref[g] = pwg
                    dsc_ref[:, cs] = psc

                @pl.when(i > 0)
                def _():
                    dpw_ref[g] += pwg
                    dsc_ref[:, cs] += psc

        _tile_put(obuf, osem, lambda st: dp_ref.at[pl.ds(st * tm, tm), pl.ds(P_PIN * W, 2 * W)], i, ni, fill)

    mats = pl.BlockSpec((4, HD, HD), lambda i: (0, 0, 0))
    vec = pl.BlockSpec((1, W), lambda i: (0, 0))
    return pl.pallas_call(
        body, grid=(ni,),
        in_specs=[pl.BlockSpec((tm, W), lambda i: (i, P_PIN)),
                  pl.BlockSpec((HALO, W), lambda i: (jnp.maximum(i * rb - 1, 0), P_PIN)),
                  pl.BlockSpec((tm, W), lambda i: (i, P_PGATE)),
                  pl.BlockSpec((HALO, W), lambda i: (jnp.minimum((i + 1) * rb, last_rb), P_PGATE)),
                  pl.BlockSpec((None, tm, W), lambda i: (1, i, 0)),
                  pl.BlockSpec((None, HALO, W), lambda i: (1, jnp.minimum((i + 1) * rb, last_rb), 0)),
                  mats, vec, ANY],
        out_specs=[ANY, mats, vec],
        out_shape=[_sds((S, DIN), BF16), _sds((4, HD, HD), F32), _sds((1, W), F32)],
        scratch_shapes=[pltpu.VMEM((2, tm, 2 * W), BF16), pltpu.SemaphoreType.DMA((2,))],
        input_output_aliases={8: 0}, compiler_params=_cp("arbitrary"), name="pool_bwd")(proj, proj, proj, proj, dy_all, dy_all, pw, sc, dproj)


ATT_STEP = ((1, 4), (4, 1), (4, 1))
ATT_GROUP = 16
ATT_GROUP_BWD = 8


def _att_band():
    qi = lax.broadcasted_iota(jnp.int32, (HD, 2 * HD), 0)
    kj = lax.broadcasted_iota(jnp.int32, (HD, 2 * HD), 1)
    return jnp.logical_and(kj >= qi, kj <= qi + HD), kj < HD


def _att_keys(kp_ref, ko_ref, vp_ref, vo_ref, a, jj):
    if jj == 0:
        return (jnp.concatenate([kp_ref[a], ko_ref[a, :HD, :]], axis=0), jnp.concatenate([vp_ref[a], vo_ref[a, :HD, :]], axis=0))
    return ko_ref[a, (jj - 1) * HD:(jj + 1) * HD, :], vo_ref[a, (jj - 1) * HD:(jj + 1) * HD, :]


def _dilate(src, dst, d, rows, cast=None):
    for r in range(d):
        for h in range(4):
            v = src.at[h][pl.ds(r, rows // d, stride=d), :] if d > 1 else src[h]
            dst[r * 4 + h] = v if cast is None else v.astype(cast)


def _undilate(src, dst, d, rows):
    for r in range(d):
        for h in range(4):
            if d > 1:
                dst.at[h][pl.ds(r, rows // d, stride=d), :] = src[r * 4 + h].astype(F32)
            else:
                dst[h] = src[h].astype(F32)


def _dil_spec(d, tm):
    return pl.BlockSpec((4 * d, tm // d, HD), lambda i: (0, i, 0))


def _att_prep(proj, tm=512):
    def body(q0, q1, q2, k_ref, v_ref, *rest):
        outs, scr = rest[:9], rest[9]
        for j, (src, dsts) in enumerate(((q0, ((0, outs[0]),)), (q1, ((1, outs[1]),)), (q2, ((2, outs[2]),)),
                                         (k_ref, tuple((g, outs[3 + g]) for g in range(3))),
                                         (v_ref, tuple((g, outs[6 + g]) for g in range(3))))):
            for h in range(4):
                scr[j, h] = src[:, h * HD:(h + 1) * HD]
            for g, dst in dsts:
                _dilate(scr.at[j], dst, DILATIONS[g], tm, BF16)

    def piece(p):
        return pl.BlockSpec((tm, W), lambda i: (i, p))

    shapes = [_sds((4 * d, S // d, HD), BF16) for d in DILATIONS]
    res = pl.pallas_call(
        body, grid=(S // tm,),
        in_specs=[piece(P_CQ), piece(P_CQ + 1), piece(P_CQ + 2), piece(P_CK), piece(P_CV)],
        out_specs=[_dil_spec(d, tm) for d in DILATIONS] * 3, out_shape=shapes * 3,
        scratch_shapes=[pltpu.VMEM((5, 4, tm, HD), F32)],
        compiler_params=_cp("parallel"), name="att_prep")(proj, proj, proj, proj, proj)
    return res[0:3], res[3:6], res[6:9]


def _att_specs(g):
    d = DILATIONS[g]
    nres, njb = ATT_STEP[g]
    nb = S // d // HD
    own = pl.BlockSpec((4 * nres, njb * HD, HD), lambda r, j: (r, j, 0))
    prev = pl.BlockSpec((4 * nres, HD, HD), lambda r, j: (r, jnp.maximum(j * njb - 1, 0), 0))
    nxt = pl.BlockSpec((4 * nres, HD, HD), lambda r, j: (r, jnp.minimum((j + 1) * njb, nb - 1), 0))
    return (d // nres, nb // njb), own, prev, nxt


def _att_fwd(q, k, v, g):
    d = DILATIONS[g]
    nres, njb = ATT_STEP[g]
    grid, own, prev, _ = _att_specs(g)

    def body(q_ref, kp_ref, ko_ref, vp_ref, vo_ref, o_ref, l_ref):
        jb = pl.program_id(1)
        band, is_prev = _att_band()
        no_prev = jnp.where(is_prev, jnp.where(jb > 0, 0.0, NEG), 0.0)
        blocks = [(a, jj) for jj in range(njb) for a in range(4 * nres)]
        for g0 in range(0, len(blocks), ATT_GROUP):
            grp = blocks[g0:g0 + ATT_GROUP]
            s, v2 = [], []
            for a, jj in grp:
                k2_, v2_ = _att_keys(kp_ref, ko_ref, vp_ref, vo_ref, a, jj)
                s_ = jnp.where(band, _dot_nt(q_ref[a, jj * HD:(jj + 1) * HD, :], k2_) * SCALE, NEG)
                s.append(s_ + no_prev if jj == 0 else s_)
                v2.append(v2_)
            m = [jnp.max(s_, axis=-1, keepdims=True) for s_ in s]
            e = [jnp.exp(s_ - m_) for s_, m_ in zip(s, m)]
            den = [jnp.sum(e_, axis=-1, keepdims=True) for e_ in e]
            inv = [1.0 / d_ for d_ in den]
            for i, (a, jj) in enumerate(grp):
                rs_ = slice(jj * HD, (jj + 1) * HD)
                o_ref[a, rs_, :] = _dot((e[i] * inv[i]).astype(BF16), v2[i])
                l_ref[a, rs_, :] = jnp.broadcast_to(m[i] + jnp.log(den[i]), (HD, HD))

    return pl.pallas_call(
        body, grid=grid, in_specs=[own, prev, own, prev, own], out_specs=[own, own],
        out_shape=[_sds((4 * d, S // d, HD), F32)] * 2,
        compiler_params=_cp("parallel", "parallel"), name="att_fwd")(q, k, k, v, v)


def _att_mix(os_, ls_, proj, y_all, tm=512):
    def body(o0, o1, o2, l0, l1, l2, gt_ref, y_in, y_ref, om_ref, lt_ref, so1, so2, sl1, sl2):
        _undilate(o1, so1, DILATIONS[1], tm)
        _undilate(o2, so2, DILATIONS[2], tm)
        _undilate(l1, sl1, DILATIONS[1], tm)
        _undilate(l2, sl2, DILATIONS[2], tm)
        for h in range(4):
            a, b, c = l0[h], sl1[h], sl2[h]
            m = jnp.maximum(jnp.maximum(a, b), c)
            ea, eb, ec = jnp.exp(a - m), jnp.exp(b - m), jnp.exp(c - m)
            z = ea + eb + ec
            inv = 1.0 / z
            o = (ea * inv) * o0[h] + (eb * inv) * so1[h] + (ec * inv) * so2[h]
            gt = gt_ref[:, h * HD:(h + 1) * HD]
            om_ref[h] = o
            lt_ref[h] = m + jnp.log(z)
            y_ref[:, h * HD:(h + 1) * HD] = (o * (gt * _sigmoid(gt))).astype(BF16)

    dil = [_dil_spec(d, tm) for d in DILATIONS]
    return pl.pallas_call(
        body, grid=(S // tm,),
        in_specs=dil * 2 + [pl.BlockSpec((tm, W), lambda i: (i, P_CGATE)), ANY],
        out_specs=[pl.BlockSpec((None, tm, W), lambda i: (2, i, 0)), dil[0], dil[0]],
        out_shape=[_sds((4, S, W), BF16), _sds((4, S, HD), F32), _sds((4, S, HD), F32)],
        scratch_shapes=[pltpu.VMEM((4, tm, HD), F32)] * 4,
        input_output_aliases={7: 0}, compiler_params=_cp("parallel"), name="att_mix")(*os_, *ls_, proj, y_all)


def _att_bwd_pre(dy_all, proj, om, lse, dproj, tm=512):
    def body(dy_ref, gt_ref, om_ref, ls_ref, dp_in, *rest):
        dos, dls, lss, dp_ref, sdo, sdl = rest[0:3], rest[3:6], rest[6:8], rest[8], rest[9], rest[10]
        for h in range(4):
            cs = slice(h * HD, (h + 1) * HD)
            gt = gt_ref[:, cs]
            sg = _sigmoid(gt)
            dyv = dy_ref[:, cs]
            o = om_ref[h]
            do = dyv * (gt * sg)
            dp_ref[:, cs] = (dyv * o * (sg * (1.0 + gt * (1.0 - sg)))).astype(BF16)
            sdo[h] = do
            sdl[h] = jnp.broadcast_to(jnp.sum(do * o, axis=-1, keepdims=True), (tm, HD))
        for g, d in enumerate(DILATIONS):
            _dilate(sdo, dos[g], d, tm, BF16)
            _dilate(sdl, dls[g], d, tm)
            if g > 0:
                _dilate(ls_ref, lss[g - 1], d, tm)

    dil = [_dil_spec(d, tm) for d in DILATIONS]
    gcol = pl.BlockSpec((tm, W), lambda i: (i, P_CGATE))
    res = pl.pallas_call(
        body, grid=(S // tm,),
        in_specs=[pl.BlockSpec((None, tm, W), lambda i: (2, i, 0)), gcol, dil[0], dil[0], ANY],
        out_specs=dil + dil + dil[1:] + [gcol],
        out_shape=([_sds((4 * d, S // d, HD), BF16) for d in DILATIONS] + [_sds((4 * d, S // d, HD), F32) for d in DILATIONS]
                   + [_sds((4 * d, S // d, HD), F32) for d in DILATIONS[1:]] + [_sds((S, DIN), BF16)]),
        scratch_shapes=[pltpu.VMEM((4, tm, HD), F32)] * 2,
        input_output_aliases={4: 8}, compiler_params=_cp("parallel"), name="att_bwd_pre")(dy_all, proj, om, lse, dproj)
    return res[0:3], res[3:6], [lse] + list(res[6:8]), res[8]


def _att_bwd(q, k, v, do, lse, delta, g):
    d = DILATIONS[g]
    nres, njb = ATT_STEP[g]
    grid, own, prev, nxt = _att_specs(g)

    def body(qa_ref, qn_ref, kp_ref, ko_ref, vp_ref, vo_ref, doa_ref, don_ref, la_ref, ln_ref, da_ref, dn_ref,
             dq_ref, dk_ref, dv_ref):
        jb = pl.program_id(1)
        band, is_prev = _att_band()
        m_next = lax.broadcasted_iota(jnp.int32, (HD, HD), 1) >= lax.broadcasted_iota(jnp.int32, (HD, HD), 0)
        has_prev = jnp.where(is_prev, jnp.where(jb > 0, 1.0, 0.0), 1.0)
        has_next = jnp.where(jb < grid[1] - 1, 1.0, 0.0)

        def wide(t):
            return jnp.concatenate([t, t], axis=1)

        blocks = [(a, jj) for jj in range(njb) for a in range(4 * nres)]
        for g0 in range(0, len(blocks), ATT_GROUP_BWD):
            grp = blocks[g0:g0 + ATT_GROUP_BWD]
            ops = []
            for a, jj in grp:
                rs_ = slice(jj * HD, (jj + 1) * HD)
                k2, v2 = _att_keys(kp_ref, ko_ref, vp_ref, vo_ref, a, jj)
                if jj == njb - 1:
                    qn, don, lsn, dln, fn = qn_ref[a], don_ref[a], ln_ref[a], dn_ref[a], has_next
                else:
                    ns = slice((jj + 1) * HD, (jj + 2) * HD)
                    qn, don, lsn, dln, fn = qa_ref[a, ns, :], doa_ref[a, ns, :], la_ref[a, ns, :], da_ref[a, ns, :], None
                ops.append(dict(qa=qa_ref[a, rs_, :], doa=doa_ref[a, rs_, :], lsa=wide(la_ref[a, rs_, :]),
                                dla=wide(da_ref[a, rs_, :]), k2=k2, v2=v2, ko=ko_ref[a, rs_, :], vo=vo_ref[a, rs_, :],
                                qn=qn, don=don, lsn=lsn, dln=dln, fn=fn, first=jj == 0))
            sa = [_dot_nt(o["qa"], o["k2"]) for o in ops]
            dpa = [_dot_nt(o["doa"], o["v2"]) for o in ops]
            sn = [_dot_nt(o["qn"], o["ko"]) for o in ops]
            dpn = [_dot_nt(o["don"], o["vo"]) for o in ops]
            pa, pn = [], []
            for o, sa_, sn_ in zip(ops, sa, sn):
                p_ = jnp.where(band, jnp.exp(sa_ * SCALE - o["lsa"]), 0.0)
                pa.append(p_ * has_prev if o["first"] else p_)
                p_ = jnp.where(m_next, jnp.exp(sn_ * SCALE - o["lsn"]), 0.0)
                pn.append(p_ if o["fn"] is None else p_ * o["fn"])
            dsa = [(p_ * (dp_ - o["dla"]) * SCALE).astype(BF16) for p_, dp_, o in zip(pa, dpa, ops)]
            dsn = [(p_ * (dp_ - o["dln"]) * SCALE).astype(BF16) for p_, dp_, o in zip(pn, dpn, ops)]
            for i, (a, jj) in enumerate(grp):
                rs_ = slice(jj * HD, (jj + 1) * HD)
                o = ops[i]
                dq_ref[a, rs_, :] = _dot(dsa[i], o["k2"])
                q2 = jnp.concatenate([o["qa"], o["qn"]], axis=0)
                do2 = jnp.concatenate([o["doa"], o["don"]], axis=0)
                dk_ref[a, rs_, :] = _dot_tn(jnp.concatenate([dsa[i][:, HD:], dsn[i]], axis=0), q2)
                dv_ref[a, rs_, :] = _dot_tn(jnp.concatenate([pa[i][:, HD:].astype(BF16), pn[i].astype(BF16)], axis=0), do2)

    return pl.pallas_call(
        body, grid=grid, in_specs=[own, nxt, prev, own, prev, own, own, nxt, own, nxt, own, nxt],
        out_specs=[own, own, own], out_shape=[_sds((4 * d, S // d, HD), F32)] * 3,
        compiler_params=_cp("parallel", "parallel"), name="att_bwd")(q, q, k, k, v, v, do, do, lse, lse, delta, delta)


def _att_bwd_post(dqs, dks, dvs, dproj, tm=512):
    def body(*refs):
        dq, dk, dv, dp_ref, scr = refs[0:3], refs[3:6], refs[6:9], refs[10], refs[11]
        for g in range(3):
            _undilate(dq[g], scr, DILATIONS[g], tm)
            for h in range(4):
                dp_ref[:, g * W + h * HD:g * W + (h + 1) * HD] = scr[h].astype(BF16)
        for j, parts in enumerate((dk, dv)):
            acc = None
            for g in range(3):
                _undilate(parts[g], scr, DILATIONS[g], tm)
                vals = [scr[h] for h in range(4)]
                acc = vals if acc is None else [x + y for x, y in zip(acc, vals)]
            for h in range(4):
                dp_ref[:, (3 + j) * W + h * HD:(3 + j) * W + (h + 1) * HD] = acc[h].astype(BF16)

    dil = [_dil_spec(d, tm) for d in DILATIONS]
    return pl.pallas_call(
        body, grid=(S // tm,), in_specs=dil * 3 + [ANY],
        out_specs=pl.BlockSpec((tm, 5 * W), lambda i: (i, 1)), out_shape=_sds((S, DIN), BF16),
        scratch_shapes=[pltpu.VMEM((4, tm, HD), F32)],
        input_output_aliases={9: 0}, compiler_params=_cp("parallel"), name="att_bwd_post")(*dqs, *dks, *dvs, dproj)


def _mem_kv_fwd(mem_n, wkv):
    m = mem_n.shape[0]

    def body(a_ref, w_ref, o_ref):
        o_ref[...] = _dot(a_ref[...], w_ref[...])

    return pl.pallas_call(body, out_shape=_sds((m, 2 * W), F32), compiler_params=_cp(), name="mem_kv_fwd")(mem_n, wkv)


def _mem_fwd(proj, kv, y_all, tm=512):
    m = kv.shape[0]

    def body(q_ref, gt_ref, kv_ref, y_in, y_ref):
        gt = gt_ref[...]
        sl = gt * _sigmoid(gt)
        hs = [slice(h * HD, (h + 1) * HD) for h in range(4)]
        s = [_dot_nt(q_ref[:, cs].astype(BF16), kv_ref[:, cs].astype(BF16)) * SCALE for cs in hs]
        e = [jnp.exp(s_ - jnp.max(s_, axis=-1, keepdims=True)) for s_ in s]
        p = [(e_ * (1.0 / jnp.sum(e_, axis=-1, keepdims=True))).astype(BF16) for e_ in e]
        for h, cs in enumerate(hs):
            o = _dot(p[h], kv_ref[:, W + h * HD:W + (h + 1) * HD].astype(BF16))
            y_ref[:, cs] = (o * sl[:, cs]).astype(BF16)

    return pl.pallas_call(
        body, grid=(S // tm,),
        in_specs=[pl.BlockSpec((tm, W), lambda i: (i, P_MQ)), pl.BlockSpec((tm, W), lambda i: (i, P_MGATE)),
                  pl.BlockSpec((m, 2 * W), lambda i: (0, 0)), ANY],
        out_specs=pl.BlockSpec((None, tm, W), lambda i: (3, i, 0)), out_shape=_sds((4, S, W), BF16),
        input_output_aliases={3: 0}, compiler_params=_cp("parallel"), name="mem_fwd")(proj, proj, kv, y_all)


def _mem_bwd(proj, kv, dy_all, dproj, tm=512):
    m = kv.shape[0]
    ni = S // tm

    def body(q_ref, gt_ref, kv_ref, dy_ref, dp_in, dp_ref, dkv_ref, obuf, osem):
        i = pl.program_id(0)
        gt = gt_ref[...]
        sg = _sigmoid(gt)
        sl = gt * sg
        dsl = sg * (1.0 + gt * (1.0 - sg))
        dyv = dy_ref[...]

        def fill(slot):
            hs = [slice(h * HD, (h + 1) * HD) for h in range(4)]
            vss = [slice(W + h * HD, W + (h + 1) * HD) for h in range(4)]
            q = [q_ref[:, cs].astype(BF16) for cs in hs]
            k = [kv_ref[:, cs].astype(BF16) for cs in hs]
            v = [kv_ref[:, vs].astype(BF16) for vs in vss]
            dob = [(dyv[:, cs] * sl[:, cs]).astype(BF16) for cs in hs]
            s = [_dot_nt(q_, k_) * SCALE for q_, k_ in zip(q, k)]
            dp = [_dot_nt(d_, v_) for d_, v_ in zip(dob, v)]
            e = [jnp.exp(s_ - jnp.max(s_, axis=-1, keepdims=True)) for s_ in s]
            p = [e_ * (1.0 / jnp.sum(e_, axis=-1, keepdims=True)) for e_ in e]
            pb = [p_.astype(BF16) for p_ in p]
            dsb = [(p_ * (dp_ - jnp.sum(dp_ * p_, axis=-1, keepdims=True)) * SCALE).astype(BF16) for p_, dp_ in zip(p, dp)]
            for h, (cs, vs) in enumerate(zip(hs, vss)):
                o = _dot(pb[h], v[h])
                slot[:, cs] = _dot(dsb[h], k[h]).astype(BF16)
                slot[:, vs] = (dyv[:, cs] * o * dsl[:, cs]).astype(BF16)
                dk = _dot_tn(dsb[h], q[h])
                dv = _dot_tn(pb[h], dob[h])

                @pl.when(i == 0)
                def _():
                    dkv_ref[:, cs] = dk
                    dkv_ref[:, vs] = dv

                @pl.when(i > 0)
                def _():
                    dkv_ref[:, cs] += dk
                    dkv_ref[:, vs] += dv

        _tile_put(obuf, osem, lambda st: dp_ref.at[pl.ds(st * tm, tm), pl.ds(P_MQ * W, 2 * W)], i, ni, fill)

    return pl.pallas_call(
        body, grid=(ni,),
        in_specs=[pl.BlockSpec((tm, W), lambda i: (i, P_MQ)), pl.BlockSpec((tm, W), lambda i: (i, P_MGATE)),
                  pl.BlockSpec((m, 2 * W), lambda i: (0, 0)), pl.BlockSpec((None, tm, W), lambda i: (3, i, 0)), ANY],
        out_specs=[ANY, pl.BlockSpec((m, 2 * W), lambda i: (0, 0))],
        out_shape=[_sds((S, DIN), BF16), _sds((m, 2 * W), F32)],
        scratch_shapes=[pltpu.VMEM((2, tm, 2 * W), BF16), pltpu.SemaphoreType.DMA((2,))],
        input_output_aliases={4: 0}, compiler_params=_cp("arbitrary"), name="mem_bwd")(proj, proj, kv, dy_all, dproj)


def _mem_kv_bwd(mem, g, mem_n, wkv, dkv):
    m = mem.shape[0]

    def body(x_ref, g_ref, a_ref, w_ref, d_ref, dw_ref, dg_ref):
        db = d_ref[...].astype(BF16)
        dw_ref[...] = _dot_tn(a_ref[...], db).astype(BF16)
        dn = _dot_nt(db, w_ref[...])
        xv = x_ref[...]
        xh = xv * lax.rsqrt(jnp.mean(xv * xv, axis=-1, keepdims=True) + EPS)
        dg_ref[...] = jnp.sum(dn * xh, axis=0, keepdims=True)

    return pl.pallas_call(
        body, out_shape=[_sds((D, 2 * W), BF16), _sds((1, D), F32)], compiler_params=_cp(), name="mem_kv_bwd")(mem, g, mem_n, wkv, dkv)


def _layer_fwd(x, h, mem_n, p, wg):
    win, wkv, wbr, wo = wg
    proj = _proj_fwd(h, win)
    y_all = lax.empty((4, S, W), BF16)
    y_all = _gmlp_fwd(proj, p["gm_ln_g"], p["gm_ln_b"], p["gm_ws"], p["gm_bias"], y_all)
    y_all = _pool_fwd(proj, p["pool_w"], p["pool_scale"], y_all)
    qs, ks, vs = _att_prep(proj)
    os_, ls_ = zip(*[_att_fwd(qs[g], ks[g], vs[g], g) for g in range(3)])
    y_all, om, lse = _att_mix(os_, ls_, proj, y_all)
    kv = _mem_kv_fwd(mem_n, wkv)
    y_all = _mem_fwd(proj, kv, y_all)
    z = _merge_fwd(y_all, wbr, proj)
    x_new = _out_fwd(z, wo, x)
    return x_new, dict(x=x, h=h, proj=proj, y_all=y_all, om=om, lse=lse, mem_n=mem_n, kv=kv, z=z, qkv=(qs, ks, vs))


GRAD_PARTS = ((2, D // 2, CW), (2, D // 8, 2 * W), (2, 2 * W, D // NCHIP), (2, D // 8, D))
SUM_TILE = (64, 128, 256, 128)
ADAM_TILE = (128, 256, 2048, 256)
PLACE_TILE = (256, 256, 512, 256)


def _layer_bwd(dx, mem, p, wg, sv, exchange):
    win, wkv, wbr, wo = wg
    proj = sv["proj"]
    dz, d_wo = _out_bwd(dx, sv["z"], wo)
    dproj = lax.empty((S, DIN), BF16)
    dproj, dy_all, d_wbr = _merge_bwd(dz, sv["y_all"], wbr, proj, dproj)
    dproj, d_ws, d_bs, d_lg, d_lb = _gmlp_bwd(proj, dy_all, p["gm_ln_g"], p["gm_ln_b"], p["gm_ws"], p["gm_bias"], dproj)
    dproj, d_pw, d_sc = _pool_bwd(proj, dy_all, p["pool_w"], p["pool_scale"], dproj)
    dos, dls, lss, dproj = _att_bwd_pre(dy_all, proj, sv["om"], sv["lse"], dproj)
    qs, ks, vs = sv["qkv"]
    dqs, dks, dvs = zip(*[_att_bwd(qs[g], ks[g], vs[g], dos[g], lss[g], dls[g], g) for g in range(3)])
    dproj = _att_bwd_post(dqs, dks, dvs, dproj)
    dproj, dkv = _mem_bwd(proj, sv["kv"], dy_all, dproj)
    d_wkv, d_mg = _mem_kv_bwd(mem, p["mem_norm_g"], sv["mem_n"], wkv, dkv)
    d_win = _proj_bwd_w(sv["h"], dproj)
    big = tuple(t.reshape((NCHIP,) + s) for t, s in zip((d_win, d_wkv, d_wbr, d_wo), GRAD_PARTS))
    inflight = exchange(big)
    dh = _proj_bwd_x(dproj, win, inflight[-1])
    dx_in, d_ng = _rms_bwd(dh, sv["x"], p["norm_g"], dx)
    small = dict(norm_g=d_ng, gm_ln_g=d_lg, gm_ln_b=d_lb, gm_ws=d_ws, gm_bs=d_bs[:, :, 0], pool_w=d_pw, pool_scale=d_sc, mem_norm_g=d_mg)
    return dx_in, (big,) + inflight, small


_SMALL = ("norm_g", "gm_ln_g", "gm_ln_b", "gm_ws", "gm_bs", "pool_w", "pool_scale", "mem_norm_g")


def _layer_params(l, norm_g, gm_ln_g, gm_ln_b, gm_ws, gm_bs, pool_w, pool_scale, mem_norm_g):
    return dict(norm_g=norm_g[l][None], gm_ln_g=gm_ln_g[l][None], gm_ln_b=gm_ln_b[l][None], gm_ws=gm_ws[l],
                gm_bias=jnp.broadcast_to(gm_bs[l][:, :, None], (4, HD, HD)), pool_w=pool_w[l],
                pool_scale=pool_scale[l][None], mem_norm_g=mem_norm_g[l][None])


def kernel(x, mem, norm_g, w_in, gm_ln_g, gm_ln_b, gm_ws, gm_bs, pool_w, pool_scale, mem_norm_g, w_mem_kv, w_branch, w_out, final_norm_g, loss_target, m_norm_g, m_w_in, m_gm_ln_g, m_gm_ln_b, m_gm_ws, m_gm_bs, m_pool_w, m_pool_scale, m_mem_norm_g, m_w_mem_kv, m_w_branch, m_w_out, m_final_norm_g, v_norm_g, v_w_in, v_gm_ln_g, v_gm_ln_b, v_gm_ws, v_gm_bs, v_pool_w, v_pool_scale, v_mem_norm_g, v_w_mem_kv, v_w_branch, v_w_out, v_final_norm_g):
    xs, memv, tgt = x[0], mem[0], loss_target[0]
    params = [_layer_params(l, norm_g, gm_ln_g, gm_ln_b, gm_ws, gm_bs, pool_w, pool_scale, mem_norm_g) for l in range(NL)]

    shards = (w_in.astype(BF16), w_mem_kv.astype(BF16), w_branch.astype(BF16), w_out.astype(BF16))
    lands, gsems, after = _gather_start(shards)

    saved, wgs = [], []
    mem_ns = [_rms_fwd(memv, params[l]["mem_norm_g"], memv.shape[0]) for l in range(NL)]
    for l in range(NL):
        h = _rms_fwd(xs, params[l]["norm_g"], 512)
        if l == 0:
            got, relay_sems = _gather_relay(shards, lands[0], gsems[0], [after, h] + mem_ns)
            got = _gather_wait_relay(got, relay_sems)
        else:
            got = _gather_wait(l, shards, lands[l], gsems[l], after)
        got = [_place_own(got[k], shards[k], l, PLACE_TILE[k]) for k in range(4)]
        wgs.append((got[0], got[1].reshape(D, 2 * W), got[2], got[3].reshape(D, D)))
        xs, sv = _layer_fwd(xs, h, mem_ns[l], params[l], wgs[l])
        saved.append(sv)
        after = xs
    dx, d_fg, ls = _loss_head(xs, tgt, final_norm_g[None])
    loss = lax.psum(ls[0, 0], ("x", "y", "c"))

    flight, small = [None] * NL, [None] * NL
    for l in reversed(range(NL)):
        dx, flight[l], small[l] = _layer_bwd(dx, memv, params[l], wgs[l], saved[l], functools.partial(_exch_start, l))
    grad_x = dx[None]

    leaves = [jnp.stack([small[l][n] for l in range(NL)]) for n in _SMALL] + [d_fg]
    sizes = [t.size for t in leaves]
    packed = jnp.concatenate([t.reshape(-1, 128) for t in leaves], axis=0)
    rows = packed.shape[0]
    small_zone, small_sems, after = _small_start(packed)

    ws = dict(norm_g=norm_g, w_in=w_in, gm_ln_g=gm_ln_g, gm_ln_b=gm_ln_b, gm_ws=gm_ws, gm_bs=gm_bs, pool_w=pool_w,
              pool_scale=pool_scale, mem_norm_g=mem_norm_g, w_mem_kv=w_mem_kv, w_branch=w_branch, w_out=w_out,
              final_norm_g=final_norm_g)
    ms = dict(norm_g=m_norm_g, w_in=m_w_in, gm_ln_g=m_gm_ln_g, gm_ln_b=m_gm_ln_b, gm_ws=m_gm_ws, gm_bs=m_gm_bs,
              pool_w=m_pool_w, pool_scale=m_pool_scale, mem_norm_g=m_mem_norm_g, w_mem_kv=m_w_mem_kv,
              w_branch=m_w_branch, w_out=m_w_out, final_norm_g=m_final_norm_g)
    vs = dict(norm_g=v_norm_g, w_in=v_w_in, gm_ln_g=v_gm_ln_g, gm_ln_b=v_gm_ln_b, gm_ws=v_gm_ws, gm_bs=v_gm_bs,
              pool_w=v_pool_w, pool_scale=v_pool_scale, mem_norm_g=v_mem_norm_g, w_mem_kv=v_w_mem_kv,
              w_branch=v_w_branch, w_out=v_w_out, final_norm_g=v_final_norm_g)

    big = ("w_in", "w_mem_kv", "w_branch", "w_out")
    acc = {n: [lax.empty((ws[n].size // ws[n].shape[-1], ws[n].shape[-1]), F32) for _ in range(4)] for n in big}
    for l in reversed(range(NL)):
        parts, zones, sems, _ = flight[l]
        zones = _exch_wait(l, parts, zones, sems, after)
        full = _share_full([_sum_half(parts[k], zones[k], SUM_TILE[k]) for k in range(4)])
        for k, n in enumerate(big):
            acc[n] = _adamw_layer(l, ws[n], full[k], ms[n], vs[n], acc[n], ADAM_TILE[k])
        after = acc["w_in"][0]
    grads, upd = {}, {}
    for n in big:
        d_, m_, v_, g_ = (t.reshape(ws[n].shape) for t in acc[n])
        grads[n], upd[n] = g_, (d_, m_, v_)

    small_zone = _small_wait(packed, small_zone, small_sems, after)
    tot = _sum_small(packed, small_zone, max(t for t in range(8, 513, 8) if rows % t == 0))
    offs = [0]
    for sz in sizes:
        offs.append(offs[-1] + sz // 128)
    for i, n in enumerate(_SMALL + ("final_norm_g",)):
        grads[n] = tot[offs[i]:offs[i + 1]].reshape(ws[n].shape)
        upd[n] = _adamw(ws[n], grads[n], ms[n], vs[n])
    order = ("norm_g", "w_in", "gm_ln_g", "gm_ln_b", "gm_ws", "gm_bs", "pool_w", "pool_scale", "mem_norm_g", "w_mem_kv",
             "w_branch", "w_out", "final_norm_g")
    return (loss, grad_x, *[grads[n] for n in order], *[upd[n][0] for n in order], *[upd[n][1] for n in order],
            *[upd[n][2] for n in order])
```

```python
import functools
import math

import jax
import jax.numpy as jnp
from jax import lax
from jax.experimental import pallas as pl
from jax.experimental.pallas import tpu as pltpu

F32 = jnp.float32
BF16 = jnp.bfloat16

S = 4096
D = 1024
W = 512
DIN = 10752
NL = 4
NCHIP = 4
NDEV = 8
CW = DIN // NCHIP
TN_IN = 896
NJ = CW // TN_IN
HD = 128
EPS = 1e-6
NEG = -1e30
SCALE = HD ** -0.5
INV_SQRT2 = 1.0 / math.sqrt(2.0)
INV_SQRT2PI = 1.0 / math.sqrt(2.0 * math.pi)
POOL_WINDOWS = (2, 4, 8, 16)
DILATIONS = (1, 4, 16)
HALO = 16
NPIECE = DIN // W
P_AGATE, P_PIN, P_PGATE, P_CQ, P_CK, P_CV, P_CGATE, P_MQ, P_MGATE, P_GM = 2, 3, 4, 5, 8, 9, 10, 11, 12, 13
VMEM_LIMIT = 56 * 1024 * 1024

ADAM_LR, ADAM_B1, ADAM_B2, ADAM_EPS, ADAM_WD, ADAM_STEP = 0.001, 0.9, 0.999, 1e-08, 0.01, 10

MESH = pl.DeviceIdType.MESH
ANY = pl.BlockSpec(memory_space=pl.ANY)


def _cp(*sem):
    return pltpu.CompilerParams(dimension_semantics=sem or None, vmem_limit_bytes=VMEM_LIMIT)


def _sds(shape, dtype):
    return jax.ShapeDtypeStruct(shape, dtype)


def _sigmoid(v):
    return 1.0 / (1.0 + jnp.exp(-v))


def _dot(a, b):
    return jnp.dot(a, b, preferred_element_type=F32)


def _dot_nt(a, b):
    return lax.dot_general(a, b, (((1,), (1,)), ((), ())), preferred_element_type=F32)


def _dot_tn(a, b):
    return lax.dot_general(a, b, (((0,), (0,)), ((), ())), preferred_element_type=F32)


def _tile_put(buf, sem, dst_of, step, nsteps, fill):
    slot = step % 2

    def copy(s, st):
        return pltpu.make_async_copy(buf.at[s], dst_of(st), sem.at[s])

    @pl.when(step >= 2)
    def _():
        copy(slot, step).wait()

    fill(buf.at[slot])
    copy(slot, step).start()

    @pl.when(step == nsteps - 1)
    def _():
        if nsteps >= 2:
            copy(1 - slot, step).wait()
        copy(slot, step).wait()


def _my_pos():
    return lax.axis_index("x"), lax.axis_index("y"), lax.axis_index("c")


_CHIP_REL = ((1, 0), (0, 1), (1, 1))
_DEV_REL = tuple((dx, dy, dc) for dx in (0, 1) for dy in (0, 1) for dc in (0, 1))[1:]


HBM = pl.BlockSpec(memory_space=pltpu.HBM)
SEM = pl.BlockSpec(memory_space=pltpu.SEMAPHORE)
EFFECT = pltpu.SideEffectType.DATAFLOW_SIDE_EFFECTING
N_GATHER = 3 * 4
N_EXCH = 7 * 4


def _in_hbm(t):
    return pltpu.with_memory_space_constraint(t, pltpu.HBM)


def _gather_start(shards):
    nk = len(shards)
    lands = [pltpu.HBM((NCHIP,) + s.shape[1:], BF16) for s in shards for _ in range(NL)]

    def body(*refs):
        ins, outs = refs[:nk], refs[nk:nk + nk * NL]
        sems = refs[nk + nk * NL:nk + nk * NL + 2 * NL]
        token = refs[-1]
        x, y, c = _my_pos()
        me = 2 * x + y
        for l in range(NL):
            for r, (dx, dy) in enumerate(_CHIP_REL):
                for k in range(nk):
                    src, dst = ins[k].at[l], outs[k * NL + l].at[me]
                    if l == 0:
                        hf = pl.ds(c * (shards[k].shape[1] // 2), shards[k].shape[1] // 2)
                        src, dst = src.at[hf], dst.at[hf]
                    pltpu.make_async_remote_copy(
                        src_ref=src, dst_ref=dst, send_sem=sems[2 * l].at[r * nk + k],
                        recv_sem=sems[2 * l + 1].at[r * nk + k], device_id=(x ^ dx, y ^ dy, c), device_id_type=MESH).start()
        token[...] = jnp.zeros_like(token)

    res = pl.pallas_call(
        body, name="gather_start",
        out_shape=lands + [pltpu.SemaphoreType.DMA((N_GATHER,))] * (2 * NL) + [_sds((8, 128), F32)],
        in_specs=[HBM] * nk, out_specs=[HBM] * (nk * NL) + [SEM] * (2 * NL) + [pl.BlockSpec(memory_space=pltpu.VMEM)],
        compiler_params=pltpu.CompilerParams(has_side_effects=EFFECT))(*[_in_hbm(s) for s in shards])
    lands = [[res[k * NL + l] for k in range(nk)] for l in range(NL)]
    sems = [(res[nk * NL + 2 * l], res[nk * NL + 2 * l + 1]) for l in range(NL)]
    return lands, sems, res[-1]


def _gather_relay(shards, lands, sems, after):
    nk = len(shards)
    half = [s.shape[1] // 2 for s in shards]

    na = len(after)

    def body(*refs):
        ins, land = refs[:nk], refs[nk:2 * nk]
        send, recv = refs[2 * nk], refs[2 * nk + 1]
        send2, recv2 = refs[3 * nk + 2 + na], refs[3 * nk + 3 + na]
        x, y, c = _my_pos()
        for r, (dx, dy) in enumerate(_CHIP_REL):
            cx, cy = x ^ dx, y ^ dy
            for k in range(nk):
                hf = pl.ds(c * half[k], half[k])
                got = land[k].at[2 * cx + cy].at[hf]
                cp = pltpu.make_async_remote_copy(
                    src_ref=ins[k].at[0].at[hf], dst_ref=got, send_sem=send.at[r * nk + k],
                    recv_sem=recv.at[r * nk + k], device_id=(cx, cy, c), device_id_type=MESH)
                cp.wait_send()
                cp.wait_recv()
                pltpu.make_async_remote_copy(
                    src_ref=got, dst_ref=got, send_sem=send2.at[r * nk + k], recv_sem=recv2.at[r * nk + k],
                    device_id=(x, y, 1 - c), device_id_type=MESH).start()

    res = pl.pallas_call(
        body, name="gather_relay",
        out_shape=[pltpu.HBM(t.shape, t.dtype) for t in lands] + [pltpu.SemaphoreType.DMA((N_GATHER,))] * 2,
        in_specs=[ANY] * nk + [HBM] * nk + [SEM, SEM] + [ANY] * na, out_specs=[HBM] * nk + [SEM, SEM],
        input_output_aliases={nk + k: k for k in range(nk)},
        compiler_params=pltpu.CompilerParams(has_side_effects=EFFECT))(*shards, *lands, *sems, *after)
    return res[:nk], (res[nk], res[nk + 1])


def _gather_wait_relay(lands, sems):
    nk = len(lands)
    half = [t.shape[1] // 2 for t in lands]

    def body(*refs):
        land = refs[:nk]
        send, recv = refs[nk], refs[nk + 1]
        x, y, c = _my_pos()
        for r, (dx, dy) in enumerate(_CHIP_REL):
            chip = 2 * (x ^ dx) + (y ^ dy)
            for k in range(nk):
                mine = land[k].at[chip].at[pl.ds(c * half[k], half[k])]
                theirs = land[k].at[chip].at[pl.ds((1 - c) * half[k], half[k])]
                cp = pltpu.make_async_remote_copy(
                    src_ref=mine, dst_ref=theirs, send_sem=send.at[r * nk + k], recv_sem=recv.at[r * nk + k],
                    device_id=(x, y, 1 - c), device_id_type=MESH)
                cp.wait_send()
                cp.wait_recv()

    return pl.pallas_call(
        body, name="gather_wait_relay", out_shape=[pltpu.HBM(t.shape, t.dtype) for t in lands],
        in_specs=[HBM] * nk + [SEM, SEM], out_specs=[HBM] * nk, input_output_aliases={k: k for k in range(nk)},
        compiler_params=pltpu.CompilerParams(has_side_effects=EFFECT))(*lands, *sems)


def _gather_wait(l, shards, lands, sems, after):
    nk = len(shards)

    def body(*refs):
        ins, land = refs[:nk], refs[nk:2 * nk]
        send, recv = refs[2 * nk], refs[2 * nk + 1]
        x, y, c = _my_pos()
        for r, (dx, dy) in enumerate(_CHIP_REL):
            cx, cy = x ^ dx, y ^ dy
            for k in range(nk):
                cp = pltpu.make_async_remote_copy(
                    src_ref=ins[k].at[l], dst_ref=land[k].at[2 * cx + cy], send_sem=send.at[r * nk + k],
                    recv_sem=recv.at[r * nk + k], device_id=(cx, cy, c), device_id_type=MESH)
                cp.wait_send()
                cp.wait_recv()

    return pl.pallas_call(
        body, name=f"gather_wait_{l}", out_shape=[pltpu.HBM(t.shape, t.dtype) for t in lands],
        in_specs=[ANY] * nk + [HBM] * nk + [SEM, SEM, ANY], out_specs=[HBM] * nk,
        input_output_aliases={nk + k: k for k in range(nk)},
        compiler_params=pltpu.CompilerParams(has_side_effects=EFFECT))(*shards, *lands, *sems, after)


def _place_own(land, shard, l, tr):
    _, rows, cols = shard.shape[0], shard.shape[-2], shard.shape[-1]
    lead = shard.shape[1:-2]
    nlead = math.prod(lead)
    sh = shard.reshape((NL, nlead, rows, cols))
    ld = land.reshape((NCHIP, nlead, rows, cols))
    me = (2 * lax.axis_index("x") + lax.axis_index("y")).astype(jnp.int32).reshape(1)

    def body(me_ref, s_ref, l_in, o_ref):
        o_ref[...] = s_ref[...]

    out = pl.pallas_call(
        body,
        grid_spec=pltpu.PrefetchScalarGridSpec(
            num_scalar_prefetch=1, grid=(nlead, rows // tr),
            in_specs=[pl.BlockSpec((None, None, tr, cols), lambda b, i, me_ref: (l, b, i, 0)), ANY],
            out_specs=pl.BlockSpec((None, None, tr, cols), lambda b, i, me_ref: (me_ref[0], b, i, 0))),
        out_shape=_sds(ld.shape, BF16), input_output_aliases={2: 0},
        compiler_params=_cp("parallel", "parallel"), name="place_own")(me, sh, ld)
    return out.reshape(land.shape)


def _exch_start(l, parts):
    nk = len(parts)

    def body(*refs):
        ins, outs = refs[:nk], refs[nk:2 * nk]
        send, recv, token = refs[2 * nk:]
        x, y, c = _my_pos()
        for r, (dx, dy, dc) in enumerate(_DEV_REL):
            px, py, pc = x ^ dx, y ^ dy, c ^ dc
            for k in range(nk):
                pltpu.make_async_remote_copy(
                    src_ref=ins[k].at[2 * px + py, pc], dst_ref=outs[k].at[r], send_sem=send.at[r * nk + k],
                    recv_sem=recv.at[r * nk + k], device_id=(px, py, pc), device_id_type=MESH).start()
        token[...] = jnp.zeros_like(token)

    res = pl.pallas_call(
        body, name=f"exch_start_{l}",
        out_shape=[pltpu.HBM((7,) + p.shape[2:], BF16) for p in parts] + [pltpu.SemaphoreType.DMA((N_EXCH,))] * 2 + [_sds((8, 128), F32)],
        in_specs=[HBM] * nk, out_specs=[HBM] * nk + [SEM, SEM, pl.BlockSpec(memory_space=pltpu.VMEM)],
        compiler_params=pltpu.CompilerParams(has_side_effects=EFFECT))(*[_in_hbm(p) for p in parts])
    return res[:nk], (res[nk], res[nk + 1]), res[-1]


def _exch_wait(l, parts, lands, sems, after):
    nk = len(parts)

    def body(*refs):
        ins, land = refs[:nk], refs[nk:2 * nk]
        send, recv = refs[2 * nk], refs[2 * nk + 1]
        x, y, c = _my_pos()
        for r, (dx, dy, dc) in enumerate(_DEV_REL):
            px, py, pc = x ^ dx, y ^ dy, c ^ dc
            for k in range(nk):
                cp = pltpu.make_async_remote_copy(
                    src_ref=ins[k].at[2 * px + py, pc], dst_ref=land[k].at[r], send_sem=send.at[r * nk + k],
                    recv_sem=recv.at[r * nk + k], device_id=(px, py, pc), device_id_type=MESH)
                cp.wait_send()
                cp.wait_recv()

    return pl.pallas_call(
        body, name=f"exch_wait_{l}", out_shape=[pltpu.HBM(t.shape, t.dtype) for t in lands],
        in_specs=[ANY] * nk + [HBM] * nk + [SEM, SEM, ANY], out_specs=[HBM] * nk,
        input_output_aliases={nk + k: k for k in range(nk)},
        compiler_params=pltpu.CompilerParams(has_side_effects=EFFECT))(*parts, *lands, *sems, after)


def _chip_half():
    x, y, c = _my_pos()
    return jnp.stack([2 * x + y, c]).astype(jnp.int32)


def _sum_half(part, land, tr):
    _, _, r2, cols = part.shape

    def body(pos_ref, p_ref, r_ref, o_ref):
        acc = p_ref[...].astype(F32)
        for r in range(7):
            acc = acc + r_ref[r].astype(F32)
        o_ref[...] = acc

    return pl.pallas_call(
        body,
        grid_spec=pltpu.PrefetchScalarGridSpec(
            num_scalar_prefetch=1, grid=(r2 // tr,),
            in_specs=[pl.BlockSpec((None, None, tr, cols), lambda i, pos: (pos[0], pos[1], i, 0)),
                      pl.BlockSpec((7, tr, cols), lambda i, pos: (0, i, 0))],
            out_specs=pl.BlockSpec((None, tr, cols), lambda i, pos: (pos[1], i, 0))),
        out_shape=_sds((2, r2, cols), F32), compiler_params=_cp("parallel"), name="sum_half")(_chip_half(), part, land)


def _share_full(fulls):
    nk = len(fulls)

    def body(*refs):
        ins, outs = refs[:nk], refs[nk:2 * nk]
        send, recv = refs[2 * nk:]
        x, y, c = _my_pos()

        def copy(k, hf):
            return pltpu.make_async_remote_copy(
                src_ref=ins[k].at[hf], dst_ref=outs[k].at[hf], send_sem=send.at[k], recv_sem=recv.at[k],
                device_id=(x, y, 1 - c), device_id_type=MESH)

        for k in range(nk):
            copy(k, c).start()
        for k in range(nk):
            copy(k, 1 - c).wait_recv()
        for k in range(nk):
            copy(k, c).wait_send()

    return pl.pallas_call(
        body, out_shape=[_sds(f.shape, F32) for f in fulls], in_specs=[ANY] * nk, out_specs=[ANY] * nk,
        scratch_shapes=[pltpu.SemaphoreType.DMA((nk,))] * 2, input_output_aliases={k: k for k in range(nk)},
        name="share_full")(*fulls)


def _small_start(packed):
    def body(in_ref, out_ref, send, recv, token):
        x, y, c = _my_pos()
        for r, (dx, dy, dc) in enumerate(_DEV_REL):
            pltpu.make_async_remote_copy(
                src_ref=in_ref, dst_ref=out_ref.at[r], send_sem=send.at[r], recv_sem=recv.at[r],
                device_id=(x ^ dx, y ^ dy, c ^ dc), device_id_type=MESH).start()
        token[...] = jnp.zeros_like(token)

    res = pl.pallas_call(
        body, name="small_start",
        out_shape=[pltpu.HBM((7,) + packed.shape, F32)] + [pltpu.SemaphoreType.DMA((7,))] * 2 + [_sds((8, 128), F32)],
        in_specs=[HBM], out_specs=[HBM, SEM, SEM, pl.BlockSpec(memory_space=pltpu.VMEM)],
        compiler_params=pltpu.CompilerParams(has_side_effects=EFFECT))(_in_hbm(packed))
    return res[0], (res[1], res[2]), res[3]


def _small_wait(packed, land, sems, after):
    def body(in_ref, land_ref, send, recv, after_ref, out_ref):
        x, y, c = _my_pos()
        for r, (dx, dy, dc) in enumerate(_DEV_REL):
            cp = pltpu.make_async_remote_copy(
                src_ref=in_ref, dst_ref=land_ref.at[r], send_sem=send.at[r], recv_sem=recv.at[r],
                device_id=(x ^ dx, y ^ dy, c ^ dc), device_id_type=MESH)
            cp.wait_send()
            cp.wait_recv()

    return pl.pallas_call(
        body, name="small_wait", out_shape=pltpu.HBM(land.shape, land.dtype),
        in_specs=[ANY, HBM, SEM, SEM, ANY], out_specs=HBM, input_output_aliases={1: 0},
        compiler_params=pltpu.CompilerParams(has_side_effects=EFFECT))(packed, land, *sems, after)


def _sum_small(packed, land, tr):
    rows = packed.shape[0]
    x, y, c = _my_pos()
    me = (4 * x + 2 * y + c).astype(jnp.int32).reshape(1)

    def sbody(me_ref, p_ref, r_ref, o_ref):
        me_dev = me_ref[0]
        own = p_ref[...]
        acc = None
        for s in range(NDEV):
            rel = s ^ me_dev
            v = jnp.where(rel == 0, own, r_ref[jnp.maximum(rel - 1, 0)])
            acc = v if acc is None else acc + v
        o_ref[...] = acc

    return pl.pallas_call(
        sbody,
        grid_spec=pltpu.PrefetchScalarGridSpec(
            num_scalar_prefetch=1, grid=(rows // tr,),
            in_specs=[pl.BlockSpec((tr, 128), lambda i, me_ref: (i, 0)), pl.BlockSpec((7, tr, 128), lambda i, me_ref: (0, i, 0))],
            out_specs=pl.BlockSpec((tr, 128), lambda i, me_ref: (i, 0))),
        out_shape=_sds((rows, 128), F32), compiler_params=_cp("parallel"), name="sum_small")(me, packed, land)


def _rms_fwd(x, g, tm):
    n = x.shape[0]

    def body(x_ref, g_ref, h_ref):
        xv = x_ref[...]
        r = lax.rsqrt(jnp.mean(xv * xv, axis=-1, keepdims=True) + EPS)
        h_ref[...] = (xv * r * g_ref[...]).astype(BF16)

    return pl.pallas_call(
        body, grid=(n // tm,),
        in_specs=[pl.BlockSpec((tm, D), lambda i: (i, 0)), pl.BlockSpec((1, D), lambda i: (0, 0))],
        out_specs=pl.BlockSpec((tm, D), lambda i: (i, 0)), out_shape=_sds((n, D), BF16),
        compiler_params=_cp("parallel"), name="rms_fwd")(x, g)


def _rms_fwd_t(x, g, tm=512):
    n = x.shape[0]

    def body(x_ref, g_ref, h_ref, ht_ref):
        xv = x_ref[...]
        r = lax.rsqrt(jnp.mean(xv * xv, axis=-1, keepdims=True) + EPS)
        h = xv * r * g_ref[...]
        h_ref[...] = h.astype(BF16)
        ht_ref[...] = h.T.astype(BF16)

    return pl.pallas_call(
        body, grid=(n // tm,),
        in_specs=[pl.BlockSpec((tm, D), lambda i: (i, 0)), pl.BlockSpec((1, D), lambda i: (0, 0))],
        out_specs=[pl.BlockSpec((tm, D), lambda i: (i, 0)), pl.BlockSpec((D, tm), lambda i: (0, i))],
        out_shape=[_sds((n, D), BF16), _sds((D, n), BF16)], compiler_params=_cp("parallel"), name="rms_fwd_t")(x, g)


def _rms_bwd(dh, x, g, dres, tm=512):
    n = x.shape[0]

    def body(dh_ref, x_ref, g_ref, dr_ref, dx_ref, dg_ref):
        i = pl.program_id(0)
        xv = x_ref[...]
        r = lax.rsqrt(jnp.mean(xv * xv, axis=-1, keepdims=True) + EPS)
        xh = xv * r
        dhv = dh_ref[...]
        dxh = dhv * g_ref[...]
        dx_ref[...] = dr_ref[...] + r * (dxh - xh * jnp.mean(dxh * xh, axis=-1, keepdims=True))
        part = jnp.sum(dhv * xh, axis=0, keepdims=True)

        @pl.when(i == 0)
        def _():
            dg_ref[...] = part

        @pl.when(i > 0)
        def _():
            dg_ref[...] += part

    row = pl.BlockSpec((tm, D), lambda i: (i, 0))
    vec = pl.BlockSpec((1, D), lambda i: (0, 0))
    return pl.pallas_call(
        body, grid=(n // tm,), in_specs=[row, row, vec, row], out_specs=[row, vec],
        out_shape=[_sds((n, D), F32), _sds((1, D), F32)], compiler_params=_cp("arbitrary"), name="rms_bwd")(dh, x, g, dres)


def _loss_head(x, tgt, g, tm=512):
    def body(x_ref, t_ref, g_ref, dx_ref, dg_ref, ls_ref):
        i = pl.program_id(0)
        xv = x_ref[...]
        r = lax.rsqrt(jnp.mean(xv * xv, axis=-1, keepdims=True) + EPS)
        xh = xv * r
        gv = g_ref[...]
        diff = xh * gv - t_ref[...]
        dy = diff * (1.0 / D)
        dxh = dy * gv
        dx_ref[...] = r * (dxh - xh * jnp.mean(dxh * xh, axis=-1, keepdims=True))
        part_g = jnp.sum(dy * xh, axis=0, keepdims=True)
        part_l = jnp.sum(diff * diff, axis=0, keepdims=True)

        @pl.when(i == 0)
        def _():
            dg_ref[...] = part_g
            ls_ref[...] = part_l

        @pl.when(i > 0)
        def _():
            dg_ref[...] += part_g
            ls_ref[...] += part_l

        @pl.when(i == pl.num_programs(0) - 1)
        def _():
            tot = jnp.sum(ls_ref[...], axis=-1, keepdims=True) * (0.5 / D)
            ls_ref[...] = jnp.broadcast_to(tot, (1, D))

    row = pl.BlockSpec((tm, D), lambda i: (i, 0))
    vec = pl.BlockSpec((1, D), lambda i: (0, 0))
    return pl.pallas_call(
        body, grid=(S // tm,), in_specs=[row, row, vec], out_specs=[row, vec, vec],
        out_shape=[_sds((S, D), F32), _sds((1, D), F32), _sds((1, D), F32)],
        compiler_params=_cp("arbitrary"), name="loss_head")(x, tgt, g)


def _adamw(w, g, m, v):
    shape = w.shape
    cols = shape[-1] if w.ndim > 1 else shape[0]
    rows = w.size // cols
    w2, g2, m2, v2 = (t.reshape(rows, cols) for t in (w, g, m, v))
    tr = rows
    while tr * cols * 4 > (1 << 20) and tr % 16 == 0:
        tr //= 2
    c1 = 1.0 - ADAM_B1 ** ADAM_STEP
    c2 = 1.0 - ADAM_B2 ** ADAM_STEP

    def body(w_ref, g_ref, m_ref, v_ref, d_ref, nm_ref, nv_ref):
        gv = g_ref[...]
        mn = ADAM_B1 * m_ref[...] + (1.0 - ADAM_B1) * gv
        vn = ADAM_B2 * v_ref[...] + (1.0 - ADAM_B2) * (gv * gv)
        d_ref[...] = -ADAM_LR * ((mn / c1) / (jnp.sqrt(vn / c2) + ADAM_EPS) + ADAM_WD * w_ref[...])
        nm_ref[...] = mn
        nv_ref[...] = vn

    blk = pl.BlockSpec((tr, cols), lambda i: (i, 0))
    outs = pl.pallas_call(
        body, grid=(rows // tr,), in_specs=[blk] * 4, out_specs=[blk] * 3,
        out_shape=[_sds((rows, cols), F32)] * 3, compiler_params=_cp("parallel"), name="adamw")(w2, g2, m2, v2)
    return tuple(o.reshape(shape) for o in outs)


def _adamw_layer(l, w, g, m, v, outs, tr):
    cols = w.shape[-1]
    rows = w.size // (NL * cols)
    nb = rows // tr
    w2, m2, v2 = (t.reshape(NL * rows, cols) for t in (w, m, v))
    g2 = g.reshape(rows, cols)
    c1 = 1.0 - ADAM_B1 ** ADAM_STEP
    c2 = 1.0 - ADAM_B2 ** ADAM_STEP

    def body(w_ref, g_ref, m_ref, v_ref, d_in, nm_in, nv_in, go_in, d_ref, nm_ref, nv_ref, go_ref):
        gv = g_ref[...]
        mn = ADAM_B1 * m_ref[...] + (1.0 - ADAM_B1) * gv
        vn = ADAM_B2 * v_ref[...] + (1.0 - ADAM_B2) * (gv * gv)
        d_ref[...] = -ADAM_LR * ((mn / c1) / (jnp.sqrt(vn / c2) + ADAM_EPS) + ADAM_WD * w_ref[...])
        nm_ref[...] = mn
        nv_ref[...] = vn
        go_ref[...] = gv

    lay = pl.BlockSpec((tr, cols), lambda i: (l * nb + i, 0))
    return pl.pallas_call(
        body, grid=(nb,), in_specs=[lay, pl.BlockSpec((tr, cols), lambda i: (i, 0)), lay, lay] + [ANY] * 4,
        out_specs=[lay] * 4, out_shape=[_sds((NL * rows, cols), F32)] * 4,
        input_output_aliases={4: 0, 5: 1, 6: 2, 7: 3}, compiler_params=_cp("parallel"), name="adamw_layer")(w2, g2, m2, v2, *outs)


def _proj_fwd(h, wg, tm=512):
    def body(h_ref, w_ref, o_ref):
        o_ref[...] = _dot(h_ref[...], w_ref[...])

    return pl.pallas_call(
        body, grid=(NCHIP, S // tm),
        in_specs=[pl.BlockSpec((tm, D), lambda c, i: (i, 0)), pl.BlockSpec((None, D, CW), lambda c, i: (c, 0, 0))],
        out_specs=pl.BlockSpec((tm, CW), lambda c, i: (i, c)), out_shape=_sds((S, DIN), F32),
        compiler_params=_cp("parallel", "parallel"), name="proj_fwd")(h, wg)


def _proj_bwd_x(dproj, wg, dep, tm=1024):
    def body(d_ref, w_ref, dep_ref, o_ref):
        k = pl.program_id(1)
        part = _dot_nt(d_ref[...], w_ref[...])

        @pl.when(k == 0)
        def _():
            o_ref[...] = part

        @pl.when(k > 0)
        def _():
            o_ref[...] += part

    return pl.pallas_call(
        body, grid=(S // tm, NCHIP),
        in_specs=[pl.BlockSpec((tm, CW), lambda i, k: (i, k)), pl.BlockSpec((None, D, CW), lambda i, k: (k, 0, 0)), ANY],
        out_specs=pl.BlockSpec((tm, D), lambda i, k: (i, 0)), out_shape=_sds((S, D), F32),
        compiler_params=_cp("parallel", "arbitrary"), name="proj_bwd_x")(dproj, wg, dep)


def _proj_bwd_w(ht, dproj):
    def body(h_ref, d_ref, o_ref):
        o_ref[...] = _dot(h_ref[...], d_ref[...]).astype(BF16)

    return pl.pallas_call(
        body, grid=(NCHIP, NJ),
        in_specs=[pl.BlockSpec((D, S), lambda c, j: (0, 0)), pl.BlockSpec((S, TN_IN), lambda c, j: (0, c * NJ + j))],
        out_specs=pl.BlockSpec((None, D, TN_IN), lambda c, j: (c, 0, j)), out_shape=_sds((NCHIP, D, CW), BF16),
        compiler_params=_cp("parallel", "parallel"), name="proj_bwd_w")(ht, dproj)


def _merge_fwd(y_all, wbr, proj, tm=256):
    cb = D // NCHIP

    def body(y_ref, w_ref, *rest):
        g_refs, z_ref = rest[:8], rest[8]
        for c in range(NCHIP):
            acc = None
            for b in range(4):
                g = g_refs[2 * b + c // 2][:, (c % 2) * cb:(c % 2 + 1) * cb]
                t = _dot(y_ref[b], w_ref[c, b]) * _sigmoid(g)
                acc = t if acc is None else acc + t
            z_ref[:, c * cb:(c + 1) * cb] = acc.astype(BF16)

    g_specs = [pl.BlockSpec((tm, W), functools.partial(lambda j, i: (i, P_GM + j), j)) for j in range(8)]
    return pl.pallas_call(
        body, grid=(S // tm,),
        in_specs=[pl.BlockSpec((4, tm, W), lambda i: (0, i, 0)), pl.BlockSpec((NCHIP, 4, W, cb), lambda i: (0, 0, 0, 0))] + g_specs,
        out_specs=pl.BlockSpec((tm, D), lambda i: (i, 0)), out_shape=_sds((S, D), BF16),
        compiler_params=_cp("parallel"), name="merge_fwd")(y_all, wbr, *([proj] * 8))


def _merge_bwd(dz, y_all, wbr, proj, dproj, tm=512):
    cb = D // NCHIP
    ni = S // tm

    def body(dz_ref, y_ref, w_ref, ga_ref, gb_ref, dp_in, dp_ref, dy_ref, dw_ref, acc, obuf, osem):
        b = pl.program_id(0)
        i = pl.program_id(1)

        @pl.when(i == 0)
        def _():
            acc[...] = jnp.zeros_like(acc)

        yv = y_ref[...]
        dys = []

        def fill(slot):
            ws_ = [w_ref[c] for c in range(NCHIP)]
            t = [_dot(yv, wv) for wv in ws_]
            dts = []
            for c in range(NCHIP):
                g_ref = ga_ref if c < 2 else gb_ref
                g = _sigmoid(g_ref[:, (c % 2) * cb:(c % 2 + 1) * cb])
                dzc = dz_ref[:, c * cb:(c + 1) * cb]
                slot[:, c * cb:(c + 1) * cb] = (dzc * t[c] * g * (1.0 - g)).astype(BF16)
                dts.append((dzc * g).astype(BF16))
            dy = None
            for c in range(NCHIP):
                part = _dot_nt(dts[c], ws_[c])
                dy = part if dy is None else dy + part
            for c in range(NCHIP):
                acc[c] += _dot_tn(yv, dts[c])
            dys.append(dy)

        _tile_put(obuf, osem, lambda st: dp_ref.at[pl.ds((st % ni) * tm, tm), pl.ds(P_GM * W + (st // ni) * D, D)],
                  b * ni + i, 4 * ni, fill)
        dy_ref[...] = dys[0]

        @pl.when(i == ni - 1)
        def _():
            dw_ref[...] = acc[...].astype(BF16)

    return pl.pallas_call(
        body, grid=(4, ni),
        in_specs=[pl.BlockSpec((tm, D), lambda b, i: (i, 0)), pl.BlockSpec((None, tm, W), lambda b, i: (b, i, 0)),
                  pl.BlockSpec((NCHIP, None, W, cb), lambda b, i: (0, b, 0, 0)),
                  pl.BlockSpec((tm, W), lambda b, i: (i, P_GM + 2 * b)), pl.BlockSpec((tm, W), lambda b, i: (i, P_GM + 2 * b + 1)), ANY],
        out_specs=[ANY, pl.BlockSpec((None, tm, W), lambda b, i: (b, i, 0)), pl.BlockSpec((NCHIP, None, W, cb), lambda b, i: (0, b, 0, 0))],
        out_shape=[_sds((S, DIN), BF16), _sds((4, S, W), F32), _sds((NCHIP, 4, W, cb), BF16)],
        scratch_shapes=[pltpu.VMEM((NCHIP, W, cb), F32), pltpu.VMEM((2, tm, D), BF16), pltpu.SemaphoreType.DMA((2,))],
        input_output_aliases={5: 0}, compiler_params=_cp("arbitrary", "arbitrary"), name="merge_bwd")(dz, y_all, wbr, proj, proj, dproj)


def _out_fwd(z, wo, x, tm=512):
    def body(z_ref, w_ref, x_ref, o_ref):
        o_ref[...] = x_ref[...] + _dot(z_ref[...], w_ref[...])

    row = pl.BlockSpec((tm, D), lambda i: (i, 0))
    return pl.pallas_call(
        body, grid=(S // tm,), in_specs=[row, pl.BlockSpec((D, D), lambda i: (0, 0)), row], out_specs=row,
        out_shape=_sds((S, D), F32), compiler_params=_cp("parallel"), name="out_fwd")(z, wo, x)


def _out_bwd(dx, z, wo, tm=512):
    ni = S // tm

    def body(dx_ref, z_ref, w_ref, dz_ref, dw_ref, acc):
        i = pl.program_id(0)
        dxb = dx_ref[...].astype(BF16)
        dz_ref[...] = _dot_nt(dxb, w_ref[...])
        part = _dot_tn(z_ref[...], dxb)

        @pl.when(i == 0)
        def _():
            acc[...] = part

        @pl.when(i > 0)
        def _():
            acc[...] += part

        @pl.when(i == ni - 1)
        def _():
            dw_ref[...] = acc[...].astype(BF16)

    row = pl.BlockSpec((tm, D), lambda i: (i, 0))
    full = pl.BlockSpec((D, D), lambda i: (0, 0))
    return pl.pallas_call(
        body, grid=(ni,), in_specs=[row, row, full], out_specs=[row, full],
        out_shape=[_sds((S, D), F32), _sds((D, D), BF16)], scratch_shapes=[pltpu.VMEM((D, D), F32)],
        compiler_params=_cp("arbitrary"), name="out_bwd")(dx, z, wo)


def _gelu_parts(a):
    cdf = 0.5 * (1.0 + lax.erf(a * INV_SQRT2))
    return a * cdf, cdf


def _ln_parts(v):
    mu = jnp.mean(v, axis=-1, keepdims=True)
    vc = v - mu
    rs = lax.rsqrt(jnp.mean(vc * vc, axis=-1, keepdims=True) + EPS)
    return vc * rs, rs


def _causal_mask():
    return lax.broadcasted_iota(jnp.int32, (HD, HD), 0) >= lax.broadcasted_iota(jnp.int32, (HD, HD), 1)


def _gmlp_fwd(proj, lg, lb, ws, bias, y_all, tm=512):
    def body(uv_ref, gt_ref, lg_ref, lb_ref, ws_ref, b_ref, y_in, y_ref):
        act, _ = _gelu_parts(uv_ref[...])
        u = act[:, :W]
        xh, _ = _ln_parts(act[:, W:])
        vn = (xh * lg_ref[...] + lb_ref[...]).astype(BF16)
        gt = gt_ref[...]
        us = u * (gt * _sigmoid(gt))
        mask = _causal_mask()
        for h in range(4):
            wm = jnp.where(mask, ws_ref[h], 0.0).astype(BF16)
            cs = slice(h * HD, (h + 1) * HD)
            for c in range(tm // HD):
                rs_ = slice(c * HD, (c + 1) * HD)
                mixed = _dot(wm, vn[rs_, cs]) + b_ref[h]
                y_ref[rs_, cs] = (us[rs_, cs] * mixed).astype(BF16)

    vec = pl.BlockSpec((1, W), lambda i: (0, 0))
    mats = pl.BlockSpec((4, HD, HD), lambda i: (0, 0, 0))
    return pl.pallas_call(
        body, grid=(S // tm,),
        in_specs=[pl.BlockSpec((tm, 2 * W), lambda i: (i, 0)), pl.BlockSpec((tm, W), lambda i: (i, P_AGATE)), vec, vec, mats, mats, ANY],
        out_specs=pl.BlockSpec((None, tm, W), lambda i: (0, i, 0)), out_shape=_sds((4, S, W), BF16),
        input_output_aliases={6: 0}, compiler_params=_cp("parallel"), name="gmlp_fwd")(proj, proj, lg, lb, ws, bias, y_all)


def _gmlp_bwd(proj, dy_all, lg, lb, ws, bias, dproj, tm=256):
    ni = S // tm

    def body(uv_ref, gt_ref, dy_ref, lg_ref, lb_ref, ws_ref, b_ref, dp_in, dp_ref, dws_ref, dbs_ref, dlg_ref, dlb_ref, mix_s, dvn_s):
        i = pl.program_id(0)

        @pl.when(i == 0)
        def _():
            dws_ref[...] = jnp.zeros_like(dws_ref)
            dbs_ref[...] = jnp.zeros_like(dbs_ref)
            dlg_ref[...] = jnp.zeros_like(dlg_ref)
            dlb_ref[...] = jnp.zeros_like(dlb_ref)

        a0 = uv_ref[...]
        act, cdf = _gelu_parts(a0)
        u = act[:, :W]
        xh, rs = _ln_parts(act[:, W:])
        lgv = lg_ref[...]
        vn = (xh * lgv + lb_ref[...]).astype(BF16)
        mask = _causal_mask()
        wms = [jnp.where(mask, ws_ref[h], 0.0).astype(BF16) for h in range(4)]
        blocks = [(slice(c * HD, (c + 1) * HD), slice(h * HD, (h + 1) * HD), h) for h in range(4) for c in range(tm // HD)]
        for rs_, cs, h in blocks:
            mix_s[rs_, cs] = _dot(wms[h], vn[rs_, cs]) + b_ref[h]
        mixed = mix_s[...]
        gt = gt_ref[...]
        sg = _sigmoid(gt)
        sl = gt * sg
        dyv = dy_ref[...]
        dum = dyv * sl
        dgate = dyv * (u * mixed) * (sg * (1.0 + gt * (1.0 - sg)))
        du = dum * mixed
        dmix = dum * u
        dmb = dmix.astype(BF16)
        for rs_, cs, h in blocks:
            dvn_s[rs_, cs] = _dot_tn(wms[h], dmb[rs_, cs])
        for rs_, cs, h in blocks:
            dws_ref[h] += _dot_nt(dmb[rs_, cs], vn[rs_, cs])
            dbs_ref[h] += dmix[rs_, cs]
        dvn = dvn_s[...]
        dlg_ref[...] += jnp.sum(dvn * xh, axis=0, keepdims=True)
        dlb_ref[...] += jnp.sum(dvn, axis=0, keepdims=True)
        dxh = dvn * lgv
        dv = rs * (dxh - jnp.mean(dxh, axis=-1, keepdims=True) - xh * jnp.mean(dxh * xh, axis=-1, keepdims=True))
        gp = cdf + a0 * (jnp.exp(-0.5 * a0 * a0) * INV_SQRT2PI)
        dp_ref[:, :W] = (du * gp[:, :W]).astype(BF16)
        dp_ref[:, W:2 * W] = (dv * gp[:, W:]).astype(BF16)
        dp_ref[:, 2 * W:] = dgate.astype(BF16)

        @pl.when(i == ni - 1)
        def _():
            for h in range(4):
                dws_ref[h] = jnp.where(mask, dws_ref[h], 0.0)
                dbs_ref[h] = jnp.broadcast_to(jnp.sum(dbs_ref[h], axis=1, keepdims=True), (HD, HD))

    vec = pl.BlockSpec((1, W), lambda i: (0, 0))
    mats = pl.BlockSpec((4, HD, HD), lambda i: (0, 0, 0))
    return pl.pallas_call(
        body, grid=(ni,),
        in_specs=[pl.BlockSpec((tm, 2 * W), lambda i: (i, 0)), pl.BlockSpec((tm, W), lambda i: (i, P_AGATE)),
                  pl.BlockSpec((None, tm, W), lambda i: (0, i, 0)), vec, vec, mats, mats, ANY],
        out_specs=[pl.BlockSpec((tm, 3 * W), lambda i: (i, 0)), mats, mats, vec, vec],
        out_shape=[_sds((S, DIN), BF16), _sds((4, HD, HD), F32), _sds((4, HD, HD), F32), _sds((1, W), F32), _sds((1, W), F32)],
        scratch_shapes=[pltpu.VMEM((tm, W), F32), pltpu.VMEM((tm, W), F32)],
        input_output_aliases={7: 0}, compiler_params=_cp("arbitrary"), name="gmlp_bwd")(proj, proj, dy_all, lg, lb, ws, bias, dproj)


def _pool_diff(p, halo, row0, tm):
    xx = jnp.concatenate([halo, p], axis=0)
    t1 = (row0 + 1 + lax.broadcasted_iota(jnp.int32, (tm, 1), 0)).astype(F32)
    out = []
    for g, win in enumerate(POOL_WINDOWS):
        s = xx[:, g * HD:(g + 1) * HD]
        sh = 1
        while sh < win:
            s = s + pltpu.roll(s, sh, 0)
            sh *= 2
        out.append(s[HALO:] / jnp.minimum(t1, float(win)) - p[:, g * HD:(g + 1) * HD])
    return out


def _pool_fwd(proj, pw, sc, y_all, tm=512):
    rb = tm // HALO

    def body(p_ref, h_ref, gt_ref, pw_ref, sc_ref, y_in, y_ref):
        i = pl.program_id(0)
        halo = jnp.where(i > 0, h_ref[...], 0.0)
        ds = _pool_diff(p_ref[...], halo, i * tm, tm)
        gt = gt_ref[...]
        sl = gt * _sigmoid(gt)
        for g in range(4):
            cs = slice(g * HD, (g + 1) * HD)
            lin = _dot(ds[g].astype(BF16), pw_ref[g].astype(BF16))
            y_ref[:, cs] = (lin * sc_ref[:, cs] * sl[:, cs]).astype(BF16)

    return pl.pallas_call(
        body, grid=(S // tm,),
        in_specs=[pl.BlockSpec((tm, W), lambda i: (i, P_PIN)),
                  pl.BlockSpec((HALO, W), lambda i: (jnp.maximum(i * rb - 1, 0), P_PIN)),
                  pl.BlockSpec((tm, W), lambda i: (i, P_PGATE)),
                  pl.BlockSpec((4, HD, HD), lambda i: (0, 0, 0)), pl.BlockSpec((1, W), lambda i: (0, 0)), ANY],
        out_specs=pl.BlockSpec((None, tm, W), lambda i: (1, i, 0)), out_shape=_sds((4, S, W), BF16),
        input_output_aliases={5: 0}, compiler_params=_cp("parallel"), name="pool_fwd")(proj, proj, proj, pw, sc, y_all)


def _pool_bwd(proj, dy_all, pw, sc, dproj, tm=256):
    ni = S // tm
    rb = tm // HALO
    last_rb = S // HALO - 1
    rx = tm + HALO

    def body(p_ref, h_ref, gt_ref, gh_ref, dy_ref, dyh_ref, pw_ref, sc_ref, dp_in, dp_ref, dpw_ref, dsc_ref, obuf, osem):
        i = pl.program_id(0)

        @pl.when(i == 0)
        def _():
            dpw_ref[...] = jnp.zeros_like(dpw_ref)
            dsc_ref[...] = jnp.zeros_like(dsc_ref)

        halo = jnp.where(i > 0, h_ref[...], 0.0)
        ds = _pool_diff(p_ref[...], halo, i * tm, tm)
        nxt = i < ni - 1
        gx = jnp.concatenate([gt_ref[...], gh_ref[...]], axis=0)
        dyx = jnp.concatenate([dy_ref[...], jnp.where(nxt, dyh_ref[...], 0.0)], axis=0)
        sgx = _sigmoid(gx)
        slx = gx * sgx
        scv = sc_ref[...]
        dlinx = dyx * slx * scv
        t1 = (i * tm + 1 + lax.broadcasted_iota(jnp.int32, (rx, 1), 0)).astype(F32)
        gt, sg, sl, dyv = gx[:tm], sgx[:tm], slx[:tm], dyx[:tm]
        dsl = sg * (1.0 + gt * (1.0 - sg))

        def fill(slot):
            for g, win in enumerate(POOL_WINDOWS):
                cs = slice(g * HD, (g + 1) * HD)
                wv = pw_ref[g].astype(BF16)
                dlb = dlinx[:, cs].astype(BF16)
                ddx = _dot_nt(dlb, wv)
                f = ddx / jnp.minimum(t1, float(win))
                sh = 1
                while sh < win:
                    f = f + pltpu.roll(f, rx - sh, 0)
                    sh *= 2
                slot[:, cs] = (f[:tm] - ddx[:tm]).astype(BF16)
                db = ds[g].astype(BF16)
                lin = _dot(db, wv)
                slot[:, W + g * HD:W + (g + 1) * HD] = (dyv[:, cs] * lin * scv[:, cs] * dsl[:, cs]).astype(BF16)
                dsc_ref[:, cs] += jnp.sum(dyv[:, cs] * sl[:, cs] * lin, axis=0, keepdims=True)
                dpw_ref[g] += _dot_tn(db, dlb[:tm])

        _tile_put(obuf, osem, lambda st: dp_ref.at[pl.ds(st * tm, tm), pl.ds(P_PIN * W, 2 * W)], i, ni, fill)

    mats = pl.BlockSpec((4, HD, HD), lambda i: (0, 0, 0))
    vec = pl.BlockSpec((1, W), lambda i: (0, 0))
    return pl.pallas_call(
        body, grid=(ni,),
        in_specs=[pl.BlockSpec((tm, W), lambda i: (i, P_PIN)),
                  pl.BlockSpec((HALO, W), lambda i: (jnp.maximum(i * rb - 1, 0), P_PIN)),
                  pl.BlockSpec((tm, W), lambda i: (i, P_PGATE)),
                  pl.BlockSpec((HALO, W), lambda i: (jnp.minimum((i + 1) * rb, last_rb), P_PGATE)),
                  pl.BlockSpec((None, tm, W), lambda i: (1, i, 0)),
                  pl.BlockSpec((None, HALO, W), lambda i: (1, jnp.minimum((i + 1) * rb, last_rb), 0)),
                  mats, vec, ANY],
        out_specs=[ANY, mats, vec],
        out_shape=[_sds((S, DIN), BF16), _sds((4, HD, HD), F32), _sds((1, W), F32)],
        scratch_shapes=[pltpu.VMEM((2, tm, 2 * W), BF16), pltpu.SemaphoreType.DMA((2,))],
        input_output_aliases={8: 0}, compiler_params=_cp("arbitrary"), name="pool_bwd")(proj, proj, proj, proj, dy_all, dy_all, pw, sc, dproj)


ATT_STEP = ((1, 4), (4, 1), (4, 1))
ATT_GROUP = 16
ATT_GROUP_BWD = 8


def _att_band():
    qi = lax.broadcasted_iota(jnp.int32, (HD, 2 * HD), 0)
    kj = lax.broadcasted_iota(jnp.int32, (HD, 2 * HD), 1)
    return jnp.logical_and(kj >= qi, kj <= qi + HD), kj < HD


def _att_keys(kp_ref, ko_ref, vp_ref, vo_ref, a, jj):
    if jj == 0:
        return (jnp.concatenate([kp_ref[a], ko_ref[a, :HD, :]], axis=0), jnp.concatenate([vp_ref[a], vo_ref[a, :HD, :]], axis=0))
    return ko_ref[a, (jj - 1) * HD:(jj + 1) * HD, :], vo_ref[a, (jj - 1) * HD:(jj + 1) * HD, :]


def _dilate(src, dst, d, rows, cast=None):
    for r in range(d):
        for h in range(4):
            v = src.at[h][pl.ds(r, rows // d, stride=d), :] if d > 1 else src[h]
            dst[r * 4 + h] = v if cast is None else v.astype(cast)


def _undilate(src, dst, d, rows):
    for r in range(d):
        for h in range(4):
            if d > 1:
                dst.at[h][pl.ds(r, rows // d, stride=d), :] = src[r * 4 + h].astype(F32)
            else:
                dst[h] = src[h].astype(F32)


def _dil_spec(d, tm):
    return pl.BlockSpec((4 * d, tm // d, HD), lambda i: (0, i, 0))


def _att_prep(proj, tm=512):
    def body(q0, q1, q2, k_ref, v_ref, *rest):
        outs, scr = rest[:9], rest[9]
        for j, (src, dsts) in enumerate(((q0, ((0, outs[0]),)), (q1, ((1, outs[1]),)), (q2, ((2, outs[2]),)),
                                         (k_ref, tuple((g, outs[3 + g]) for g in range(3))),
                                         (v_ref, tuple((g, outs[6 + g]) for g in range(3))))):
            for h in range(4):
                scr[j, h] = src[:, h * HD:(h + 1) * HD]
            for g, dst in dsts:
                _dilate(scr.at[j], dst, DILATIONS[g], tm, BF16)

    def piece(p):
        return pl.BlockSpec((tm, W), lambda i: (i, p))

    shapes = [_sds((4 * d, S // d, HD), BF16) for d in DILATIONS]
    res = pl.pallas_call(
        body, grid=(S // tm,),
        in_specs=[piece(P_CQ), piece(P_CQ + 1), piece(P_CQ + 2), piece(P_CK), piece(P_CV)],
        out_specs=[_dil_spec(d, tm) for d in DILATIONS] * 3, out_shape=shapes * 3,
        scratch_shapes=[pltpu.VMEM((5, 4, tm, HD), F32)],
        compiler_params=_cp("parallel"), name="att_prep")(proj, proj, proj, proj, proj)
    return res[0:3], res[3:6], res[6:9]


def _att_specs(g):
    d = DILATIONS[g]
    nres, njb = ATT_STEP[g]
    nb = S // d // HD
    own = pl.BlockSpec((4 * nres, njb * HD, HD), lambda r, j: (r, j, 0))
    prev = pl.BlockSpec((4 * nres, HD, HD), lambda r, j: (r, jnp.maximum(j * njb - 1, 0), 0))
    nxt = pl.BlockSpec((4 * nres, HD, HD), lambda r, j: (r, jnp.minimum((j + 1) * njb, nb - 1), 0))
    return (d // nres, nb // njb), own, prev, nxt


def _att_fwd(q, k, v, g):
    d = DILATIONS[g]
    nres, njb = ATT_STEP[g]
    grid, own, prev, _ = _att_specs(g)

    def body(q_ref, kp_ref, ko_ref, vp_ref, vo_ref, o_ref, l_ref):
        jb = pl.program_id(1)
        band, is_prev = _att_band()
        no_prev = jnp.where(is_prev, jnp.where(jb > 0, 0.0, NEG), 0.0)
        blocks = [(a, jj) for jj in range(njb) for a in range(4 * nres)]
        for g0 in range(0, len(blocks), ATT_GROUP):
            grp = blocks[g0:g0 + ATT_GROUP]
            s, v2 = [], []
            for a, jj in grp:
                k2_, v2_ = _att_keys(kp_ref, ko_ref, vp_ref, vo_ref, a, jj)
                s_ = jnp.where(band, _dot_nt(q_ref[a, jj * HD:(jj + 1) * HD, :], k2_) * SCALE, NEG)
                s.append(s_ + no_prev if jj == 0 else s_)
                v2.append(v2_)
            m = [jnp.max(s_, axis=-1, keepdims=True) for s_ in s]
            e = [jnp.exp(s_ - m_) for s_, m_ in zip(s, m)]
            den = [jnp.sum(e_, axis=-1, keepdims=True) for e_ in e]
            inv = [1.0 / d_ for d_ in den]
            for i, (a, jj) in enumerate(grp):
                rs_ = slice(jj * HD, (jj + 1) * HD)
                o_ref[a, rs_, :] = _dot((e[i] * inv[i]).astype(BF16), v2[i])
                l_ref[a, rs_, :] = jnp.broadcast_to(m[i] + jnp.log(den[i]), (HD, HD))

    return pl.pallas_call(
        body, grid=grid, in_specs=[own, prev, own, prev, own], out_specs=[own, own],
        out_shape=[_sds((4 * d, S // d, HD), F32)] * 2,
        compiler_params=_cp("parallel", "parallel"), name="att_fwd")(q, k, k, v, v)


def _att_mix(os_, ls_, proj, y_all, tm=512):
    def body(o0, o1, o2, l0, l1, l2, gt_ref, y_in, y_ref, om_ref, lt_ref, so1, so2, sl1, sl2):
        _undilate(o1, so1, DILATIONS[1], tm)
        _undilate(o2, so2, DILATIONS[2], tm)
        _undilate(l1, sl1, DILATIONS[1], tm)
        _undilate(l2, sl2, DILATIONS[2], tm)
        for h in range(4):
            a, b, c = l0[h], sl1[h], sl2[h]
            m = jnp.maximum(jnp.maximum(a, b), c)
            ea, eb, ec = jnp.exp(a - m), jnp.exp(b - m), jnp.exp(c - m)
            z = ea + eb + ec
            inv = 1.0 / z
            o = (ea * inv) * o0[h] + (eb * inv) * so1[h] + (ec * inv) * so2[h]
            gt = gt_ref[:, h * HD:(h + 1) * HD]
            om_ref[h] = o
            lt_ref[h] = m + jnp.log(z)
            y_ref[:, h * HD:(h + 1) * HD] = (o * (gt * _sigmoid(gt))).astype(BF16)

    dil = [_dil_spec(d, tm) for d in DILATIONS]
    return pl.pallas_call(
        body, grid=(S // tm,),
        in_specs=dil * 2 + [pl.BlockSpec((tm, W), lambda i: (i, P_CGATE)), ANY],
        out_specs=[pl.BlockSpec((None, tm, W), lambda i: (2, i, 0)), dil[0], dil[0]],
        out_shape=[_sds((4, S, W), BF16), _sds((4, S, HD), F32), _sds((4, S, HD), F32)],
        scratch_shapes=[pltpu.VMEM((4, tm, HD), F32)] * 4,
        input_output_aliases={7: 0}, compiler_params=_cp("parallel"), name="att_mix")(*os_, *ls_, proj, y_all)


def _att_bwd_pre(dy_all, proj, om, lse, dproj, tm=512):
    def body(dy_ref, gt_ref, om_ref, ls_ref, dp_in, *rest):
        dos, dls, lss, dp_ref, sdo, sdl = rest[0:3], rest[3:6], rest[6:8], rest[8], rest[9], rest[10]
        for h in range(4):
            cs = slice(h * HD, (h + 1) * HD)
            gt = gt_ref[:, cs]
            sg = _sigmoid(gt)
            dyv = dy_ref[:, cs]
            o = om_ref[h]
            do = dyv * (gt * sg)
            dp_ref[:, cs] = (dyv * o * (sg * (1.0 + gt * (1.0 - sg)))).astype(BF16)
            sdo[h] = do
            sdl[h] = jnp.broadcast_to(jnp.sum(do * o, axis=-1, keepdims=True), (tm, HD))
        for g, d in enumerate(DILATIONS):
            _dilate(sdo, dos[g], d, tm, BF16)
            _dilate(sdl, dls[g], d, tm)
            if g > 0:
                _dilate(ls_ref, lss[g - 1], d, tm)

    dil = [_dil_spec(d, tm) for d in DILATIONS]
    gcol = pl.BlockSpec((tm, W), lambda i: (i, P_CGATE))
    res = pl.pallas_call(
        body, grid=(S // tm,),
        in_specs=[pl.BlockSpec((None, tm, W), lambda i: (2, i, 0)), gcol, dil[0], dil[0], ANY],
        out_specs=dil + dil + dil[1:] + [gcol],
        out_shape=([_sds((4 * d, S // d, HD), BF16) for d in DILATIONS] + [_sds((4 * d, S // d, HD), F32) for d in DILATIONS]
                   + [_sds((4 * d, S // d, HD), F32) for d in DILATIONS[1:]] + [_sds((S, DIN), BF16)]),
        scratch_shapes=[pltpu.VMEM((4, tm, HD), F32)] * 2,
        input_output_aliases={4: 8}, compiler_params=_cp("parallel"), name="att_bwd_pre")(dy_all, proj, om, lse, dproj)
    return res[0:3], res[3:6], [lse] + list(res[6:8]), res[8]


def _att_bwd(q, k, v, do, lse, delta, g):
    d = DILATIONS[g]
    nres, njb = ATT_STEP[g]
    grid, own, prev, nxt = _att_specs(g)

    def body(qa_ref, qn_ref, kp_ref, ko_ref, vp_ref, vo_ref, doa_ref, don_ref, la_ref, ln_ref, da_ref, dn_ref,
             dq_ref, dk_ref, dv_ref):
        jb = pl.program_id(1)
        band, is_prev = _att_band()
        m_next = lax.broadcasted_iota(jnp.int32, (HD, HD), 1) >= lax.broadcasted_iota(jnp.int32, (HD, HD), 0)
        has_prev = jnp.where(is_prev, jnp.where(jb > 0, 1.0, 0.0), 1.0)
        has_next = jnp.where(jb < grid[1] - 1, 1.0, 0.0)

        def wide(t):
            return jnp.concatenate([t, t], axis=1)

        blocks = [(a, jj) for jj in range(njb) for a in range(4 * nres)]
        for g0 in range(0, len(blocks), ATT_GROUP_BWD):
            grp = blocks[g0:g0 + ATT_GROUP_BWD]
            ops = []
            for a, jj in grp:
                rs_ = slice(jj * HD, (jj + 1) * HD)
                k2, v2 = _att_keys(kp_ref, ko_ref, vp_ref, vo_ref, a, jj)
                if jj == njb - 1:
                    qn, don, lsn, dln, fn = qn_ref[a], don_ref[a], ln_ref[a], dn_ref[a], has_next
                else:
                    ns = slice((jj + 1) * HD, (jj + 2) * HD)
                    qn, don, lsn, dln, fn = qa_ref[a, ns, :], doa_ref[a, ns, :], la_ref[a, ns, :], da_ref[a, ns, :], None
                ops.append(dict(qa=qa_ref[a, rs_, :], doa=doa_ref[a, rs_, :], lsa=wide(la_ref[a, rs_, :]),
                                dla=wide(da_ref[a, rs_, :]), k2=k2, v2=v2, ko=ko_ref[a, rs_, :], vo=vo_ref[a, rs_, :],
                                qn=qn, don=don, lsn=lsn, dln=dln, fn=fn, first=jj == 0))
            sa = [_dot_nt(o["qa"], o["k2"]) for o in ops]
            dpa = [_dot_nt(o["doa"], o["v2"]) for o in ops]
            sn = [_dot_nt(o["qn"], o["ko"]) for o in ops]
            dpn = [_dot_nt(o["don"], o["vo"]) for o in ops]
            pa, pn = [], []
            for o, sa_, sn_ in zip(ops, sa, sn):
                p_ = jnp.where(band, jnp.exp(sa_ * SCALE - o["lsa"]), 0.0)
                pa.append(p_ * has_prev if o["first"] else p_)
                p_ = jnp.where(m_next, jnp.exp(sn_ * SCALE - o["lsn"]), 0.0)
                pn.append(p_ if o["fn"] is None else p_ * o["fn"])
            dsa = [(p_ * (dp_ - o["dla"]) * SCALE).astype(BF16) for p_, dp_, o in zip(pa, dpa, ops)]
            dsn = [(p_ * (dp_ - o["dln"]) * SCALE).astype(BF16) for p_, dp_, o in zip(pn, dpn, ops)]
            for i, (a, jj) in enumerate(grp):
                rs_ = slice(jj * HD, (jj + 1) * HD)
                o = ops[i]
                dq_ref[a, rs_, :] = _dot(dsa[i], o["k2"])
                q2 = jnp.concatenate([o["qa"], o["qn"]], axis=0)
                do2 = jnp.concatenate([o["doa"], o["don"]], axis=0)
                dk_ref[a, rs_, :] = _dot_tn(jnp.concatenate([dsa[i][:, HD:], dsn[i]], axis=0), q2)
                dv_ref[a, rs_, :] = _dot_tn(jnp.concatenate([pa[i][:, HD:].astype(BF16), pn[i].astype(BF16)], axis=0), do2)

    return pl.pallas_call(
        body, grid=grid, in_specs=[own, nxt, prev, own, prev, own, own, nxt, own, nxt, own, nxt],
        out_specs=[own, own, own], out_shape=[_sds((4 * d, S // d, HD), F32)] * 3,
        compiler_params=_cp("parallel", "parallel"), name="att_bwd")(q, q, k, k, v, v, do, do, lse, lse, delta, delta)


def _att_bwd_post(dqs, dks, dvs, dproj, tm=512):
    def body(*refs):
        dq, dk, dv, dp_ref, scr = refs[0:3], refs[3:6], refs[6:9], refs[10], refs[11]
        for g in range(3):
            _undilate(dq[g], scr, DILATIONS[g], tm)
            for h in range(4):
                dp_ref[:, g * W + h * HD:g * W + (h + 1) * HD] = scr[h].astype(BF16)
        for j, parts in enumerate((dk, dv)):
            acc = None
            for g in range(3):
                _undilate(parts[g], scr, DILATIONS[g], tm)
                vals = [scr[h] for h in range(4)]
                acc = vals if acc is None else [x + y for x, y in zip(acc, vals)]
            for h in range(4):
                dp_ref[:, (3 + j) * W + h * HD:(3 + j) * W + (h + 1) * HD] = acc[h].astype(BF16)

    dil = [_dil_spec(d, tm) for d in DILATIONS]
    return pl.pallas_call(
        body, grid=(S // tm,), in_specs=dil * 3 + [ANY],
        out_specs=pl.BlockSpec((tm, 5 * W), lambda i: (i, 1)), out_shape=_sds((S, DIN), BF16),
        scratch_shapes=[pltpu.VMEM((4, tm, HD), F32)],
        input_output_aliases={9: 0}, compiler_params=_cp("parallel"), name="att_bwd_post")(*dqs, *dks, *dvs, dproj)


def _mem_kv_fwd(mem_n, wkv):
    m = mem_n.shape[0]

    def body(a_ref, w_ref, o_ref):
        o_ref[...] = _dot(a_ref[...], w_ref[...])

    return pl.pallas_call(body, out_shape=_sds((m, 2 * W), F32), compiler_params=_cp(), name="mem_kv_fwd")(mem_n, wkv)


def _mem_fwd(proj, kv, y_all, tm=512):
    m = kv.shape[0]

    def body(q_ref, gt_ref, kv_ref, y_in, y_ref):
        gt = gt_ref[...]
        sl = gt * _sigmoid(gt)
        hs = [slice(h * HD, (h + 1) * HD) for h in range(4)]
        s = [_dot_nt(q_ref[:, cs].astype(BF16), kv_ref[:, cs].astype(BF16)) * SCALE for cs in hs]
        e = [jnp.exp(s_ - jnp.max(s_, axis=-1, keepdims=True)) for s_ in s]
        p = [(e_ * (1.0 / jnp.sum(e_, axis=-1, keepdims=True))).astype(BF16) for e_ in e]
        for h, cs in enumerate(hs):
            o = _dot(p[h], kv_ref[:, W + h * HD:W + (h + 1) * HD].astype(BF16))
            y_ref[:, cs] = (o * sl[:, cs]).astype(BF16)

    return pl.pallas_call(
        body, grid=(S // tm,),
        in_specs=[pl.BlockSpec((tm, W), lambda i: (i, P_MQ)), pl.BlockSpec((tm, W), lambda i: (i, P_MGATE)),
                  pl.BlockSpec((m, 2 * W), lambda i: (0, 0)), ANY],
        out_specs=pl.BlockSpec((None, tm, W), lambda i: (3, i, 0)), out_shape=_sds((4, S, W), BF16),
        input_output_aliases={3: 0}, compiler_params=_cp("parallel"), name="mem_fwd")(proj, proj, kv, y_all)


def _mem_bwd(proj, kv, dy_all, dproj, tm=512):
    m = kv.shape[0]
    ni = S // tm

    def body(q_ref, gt_ref, kv_ref, dy_ref, dp_in, dp_ref, dkv_ref, obuf, osem):
        i = pl.program_id(0)

        @pl.when(i == 0)
        def _():
            dkv_ref[...] = jnp.zeros_like(dkv_ref)

        gt = gt_ref[...]
        sg = _sigmoid(gt)
        sl = gt * sg
        dsl = sg * (1.0 + gt * (1.0 - sg))
        dyv = dy_ref[...]

        def fill(slot):
            hs = [slice(h * HD, (h + 1) * HD) for h in range(4)]
            vss = [slice(W + h * HD, W + (h + 1) * HD) for h in range(4)]
            q = [q_ref[:, cs].astype(BF16) for cs in hs]
            k = [kv_ref[:, cs].astype(BF16) for cs in hs]
            v = [kv_ref[:, vs].astype(BF16) for vs in vss]
            dob = [(dyv[:, cs] * sl[:, cs]).astype(BF16) for cs in hs]
            s = [_dot_nt(q_, k_) * SCALE for q_, k_ in zip(q, k)]
            dp = [_dot_nt(d_, v_) for d_, v_ in zip(dob, v)]
            e = [jnp.exp(s_ - jnp.max(s_, axis=-1, keepdims=True)) for s_ in s]
            p = [e_ * (1.0 / jnp.sum(e_, axis=-1, keepdims=True)) for e_ in e]
            pb = [p_.astype(BF16) for p_ in p]
            dsb = [(p_ * (dp_ - jnp.sum(dp_ * p_, axis=-1, keepdims=True)) * SCALE).astype(BF16) for p_, dp_ in zip(p, dp)]
            for h, (cs, vs) in enumerate(zip(hs, vss)):
                o = _dot(pb[h], v[h])
                slot[:, cs] = _dot(dsb[h], k[h]).astype(BF16)
                slot[:, vs] = (dyv[:, cs] * o * dsl[:, cs]).astype(BF16)
                dkv_ref[:, cs] += _dot_tn(dsb[h], q[h])
                dkv_ref[:, vs] += _dot_tn(pb[h], dob[h])

        _tile_put(obuf, osem, lambda st: dp_ref.at[pl.ds(st * tm, tm), pl.ds(P_MQ * W, 2 * W)], i, ni, fill)

    return pl.pallas_call(
        body, grid=(ni,),
        in_specs=[pl.BlockSpec((tm, W), lambda i: (i, P_MQ)), pl.BlockSpec((tm, W), lambda i: (i, P_MGATE)),
                  pl.BlockSpec((m, 2 * W), lambda i: (0, 0)), pl.BlockSpec((None, tm, W), lambda i: (3, i, 0)), ANY],
        out_specs=[ANY, pl.BlockSpec((m, 2 * W), lambda i: (0, 0))],
        out_shape=[_sds((S, DIN), BF16), _sds((m, 2 * W), F32)],
        scratch_shapes=[pltpu.VMEM((2, tm, 2 * W), BF16), pltpu.SemaphoreType.DMA((2,))],
        input_output_aliases={4: 0}, compiler_params=_cp("arbitrary"), name="mem_bwd")(proj, proj, kv, dy_all, dproj)


def _mem_kv_bwd(mem, g, mem_n, wkv, dkv):
    m = mem.shape[0]

    def body(x_ref, g_ref, a_ref, w_ref, d_ref, dw_ref, dg_ref):
        db = d_ref[...].astype(BF16)
        dw_ref[...] = _dot_tn(a_ref[...], db).astype(BF16)
        dn = _dot_nt(db, w_ref[...])
        xv = x_ref[...]
        xh = xv * lax.rsqrt(jnp.mean(xv * xv, axis=-1, keepdims=True) + EPS)
        dg_ref[...] = jnp.sum(dn * xh, axis=0, keepdims=True)

    return pl.pallas_call(
        body, out_shape=[_sds((D, 2 * W), BF16), _sds((1, D), F32)], compiler_params=_cp(), name="mem_kv_bwd")(mem, g, mem_n, wkv, dkv)


def _layer_fwd(x, h, ht, mem_n, p, wg):
    win, wkv, wbr, wo = wg
    proj = _proj_fwd(h, win)
    y_all = lax.empty((4, S, W), BF16)
    y_all = _gmlp_fwd(proj, p["gm_ln_g"], p["gm_ln_b"], p["gm_ws"], p["gm_bias"], y_all)
    y_all = _pool_fwd(proj, p["pool_w"], p["pool_scale"], y_all)
    qs, ks, vs = _att_prep(proj)
    os_, ls_ = zip(*[_att_fwd(qs[g], ks[g], vs[g], g) for g in range(3)])
    y_all, om, lse = _att_mix(os_, ls_, proj, y_all)
    kv = _mem_kv_fwd(mem_n, wkv)
    y_all = _mem_fwd(proj, kv, y_all)
    z = _merge_fwd(y_all, wbr, proj)
    x_new = _out_fwd(z, wo, x)
    return x_new, dict(x=x, ht=ht, proj=proj, y_all=y_all, om=om, lse=lse, mem_n=mem_n, kv=kv, z=z, qkv=(qs, ks, vs))


GRAD_PARTS = ((2, D // 2, CW), (2, D // 8, 2 * W), (2, 2 * W, D // NCHIP), (2, D // 8, D))
SUM_TILE = (64, 128, 256, 128)
ADAM_TILE = (128, 256, 2048, 256)
PLACE_TILE = (256, 256, 512, 256)


def _layer_bwd(dx, mem, p, wg, sv, exchange):
    win, wkv, wbr, wo = wg
    proj = sv["proj"]
    dz, d_wo = _out_bwd(dx, sv["z"], wo)
    dproj = lax.empty((S, DIN), BF16)
    dproj, dy_all, d_wbr = _merge_bwd(dz, sv["y_all"], wbr, proj, dproj)
    dproj, d_ws, d_bs, d_lg, d_lb = _gmlp_bwd(proj, dy_all, p["gm_ln_g"], p["gm_ln_b"], p["gm_ws"], p["gm_bias"], dproj)
    dproj, d_pw, d_sc = _pool_bwd(proj, dy_all, p["pool_w"], p["pool_scale"], dproj)
    dos, dls, lss, dproj = _att_bwd_pre(dy_all, proj, sv["om"], sv["lse"], dproj)
    qs, ks, vs = sv["qkv"]
    dqs, dks, dvs = zip(*[_att_bwd(qs[g], ks[g], vs[g], dos[g], lss[g], dls[g], g) for g in range(3)])
    dproj = _att_bwd_post(dqs, dks, dvs, dproj)
    dproj, dkv = _mem_bwd(proj, sv["kv"], dy_all, dproj)
    d_wkv, d_mg = _mem_kv_bwd(mem, p["mem_norm_g"], sv["mem_n"], wkv, dkv)
    d_win = _proj_bwd_w(sv["ht"], dproj)
    big = tuple(t.reshape((NCHIP,) + s) for t, s in zip((d_win, d_wkv, d_wbr, d_wo), GRAD_PARTS))
    inflight = exchange(big)
    dh = _proj_bwd_x(dproj, win, inflight[-1])
    dx_in, d_ng = _rms_bwd(dh, sv["x"], p["norm_g"], dx)
    small = dict(norm_g=d_ng, gm_ln_g=d_lg, gm_ln_b=d_lb, gm_ws=d_ws, gm_bs=d_bs[:, :, 0], pool_w=d_pw, pool_scale=d_sc, mem_norm_g=d_mg)
    return dx_in, (big,) + inflight, small


_SMALL = ("norm_g", "gm_ln_g", "gm_ln_b", "gm_ws", "gm_bs", "pool_w", "pool_scale", "mem_norm_g")


def _layer_params(l, norm_g, gm_ln_g, gm_ln_b, gm_ws, gm_bs, pool_w, pool_scale, mem_norm_g):
    return dict(norm_g=norm_g[l][None], gm_ln_g=gm_ln_g[l][None], gm_ln_b=gm_ln_b[l][None], gm_ws=gm_ws[l],
                gm_bias=jnp.broadcast_to(gm_bs[l][:, :, None], (4, HD, HD)), pool_w=pool_w[l],
                pool_scale=pool_scale[l][None], mem_norm_g=mem_norm_g[l][None])


def kernel(x, mem, norm_g, w_in, gm_ln_g, gm_ln_b, gm_ws, gm_bs, pool_w, pool_scale, mem_norm_g, w_mem_kv, w_branch, w_out, final_norm_g, loss_target, m_norm_g, m_w_in, m_gm_ln_g, m_gm_ln_b, m_gm_ws, m_gm_bs, m_pool_w, m_pool_scale, m_mem_norm_g, m_w_mem_kv, m_w_branch, m_w_out, m_final_norm_g, v_norm_g, v_w_in, v_gm_ln_g, v_gm_ln_b, v_gm_ws, v_gm_bs, v_pool_w, v_pool_scale, v_mem_norm_g, v_w_mem_kv, v_w_branch, v_w_out, v_final_norm_g):
    xs, memv, tgt = x[0], mem[0], loss_target[0]
    params = [_layer_params(l, norm_g, gm_ln_g, gm_ln_b, gm_ws, gm_bs, pool_w, pool_scale, mem_norm_g) for l in range(NL)]

    shards = (w_in.astype(BF16), w_mem_kv.astype(BF16), w_branch.astype(BF16), w_out.astype(BF16))
    lands, gsems, after = _gather_start(shards)

    saved, wgs = [], []
    mem_ns = [_rms_fwd(memv, params[l]["mem_norm_g"], memv.shape[0]) for l in range(NL)]
    for l in range(NL):
        h, ht = _rms_fwd_t(xs, params[l]["norm_g"])
        if l == 0:
            got, relay_sems = _gather_relay(shards, lands[0], gsems[0], [after, h] + mem_ns)
            got = _gather_wait_relay(got, relay_sems)
        else:
            got = _gather_wait(l, shards, lands[l], gsems[l], after)
        got = [_place_own(got[k], shards[k], l, PLACE_TILE[k]) for k in range(4)]
        wgs.append((got[0], got[1].reshape(D, 2 * W), got[2], got[3].reshape(D, D)))
        xs, sv = _layer_fwd(xs, h, ht, mem_ns[l], params[l], wgs[l])
        saved.append(sv)
        after = xs
    dx, d_fg, ls = _loss_head(xs, tgt, final_norm_g[None])
    loss = lax.psum(ls[0, 0], ("x", "y", "c"))

    flight, small = [None] * NL, [None] * NL
    for l in reversed(range(NL)):
        dx, flight[l], small[l] = _layer_bwd(dx, memv, params[l], wgs[l], saved[l], functools.partial(_exch_start, l))
    grad_x = dx[None]

    leaves = [jnp.stack([small[l][n] for l in range(NL)]) for n in _SMALL] + [d_fg]
    sizes = [t.size for t in leaves]
    packed = jnp.concatenate([t.reshape(-1, 128) for t in leaves], axis=0)
    rows = packed.shape[0]
    small_zone, small_sems, after = _small_start(packed)

    ws = dict(norm_g=norm_g, w_in=w_in, gm_ln_g=gm_ln_g, gm_ln_b=gm_ln_b, gm_ws=gm_ws, gm_bs=gm_bs, pool_w=pool_w,
              pool_scale=pool_scale, mem_norm_g=mem_norm_g, w_mem_kv=w_mem_kv, w_branch=w_branch, w_out=w_out,
              final_norm_g=final_norm_g)
    ms = dict(norm_g=m_norm_g, w_in=m_w_in, gm_ln_g=m_gm_ln_g, gm_ln_b=m_gm_ln_b, gm_ws=m_gm_ws, gm_bs=m_gm_bs,
              pool_w=m_pool_w, pool_scale=m_pool_scale, mem_norm_g=m_mem_norm_g, w_mem_kv=m_w_mem_kv,
              w_branch=m_w_branch, w_out=m_w_out, final_norm_g=m_final_norm_g)
    vs = dict(norm_g=v_norm_g, w_in=v_w_in, gm_ln_g=v_gm_ln_g, gm_ln_b=v_gm_ln_b, gm_ws=v_gm_ws, gm_bs=v_gm_bs,
              pool_w=v_pool_w, pool_scale=v_pool_scale, mem_norm_g=v_mem_norm_g, w_mem_kv=v_w_mem_kv,
              w_branch=v_w_branch, w_out=v_w_out, final_norm_g=v_final_norm_g)

    big = ("w_in", "w_mem_kv", "w_branch", "w_out")
    acc = {n: [lax.empty((ws[n].size // ws[n].shape[-1], ws[n].shape[-1]), F32) for _ in range(4)] for n in big}
    for l in reversed(range(NL)):
        parts, zones, sems, _ = flight[l]
        zones = _exch_wait(l, parts, zones, sems, after)
        full = _share_full([_sum_half(parts[k], zones[k], SUM_TILE[k]) for k in range(4)])
        for k, n in enumerate(big):
            acc[n] = _adamw_layer(l, ws[n], full[k], ms[n], vs[n], acc[n], ADAM_TILE[k])
        after = acc["w_in"][0]
    grads, upd = {}, {}
    for n in big:
        d_, m_, v_, g_ = (t.reshape(ws[n].shape) for t in acc[n])
        grads[n], upd[n] = g_, (d_, m_, v_)

    small_zone = _small_wait(packed, small_zone, small_sems, after)
    tot = _sum_small(packed, small_zone, max(t for t in range(8, 513, 8) if rows % t == 0))
    offs = [0]
    for sz in sizes:
        offs.append(offs[-1] + sz // 128)
    for i, n in enumerate(_SMALL + ("final_norm_g",)):
        grads[n] = tot[offs[i]:offs[i + 1]].reshape(ws[n].shape)
        upd[n] = _adamw(ws[n], grads[n], ms[n], vs[n])
    order = ("norm_g", "w_in", "gm_ln_g", "gm_ln_b", "gm_ws", "gm_bs", "pool_w", "pool_scale", "mem_norm_g", "w_mem_kv",
             "w_branch", "w_out", "final_norm_g")
    return (loss, grad_x, *[grads[n] for n in order], *[upd[n][0] for n in order], *[upd[n][1] for n in order],
            *[upd[n][2] for n in order])
```

```python
import functools
import math

import jax
import jax.numpy as jnp
from jax import lax
from jax.experimental import pallas as pl
from jax.experimental.pallas import tpu as pltpu

F32 = jnp.float32
BF16 = jnp.bfloat16

S = 4096
D = 1024
W = 512
DIN = 10752
NL = 4
NCHIP = 4
NDEV = 8
CW = DIN // NCHIP
TN_IN = 896
NJ = CW // TN_IN
HD = 128
EPS = 1e-6
NEG = -1e30
SCALE = HD ** -0.5
INV_SQRT2 = 1.0 / math.sqrt(2.0)
INV_SQRT2PI = 1.0 / math.sqrt(2.0 * math.pi)
POOL_WINDOWS = (2, 4, 8, 16)
DILATIONS = (1, 4, 16)
HALO = 16
NPIECE = DIN // W
P_AGATE, P_PIN, P_PGATE, P_CQ, P_CK, P_CV, P_CGATE, P_MQ, P_MGATE, P_GM = 2, 3, 4, 5, 8, 9, 10, 11, 12, 13
VMEM_LIMIT = 56 * 1024 * 1024

ADAM_LR, ADAM_B1, ADAM_B2, ADAM_EPS, ADAM_WD, ADAM_STEP = 0.001, 0.9, 0.999, 1e-08, 0.01, 10

MESH = pl.DeviceIdType.MESH
ANY = pl.BlockSpec(memory_space=pl.ANY)


def _cp(*sem):
    return pltpu.CompilerParams(dimension_semantics=sem or None, vmem_limit_bytes=VMEM_LIMIT)


def _sds(shape, dtype):
    return jax.ShapeDtypeStruct(shape, dtype)


def _sigmoid(v):
    return 1.0 / (1.0 + jnp.exp(-v))


def _dot(a, b):
    return jnp.dot(a, b, preferred_element_type=F32)


def _dot_nt(a, b):
    return lax.dot_general(a, b, (((1,), (1,)), ((), ())), preferred_element_type=F32)


def _dot_tn(a, b):
    return lax.dot_general(a, b, (((0,), (0,)), ((), ())), preferred_element_type=F32)


def _tile_put(buf, sem, dst_of, step, nsteps, fill):
    slot = step % 2

    def copy(s, st):
        return pltpu.make_async_copy(buf.at[s], dst_of(st), sem.at[s])

    @pl.when(step >= 2)
    def _():
        copy(slot, step).wait()

    fill(buf.at[slot])
    copy(slot, step).start()

    @pl.when(step == nsteps - 1)
    def _():
        if nsteps >= 2:
            copy(1 - slot, step).wait()
        copy(slot, step).wait()


def _my_pos():
    return lax.axis_index("x"), lax.axis_index("y"), lax.axis_index("c")


_CHIP_REL = ((1, 0), (0, 1), (1, 1))
_DEV_REL = tuple((dx, dy, dc) for dx in (0, 1) for dy in (0, 1) for dc in (0, 1))[1:]


HBM = pl.BlockSpec(memory_space=pltpu.HBM)
SEM = pl.BlockSpec(memory_space=pltpu.SEMAPHORE)
EFFECT = pltpu.SideEffectType.DATAFLOW_SIDE_EFFECTING
N_GATHER = 3 * 4
N_EXCH = 7 * 4


def _in_hbm(t):
    return pltpu.with_memory_space_constraint(t, pltpu.HBM)


def _gather_start(shards):
    nk = len(shards)
    lands = [pltpu.HBM((NCHIP,) + s.shape[1:], BF16) for s in shards for _ in range(NL)]

    def body(*refs):
        ins, outs = refs[:nk], refs[nk:nk + nk * NL]
        sems = refs[nk + nk * NL:nk + nk * NL + 2 * NL]
        token = refs[-1]
        x, y, c = _my_pos()
        me = 2 * x + y
        for l in range(NL):
            for r, (dx, dy) in enumerate(_CHIP_REL):
                for k in range(nk):
                    src, dst = ins[k].at[l], outs[k * NL + l].at[me]
                    if l == 0:
                        hf = pl.ds(c * (shards[k].shape[1] // 2), shards[k].shape[1] // 2)
                        src, dst = src.at[hf], dst.at[hf]
                    pltpu.make_async_remote_copy(
                        src_ref=src, dst_ref=dst, send_sem=sems[2 * l].at[r * nk + k],
                        recv_sem=sems[2 * l + 1].at[r * nk + k], device_id=(x ^ dx, y ^ dy, c), device_id_type=MESH).start()
        token[...] = jnp.zeros_like(token)

    res = pl.pallas_call(
        body, name="gather_start",
        out_shape=lands + [pltpu.SemaphoreType.DMA((N_GATHER,))] * (2 * NL) + [_sds((8, 128), F32)],
        in_specs=[HBM] * nk, out_specs=[HBM] * (nk * NL) + [SEM] * (2 * NL) + [pl.BlockSpec(memory_space=pltpu.VMEM)],
        compiler_params=pltpu.CompilerParams(has_side_effects=EFFECT))(*[_in_hbm(s) for s in shards])
    lands = [[res[k * NL + l] for k in range(nk)] for l in range(NL)]
    sems = [(res[nk * NL + 2 * l], res[nk * NL + 2 * l + 1]) for l in range(NL)]
    return lands, sems, res[-1]


def _gather_relay(shards, lands, sems, after):
    nk = len(shards)
    half = [s.shape[1] // 2 for s in shards]

    na = len(after)

    def body(*refs):
        ins, land = refs[:nk], refs[nk:2 * nk]
        send, recv = refs[2 * nk], refs[2 * nk + 1]
        send2, recv2 = refs[3 * nk + 2 + na], refs[3 * nk + 3 + na]
        x, y, c = _my_pos()
        for r, (dx, dy) in enumerate(_CHIP_REL):
            cx, cy = x ^ dx, y ^ dy
            for k in range(nk):
                hf = pl.ds(c * half[k], half[k])
                got = land[k].at[2 * cx + cy].at[hf]
                cp = pltpu.make_async_remote_copy(
                    src_ref=ins[k].at[0].at[hf], dst_ref=got, send_sem=send.at[r * nk + k],
                    recv_sem=recv.at[r * nk + k], device_id=(cx, cy, c), device_id_type=MESH)
                cp.wait_send()
                cp.wait_recv()
                pltpu.make_async_remote_copy(
                    src_ref=got, dst_ref=got, send_sem=send2.at[r * nk + k], recv_sem=recv2.at[r * nk + k],
                    device_id=(x, y, 1 - c), device_id_type=MESH).start()

    res = pl.pallas_call(
        body, name="gather_relay",
        out_shape=[pltpu.HBM(t.shape, t.dtype) for t in lands] + [pltpu.SemaphoreType.DMA((N_GATHER,))] * 2,
        in_specs=[ANY] * nk + [HBM] * nk + [SEM, SEM] + [ANY] * na, out_specs=[HBM] * nk + [SEM, SEM],
        input_output_aliases={nk + k: k for k in range(nk)},
        compiler_params=pltpu.CompilerParams(has_side_effects=EFFECT))(*shards, *lands, *sems, *after)
    return res[:nk], (res[nk], res[nk + 1])


def _gather_wait_relay(lands, sems):
    nk = len(lands)
    half = [t.shape[1] // 2 for t in lands]

    def body(*refs):
        land = refs[:nk]
        send, recv = refs[nk], refs[nk + 1]
        x, y, c = _my_pos()
        for r, (dx, dy) in enumerate(_CHIP_REL):
            chip = 2 * (x ^ dx) + (y ^ dy)
            for k in range(nk):
                mine = land[k].at[chip].at[pl.ds(c * half[k], half[k])]
                theirs = land[k].at[chip].at[pl.ds((1 - c) * half[k], half[k])]
                cp = pltpu.make_async_remote_copy(
                    src_ref=mine, dst_ref=theirs, send_sem=send.at[r * nk + k], recv_sem=recv.at[r * nk + k],
                    device_id=(x, y, 1 - c), device_id_type=MESH)
                cp.wait_send()
                cp.wait_recv()

    return pl.pallas_call(
        body, name="gather_wait_relay", out_shape=[pltpu.HBM(t.shape, t.dtype) for t in lands],
        in_specs=[HBM] * nk + [SEM, SEM], out_specs=[HBM] * nk, input_output_aliases={k: k for k in range(nk)},
        compiler_params=pltpu.CompilerParams(has_side_effects=EFFECT))(*lands, *sems)


def _gather_wait(l, shards, lands, sems, after):
    nk = len(shards)

    def body(*refs):
        ins, land = refs[:nk], refs[nk:2 * nk]
        send, recv = refs[2 * nk], refs[2 * nk + 1]
        x, y, c = _my_pos()
        for r, (dx, dy) in enumerate(_CHIP_REL):
            cx, cy = x ^ dx, y ^ dy
            for k in range(nk):
                cp = pltpu.make_async_remote_copy(
                    src_ref=ins[k].at[l], dst_ref=land[k].at[2 * cx + cy], send_sem=send.at[r * nk + k],
                    recv_sem=recv.at[r * nk + k], device_id=(cx, cy, c), device_id_type=MESH)
                cp.wait_send()
                cp.wait_recv()

    return pl.pallas_call(
        body, name=f"gather_wait_{l}", out_shape=[pltpu.HBM(t.shape, t.dtype) for t in lands],
        in_specs=[ANY] * nk + [HBM] * nk + [SEM, SEM, ANY], out_specs=[HBM] * nk,
        input_output_aliases={nk + k: k for k in range(nk)},
        compiler_params=pltpu.CompilerParams(has_side_effects=EFFECT))(*shards, *lands, *sems, after)


def _place_own(land, shard, l, tr):
    _, rows, cols = shard.shape[0], shard.shape[-2], shard.shape[-1]
    lead = shard.shape[1:-2]
    nlead = math.prod(lead)
    sh = shard.reshape((NL, nlead, rows, cols))
    ld = land.reshape((NCHIP, nlead, rows, cols))
    me = (2 * lax.axis_index("x") + lax.axis_index("y")).astype(jnp.int32).reshape(1)

    def body(me_ref, s_ref, l_in, o_ref):
        o_ref[...] = s_ref[...]

    out = pl.pallas_call(
        body,
        grid_spec=pltpu.PrefetchScalarGridSpec(
            num_scalar_prefetch=1, grid=(nlead, rows // tr),
            in_specs=[pl.BlockSpec((None, None, tr, cols), lambda b, i, me_ref: (l, b, i, 0)), ANY],
            out_specs=pl.BlockSpec((None, None, tr, cols), lambda b, i, me_ref: (me_ref[0], b, i, 0))),
        out_shape=_sds(ld.shape, BF16), input_output_aliases={2: 0},
        compiler_params=_cp("parallel", "parallel"), name="place_own")(me, sh, ld)
    return out.reshape(land.shape)


def _exch_start(l, parts):
    nk = len(parts)

    def body(*refs):
        ins, outs = refs[:nk], refs[nk:2 * nk]
        send, recv, token = refs[2 * nk:]
        x, y, c = _my_pos()
        for r, (dx, dy, dc) in enumerate(_DEV_REL):
            px, py, pc = x ^ dx, y ^ dy, c ^ dc
            for k in range(nk):
                pltpu.make_async_remote_copy(
                    src_ref=ins[k].at[2 * px + py, pc], dst_ref=outs[k].at[r], send_sem=send.at[r * nk + k],
                    recv_sem=recv.at[r * nk + k], device_id=(px, py, pc), device_id_type=MESH).start()
        token[...] = jnp.zeros_like(token)

    res = pl.pallas_call(
        body, name=f"exch_start_{l}",
        out_shape=[pltpu.HBM((7,) + p.shape[2:], BF16) for p in parts] + [pltpu.SemaphoreType.DMA((N_EXCH,))] * 2 + [_sds((8, 128), F32)],
        in_specs=[HBM] * nk, out_specs=[HBM] * nk + [SEM, SEM, pl.BlockSpec(memory_space=pltpu.VMEM)],
        compiler_params=pltpu.CompilerParams(has_side_effects=EFFECT))(*[_in_hbm(p) for p in parts])
    return res[:nk], (res[nk], res[nk + 1]), res[-1]


def _exch_wait(l, parts, lands, sems, after):
    nk = len(parts)

    def body(*refs):
        ins, land = refs[:nk], refs[nk:2 * nk]
        send, recv = refs[2 * nk], refs[2 * nk + 1]
        x, y, c = _my_pos()
        for r, (dx, dy, dc) in enumerate(_DEV_REL):
            px, py, pc = x ^ dx, y ^ dy, c ^ dc
            for k in range(nk):
                cp = pltpu.make_async_remote_copy(
                    src_ref=ins[k].at[2 * px + py, pc], dst_ref=land[k].at[r], send_sem=send.at[r * nk + k],
                    recv_sem=recv.at[r * nk + k], device_id=(px, py, pc), device_id_type=MESH)
                cp.wait_send()
                cp.wait_recv()

    return pl.pallas_call(
        body, name=f"exch_wait_{l}", out_shape=[pltpu.HBM(t.shape, t.dtype) for t in lands],
        in_specs=[ANY] * nk + [HBM] * nk + [SEM, SEM, ANY], out_specs=[HBM] * nk,
        input_output_aliases={nk + k: k for k in range(nk)},
        compiler_params=pltpu.CompilerParams(has_side_effects=EFFECT))(*parts, *lands, *sems, after)


def _chip_half():
    x, y, c = _my_pos()
    return jnp.stack([2 * x + y, c]).astype(jnp.int32)


def _sum_half(part, land, tr):
    _, _, r2, cols = part.shape

    def body(pos_ref, p_ref, r_ref, o_ref):
        acc = p_ref[...].astype(F32)
        for r in range(7):
            acc = acc + r_ref[r].astype(F32)
        o_ref[...] = acc

    return pl.pallas_call(
        body,
        grid_spec=pltpu.PrefetchScalarGridSpec(
            num_scalar_prefetch=1, grid=(r2 // tr,),
            in_specs=[pl.BlockSpec((None, None, tr, cols), lambda i, pos: (pos[0], pos[1], i, 0)),
                      pl.BlockSpec((7, tr, cols), lambda i, pos: (0, i, 0))],
            out_specs=pl.BlockSpec((None, tr, cols), lambda i, pos: (pos[1], i, 0))),
        out_shape=_sds((2, r2, cols), F32), compiler_params=_cp("parallel"), name="sum_half")(_chip_half(), part, land)


def _share_full(fulls):
    nk = len(fulls)

    def body(*refs):
        ins, outs = refs[:nk], refs[nk:2 * nk]
        send, recv = refs[2 * nk:]
        x, y, c = _my_pos()

        def copy(k, hf):
            return pltpu.make_async_remote_copy(
                src_ref=ins[k].at[hf], dst_ref=outs[k].at[hf], send_sem=send.at[k], recv_sem=recv.at[k],
                device_id=(x, y, 1 - c), device_id_type=MESH)

        for k in range(nk):
            copy(k, c).start()
        for k in range(nk):
            copy(k, 1 - c).wait_recv()
        for k in range(nk):
            copy(k, c).wait_send()

    return pl.pallas_call(
        body, out_shape=[_sds(f.shape, F32) for f in fulls], in_specs=[ANY] * nk, out_specs=[ANY] * nk,
        scratch_shapes=[pltpu.SemaphoreType.DMA((nk,))] * 2, input_output_aliases={k: k for k in range(nk)},
        name="share_full")(*fulls)


def _small_start(packed):
    def body(in_ref, out_ref, send, recv, token):
        x, y, c = _my_pos()
        for r, (dx, dy, dc) in enumerate(_DEV_REL):
            pltpu.make_async_remote_copy(
                src_ref=in_ref, dst_ref=out_ref.at[r], send_sem=send.at[r], recv_sem=recv.at[r],
                device_id=(x ^ dx, y ^ dy, c ^ dc), device_id_type=MESH).start()
        token[...] = jnp.zeros_like(token)

    res = pl.pallas_call(
        body, name="small_start",
        out_shape=[pltpu.HBM((7,) + packed.shape, F32)] + [pltpu.SemaphoreType.DMA((7,))] * 2 + [_sds((8, 128), F32)],
        in_specs=[HBM], out_specs=[HBM, SEM, SEM, pl.BlockSpec(memory_space=pltpu.VMEM)],
        compiler_params=pltpu.CompilerParams(has_side_effects=EFFECT))(_in_hbm(packed))
    return res[0], (res[1], res[2]), res[3]


def _small_wait(packed, land, sems, after):
    def body(in_ref, land_ref, send, recv, after_ref, out_ref):
        x, y, c = _my_pos()
        for r, (dx, dy, dc) in enumerate(_DEV_REL):
            cp = pltpu.make_async_remote_copy(
                src_ref=in_ref, dst_ref=land_ref.at[r], send_sem=send.at[r], recv_sem=recv.at[r],
                device_id=(x ^ dx, y ^ dy, c ^ dc), device_id_type=MESH)
            cp.wait_send()
            cp.wait_recv()

    return pl.pallas_call(
        body, name="small_wait", out_shape=pltpu.HBM(land.shape, land.dtype),
        in_specs=[ANY, HBM, SEM, SEM, ANY], out_specs=HBM, input_output_aliases={1: 0},
        compiler_params=pltpu.CompilerParams(has_side_effects=EFFECT))(packed, land, *sems, after)


def _sum_small(packed, land, tr):
    rows = packed.shape[0]
    x, y, c = _my_pos()
    me = (4 * x + 2 * y + c).astype(jnp.int32).reshape(1)

    def sbody(me_ref, p_ref, r_ref, o_ref):
        me_dev = me_ref[0]
        own = p_ref[...]
        acc = None
        for s in range(NDEV):
            rel = s ^ me_dev
            v = jnp.where(rel == 0, own, r_ref[jnp.maximum(rel - 1, 0)])
            acc = v if acc is None else acc + v
        o_ref[...] = acc

    return pl.pallas_call(
        sbody,
        grid_spec=pltpu.PrefetchScalarGridSpec(
            num_scalar_prefetch=1, grid=(rows // tr,),
            in_specs=[pl.BlockSpec((tr, 128), lambda i, me_ref: (i, 0)), pl.BlockSpec((7, tr, 128), lambda i, me_ref: (0, i, 0))],
            out_specs=pl.BlockSpec((tr, 128), lambda i, me_ref: (i, 0))),
        out_shape=_sds((rows, 128), F32), compiler_params=_cp("parallel"), name="sum_small")(me, packed, land)


def _rms_fwd(x, g, tm):
    n = x.shape[0]

    def body(x_ref, g_ref, h_ref):
        xv = x_ref[...]
        r = lax.rsqrt(jnp.mean(xv * xv, axis=-1, keepdims=True) + EPS)
        h_ref[...] = (xv * r * g_ref[...]).astype(BF16)

    return pl.pallas_call(
        body, grid=(n // tm,),
        in_specs=[pl.BlockSpec((tm, D), lambda i: (i, 0)), pl.BlockSpec((1, D), lambda i: (0, 0))],
        out_specs=pl.BlockSpec((tm, D), lambda i: (i, 0)), out_shape=_sds((n, D), BF16),
        compiler_params=_cp("parallel"), name="rms_fwd")(x, g)


def _rms_fwd_t(x, g, tm=512):
    n = x.shape[0]

    def body(x_ref, g_ref, h_ref, ht_ref):
        xv = x_ref[...]
        r = lax.rsqrt(jnp.mean(xv * xv, axis=-1, keepdims=True) + EPS)
        h = xv * r * g_ref[...]
        h_ref[...] = h.astype(BF16)
        ht_ref[...] = h.T.astype(BF16)

    return pl.pallas_call(
        body, grid=(n // tm,),
        in_specs=[pl.BlockSpec((tm, D), lambda i: (i, 0)), pl.BlockSpec((1, D), lambda i: (0, 0))],
        out_specs=[pl.BlockSpec((tm, D), lambda i: (i, 0)), pl.BlockSpec((D, tm), lambda i: (0, i))],
        out_shape=[_sds((n, D), BF16), _sds((D, n), BF16)], compiler_params=_cp("parallel"), name="rms_fwd_t")(x, g)


def _rms_bwd(dh, x, g, dres, tm=512):
    n = x.shape[0]

    def body(dh_ref, x_ref, g_ref, dr_ref, dx_ref, dg_ref):
        i = pl.program_id(0)
        xv = x_ref[...]
        r = lax.rsqrt(jnp.mean(xv * xv, axis=-1, keepdims=True) + EPS)
        xh = xv * r
        dhv = dh_ref[...]
        dxh = dhv * g_ref[...]
        dx_ref[...] = dr_ref[...] + r * (dxh - xh * jnp.mean(dxh * xh, axis=-1, keepdims=True))
        part = jnp.sum(dhv * xh, axis=0, keepdims=True)

        @pl.when(i == 0)
        def _():
            dg_ref[...] = part

        @pl.when(i > 0)
        def _():
            dg_ref[...] += part

    row = pl.BlockSpec((tm, D), lambda i: (i, 0))
    vec = pl.BlockSpec((1, D), lambda i: (0, 0))
    return pl.pallas_call(
        body, grid=(n // tm,), in_specs=[row, row, vec, row], out_specs=[row, vec],
        out_shape=[_sds((n, D), F32), _sds((1, D), F32)], compiler_params=_cp("arbitrary"), name="rms_bwd")(dh, x, g, dres)


def _loss_head(x, tgt, g, tm=512):
    def body(x_ref, t_ref, g_ref, dx_ref, dg_ref, ls_ref):
        i = pl.program_id(0)
        xv = x_ref[...]
        r = lax.rsqrt(jnp.mean(xv * xv, axis=-1, keepdims=True) + EPS)
        xh = xv * r
        gv = g_ref[...]
        diff = xh * gv - t_ref[...]
        dy = diff * (1.0 / D)
        dxh = dy * gv
        dx_ref[...] = r * (dxh - xh * jnp.mean(dxh * xh, axis=-1, keepdims=True))
        part_g = jnp.sum(dy * xh, axis=0, keepdims=True)
        part_l = jnp.sum(diff * diff, axis=0, keepdims=True)

        @pl.when(i == 0)
        def _():
            dg_ref[...] = part_g
            ls_ref[...] = part_l

        @pl.when(i > 0)
        def _():
            dg_ref[...] += part_g
            ls_ref[...] += part_l

        @pl.when(i == pl.num_programs(0) - 1)
        def _():
            tot = jnp.sum(ls_ref[...], axis=-1, keepdims=True) * (0.5 / D)
            ls_ref[...] = jnp.broadcast_to(tot, (1, D))

    row = pl.BlockSpec((tm, D), lambda i: (i, 0))
    vec = pl.BlockSpec((1, D), lambda i: (0, 0))
    return pl.pallas_call(
        body, grid=(S // tm,), in_specs=[row, row, vec], out_specs=[row, vec, vec],
        out_shape=[_sds((S, D), F32), _sds((1, D), F32), _sds((1, D), F32)],
        compiler_params=_cp("arbitrary"), name="loss_head")(x, tgt, g)


def _adamw(w, g, m, v):
    shape = w.shape
    cols = shape[-1] if w.ndim > 1 else shape[0]
    rows = w.size // cols
    w2, g2, m2, v2 = (t.reshape(rows, cols) for t in (w, g, m, v))
    tr = rows
    while tr * cols * 4 > (1 << 20) and tr % 16 == 0:
        tr //= 2
    c1 = 1.0 - ADAM_B1 ** ADAM_STEP
    c2 = 1.0 - ADAM_B2 ** ADAM_STEP

    def body(w_ref, g_ref, m_ref, v_ref, d_ref, nm_ref, nv_ref):
        gv = g_ref[...]
        mn = ADAM_B1 * m_ref[...] + (1.0 - ADAM_B1) * gv
        vn = ADAM_B2 * v_ref[...] + (1.0 - ADAM_B2) * (gv * gv)
        d_ref[...] = -ADAM_LR * ((mn / c1) / (jnp.sqrt(vn / c2) + ADAM_EPS) + ADAM_WD * w_ref[...])
        nm_ref[...] = mn
        nv_ref[...] = vn

    blk = pl.BlockSpec((tr, cols), lambda i: (i, 0))
    outs = pl.pallas_call(
        body, grid=(rows // tr,), in_specs=[blk] * 4, out_specs=[blk] * 3,
        out_shape=[_sds((rows, cols), F32)] * 3, compiler_params=_cp("parallel"), name="adamw")(w2, g2, m2, v2)
    return tuple(o.reshape(shape) for o in outs)


def _adamw_layer(l, w, g, m, v, outs, tr):
    cols = w.shape[-1]
    rows = w.size // (NL * cols)
    nb = rows // tr
    w2, m2, v2 = (t.reshape(NL * rows, cols) for t in (w, m, v))
    g2 = g.reshape(rows, cols)
    c1 = 1.0 - ADAM_B1 ** ADAM_STEP
    c2 = 1.0 - ADAM_B2 ** ADAM_STEP

    def body(w_ref, g_ref, m_ref, v_ref, d_in, nm_in, nv_in, go_in, d_ref, nm_ref, nv_ref, go_ref):
        gv = g_ref[...]
        mn = ADAM_B1 * m_ref[...] + (1.0 - ADAM_B1) * gv
        vn = ADAM_B2 * v_ref[...] + (1.0 - ADAM_B2) * (gv * gv)
        d_ref[...] = -ADAM_LR * ((mn / c1) / (jnp.sqrt(vn / c2) + ADAM_EPS) + ADAM_WD * w_ref[...])
        nm_ref[...] = mn
        nv_ref[...] = vn
        go_ref[...] = gv

    lay = pl.BlockSpec((tr, cols), lambda i: (l * nb + i, 0))
    return pl.pallas_call(
        body, grid=(nb,), in_specs=[lay, pl.BlockSpec((tr, cols), lambda i: (i, 0)), lay, lay] + [ANY] * 4,
        out_specs=[lay] * 4, out_shape=[_sds((NL * rows, cols), F32)] * 4,
        input_output_aliases={4: 0, 5: 1, 6: 2, 7: 3}, compiler_params=_cp("parallel"), name="adamw_layer")(w2, g2, m2, v2, *outs)


def _proj_fwd(h, wg, tm=512):
    def body(h_ref, w_ref, o_ref):
        o_ref[...] = _dot(h_ref[...], w_ref[...]).astype(BF16)

    return pl.pallas_call(
        body, grid=(NCHIP, S // tm),
        in_specs=[pl.BlockSpec((tm, D), lambda c, i: (i, 0)), pl.BlockSpec((None, D, CW), lambda c, i: (c, 0, 0))],
        out_specs=pl.BlockSpec((tm, CW), lambda c, i: (i, c)), out_shape=_sds((S, DIN), BF16),
        compiler_params=_cp("parallel", "parallel"), name="proj_fwd")(h, wg)


def _proj_bwd_x(dproj, wg, dep, tm=1024):
    def body(d_ref, w_ref, dep_ref, o_ref):
        k = pl.program_id(1)
        part = _dot_nt(d_ref[...], w_ref[...])

        @pl.when(k == 0)
        def _():
            o_ref[...] = part

        @pl.when(k > 0)
        def _():
            o_ref[...] += part

    return pl.pallas_call(
        body, grid=(S // tm, NCHIP),
        in_specs=[pl.BlockSpec((tm, CW), lambda i, k: (i, k)), pl.BlockSpec((None, D, CW), lambda i, k: (k, 0, 0)), ANY],
        out_specs=pl.BlockSpec((tm, D), lambda i, k: (i, 0)), out_shape=_sds((S, D), F32),
        compiler_params=_cp("parallel", "arbitrary"), name="proj_bwd_x")(dproj, wg, dep)


def _proj_bwd_w(ht, dproj):
    def body(h_ref, d_ref, o_ref):
        o_ref[...] = _dot(h_ref[...], d_ref[...]).astype(BF16)

    return pl.pallas_call(
        body, grid=(NCHIP, NJ),
        in_specs=[pl.BlockSpec((D, S), lambda c, j: (0, 0)), pl.BlockSpec((S, TN_IN), lambda c, j: (0, c * NJ + j))],
        out_specs=pl.BlockSpec((None, D, TN_IN), lambda c, j: (c, 0, j)), out_shape=_sds((NCHIP, D, CW), BF16),
        compiler_params=_cp("parallel", "parallel"), name="proj_bwd_w")(ht, dproj)


def _merge_fwd(y_all, wbr, proj, tm=256):
    cb = D // NCHIP

    def body(y_ref, w_ref, *rest):
        g_refs, z_ref = rest[:8], rest[8]
        for c in range(NCHIP):
            acc = None
            for b in range(4):
                g = g_refs[2 * b + c // 2][:, (c % 2) * cb:(c % 2 + 1) * cb].astype(F32)
                t = _dot(y_ref[b], w_ref[c, b]) * _sigmoid(g)
                acc = t if acc is None else acc + t
            z_ref[:, c * cb:(c + 1) * cb] = acc.astype(BF16)

    g_specs = [pl.BlockSpec((tm, W), functools.partial(lambda j, i: (i, P_GM + j), j)) for j in range(8)]
    return pl.pallas_call(
        body, grid=(S // tm,),
        in_specs=[pl.BlockSpec((4, tm, W), lambda i: (0, i, 0)), pl.BlockSpec((NCHIP, 4, W, cb), lambda i: (0, 0, 0, 0))] + g_specs,
        out_specs=pl.BlockSpec((tm, D), lambda i: (i, 0)), out_shape=_sds((S, D), BF16),
        compiler_params=_cp("parallel"), name="merge_fwd")(y_all, wbr, *([proj] * 8))


def _merge_bwd(dz, y_all, wbr, proj, dproj, tm=512):
    cb = D // NCHIP
    ni = S // tm

    def body(dz_ref, y_ref, w_ref, ga_ref, gb_ref, dp_in, dp_ref, dy_ref, dw_ref, acc, obuf, osem):
        b = pl.program_id(0)
        i = pl.program_id(1)

        @pl.when(i == 0)
        def _():
            acc[...] = jnp.zeros_like(acc)

        yv = y_ref[...]
        dys = []

        def fill(slot):
            ws_ = [w_ref[c] for c in range(NCHIP)]
            t = [_dot(yv, wv) for wv in ws_]
            dts = []
            for c in range(NCHIP):
                g_ref = ga_ref if c < 2 else gb_ref
                g = _sigmoid(g_ref[:, (c % 2) * cb:(c % 2 + 1) * cb].astype(F32))
                dzc = dz_ref[:, c * cb:(c + 1) * cb]
                slot[:, c * cb:(c + 1) * cb] = (dzc * t[c] * g * (1.0 - g)).astype(BF16)
                dts.append((dzc * g).astype(BF16))
            dy = None
            for c in range(NCHIP):
                part = _dot_nt(dts[c], ws_[c])
                dy = part if dy is None else dy + part
            for c in range(NCHIP):
                acc[c] += _dot_tn(yv, dts[c])
            dys.append(dy)

        _tile_put(obuf, osem, lambda st: dp_ref.at[pl.ds((st % ni) * tm, tm), pl.ds(P_GM * W + (st // ni) * D, D)],
                  b * ni + i, 4 * ni, fill)
        dy_ref[...] = dys[0]

        @pl.when(i == ni - 1)
        def _():
            dw_ref[...] = acc[...].astype(BF16)

    return pl.pallas_call(
        body, grid=(4, ni),
        in_specs=[pl.BlockSpec((tm, D), lambda b, i: (i, 0)), pl.BlockSpec((None, tm, W), lambda b, i: (b, i, 0)),
                  pl.BlockSpec((NCHIP, None, W, cb), lambda b, i: (0, b, 0, 0)),
                  pl.BlockSpec((tm, W), lambda b, i: (i, P_GM + 2 * b)), pl.BlockSpec((tm, W), lambda b, i: (i, P_GM + 2 * b + 1)), ANY],
        out_specs=[ANY, pl.BlockSpec((None, tm, W), lambda b, i: (b, i, 0)), pl.BlockSpec((NCHIP, None, W, cb), lambda b, i: (0, b, 0, 0))],
        out_shape=[_sds((S, DIN), BF16), _sds((4, S, W), F32), _sds((NCHIP, 4, W, cb), BF16)],
        scratch_shapes=[pltpu.VMEM((NCHIP, W, cb), F32), pltpu.VMEM((2, tm, D), BF16), pltpu.SemaphoreType.DMA((2,))],
        input_output_aliases={5: 0}, compiler_params=_cp("arbitrary", "arbitrary"), name="merge_bwd")(dz, y_all, wbr, proj, proj, dproj)


def _out_fwd(z, wo, x, tm=512):
    def body(z_ref, w_ref, x_ref, o_ref):
        o_ref[...] = x_ref[...] + _dot(z_ref[...], w_ref[...])

    row = pl.BlockSpec((tm, D), lambda i: (i, 0))
    return pl.pallas_call(
        body, grid=(S // tm,), in_specs=[row, pl.BlockSpec((D, D), lambda i: (0, 0)), row], out_specs=row,
        out_shape=_sds((S, D), F32), compiler_params=_cp("parallel"), name="out_fwd")(z, wo, x)


def _out_bwd(dx, z, wo, tm=512):
    ni = S // tm

    def body(dx_ref, z_ref, w_ref, dz_ref, dw_ref, acc):
        i = pl.program_id(0)
        dxb = dx_ref[...].astype(BF16)
        dz_ref[...] = _dot_nt(dxb, w_ref[...])
        part = _dot_tn(z_ref[...], dxb)

        @pl.when(i == 0)
        def _():
            acc[...] = part

        @pl.when(i > 0)
        def _():
            acc[...] += part

        @pl.when(i == ni - 1)
        def _():
            dw_ref[...] = acc[...].astype(BF16)

    row = pl.BlockSpec((tm, D), lambda i: (i, 0))
    full = pl.BlockSpec((D, D), lambda i: (0, 0))
    return pl.pallas_call(
        body, grid=(ni,), in_specs=[row, row, full], out_specs=[row, full],
        out_shape=[_sds((S, D), F32), _sds((D, D), BF16)], scratch_shapes=[pltpu.VMEM((D, D), F32)],
        compiler_params=_cp("arbitrary"), name="out_bwd")(dx, z, wo)


def _gelu_parts(a):
    cdf = 0.5 * (1.0 + lax.erf(a * INV_SQRT2))
    return a * cdf, cdf


def _ln_parts(v):
    mu = jnp.mean(v, axis=-1, keepdims=True)
    vc = v - mu
    rs = lax.rsqrt(jnp.mean(vc * vc, axis=-1, keepdims=True) + EPS)
    return vc * rs, rs


def _causal_mask():
    return lax.broadcasted_iota(jnp.int32, (HD, HD), 0) >= lax.broadcasted_iota(jnp.int32, (HD, HD), 1)


def _gmlp_fwd(proj, lg, lb, ws, bias, y_all, tm=512):
    def body(uv_ref, gt_ref, lg_ref, lb_ref, ws_ref, b_ref, y_in, y_ref):
        act, _ = _gelu_parts(uv_ref[...].astype(F32))
        u = act[:, :W]
        xh, _ = _ln_parts(act[:, W:])
        vn = (xh * lg_ref[...] + lb_ref[...]).astype(BF16)
        gt = gt_ref[...].astype(F32)
        us = u * (gt * _sigmoid(gt))
        mask = _causal_mask()
        for h in range(4):
            wm = jnp.where(mask, ws_ref[h], 0.0).astype(BF16)
            cs = slice(h * HD, (h + 1) * HD)
            for c in range(tm // HD):
                rs_ = slice(c * HD, (c + 1) * HD)
                mixed = _dot(wm, vn[rs_, cs]) + b_ref[h]
                y_ref[rs_, cs] = (us[rs_, cs] * mixed).astype(BF16)

    vec = pl.BlockSpec((1, W), lambda i: (0, 0))
    mats = pl.BlockSpec((4, HD, HD), lambda i: (0, 0, 0))
    return pl.pallas_call(
        body, grid=(S // tm,),
        in_specs=[pl.BlockSpec((tm, 2 * W), lambda i: (i, 0)), pl.BlockSpec((tm, W), lambda i: (i, P_AGATE)), vec, vec, mats, mats, ANY],
        out_specs=pl.BlockSpec((None, tm, W), lambda i: (0, i, 0)), out_shape=_sds((4, S, W), BF16),
        input_output_aliases={6: 0}, compiler_params=_cp("parallel"), name="gmlp_fwd")(proj, proj, lg, lb, ws, bias, y_all)


def _gmlp_bwd(proj, dy_all, lg, lb, ws, bias, dproj, tm=256):
    ni = S // tm

    def body(uv_ref, gt_ref, dy_ref, lg_ref, lb_ref, ws_ref, b_ref, dp_in, dp_ref, dws_ref, dbs_ref, dlg_ref, dlb_ref, mix_s, dvn_s):
        i = pl.program_id(0)

        @pl.when(i == 0)
        def _():
            dws_ref[...] = jnp.zeros_like(dws_ref)
            dbs_ref[...] = jnp.zeros_like(dbs_ref)
            dlg_ref[...] = jnp.zeros_like(dlg_ref)
            dlb_ref[...] = jnp.zeros_like(dlb_ref)

        a0 = uv_ref[...].astype(F32)
        act, cdf = _gelu_parts(a0)
        u = act[:, :W]
        xh, rs = _ln_parts(act[:, W:])
        lgv = lg_ref[...]
        vn = (xh * lgv + lb_ref[...]).astype(BF16)
        mask = _causal_mask()
        wms = [jnp.where(mask, ws_ref[h], 0.0).astype(BF16) for h in range(4)]
        blocks = [(slice(c * HD, (c + 1) * HD), slice(h * HD, (h + 1) * HD), h) for h in range(4) for c in range(tm // HD)]
        for rs_, cs, h in blocks:
            mix_s[rs_, cs] = _dot(wms[h], vn[rs_, cs]) + b_ref[h]
        mixed = mix_s[...]
        gt = gt_ref[...].astype(F32)
        sg = _sigmoid(gt)
        sl = gt * sg
        dyv = dy_ref[...]
        dum = dyv * sl
        dgate = dyv * (u * mixed) * (sg * (1.0 + gt * (1.0 - sg)))
        du = dum * mixed
        dmix = dum * u
        dmb = dmix.astype(BF16)
        for rs_, cs, h in blocks:
            dvn_s[rs_, cs] = _dot_tn(wms[h], dmb[rs_, cs])
        for rs_, cs, h in blocks:
            dws_ref[h] += _dot_nt(dmb[rs_, cs], vn[rs_, cs])
            dbs_ref[h] += dmix[rs_, cs]
        dvn = dvn_s[...]
        dlg_ref[...] += jnp.sum(dvn * xh, axis=0, keepdims=True)
        dlb_ref[...] += jnp.sum(dvn, axis=0, keepdims=True)
        dxh = dvn * lgv
        dv = rs * (dxh - jnp.mean(dxh, axis=-1, keepdims=True) - xh * jnp.mean(dxh * xh, axis=-1, keepdims=True))
        gp = cdf + a0 * (jnp.exp(-0.5 * a0 * a0) * INV_SQRT2PI)
        dp_ref[:, :W] = (du * gp[:, :W]).astype(BF16)
        dp_ref[:, W:2 * W] = (dv * gp[:, W:]).astype(BF16)
        dp_ref[:, 2 * W:] = dgate.astype(BF16)

        @pl.when(i == ni - 1)
        def _():
            for h in range(4):
                dws_ref[h] = jnp.where(mask, dws_ref[h], 0.0)
                dbs_ref[h] = jnp.broadcast_to(jnp.sum(dbs_ref[h], axis=1, keepdims=True), (HD, HD))

    vec = pl.BlockSpec((1, W), lambda i: (0, 0))
    mats = pl.BlockSpec((4, HD, HD), lambda i: (0, 0, 0))
    return pl.pallas_call(
        body, grid=(ni,),
        in_specs=[pl.BlockSpec((tm, 2 * W), lambda i: (i, 0)), pl.BlockSpec((tm, W), lambda i: (i, P_AGATE)),
                  pl.BlockSpec((None, tm, W), lambda i: (0, i, 0)), vec, vec, mats, mats, ANY],
        out_specs=[pl.BlockSpec((tm, 3 * W), lambda i: (i, 0)), mats, mats, vec, vec],
        out_shape=[_sds((S, DIN), BF16), _sds((4, HD, HD), F32), _sds((4, HD, HD), F32), _sds((1, W), F32), _sds((1, W), F32)],
        scratch_shapes=[pltpu.VMEM((tm, W), F32), pltpu.VMEM((tm, W), F32)],
        input_output_aliases={7: 0}, compiler_params=_cp("arbitrary"), name="gmlp_bwd")(proj, proj, dy_all, lg, lb, ws, bias, dproj)


def _pool_diff(p, halo, row0, tm):
    xx = jnp.concatenate([halo, p], axis=0)
    t1 = (row0 + 1 + lax.broadcasted_iota(jnp.int32, (tm, 1), 0)).astype(F32)
    out = []
    for g, win in enumerate(POOL_WINDOWS):
        s = xx[:, g * HD:(g + 1) * HD]
        sh = 1
        while sh < win:
            s = s + pltpu.roll(s, sh, 0)
            sh *= 2
        out.append(s[HALO:] / jnp.minimum(t1, float(win)) - p[:, g * HD:(g + 1) * HD])
    return out


def _pool_fwd(proj, pw, sc, y_all, tm=512):
    rb = tm // HALO

    def body(p_ref, h_ref, gt_ref, pw_ref, sc_ref, y_in, y_ref):
        i = pl.program_id(0)
        halo = jnp.where(i > 0, h_ref[...].astype(F32), 0.0)
        ds = _pool_diff(p_ref[...].astype(F32), halo, i * tm, tm)
        gt = gt_ref[...].astype(F32)
        sl = gt * _sigmoid(gt)
        for g in range(4):
            cs = slice(g * HD, (g + 1) * HD)
            lin = _dot(ds[g].astype(BF16), pw_ref[g].astype(BF16))
            y_ref[:, cs] = (lin * sc_ref[:, cs] * sl[:, cs]).astype(BF16)

    return pl.pallas_call(
        body, grid=(S // tm,),
        in_specs=[pl.BlockSpec((tm, W), lambda i: (i, P_PIN)),
                  pl.BlockSpec((HALO, W), lambda i: (jnp.maximum(i * rb - 1, 0), P_PIN)),
                  pl.BlockSpec((tm, W), lambda i: (i, P_PGATE)),
                  pl.BlockSpec((4, HD, HD), lambda i: (0, 0, 0)), pl.BlockSpec((1, W), lambda i: (0, 0)), ANY],
        out_specs=pl.BlockSpec((None, tm, W), lambda i: (1, i, 0)), out_shape=_sds((4, S, W), BF16),
        input_output_aliases={5: 0}, compiler_params=_cp("parallel"), name="pool_fwd")(proj, proj, proj, pw, sc, y_all)


def _pool_bwd(proj, dy_all, pw, sc, dproj, tm=256):
    ni = S // tm
    rb = tm // HALO
    last_rb = S // HALO - 1
    rx = tm + HALO

    def body(p_ref, h_ref, gt_ref, gh_ref, dy_ref, dyh_ref, pw_ref, sc_ref, dp_in, dp_ref, dpw_ref, dsc_ref, obuf, osem):
        i = pl.program_id(0)

        @pl.when(i == 0)
        def _():
            dpw_ref[...] = jnp.zeros_like(dpw_ref)
            dsc_ref[...] = jnp.zeros_like(dsc_ref)

        halo = jnp.where(i > 0, h_ref[...].astype(F32), 0.0)
        ds = _pool_diff(p_ref[...].astype(F32), halo, i * tm, tm)
        nxt = i < ni - 1
        gx = jnp.concatenate([gt_ref[...], gh_ref[...]], axis=0).astype(F32)
        dyx = jnp.concatenate([dy_ref[...], jnp.where(nxt, dyh_ref[...], 0.0)], axis=0)
        sgx = _sigmoid(gx)
        slx = gx * sgx
        scv = sc_ref[...]
        dlinx = dyx * slx * scv
        t1 = (i * tm + 1 + lax.broadcasted_iota(jnp.int32, (rx, 1), 0)).astype(F32)
        gt, sg, sl, dyv = gx[:tm], sgx[:tm], slx[:tm], dyx[:tm]
        dsl = sg * (1.0 + gt * (1.0 - sg))

        def fill(slot):
            for g, win in enumerate(POOL_WINDOWS):
                cs = slice(g * HD, (g + 1) * HD)
                wv = pw_ref[g].astype(BF16)
                dlb = dlinx[:, cs].astype(BF16)
                ddx = _dot_nt(dlb, wv)
                f = ddx / jnp.minimum(t1, float(win))
                sh = 1
                while sh < win:
                    f = f + pltpu.roll(f, rx - sh, 0)
                    sh *= 2
                slot[:, cs] = (f[:tm] - ddx[:tm]).astype(BF16)
                db = ds[g].astype(BF16)
                lin = _dot(db, wv)
                slot[:, W + g * HD:W + (g + 1) * HD] = (dyv[:, cs] * lin * scv[:, cs] * dsl[:, cs]).astype(BF16)
                dsc_ref[:, cs] += jnp.sum(dyv[:, cs] * sl[:, cs] * lin, axis=0, keepdims=True)
                dpw_ref[g] += _dot_tn(db, dlb[:tm])

        _tile_put(obuf, osem, lambda st: dp_ref.at[pl.ds(st * tm, tm), pl.ds(P_PIN * W, 2 * W)], i, ni, fill)

    mats = pl.BlockSpec((4, HD, HD), lambda i: (0, 0, 0))
    vec = pl.BlockSpec((1, W), lambda i: (0, 0))
    return pl.pallas_call(
        body, grid=(ni,),
        in_specs=[pl.BlockSpec((tm, W), lambda i: (i, P_PIN)),
                  pl.BlockSpec((HALO, W), lambda i: (jnp.maximum(i * rb - 1, 0), P_PIN)),
                  pl.BlockSpec((tm, W), lambda i: (i, P_PGATE)),
                  pl.BlockSpec((HALO, W), lambda i: (jnp.minimum((i + 1) * rb, last_rb), P_PGATE)),
                  pl.BlockSpec((None, tm, W), lambda i: (1, i, 0)),
                  pl.BlockSpec((None, HALO, W), lambda i: (1, jnp.minimum((i + 1) * rb, last_rb), 0)),
                  mats, vec, ANY],
        out_specs=[ANY, mats, vec],
        out_shape=[_sds((S, DIN), BF16), _sds((4, HD, HD), F32), _sds((1, W), F32)],
        scratch_shapes=[pltpu.VMEM((2, tm, 2 * W), BF16), pltpu.SemaphoreType.DMA((2,))],
        input_output_aliases={8: 0}, compiler_params=_cp("arbitrary"), name="pool_bwd")(proj, proj, proj, proj, dy_all, dy_all, pw, sc, dproj)


ATT_STEP = ((1, 4), (4, 1), (4, 1))
ATT_GROUP = 16
ATT_GROUP_BWD = 8


def _att_band():
    qi = lax.broadcasted_iota(jnp.int32, (HD, 2 * HD), 0)
    kj = lax.broadcasted_iota(jnp.int32, (HD, 2 * HD), 1)
    return jnp.logical_and(kj >= qi, kj <= qi + HD), kj < HD


def _att_keys(kp_ref, ko_ref, vp_ref, vo_ref, a, jj):
    if jj == 0:
        return (jnp.concatenate([kp_ref[a], ko_ref[a, :HD, :]], axis=0), jnp.concatenate([vp_ref[a], vo_ref[a, :HD, :]], axis=0))
    return ko_ref[a, (jj - 1) * HD:(jj + 1) * HD, :], vo_ref[a, (jj - 1) * HD:(jj + 1) * HD, :]


def _dilate(src, dst, d, rows, cast=None):
    for r in range(d):
        for h in range(4):
            v = src.at[h][pl.ds(r, rows // d, stride=d), :] if d > 1 else src[h]
            dst[r * 4 + h] = v if cast is None else v.astype(cast)


def _undilate(src, dst, d, rows):
    for r in range(d):
        for h in range(4):
            if d > 1:
                dst.at[h][pl.ds(r, rows // d, stride=d), :] = src[r * 4 + h].astype(F32)
            else:
                dst[h] = src[h].astype(F32)


def _dil_spec(d, tm):
    return pl.BlockSpec((4 * d, tm // d, HD), lambda i: (0, i, 0))


def _att_prep(proj, tm=512):
    def body(q0, q1, q2, k_ref, v_ref, *rest):
        outs, scr = rest[:9], rest[9]
        for j, (src, dsts) in enumerate(((q0, ((0, outs[0]),)), (q1, ((1, outs[1]),)), (q2, ((2, outs[2]),)),
                                         (k_ref, tuple((g, outs[3 + g]) for g in range(3))),
                                         (v_ref, tuple((g, outs[6 + g]) for g in range(3))))):
            for h in range(4):
                scr[j, h] = src[:, h * HD:(h + 1) * HD].astype(F32)
            for g, dst in dsts:
                _dilate(scr.at[j], dst, DILATIONS[g], tm, BF16)

    def piece(p):
        return pl.BlockSpec((tm, W), lambda i: (i, p))

    shapes = [_sds((4 * d, S // d, HD), BF16) for d in DILATIONS]
    res = pl.pallas_call(
        body, grid=(S // tm,),
        in_specs=[piece(P_CQ), piece(P_CQ + 1), piece(P_CQ + 2), piece(P_CK), piece(P_CV)],
        out_specs=[_dil_spec(d, tm) for d in DILATIONS] * 3, out_shape=shapes * 3,
        scratch_shapes=[pltpu.VMEM((5, 4, tm, HD), F32)],
        compiler_params=_cp("parallel"), name="att_prep")(proj, proj, proj, proj, proj)
    return res[0:3], res[3:6], res[6:9]


def _att_specs(g):
    d = DILATIONS[g]
    nres, njb = ATT_STEP[g]
    nb = S // d // HD
    own = pl.BlockSpec((4 * nres, njb * HD, HD), lambda r, j: (r, j, 0))
    prev = pl.BlockSpec((4 * nres, HD, HD), lambda r, j: (r, jnp.maximum(j * njb - 1, 0), 0))
    nxt = pl.BlockSpec((4 * nres, HD, HD), lambda r, j: (r, jnp.minimum((j + 1) * njb, nb - 1), 0))
    return (d // nres, nb // njb), own, prev, nxt


def _att_fwd(q, k, v, g):
    d = DILATIONS[g]
    nres, njb = ATT_STEP[g]
    grid, own, prev, _ = _att_specs(g)

    def body(q_ref, kp_ref, ko_ref, vp_ref, vo_ref, o_ref, l_ref):
        jb = pl.program_id(1)
        band, is_prev = _att_band()
        no_prev = jnp.where(is_prev, jnp.where(jb > 0, 0.0, NEG), 0.0)
        blocks = [(a, jj) for jj in range(njb) for a in range(4 * nres)]
        for g0 in range(0, len(blocks), ATT_GROUP):
            grp = blocks[g0:g0 + ATT_GROUP]
            s, v2 = [], []
            for a, jj in grp:
                k2_, v2_ = _att_keys(kp_ref, ko_ref, vp_ref, vo_ref, a, jj)
                s_ = jnp.where(band, _dot_nt(q_ref[a, jj * HD:(jj + 1) * HD, :], k2_) * SCALE, NEG)
                s.append(s_ + no_prev if jj == 0 else s_)
                v2.append(v2_)
            m = [jnp.max(s_, axis=-1, keepdims=True) for s_ in s]
            e = [jnp.exp(s_ - m_) for s_, m_ in zip(s, m)]
            den = [jnp.sum(e_, axis=-1, keepdims=True) for e_ in e]
            inv = [1.0 / d_ for d_ in den]
            for i, (a, jj) in enumerate(grp):
                rs_ = slice(jj * HD, (jj + 1) * HD)
                o_ref[a, rs_, :] = _dot((e[i] * inv[i]).astype(BF16), v2[i])
                l_ref[a, rs_, :] = jnp.broadcast_to(m[i] + jnp.log(den[i]), (HD, HD))

    return pl.pallas_call(
        body, grid=grid, in_specs=[own, prev, own, prev, own], out_specs=[own, own],
        out_shape=[_sds((4 * d, S // d, HD), F32)] * 2,
        compiler_params=_cp("parallel", "parallel"), name="att_fwd")(q, k, k, v, v)


def _att_mix(os_, ls_, proj, y_all, tm=512):
    def body(o0, o1, o2, l0, l1, l2, gt_ref, y_in, y_ref, om_ref, lt_ref, so1, so2, sl1, sl2):
        _undilate(o1, so1, DILATIONS[1], tm)
        _undilate(o2, so2, DILATIONS[2], tm)
        _undilate(l1, sl1, DILATIONS[1], tm)
        _undilate(l2, sl2, DILATIONS[2], tm)
        for h in range(4):
            a, b, c = l0[h], sl1[h], sl2[h]
            m = jnp.maximum(jnp.maximum(a, b), c)
            ea, eb, ec = jnp.exp(a - m), jnp.exp(b - m), jnp.exp(c - m)
            z = ea + eb + ec
            inv = 1.0 / z
            o = (ea * inv) * o0[h] + (eb * inv) * so1[h] + (ec * inv) * so2[h]
            gt = gt_ref[:, h * HD:(h + 1) * HD].astype(F32)
            om_ref[h] = o
            lt_ref[h] = m + jnp.log(z)
            y_ref[:, h * HD:(h + 1) * HD] = (o * (gt * _sigmoid(gt))).astype(BF16)

    dil = [_dil_spec(d, tm) for d in DILATIONS]
    return pl.pallas_call(
        body, grid=(S // tm,),
        in_specs=dil * 2 + [pl.BlockSpec((tm, W), lambda i: (i, P_CGATE)), ANY],
        out_specs=[pl.BlockSpec((None, tm, W), lambda i: (2, i, 0)), dil[0], dil[0]],
        out_shape=[_sds((4, S, W), BF16), _sds((4, S, HD), F32), _sds((4, S, HD), F32)],
        scratch_shapes=[pltpu.VMEM((4, tm, HD), F32)] * 4,
        input_output_aliases={7: 0}, compiler_params=_cp("parallel"), name="att_mix")(*os_, *ls_, proj, y_all)


def _att_bwd_pre(dy_all, proj, om, lse, dproj, tm=512):
    def body(dy_ref, gt_ref, om_ref, ls_ref, dp_in, *rest):
        dos, dls, lss, dp_ref, sdo, sdl = rest[0:3], rest[3:6], rest[6:8], rest[8], rest[9], rest[10]
        for h in range(4):
            cs = slice(h * HD, (h + 1) * HD)
            gt = gt_ref[:, cs].astype(F32)
            sg = _sigmoid(gt)
            dyv = dy_ref[:, cs]
            o = om_ref[h]
            do = dyv * (gt * sg)
            dp_ref[:, cs] = (dyv * o * (sg * (1.0 + gt * (1.0 - sg)))).astype(BF16)
            sdo[h] = do
            sdl[h] = jnp.broadcast_to(jnp.sum(do * o, axis=-1, keepdims=True), (tm, HD))
        for g, d in enumerate(DILATIONS):
            _dilate(sdo, dos[g], d, tm, BF16)
            _dilate(sdl, dls[g], d, tm)
            if g > 0:
                _dilate(ls_ref, lss[g - 1], d, tm)

    dil = [_dil_spec(d, tm) for d in DILATIONS]
    gcol = pl.BlockSpec((tm, W), lambda i: (i, P_CGATE))
    res = pl.pallas_call(
        body, grid=(S // tm,),
        in_specs=[pl.BlockSpec((None, tm, W), lambda i: (2, i, 0)), gcol, dil[0], dil[0], ANY],
        out_specs=dil + dil + dil[1:] + [gcol],
        out_shape=([_sds((4 * d, S // d, HD), BF16) for d in DILATIONS] + [_sds((4 * d, S // d, HD), F32) for d in DILATIONS]
                   + [_sds((4 * d, S // d, HD), F32) for d in DILATIONS[1:]] + [_sds((S, DIN), BF16)]),
        scratch_shapes=[pltpu.VMEM((4, tm, HD), F32)] * 2,
        input_output_aliases={4: 8}, compiler_params=_cp("parallel"), name="att_bwd_pre")(dy_all, proj, om, lse, dproj)
    return res[0:3], res[3:6], [lse] + list(res[6:8]), res[8]


def _att_bwd(q, k, v, do, lse, delta, g):
    d = DILATIONS[g]
    nres, njb = ATT_STEP[g]
    grid, own, prev, nxt = _att_specs(g)

    def body(qa_ref, qn_ref, kp_ref, ko_ref, vp_ref, vo_ref, doa_ref, don_ref, la_ref, ln_ref, da_ref, dn_ref,
             dq_ref, dk_ref, dv_ref):
        jb = pl.program_id(1)
        band, is_prev = _att_band()
        m_next = lax.broadcasted_iota(jnp.int32, (HD, HD), 1) >= lax.broadcasted_iota(jnp.int32, (HD, HD), 0)
        has_prev = jnp.where(is_prev, jnp.where(jb > 0, 1.0, 0.0), 1.0)
        has_next = jnp.where(jb < grid[1] - 1, 1.0, 0.0)

        def wide(t):
            return jnp.concatenate([t, t], axis=1)

        blocks = [(a, jj) for jj in range(njb) for a in range(4 * nres)]
        for g0 in range(0, len(blocks), ATT_GROUP_BWD):
            grp = blocks[g0:g0 + ATT_GROUP_BWD]
            ops = []
            for a, jj in grp:
                rs_ = slice(jj * HD, (jj + 1) * HD)
                k2, v2 = _att_keys(kp_ref, ko_ref, vp_ref, vo_ref, a, jj)
                if jj == njb - 1:
                    qn, don, lsn, dln, fn = qn_ref[a], don_ref[a], ln_ref[a], dn_ref[a], has_next
                else:
                    ns = slice((jj + 1) * HD, (jj + 2) * HD)
                    qn, don, lsn, dln, fn = qa_ref[a, ns, :], doa_ref[a, ns, :], la_ref[a, ns, :], da_ref[a, ns, :], None
                ops.append(dict(qa=qa_ref[a, rs_, :], doa=doa_ref[a, rs_, :], lsa=wide(la_ref[a, rs_, :]),
                                dla=wide(da_ref[a, rs_, :]), k2=k2, v2=v2, ko=ko_ref[a, rs_, :], vo=vo_ref[a, rs_, :],
                                qn=qn, don=don, lsn=lsn, dln=dln, fn=fn, first=jj == 0))
            sa = [_dot_nt(o["qa"], o["k2"]) for o in ops]
            dpa = [_dot_nt(o["doa"], o["v2"]) for o in ops]
            sn = [_dot_nt(o["qn"], o["ko"]) for o in ops]
            dpn = [_dot_nt(o["don"], o["vo"]) for o in ops]
            pa, pn = [], []
            for o, sa_, sn_ in zip(ops, sa, sn):
                p_ = jnp.where(band, jnp.exp(sa_ * SCALE - o["lsa"]), 0.0)
                pa.append(p_ * has_prev if o["first"] else p_)
                p_ = jnp.where(m_next, jnp.exp(sn_ * SCALE - o["lsn"]), 0.0)
                pn.append(p_ if o["fn"] is None else p_ * o["fn"])
            dsa = [(p_ * (dp_ - o["dla"]) * SCALE).astype(BF16) for p_, dp_, o in zip(pa, dpa, ops)]
            dsn = [(p_ * (dp_ - o["dln"]) * SCALE).astype(BF16) for p_, dp_, o in zip(pn, dpn, ops)]
            for i, (a, jj) in enumerate(grp):
                rs_ = slice(jj * HD, (jj + 1) * HD)
                o = ops[i]
                dq_ref[a, rs_, :] = _dot(dsa[i], o["k2"])
                q2 = jnp.concatenate([o["qa"], o["qn"]], axis=0)
                do2 = jnp.concatenate([o["doa"], o["don"]], axis=0)
                dk_ref[a, rs_, :] = _dot_tn(jnp.concatenate([dsa[i][:, HD:], dsn[i]], axis=0), q2)
                dv_ref[a, rs_, :] = _dot_tn(jnp.concatenate([pa[i][:, HD:].astype(BF16), pn[i].astype(BF16)], axis=0), do2)

    return pl.pallas_call(
        body, grid=grid, in_specs=[own, nxt, prev, own, prev, own, own, nxt, own, nxt, own, nxt],
        out_specs=[own, own, own], out_shape=[_sds((4 * d, S // d, HD), F32)] * 3,
        compiler_params=_cp("parallel", "parallel"), name="att_bwd")(q, q, k, k, v, v, do, do, lse, lse, delta, delta)


def _att_bwd_post(dqs, dks, dvs, dproj, tm=512):
    def body(*refs):
        dq, dk, dv, dp_ref, scr = refs[0:3], refs[3:6], refs[6:9], refs[10], refs[11]
        for g in range(3):
            _undilate(dq[g], scr, DILATIONS[g], tm)
            for h in range(4):
                dp_ref[:, g * W + h * HD:g * W + (h + 1) * HD] = scr[h].astype(BF16)
        for j, parts in enumerate((dk, dv)):
            acc = None
            for g in range(3):
                _undilate(parts[g], scr, DILATIONS[g], tm)
                vals = [scr[h] for h in range(4)]
                acc = vals if acc is None else [x + y for x, y in zip(acc, vals)]
            for h in range(4):
                dp_ref[:, (3 + j) * W + h * HD:(3 + j) * W + (h + 1) * HD] = acc[h].astype(BF16)

    dil = [_dil_spec(d, tm) for d in DILATIONS]
    return pl.pallas_call(
        body, grid=(S // tm,), in_specs=dil * 3 + [ANY],
        out_specs=pl.BlockSpec((tm, 5 * W), lambda i: (i, 1)), out_shape=_sds((S, DIN), BF16),
        scratch_shapes=[pltpu.VMEM((4, tm, HD), F32)],
        input_output_aliases={9: 0}, compiler_params=_cp("parallel"), name="att_bwd_post")(*dqs, *dks, *dvs, dproj)


def _mem_kv_fwd(mem_n, wkv):
    m = mem_n.shape[0]

    def body(a_ref, w_ref, o_ref):
        o_ref[...] = _dot(a_ref[...], w_ref[...])

    return pl.pallas_call(body, out_shape=_sds((m, 2 * W), F32), compiler_params=_cp(), name="mem_kv_fwd")(mem_n, wkv)


def _mem_fwd(proj, kv, y_all, tm=512):
    m = kv.shape[0]

    def body(q_ref, gt_ref, kv_ref, y_in, y_ref):
        gt = gt_ref[...].astype(F32)
        sl = gt * _sigmoid(gt)
        hs = [slice(h * HD, (h + 1) * HD) for h in range(4)]
        s = [_dot_nt(q_ref[:, cs].astype(BF16), kv_ref[:, cs].astype(BF16)) * SCALE for cs in hs]
        e = [jnp.exp(s_ - jnp.max(s_, axis=-1, keepdims=True)) for s_ in s]
        p = [(e_ * (1.0 / jnp.sum(e_, axis=-1, keepdims=True))).astype(BF16) for e_ in e]
        for h, cs in enumerate(hs):
            o = _dot(p[h], kv_ref[:, W + h * HD:W + (h + 1) * HD].astype(BF16))
            y_ref[:, cs] = (o * sl[:, cs]).astype(BF16)

    return pl.pallas_call(
        body, grid=(S // tm,),
        in_specs=[pl.BlockSpec((tm, W), lambda i: (i, P_MQ)), pl.BlockSpec((tm, W), lambda i: (i, P_MGATE)),
                  pl.BlockSpec((m, 2 * W), lambda i: (0, 0)), ANY],
        out_specs=pl.BlockSpec((None, tm, W), lambda i: (3, i, 0)), out_shape=_sds((4, S, W), BF16),
        input_output_aliases={3: 0}, compiler_params=_cp("parallel"), name="mem_fwd")(proj, proj, kv, y_all)


def _mem_bwd(proj, kv, dy_all, dproj, tm=512):
    m = kv.shape[0]
    ni = S // tm

    def body(q_ref, gt_ref, kv_ref, dy_ref, dp_in, dp_ref, dkv_ref, obuf, osem):
        i = pl.program_id(0)

        @pl.when(i == 0)
        def _():
            dkv_ref[...] = jnp.zeros_like(dkv_ref)

        gt = gt_ref[...].astype(F32)
        sg = _sigmoid(gt)
        sl = gt * sg
        dsl = sg * (1.0 + gt * (1.0 - sg))
        dyv = dy_ref[...]

        def fill(slot):
            hs = [slice(h * HD, (h + 1) * HD) for h in range(4)]
            vss = [slice(W + h * HD, W + (h + 1) * HD) for h in range(4)]
            q = [q_ref[:, cs].astype(BF16) for cs in hs]
            k = [kv_ref[:, cs].astype(BF16) for cs in hs]
            v = [kv_ref[:, vs].astype(BF16) for vs in vss]
            dob = [(dyv[:, cs] * sl[:, cs]).astype(BF16) for cs in hs]
            s = [_dot_nt(q_, k_) * SCALE for q_, k_ in zip(q, k)]
            dp = [_dot_nt(d_, v_) for d_, v_ in zip(dob, v)]
            e = [jnp.exp(s_ - jnp.max(s_, axis=-1, keepdims=True)) for s_ in s]
            p = [e_ * (1.0 / jnp.sum(e_, axis=-1, keepdims=True)) for e_ in e]
            pb = [p_.astype(BF16) for p_ in p]
            dsb = [(p_ * (dp_ - jnp.sum(dp_ * p_, axis=-1, keepdims=True)) * SCALE).astype(BF16) for p_, dp_ in zip(p, dp)]
            for h, (cs, vs) in enumerate(zip(hs, vss)):
                o = _dot(pb[h], v[h])
                slot[:, cs] = _dot(dsb[h], k[h]).astype(BF16)
                slot[:, vs] = (dyv[:, cs] * o * dsl[:, cs]).astype(BF16)
                dkv_ref[:, cs] += _dot_tn(dsb[h], q[h])
                dkv_ref[:, vs] += _dot_tn(pb[h], dob[h])

        _tile_put(obuf, osem, lambda st: dp_ref.at[pl.ds(st * tm, tm), pl.ds(P_MQ * W, 2 * W)], i, ni, fill)

    return pl.pallas_call(
        body, grid=(ni,),
        in_specs=[pl.BlockSpec((tm, W), lambda i: (i, P_MQ)), pl.BlockSpec((tm, W), lambda i: (i, P_MGATE)),
                  pl.BlockSpec((m, 2 * W), lambda i: (0, 0)), pl.BlockSpec((None, tm, W), lambda i: (3, i, 0)), ANY],
        out_specs=[ANY, pl.BlockSpec((m, 2 * W), lambda i: (0, 0))],
        out_shape=[_sds((S, DIN), BF16), _sds((m, 2 * W), F32)],
        scratch_shapes=[pltpu.VMEM((2, tm, 2 * W), BF16), pltpu.SemaphoreType.DMA((2,))],
        input_output_aliases={4: 0}, compiler_params=_cp("arbitrary"), name="mem_bwd")(proj, proj, kv, dy_all, dproj)


def _mem_kv_bwd(mem, g, mem_n, wkv, dkv):
    m = mem.shape[0]

    def body(x_ref, g_ref, a_ref, w_ref, d_ref, dw_ref, dg_ref):
        db = d_ref[...].astype(BF16)
        dw_ref[...] = _dot_tn(a_ref[...], db).astype(BF16)
        dn = _dot_nt(db, w_ref[...])
        xv = x_ref[...]
        xh = xv * lax.rsqrt(jnp.mean(xv * xv, axis=-1, keepdims=True) + EPS)
        dg_ref[...] = jnp.sum(dn * xh, axis=0, keepdims=True)

    return pl.pallas_call(
        body, out_shape=[_sds((D, 2 * W), BF16), _sds((1, D), F32)], compiler_params=_cp(), name="mem_kv_bwd")(mem, g, mem_n, wkv, dkv)


def _layer_fwd(x, h, ht, mem_n, p, wg):
    win, wkv, wbr, wo = wg
    proj = _proj_fwd(h, win)
    y_all = lax.empty((4, S, W), BF16)
    y_all = _gmlp_fwd(proj, p["gm_ln_g"], p["gm_ln_b"], p["gm_ws"], p["gm_bias"], y_all)
    y_all = _pool_fwd(proj, p["pool_w"], p["pool_scale"], y_all)
    qs, ks, vs = _att_prep(proj)
    os_, ls_ = zip(*[_att_fwd(qs[g], ks[g], vs[g], g) for g in range(3)])
    y_all, om, lse = _att_mix(os_, ls_, proj, y_all)
    kv = _mem_kv_fwd(mem_n, wkv)
    y_all = _mem_fwd(proj, kv, y_all)
    z = _merge_fwd(y_all, wbr, proj)
    x_new = _out_fwd(z, wo, x)
    return x_new, dict(x=x, ht=ht, proj=proj, y_all=y_all, om=om, lse=lse, mem_n=mem_n, kv=kv, z=z, qkv=(qs, ks, vs))


GRAD_PARTS = ((2, D // 2, CW), (2, D // 8, 2 * W), (2, 2 * W, D // NCHIP), (2, D // 8, D))
SUM_TILE = (64, 128, 256, 128)
ADAM_TILE = (128, 256, 2048, 256)
PLACE_TILE = (256, 256, 512, 256)


def _layer_bwd(dx, mem, p, wg, sv, exchange):
    win, wkv, wbr, wo = wg
    proj = sv["proj"]
    dz, d_wo = _out_bwd(dx, sv["z"], wo)
    dproj = lax.empty((S, DIN), BF16)
    dproj, dy_all, d_wbr = _merge_bwd(dz, sv["y_all"], wbr, proj, dproj)
    dproj, d_ws, d_bs, d_lg, d_lb = _gmlp_bwd(proj, dy_all, p["gm_ln_g"], p["gm_ln_b"], p["gm_ws"], p["gm_bias"], dproj)
    dproj, d_pw, d_sc = _pool_bwd(proj, dy_all, p["pool_w"], p["pool_scale"], dproj)
    dos, dls, lss, dproj = _att_bwd_pre(dy_all, proj, sv["om"], sv["lse"], dproj)
    qs, ks, vs = sv["qkv"]
    dqs, dks, dvs = zip(*[_att_bwd(qs[g], ks[g], vs[g], dos[g], lss[g], dls[g], g) for g in range(3)])
    dproj = _att_bwd_post(dqs, dks, dvs, dproj)
    dproj, dkv = _mem_bwd(proj, sv["kv"], dy_all, dproj)
    d_wkv, d_mg = _mem_kv_bwd(mem, p["mem_norm_g"], sv["mem_n"], wkv, dkv)
    d_win = _proj_bwd_w(sv["ht"], dproj)
    big = tuple(t.reshape((NCHIP,) + s) for t, s in zip((d_win, d_wkv, d_wbr, d_wo), GRAD_PARTS))
    inflight = exchange(big)
    dh = _proj_bwd_x(dproj, win, inflight[-1])
    dx_in, d_ng = _rms_bwd(dh, sv["x"], p["norm_g"], dx)
    small = dict(norm_g=d_ng, gm_ln_g=d_lg, gm_ln_b=d_lb, gm_ws=d_ws, gm_bs=d_bs[:, :, 0], pool_w=d_pw, pool_scale=d_sc, mem_norm_g=d_mg)
    return dx_in, (big,) + inflight, small


_SMALL = ("norm_g", "gm_ln_g", "gm_ln_b", "gm_ws", "gm_bs", "pool_w", "pool_scale", "mem_norm_g")


def _layer_params(l, norm_g, gm_ln_g, gm_ln_b, gm_ws, gm_bs, pool_w, pool_scale, mem_norm_g):
    return dict(norm_g=norm_g[l][None], gm_ln_g=gm_ln_g[l][None], gm_ln_b=gm_ln_b[l][None], gm_ws=gm_ws[l],
                gm_bias=jnp.broadcast_to(gm_bs[l][:, :, None], (4, HD, HD)), pool_w=pool_w[l],
                pool_scale=pool_scale[l][None], mem_norm_g=mem_norm_g[l][None])


def kernel(x, mem, norm_g, w_in, gm_ln_g, gm_ln_b, gm_ws, gm_bs, pool_w, pool_scale, mem_norm_g, w_mem_kv, w_branch, w_out, final_norm_g, loss_target, m_norm_g, m_w_in, m_gm_ln_g, m_gm_ln_b, m_gm_ws, m_gm_bs, m_pool_w, m_pool_scale, m_mem_norm_g, m_w_mem_kv, m_w_branch, m_w_out, m_final_norm_g, v_norm_g, v_w_in, v_gm_ln_g, v_gm_ln_b, v_gm_ws, v_gm_bs, v_pool_w, v_pool_scale, v_mem_norm_g, v_w_mem_kv, v_w_branch, v_w_out, v_final_norm_g):
    xs, memv, tgt = x[0], mem[0], loss_target[0]
    params = [_layer_params(l, norm_g, gm_ln_g, gm_ln_b, gm_ws, gm_bs, pool_w, pool_scale, mem_norm_g) for l in range(NL)]

    shards = (w_in.astype(BF16), w_mem_kv.astype(BF16), w_branch.astype(BF16), w_out.astype(BF16))
    lands, gsems, after = _gather_start(shards)

    saved, wgs = [], []
    mem_ns = [_rms_fwd(memv, params[l]["mem_norm_g"], memv.shape[0]) for l in range(NL)]
    for l in range(NL):
        h, ht = _rms_fwd_t(xs, params[l]["norm_g"])
        if l == 0:
            got, relay_sems = _gather_relay(shards, lands[0], gsems[0], [after, h] + mem_ns)
            got = _gather_wait_relay(got, relay_sems)
        else:
            got = _gather_wait(l, shards, lands[l], gsems[l], after)
        got = [_place_own(got[k], shards[k], l, PLACE_TILE[k]) for k in range(4)]
        wgs.append((got[0], got[1].reshape(D, 2 * W), got[2], got[3].reshape(D, D)))
        xs, sv = _layer_fwd(xs, h, ht, mem_ns[l], params[l], wgs[l])
        saved.append(sv)
        after = xs
    dx, d_fg, ls = _loss_head(xs, tgt, final_norm_g[None])
    loss = lax.psum(ls[0, 0], ("x", "y", "c"))

    flight, small = [None] * NL, [None] * NL
    for l in reversed(range(NL)):
        dx, flight[l], small[l] = _layer_bwd(dx, memv, params[l], wgs[l], saved[l], functools.partial(_exch_start, l))
    grad_x = dx[None]

    leaves = [jnp.stack([small[l][n] for l in range(NL)]) for n in _SMALL] + [d_fg]
    sizes = [t.size for t in leaves]
    packed = jnp.concatenate([t.reshape(-1, 128) for t in leaves], axis=0)
    rows = packed.shape[0]
    small_zone, small_sems, after = _small_start(packed)

    ws = dict(norm_g=norm_g, w_in=w_in, gm_ln_g=gm_ln_g, gm_ln_b=gm_ln_b, gm_ws=gm_ws, gm_bs=gm_bs, pool_w=pool_w,
              pool_scale=pool_scale, mem_norm_g=mem_norm_g, w_mem_kv=w_mem_kv, w_branch=w_branch, w_out=w_out,
              final_norm_g=final_norm_g)
    ms = dict(norm_g=m_norm_g, w_in=m_w_in, gm_ln_g=m_gm_ln_g, gm_ln_b=m_gm_ln_b, gm_ws=m_gm_ws, gm_bs=m_gm_bs,
              pool_w=m_pool_w, pool_scale=m_pool_scale, mem_norm_g=m_mem_norm_g, w_mem_kv=m_w_mem_kv,
              w_branch=m_w_branch, w_out=m_w_out, final_norm_g=m_final_norm_g)
    vs = dict(norm_g=v_norm_g, w_in=v_w_in, gm_ln_g=v_gm_ln_g, gm_ln_b=v_gm_ln_b, gm_ws=v_gm_ws, gm_bs=v_gm_bs,
              pool_w=v_pool_w, pool_scale=v_pool_scale, mem_norm_g=v_mem_norm_g, w_mem_kv=v_w_mem_kv,
              w_branch=v_w_branch, w_out=v_w_out, final_norm_g=v_final_norm_g)

    big = ("w_in", "w_mem_kv", "w_branch", "w_out")
    acc = {n: [lax.empty((ws[n].size // ws[n].shape[-1], ws[n].shape[-1]), F32) for _ in range(4)] for n in big}
    for l in reversed(range(NL)):
        parts, zones, sems, _ = flight[l]
        zones = _exch_wait(l, parts, zones, sems, after)
        full = _share_full([_sum_half(parts[k], zones[k], SUM_TILE[k]) for k in range(4)])
        for k, n in enumerate(big):
            acc[n] = _adamw_layer(l, ws[n], full[k], ms[n], vs[n], acc[n], ADAM_TILE[k])
        after = acc["w_in"][0]
    grads, upd = {}, {}
    for n in big:
        d_, m_, v_, g_ = (t.reshape(ws[n].shape) for t in acc[n])
        grads[n], upd[n] = g_, (d_, m_, v_)

    small_zone = _small_wait(packed, small_zone, small_sems, after)
    tot = _sum_small(packed, small_zone, max(t for t in range(8, 513, 8) if rows % t == 0))
    offs = [0]
    for sz in sizes:
        offs.append(offs[-1] + sz // 128)
    for i, n in enumerate(_SMALL + ("final_norm_g",)):
        grads[n] = tot[offs[i]:offs[i + 1]].reshape(ws[n].shape)
        upd[n] = _adamw(ws[n], grads[n], ms[n], vs[n])
    order = ("norm_g", "w_in", "gm_ln_g", "gm_ln_b", "gm_ws", "gm_bs", "pool_w", "pool_scale", "mem_norm_g", "w_mem_kv",
             "w_branch", "w_out", "final_norm_g")
    return (loss, grad_x, *[grads[n] for n in order], *[upd[n][0] for n in order], *[upd[n][1] for n in order],
            *[upd[n][2] for n in order])
```

```python
import functools
import math

import jax
import jax.numpy as jnp
from jax import lax
from jax.experimental import pallas as pl
from jax.experimental.pallas import tpu as pltpu

F32 = jnp.float32
BF16 = jnp.bfloat16

S = 4096
D = 1024
W = 512
DIN = 10752
NL = 4
NCHIP = 4
NDEV = 8
CW = DIN // NCHIP
TN_IN = 896
NJ = CW // TN_IN
HD = 128
EPS = 1e-6
NEG = -1e30
SCALE = HD ** -0.5
INV_SQRT2 = 1.0 / math.sqrt(2.0)
INV_SQRT2PI = 1.0 / math.sqrt(2.0 * math.pi)
POOL_WINDOWS = (2, 4, 8, 16)
DILATIONS = (1, 4, 16)
HALO = 16
NPIECE = DIN // W
P_AGATE, P_PIN, P_PGATE, P_CQ, P_CK, P_CV, P_CGATE, P_MQ, P_MGATE, P_GM = 2, 3, 4, 5, 8, 9, 10, 11, 12, 13
VMEM_LIMIT = 56 * 1024 * 1024

ADAM_LR, ADAM_B1, ADAM_B2, ADAM_EPS, ADAM_WD, ADAM_STEP = 0.001, 0.9, 0.999, 1e-08, 0.01, 10

MESH = pl.DeviceIdType.MESH
ANY = pl.BlockSpec(memory_space=pl.ANY)


def _cp(*sem):
    return pltpu.CompilerParams(dimension_semantics=sem or None, vmem_limit_bytes=VMEM_LIMIT)


def _sds(shape, dtype):
    return jax.ShapeDtypeStruct(shape, dtype)


def _sigmoid(v):
    return 1.0 / (1.0 + jnp.exp(-v))


def _dot(a, b):
    return jnp.dot(a, b, preferred_element_type=F32)


def _dot_nt(a, b):
    return lax.dot_general(a, b, (((1,), (1,)), ((), ())), preferred_element_type=F32)


def _dot_tn(a, b):
    return lax.dot_general(a, b, (((0,), (0,)), ((), ())), preferred_element_type=F32)


def _tile_put(buf, sem, dst_of, step, nsteps, fill):
    slot = step % 2

    def copy(s, st):
        return pltpu.make_async_copy(buf.at[s], dst_of(st), sem.at[s])

    @pl.when(step >= 2)
    def _():
        copy(slot, step).wait()

    fill(buf.at[slot])
    copy(slot, step).start()

    @pl.when(step == nsteps - 1)
    def _():
        if nsteps >= 2:
            copy(1 - slot, step).wait()
        copy(slot, step).wait()


def _my_pos():
    return lax.axis_index("x"), lax.axis_index("y"), lax.axis_index("c")


_CHIP_REL = ((1, 0), (0, 1), (1, 1))
_DEV_REL = tuple((dx, dy, dc) for dx in (0, 1) for dy in (0, 1) for dc in (0, 1))[1:]


HBM = pl.BlockSpec(memory_space=pltpu.HBM)
SEM = pl.BlockSpec(memory_space=pltpu.SEMAPHORE)
EFFECT = pltpu.SideEffectType.DATAFLOW_SIDE_EFFECTING
N_GATHER = 3 * 4
N_EXCH = 7 * 4


def _in_hbm(t):
    return pltpu.with_memory_space_constraint(t, pltpu.HBM)


def _gather_start(shards):
    nk = len(shards)
    lands = [pltpu.HBM((NCHIP,) + s.shape[1:], BF16) for s in shards for _ in range(NL)]

    def body(*refs):
        ins, outs = refs[:nk], refs[nk:nk + nk * NL]
        sems = refs[nk + nk * NL:nk + nk * NL + 2 * NL]
        token = refs[-1]
        x, y, c = _my_pos()
        me = 2 * x + y
        for l in range(NL):
            for r, (dx, dy) in enumerate(_CHIP_REL):
                for k in range(nk):
                    src, dst = ins[k].at[l], outs[k * NL + l].at[me]
                    if l == 0:
                        hf = pl.ds(c * (shards[k].shape[1] // 2), shards[k].shape[1] // 2)
                        src, dst = src.at[hf], dst.at[hf]
                    pltpu.make_async_remote_copy(
                        src_ref=src, dst_ref=dst, send_sem=sems[2 * l].at[r * nk + k],
                        recv_sem=sems[2 * l + 1].at[r * nk + k], device_id=(x ^ dx, y ^ dy, c), device_id_type=MESH).start()
        token[...] = jnp.zeros_like(token)

    res = pl.pallas_call(
        body, name="gather_start",
        out_shape=lands + [pltpu.SemaphoreType.DMA((N_GATHER,))] * (2 * NL) + [_sds((8, 128), F32)],
        in_specs=[HBM] * nk, out_specs=[HBM] * (nk * NL) + [SEM] * (2 * NL) + [pl.BlockSpec(memory_space=pltpu.VMEM)],
        compiler_params=pltpu.CompilerParams(has_side_effects=EFFECT))(*[_in_hbm(s) for s in shards])
    lands = [[res[k * NL + l] for k in range(nk)] for l in range(NL)]
    sems = [(res[nk * NL + 2 * l], res[nk * NL + 2 * l + 1]) for l in range(NL)]
    return lands, sems, res[-1]


def _gather_relay(shards, lands, sems, after):
    nk = len(shards)
    half = [s.shape[1] // 2 for s in shards]

    na = len(after)

    def body(*refs):
        ins, land = refs[:nk], refs[nk:2 * nk]
        send, recv = refs[2 * nk], refs[2 * nk + 1]
        send2, recv2 = refs[3 * nk + 2 + na], refs[3 * nk + 3 + na]
        x, y, c = _my_pos()
        for r, (dx, dy) in enumerate(_CHIP_REL):
            cx, cy = x ^ dx, y ^ dy
            for k in range(nk):
                hf = pl.ds(c * half[k], half[k])
                got = land[k].at[2 * cx + cy].at[hf]
                cp = pltpu.make_async_remote_copy(
                    src_ref=ins[k].at[0].at[hf], dst_ref=got, send_sem=send.at[r * nk + k],
                    recv_sem=recv.at[r * nk + k], device_id=(cx, cy, c), device_id_type=MESH)
                cp.wait_send()
                cp.wait_recv()
                pltpu.make_async_remote_copy(
                    src_ref=got, dst_ref=got, send_sem=send2.at[r * nk + k], recv_sem=recv2.at[r * nk + k],
                    device_id=(x, y, 1 - c), device_id_type=MESH).start()

    res = pl.pallas_call(
        body, name="gather_relay",
        out_shape=[pltpu.HBM(t.shape, t.dtype) for t in lands] + [pltpu.SemaphoreType.DMA((N_GATHER,))] * 2,
        in_specs=[ANY] * nk + [HBM] * nk + [SEM, SEM] + [ANY] * na, out_specs=[HBM] * nk + [SEM, SEM],
        input_output_aliases={nk + k: k for k in range(nk)},
        compiler_params=pltpu.CompilerParams(has_side_effects=EFFECT))(*shards, *lands, *sems, *after)
    return res[:nk], (res[nk], res[nk + 1])


def _gather_wait_relay(lands, sems):
    nk = len(lands)
    half = [t.shape[1] // 2 for t in lands]

    def body(*refs):
        land = refs[:nk]
        send, recv = refs[nk], refs[nk + 1]
        x, y, c = _my_pos()
        for r, (dx, dy) in enumerate(_CHIP_REL):
            chip = 2 * (x ^ dx) + (y ^ dy)
            for k in range(nk):
                mine = land[k].at[chip].at[pl.ds(c * half[k], half[k])]
                theirs = land[k].at[chip].at[pl.ds((1 - c) * half[k], half[k])]
                cp = pltpu.make_async_remote_copy(
                    src_ref=mine, dst_ref=theirs, send_sem=send.at[r * nk + k], recv_sem=recv.at[r * nk + k],
                    device_id=(x, y, 1 - c), device_id_type=MESH)
                cp.wait_send()
                cp.wait_recv()

    return pl.pallas_call(
        body, name="gather_wait_relay", out_shape=[pltpu.HBM(t.shape, t.dtype) for t in lands],
        in_specs=[HBM] * nk + [SEM, SEM], out_specs=[HBM] * nk, input_output_aliases={k: k for k in range(nk)},
        compiler_params=pltpu.CompilerParams(has_side_effects=EFFECT))(*lands, *sems)


def _gather_wait(l, shards, lands, sems, after):
    nk = len(shards)

    def body(*refs):
        ins, land = refs[:nk], refs[nk:2 * nk]
        send, recv = refs[2 * nk], refs[2 * nk + 1]
        x, y, c = _my_pos()
        for r, (dx, dy) in enumerate(_CHIP_REL):
            cx, cy = x ^ dx, y ^ dy
            for k in range(nk):
                cp = pltpu.make_async_remote_copy(
                    src_ref=ins[k].at[l], dst_ref=land[k].at[2 * cx + cy], send_sem=send.at[r * nk + k],
                    recv_sem=recv.at[r * nk + k], device_id=(cx, cy, c), device_id_type=MESH)
                cp.wait_send()
                cp.wait_recv()

    return pl.pallas_call(
        body, name=f"gather_wait_{l}", out_shape=[pltpu.HBM(t.shape, t.dtype) for t in lands],
        in_specs=[ANY] * nk + [HBM] * nk + [SEM, SEM, ANY], out_specs=[HBM] * nk,
        input_output_aliases={nk + k: k for k in range(nk)},
        compiler_params=pltpu.CompilerParams(has_side_effects=EFFECT))(*shards, *lands, *sems, after)


def _place_own(land, shard, l, tr):
    _, rows, cols = shard.shape[0], shard.shape[-2], shard.shape[-1]
    lead = shard.shape[1:-2]
    nlead = math.prod(lead)
    sh = shard.reshape((NL, nlead, rows, cols))
    ld = land.reshape((NCHIP, nlead, rows, cols))
    me = (2 * lax.axis_index("x") + lax.axis_index("y")).astype(jnp.int32).reshape(1)

    def body(me_ref, s_ref, l_in, o_ref):
        o_ref[...] = s_ref[...]

    out = pl.pallas_call(
        body,
        grid_spec=pltpu.PrefetchScalarGridSpec(
            num_scalar_prefetch=1, grid=(nlead, rows // tr),
            in_specs=[pl.BlockSpec((None, None, tr, cols), lambda b, i, me_ref: (l, b, i, 0)), ANY],
            out_specs=pl.BlockSpec((None, None, tr, cols), lambda b, i, me_ref: (me_ref[0], b, i, 0))),
        out_shape=_sds(ld.shape, BF16), input_output_aliases={2: 0},
        compiler_params=_cp("parallel", "parallel"), name="place_own")(me, sh, ld)
    return out.reshape(land.shape)


def _exch_start(l, parts):
    nk = len(parts)

    def body(*refs):
        ins, outs = refs[:nk], refs[nk:2 * nk]
        send, recv, token = refs[2 * nk:]
        x, y, c = _my_pos()
        for r, (dx, dy, dc) in enumerate(_DEV_REL):
            px, py, pc = x ^ dx, y ^ dy, c ^ dc
            for k in range(nk):
                pltpu.make_async_remote_copy(
                    src_ref=ins[k].at[2 * px + py, pc], dst_ref=outs[k].at[r], send_sem=send.at[r * nk + k],
                    recv_sem=recv.at[r * nk + k], device_id=(px, py, pc), device_id_type=MESH).start()
        token[...] = jnp.zeros_like(token)

    res = pl.pallas_call(
        body, name=f"exch_start_{l}",
        out_shape=[pltpu.HBM((7,) + p.shape[2:], BF16) for p in parts] + [pltpu.SemaphoreType.DMA((N_EXCH,))] * 2 + [_sds((8, 128), F32)],
        in_specs=[HBM] * nk, out_specs=[HBM] * nk + [SEM, SEM, pl.BlockSpec(memory_space=pltpu.VMEM)],
        compiler_params=pltpu.CompilerParams(has_side_effects=EFFECT))(*[_in_hbm(p) for p in parts])
    return res[:nk], (res[nk], res[nk + 1]), res[-1]


def _exch_wait(l, parts, lands, sems, after):
    nk = len(parts)

    def body(*refs):
        ins, land = refs[:nk], refs[nk:2 * nk]
        send, recv = refs[2 * nk], refs[2 * nk + 1]
        x, y, c = _my_pos()
        for r, (dx, dy, dc) in enumerate(_DEV_REL):
            px, py, pc = x ^ dx, y ^ dy, c ^ dc
            for k in range(nk):
                cp = pltpu.make_async_remote_copy(
                    src_ref=ins[k].at[2 * px + py, pc], dst_ref=land[k].at[r], send_sem=send.at[r * nk + k],
                    recv_sem=recv.at[r * nk + k], device_id=(px, py, pc), device_id_type=MESH)
                cp.wait_send()
                cp.wait_recv()

    return pl.pallas_call(
        body, name=f"exch_wait_{l}", out_shape=[pltpu.HBM(t.shape, t.dtype) for t in lands],
        in_specs=[ANY] * nk + [HBM] * nk + [SEM, SEM, ANY], out_specs=[HBM] * nk,
        input_output_aliases={nk + k: k for k in range(nk)},
        compiler_params=pltpu.CompilerParams(has_side_effects=EFFECT))(*parts, *lands, *sems, after)


def _chip_half():
    x, y, c = _my_pos()
    return jnp.stack([2 * x + y, c]).astype(jnp.int32)


def _sum_half(part, land, tr):
    _, _, r2, cols = part.shape

    def body(pos_ref, p_ref, r_ref, o_ref):
        acc = p_ref[...].astype(F32)
        for r in range(7):
            acc = acc + r_ref[r].astype(F32)
        o_ref[...] = acc

    return pl.pallas_call(
        body,
        grid_spec=pltpu.PrefetchScalarGridSpec(
            num_scalar_prefetch=1, grid=(r2 // tr,),
            in_specs=[pl.BlockSpec((None, None, tr, cols), lambda i, pos: (pos[0], pos[1], i, 0)),
                      pl.BlockSpec((7, tr, cols), lambda i, pos: (0, i, 0))],
            out_specs=pl.BlockSpec((None, tr, cols), lambda i, pos: (pos[1], i, 0))),
        out_shape=_sds((2, r2, cols), F32), compiler_params=_cp("parallel"), name="sum_half")(_chip_half(), part, land)


def _share_full(fulls):
    nk = len(fulls)

    def body(*refs):
        ins, outs = refs[:nk], refs[nk:2 * nk]
        send, recv = refs[2 * nk:]
        x, y, c = _my_pos()

        def copy(k, hf):
            return pltpu.make_async_remote_copy(
                src_ref=ins[k].at[hf], dst_ref=outs[k].at[hf], send_sem=send.at[k], recv_sem=recv.at[k],
                device_id=(x, y, 1 - c), device_id_type=MESH)

        for k in range(nk):
            copy(k, c).start()
        for k in range(nk):
            copy(k, 1 - c).wait_recv()
        for k in range(nk):
            copy(k, c).wait_send()

    return pl.pallas_call(
        body, out_shape=[_sds(f.shape, F32) for f in fulls], in_specs=[ANY] * nk, out_specs=[ANY] * nk,
        scratch_shapes=[pltpu.SemaphoreType.DMA((nk,))] * 2, input_output_aliases={k: k for k in range(nk)},
        name="share_full")(*fulls)


def _small_start(packed):
    def body(in_ref, out_ref, send, recv, token):
        x, y, c = _my_pos()
        for r, (dx, dy, dc) in enumerate(_DEV_REL):
            pltpu.make_async_remote_copy(
                src_ref=in_ref, dst_ref=out_ref.at[r], send_sem=send.at[r], recv_sem=recv.at[r],
                device_id=(x ^ dx, y ^ dy, c ^ dc), device_id_type=MESH).start()
        token[...] = jnp.zeros_like(token)

    res = pl.pallas_call(
        body, name="small_start",
        out_shape=[pltpu.HBM((7,) + packed.shape, F32)] + [pltpu.SemaphoreType.DMA((7,))] * 2 + [_sds((8, 128), F32)],
        in_specs=[HBM], out_specs=[HBM, SEM, SEM, pl.BlockSpec(memory_space=pltpu.VMEM)],
        compiler_params=pltpu.CompilerParams(has_side_effects=EFFECT))(_in_hbm(packed))
    return res[0], (res[1], res[2]), res[3]


def _small_wait(packed, land, sems, after):
    def body(in_ref, land_ref, send, recv, after_ref, out_ref):
        x, y, c = _my_pos()
        for r, (dx, dy, dc) in enumerate(_DEV_REL):
            cp = pltpu.make_async_remote_copy(
                src_ref=in_ref, dst_ref=land_ref.at[r], send_sem=send.at[r], recv_sem=recv.at[r],
                device_id=(x ^ dx, y ^ dy, c ^ dc), device_id_type=MESH)
            cp.wait_send()
            cp.wait_recv()

    return pl.pallas_call(
        body, name="small_wait", out_shape=pltpu.HBM(land.shape, land.dtype),
        in_specs=[ANY, HBM, SEM, SEM, ANY], out_specs=HBM, input_output_aliases={1: 0},
        compiler_params=pltpu.CompilerParams(has_side_effects=EFFECT))(packed, land, *sems, after)


def _sum_small(packed, land, tr):
    rows = packed.shape[0]
    x, y, c = _my_pos()
    me = (4 * x + 2 * y + c).astype(jnp.int32).reshape(1)

    def sbody(me_ref, p_ref, r_ref, o_ref):
        me_dev = me_ref[0]
        own = p_ref[...]
        acc = None
        for s in range(NDEV):
            rel = s ^ me_dev
            v = jnp.where(rel == 0, own, r_ref[jnp.maximum(rel - 1, 0)])
            acc = v if acc is None else acc + v
        o_ref[...] = acc

    return pl.pallas_call(
        sbody,
        grid_spec=pltpu.PrefetchScalarGridSpec(
            num_scalar_prefetch=1, grid=(rows // tr,),
            in_specs=[pl.BlockSpec((tr, 128), lambda i, me_ref: (i, 0)), pl.BlockSpec((7, tr, 128), lambda i, me_ref: (0, i, 0))],
            out_specs=pl.BlockSpec((tr, 128), lambda i, me_ref: (i, 0))),
        out_shape=_sds((rows, 128), F32), compiler_params=_cp("parallel"), name="sum_small")(me, packed, land)


def _rms_fwd(x, g, tm):
    n = x.shape[0]

    def body(x_ref, g_ref, h_ref):
        xv = x_ref[...]
        r = lax.rsqrt(jnp.mean(xv * xv, axis=-1, keepdims=True) + EPS)
        h_ref[...] = (xv * r * g_ref[...]).astype(BF16)

    return pl.pallas_call(
        body, grid=(n // tm,),
        in_specs=[pl.BlockSpec((tm, D), lambda i: (i, 0)), pl.BlockSpec((1, D), lambda i: (0, 0))],
        out_specs=pl.BlockSpec((tm, D), lambda i: (i, 0)), out_shape=_sds((n, D), BF16),
        compiler_params=_cp("parallel"), name="rms_fwd")(x, g)


def _rms_fwd_t(x, g, tm=512):
    n = x.shape[0]

    def body(x_ref, g_ref, h_ref, ht_ref):
        xv = x_ref[...]
        r = lax.rsqrt(jnp.mean(xv * xv, axis=-1, keepdims=True) + EPS)
        h = xv * r * g_ref[...]
        h_ref[...] = h.astype(BF16)
        ht_ref[...] = h.T.astype(BF16)

    return pl.pallas_call(
        body, grid=(n // tm,),
        in_specs=[pl.BlockSpec((tm, D), lambda i: (i, 0)), pl.BlockSpec((1, D), lambda i: (0, 0))],
        out_specs=[pl.BlockSpec((tm, D), lambda i: (i, 0)), pl.BlockSpec((D, tm), lambda i: (0, i))],
        out_shape=[_sds((n, D), BF16), _sds((D, n), BF16)], compiler_params=_cp("parallel"), name="rms_fwd_t")(x, g)


def _rms_bwd(dh, x, g, dres, tm=512):
    n = x.shape[0]

    def body(dh_ref, x_ref, g_ref, dr_ref, dx_ref, dg_ref):
        i = pl.program_id(0)
        xv = x_ref[...]
        r = lax.rsqrt(jnp.mean(xv * xv, axis=-1, keepdims=True) + EPS)
        xh = xv * r
        dhv = dh_ref[...]
        dxh = dhv * g_ref[...]
        dx_ref[...] = dr_ref[...] + r * (dxh - xh * jnp.mean(dxh * xh, axis=-1, keepdims=True))
        part = jnp.sum(dhv * xh, axis=0, keepdims=True)

        @pl.when(i == 0)
        def _():
            dg_ref[...] = part

        @pl.when(i > 0)
        def _():
            dg_ref[...] += part

    row = pl.BlockSpec((tm, D), lambda i: (i, 0))
    vec = pl.BlockSpec((1, D), lambda i: (0, 0))
    return pl.pallas_call(
        body, grid=(n // tm,), in_specs=[row, row, vec, row], out_specs=[row, vec],
        out_shape=[_sds((n, D), F32), _sds((1, D), F32)], compiler_params=_cp("arbitrary"), name="rms_bwd")(dh, x, g, dres)


def _loss_head(x, tgt, g, tm=512):
    def body(x_ref, t_ref, g_ref, dx_ref, dg_ref, ls_ref):
        i = pl.program_id(0)
        xv = x_ref[...]
        r = lax.rsqrt(jnp.mean(xv * xv, axis=-1, keepdims=True) + EPS)
        xh = xv * r
        gv = g_ref[...]
        diff = xh * gv - t_ref[...]
        dy = diff * (1.0 / D)
        dxh = dy * gv
        dx_ref[...] = r * (dxh - xh * jnp.mean(dxh * xh, axis=-1, keepdims=True))
        part_g = jnp.sum(dy * xh, axis=0, keepdims=True)
        part_l = jnp.sum(diff * diff, axis=0, keepdims=True)

        @pl.when(i == 0)
        def _():
            dg_ref[...] = part_g
            ls_ref[...] = part_l

        @pl.when(i > 0)
        def _():
            dg_ref[...] += part_g
            ls_ref[...] += part_l

        @pl.when(i == pl.num_programs(0) - 1)
        def _():
            tot = jnp.sum(ls_ref[...], axis=-1, keepdims=True) * (0.5 / D)
            ls_ref[...] = jnp.broadcast_to(tot, (1, D))

    row = pl.BlockSpec((tm, D), lambda i: (i, 0))
    vec = pl.BlockSpec((1, D), lambda i: (0, 0))
    return pl.pallas_call(
        body, grid=(S // tm,), in_specs=[row, row, vec], out_specs=[row, vec, vec],
        out_shape=[_sds((S, D), F32), _sds((1, D), F32), _sds((1, D), F32)],
        compiler_params=_cp("arbitrary"), name="loss_head")(x, tgt, g)


def _adamw(w, g, m, v):
    shape = w.shape
    cols = shape[-1] if w.ndim > 1 else shape[0]
    rows = w.size // cols
    w2, g2, m2, v2 = (t.reshape(rows, cols) for t in (w, g, m, v))
    tr = rows
    while tr * cols * 4 > (1 << 20) and tr % 16 == 0:
        tr //= 2
    c1 = 1.0 - ADAM_B1 ** ADAM_STEP
    c2 = 1.0 - ADAM_B2 ** ADAM_STEP

    def body(w_ref, g_ref, m_ref, v_ref, d_ref, nm_ref, nv_ref):
        gv = g_ref[...]
        mn = ADAM_B1 * m_ref[...] + (1.0 - ADAM_B1) * gv
        vn = ADAM_B2 * v_ref[...] + (1.0 - ADAM_B2) * (gv * gv)
        d_ref[...] = -ADAM_LR * ((mn / c1) / (jnp.sqrt(vn / c2) + ADAM_EPS) + ADAM_WD * w_ref[...])
        nm_ref[...] = mn
        nv_ref[...] = vn

    blk = pl.BlockSpec((tr, cols), lambda i: (i, 0))
    outs = pl.pallas_call(
        body, grid=(rows // tr,), in_specs=[blk] * 4, out_specs=[blk] * 3,
        out_shape=[_sds((rows, cols), F32)] * 3, compiler_params=_cp("parallel"), name="adamw")(w2, g2, m2, v2)
    return tuple(o.reshape(shape) for o in outs)


def _adamw_layer(l, w, g, m, v, outs, tr):
    cols = w.shape[-1]
    rows = w.size // (NL * cols)
    nb = rows // tr
    w2, m2, v2 = (t.reshape(NL * rows, cols) for t in (w, m, v))
    g2 = g.reshape(rows, cols)
    c1 = 1.0 - ADAM_B1 ** ADAM_STEP
    c2 = 1.0 - ADAM_B2 ** ADAM_STEP

    def body(w_ref, g_ref, m_ref, v_ref, d_in, nm_in, nv_in, go_in, d_ref, nm_ref, nv_ref, go_ref):
        gv = g_ref[...]
        mn = ADAM_B1 * m_ref[...] + (1.0 - ADAM_B1) * gv
        vn = ADAM_B2 * v_ref[...] + (1.0 - ADAM_B2) * (gv * gv)
        d_ref[...] = -ADAM_LR * ((mn / c1) / (jnp.sqrt(vn / c2) + ADAM_EPS) + ADAM_WD * w_ref[...])
        nm_ref[...] = mn
        nv_ref[...] = vn
        go_ref[...] = gv

    lay = pl.BlockSpec((tr, cols), lambda i: (l * nb + i, 0))
    return pl.pallas_call(
        body, grid=(nb,), in_specs=[lay, pl.BlockSpec((tr, cols), lambda i: (i, 0)), lay, lay] + [ANY] * 4,
        out_specs=[lay] * 4, out_shape=[_sds((NL * rows, cols), F32)] * 4,
        input_output_aliases={4: 0, 5: 1, 6: 2, 7: 3}, compiler_params=_cp("parallel"), name="adamw_layer")(w2, g2, m2, v2, *outs)


def _proj_fwd(h, wg, tm=512):
    def body(h_ref, w_ref, o_ref):
        o_ref[...] = _dot(h_ref[...], w_ref[...]).astype(BF16)

    return pl.pallas_call(
        body, grid=(NCHIP, S // tm),
        in_specs=[pl.BlockSpec((tm, D), lambda c, i: (i, 0)), pl.BlockSpec((None, D, CW), lambda c, i: (c, 0, 0))],
        out_specs=pl.BlockSpec((tm, CW), lambda c, i: (i, c)), out_shape=_sds((S, DIN), BF16),
        compiler_params=_cp("parallel", "parallel"), name="proj_fwd")(h, wg)


def _proj_bwd_x(dproj, wg, dep, tm=1024):
    def body(d_ref, w_ref, dep_ref, o_ref):
        k = pl.program_id(1)
        part = _dot_nt(d_ref[...], w_ref[...])

        @pl.when(k == 0)
        def _():
            o_ref[...] = part

        @pl.when(k > 0)
        def _():
            o_ref[...] += part

    return pl.pallas_call(
        body, grid=(S // tm, NCHIP),
        in_specs=[pl.BlockSpec((tm, CW), lambda i, k: (i, k)), pl.BlockSpec((None, D, CW), lambda i, k: (k, 0, 0)), ANY],
        out_specs=pl.BlockSpec((tm, D), lambda i, k: (i, 0)), out_shape=_sds((S, D), F32),
        compiler_params=_cp("parallel", "arbitrary"), name="proj_bwd_x")(dproj, wg, dep)


def _proj_bwd_w(ht, dproj):
    def body(h_ref, d_ref, o_ref):
        o_ref[...] = _dot(h_ref[...], d_ref[...]).astype(BF16)

    return pl.pallas_call(
        body, grid=(NCHIP, NJ),
        in_specs=[pl.BlockSpec((D, S), lambda c, j: (0, 0)), pl.BlockSpec((S, TN_IN), lambda c, j: (0, c * NJ + j))],
        out_specs=pl.BlockSpec((None, D, TN_IN), lambda c, j: (c, 0, j)), out_shape=_sds((NCHIP, D, CW), BF16),
        compiler_params=_cp("parallel", "parallel"), name="proj_bwd_w")(ht, dproj)


def _merge_fwd(y_all, wbr, proj, tm=256):
    cb = D // NCHIP

    def body(y_ref, w_ref, *rest):
        g_refs, z_ref = rest[:8], rest[8]
        for c in range(NCHIP):
            acc = None
            for b in range(4):
                g = g_refs[2 * b + c // 2][:, (c % 2) * cb:(c % 2 + 1) * cb].astype(F32)
                t = _dot(y_ref[b], w_ref[c, b]) * _sigmoid(g)
                acc = t if acc is None else acc + t
            z_ref[:, c * cb:(c + 1) * cb] = acc.astype(BF16)

    g_specs = [pl.BlockSpec((tm, W), functools.partial(lambda j, i: (i, P_GM + j), j)) for j in range(8)]
    return pl.pallas_call(
        body, grid=(S // tm,),
        in_specs=[pl.BlockSpec((4, tm, W), lambda i: (0, i, 0)), pl.BlockSpec((NCHIP, 4, W, cb), lambda i: (0, 0, 0, 0))] + g_specs,
        out_specs=pl.BlockSpec((tm, D), lambda i: (i, 0)), out_shape=_sds((S, D), BF16),
        compiler_params=_cp("parallel"), name="merge_fwd")(y_all, wbr, *([proj] * 8))


def _merge_bwd(dz, y_all, wbr, proj, dproj, tm=512):
    cb = D // NCHIP
    ni = S // tm

    def body(dz_ref, y_ref, w_ref, ga_ref, gb_ref, dp_in, dp_ref, dy_ref, dw_ref, acc, obuf, osem):
        b = pl.program_id(0)
        i = pl.program_id(1)

        @pl.when(i == 0)
        def _():
            acc[...] = jnp.zeros_like(acc)

        yv = y_ref[...]
        dys = []

        def fill(slot):
            ws_ = [w_ref[c] for c in range(NCHIP)]
            t = [_dot(yv, wv) for wv in ws_]
            dts = []
            for c in range(NCHIP):
                g_ref = ga_ref if c < 2 else gb_ref
                g = _sigmoid(g_ref[:, (c % 2) * cb:(c % 2 + 1) * cb].astype(F32))
                dzc = dz_ref[:, c * cb:(c + 1) * cb].astype(F32)
                slot[:, c * cb:(c + 1) * cb] = (dzc * t[c] * g * (1.0 - g)).astype(BF16)
                dts.append((dzc * g).astype(BF16))
            dy = None
            for c in range(NCHIP):
                part = _dot_nt(dts[c], ws_[c])
                dy = part if dy is None else dy + part
            for c in range(NCHIP):
                acc[c] += _dot_tn(yv, dts[c])
            dys.append(dy)

        _tile_put(obuf, osem, lambda st: dp_ref.at[pl.ds((st % ni) * tm, tm), pl.ds(P_GM * W + (st // ni) * D, D)],
                  b * ni + i, 4 * ni, fill)
        dy_ref[...] = dys[0].astype(BF16)

        @pl.when(i == ni - 1)
        def _():
            dw_ref[...] = acc[...].astype(BF16)

    return pl.pallas_call(
        body, grid=(4, ni),
        in_specs=[pl.BlockSpec((tm, D), lambda b, i: (i, 0)), pl.BlockSpec((None, tm, W), lambda b, i: (b, i, 0)),
                  pl.BlockSpec((NCHIP, None, W, cb), lambda b, i: (0, b, 0, 0)),
                  pl.BlockSpec((tm, W), lambda b, i: (i, P_GM + 2 * b)), pl.BlockSpec((tm, W), lambda b, i: (i, P_GM + 2 * b + 1)), ANY],
        out_specs=[ANY, pl.BlockSpec((None, tm, W), lambda b, i: (b, i, 0)), pl.BlockSpec((NCHIP, None, W, cb), lambda b, i: (0, b, 0, 0))],
        out_shape=[_sds((S, DIN), BF16), _sds((4, S, W), BF16), _sds((NCHIP, 4, W, cb), BF16)],
        scratch_shapes=[pltpu.VMEM((NCHIP, W, cb), F32), pltpu.VMEM((2, tm, D), BF16), pltpu.SemaphoreType.DMA((2,))],
        input_output_aliases={5: 0}, compiler_params=_cp("arbitrary", "arbitrary"), name="merge_bwd")(dz, y_all, wbr, proj, proj, dproj)


def _out_fwd(z, wo, x, tm=512):
    def body(z_ref, w_ref, x_ref, o_ref):
        o_ref[...] = x_ref[...] + _dot(z_ref[...], w_ref[...])

    row = pl.BlockSpec((tm, D), lambda i: (i, 0))
    return pl.pallas_call(
        body, grid=(S // tm,), in_specs=[row, pl.BlockSpec((D, D), lambda i: (0, 0)), row], out_specs=row,
        out_shape=_sds((S, D), F32), compiler_params=_cp("parallel"), name="out_fwd")(z, wo, x)


def _out_bwd(dx, z, wo, tm=512):
    ni = S // tm

    def body(dx_ref, z_ref, w_ref, dz_ref, dw_ref, acc):
        i = pl.program_id(0)
        dxb = dx_ref[...].astype(BF16)
        dz_ref[...] = _dot_nt(dxb, w_ref[...]).astype(BF16)
        part = _dot_tn(z_ref[...], dxb)

        @pl.when(i == 0)
        def _():
            acc[...] = part

        @pl.when(i > 0)
        def _():
            acc[...] += part

        @pl.when(i == ni - 1)
        def _():
            dw_ref[...] = acc[...].astype(BF16)

    row = pl.BlockSpec((tm, D), lambda i: (i, 0))
    full = pl.BlockSpec((D, D), lambda i: (0, 0))
    return pl.pallas_call(
        body, grid=(ni,), in_specs=[row, row, full], out_specs=[row, full],
        out_shape=[_sds((S, D), BF16), _sds((D, D), BF16)], scratch_shapes=[pltpu.VMEM((D, D), F32)],
        compiler_params=_cp("arbitrary"), name="out_bwd")(dx, z, wo)


def _gelu_parts(a):
    cdf = 0.5 * (1.0 + lax.erf(a * INV_SQRT2))
    return a * cdf, cdf


def _ln_parts(v):
    mu = jnp.mean(v, axis=-1, keepdims=True)
    vc = v - mu
    rs = lax.rsqrt(jnp.mean(vc * vc, axis=-1, keepdims=True) + EPS)
    return vc * rs, rs


def _causal_mask():
    return lax.broadcasted_iota(jnp.int32, (HD, HD), 0) >= lax.broadcasted_iota(jnp.int32, (HD, HD), 1)


def _gmlp_fwd(proj, lg, lb, ws, bias, y_all, tm=512):
    def body(uv_ref, gt_ref, lg_ref, lb_ref, ws_ref, b_ref, y_in, y_ref):
        act, _ = _gelu_parts(uv_ref[...].astype(F32))
        u = act[:, :W]
        xh, _ = _ln_parts(act[:, W:])
        vn = (xh * lg_ref[...] + lb_ref[...]).astype(BF16)
        gt = gt_ref[...].astype(F32)
        us = u * (gt * _sigmoid(gt))
        mask = _causal_mask()
        for h in range(4):
            wm = jnp.where(mask, ws_ref[h], 0.0).astype(BF16)
            cs = slice(h * HD, (h + 1) * HD)
            for c in range(tm // HD):
                rs_ = slice(c * HD, (c + 1) * HD)
                mixed = _dot(wm, vn[rs_, cs]) + b_ref[h]
                y_ref[rs_, cs] = (us[rs_, cs] * mixed).astype(BF16)

    vec = pl.BlockSpec((1, W), lambda i: (0, 0))
    mats = pl.BlockSpec((4, HD, HD), lambda i: (0, 0, 0))
    return pl.pallas_call(
        body, grid=(S // tm,),
        in_specs=[pl.BlockSpec((tm, 2 * W), lambda i: (i, 0)), pl.BlockSpec((tm, W), lambda i: (i, P_AGATE)), vec, vec, mats, mats, ANY],
        out_specs=pl.BlockSpec((None, tm, W), lambda i: (0, i, 0)), out_shape=_sds((4, S, W), BF16),
        input_output_aliases={6: 0}, compiler_params=_cp("parallel"), name="gmlp_fwd")(proj, proj, lg, lb, ws, bias, y_all)


def _gmlp_bwd(proj, dy_all, lg, lb, ws, bias, dproj, tm=256):
    ni = S // tm

    def body(uv_ref, gt_ref, dy_ref, lg_ref, lb_ref, ws_ref, b_ref, dp_in, dp_ref, dws_ref, dbs_ref, dlg_ref, dlb_ref, mix_s, dvn_s):
        i = pl.program_id(0)

        @pl.when(i == 0)
        def _():
            dws_ref[...] = jnp.zeros_like(dws_ref)
            dbs_ref[...] = jnp.zeros_like(dbs_ref)
            dlg_ref[...] = jnp.zeros_like(dlg_ref)
            dlb_ref[...] = jnp.zeros_like(dlb_ref)

        a0 = uv_ref[...].astype(F32)
        act, cdf = _gelu_parts(a0)
        u = act[:, :W]
        xh, rs = _ln_parts(act[:, W:])
        lgv = lg_ref[...]
        vn = (xh * lgv + lb_ref[...]).astype(BF16)
        mask = _causal_mask()
        wms = [jnp.where(mask, ws_ref[h], 0.0).astype(BF16) for h in range(4)]
        blocks = [(slice(c * HD, (c + 1) * HD), slice(h * HD, (h + 1) * HD), h) for h in range(4) for c in range(tm // HD)]
        for rs_, cs, h in blocks:
            mix_s[rs_, cs] = _dot(wms[h], vn[rs_, cs]) + b_ref[h]
        mixed = mix_s[...]
        gt = gt_ref[...].astype(F32)
        sg = _sigmoid(gt)
        sl = gt * sg
        dyv = dy_ref[...].astype(F32)
        dum = dyv * sl
        dgate = dyv * (u * mixed) * (sg * (1.0 + gt * (1.0 - sg)))
        du = dum * mixed
        dmix = dum * u
        dmb = dmix.astype(BF16)
        for rs_, cs, h in blocks:
            dvn_s[rs_, cs] = _dot_tn(wms[h], dmb[rs_, cs])
        for rs_, cs, h in blocks:
            dws_ref[h] += _dot_nt(dmb[rs_, cs], vn[rs_, cs])
            dbs_ref[h] += dmix[rs_, cs]
        dvn = dvn_s[...]
        dlg_ref[...] += jnp.sum(dvn * xh, axis=0, keepdims=True)
        dlb_ref[...] += jnp.sum(dvn, axis=0, keepdims=True)
        dxh = dvn * lgv
        dv = rs * (dxh - jnp.mean(dxh, axis=-1, keepdims=True) - xh * jnp.mean(dxh * xh, axis=-1, keepdims=True))
        gp = cdf + a0 * (jnp.exp(-0.5 * a0 * a0) * INV_SQRT2PI)
        dp_ref[:, :W] = (du * gp[:, :W]).astype(BF16)
        dp_ref[:, W:2 * W] = (dv * gp[:, W:]).astype(BF16)
        dp_ref[:, 2 * W:] = dgate.astype(BF16)

        @pl.when(i == ni - 1)
        def _():
            for h in range(4):
                dws_ref[h] = jnp.where(mask, dws_ref[h], 0.0)
                dbs_ref[h] = jnp.broadcast_to(jnp.sum(dbs_ref[h], axis=1, keepdims=True), (HD, HD))

    vec = pl.BlockSpec((1, W), lambda i: (0, 0))
    mats = pl.BlockSpec((4, HD, HD), lambda i: (0, 0, 0))
    return pl.pallas_call(
        body, grid=(ni,),
        in_specs=[pl.BlockSpec((tm, 2 * W), lambda i: (i, 0)), pl.BlockSpec((tm, W), lambda i: (i, P_AGATE)),
                  pl.BlockSpec((None, tm, W), lambda i: (0, i, 0)), vec, vec, mats, mats, ANY],
        out_specs=[pl.BlockSpec((tm, 3 * W), lambda i: (i, 0)), mats, mats, vec, vec],
        out_shape=[_sds((S, DIN), BF16), _sds((4, HD, HD), F32), _sds((4, HD, HD), F32), _sds((1, W), F32), _sds((1, W), F32)],
        scratch_shapes=[pltpu.VMEM((tm, W), F32), pltpu.VMEM((tm, W), F32)],
        input_output_aliases={7: 0}, compiler_params=_cp("arbitrary"), name="gmlp_bwd")(proj, proj, dy_all, lg, lb, ws, bias, dproj)


def _pool_diff(p, halo, row0, tm):
    xx = jnp.concatenate([halo, p], axis=0)
    t1 = (row0 + 1 + lax.broadcasted_iota(jnp.int32, (tm, 1), 0)).astype(F32)
    out = []
    for g, win in enumerate(POOL_WINDOWS):
        s = xx[:, g * HD:(g + 1) * HD]
        sh = 1
        while sh < win:
            s = s + pltpu.roll(s, sh, 0)
            sh *= 2
        out.append(s[HALO:] / jnp.minimum(t1, float(win)) - p[:, g * HD:(g + 1) * HD])
    return out


def _pool_fwd(proj, pw, sc, y_all, tm=512):
    rb = tm // HALO

    def body(p_ref, h_ref, gt_ref, pw_ref, sc_ref, y_in, y_ref):
        i = pl.program_id(0)
        halo = jnp.where(i > 0, h_ref[...].astype(F32), 0.0)
        ds = _pool_diff(p_ref[...].astype(F32), halo, i * tm, tm)
        gt = gt_ref[...].astype(F32)
        sl = gt * _sigmoid(gt)
        for g in range(4):
            cs = slice(g * HD, (g + 1) * HD)
            lin = _dot(ds[g].astype(BF16), pw_ref[g].astype(BF16))
            y_ref[:, cs] = (lin * sc_ref[:, cs] * sl[:, cs]).astype(BF16)

    return pl.pallas_call(
        body, grid=(S // tm,),
        in_specs=[pl.BlockSpec((tm, W), lambda i: (i, P_PIN)),
                  pl.BlockSpec((HALO, W), lambda i: (jnp.maximum(i * rb - 1, 0), P_PIN)),
                  pl.BlockSpec((tm, W), lambda i: (i, P_PGATE)),
                  pl.BlockSpec((4, HD, HD), lambda i: (0, 0, 0)), pl.BlockSpec((1, W), lambda i: (0, 0)), ANY],
        out_specs=pl.BlockSpec((None, tm, W), lambda i: (1, i, 0)), out_shape=_sds((4, S, W), BF16),
        input_output_aliases={5: 0}, compiler_params=_cp("parallel"), name="pool_fwd")(proj, proj, proj, pw, sc, y_all)


def _pool_bwd(proj, dy_all, pw, sc, dproj, tm=256):
    ni = S // tm
    rb = tm // HALO
    last_rb = S // HALO - 1
    rx = tm + HALO

    def body(p_ref, h_ref, gt_ref, gh_ref, dy_ref, dyh_ref, pw_ref, sc_ref, dp_in, dp_ref, dpw_ref, dsc_ref, obuf, osem):
        i = pl.program_id(0)

        @pl.when(i == 0)
        def _():
            dpw_ref[...] = jnp.zeros_like(dpw_ref)
            dsc_ref[...] = jnp.zeros_like(dsc_ref)

        halo = jnp.where(i > 0, h_ref[...].astype(F32), 0.0)
        ds = _pool_diff(p_ref[...].astype(F32), halo, i * tm, tm)
        nxt = i < ni - 1
        gx = jnp.concatenate([gt_ref[...], gh_ref[...]], axis=0).astype(F32)
        dyx = jnp.concatenate([dy_ref[...].astype(F32), jnp.where(nxt, dyh_ref[...].astype(F32), 0.0)], axis=0)
        sgx = _sigmoid(gx)
        slx = gx * sgx
        scv = sc_ref[...]
        dlinx = dyx * slx * scv
        t1 = (i * tm + 1 + lax.broadcasted_iota(jnp.int32, (rx, 1), 0)).astype(F32)
        gt, sg, sl, dyv = gx[:tm], sgx[:tm], slx[:tm], dyx[:tm]
        dsl = sg * (1.0 + gt * (1.0 - sg))

        def fill(slot):
            for g, win in enumerate(POOL_WINDOWS):
                cs = slice(g * HD, (g + 1) * HD)
                wv = pw_ref[g].astype(BF16)
                dlb = dlinx[:, cs].astype(BF16)
                ddx = _dot_nt(dlb, wv)
                f = ddx / jnp.minimum(t1, float(win))
                sh = 1
                while sh < win:
                    f = f + pltpu.roll(f, rx - sh, 0)
                    sh *= 2
                slot[:, cs] = (f[:tm] - ddx[:tm]).astype(BF16)
                db = ds[g].astype(BF16)
                lin = _dot(db, wv)
                slot[:, W + g * HD:W + (g + 1) * HD] = (dyv[:, cs] * lin * scv[:, cs] * dsl[:, cs]).astype(BF16)
                dsc_ref[:, cs] += jnp.sum(dyv[:, cs] * sl[:, cs] * lin, axis=0, keepdims=True)
                dpw_ref[g] += _dot_tn(db, dlb[:tm])

        _tile_put(obuf, osem, lambda st: dp_ref.at[pl.ds(st * tm, tm), pl.ds(P_PIN * W, 2 * W)], i, ni, fill)

    mats = pl.BlockSpec((4, HD, HD), lambda i: (0, 0, 0))
    vec = pl.BlockSpec((1, W), lambda i: (0, 0))
    return pl.pallas_call(
        body, grid=(ni,),
        in_specs=[pl.BlockSpec((tm, W), lambda i: (i, P_PIN)),
                  pl.BlockSpec((HALO, W), lambda i: (jnp.maximum(i * rb - 1, 0), P_PIN)),
                  pl.BlockSpec((tm, W), lambda i: (i, P_PGATE)),
                  pl.BlockSpec((HALO, W), lambda i: (jnp.minimum((i + 1) * rb, last_rb), P_PGATE)),
                  pl.BlockSpec((None, tm, W), lambda i: (1, i, 0)),
                  pl.BlockSpec((None, HALO, W), lambda i: (1, jnp.minimum((i + 1) * rb, last_rb), 0)),
                  mats, vec, ANY],
        out_specs=[ANY, mats, vec],
        out_shape=[_sds((S, DIN), BF16), _sds((4, HD, HD), F32), _sds((1, W), F32)],
        scratch_shapes=[pltpu.VMEM((2, tm, 2 * W), BF16), pltpu.SemaphoreType.DMA((2,))],
        input_output_aliases={8: 0}, compiler_params=_cp("arbitrary"), name="pool_bwd")(proj, proj, proj, proj, dy_all, dy_all, pw, sc, dproj)


ATT_STEP = ((1, 4), (4, 1), (4, 1))
ATT_GROUP = 16
ATT_GROUP_BWD = 8


def _att_band():
    qi = lax.broadcasted_iota(jnp.int32, (HD, 2 * HD), 0)
    kj = lax.broadcasted_iota(jnp.int32, (HD, 2 * HD), 1)
    return jnp.logical_and(kj >= qi, kj <= qi + HD), kj < HD


def _att_keys(kp_ref, ko_ref, vp_ref, vo_ref, a, jj):
    if jj == 0:
        return (jnp.concatenate([kp_ref[a], ko_ref[a, :HD, :]], axis=0), jnp.concatenate([vp_ref[a], vo_ref[a, :HD, :]], axis=0))
    return ko_ref[a, (jj - 1) * HD:(jj + 1) * HD, :], vo_ref[a, (jj - 1) * HD:(jj + 1) * HD, :]


def _dilate(src, dst, d, rows, cast=None):
    for r in range(d):
        for h in range(4):
            v = src.at[h][pl.ds(r, rows // d, stride=d), :] if d > 1 else src[h]
            dst[r * 4 + h] = v if cast is None else v.astype(cast)


def _undilate(src, dst, d, rows):
    for r in range(d):
        for h in range(4):
            if d > 1:
                dst.at[h][pl.ds(r, rows // d, stride=d), :] = src[r * 4 + h].astype(F32)
            else:
                dst[h] = src[h].astype(F32)


def _dil_spec(d, tm):
    return pl.BlockSpec((4 * d, tm // d, HD), lambda i: (0, i, 0))


def _att_prep(proj, tm=512):
    def body(q0, q1, q2, k_ref, v_ref, *rest):
        outs, scr = rest[:9], rest[9]
        for j, (src, dsts) in enumerate(((q0, ((0, outs[0]),)), (q1, ((1, outs[1]),)), (q2, ((2, outs[2]),)),
                                         (k_ref, tuple((g, outs[3 + g]) for g in range(3))),
                                         (v_ref, tuple((g, outs[6 + g]) for g in range(3))))):
            for h in range(4):
                scr[j, h] = src[:, h * HD:(h + 1) * HD].astype(F32)
            for g, dst in dsts:
                _dilate(scr.at[j], dst, DILATIONS[g], tm, BF16)

    def piece(p):
        return pl.BlockSpec((tm, W), lambda i: (i, p))

    shapes = [_sds((4 * d, S // d, HD), BF16) for d in DILATIONS]
    res = pl.pallas_call(
        body, grid=(S // tm,),
        in_specs=[piece(P_CQ), piece(P_CQ + 1), piece(P_CQ + 2), piece(P_CK), piece(P_CV)],
        out_specs=[_dil_spec(d, tm) for d in DILATIONS] * 3, out_shape=shapes * 3,
        scratch_shapes=[pltpu.VMEM((5, 4, tm, HD), F32)],
        compiler_params=_cp("parallel"), name="att_prep")(proj, proj, proj, proj, proj)
    return res[0:3], res[3:6], res[6:9]


def _att_specs(g):
    d = DILATIONS[g]
    nres, njb = ATT_STEP[g]
    nb = S // d // HD
    own = pl.BlockSpec((4 * nres, njb * HD, HD), lambda r, j: (r, j, 0))
    prev = pl.BlockSpec((4 * nres, HD, HD), lambda r, j: (r, jnp.maximum(j * njb - 1, 0), 0))
    nxt = pl.BlockSpec((4 * nres, HD, HD), lambda r, j: (r, jnp.minimum((j + 1) * njb, nb - 1), 0))
    return (d // nres, nb // njb), own, prev, nxt


def _att_fwd(q, k, v, g):
    d = DILATIONS[g]
    nres, njb = ATT_STEP[g]
    grid, own, prev, _ = _att_specs(g)

    def body(q_ref, kp_ref, ko_ref, vp_ref, vo_ref, o_ref, l_ref):
        jb = pl.program_id(1)
        band, is_prev = _att_band()
        no_prev = jnp.where(is_prev, jnp.where(jb > 0, 0.0, NEG), 0.0)
        blocks = [(a, jj) for jj in range(njb) for a in range(4 * nres)]
        for g0 in range(0, len(blocks), ATT_GROUP):
            grp = blocks[g0:g0 + ATT_GROUP]
            s, v2 = [], []
            for a, jj in grp:
                k2_, v2_ = _att_keys(kp_ref, ko_ref, vp_ref, vo_ref, a, jj)
                s_ = jnp.where(band, _dot_nt(q_ref[a, jj * HD:(jj + 1) * HD, :], k2_) * SCALE, NEG)
                s.append(s_ + no_prev if jj == 0 else s_)
                v2.append(v2_)
            m = [jnp.max(s_, axis=-1, keepdims=True) for s_ in s]
            e = [jnp.exp(s_ - m_) for s_, m_ in zip(s, m)]
            den = [jnp.sum(e_, axis=-1, keepdims=True) for e_ in e]
            inv = [1.0 / d_ for d_ in den]
            for i, (a, jj) in enumerate(grp):
                rs_ = slice(jj * HD, (jj + 1) * HD)
                o_ref[a, rs_, :] = _dot((e[i] * inv[i]).astype(BF16), v2[i]).astype(BF16)
                l_ref[a, rs_, :] = jnp.broadcast_to(m[i] + jnp.log(den[i]), (HD, HD))

    return pl.pallas_call(
        body, grid=grid, in_specs=[own, prev, own, prev, own], out_specs=[own, own],
        out_shape=[_sds((4 * d, S // d, HD), BF16), _sds((4 * d, S // d, HD), F32)],
        compiler_params=_cp("parallel", "parallel"), name="att_fwd")(q, k, k, v, v)


def _att_mix(os_, ls_, proj, y_all, tm=512):
    def body(o0, o1, o2, l0, l1, l2, gt_ref, y_in, y_ref, om_ref, lt_ref, so1, so2, sl1, sl2):
        _undilate(o1, so1, DILATIONS[1], tm)
        _undilate(o2, so2, DILATIONS[2], tm)
        _undilate(l1, sl1, DILATIONS[1], tm)
        _undilate(l2, sl2, DILATIONS[2], tm)
        for h in range(4):
            a, b, c = l0[h], sl1[h], sl2[h]
            m = jnp.maximum(jnp.maximum(a, b), c)
            ea, eb, ec = jnp.exp(a - m), jnp.exp(b - m), jnp.exp(c - m)
            z = ea + eb + ec
            inv = 1.0 / z
            o = (ea * inv) * o0[h] + (eb * inv) * so1[h] + (ec * inv) * so2[h]
            gt = gt_ref[:, h * HD:(h + 1) * HD].astype(F32)
            om_ref[h] = o
            lt_ref[h] = m + jnp.log(z)
            y_ref[:, h * HD:(h + 1) * HD] = (o * (gt * _sigmoid(gt))).astype(BF16)

    dil = [_dil_spec(d, tm) for d in DILATIONS]
    return pl.pallas_call(
        body, grid=(S // tm,),
        in_specs=dil * 2 + [pl.BlockSpec((tm, W), lambda i: (i, P_CGATE)), ANY],
        out_specs=[pl.BlockSpec((None, tm, W), lambda i: (2, i, 0)), dil[0], dil[0]],
        out_shape=[_sds((4, S, W), BF16), _sds((4, S, HD), F32), _sds((4, S, HD), F32)],
        scratch_shapes=[pltpu.VMEM((4, tm, HD), F32)] * 4,
        input_output_aliases={7: 0}, compiler_params=_cp("parallel"), name="att_mix")(*os_, *ls_, proj, y_all)


def _att_bwd_pre(dy_all, proj, om, lse, dproj, tm=512):
    def body(dy_ref, gt_ref, om_ref, ls_ref, dp_in, *rest):
        dos, dls, lss, dp_ref, sdo, sdl = rest[0:3], rest[3:6], rest[6:8], rest[8], rest[9], rest[10]
        for h in range(4):
            cs = slice(h * HD, (h + 1) * HD)
            gt = gt_ref[:, cs].astype(F32)
            sg = _sigmoid(gt)
            dyv = dy_ref[:, cs].astype(F32)
            o = om_ref[h]
            do = dyv * (gt * sg)
            dp_ref[:, cs] = (dyv * o * (sg * (1.0 + gt * (1.0 - sg)))).astype(BF16)
            sdo[h] = do
            sdl[h] = jnp.broadcast_to(jnp.sum(do * o, axis=-1, keepdims=True), (tm, HD))
        for g, d in enumerate(DILATIONS):
            _dilate(sdo, dos[g], d, tm, BF16)
            _dilate(sdl, dls[g], d, tm)
            if g > 0:
                _dilate(ls_ref, lss[g - 1], d, tm)

    dil = [_dil_spec(d, tm) for d in DILATIONS]
    gcol = pl.BlockSpec((tm, W), lambda i: (i, P_CGATE))
    res = pl.pallas_call(
        body, grid=(S // tm,),
        in_specs=[pl.BlockSpec((None, tm, W), lambda i: (2, i, 0)), gcol, dil[0], dil[0], ANY],
        out_specs=dil + dil + dil[1:] + [gcol],
        out_shape=([_sds((4 * d, S // d, HD), BF16) for d in DILATIONS] + [_sds((4 * d, S // d, HD), F32) for d in DILATIONS]
                   + [_sds((4 * d, S // d, HD), F32) for d in DILATIONS[1:]] + [_sds((S, DIN), BF16)]),
        scratch_shapes=[pltpu.VMEM((4, tm, HD), F32)] * 2,
        input_output_aliases={4: 8}, compiler_params=_cp("parallel"), name="att_bwd_pre")(dy_all, proj, om, lse, dproj)
    return res[0:3], res[3:6], [lse] + list(res[6:8]), res[8]


def _att_bwd(q, k, v, do, lse, delta, g):
    d = DILATIONS[g]
    nres, njb = ATT_STEP[g]
    grid, own, prev, nxt = _att_specs(g)

    def body(qa_ref, qn_ref, kp_ref, ko_ref, vp_ref, vo_ref, doa_ref, don_ref, la_ref, ln_ref, da_ref, dn_ref,
             dq_ref, dk_ref, dv_ref):
        jb = pl.program_id(1)
        band, is_prev = _att_band()
        m_next = lax.broadcasted_iota(jnp.int32, (HD, HD), 1) >= lax.broadcasted_iota(jnp.int32, (HD, HD), 0)
        has_prev = jnp.where(is_prev, jnp.where(jb > 0, 1.0, 0.0), 1.0)
        has_next = jnp.where(jb < grid[1] - 1, 1.0, 0.0)

        def wide(t):
            return jnp.concatenate([t, t], axis=1)

        blocks = [(a, jj) for jj in range(njb) for a in range(4 * nres)]
        for g0 in range(0, len(blocks), ATT_GROUP_BWD):
            grp = blocks[g0:g0 + ATT_GROUP_BWD]
            ops = []
            for a, jj in grp:
                rs_ = slice(jj * HD, (jj + 1) * HD)
                k2, v2 = _att_keys(kp_ref, ko_ref, vp_ref, vo_ref, a, jj)
                if jj == njb - 1:
                    qn, don, lsn, dln, fn = qn_ref[a], don_ref[a], ln_ref[a], dn_ref[a], has_next
                else:
                    ns = slice((jj + 1) * HD, (jj + 2) * HD)
                    qn, don, lsn, dln, fn = qa_ref[a, ns, :], doa_ref[a, ns, :], la_ref[a, ns, :], da_ref[a, ns, :], None
                ops.append(dict(qa=qa_ref[a, rs_, :], doa=doa_ref[a, rs_, :], lsa=wide(la_ref[a, rs_, :]),
                                dla=wide(da_ref[a, rs_, :]), k2=k2, v2=v2, ko=ko_ref[a, rs_, :], vo=vo_ref[a, rs_, :],
                                qn=qn, don=don, lsn=lsn, dln=dln, fn=fn, first=jj == 0))
            sa = [_dot_nt(o["qa"], o["k2"]) for o in ops]
            dpa = [_dot_nt(o["doa"], o["v2"]) for o in ops]
            sn = [_dot_nt(o["qn"], o["ko"]) for o in ops]
            dpn = [_dot_nt(o["don"], o["vo"]) for o in ops]
            pa, pn = [], []
            for o, sa_, sn_ in zip(ops, sa, sn):
                p_ = jnp.where(band, jnp.exp(sa_ * SCALE - o["lsa"]), 0.0)
                pa.append(p_ * has_prev if o["first"] else p_)
                p_ = jnp.where(m_next, jnp.exp(sn_ * SCALE - o["lsn"]), 0.0)
                pn.append(p_ if o["fn"] is None else p_ * o["fn"])
            dsa = [(p_ * (dp_ - o["dla"]) * SCALE).astype(BF16) for p_, dp_, o in zip(pa, dpa, ops)]
            dsn = [(p_ * (dp_ - o["dln"]) * SCALE).astype(BF16) for p_, dp_, o in zip(pn, dpn, ops)]
            for i, (a, jj) in enumerate(grp):
                rs_ = slice(jj * HD, (jj + 1) * HD)
                o = ops[i]
                dq_ref[a, rs_, :] = _dot(dsa[i], o["k2"]).astype(BF16)
                q2 = jnp.concatenate([o["qa"], o["qn"]], axis=0)
                do2 = jnp.concatenate([o["doa"], o["don"]], axis=0)
                dk_ref[a, rs_, :] = _dot_tn(jnp.concatenate([dsa[i][:, HD:], dsn[i]], axis=0), q2).astype(BF16)
                dv_ref[a, rs_, :] = _dot_tn(jnp.concatenate([pa[i][:, HD:].astype(BF16), pn[i].astype(BF16)], axis=0),
                                            do2).astype(BF16)

    return pl.pallas_call(
        body, grid=grid, in_specs=[own, nxt, prev, own, prev, own, own, nxt, own, nxt, own, nxt],
        out_specs=[own, own, own], out_shape=[_sds((4 * d, S // d, HD), BF16)] * 3,
        compiler_params=_cp("parallel", "parallel"), name="att_bwd")(q, q, k, k, v, v, do, do, lse, lse, delta, delta)


def _att_bwd_post(dqs, dks, dvs, dproj, tm=512):
    def body(*refs):
        dq, dk, dv, dp_ref, scr = refs[0:3], refs[3:6], refs[6:9], refs[10], refs[11]
        for g in range(3):
            _undilate(dq[g], scr, DILATIONS[g], tm)
            for h in range(4):
                dp_ref[:, g * W + h * HD:g * W + (h + 1) * HD] = scr[h].astype(BF16)
        for j, parts in enumerate((dk, dv)):
            acc = None
            for g in range(3):
                _undilate(parts[g], scr, DILATIONS[g], tm)
                vals = [scr[h] for h in range(4)]
                acc = vals if acc is None else [x + y for x, y in zip(acc, vals)]
            for h in range(4):
                dp_ref[:, (3 + j) * W + h * HD:(3 + j) * W + (h + 1) * HD] = acc[h].astype(BF16)

    dil = [_dil_spec(d, tm) for d in DILATIONS]
    return pl.pallas_call(
        body, grid=(S // tm,), in_specs=dil * 3 + [ANY],
        out_specs=pl.BlockSpec((tm, 5 * W), lambda i: (i, 1)), out_shape=_sds((S, DIN), BF16),
        scratch_shapes=[pltpu.VMEM((4, tm, HD), F32)],
        input_output_aliases={9: 0}, compiler_params=_cp("parallel"), name="att_bwd_post")(*dqs, *dks, *dvs, dproj)


def _mem_kv_fwd(mem_n, wkv):
    m = mem_n.shape[0]

    def body(a_ref, w_ref, o_ref):
        o_ref[...] = _dot(a_ref[...], w_ref[...])

    return pl.pallas_call(body, out_shape=_sds((m, 2 * W), F32), compiler_params=_cp(), name="mem_kv_fwd")(mem_n, wkv)


def _mem_fwd(proj, kv, y_all, tm=512):
    m = kv.shape[0]

    def body(q_ref, gt_ref, kv_ref, y_in, y_ref):
        gt = gt_ref[...].astype(F32)
        sl = gt * _sigmoid(gt)
        hs = [slice(h * HD, (h + 1) * HD) for h in range(4)]
        s = [_dot_nt(q_ref[:, cs].astype(BF16), kv_ref[:, cs].astype(BF16)) * SCALE for cs in hs]
        e = [jnp.exp(s_ - jnp.max(s_, axis=-1, keepdims=True)) for s_ in s]
        p = [(e_ * (1.0 / jnp.sum(e_, axis=-1, keepdims=True))).astype(BF16) for e_ in e]
        for h, cs in enumerate(hs):
            o = _dot(p[h], kv_ref[:, W + h * HD:W + (h + 1) * HD].astype(BF16))
            y_ref[:, cs] = (o * sl[:, cs]).astype(BF16)

    return pl.pallas_call(
        body, grid=(S // tm,),
        in_specs=[pl.BlockSpec((tm, W), lambda i: (i, P_MQ)), pl.BlockSpec((tm, W), lambda i: (i, P_MGATE)),
                  pl.BlockSpec((m, 2 * W), lambda i: (0, 0)), ANY],
        out_specs=pl.BlockSpec((None, tm, W), lambda i: (3, i, 0)), out_shape=_sds((4, S, W), BF16),
        input_output_aliases={3: 0}, compiler_params=_cp("parallel"), name="mem_fwd")(proj, proj, kv, y_all)


def _mem_bwd(proj, kv, dy_all, dproj, tm=512):
    m = kv.shape[0]
    ni = S // tm

    def body(q_ref, gt_ref, kv_ref, dy_ref, dp_in, dp_ref, dkv_ref, obuf, osem):
        i = pl.program_id(0)

        @pl.when(i == 0)
        def _():
            dkv_ref[...] = jnp.zeros_like(dkv_ref)

        gt = gt_ref[...].astype(F32)
        sg = _sigmoid(gt)
        sl = gt * sg
        dsl = sg * (1.0 + gt * (1.0 - sg))
        dyv = dy_ref[...].astype(F32)

        def fill(slot):
            hs = [slice(h * HD, (h + 1) * HD) for h in range(4)]
            vss = [slice(W + h * HD, W + (h + 1) * HD) for h in range(4)]
            q = [q_ref[:, cs].astype(BF16) for cs in hs]
            k = [kv_ref[:, cs].astype(BF16) for cs in hs]
            v = [kv_ref[:, vs].astype(BF16) for vs in vss]
            dob = [(dyv[:, cs] * sl[:, cs]).astype(BF16) for cs in hs]
            s = [_dot_nt(q_, k_) * SCALE for q_, k_ in zip(q, k)]
            dp = [_dot_nt(d_, v_) for d_, v_ in zip(dob, v)]
            e = [jnp.exp(s_ - jnp.max(s_, axis=-1, keepdims=True)) for s_ in s]
            p = [e_ * (1.0 / jnp.sum(e_, axis=-1, keepdims=True)) for e_ in e]
            pb = [p_.astype(BF16) for p_ in p]
            dsb = [(p_ * (dp_ - jnp.sum(dp_ * p_, axis=-1, keepdims=True)) * SCALE).astype(BF16) for p_, dp_ in zip(p, dp)]
            for h, (cs, vs) in enumerate(zip(hs, vss)):
                o = _dot(pb[h], v[h])
                slot[:, cs] = _dot(dsb[h], k[h]).astype(BF16)
                slot[:, vs] = (dyv[:, cs] * o * dsl[:, cs]).astype(BF16)
                dkv_ref[:, cs] += _dot_tn(dsb[h], q[h])
                dkv_ref[:, vs] += _dot_tn(pb[h], dob[h])

        _tile_put(obuf, osem, lambda st: dp_ref.at[pl.ds(st * tm, tm), pl.ds(P_MQ * W, 2 * W)], i, ni, fill)

    return pl.pallas_call(
        body, grid=(ni,),
        in_specs=[pl.BlockSpec((tm, W), lambda i: (i, P_MQ)), pl.BlockSpec((tm, W), lambda i: (i, P_MGATE)),
                  pl.BlockSpec((m, 2 * W), lambda i: (0, 0)), pl.BlockSpec((None, tm, W), lambda i: (3, i, 0)), ANY],
        out_specs=[ANY, pl.BlockSpec((m, 2 * W), lambda i: (0, 0))],
        out_shape=[_sds((S, DIN), BF16), _sds((m, 2 * W), F32)],
        scratch_shapes=[pltpu.VMEM((2, tm, 2 * W), BF16), pltpu.SemaphoreType.DMA((2,))],
        input_output_aliases={4: 0}, compiler_params=_cp("arbitrary"), name="mem_bwd")(proj, proj, kv, dy_all, dproj)


def _mem_kv_bwd(mem, g, mem_n, wkv, dkv):
    m = mem.shape[0]

    def body(x_ref, g_ref, a_ref, w_ref, d_ref, dw_ref, dg_ref):
        db = d_ref[...].astype(BF16)
        dw_ref[...] = _dot_tn(a_ref[...], db).astype(BF16)
        dn = _dot_nt(db, w_ref[...])
        xv = x_ref[...]
        xh = xv * lax.rsqrt(jnp.mean(xv * xv, axis=-1, keepdims=True) + EPS)
        dg_ref[...] = jnp.sum(dn * xh, axis=0, keepdims=True)

    return pl.pallas_call(
        body, out_shape=[_sds((D, 2 * W), BF16), _sds((1, D), F32)], compiler_params=_cp(), name="mem_kv_bwd")(mem, g, mem_n, wkv, dkv)


def _layer_fwd(x, h, ht, mem_n, p, wg):
    win, wkv, wbr, wo = wg
    proj = _proj_fwd(h, win)
    y_all = lax.empty((4, S, W), BF16)
    y_all = _gmlp_fwd(proj, p["gm_ln_g"], p["gm_ln_b"], p["gm_ws"], p["gm_bias"], y_all)
    y_all = _pool_fwd(proj, p["pool_w"], p["pool_scale"], y_all)
    qs, ks, vs = _att_prep(proj)
    os_, ls_ = zip(*[_att_fwd(qs[g], ks[g], vs[g], g) for g in range(3)])
    y_all, om, lse = _att_mix(os_, ls_, proj, y_all)
    kv = _mem_kv_fwd(mem_n, wkv)
    y_all = _mem_fwd(proj, kv, y_all)
    z = _merge_fwd(y_all, wbr, proj)
    x_new = _out_fwd(z, wo, x)
    return x_new, dict(x=x, ht=ht, proj=proj, y_all=y_all, om=om, lse=lse, mem_n=mem_n, kv=kv, z=z, qkv=(qs, ks, vs))


GRAD_PARTS = ((2, D // 2, CW), (2, D // 8, 2 * W), (2, 2 * W, D // NCHIP), (2, D // 8, D))
SUM_TILE = (64, 128, 256, 128)
ADAM_TILE = (128, 256, 2048, 256)
PLACE_TILE = (256, 256, 512, 256)


def _layer_bwd(dx, mem, p, wg, sv, exchange):
    win, wkv, wbr, wo = wg
    proj = sv["proj"]
    dz, d_wo = _out_bwd(dx, sv["z"], wo)
    dproj = lax.empty((S, DIN), BF16)
    dproj, dy_all, d_wbr = _merge_bwd(dz, sv["y_all"], wbr, proj, dproj)
    dproj, d_ws, d_bs, d_lg, d_lb = _gmlp_bwd(proj, dy_all, p["gm_ln_g"], p["gm_ln_b"], p["gm_ws"], p["gm_bias"], dproj)
    dproj, d_pw, d_sc = _pool_bwd(proj, dy_all, p["pool_w"], p["pool_scale"], dproj)
    dos, dls, lss, dproj = _att_bwd_pre(dy_all, proj, sv["om"], sv["lse"], dproj)
    qs, ks, vs = sv["qkv"]
    dqs, dks, dvs = zip(*[_att_bwd(qs[g], ks[g], vs[g], dos[g], lss[g], dls[g], g) for g in range(3)])
    dproj = _att_bwd_post(dqs, dks, dvs, dproj)
    dproj, dkv = _mem_bwd(proj, sv["kv"], dy_all, dproj)
    d_wkv, d_mg = _mem_kv_bwd(mem, p["mem_norm_g"], sv["mem_n"], wkv, dkv)
    d_win = _proj_bwd_w(sv["ht"], dproj)
    big = tuple(t.reshape((NCHIP,) + s) for t, s in zip((d_win, d_wkv, d_wbr, d_wo), GRAD_PARTS))
    inflight = exchange(big)
    dh = _proj_bwd_x(dproj, win, inflight[-1])
    dx_in, d_ng = _rms_bwd(dh, sv["x"], p["norm_g"], dx)
    small = dict(norm_g=d_ng, gm_ln_g=d_lg, gm_ln_b=d_lb, gm_ws=d_ws, gm_bs=d_bs[:, :, 0], pool_w=d_pw, pool_scale=d_sc, mem_norm_g=d_mg)
    return dx_in, (big,) + inflight, small


_SMALL = ("norm_g", "gm_ln_g", "gm_ln_b", "gm_ws", "gm_bs", "pool_w", "pool_scale", "mem_norm_g")


def _layer_params(l, norm_g, gm_ln_g, gm_ln_b, gm_ws, gm_bs, pool_w, pool_scale, mem_norm_g):
    return dict(norm_g=norm_g[l][None], gm_ln_g=gm_ln_g[l][None], gm_ln_b=gm_ln_b[l][None], gm_ws=gm_ws[l],
                gm_bias=jnp.broadcast_to(gm_bs[l][:, :, None], (4, HD, HD)), pool_w=pool_w[l],
                pool_scale=pool_scale[l][None], mem_norm_g=mem_norm_g[l][None])


def kernel(x, mem, norm_g, w_in, gm_ln_g, gm_ln_b, gm_ws, gm_bs, pool_w, pool_scale, mem_norm_g, w_mem_kv, w_branch, w_out, final_norm_g, loss_target, m_norm_g, m_w_in, m_gm_ln_g, m_gm_ln_b, m_gm_ws, m_gm_bs, m_pool_w, m_pool_scale, m_mem_norm_g, m_w_mem_kv, m_w_branch, m_w_out, m_final_norm_g, v_norm_g, v_w_in, v_gm_ln_g, v_gm_ln_b, v_gm_ws, v_gm_bs, v_pool_w, v_pool_scale, v_mem_norm_g, v_w_mem_kv, v_w_branch, v_w_out, v_final_norm_g):
    xs, memv, tgt = x[0], mem[0], loss_target[0]
    params = [_layer_params(l, norm_g, gm_ln_g, gm_ln_b, gm_ws, gm_bs, pool_w, pool_scale, mem_norm_g) for l in range(NL)]

    shards = (w_in.astype(BF16), w_mem_kv.astype(BF16), w_branch.astype(BF16), w_out.astype(BF16))
    lands, gsems, after = _gather_start(shards)

    saved, wgs = [], []
    mem_ns = [_rms_fwd(memv, params[l]["mem_norm_g"], memv.shape[0]) for l in range(NL)]
    for l in range(NL):
        h, ht = _rms_fwd_t(xs, params[l]["norm_g"])
        if l == 0:
            got, relay_sems = _gather_relay(shards, lands[0], gsems[0], [after, h] + mem_ns)
            got = _gather_wait_relay(got, relay_sems)
        else:
            got = _gather_wait(l, shards, lands[l], gsems[l], after)
        got = [_place_own(got[k], shards[k], l, PLACE_TILE[k]) for k in range(4)]
        wgs.append((got[0], got[1].reshape(D, 2 * W), got[2], got[3].reshape(D, D)))
        xs, sv = _layer_fwd(xs, h, ht, mem_ns[l], params[l], wgs[l])
        saved.append(sv)
        after = xs
    dx, d_fg, ls = _loss_head(xs, tgt, final_norm_g[None])
    loss = lax.psum(ls[0, 0], ("x", "y", "c"))

    flight, small = [None] * NL, [None] * NL
    for l in reversed(range(NL)):
        dx, flight[l], small[l] = _layer_bwd(dx, memv, params[l], wgs[l], saved[l], functools.partial(_exch_start, l))
    grad_x = dx[None]

    leaves = [jnp.stack([small[l][n] for l in range(NL)]) for n in _SMALL] + [d_fg]
    sizes = [t.size for t in leaves]
    packed = jnp.concatenate([t.reshape(-1, 128) for t in leaves], axis=0)
    rows = packed.shape[0]
    small_zone, small_sems, after = _small_start(packed)

    ws = dict(norm_g=norm_g, w_in=w_in, gm_ln_g=gm_ln_g, gm_ln_b=gm_ln_b, gm_ws=gm_ws, gm_bs=gm_bs, pool_w=pool_w,
              pool_scale=pool_scale, mem_norm_g=mem_norm_g, w_mem_kv=w_mem_kv, w_branch=w_branch, w_out=w_out,
              final_norm_g=final_norm_g)
    ms = dict(norm_g=m_norm_g, w_in=m_w_in, gm_ln_g=m_gm_ln_g, gm_ln_b=m_gm_ln_b, gm_ws=m_gm_ws, gm_bs=m_gm_bs,
              pool_w=m_pool_w, pool_scale=m_pool_scale, mem_norm_g=m_mem_norm_g, w_mem_kv=m_w_mem_kv,
              w_branch=m_w_branch, w_out=m_w_out, final_norm_g=m_final_norm_g)
    vs = dict(norm_g=v_norm_g, w_in=v_w_in, gm_ln_g=v_gm_ln_g, gm_ln_b=v_gm_ln_b, gm_ws=v_gm_ws, gm_bs=v_gm_bs,
              pool_w=v_pool_w, pool_scale=v_pool_scale, mem_norm_g=v_mem_norm_g, w_mem_kv=v_w_mem_kv,
              w_branch=v_w_branch, w_out=v_w_out, final_norm_g=v_final_norm_g)

    big = ("w_in", "w_mem_kv", "w_branch", "w_out")
    acc = {n: [lax.empty((ws[n].size // ws[n].shape[-1], ws[n].shape[-1]), F32) for _ in range(4)] for n in big}
    for l in reversed(range(NL)):
        parts, zones, sems, _ = flight[l]
        zones = _exch_wait(l, parts, zones, sems, after)
        full = _share_full([_sum_half(parts[k], zones[k], SUM_TILE[k]) for k in range(4)])
        for k, n in enumerate(big):
            acc[n] = _adamw_layer(l, ws[n], full[k], ms[n], vs[n], acc[n], ADAM_TILE[k])
        after = acc["w_in"][0]
    grads, upd = {}, {}
    for n in big:
        d_, m_, v_, g_ = (t.reshape(ws[n].shape) for t in acc[n])
        grads[n], upd[n] = g_, (d_, m_, v_)

    small_zone = _small_wait(packed, small_zone, small_sems, after)
    tot = _sum_small(packed, small_zone, max(t for t in range(8, 513, 8) if rows % t == 0))
    offs = [0]
    for sz in sizes:
        offs.append(offs[-1] + sz // 128)
    for i, n in enumerate(_SMALL + ("final_norm_g",)):
        grads[n] = tot[offs[i]:offs[i + 1]].reshape(ws[n].shape)
        upd[n] = _adamw(ws[n], grads[n], ms[n], vs[n])
    order = ("norm_g", "w_in", "gm_ln_g", "gm_ln_b", "gm_ws", "gm_bs", "pool_w", "pool_scale", "mem_norm_g", "w_mem_kv",
             "w_branch", "w_out", "final_norm_g")
    return (loss, grad_x, *[grads[n] for n in order], *[upd[n][0] for n in order], *[upd[n][1] for n in order],
            *[upd[n][2] for n in order])
```

```python
import functools
import math

import jax
import jax.numpy as jnp
from jax import lax
from jax.experimental import pallas as pl
from jax.experimental.pallas import tpu as pltpu

F32 = jnp.float32
BF16 = jnp.bfloat16

S = 4096
D = 1024
W = 512
DIN = 10752
NL = 4
NCHIP = 4
NDEV = 8
CW = DIN // NCHIP
TN_IN = 896
NJ = CW // TN_IN
HD = 128
EPS = 1e-6
NEG = -1e30
SCALE = HD ** -0.5
INV_SQRT2 = 1.0 / math.sqrt(2.0)
INV_SQRT2PI = 1.0 / math.sqrt(2.0 * math.pi)
POOL_WINDOWS = (2, 4, 8, 16)
DILATIONS = (1, 4, 16)
HALO = 16
NPIECE = DIN // W
P_AGATE, P_PIN, P_PGATE, P_CQ, P_CK, P_CV, P_CGATE, P_MQ, P_MGATE, P_GM = 2, 3, 4, 5, 8, 9, 10, 11, 12, 13
VMEM_LIMIT = 56 * 1024 * 1024

ADAM_LR, ADAM_B1, ADAM_B2, ADAM_EPS, ADAM_WD, ADAM_STEP = 0.001, 0.9, 0.999, 1e-08, 0.01, 10

MESH = pl.DeviceIdType.MESH
ANY = pl.BlockSpec(memory_space=pl.ANY)


def _cp(*sem):
    return pltpu.CompilerParams(dimension_semantics=sem or None, vmem_limit_bytes=VMEM_LIMIT)


def _sds(shape, dtype):
    return jax.ShapeDtypeStruct(shape, dtype)


def _sigmoid(v):
    return 1.0 / (1.0 + jnp.exp(-v))


def _dot(a, b):
    return jnp.dot(a, b, preferred_element_type=F32)


def _dot_nt(a, b):
    return lax.dot_general(a, b, (((1,), (1,)), ((), ())), preferred_element_type=F32)


def _dot_tn(a, b):
    return lax.dot_general(a, b, (((0,), (0,)), ((), ())), preferred_element_type=F32)


def _tile_put(buf, sem, dst_of, step, nsteps, fill):
    slot = step % 2

    def copy(s, st):
        return pltpu.make_async_copy(buf.at[s], dst_of(st), sem.at[s])

    @pl.when(step >= 2)
    def _():
        copy(slot, step).wait()

    fill(buf.at[slot])
    copy(slot, step).start()

    @pl.when(step == nsteps - 1)
    def _():
        if nsteps >= 2:
            copy(1 - slot, step).wait()
        copy(slot, step).wait()


def _my_pos():
    return lax.axis_index("x"), lax.axis_index("y"), lax.axis_index("c")


_CHIP_REL = ((1, 0), (0, 1), (1, 1))
_DEV_REL = tuple((dx, dy, dc) for dx in (0, 1) for dy in (0, 1) for dc in (0, 1))[1:]


HBM = pl.BlockSpec(memory_space=pltpu.HBM)
SEM = pl.BlockSpec(memory_space=pltpu.SEMAPHORE)
EFFECT = pltpu.SideEffectType.DATAFLOW_SIDE_EFFECTING
N_GATHER = 3 * 4
N_EXCH = 7 * 4


def _in_hbm(t):
    return pltpu.with_memory_space_constraint(t, pltpu.HBM)


def _gather_start(shards, first):
    nk = len(shards)
    nl = shards[0].shape[0]
    lands = [pltpu.HBM((NCHIP,) + s.shape[1:], BF16) for s in shards for _ in range(nl)]

    def body(*refs):
        ins, outs = refs[:nk], refs[nk:nk + nk * nl]
        sems = refs[nk + nk * nl:nk + nk * nl + 2 * nl]
        token = refs[-1]
        x, y, c = _my_pos()
        me = 2 * x + y
        for l in range(nl):
            for r, (dx, dy) in enumerate(_CHIP_REL):
                for k in range(nk):
                    src, dst = ins[k].at[l], outs[k * nl + l].at[me]
                    if first:
                        hf = pl.ds(c * (shards[k].shape[1] // 2), shards[k].shape[1] // 2)
                        src, dst = src.at[hf], dst.at[hf]
                    pltpu.make_async_remote_copy(
                        src_ref=src, dst_ref=dst, send_sem=sems[2 * l].at[r * nk + k],
                        recv_sem=sems[2 * l + 1].at[r * nk + k], device_id=(x ^ dx, y ^ dy, c), device_id_type=MESH).start()
        token[...] = jnp.zeros_like(token)

    res = pl.pallas_call(
        body, name="gather_start_first" if first else "gather_start_rest",
        out_shape=lands + [pltpu.SemaphoreType.DMA((N_GATHER,))] * (2 * nl) + [_sds((8, 128), F32)],
        in_specs=[HBM] * nk, out_specs=[HBM] * (nk * nl) + [SEM] * (2 * nl) + [pl.BlockSpec(memory_space=pltpu.VMEM)],
        compiler_params=pltpu.CompilerParams(has_side_effects=EFFECT))(*[_in_hbm(s) for s in shards])
    lands = [[res[k * nl + l] for k in range(nk)] for l in range(nl)]
    sems = [(res[nk * nl + 2 * l], res[nk * nl + 2 * l + 1]) for l in range(nl)]
    return lands, sems, res[-1]


def _gather_relay(shards, lands, sems, after):
    nk = len(shards)
    half = [s.shape[1] // 2 for s in shards]

    na = len(after)

    def body(*refs):
        ins, land = refs[:nk], refs[nk:2 * nk]
        send, recv = refs[2 * nk], refs[2 * nk + 1]
        send2, recv2 = refs[3 * nk + 2 + na], refs[3 * nk + 3 + na]
        x, y, c = _my_pos()
        for r, (dx, dy) in enumerate(_CHIP_REL):
            cx, cy = x ^ dx, y ^ dy
            for k in range(nk):
                hf = pl.ds(c * half[k], half[k])
                got = land[k].at[2 * cx + cy].at[hf]
                cp = pltpu.make_async_remote_copy(
                    src_ref=ins[k].at[0].at[hf], dst_ref=got, send_sem=send.at[r * nk + k],
                    recv_sem=recv.at[r * nk + k], device_id=(cx, cy, c), device_id_type=MESH)
                cp.wait_send()
                cp.wait_recv()
                pltpu.make_async_remote_copy(
                    src_ref=got, dst_ref=got, send_sem=send2.at[r * nk + k], recv_sem=recv2.at[r * nk + k],
                    device_id=(x, y, 1 - c), device_id_type=MESH).start()

    res = pl.pallas_call(
        body, name="gather_relay",
        out_shape=[pltpu.HBM(t.shape, t.dtype) for t in lands] + [pltpu.SemaphoreType.DMA((N_GATHER,))] * 2,
        in_specs=[ANY] * nk + [HBM] * nk + [SEM, SEM] + [ANY] * na, out_specs=[HBM] * nk + [SEM, SEM],
        input_output_aliases={nk + k: k for k in range(nk)},
        compiler_params=pltpu.CompilerParams(has_side_effects=EFFECT))(*shards, *lands, *sems, *after)
    return res[:nk], (res[nk], res[nk + 1])


def _gather_wait_relay(lands, sems):
    nk = len(lands)
    half = [t.shape[1] // 2 for t in lands]

    def body(*refs):
        land = refs[:nk]
        send, recv = refs[nk], refs[nk + 1]
        x, y, c = _my_pos()
        for r, (dx, dy) in enumerate(_CHIP_REL):
            chip = 2 * (x ^ dx) + (y ^ dy)
            for k in range(nk):
                mine = land[k].at[chip].at[pl.ds(c * half[k], half[k])]
                theirs = land[k].at[chip].at[pl.ds((1 - c) * half[k], half[k])]
                cp = pltpu.make_async_remote_copy(
                    src_ref=mine, dst_ref=theirs, send_sem=send.at[r * nk + k], recv_sem=recv.at[r * nk + k],
                    device_id=(x, y, 1 - c), device_id_type=MESH)
                cp.wait_send()
                cp.wait_recv()

    return pl.pallas_call(
        body, name="gather_wait_relay", out_shape=[pltpu.HBM(t.shape, t.dtype) for t in lands],
        in_specs=[HBM] * nk + [SEM, SEM], out_specs=[HBM] * nk, input_output_aliases={k: k for k in range(nk)},
        compiler_params=pltpu.CompilerParams(has_side_effects=EFFECT))(*lands, *sems)


def _gather_wait(name, l, shards, lands, sems, after):
    nk = len(shards)

    def body(*refs):
        ins, land = refs[:nk], refs[nk:2 * nk]
        send, recv = refs[2 * nk], refs[2 * nk + 1]
        x, y, c = _my_pos()
        for r, (dx, dy) in enumerate(_CHIP_REL):
            cx, cy = x ^ dx, y ^ dy
            for k in range(nk):
                cp = pltpu.make_async_remote_copy(
                    src_ref=ins[k].at[l], dst_ref=land[k].at[2 * cx + cy], send_sem=send.at[r * nk + k],
                    recv_sem=recv.at[r * nk + k], device_id=(cx, cy, c), device_id_type=MESH)
                cp.wait_send()
                cp.wait_recv()

    return pl.pallas_call(
        body, name=name, out_shape=[pltpu.HBM(t.shape, t.dtype) for t in lands],
        in_specs=[ANY] * nk + [HBM] * nk + [SEM, SEM, ANY], out_specs=[HBM] * nk,
        input_output_aliases={nk + k: k for k in range(nk)},
        compiler_params=pltpu.CompilerParams(has_side_effects=EFFECT))(*shards, *lands, *sems, after)


def _place_own(land, shard, l, tr):
    _, rows, cols = shard.shape[0], shard.shape[-2], shard.shape[-1]
    lead = shard.shape[1:-2]
    nlead = math.prod(lead)
    sh = shard.reshape((shard.shape[0], nlead, rows, cols))
    ld = land.reshape((NCHIP, nlead, rows, cols))
    me = (2 * lax.axis_index("x") + lax.axis_index("y")).astype(jnp.int32).reshape(1)

    def body(me_ref, s_ref, l_in, o_ref):
        o_ref[...] = s_ref[...]

    out = pl.pallas_call(
        body,
        grid_spec=pltpu.PrefetchScalarGridSpec(
            num_scalar_prefetch=1, grid=(nlead, rows // tr),
            in_specs=[pl.BlockSpec((None, None, tr, cols), lambda b, i, me_ref: (l, b, i, 0)), ANY],
            out_specs=pl.BlockSpec((None, None, tr, cols), lambda b, i, me_ref: (me_ref[0], b, i, 0))),
        out_shape=_sds(ld.shape, BF16), input_output_aliases={2: 0},
        compiler_params=_cp("parallel", "parallel"), name="place_own")(me, sh, ld)
    return out.reshape(land.shape)


def _exch_start(l, parts):
    nk = len(parts)

    def body(*refs):
        ins, outs = refs[:nk], refs[nk:2 * nk]
        send, recv, token = refs[2 * nk:]
        x, y, c = _my_pos()
        for r, (dx, dy, dc) in enumerate(_DEV_REL):
            px, py, pc = x ^ dx, y ^ dy, c ^ dc
            for k in range(nk):
                pltpu.make_async_remote_copy(
                    src_ref=ins[k].at[2 * px + py, pc], dst_ref=outs[k].at[r], send_sem=send.at[r * nk + k],
                    recv_sem=recv.at[r * nk + k], device_id=(px, py, pc), device_id_type=MESH).start()
        token[...] = jnp.zeros_like(token)

    res = pl.pallas_call(
        body, name=f"exch_start_{l}",
        out_shape=[pltpu.HBM((7,) + p.shape[2:], BF16) for p in parts] + [pltpu.SemaphoreType.DMA((N_EXCH,))] * 2 + [_sds((8, 128), F32)],
        in_specs=[HBM] * nk, out_specs=[HBM] * nk + [SEM, SEM, pl.BlockSpec(memory_space=pltpu.VMEM)],
        compiler_params=pltpu.CompilerParams(has_side_effects=EFFECT))(*[_in_hbm(p) for p in parts])
    return res[:nk], (res[nk], res[nk + 1]), res[-1]


def _exch_wait(l, parts, lands, sems, after):
    nk = len(parts)

    def body(*refs):
        ins, land = refs[:nk], refs[nk:2 * nk]
        send, recv = refs[2 * nk], refs[2 * nk + 1]
        x, y, c = _my_pos()
        for r, (dx, dy, dc) in enumerate(_DEV_REL):
            px, py, pc = x ^ dx, y ^ dy, c ^ dc
            for k in range(nk):
                cp = pltpu.make_async_remote_copy(
                    src_ref=ins[k].at[2 * px + py, pc], dst_ref=land[k].at[r], send_sem=send.at[r * nk + k],
                    recv_sem=recv.at[r * nk + k], device_id=(px, py, pc), device_id_type=MESH)
                cp.wait_send()
                cp.wait_recv()

    return pl.pallas_call(
        body, name=f"exch_wait_{l}", out_shape=[pltpu.HBM(t.shape, t.dtype) for t in lands],
        in_specs=[ANY] * nk + [HBM] * nk + [SEM, SEM, ANY], out_specs=[HBM] * nk,
        input_output_aliases={nk + k: k for k in range(nk)},
        compiler_params=pltpu.CompilerParams(has_side_effects=EFFECT))(*parts, *lands, *sems, after)


def _chip_half():
    x, y, c = _my_pos()
    return jnp.stack([2 * x + y, c]).astype(jnp.int32)


def _sum_half(part, land, tr):
    _, _, r2, cols = part.shape

    def body(pos_ref, p_ref, r_ref, o_ref):
        acc = p_ref[...].astype(F32)
        for r in range(7):
            acc = acc + r_ref[r].astype(F32)
        o_ref[...] = acc

    return pl.pallas_call(
        body,
        grid_spec=pltpu.PrefetchScalarGridSpec(
            num_scalar_prefetch=1, grid=(r2 // tr,),
            in_specs=[pl.BlockSpec((None, None, tr, cols), lambda i, pos: (pos[0], pos[1], i, 0)),
                      pl.BlockSpec((7, tr, cols), lambda i, pos: (0, i, 0))],
            out_specs=pl.BlockSpec((None, tr, cols), lambda i, pos: (pos[1], i, 0))),
        out_shape=_sds((2, r2, cols), F32), compiler_params=_cp("parallel"), name="sum_half")(_chip_half(), part, land)


def _share_full(fulls):
    nk = len(fulls)

    def body(*refs):
        ins, outs = refs[:nk], refs[nk:2 * nk]
        send, recv = refs[2 * nk:]
        x, y, c = _my_pos()

        def copy(k, hf):
            return pltpu.make_async_remote_copy(
                src_ref=ins[k].at[hf], dst_ref=outs[k].at[hf], send_sem=send.at[k], recv_sem=recv.at[k],
                device_id=(x, y, 1 - c), device_id_type=MESH)

        for k in range(nk):
            copy(k, c).start()
        for k in range(nk):
            copy(k, 1 - c).wait_recv()
        for k in range(nk):
            copy(k, c).wait_send()

    return pl.pallas_call(
        body, out_shape=[_sds(f.shape, F32) for f in fulls], in_specs=[ANY] * nk, out_specs=[ANY] * nk,
        scratch_shapes=[pltpu.SemaphoreType.DMA((nk,))] * 2, input_output_aliases={k: k for k in range(nk)},
        name="share_full")(*fulls)


def _small_start(packed):
    def body(in_ref, out_ref, send, recv, token):
        x, y, c = _my_pos()
        for r, (dx, dy, dc) in enumerate(_DEV_REL):
            pltpu.make_async_remote_copy(
                src_ref=in_ref, dst_ref=out_ref.at[r], send_sem=send.at[r], recv_sem=recv.at[r],
                device_id=(x ^ dx, y ^ dy, c ^ dc), device_id_type=MESH).start()
        token[...] = jnp.zeros_like(token)

    res = pl.pallas_call(
        body, name="small_start",
        out_shape=[pltpu.HBM((7,) + packed.shape, F32)] + [pltpu.SemaphoreType.DMA((7,))] * 2 + [_sds((8, 128), F32)],
        in_specs=[HBM], out_specs=[HBM, SEM, SEM, pl.BlockSpec(memory_space=pltpu.VMEM)],
        compiler_params=pltpu.CompilerParams(has_side_effects=EFFECT))(_in_hbm(packed))
    return res[0], (res[1], res[2]), res[3]


def _small_wait(packed, land, sems, after):
    def body(in_ref, land_ref, send, recv, after_ref, out_ref):
        x, y, c = _my_pos()
        for r, (dx, dy, dc) in enumerate(_DEV_REL):
            cp = pltpu.make_async_remote_copy(
                src_ref=in_ref, dst_ref=land_ref.at[r], send_sem=send.at[r], recv_sem=recv.at[r],
                device_id=(x ^ dx, y ^ dy, c ^ dc), device_id_type=MESH)
            cp.wait_send()
            cp.wait_recv()

    return pl.pallas_call(
        body, name="small_wait", out_shape=pltpu.HBM(land.shape, land.dtype),
        in_specs=[ANY, HBM, SEM, SEM, ANY], out_specs=HBM, input_output_aliases={1: 0},
        compiler_params=pltpu.CompilerParams(has_side_effects=EFFECT))(packed, land, *sems, after)


def _sum_small(packed, land, tr):
    rows = packed.shape[0]
    x, y, c = _my_pos()
    me = (4 * x + 2 * y + c).astype(jnp.int32).reshape(1)

    def sbody(me_ref, p_ref, r_ref, o_ref):
        me_dev = me_ref[0]
        own = p_ref[...]
        acc = None
        for s in range(NDEV):
            rel = s ^ me_dev
            v = jnp.where(rel == 0, own, r_ref[jnp.maximum(rel - 1, 0)])
            acc = v if acc is None else acc + v
        o_ref[...] = acc

    return pl.pallas_call(
        sbody,
        grid_spec=pltpu.PrefetchScalarGridSpec(
            num_scalar_prefetch=1, grid=(rows // tr,),
            in_specs=[pl.BlockSpec((tr, 128), lambda i, me_ref: (i, 0)), pl.BlockSpec((7, tr, 128), lambda i, me_ref: (0, i, 0))],
            out_specs=pl.BlockSpec((tr, 128), lambda i, me_ref: (i, 0))),
        out_shape=_sds((rows, 128), F32), compiler_params=_cp("parallel"), name="sum_small")(me, packed, land)


def _rms_fwd(x, g, tm):
    n = x.shape[0]

    def body(x_ref, g_ref, h_ref):
        xv = x_ref[...]
        r = lax.rsqrt(jnp.mean(xv * xv, axis=-1, keepdims=True) + EPS)
        h_ref[...] = (xv * r * g_ref[...]).astype(BF16)

    return pl.pallas_call(
        body, grid=(n // tm,),
        in_specs=[pl.BlockSpec((tm, D), lambda i: (i, 0)), pl.BlockSpec((1, D), lambda i: (0, 0))],
        out_specs=pl.BlockSpec((tm, D), lambda i: (i, 0)), out_shape=_sds((n, D), BF16),
        compiler_params=_cp("parallel"), name="rms_fwd")(x, g)


def _rms_fwd_t(x, g, tm=512):
    n = x.shape[0]

    def body(x_ref, g_ref, h_ref, ht_ref):
        xv = x_ref[...]
        r = lax.rsqrt(jnp.mean(xv * xv, axis=-1, keepdims=True) + EPS)
        h = xv * r * g_ref[...]
        h_ref[...] = h.astype(BF16)
        ht_ref[...] = h.T.astype(BF16)

    return pl.pallas_call(
        body, grid=(n // tm,),
        in_specs=[pl.BlockSpec((tm, D), lambda i: (i, 0)), pl.BlockSpec((1, D), lambda i: (0, 0))],
        out_specs=[pl.BlockSpec((tm, D), lambda i: (i, 0)), pl.BlockSpec((D, tm), lambda i: (0, i))],
        out_shape=[_sds((n, D), BF16), _sds((D, n), BF16)], compiler_params=_cp("parallel"), name="rms_fwd_t")(x, g)


def _rms_bwd(dh, x, g, dres, tm=512):
    n = x.shape[0]

    def body(dh_ref, x_ref, g_ref, dr_ref, dx_ref, dg_ref):
        i = pl.program_id(0)
        xv = x_ref[...]
        r = lax.rsqrt(jnp.mean(xv * xv, axis=-1, keepdims=True) + EPS)
        xh = xv * r
        dhv = dh_ref[...]
        dxh = dhv * g_ref[...]
        dx_ref[...] = dr_ref[...] + r * (dxh - xh * jnp.mean(dxh * xh, axis=-1, keepdims=True))
        part = jnp.sum(dhv * xh, axis=0, keepdims=True)

        @pl.when(i == 0)
        def _():
            dg_ref[...] = part

        @pl.when(i > 0)
        def _():
            dg_ref[...] += part

    row = pl.BlockSpec((tm, D), lambda i: (i, 0))
    vec = pl.BlockSpec((1, D), lambda i: (0, 0))
    return pl.pallas_call(
        body, grid=(n // tm,), in_specs=[row, row, vec, row], out_specs=[row, vec],
        out_shape=[_sds((n, D), F32), _sds((1, D), F32)], compiler_params=_cp("arbitrary"), name="rms_bwd")(dh, x, g, dres)


def _loss_head(x, tgt, g, tm=512):
    def body(x_ref, t_ref, g_ref, dx_ref, dg_ref, ls_ref):
        i = pl.program_id(0)
        xv = x_ref[...]
        r = lax.rsqrt(jnp.mean(xv * xv, axis=-1, keepdims=True) + EPS)
        xh = xv * r
        gv = g_ref[...]
        diff = xh * gv - t_ref[...]
        dy = diff * (1.0 / D)
        dxh = dy * gv
        dx_ref[...] = r * (dxh - xh * jnp.mean(dxh * xh, axis=-1, keepdims=True))
        part_g = jnp.sum(dy * xh, axis=0, keepdims=True)
        part_l = jnp.sum(diff * diff, axis=0, keepdims=True)

        @pl.when(i == 0)
        def _():
            dg_ref[...] = part_g
            ls_ref[...] = part_l

        @pl.when(i > 0)
        def _():
            dg_ref[...] += part_g
            ls_ref[...] += part_l

        @pl.when(i == pl.num_programs(0) - 1)
        def _():
            tot = jnp.sum(ls_ref[...], axis=-1, keepdims=True) * (0.5 / D)
            ls_ref[...] = jnp.broadcast_to(tot, (1, D))

    row = pl.BlockSpec((tm, D), lambda i: (i, 0))
    vec = pl.BlockSpec((1, D), lambda i: (0, 0))
    return pl.pallas_call(
        body, grid=(S // tm,), in_specs=[row, row, vec], out_specs=[row, vec, vec],
        out_shape=[_sds((S, D), F32), _sds((1, D), F32), _sds((1, D), F32)],
        compiler_params=_cp("arbitrary"), name="loss_head")(x, tgt, g)


def _adamw(w, g, m, v):
    shape = w.shape
    cols = shape[-1] if w.ndim > 1 else shape[0]
    rows = w.size // cols
    w2, g2, m2, v2 = (t.reshape(rows, cols) for t in (w, g, m, v))
    tr = rows
    while tr * cols * 4 > (1 << 20) and tr % 16 == 0:
        tr //= 2
    c1 = 1.0 - ADAM_B1 ** ADAM_STEP
    c2 = 1.0 - ADAM_B2 ** ADAM_STEP

    def body(w_ref, g_ref, m_ref, v_ref, d_ref, nm_ref, nv_ref):
        gv = g_ref[...]
        mn = ADAM_B1 * m_ref[...] + (1.0 - ADAM_B1) * gv
        vn = ADAM_B2 * v_ref[...] + (1.0 - ADAM_B2) * (gv * gv)
        d_ref[...] = -ADAM_LR * ((mn / c1) / (jnp.sqrt(vn / c2) + ADAM_EPS) + ADAM_WD * w_ref[...])
        nm_ref[...] = mn
        nv_ref[...] = vn

    blk = pl.BlockSpec((tr, cols), lambda i: (i, 0))
    outs = pl.pallas_call(
        body, grid=(rows // tr,), in_specs=[blk] * 4, out_specs=[blk] * 3,
        out_shape=[_sds((rows, cols), F32)] * 3, compiler_params=_cp("parallel"), name="adamw")(w2, g2, m2, v2)
    return tuple(o.reshape(shape) for o in outs)


def _adamw_layer(l, w, g, m, v, outs, tr):
    cols = w.shape[-1]
    rows = w.size // (NL * cols)
    nb = rows // tr
    w2, m2, v2 = (t.reshape(NL * rows, cols) for t in (w, m, v))
    g2 = g.reshape(rows, cols)
    c1 = 1.0 - ADAM_B1 ** ADAM_STEP
    c2 = 1.0 - ADAM_B2 ** ADAM_STEP

    def body(w_ref, g_ref, m_ref, v_ref, d_in, nm_in, nv_in, go_in, d_ref, nm_ref, nv_ref, go_ref):
        gv = g_ref[...]
        mn = ADAM_B1 * m_ref[...] + (1.0 - ADAM_B1) * gv
        vn = ADAM_B2 * v_ref[...] + (1.0 - ADAM_B2) * (gv * gv)
        d_ref[...] = -ADAM_LR * ((mn / c1) / (jnp.sqrt(vn / c2) + ADAM_EPS) + ADAM_WD * w_ref[...])
        nm_ref[...] = mn
        nv_ref[...] = vn
        go_ref[...] = gv

    lay = pl.BlockSpec((tr, cols), lambda i: (l * nb + i, 0))
    return pl.pallas_call(
        body, grid=(nb,), in_specs=[lay, pl.BlockSpec((tr, cols), lambda i: (i, 0)), lay, lay] + [ANY] * 4,
        out_specs=[lay] * 4, out_shape=[_sds((NL * rows, cols), F32)] * 4,
        input_output_aliases={4: 0, 5: 1, 6: 2, 7: 3}, compiler_params=_cp("parallel"), name="adamw_layer")(w2, g2, m2, v2, *outs)


def _proj_fwd(h, wg, tm=512):
    def body(h_ref, w_ref, o_ref):
        o_ref[...] = _dot(h_ref[...], w_ref[...]).astype(BF16)

    return pl.pallas_call(
        body, grid=(NCHIP, S // tm),
        in_specs=[pl.BlockSpec((tm, D), lambda c, i: (i, 0)), pl.BlockSpec((None, D, CW), lambda c, i: (c, 0, 0))],
        out_specs=pl.BlockSpec((tm, CW), lambda c, i: (i, c)), out_shape=_sds((S, DIN), BF16),
        compiler_params=_cp("parallel", "parallel"), name="proj_fwd")(h, wg)


def _proj_bwd_x(dproj, wg, dep, tm=1024):
    def body(d_ref, w_ref, dep_ref, o_ref):
        k = pl.program_id(1)
        part = _dot_nt(d_ref[...], w_ref[...])

        @pl.when(k == 0)
        def _():
            o_ref[...] = part

        @pl.when(k > 0)
        def _():
            o_ref[...] += part

    return pl.pallas_call(
        body, grid=(S // tm, NCHIP),
        in_specs=[pl.BlockSpec((tm, CW), lambda i, k: (i, k)), pl.BlockSpec((None, D, CW), lambda i, k: (k, 0, 0)), ANY],
        out_specs=pl.BlockSpec((tm, D), lambda i, k: (i, 0)), out_shape=_sds((S, D), F32),
        compiler_params=_cp("parallel", "arbitrary"), name="proj_bwd_x")(dproj, wg, dep)


def _proj_bwd_w(ht, dproj):
    def body(h_ref, d_ref, o_ref):
        o_ref[...] = _dot(h_ref[...], d_ref[...]).astype(BF16)

    return pl.pallas_call(
        body, grid=(NCHIP, NJ),
        in_specs=[pl.BlockSpec((D, S), lambda c, j: (0, 0)), pl.BlockSpec((S, TN_IN), lambda c, j: (0, c * NJ + j))],
        out_specs=pl.BlockSpec((None, D, TN_IN), lambda c, j: (c, 0, j)), out_shape=_sds((NCHIP, D, CW), BF16),
        compiler_params=_cp("parallel", "parallel"), name="proj_bwd_w")(ht, dproj)


def _merge_fwd(y_all, wbr, proj, tm=256):
    cb = D // NCHIP

    def body(y_ref, w_ref, *rest):
        g_refs, z_ref = rest[:8], rest[8]
        for c in range(NCHIP):
            acc = None
            for b in range(4):
                g = g_refs[2 * b + c // 2][:, (c % 2) * cb:(c % 2 + 1) * cb].astype(F32)
                t = _dot(y_ref[b], w_ref[c, b]) * _sigmoid(g)
                acc = t if acc is None else acc + t
            z_ref[:, c * cb:(c + 1) * cb] = acc.astype(BF16)

    g_specs = [pl.BlockSpec((tm, W), functools.partial(lambda j, i: (i, P_GM + j), j)) for j in range(8)]
    return pl.pallas_call(
        body, grid=(S // tm,),
        in_specs=[pl.BlockSpec((4, tm, W), lambda i: (0, i, 0)), pl.BlockSpec((NCHIP, 4, W, cb), lambda i: (0, 0, 0, 0))] + g_specs,
        out_specs=pl.BlockSpec((tm, D), lambda i: (i, 0)), out_shape=_sds((S, D), BF16),
        compiler_params=_cp("parallel"), name="merge_fwd")(y_all, wbr, *([proj] * 8))


def _merge_bwd(dz, y_all, wbr, proj, dproj, tm=512):
    cb = D // NCHIP
    ni = S // tm

    def body(dz_ref, y_ref, w_ref, ga_ref, gb_ref, dp_in, dp_ref, dy_ref, dw_ref, acc, obuf, osem):
        b = pl.program_id(0)
        i = pl.program_id(1)

        @pl.when(i == 0)
        def _():
            acc[...] = jnp.zeros_like(acc)

        yv = y_ref[...]
        dys = []

        def fill(slot):
            ws_ = [w_ref[c] for c in range(NCHIP)]
            t = [_dot(yv, wv) for wv in ws_]
            dts = []
            for c in range(NCHIP):
                g_ref = ga_ref if c < 2 else gb_ref
                g = _sigmoid(g_ref[:, (c % 2) * cb:(c % 2 + 1) * cb].astype(F32))
                dzc = dz_ref[:, c * cb:(c + 1) * cb].astype(F32)
                slot[:, c * cb:(c + 1) * cb] = (dzc * t[c] * g * (1.0 - g)).astype(BF16)
                dts.append((dzc * g).astype(BF16))
            dy = None
            for c in range(NCHIP):
                part = _dot_nt(dts[c], ws_[c])
                dy = part if dy is None else dy + part
            for c in range(NCHIP):
                acc[c] += _dot_tn(yv, dts[c])
            dys.append(dy)

        _tile_put(obuf, osem, lambda st: dp_ref.at[pl.ds((st % ni) * tm, tm), pl.ds(P_GM * W + (st // ni) * D, D)],
                  b * ni + i, 4 * ni, fill)
        dy_ref[...] = dys[0].astype(BF16)

        @pl.when(i == ni - 1)
        def _():
            dw_ref[...] = acc[...].astype(BF16)

    return pl.pallas_call(
        body, grid=(4, ni),
        in_specs=[pl.BlockSpec((tm, D), lambda b, i: (i, 0)), pl.BlockSpec((None, tm, W), lambda b, i: (b, i, 0)),
                  pl.BlockSpec((NCHIP, None, W, cb), lambda b, i: (0, b, 0, 0)),
                  pl.BlockSpec((tm, W), lambda b, i: (i, P_GM + 2 * b)), pl.BlockSpec((tm, W), lambda b, i: (i, P_GM + 2 * b + 1)), ANY],
        out_specs=[ANY, pl.BlockSpec((None, tm, W), lambda b, i: (b, i, 0)), pl.BlockSpec((NCHIP, None, W, cb), lambda b, i: (0, b, 0, 0))],
        out_shape=[_sds((S, DIN), BF16), _sds((4, S, W), BF16), _sds((NCHIP, 4, W, cb), BF16)],
        scratch_shapes=[pltpu.VMEM((NCHIP, W, cb), F32), pltpu.VMEM((2, tm, D), BF16), pltpu.SemaphoreType.DMA((2,))],
        input_output_aliases={5: 0}, compiler_params=_cp("arbitrary", "arbitrary"), name="merge_bwd")(dz, y_all, wbr, proj, proj, dproj)


def _out_fwd(z, wo, x, tm=512):
    def body(z_ref, w_ref, x_ref, o_ref):
        o_ref[...] = x_ref[...] + _dot(z_ref[...], w_ref[...])

    row = pl.BlockSpec((tm, D), lambda i: (i, 0))
    return pl.pallas_call(
        body, grid=(S // tm,), in_specs=[row, pl.BlockSpec((D, D), lambda i: (0, 0)), row], out_specs=row,
        out_shape=_sds((S, D), F32), compiler_params=_cp("parallel"), name="out_fwd")(z, wo, x)


def _out_bwd(dx, z, wo, tm=512):
    ni = S // tm

    def body(dx_ref, z_ref, w_ref, dz_ref, dw_ref, acc):
        i = pl.program_id(0)
        dxb = dx_ref[...].astype(BF16)
        dz_ref[...] = _dot_nt(dxb, w_ref[...]).astype(BF16)
        part = _dot_tn(z_ref[...], dxb)

        @pl.when(i == 0)
        def _():
            acc[...] = part

        @pl.when(i > 0)
        def _():
            acc[...] += part

        @pl.when(i == ni - 1)
        def _():
            dw_ref[...] = acc[...].astype(BF16)

    row = pl.BlockSpec((tm, D), lambda i: (i, 0))
    full = pl.BlockSpec((D, D), lambda i: (0, 0))
    return pl.pallas_call(
        body, grid=(ni,), in_specs=[row, row, full], out_specs=[row, full],
        out_shape=[_sds((S, D), BF16), _sds((D, D), BF16)], scratch_shapes=[pltpu.VMEM((D, D), F32)],
        compiler_params=_cp("arbitrary"), name="out_bwd")(dx, z, wo)


def _gelu_parts(a):
    cdf = 0.5 * (1.0 + lax.erf(a * INV_SQRT2))
    return a * cdf, cdf


def _ln_parts(v):
    mu = jnp.mean(v, axis=-1, keepdims=True)
    vc = v - mu
    rs = lax.rsqrt(jnp.mean(vc * vc, axis=-1, keepdims=True) + EPS)
    return vc * rs, rs


def _causal_mask():
    return lax.broadcasted_iota(jnp.int32, (HD, HD), 0) >= lax.broadcasted_iota(jnp.int32, (HD, HD), 1)


def _gmlp_fwd(proj, lg, lb, ws, bias, y_all, tm=512):
    def body(uv_ref, gt_ref, lg_ref, lb_ref, ws_ref, b_ref, y_in, y_ref):
        act, _ = _gelu_parts(uv_ref[...].astype(F32))
        u = act[:, :W]
        xh, _ = _ln_parts(act[:, W:])
        vn = (xh * lg_ref[...] + lb_ref[...]).astype(BF16)
        gt = gt_ref[...].astype(F32)
        us = u * (gt * _sigmoid(gt))
        mask = _causal_mask()
        for h in range(4):
            wm = jnp.where(mask, ws_ref[h], 0.0).astype(BF16)
            cs = slice(h * HD, (h + 1) * HD)
            for c in range(tm // HD):
                rs_ = slice(c * HD, (c + 1) * HD)
                mixed = _dot(wm, vn[rs_, cs]) + b_ref[h]
                y_ref[rs_, cs] = (us[rs_, cs] * mixed).astype(BF16)

    vec = pl.BlockSpec((1, W), lambda i: (0, 0))
    mats = pl.BlockSpec((4, HD, HD), lambda i: (0, 0, 0))
    return pl.pallas_call(
        body, grid=(S // tm,),
        in_specs=[pl.BlockSpec((tm, 2 * W), lambda i: (i, 0)), pl.BlockSpec((tm, W), lambda i: (i, P_AGATE)), vec, vec, mats, mats, ANY],
        out_specs=pl.BlockSpec((None, tm, W), lambda i: (0, i, 0)), out_shape=_sds((4, S, W), BF16),
        input_output_aliases={6: 0}, compiler_params=_cp("parallel"), name="gmlp_fwd")(proj, proj, lg, lb, ws, bias, y_all)


def _gmlp_bwd(proj, dy_all, lg, lb, ws, bias, dproj, tm=256):
    ni = S // tm

    def body(uv_ref, gt_ref, dy_ref, lg_ref, lb_ref, ws_ref, b_ref, dp_in, dp_ref, dws_ref, dbs_ref, dlg_ref, dlb_ref, mix_s, dvn_s):
        i = pl.program_id(0)

        @pl.when(i == 0)
        def _():
            dws_ref[...] = jnp.zeros_like(dws_ref)
            dbs_ref[...] = jnp.zeros_like(dbs_ref)
            dlg_ref[...] = jnp.zeros_like(dlg_ref)
            dlb_ref[...] = jnp.zeros_like(dlb_ref)

        a0 = uv_ref[...].astype(F32)
        act, cdf = _gelu_parts(a0)
        u = act[:, :W]
        xh, rs = _ln_parts(act[:, W:])
        lgv = lg_ref[...]
        vn = (xh * lgv + lb_ref[...]).astype(BF16)
        mask = _causal_mask()
        wms = [jnp.where(mask, ws_ref[h], 0.0).astype(BF16) for h in range(4)]
        blocks = [(slice(c * HD, (c + 1) * HD), slice(h * HD, (h + 1) * HD), h) for h in range(4) for c in range(tm // HD)]
        for rs_, cs, h in blocks:
            mix_s[rs_, cs] = _dot(wms[h], vn[rs_, cs]) + b_ref[h]
        mixed = mix_s[...]
        gt = gt_ref[...].astype(F32)
        sg = _sigmoid(gt)
        sl = gt * sg
        dyv = dy_ref[...].astype(F32)
        dum = dyv * sl
        dgate = dyv * (u * mixed) * (sg * (1.0 + gt * (1.0 - sg)))
        du = dum * mixed
        dmix = dum * u
        dmb = dmix.astype(BF16)
        for rs_, cs, h in blocks:
            dvn_s[rs_, cs] = _dot_tn(wms[h], dmb[rs_, cs])
        for rs_, cs, h in blocks:
            dws_ref[h] += _dot_nt(dmb[rs_, cs], vn[rs_, cs])
            dbs_ref[h] += dmix[rs_, cs]
        dvn = dvn_s[...]
        dlg_ref[...] += jnp.sum(dvn * xh, axis=0, keepdims=True)
        dlb_ref[...] += jnp.sum(dvn, axis=0, keepdims=True)
        dxh = dvn * lgv
        dv = rs * (dxh - jnp.mean(dxh, axis=-1, keepdims=True) - xh * jnp.mean(dxh * xh, axis=-1, keepdims=True))
        gp = cdf + a0 * (jnp.exp(-0.5 * a0 * a0) * INV_SQRT2PI)
        dp_ref[:, :W] = (du * gp[:, :W]).astype(BF16)
        dp_ref[:, W:2 * W] = (dv * gp[:, W:]).astype(BF16)
        dp_ref[:, 2 * W:] = dgate.astype(BF16)

        @pl.when(i == ni - 1)
        def _():
            for h in range(4):
                dws_ref[h] = jnp.where(mask, dws_ref[h], 0.0)
                dbs_ref[h] = jnp.broadcast_to(jnp.sum(dbs_ref[h], axis=1, keepdims=True), (HD, HD))

    vec = pl.BlockSpec((1, W), lambda i: (0, 0))
    mats = pl.BlockSpec((4, HD, HD), lambda i: (0, 0, 0))
    return pl.pallas_call(
        body, grid=(ni,),
        in_specs=[pl.BlockSpec((tm, 2 * W), lambda i: (i, 0)), pl.BlockSpec((tm, W), lambda i: (i, P_AGATE)),
                  pl.BlockSpec((None, tm, W), lambda i: (0, i, 0)), vec, vec, mats, mats, ANY],
        out_specs=[pl.BlockSpec((tm, 3 * W), lambda i: (i, 0)), mats, mats, vec, vec],
        out_shape=[_sds((S, DIN), BF16), _sds((4, HD, HD), F32), _sds((4, HD, HD), F32), _sds((1, W), F32), _sds((1, W), F32)],
        scratch_shapes=[pltpu.VMEM((tm, W), F32), pltpu.VMEM((tm, W), F32)],
        input_output_aliases={7: 0}, compiler_params=_cp("arbitrary"), name="gmlp_bwd")(proj, proj, dy_all, lg, lb, ws, bias, dproj)


def _pool_diff(p, halo, row0, tm):
    xx = jnp.concatenate([halo, p], axis=0)
    t1 = (row0 + 1 + lax.broadcasted_iota(jnp.int32, (tm, 1), 0)).astype(F32)
    out = []
    for g, win in enumerate(POOL_WINDOWS):
        s = xx[:, g * HD:(g + 1) * HD]
        sh = 1
        while sh < win:
            s = s + pltpu.roll(s, sh, 0)
            sh *= 2
        out.append(s[HALO:] / jnp.minimum(t1, float(win)) - p[:, g * HD:(g + 1) * HD])
    return out


def _pool_fwd(proj, pw, sc, y_all, tm=512):
    rb = tm // HALO

    def body(p_ref, h_ref, gt_ref, pw_ref, sc_ref, y_in, y_ref):
        i = pl.program_id(0)
        halo = jnp.where(i > 0, h_ref[...].astype(F32), 0.0)
        ds = _pool_diff(p_ref[...].astype(F32), halo, i * tm, tm)
        gt = gt_ref[...].astype(F32)
        sl = gt * _sigmoid(gt)
        for g in range(4):
            cs = slice(g * HD, (g + 1) * HD)
            lin = _dot(ds[g].astype(BF16), pw_ref[g].astype(BF16))
            y_ref[:, cs] = (lin * sc_ref[:, cs] * sl[:, cs]).astype(BF16)

    return pl.pallas_call(
        body, grid=(S // tm,),
        in_specs=[pl.BlockSpec((tm, W), lambda i: (i, P_PIN)),
                  pl.BlockSpec((HALO, W), lambda i: (jnp.maximum(i * rb - 1, 0), P_PIN)),
                  pl.BlockSpec((tm, W), lambda i: (i, P_PGATE)),
                  pl.BlockSpec((4, HD, HD), lambda i: (0, 0, 0)), pl.BlockSpec((1, W), lambda i: (0, 0)), ANY],
        out_specs=pl.BlockSpec((None, tm, W), lambda i: (1, i, 0)), out_shape=_sds((4, S, W), BF16),
        input_output_aliases={5: 0}, compiler_params=_cp("parallel"), name="pool_fwd")(proj, proj, proj, pw, sc, y_all)


def _pool_bwd(proj, dy_all, pw, sc, dproj, tm=256):
    ni = S // tm
    rb = tm // HALO
    last_rb = S // HALO - 1
    rx = tm + HALO

    def body(p_ref, h_ref, gt_ref, gh_ref, dy_ref, dyh_ref, pw_ref, sc_ref, dp_in, dp_ref, dpw_ref, dsc_ref, obuf, osem):
        i = pl.program_id(0)

        @pl.when(i == 0)
        def _():
            dpw_ref[...] = jnp.zeros_like(dpw_ref)
            dsc_ref[...] = jnp.zeros_like(dsc_ref)

        halo = jnp.where(i > 0, h_ref[...].astype(F32), 0.0)
        ds = _pool_diff(p_ref[...].astype(F32), halo, i * tm, tm)
        nxt = i < ni - 1
        gx = jnp.concatenate([gt_ref[...], gh_ref[...]], axis=0).astype(F32)
        dyx = jnp.concatenate([dy_ref[...].astype(F32), jnp.where(nxt, dyh_ref[...].astype(F32), 0.0)], axis=0)
        sgx = _sigmoid(gx)
        slx = gx * sgx
        scv = sc_ref[...]
        dlinx = dyx * slx * scv
        t1 = (i * tm + 1 + lax.broadcasted_iota(jnp.int32, (rx, 1), 0)).astype(F32)
        gt, sg, sl, dyv = gx[:tm], sgx[:tm], slx[:tm], dyx[:tm]
        dsl = sg * (1.0 + gt * (1.0 - sg))

        def fill(slot):
            for g, win in enumerate(POOL_WINDOWS):
                cs = slice(g * HD, (g + 1) * HD)
                wv = pw_ref[g].astype(BF16)
                dlb = dlinx[:, cs].astype(BF16)
                ddx = _dot_nt(dlb, wv)
                f = ddx / jnp.minimum(t1, float(win))
                sh = 1
                while sh < win:
                    f = f + pltpu.roll(f, rx - sh, 0)
                    sh *= 2
                slot[:, cs] = (f[:tm] - ddx[:tm]).astype(BF16)
                db = ds[g].astype(BF16)
                lin = _dot(db, wv)
                slot[:, W + g * HD:W + (g + 1) * HD] = (dyv[:, cs] * lin * scv[:, cs] * dsl[:, cs]).astype(BF16)
                dsc_ref[:, cs] += jnp.sum(dyv[:, cs] * sl[:, cs] * lin, axis=0, keepdims=True)
                dpw_ref[g] += _dot_tn(db, dlb[:tm])

        _tile_put(obuf, osem, lambda st: dp_ref.at[pl.ds(st * tm, tm), pl.ds(P_PIN * W, 2 * W)], i, ni, fill)

    mats = pl.BlockSpec((4, HD, HD), lambda i: (0, 0, 0))
    vec = pl.BlockSpec((1, W), lambda i: (0, 0))
    return pl.pallas_call(
        body, grid=(ni,),
        in_specs=[pl.BlockSpec((tm, W), lambda i: (i, P_PIN)),
                  pl.BlockSpec((HALO, W), lambda i: (jnp.maximum(i * rb - 1, 0), P_PIN)),
                  pl.BlockSpec((tm, W), lambda i: (i, P_PGATE)),
                  pl.BlockSpec((HALO, W), lambda i: (jnp.minimum((i + 1) * rb, last_rb), P_PGATE)),
                  pl.BlockSpec((None, tm, W), lambda i: (1, i, 0)),
                  pl.BlockSpec((None, HALO, W), lambda i: (1, jnp.minimum((i + 1) * rb, last_rb), 0)),
                  mats, vec, ANY],
        out_specs=[ANY, mats, vec],
        out_shape=[_sds((S, DIN), BF16), _sds((4, HD, HD), F32), _sds((1, W), F32)],
        scratch_shapes=[pltpu.VMEM((2, tm, 2 * W), BF16), pltpu.SemaphoreType.DMA((2,))],
        input_output_aliases={8: 0}, compiler_params=_cp("arbitrary"), name="pool_bwd")(proj, proj, proj, proj, dy_all, dy_all, pw, sc, dproj)


ATT_STEP = ((1, 4), (4, 1), (4, 1))
ATT_GROUP = 16
ATT_GROUP_BWD = 8


def _att_band():
    qi = lax.broadcasted_iota(jnp.int32, (HD, 2 * HD), 0)
    kj = lax.broadcasted_iota(jnp.int32, (HD, 2 * HD), 1)
    return jnp.logical_and(kj >= qi, kj <= qi + HD), kj < HD


def _att_keys(kp_ref, ko_ref, vp_ref, vo_ref, a, jj):
    if jj == 0:
        return (jnp.concatenate([kp_ref[a], ko_ref[a, :HD, :]], axis=0), jnp.concatenate([vp_ref[a], vo_ref[a, :HD, :]], axis=0))
    return ko_ref[a, (jj - 1) * HD:(jj + 1) * HD, :], vo_ref[a, (jj - 1) * HD:(jj + 1) * HD, :]


def _dilate(src, dst, d, rows, cast=None):
    for r in range(d):
        for h in range(4):
            v = src.at[h][pl.ds(r, rows // d, stride=d), :] if d > 1 else src[h]
            dst[r * 4 + h] = v if cast is None else v.astype(cast)


def _undilate(src, dst, d, rows):
    for r in range(d):
        for h in range(4):
            if d > 1:
                dst.at[h][pl.ds(r, rows // d, stride=d), :] = src[r * 4 + h].astype(F32)
            else:
                dst[h] = src[h].astype(F32)


def _dil_spec(d, tm):
    return pl.BlockSpec((4 * d, tm // d, HD), lambda i: (0, i, 0))


def _att_prep(proj, tm=512):
    def body(q0, q1, q2, k_ref, v_ref, *rest):
        outs, scr = rest[:9], rest[9]
        for j, (src, dsts) in enumerate(((q0, ((0, outs[0]),)), (q1, ((1, outs[1]),)), (q2, ((2, outs[2]),)),
                                         (k_ref, tuple((g, outs[3 + g]) for g in range(3))),
                                         (v_ref, tuple((g, outs[6 + g]) for g in range(3))))):
            for h in range(4):
                scr[j, h] = src[:, h * HD:(h + 1) * HD].astype(F32)
            for g, dst in dsts:
                _dilate(scr.at[j], dst, DILATIONS[g], tm, BF16)

    def piece(p):
        return pl.BlockSpec((tm, W), lambda i: (i, p))

    shapes = [_sds((4 * d, S // d, HD), BF16) for d in DILATIONS]
    res = pl.pallas_call(
        body, grid=(S // tm,),
        in_specs=[piece(P_CQ), piece(P_CQ + 1), piece(P_CQ + 2), piece(P_CK), piece(P_CV)],
        out_specs=[_dil_spec(d, tm) for d in DILATIONS] * 3, out_shape=shapes * 3,
        scratch_shapes=[pltpu.VMEM((5, 4, tm, HD), F32)],
        compiler_params=_cp("parallel"), name="att_prep")(proj, proj, proj, proj, proj)
    return res[0:3], res[3:6], res[6:9]


def _att_specs(g):
    d = DILATIONS[g]
    nres, njb = ATT_STEP[g]
    nb = S // d // HD
    own = pl.BlockSpec((4 * nres, njb * HD, HD), lambda r, j: (r, j, 0))
    prev = pl.BlockSpec((4 * nres, HD, HD), lambda r, j: (r, jnp.maximum(j * njb - 1, 0), 0))
    nxt = pl.BlockSpec((4 * nres, HD, HD), lambda r, j: (r, jnp.minimum((j + 1) * njb, nb - 1), 0))
    return (d // nres, nb // njb), own, prev, nxt


def _att_fwd(q, k, v, g):
    d = DILATIONS[g]
    nres, njb = ATT_STEP[g]
    grid, own, prev, _ = _att_specs(g)

    def body(q_ref, kp_ref, ko_ref, vp_ref, vo_ref, o_ref, l_ref):
        jb = pl.program_id(1)
        band, is_prev = _att_band()
        no_prev = jnp.where(is_prev, jnp.where(jb > 0, 0.0, NEG), 0.0)
        blocks = [(a, jj) for jj in range(njb) for a in range(4 * nres)]
        for g0 in range(0, len(blocks), ATT_GROUP):
            grp = blocks[g0:g0 + ATT_GROUP]
            s, v2 = [], []
            for a, jj in grp:
                k2_, v2_ = _att_keys(kp_ref, ko_ref, vp_ref, vo_ref, a, jj)
                s_ = jnp.where(band, _dot_nt(q_ref[a, jj * HD:(jj + 1) * HD, :], k2_) * SCALE, NEG)
                s.append(s_ + no_prev if jj == 0 else s_)
                v2.append(v2_)
            m = [jnp.max(s_, axis=-1, keepdims=True) for s_ in s]
            e = [jnp.exp(s_ - m_) for s_, m_ in zip(s, m)]
            den = [jnp.sum(e_, axis=-1, keepdims=True) for e_ in e]
            inv = [1.0 / d_ for d_ in den]
            for i, (a, jj) in enumerate(grp):
                rs_ = slice(jj * HD, (jj + 1) * HD)
                o_ref[a, rs_, :] = _dot((e[i] * inv[i]).astype(BF16), v2[i]).astype(BF16)
                l_ref[a, rs_, :] = jnp.broadcast_to(m[i] + jnp.log(den[i]), (HD, HD))

    return pl.pallas_call(
        body, grid=grid, in_specs=[own, prev, own, prev, own], out_specs=[own, own],
        out_shape=[_sds((4 * d, S // d, HD), BF16), _sds((4 * d, S // d, HD), F32)],
        compiler_params=_cp("parallel", "parallel"), name="att_fwd")(q, k, k, v, v)


def _att_mix(os_, ls_, proj, y_all, tm=512):
    def body(o0, o1, o2, l0, l1, l2, gt_ref, y_in, y_ref, om_ref, lt_ref, so1, so2, sl1, sl2):
        _undilate(o1, so1, DILATIONS[1], tm)
        _undilate(o2, so2, DILATIONS[2], tm)
        _undilate(l1, sl1, DILATIONS[1], tm)
        _undilate(l2, sl2, DILATIONS[2], tm)
        for h in range(4):
            a, b, c = l0[h], sl1[h], sl2[h]
            m = jnp.maximum(jnp.maximum(a, b), c)
            ea, eb, ec = jnp.exp(a - m), jnp.exp(b - m), jnp.exp(c - m)
            z = ea + eb + ec
            inv = 1.0 / z
            o = (ea * inv) * o0[h] + (eb * inv) * so1[h] + (ec * inv) * so2[h]
            gt = gt_ref[:, h * HD:(h + 1) * HD].astype(F32)
            om_ref[h] = o
            lt_ref[h] = m + jnp.log(z)
            y_ref[:, h * HD:(h + 1) * HD] = (o * (gt * _sigmoid(gt))).astype(BF16)

    dil = [_dil_spec(d, tm) for d in DILATIONS]
    return pl.pallas_call(
        body, grid=(S // tm,),
        in_specs=dil * 2 + [pl.BlockSpec((tm, W), lambda i: (i, P_CGATE)), ANY],
        out_specs=[pl.BlockSpec((None, tm, W), lambda i: (2, i, 0)), dil[0], dil[0]],
        out_shape=[_sds((4, S, W), BF16), _sds((4, S, HD), F32), _sds((4, S, HD), F32)],
        scratch_shapes=[pltpu.VMEM((4, tm, HD), F32)] * 4,
        input_output_aliases={7: 0}, compiler_params=_cp("parallel"), name="att_mix")(*os_, *ls_, proj, y_all)


def _att_bwd_pre(dy_all, proj, om, lse, dproj, tm=512):
    def body(dy_ref, gt_ref, om_ref, ls_ref, dp_in, *rest):
        dos, dls, lss, dp_ref, sdo, sdl = rest[0:3], rest[3:6], rest[6:8], rest[8], rest[9], rest[10]
        for h in range(4):
            cs = slice(h * HD, (h + 1) * HD)
            gt = gt_ref[:, cs].astype(F32)
            sg = _sigmoid(gt)
            dyv = dy_ref[:, cs].astype(F32)
            o = om_ref[h]
            do = dyv * (gt * sg)
            dp_ref[:, cs] = (dyv * o * (sg * (1.0 + gt * (1.0 - sg)))).astype(BF16)
            sdo[h] = do
            sdl[h] = jnp.broadcast_to(jnp.sum(do * o, axis=-1, keepdims=True), (tm, HD))
        for g, d in enumerate(DILATIONS):
            _dilate(sdo, dos[g], d, tm, BF16)
            _dilate(sdl, dls[g], d, tm)
            if g > 0:
                _dilate(ls_ref, lss[g - 1], d, tm)

    dil = [_dil_spec(d, tm) for d in DILATIONS]
    gcol = pl.BlockSpec((tm, W), lambda i: (i, P_CGATE))
    res = pl.pallas_call(
        body, grid=(S // tm,),
        in_specs=[pl.BlockSpec((None, tm, W), lambda i: (2, i, 0)), gcol, dil[0], dil[0], ANY],
        out_specs=dil + dil + dil[1:] + [gcol],
        out_shape=([_sds((4 * d, S // d, HD), BF16) for d in DILATIONS] + [_sds((4 * d, S // d, HD), F32) for d in DILATIONS]
                   + [_sds((4 * d, S // d, HD), F32) for d in DILATIONS[1:]] + [_sds((S, DIN), BF16)]),
        scratch_shapes=[pltpu.VMEM((4, tm, HD), F32)] * 2,
        input_output_aliases={4: 8}, compiler_params=_cp("parallel"), name="att_bwd_pre")(dy_all, proj, om, lse, dproj)
    return res[0:3], res[3:6], [lse] + list(res[6:8]), res[8]


def _att_bwd(q, k, v, do, lse, delta, g):
    d = DILATIONS[g]
    nres, njb = ATT_STEP[g]
    grid, own, prev, nxt = _att_specs(g)

    def body(qa_ref, qn_ref, kp_ref, ko_ref, vp_ref, vo_ref, doa_ref, don_ref, la_ref, ln_ref, da_ref, dn_ref,
             dq_ref, dk_ref, dv_ref):
        jb = pl.program_id(1)
        band, is_prev = _att_band()
        m_next = lax.broadcasted_iota(jnp.int32, (HD, HD), 1) >= lax.broadcasted_iota(jnp.int32, (HD, HD), 0)
        has_prev = jnp.where(is_prev, jnp.where(jb > 0, 1.0, 0.0), 1.0)
        has_next = jnp.where(jb < grid[1] - 1, 1.0, 0.0)

        def wide(t):
            return jnp.concatenate([t, t], axis=1)

        blocks = [(a, jj) for jj in range(njb) for a in range(4 * nres)]
        for g0 in range(0, len(blocks), ATT_GROUP_BWD):
            grp = blocks[g0:g0 + ATT_GROUP_BWD]
            ops = []
            for a, jj in grp:
                rs_ = slice(jj * HD, (jj + 1) * HD)
                k2, v2 = _att_keys(kp_ref, ko_ref, vp_ref, vo_ref, a, jj)
                if jj == njb - 1:
                    qn, don, lsn, dln, fn = qn_ref[a], don_ref[a], ln_ref[a], dn_ref[a], has_next
                else:
                    ns = slice((jj + 1) * HD, (jj + 2) * HD)
                    qn, don, lsn, dln, fn = qa_ref[a, ns, :], doa_ref[a, ns, :], la_ref[a, ns, :], da_ref[a, ns, :], None
                ops.append(dict(qa=qa_ref[a, rs_, :], doa=doa_ref[a, rs_, :], lsa=wide(la_ref[a, rs_, :]),
                                dla=wide(da_ref[a, rs_, :]), k2=k2, v2=v2, ko=ko_ref[a, rs_, :], vo=vo_ref[a, rs_, :],
                                qn=qn, don=don, lsn=lsn, dln=dln, fn=fn, first=jj == 0))
            sa = [_dot_nt(o["qa"], o["k2"]) for o in ops]
            dpa = [_dot_nt(o["doa"], o["v2"]) for o in ops]
            sn = [_dot_nt(o["qn"], o["ko"]) for o in ops]
            dpn = [_dot_nt(o["don"], o["vo"]) for o in ops]
            pa, pn = [], []
            for o, sa_, sn_ in zip(ops, sa, sn):
                p_ = jnp.where(band, jnp.exp(sa_ * SCALE - o["lsa"]), 0.0)
                pa.append(p_ * has_prev if o["first"] else p_)
                p_ = jnp.where(m_next, jnp.exp(sn_ * SCALE - o["lsn"]), 0.0)
                pn.append(p_ if o["fn"] is None else p_ * o["fn"])
            dsa = [(p_ * (dp_ - o["dla"]) * SCALE).astype(BF16) for p_, dp_, o in zip(pa, dpa, ops)]
            dsn = [(p_ * (dp_ - o["dln"]) * SCALE).astype(BF16) for p_, dp_, o in zip(pn, dpn, ops)]
            for i, (a, jj) in enumerate(grp):
                rs_ = slice(jj * HD, (jj + 1) * HD)
                o = ops[i]
                dq_ref[a, rs_, :] = _dot(dsa[i], o["k2"]).astype(BF16)
                q2 = jnp.concatenate([o["qa"], o["qn"]], axis=0)
                do2 = jnp.concatenate([o["doa"], o["don"]], axis=0)
                dk_ref[a, rs_, :] = _dot_tn(jnp.concatenate([dsa[i][:, HD:], dsn[i]], axis=0), q2).astype(BF16)
                dv_ref[a, rs_, :] = _dot_tn(jnp.concatenate([pa[i][:, HD:].astype(BF16), pn[i].astype(BF16)], axis=0),
                                            do2).astype(BF16)

    return pl.pallas_call(
        body, grid=grid, in_specs=[own, nxt, prev, own, prev, own, own, nxt, own, nxt, own, nxt],
        out_specs=[own, own, own], out_shape=[_sds((4 * d, S // d, HD), BF16)] * 3,
        compiler_params=_cp("parallel", "parallel"), name="att_bwd")(q, q, k, k, v, v, do, do, lse, lse, delta, delta)


def _att_bwd_post(dqs, dks, dvs, dproj, tm=512):
    def body(*refs):
        dq, dk, dv, dp_ref, scr = refs[0:3], refs[3:6], refs[6:9], refs[10], refs[11]
        for g in range(3):
            _undilate(dq[g], scr, DILATIONS[g], tm)
            for h in range(4):
                dp_ref[:, g * W + h * HD:g * W + (h + 1) * HD] = scr[h].astype(BF16)
        for j, parts in enumerate((dk, dv)):
            acc = None
            for g in range(3):
                _undilate(parts[g], scr, DILATIONS[g], tm)
                vals = [scr[h] for h in range(4)]
                acc = vals if acc is None else [x + y for x, y in zip(acc, vals)]
            for h in range(4):
                dp_ref[:, (3 + j) * W + h * HD:(3 + j) * W + (h + 1) * HD] = acc[h].astype(BF16)

    dil = [_dil_spec(d, tm) for d in DILATIONS]
    return pl.pallas_call(
        body, grid=(S // tm,), in_specs=dil * 3 + [ANY],
        out_specs=pl.BlockSpec((tm, 5 * W), lambda i: (i, 1)), out_shape=_sds((S, DIN), BF16),
        scratch_shapes=[pltpu.VMEM((4, tm, HD), F32)],
        input_output_aliases={9: 0}, compiler_params=_cp("parallel"), name="att_bwd_post")(*dqs, *dks, *dvs, dproj)


def _mem_kv_fwd(mem_n, wkv):
    m = mem_n.shape[0]

    def body(a_ref, w_ref, o_ref):
        o_ref[...] = _dot(a_ref[...], w_ref[...])

    return pl.pallas_call(body, out_shape=_sds((m, 2 * W), F32), compiler_params=_cp(), name="mem_kv_fwd")(mem_n, wkv)


def _mem_fwd(proj, kv, y_all, tm=512):
    m = kv.shape[0]

    def body(q_ref, gt_ref, kv_ref, y_in, y_ref):
        gt = gt_ref[...].astype(F32)
        sl = gt * _sigmoid(gt)
        hs = [slice(h * HD, (h + 1) * HD) for h in range(4)]
        s = [_dot_nt(q_ref[:, cs].astype(BF16), kv_ref[:, cs].astype(BF16)) * SCALE for cs in hs]
        e = [jnp.exp(s_ - jnp.max(s_, axis=-1, keepdims=True)) for s_ in s]
        p = [(e_ * (1.0 / jnp.sum(e_, axis=-1, keepdims=True))).astype(BF16) for e_ in e]
        for h, cs in enumerate(hs):
            o = _dot(p[h], kv_ref[:, W + h * HD:W + (h + 1) * HD].astype(BF16))
            y_ref[:, cs] = (o * sl[:, cs]).astype(BF16)

    return pl.pallas_call(
        body, grid=(S // tm,),
        in_specs=[pl.BlockSpec((tm, W), lambda i: (i, P_MQ)), pl.BlockSpec((tm, W), lambda i: (i, P_MGATE)),
                  pl.BlockSpec((m, 2 * W), lambda i: (0, 0)), ANY],
        out_specs=pl.BlockSpec((None, tm, W), lambda i: (3, i, 0)), out_shape=_sds((4, S, W), BF16),
        input_output_aliases={3: 0}, compiler_params=_cp("parallel"), name="mem_fwd")(proj, proj, kv, y_all)


def _mem_bwd(proj, kv, dy_all, dproj, tm=512):
    m = kv.shape[0]
    ni = S // tm

    def body(q_ref, gt_ref, kv_ref, dy_ref, dp_in, dp_ref, dkv_ref, obuf, osem):
        i = pl.program_id(0)

        @pl.when(i == 0)
        def _():
            dkv_ref[...] = jnp.zeros_like(dkv_ref)

        gt = gt_ref[...].astype(F32)
        sg = _sigmoid(gt)
        sl = gt * sg
        dsl = sg * (1.0 + gt * (1.0 - sg))
        dyv = dy_ref[...].astype(F32)

        def fill(slot):
            hs = [slice(h * HD, (h + 1) * HD) for h in range(4)]
            vss = [slice(W + h * HD, W + (h + 1) * HD) for h in range(4)]
            q = [q_ref[:, cs].astype(BF16) for cs in hs]
            k = [kv_ref[:, cs].astype(BF16) for cs in hs]
            v = [kv_ref[:, vs].astype(BF16) for vs in vss]
            dob = [(dyv[:, cs] * sl[:, cs]).astype(BF16) for cs in hs]
            s = [_dot_nt(q_, k_) * SCALE for q_, k_ in zip(q, k)]
            dp = [_dot_nt(d_, v_) for d_, v_ in zip(dob, v)]
            e = [jnp.exp(s_ - jnp.max(s_, axis=-1, keepdims=True)) for s_ in s]
            p = [e_ * (1.0 / jnp.sum(e_, axis=-1, keepdims=True)) for e_ in e]
            pb = [p_.astype(BF16) for p_ in p]
            dsb = [(p_ * (dp_ - jnp.sum(dp_ * p_, axis=-1, keepdims=True)) * SCALE).astype(BF16) for p_, dp_ in zip(p, dp)]
            for h, (cs, vs) in enumerate(zip(hs, vss)):
                o = _dot(pb[h], v[h])
                slot[:, cs] = _dot(dsb[h], k[h]).astype(BF16)
                slot[:, vs] = (dyv[:, cs] * o * dsl[:, cs]).astype(BF16)
                dkv_ref[:, cs] += _dot_tn(dsb[h], q[h])
                dkv_ref[:, vs] += _dot_tn(pb[h], dob[h])

        _tile_put(obuf, osem, lambda st: dp_ref.at[pl.ds(st * tm, tm), pl.ds(P_MQ * W, 2 * W)], i, ni, fill)

    return pl.pallas_call(
        body, grid=(ni,),
        in_specs=[pl.BlockSpec((tm, W), lambda i: (i, P_MQ)), pl.BlockSpec((tm, W), lambda i: (i, P_MGATE)),
                  pl.BlockSpec((m, 2 * W), lambda i: (0, 0)), pl.BlockSpec((None, tm, W), lambda i: (3, i, 0)), ANY],
        out_specs=[ANY, pl.BlockSpec((m, 2 * W), lambda i: (0, 0))],
        out_shape=[_sds((S, DIN), BF16), _sds((m, 2 * W), F32)],
        scratch_shapes=[pltpu.VMEM((2, tm, 2 * W), BF16), pltpu.SemaphoreType.DMA((2,))],
        input_output_aliases={4: 0}, compiler_params=_cp("arbitrary"), name="mem_bwd")(proj, proj, kv, dy_all, dproj)


def _mem_kv_bwd(mem, g, mem_n, wkv, dkv):
    m = mem.shape[0]

    def body(x_ref, g_ref, a_ref, w_ref, d_ref, dw_ref, dg_ref):
        db = d_ref[...].astype(BF16)
        dw_ref[...] = _dot_tn(a_ref[...], db).astype(BF16)
        dn = _dot_nt(db, w_ref[...])
        xv = x_ref[...]
        xh = xv * lax.rsqrt(jnp.mean(xv * xv, axis=-1, keepdims=True) + EPS)
        dg_ref[...] = jnp.sum(dn * xh, axis=0, keepdims=True)

    return pl.pallas_call(
        body, out_shape=[_sds((D, 2 * W), BF16), _sds((1, D), F32)], compiler_params=_cp(), name="mem_kv_bwd")(mem, g, mem_n, wkv, dkv)


def _layer_fwd(x, h, ht, mem_n, p, wg):
    win, wkv, wbr, wo = wg
    proj = _proj_fwd(h, win)
    y_all = lax.empty((4, S, W), BF16)
    y_all = _gmlp_fwd(proj, p["gm_ln_g"], p["gm_ln_b"], p["gm_ws"], p["gm_bias"], y_all)
    y_all = _pool_fwd(proj, p["pool_w"], p["pool_scale"], y_all)
    qs, ks, vs = _att_prep(proj)
    os_, ls_ = zip(*[_att_fwd(qs[g], ks[g], vs[g], g) for g in range(3)])
    y_all, om, lse = _att_mix(os_, ls_, proj, y_all)
    kv = _mem_kv_fwd(mem_n, wkv)
    y_all = _mem_fwd(proj, kv, y_all)
    z = _merge_fwd(y_all, wbr, proj)
    x_new = _out_fwd(z, wo, x)
    return x_new, dict(x=x, ht=ht, proj=proj, y_all=y_all, om=om, lse=lse, mem_n=mem_n, kv=kv, z=z, qkv=(qs, ks, vs))


GRAD_PARTS = ((2, D // 2, CW), (2, D // 8, 2 * W), (2, 2 * W, D // NCHIP), (2, D // 8, D))
SUM_TILE = (64, 128, 256, 128)
ADAM_TILE = (128, 256, 2048, 256)
PLACE_TILE = (256, 256, 512, 256)


def _layer_bwd(dx, mem, p, wg, sv, exchange):
    win, wkv, wbr, wo = wg
    proj = sv["proj"]
    dz, d_wo = _out_bwd(dx, sv["z"], wo)
    dproj = lax.empty((S, DIN), BF16)
    dproj, dy_all, d_wbr = _merge_bwd(dz, sv["y_all"], wbr, proj, dproj)
    dproj, d_ws, d_bs, d_lg, d_lb = _gmlp_bwd(proj, dy_all, p["gm_ln_g"], p["gm_ln_b"], p["gm_ws"], p["gm_bias"], dproj)
    dproj, d_pw, d_sc = _pool_bwd(proj, dy_all, p["pool_w"], p["pool_scale"], dproj)
    dos, dls, lss, dproj = _att_bwd_pre(dy_all, proj, sv["om"], sv["lse"], dproj)
    qs, ks, vs = sv["qkv"]
    dqs, dks, dvs = zip(*[_att_bwd(qs[g], ks[g], vs[g], dos[g], lss[g], dls[g], g) for g in range(3)])
    dproj = _att_bwd_post(dqs, dks, dvs, dproj)
    dproj, dkv = _mem_bwd(proj, sv["kv"], dy_all, dproj)
    d_wkv, d_mg = _mem_kv_bwd(mem, p["mem_norm_g"], sv["mem_n"], wkv, dkv)
    d_win = _proj_bwd_w(sv["ht"], dproj)
    big = tuple(t.reshape((NCHIP,) + s) for t, s in zip((d_win, d_wkv, d_wbr, d_wo), GRAD_PARTS))
    inflight = exchange(big)
    dh = _proj_bwd_x(dproj, win, inflight[-1])
    dx_in, d_ng = _rms_bwd(dh, sv["x"], p["norm_g"], dx)
    small = dict(norm_g=d_ng, gm_ln_g=d_lg, gm_ln_b=d_lb, gm_ws=d_ws, gm_bs=d_bs[:, :, 0], pool_w=d_pw, pool_scale=d_sc, mem_norm_g=d_mg)
    return dx_in, (big,) + inflight, small


_SMALL = ("norm_g", "gm_ln_g", "gm_ln_b", "gm_ws", "gm_bs", "pool_w", "pool_scale", "mem_norm_g")


def _layer_params(l, norm_g, gm_ln_g, gm_ln_b, gm_ws, gm_bs, pool_w, pool_scale, mem_norm_g):
    return dict(norm_g=norm_g[l][None], gm_ln_g=gm_ln_g[l][None], gm_ln_b=gm_ln_b[l][None], gm_ws=gm_ws[l],
                gm_bias=jnp.broadcast_to(gm_bs[l][:, :, None], (4, HD, HD)), pool_w=pool_w[l],
                pool_scale=pool_scale[l][None], mem_norm_g=mem_norm_g[l][None])


def kernel(x, mem, norm_g, w_in, gm_ln_g, gm_ln_b, gm_ws, gm_bs, pool_w, pool_scale, mem_norm_g, w_mem_kv, w_branch, w_out, final_norm_g, loss_target, m_norm_g, m_w_in, m_gm_ln_g, m_gm_ln_b, m_gm_ws, m_gm_bs, m_pool_w, m_pool_scale, m_mem_norm_g, m_w_mem_kv, m_w_branch, m_w_out, m_final_norm_g, v_norm_g, v_w_in, v_gm_ln_g, v_gm_ln_b, v_gm_ws, v_gm_bs, v_pool_w, v_pool_scale, v_mem_norm_g, v_w_mem_kv, v_w_branch, v_w_out, v_final_norm_g):
    xs, memv, tgt = x[0], mem[0], loss_target[0]
    params = [_layer_params(l, norm_g, gm_ln_g, gm_ln_b, gm_ws, gm_bs, pool_w, pool_scale, mem_norm_g) for l in range(NL)]

    large = (w_in, w_mem_kv, w_branch, w_out)
    first = tuple(t[:1].astype(BF16) for t in large)
    lands, gsems, after = _gather_start(first, True)
    rest = tuple((t[1:] + after[0, 0]).astype(BF16) for t in large)
    lands_r, gsems_r, _ = _gather_start(rest, False)
    lands, gsems = lands + lands_r, gsems + gsems_r

    saved, wgs = [], []
    mem_ns = [_rms_fwd(memv, params[l]["mem_norm_g"], memv.shape[0]) for l in range(NL)]
    for l in range(NL):
        h, ht = _rms_fwd_t(xs, params[l]["norm_g"])
        if l == 0:
            got, relay_sems = _gather_relay(first, lands[0], gsems[0], [after, h] + mem_ns)
            got = _gather_wait_relay(got, relay_sems)
            got = [_place_own(got[k], first[k], 0, PLACE_TILE[k]) for k in range(4)]
        else:
            got = _gather_wait(f"gather_wait_{l}", l - 1, rest, lands[l], gsems[l], after)
            got = [_place_own(got[k], rest[k], l - 1, PLACE_TILE[k]) for k in range(4)]
        wgs.append((got[0], got[1].reshape(D, 2 * W), got[2], got[3].reshape(D, D)))
        xs, sv = _layer_fwd(xs, h, ht, mem_ns[l], params[l], wgs[l])
        saved.append(sv)
        after = xs
    dx, d_fg, ls = _loss_head(xs, tgt, final_norm_g[None])
    loss = lax.psum(ls[0, 0], ("x", "y", "c"))

    flight, small = [None] * NL, [None] * NL
    for l in reversed(range(NL)):
        dx, flight[l], small[l] = _layer_bwd(dx, memv, params[l], wgs[l], saved[l], functools.partial(_exch_start, l))
    grad_x = dx[None]

    leaves = [jnp.stack([small[l][n] for l in range(NL)]) for n in _SMALL] + [d_fg]
    sizes = [t.size for t in leaves]
    packed = jnp.concatenate([t.reshape(-1, 128) for t in leaves], axis=0)
    rows = packed.shape[0]
    small_zone, small_sems, after = _small_start(packed)

    ws = dict(norm_g=norm_g, w_in=w_in, gm_ln_g=gm_ln_g, gm_ln_b=gm_ln_b, gm_ws=gm_ws, gm_bs=gm_bs, pool_w=pool_w,
              pool_scale=pool_scale, mem_norm_g=mem_norm_g, w_mem_kv=w_mem_kv, w_branch=w_branch, w_out=w_out,
              final_norm_g=final_norm_g)
    ms = dict(norm_g=m_norm_g, w_in=m_w_in, gm_ln_g=m_gm_ln_g, gm_ln_b=m_gm_ln_b, gm_ws=m_gm_ws, gm_bs=m_gm_bs,
              pool_w=m_pool_w, pool_scale=m_pool_scale, mem_norm_g=m_mem_norm_g, w_mem_kv=m_w_mem_kv,
              w_branch=m_w_branch, w_out=m_w_out, final_norm_g=m_final_norm_g)
    vs = dict(norm_g=v_norm_g, w_in=v_w_in, gm_ln_g=v_gm_ln_g, gm_ln_b=v_gm_ln_b, gm_ws=v_gm_ws, gm_bs=v_gm_bs,
              pool_w=v_pool_w, pool_scale=v_pool_scale, mem_norm_g=v_mem_norm_g, w_mem_kv=v_w_mem_kv,
              w_branch=v_w_branch, w_out=v_w_out, final_norm_g=v_final_norm_g)

    big = ("w_in", "w_mem_kv", "w_branch", "w_out")
    acc = {n: [lax.empty((ws[n].size // ws[n].shape[-1], ws[n].shape[-1]), F32) for _ in range(4)] for n in big}
    for l in reversed(range(NL)):
        parts, zones, sems, _ = flight[l]
        zones = _exch_wait(l, parts, zones, sems, after)
        full = _share_full([_sum_half(parts[k], zones[k], SUM_TILE[k]) for k in range(4)])
        for k, n in enumerate(big):
            acc[n] = _adamw_layer(l, ws[n], full[k], ms[n], vs[n], acc[n], ADAM_TILE[k])
        after = acc["w_in"][0]
    grads, upd = {}, {}
    for n in big:
        d_, m_, v_, g_ = (t.reshape(ws[n].shape) for t in acc[n])
        grads[n], upd[n] = g_, (d_, m_, v_)

    small_zone = _small_wait(packed, small_zone, small_sems, after)
    tot = _sum_small(packed, small_zone, max(t for t in range(8, 513, 8) if rows % t == 0))
    offs = [0]
    for sz in sizes:
        offs.append(offs[-1] + sz // 128)
    for i, n in enumerate(_SMALL + ("final_norm_g",)):
        grads[n] = tot[offs[i]:offs[i + 1]].reshape(ws[n].shape)
        upd[n] = _adamw(ws[n], grads[n], ms[n], vs[n])
    order = ("norm_g", "w_in", "gm_ln_g", "gm_ln_b", "gm_ws", "gm_bs", "pool_w", "pool_scale", "mem_norm_g", "w_mem_kv",
             "w_branch", "w_out", "final_norm_g")
    return (loss, grad_x, *[grads[n] for n in order], *[upd[n][0] for n in order], *[upd[n][1] for n in order],
            *[upd[n][2] for n in order])
```

```python
import functools
import math

import jax
import jax.numpy as jnp
from jax import lax
from jax.experimental import pallas as pl
from jax.experimental.pallas import tpu as pltpu

F32 = jnp.float32
BF16 = jnp.bfloat16

S = 4096
D = 1024
W = 512
DIN = 10752
NL = 4
NCHIP = 4
NDEV = 8
CW = DIN // NCHIP
TN_IN = 896
NJ = CW // TN_IN
HD = 128
EPS = 1e-6
NEG = -1e30
SCALE = HD ** -0.5
INV_SQRT2 = 1.0 / math.sqrt(2.0)
INV_SQRT2PI = 1.0 / math.sqrt(2.0 * math.pi)
POOL_WINDOWS = (2, 4, 8, 16)
DILATIONS = (1, 4, 16)
HALO = 16
NPIECE = DIN // W
P_AGATE, P_PIN, P_PGATE, P_CQ, P_CK, P_CV, P_CGATE, P_MQ, P_MGATE, P_GM = 2, 3, 4, 5, 8, 9, 10, 11, 12, 13
VMEM_LIMIT = 56 * 1024 * 1024

ADAM_LR, ADAM_B1, ADAM_B2, ADAM_EPS, ADAM_WD, ADAM_STEP = 0.001, 0.9, 0.999, 1e-08, 0.01, 10

MESH = pl.DeviceIdType.MESH
ANY = pl.BlockSpec(memory_space=pl.ANY)


def _cp(*sem):
    return pltpu.CompilerParams(dimension_semantics=sem or None, vmem_limit_bytes=VMEM_LIMIT)


def _sds(shape, dtype):
    return jax.ShapeDtypeStruct(shape, dtype)


def _sigmoid(v):
    return 1.0 / (1.0 + jnp.exp(-v))


def _dot(a, b):
    return jnp.dot(a, b, preferred_element_type=F32)


def _dot_nt(a, b):
    return lax.dot_general(a, b, (((1,), (1,)), ((), ())), preferred_element_type=F32)


def _dot_tn(a, b):
    return lax.dot_general(a, b, (((0,), (0,)), ((), ())), preferred_element_type=F32)


def _tile_put(buf, sem, dst_of, step, nsteps, fill):
    slot = step % 2

    def copy(s, st):
        return pltpu.make_async_copy(buf.at[s], dst_of(st), sem.at[s])

    @pl.when(step >= 2)
    def _():
        copy(slot, step).wait()

    fill(buf.at[slot])
    copy(slot, step).start()

    @pl.when(step == nsteps - 1)
    def _():
        if nsteps >= 2:
            copy(1 - slot, step).wait()
        copy(slot, step).wait()


def _my_pos():
    return lax.axis_index("x"), lax.axis_index("y"), lax.axis_index("c")


_CHIP_REL = ((1, 0), (0, 1), (1, 1))
_DEV_REL = tuple((dx, dy, dc) for dx in (0, 1) for dy in (0, 1) for dc in (0, 1))[1:]


HBM = pl.BlockSpec(memory_space=pltpu.HBM)
SEM = pl.BlockSpec(memory_space=pltpu.SEMAPHORE)
EFFECT = pltpu.SideEffectType.DATAFLOW_SIDE_EFFECTING
N_GATHER = 3 * 4
N_EXCH = 7 * 4


def _in_hbm(t):
    return pltpu.with_memory_space_constraint(t, pltpu.HBM)


def _gather_start(shards, first):
    nk = len(shards)
    nl = shards[0].shape[0]
    lands = [pltpu.HBM((NCHIP,) + s.shape[1:], BF16) for s in shards for _ in range(nl)]

    def body(*refs):
        ins, outs = refs[:nk], refs[nk:nk + nk * nl]
        sems = refs[nk + nk * nl:nk + nk * nl + 2 * nl]
        token = refs[-1]
        x, y, c = _my_pos()
        me = 2 * x + y
        for l in range(nl):
            for r, (dx, dy) in enumerate(_CHIP_REL):
                for k in range(nk):
                    src, dst = ins[k].at[l], outs[k * nl + l].at[me]
                    if first:
                        hf = pl.ds(c * (shards[k].shape[1] // 2), shards[k].shape[1] // 2)
                        src, dst = src.at[hf], dst.at[hf]
                    pltpu.make_async_remote_copy(
                        src_ref=src, dst_ref=dst, send_sem=sems[2 * l].at[r * nk + k],
                        recv_sem=sems[2 * l + 1].at[r * nk + k], device_id=(x ^ dx, y ^ dy, c), device_id_type=MESH).start()
        token[...] = jnp.zeros_like(token)

    res = pl.pallas_call(
        body, name="gather_start_first" if first else "gather_start_rest",
        out_shape=lands + [pltpu.SemaphoreType.DMA((N_GATHER,))] * (2 * nl) + [_sds((8, 128), F32)],
        in_specs=[HBM] * nk, out_specs=[HBM] * (nk * nl) + [SEM] * (2 * nl) + [pl.BlockSpec(memory_space=pltpu.VMEM)],
        compiler_params=pltpu.CompilerParams(has_side_effects=EFFECT))(*[_in_hbm(s) for s in shards])
    lands = [[res[k * nl + l] for k in range(nk)] for l in range(nl)]
    sems = [(res[nk * nl + 2 * l], res[nk * nl + 2 * l + 1]) for l in range(nl)]
    return lands, sems, res[-1]


def _gather_relay(shards, lands, sems, after):
    nk = len(shards)
    half = [s.shape[1] // 2 for s in shards]

    na = len(after)

    def body(*refs):
        ins, land = refs[:nk], refs[nk:2 * nk]
        send, recv = refs[2 * nk], refs[2 * nk + 1]
        send2, recv2 = refs[3 * nk + 2 + na], refs[3 * nk + 3 + na]
        x, y, c = _my_pos()
        for r, (dx, dy) in enumerate(_CHIP_REL):
            cx, cy = x ^ dx, y ^ dy
            for k in range(nk):
                hf = pl.ds(c * half[k], half[k])
                got = land[k].at[2 * cx + cy].at[hf]
                cp = pltpu.make_async_remote_copy(
                    src_ref=ins[k].at[0].at[hf], dst_ref=got, send_sem=send.at[r * nk + k],
                    recv_sem=recv.at[r * nk + k], device_id=(cx, cy, c), device_id_type=MESH)
                cp.wait_send()
                cp.wait_recv()
                pltpu.make_async_remote_copy(
                    src_ref=got, dst_ref=got, send_sem=send2.at[r * nk + k], recv_sem=recv2.at[r * nk + k],
                    device_id=(x, y, 1 - c), device_id_type=MESH).start()

    res = pl.pallas_call(
        body, name="gather_relay",
        out_shape=[pltpu.HBM(t.shape, t.dtype) for t in lands] + [pltpu.SemaphoreType.DMA((N_GATHER,))] * 2,
        in_specs=[ANY] * nk + [HBM] * nk + [SEM, SEM] + [ANY] * na, out_specs=[HBM] * nk + [SEM, SEM],
        input_output_aliases={nk + k: k for k in range(nk)},
        compiler_params=pltpu.CompilerParams(has_side_effects=EFFECT))(*shards, *lands, *sems, *after)
    return res[:nk], (res[nk], res[nk + 1])


def _gather_wait_relay(lands, sems):
    nk = len(lands)
    half = [t.shape[1] // 2 for t in lands]

    def body(*refs):
        land = refs[:nk]
        send, recv = refs[nk], refs[nk + 1]
        x, y, c = _my_pos()
        for r, (dx, dy) in enumerate(_CHIP_REL):
            chip = 2 * (x ^ dx) + (y ^ dy)
            for k in range(nk):
                mine = land[k].at[chip].at[pl.ds(c * half[k], half[k])]
                theirs = land[k].at[chip].at[pl.ds((1 - c) * half[k], half[k])]
                cp = pltpu.make_async_remote_copy(
                    src_ref=mine, dst_ref=theirs, send_sem=send.at[r * nk + k], recv_sem=recv.at[r * nk + k],
                    device_id=(x, y, 1 - c), device_id_type=MESH)
                cp.wait_send()
                cp.wait_recv()

    return pl.pallas_call(
        body, name="gather_wait_relay", out_shape=[pltpu.HBM(t.shape, t.dtype) for t in lands],
        in_specs=[HBM] * nk + [SEM, SEM], out_specs=[HBM] * nk, input_output_aliases={k: k for k in range(nk)},
        compiler_params=pltpu.CompilerParams(has_side_effects=EFFECT))(*lands, *sems)


def _gather_wait(name, l, shards, lands, sems, after):
    nk = len(shards)

    def body(*refs):
        ins, land = refs[:nk], refs[nk:2 * nk]
        send, recv = refs[2 * nk], refs[2 * nk + 1]
        x, y, c = _my_pos()
        for r, (dx, dy) in enumerate(_CHIP_REL):
            cx, cy = x ^ dx, y ^ dy
            for k in range(nk):
                cp = pltpu.make_async_remote_copy(
                    src_ref=ins[k].at[l], dst_ref=land[k].at[2 * cx + cy], send_sem=send.at[r * nk + k],
                    recv_sem=recv.at[r * nk + k], device_id=(cx, cy, c), device_id_type=MESH)
                cp.wait_send()
                cp.wait_recv()

    return pl.pallas_call(
        body, name=name, out_shape=[pltpu.HBM(t.shape, t.dtype) for t in lands],
        in_specs=[ANY] * nk + [HBM] * nk + [SEM, SEM, ANY], out_specs=[HBM] * nk,
        input_output_aliases={nk + k: k for k in range(nk)},
        compiler_params=pltpu.CompilerParams(has_side_effects=EFFECT))(*shards, *lands, *sems, after)


def _place_own(land, shard, l, tr):
    _, rows, cols = shard.shape[0], shard.shape[-2], shard.shape[-1]
    lead = shard.shape[1:-2]
    nlead = math.prod(lead)
    sh = shard.reshape((shard.shape[0], nlead, rows, cols))
    ld = land.reshape((NCHIP, nlead, rows, cols))
    me = (2 * lax.axis_index("x") + lax.axis_index("y")).astype(jnp.int32).reshape(1)

    def body(me_ref, s_ref, l_in, o_ref):
        o_ref[...] = s_ref[...]

    out = pl.pallas_call(
        body,
        grid_spec=pltpu.PrefetchScalarGridSpec(
            num_scalar_prefetch=1, grid=(nlead, rows // tr),
            in_specs=[pl.BlockSpec((None, None, tr, cols), lambda b, i, me_ref: (l, b, i, 0)), ANY],
            out_specs=pl.BlockSpec((None, None, tr, cols), lambda b, i, me_ref: (me_ref[0], b, i, 0))),
        out_shape=_sds(ld.shape, BF16), input_output_aliases={2: 0},
        compiler_params=_cp("parallel", "parallel"), name="place_own")(me, sh, ld)
    return out.reshape(land.shape)


def _exch_start(l, parts):
    nk = len(parts)

    def body(*refs):
        ins, outs = refs[:nk], refs[nk:2 * nk]
        send, recv, token = refs[2 * nk:]
        x, y, c = _my_pos()
        for r, (dx, dy, dc) in enumerate(_DEV_REL):
            px, py, pc = x ^ dx, y ^ dy, c ^ dc
            for k in range(nk):
                pltpu.make_async_remote_copy(
                    src_ref=ins[k].at[2 * px + py, pc], dst_ref=outs[k].at[r], send_sem=send.at[r * nk + k],
                    recv_sem=recv.at[r * nk + k], device_id=(px, py, pc), device_id_type=MESH).start()
        token[...] = jnp.zeros_like(token)

    res = pl.pallas_call(
        body, name=f"exch_start_{l}",
        out_shape=[pltpu.HBM((7,) + p.shape[2:], BF16) for p in parts] + [pltpu.SemaphoreType.DMA((N_EXCH,))] * 2 + [_sds((8, 128), F32)],
        in_specs=[HBM] * nk, out_specs=[HBM] * nk + [SEM, SEM, pl.BlockSpec(memory_space=pltpu.VMEM)],
        compiler_params=pltpu.CompilerParams(has_side_effects=EFFECT))(*[_in_hbm(p) for p in parts])
    return res[:nk], (res[nk], res[nk + 1]), res[-1]


def _exch_wait(l, parts, lands, sems, after):
    nk = len(parts)

    def body(*refs):
        ins, land = refs[:nk], refs[nk:2 * nk]
        send, recv = refs[2 * nk], refs[2 * nk + 1]
        x, y, c = _my_pos()
        for r, (dx, dy, dc) in enumerate(_DEV_REL):
            px, py, pc = x ^ dx, y ^ dy, c ^ dc
            for k in range(nk):
                cp = pltpu.make_async_remote_copy(
                    src_ref=ins[k].at[2 * px + py, pc], dst_ref=land[k].at[r], send_sem=send.at[r * nk + k],
                    recv_sem=recv.at[r * nk + k], device_id=(px, py, pc), device_id_type=MESH)
                cp.wait_send()
                cp.wait_recv()

    return pl.pallas_call(
        body, name=f"exch_wait_{l}", out_shape=[pltpu.HBM(t.shape, t.dtype) for t in lands],
        in_specs=[ANY] * nk + [HBM] * nk + [SEM, SEM, ANY], out_specs=[HBM] * nk,
        input_output_aliases={nk + k: k for k in range(nk)},
        compiler_params=pltpu.CompilerParams(has_side_effects=EFFECT))(*parts, *lands, *sems, after)


def _chip_half():
    x, y, c = _my_pos()
    return jnp.stack([2 * x + y, c]).astype(jnp.int32)


def _sum_half(part, land, tr):
    _, _, r2, cols = part.shape

    def body(pos_ref, p_ref, r_ref, o_ref):
        acc = p_ref[...].astype(F32)
        for r in range(7):
            acc = acc + r_ref[r].astype(F32)
        o_ref[...] = acc

    return pl.pallas_call(
        body,
        grid_spec=pltpu.PrefetchScalarGridSpec(
            num_scalar_prefetch=1, grid=(r2 // tr,),
            in_specs=[pl.BlockSpec((None, None, tr, cols), lambda i, pos: (pos[0], pos[1], i, 0)),
                      pl.BlockSpec((7, tr, cols), lambda i, pos: (0, i, 0))],
            out_specs=pl.BlockSpec((None, tr, cols), lambda i, pos: (pos[1], i, 0))),
        out_shape=_sds((2, r2, cols), F32), compiler_params=_cp("parallel"), name="sum_half")(_chip_half(), part, land)


def _share_full(fulls):
    nk = len(fulls)

    def body(*refs):
        ins, outs = refs[:nk], refs[nk:2 * nk]
        send, recv = refs[2 * nk:]
        x, y, c = _my_pos()

        def copy(k, hf):
            return pltpu.make_async_remote_copy(
                src_ref=ins[k].at[hf], dst_ref=outs[k].at[hf], send_sem=send.at[k], recv_sem=recv.at[k],
                device_id=(x, y, 1 - c), device_id_type=MESH)

        for k in range(nk):
            copy(k, c).start()
        for k in range(nk):
            copy(k, 1 - c).wait_recv()
        for k in range(nk):
            copy(k, c).wait_send()

    return pl.pallas_call(
        body, out_shape=[_sds(f.shape, F32) for f in fulls], in_specs=[ANY] * nk, out_specs=[ANY] * nk,
        scratch_shapes=[pltpu.SemaphoreType.DMA((nk,))] * 2, input_output_aliases={k: k for k in range(nk)},
        name="share_full")(*fulls)


def _small_start(packed):
    def body(in_ref, out_ref, send, recv, token):
        x, y, c = _my_pos()
        for r, (dx, dy, dc) in enumerate(_DEV_REL):
            pltpu.make_async_remote_copy(
                src_ref=in_ref, dst_ref=out_ref.at[r], send_sem=send.at[r], recv_sem=recv.at[r],
                device_id=(x ^ dx, y ^ dy, c ^ dc), device_id_type=MESH).start()
        token[...] = jnp.zeros_like(token)

    res = pl.pallas_call(
        body, name="small_start",
        out_shape=[pltpu.HBM((7,) + packed.shape, F32)] + [pltpu.SemaphoreType.DMA((7,))] * 2 + [_sds((8, 128), F32)],
        in_specs=[HBM], out_specs=[HBM, SEM, SEM, pl.BlockSpec(memory_space=pltpu.VMEM)],
        compiler_params=pltpu.CompilerParams(has_side_effects=EFFECT))(_in_hbm(packed))
    return res[0], (res[1], res[2]), res[3]


def _small_wait(packed, land, sems, after):
    def body(in_ref, land_ref, send, recv, after_ref, out_ref):
        x, y, c = _my_pos()
        for r, (dx, dy, dc) in enumerate(_DEV_REL):
            cp = pltpu.make_async_remote_copy(
                src_ref=in_ref, dst_ref=land_ref.at[r], send_sem=send.at[r], recv_sem=recv.at[r],
                device_id=(x ^ dx, y ^ dy, c ^ dc), device_id_type=MESH)
            cp.wait_send()
            cp.wait_recv()

    return pl.pallas_call(
        body, name="small_wait", out_shape=pltpu.HBM(land.shape, land.dtype),
        in_specs=[ANY, HBM, SEM, SEM, ANY], out_specs=HBM, input_output_aliases={1: 0},
        compiler_params=pltpu.CompilerParams(has_side_effects=EFFECT))(packed, land, *sems, after)


def _sum_small(packed, land, tr):
    rows = packed.shape[0]
    x, y, c = _my_pos()
    me = (4 * x + 2 * y + c).astype(jnp.int32).reshape(1)

    def sbody(me_ref, p_ref, r_ref, o_ref):
        me_dev = me_ref[0]
        own = p_ref[...]
        acc = None
        for s in range(NDEV):
            rel = s ^ me_dev
            v = jnp.where(rel == 0, own, r_ref[jnp.maximum(rel - 1, 0)])
            acc = v if acc is None else acc + v
        o_ref[...] = acc

    return pl.pallas_call(
        sbody,
        grid_spec=pltpu.PrefetchScalarGridSpec(
            num_scalar_prefetch=1, grid=(rows // tr,),
            in_specs=[pl.BlockSpec((tr, 128), lambda i, me_ref: (i, 0)), pl.BlockSpec((7, tr, 128), lambda i, me_ref: (0, i, 0))],
            out_specs=pl.BlockSpec((tr, 128), lambda i, me_ref: (i, 0))),
        out_shape=_sds((rows, 128), F32), compiler_params=_cp("parallel"), name="sum_small")(me, packed, land)


def _rms_fwd(x, g, tm):
    n = x.shape[0]

    def body(x_ref, g_ref, h_ref):
        xv = x_ref[...]
        r = lax.rsqrt(jnp.mean(xv * xv, axis=-1, keepdims=True) + EPS)
        h_ref[...] = (xv * r * g_ref[...]).astype(BF16)

    return pl.pallas_call(
        body, grid=(n // tm,),
        in_specs=[pl.BlockSpec((tm, D), lambda i: (i, 0)), pl.BlockSpec((1, D), lambda i: (0, 0))],
        out_specs=pl.BlockSpec((tm, D), lambda i: (i, 0)), out_shape=_sds((n, D), BF16),
        compiler_params=_cp("parallel"), name="rms_fwd")(x, g)


def _rms_fwd_t(x, g, tm=512):
    n = x.shape[0]

    def body(x_ref, g_ref, h_ref, ht_ref):
        xv = x_ref[...]
        r = lax.rsqrt(jnp.mean(xv * xv, axis=-1, keepdims=True) + EPS)
        h = xv * r * g_ref[...]
        h_ref[...] = h.astype(BF16)
        ht_ref[...] = h.T.astype(BF16)

    return pl.pallas_call(
        body, grid=(n // tm,),
        in_specs=[pl.BlockSpec((tm, D), lambda i: (i, 0)), pl.BlockSpec((1, D), lambda i: (0, 0))],
        out_specs=[pl.BlockSpec((tm, D), lambda i: (i, 0)), pl.BlockSpec((D, tm), lambda i: (0, i))],
        out_shape=[_sds((n, D), BF16), _sds((D, n), BF16)], compiler_params=_cp("parallel"), name="rms_fwd_t")(x, g)


def _rms_bwd(dh, x, g, dres, tm=512):
    n = x.shape[0]

    def body(dh_ref, x_ref, g_ref, dr_ref, dx_ref, dg_ref):
        i = pl.program_id(0)
        xv = x_ref[...]
        r = lax.rsqrt(jnp.mean(xv * xv, axis=-1, keepdims=True) + EPS)
        xh = xv * r
        dhv = dh_ref[...]
        dxh = dhv * g_ref[...]
        dx_ref[...] = dr_ref[...] + r * (dxh - xh * jnp.mean(dxh * xh, axis=-1, keepdims=True))
        part = jnp.sum(dhv * xh, axis=0, keepdims=True)

        @pl.when(i == 0)
        def _():
            dg_ref[...] = part

        @pl.when(i > 0)
        def _():
            dg_ref[...] += part

    row = pl.BlockSpec((tm, D), lambda i: (i, 0))
    vec = pl.BlockSpec((1, D), lambda i: (0, 0))
    return pl.pallas_call(
        body, grid=(n // tm,), in_specs=[row, row, vec, row], out_specs=[row, vec],
        out_shape=[_sds((n, D), F32), _sds((1, D), F32)], compiler_params=_cp("arbitrary"), name="rms_bwd")(dh, x, g, dres)


def _loss_head(x, tgt, g, tm=512):
    def body(x_ref, t_ref, g_ref, dx_ref, dg_ref, ls_ref):
        i = pl.program_id(0)
        xv = x_ref[...]
        r = lax.rsqrt(jnp.mean(xv * xv, axis=-1, keepdims=True) + EPS)
        xh = xv * r
        gv = g_ref[...]
        diff = xh * gv - t_ref[...]
        dy = diff * (1.0 / D)
        dxh = dy * gv
        dx_ref[...] = r * (dxh - xh * jnp.mean(dxh * xh, axis=-1, keepdims=True))
        part_g = jnp.sum(dy * xh, axis=0, keepdims=True)
        part_l = jnp.sum(diff * diff, axis=0, keepdims=True)

        @pl.when(i == 0)
        def _():
            dg_ref[...] = part_g
            ls_ref[...] = part_l

        @pl.when(i > 0)
        def _():
            dg_ref[...] += part_g
            ls_ref[...] += part_l

        @pl.when(i == pl.num_programs(0) - 1)
        def _():
            tot = jnp.sum(ls_ref[...], axis=-1, keepdims=True) * (0.5 / D)
            ls_ref[...] = jnp.broadcast_to(tot, (1, D))

    row = pl.BlockSpec((tm, D), lambda i: (i, 0))
    vec = pl.BlockSpec((1, D), lambda i: (0, 0))
    return pl.pallas_call(
        body, grid=(S // tm,), in_specs=[row, row, vec], out_specs=[row, vec, vec],
        out_shape=[_sds((S, D), F32), _sds((1, D), F32), _sds((1, D), F32)],
        compiler_params=_cp("arbitrary"), name="loss_head")(x, tgt, g)


def _adamw(w, g, m, v):
    shape = w.shape
    cols = shape[-1] if w.ndim > 1 else shape[0]
    rows = w.size // cols
    w2, g2, m2, v2 = (t.reshape(rows, cols) for t in (w, g, m, v))
    tr = rows
    while tr * cols * 4 > (1 << 20) and tr % 16 == 0:
        tr //= 2
    c1 = 1.0 - ADAM_B1 ** ADAM_STEP
    c2 = 1.0 - ADAM_B2 ** ADAM_STEP

    def body(w_ref, g_ref, m_ref, v_ref, d_ref, nm_ref, nv_ref):
        gv = g_ref[...]
        mn = ADAM_B1 * m_ref[...] + (1.0 - ADAM_B1) * gv
        vn = ADAM_B2 * v_ref[...] + (1.0 - ADAM_B2) * (gv * gv)
        d_ref[...] = -ADAM_LR * ((mn / c1) / (jnp.sqrt(vn / c2) + ADAM_EPS) + ADAM_WD * w_ref[...])
        nm_ref[...] = mn
        nv_ref[...] = vn

    blk = pl.BlockSpec((tr, cols), lambda i: (i, 0))
    outs = pl.pallas_call(
        body, grid=(rows // tr,), in_specs=[blk] * 4, out_specs=[blk] * 3,
        out_shape=[_sds((rows, cols), F32)] * 3, compiler_params=_cp("parallel"), name="adamw")(w2, g2, m2, v2)
    return tuple(o.reshape(shape) for o in outs)


def _adamw_layer(l, w, g, m, v, outs, tr):
    cols = w.shape[-1]
    rows = w.size // (NL * cols)
    nb = rows // tr
    w2, m2, v2 = (t.reshape(NL * rows, cols) for t in (w, m, v))
    g2 = g.reshape(rows, cols)
    c1 = 1.0 - ADAM_B1 ** ADAM_STEP
    c2 = 1.0 - ADAM_B2 ** ADAM_STEP

    def body(w_ref, g_ref, m_ref, v_ref, d_in, nm_in, nv_in, go_in, d_ref, nm_ref, nv_ref, go_ref):
        gv = g_ref[...]
        mn = ADAM_B1 * m_ref[...] + (1.0 - ADAM_B1) * gv
        vn = ADAM_B2 * v_ref[...] + (1.0 - ADAM_B2) * (gv * gv)
        d_ref[...] = -ADAM_LR * ((mn / c1) / (jnp.sqrt(vn / c2) + ADAM_EPS) + ADAM_WD * w_ref[...])
        nm_ref[...] = mn
        nv_ref[...] = vn
        go_ref[...] = gv

    lay = pl.BlockSpec((tr, cols), lambda i: (l * nb + i, 0))
    return pl.pallas_call(
        body, grid=(nb,), in_specs=[lay, pl.BlockSpec((tr, cols), lambda i: (i, 0)), lay, lay] + [ANY] * 4,
        out_specs=[lay] * 4, out_shape=[_sds((NL * rows, cols), F32)] * 4,
        input_output_aliases={4: 0, 5: 1, 6: 2, 7: 3}, compiler_params=_cp("parallel"), name="adamw_layer")(w2, g2, m2, v2, *outs)


def _proj_fwd(h, wg, tm=512):
    def body(h_ref, w_ref, o_ref):
        o_ref[...] = _dot(h_ref[...], w_ref[...]).astype(BF16)

    return pl.pallas_call(
        body, grid=(NCHIP, S // tm),
        in_specs=[pl.BlockSpec((tm, D), lambda c, i: (i, 0)), pl.BlockSpec((None, D, CW), lambda c, i: (c, 0, 0))],
        out_specs=pl.BlockSpec((tm, CW), lambda c, i: (i, c)), out_shape=_sds((S, DIN), BF16),
        compiler_params=_cp("parallel", "parallel"), name="proj_fwd")(h, wg)


def _proj_bwd_x(dproj, wg, dep, tm=1024):
    def body(d_ref, w_ref, dep_ref, o_ref):
        k = pl.program_id(1)
        part = _dot_nt(d_ref[...], w_ref[...])

        @pl.when(k == 0)
        def _():
            o_ref[...] = part

        @pl.when(k > 0)
        def _():
            o_ref[...] += part

    return pl.pallas_call(
        body, grid=(S // tm, NCHIP),
        in_specs=[pl.BlockSpec((tm, CW), lambda i, k: (i, k)), pl.BlockSpec((None, D, CW), lambda i, k: (k, 0, 0)), ANY],
        out_specs=pl.BlockSpec((tm, D), lambda i, k: (i, 0)), out_shape=_sds((S, D), F32),
        compiler_params=_cp("parallel", "arbitrary"), name="proj_bwd_x")(dproj, wg, dep)


def _proj_bwd_w(ht, dproj):
    def body(h_ref, d_ref, o_ref):
        o_ref[...] = _dot(h_ref[...], d_ref[...]).astype(BF16)

    return pl.pallas_call(
        body, grid=(NCHIP, NJ),
        in_specs=[pl.BlockSpec((D, S), lambda c, j: (0, 0)), pl.BlockSpec((S, TN_IN), lambda c, j: (0, c * NJ + j))],
        out_specs=pl.BlockSpec((None, D, TN_IN), lambda c, j: (c, 0, j)), out_shape=_sds((NCHIP, D, CW), BF16),
        compiler_params=_cp("parallel", "parallel"), name="proj_bwd_w")(ht, dproj)


def _merge_fwd(y_all, wbr, proj, tm=256):
    cb = D // NCHIP

    def body(y_ref, w_ref, *rest):
        g_refs, z_ref = rest[:8], rest[8]
        for c in range(NCHIP):
            acc = None
            for b in range(4):
                g = g_refs[2 * b + c // 2][:, (c % 2) * cb:(c % 2 + 1) * cb].astype(F32)
                t = _dot(y_ref[b], w_ref[c, b]) * _sigmoid(g)
                acc = t if acc is None else acc + t
            z_ref[:, c * cb:(c + 1) * cb] = acc.astype(BF16)

    g_specs = [pl.BlockSpec((tm, W), functools.partial(lambda j, i: (i, P_GM + j), j)) for j in range(8)]
    return pl.pallas_call(
        body, grid=(S // tm,),
        in_specs=[pl.BlockSpec((4, tm, W), lambda i: (0, i, 0)), pl.BlockSpec((NCHIP, 4, W, cb), lambda i: (0, 0, 0, 0))] + g_specs,
        out_specs=pl.BlockSpec((tm, D), lambda i: (i, 0)), out_shape=_sds((S, D), BF16),
        compiler_params=_cp("parallel"), name="merge_fwd")(y_all, wbr, *([proj] * 8))


def _merge_bwd(dz, y_all, wbr, proj, dproj, tm=512):
    cb = D // NCHIP
    ni = S // tm

    def body(dz_ref, y_ref, w_ref, ga_ref, gb_ref, dp_in, dp_ref, dy_ref, dw_ref, acc, obuf, osem):
        b = pl.program_id(0)
        i = pl.program_id(1)

        @pl.when(i == 0)
        def _():
            acc[...] = jnp.zeros_like(acc)

        yv = y_ref[...]
        dys = []

        def fill(slot):
            ws_ = [w_ref[c] for c in range(NCHIP)]
            t = [_dot(yv, wv) for wv in ws_]
            dts = []
            for c in range(NCHIP):
                g_ref = ga_ref if c < 2 else gb_ref
                g = _sigmoid(g_ref[:, (c % 2) * cb:(c % 2 + 1) * cb].astype(F32))
                dzc = dz_ref[:, c * cb:(c + 1) * cb].astype(F32)
                slot[:, c * cb:(c + 1) * cb] = (dzc * t[c] * g * (1.0 - g)).astype(BF16)
                dts.append((dzc * g).astype(BF16))
            dy = None
            for c in range(NCHIP):
                part = _dot_nt(dts[c], ws_[c])
                dy = part if dy is None else dy + part
            for c in range(NCHIP):
                acc[c] += _dot_tn(yv, dts[c])
            dys.append(dy)

        _tile_put(obuf, osem, lambda st: dp_ref.at[pl.ds((st % ni) * tm, tm), pl.ds(P_GM * W + (st // ni) * D, D)],
                  b * ni + i, 4 * ni, fill)
        dy_ref[...] = dys[0].astype(BF16)

        @pl.when(i == ni - 1)
        def _():
            dw_ref[...] = acc[...].astype(BF16)

    return pl.pallas_call(
        body, grid=(4, ni),
        in_specs=[pl.BlockSpec((tm, D), lambda b, i: (i, 0)), pl.BlockSpec((None, tm, W), lambda b, i: (b, i, 0)),
                  pl.BlockSpec((NCHIP, None, W, cb), lambda b, i: (0, b, 0, 0)),
                  pl.BlockSpec((tm, W), lambda b, i: (i, P_GM + 2 * b)), pl.BlockSpec((tm, W), lambda b, i: (i, P_GM + 2 * b + 1)), ANY],
        out_specs=[ANY, pl.BlockSpec((None, tm, W), lambda b, i: (b, i, 0)), pl.BlockSpec((NCHIP, None, W, cb), lambda b, i: (0, b, 0, 0))],
        out_shape=[_sds((S, DIN), BF16), _sds((4, S, W), BF16), _sds((NCHIP, 4, W, cb), BF16)],
        scratch_shapes=[pltpu.VMEM((NCHIP, W, cb), F32), pltpu.VMEM((2, tm, D), BF16), pltpu.SemaphoreType.DMA((2,))],
        input_output_aliases={5: 0}, compiler_params=_cp("arbitrary", "arbitrary"), name="merge_bwd")(dz, y_all, wbr, proj, proj, dproj)


def _out_fwd(z, wo, x, tm=512):
    def body(z_ref, w_ref, x_ref, o_ref):
        o_ref[...] = x_ref[...] + _dot(z_ref[...], w_ref[...])

    row = pl.BlockSpec((tm, D), lambda i: (i, 0))
    return pl.pallas_call(
        body, grid=(S // tm,), in_specs=[row, pl.BlockSpec((D, D), lambda i: (0, 0)), row], out_specs=row,
        out_shape=_sds((S, D), F32), compiler_params=_cp("parallel"), name="out_fwd")(z, wo, x)


def _out_bwd(dx, z, wo, tm=512):
    ni = S // tm

    def body(dx_ref, z_ref, w_ref, dz_ref, dw_ref, acc):
        i = pl.program_id(0)
        dxb = dx_ref[...].astype(BF16)
        dz_ref[...] = _dot_nt(dxb, w_ref[...]).astype(BF16)
        part = _dot_tn(z_ref[...], dxb)

        @pl.when(i == 0)
        def _():
            acc[...] = part

        @pl.when(i > 0)
        def _():
            acc[...] += part

        @pl.when(i == ni - 1)
        def _():
            dw_ref[...] = acc[...].astype(BF16)

    row = pl.BlockSpec((tm, D), lambda i: (i, 0))
    full = pl.BlockSpec((D, D), lambda i: (0, 0))
    return pl.pallas_call(
        body, grid=(ni,), in_specs=[row, row, full], out_specs=[row, full],
        out_shape=[_sds((S, D), BF16), _sds((D, D), BF16)], scratch_shapes=[pltpu.VMEM((D, D), F32)],
        compiler_params=_cp("arbitrary"), name="out_bwd")(dx, z, wo)


def _gelu_parts(a):
    cdf = 0.5 * (1.0 + lax.erf(a * INV_SQRT2))
    return a * cdf, cdf


def _ln_parts(v):
    mu = jnp.mean(v, axis=-1, keepdims=True)
    vc = v - mu
    rs = lax.rsqrt(jnp.mean(vc * vc, axis=-1, keepdims=True) + EPS)
    return vc * rs, rs


def _causal_mask():
    return lax.broadcasted_iota(jnp.int32, (HD, HD), 0) >= lax.broadcasted_iota(jnp.int32, (HD, HD), 1)


def _gmlp_fwd(proj, lg, lb, ws, bias, y_all, tm=512):
    def body(uv_ref, gt_ref, lg_ref, lb_ref, ws_ref, b_ref, y_in, y_ref):
        act, _ = _gelu_parts(uv_ref[...].astype(F32))
        u = act[:, :W]
        xh, _ = _ln_parts(act[:, W:])
        vn = (xh * lg_ref[...] + lb_ref[...]).astype(BF16)
        gt = gt_ref[...].astype(F32)
        us = u * (gt * _sigmoid(gt))
        mask = _causal_mask()
        for h in range(4):
            wm = jnp.where(mask, ws_ref[h], 0.0).astype(BF16)
            cs = slice(h * HD, (h + 1) * HD)
            for c in range(tm // HD):
                rs_ = slice(c * HD, (c + 1) * HD)
                mixed = _dot(wm, vn[rs_, cs]) + b_ref[h]
                y_ref[rs_, cs] = (us[rs_, cs] * mixed).astype(BF16)

    vec = pl.BlockSpec((1, W), lambda i: (0, 0))
    mats = pl.BlockSpec((4, HD, HD), lambda i: (0, 0, 0))
    return pl.pallas_call(
        body, grid=(S // tm,),
        in_specs=[pl.BlockSpec((tm, 2 * W), lambda i: (i, 0)), pl.BlockSpec((tm, W), lambda i: (i, P_AGATE)), vec, vec, mats, mats, ANY],
        out_specs=pl.BlockSpec((None, tm, W), lambda i: (0, i, 0)), out_shape=_sds((4, S, W), BF16),
        input_output_aliases={6: 0}, compiler_params=_cp("parallel"), name="gmlp_fwd")(proj, proj, lg, lb, ws, bias, y_all)


def _gmlp_bwd(proj, dy_all, lg, lb, ws, bias, dproj, tm=256):
    ni = S // tm

    def body(uv_ref, gt_ref, dy_ref, lg_ref, lb_ref, ws_ref, b_ref, dp_in, dp_ref, dws_ref, dbs_ref, dlg_ref, dlb_ref, mix_s, dvn_s):
        i = pl.program_id(0)

        @pl.when(i == 0)
        def _():
            dws_ref[...] = jnp.zeros_like(dws_ref)
            dbs_ref[...] = jnp.zeros_like(dbs_ref)
            dlg_ref[...] = jnp.zeros_like(dlg_ref)
            dlb_ref[...] = jnp.zeros_like(dlb_ref)

        a0 = uv_ref[...].astype(F32)
        act, cdf = _gelu_parts(a0)
        u = act[:, :W]
        xh, rs = _ln_parts(act[:, W:])
        lgv = lg_ref[...]
        vn = (xh * lgv + lb_ref[...]).astype(BF16)
        mask = _causal_mask()
        wms = [jnp.where(mask, ws_ref[h], 0.0).astype(BF16) for h in range(4)]
        blocks = [(slice(c * HD, (c + 1) * HD), slice(h * HD, (h + 1) * HD), h) for h in range(4) for c in range(tm // HD)]
        for rs_, cs, h in blocks:
            mix_s[rs_, cs] = _dot(wms[h], vn[rs_, cs]) + b_ref[h]
        mixed = mix_s[...]
        gt = gt_ref[...].astype(F32)
        sg = _sigmoid(gt)
        sl = gt * sg
        dyv = dy_ref[...].astype(F32)
        dum = dyv * sl
        dgate = dyv * (u * mixed) * (sg * (1.0 + gt * (1.0 - sg)))
        du = dum * mixed
        dmix = dum * u
        dmb = dmix.astype(BF16)
        for rs_, cs, h in blocks:
            dvn_s[rs_, cs] = _dot_tn(wms[h], dmb[rs_, cs])
        for rs_, cs, h in blocks:
            dws_ref[h] += _dot_nt(dmb[rs_, cs], vn[rs_, cs])
            dbs_ref[h] += dmix[rs_, cs]
        dvn = dvn_s[...]
        dlg_ref[...] += jnp.sum(dvn * xh, axis=0, keepdims=True)
        dlb_ref[...] += jnp.sum(dvn, axis=0, keepdims=True)
        dxh = dvn * lgv
        dv = rs * (dxh - jnp.mean(dxh, axis=-1, keepdims=True) - xh * jnp.mean(dxh * xh, axis=-1, keepdims=True))
        gp = cdf + a0 * (jnp.exp(-0.5 * a0 * a0) * INV_SQRT2PI)
        dp_ref[:, :W] = (du * gp[:, :W]).astype(BF16)
        dp_ref[:, W:2 * W] = (dv * gp[:, W:]).astype(BF16)
        dp_ref[:, 2 * W:] = dgate.astype(BF16)

        @pl.when(i == ni - 1)
        def _():
            for h in range(4):
                dws_ref[h] = jnp.where(mask, dws_ref[h], 0.0)
                dbs_ref[h] = jnp.broadcast_to(jnp.sum(dbs_ref[h], axis=1, keepdims=True), (HD, HD))

    vec = pl.BlockSpec((1, W), lambda i: (0, 0))
    mats = pl.BlockSpec((4, HD, HD), lambda i: (0, 0, 0))
    return pl.pallas_call(
        body, grid=(ni,),
        in_specs=[pl.BlockSpec((tm, 2 * W), lambda i: (i, 0)), pl.BlockSpec((tm, W), lambda i: (i, P_AGATE)),
                  pl.BlockSpec((None, tm, W), lambda i: (0, i, 0)), vec, vec, mats, mats, ANY],
        out_specs=[pl.BlockSpec((tm, 3 * W), lambda i: (i, 0)), mats, mats, vec, vec],
        out_shape=[_sds((S, DIN), BF16), _sds((4, HD, HD), F32), _sds((4, HD, HD), F32), _sds((1, W), F32), _sds((1, W), F32)],
        scratch_shapes=[pltpu.VMEM((tm, W), F32), pltpu.VMEM((tm, W), F32)],
        input_output_aliases={7: 0}, compiler_params=_cp("arbitrary"), name="gmlp_bwd")(proj, proj, dy_all, lg, lb, ws, bias, dproj)


def _pool_diff(p, halo, row0, tm):
    xx = jnp.concatenate([halo, p], axis=0)
    t1 = (row0 + 1 + lax.broadcasted_iota(jnp.int32, (tm, 1), 0)).astype(F32)
    out = []
    for g, win in enumerate(POOL_WINDOWS):
        s = xx[:, g * HD:(g + 1) * HD]
        sh = 1
        while sh < win:
            s = s + pltpu.roll(s, sh, 0)
            sh *= 2
        out.append(s[HALO:] / jnp.minimum(t1, float(win)) - p[:, g * HD:(g + 1) * HD])
    return out


def _pool_fwd(proj, pw, sc, y_all, tm=512):
    rb = tm // HALO

    def body(p_ref, h_ref, gt_ref, pw_ref, sc_ref, y_in, y_ref):
        i = pl.program_id(0)
        halo = jnp.where(i > 0, h_ref[...].astype(F32), 0.0)
        ds = _pool_diff(p_ref[...].astype(F32), halo, i * tm, tm)
        gt = gt_ref[...].astype(F32)
        sl = gt * _sigmoid(gt)
        for g in range(4):
            cs = slice(g * HD, (g + 1) * HD)
            lin = _dot(ds[g].astype(BF16), pw_ref[g].astype(BF16))
            y_ref[:, cs] = (lin * sc_ref[:, cs] * sl[:, cs]).astype(BF16)

    return pl.pallas_call(
        body, grid=(S // tm,),
        in_specs=[pl.BlockSpec((tm, W), lambda i: (i, P_PIN)),
                  pl.BlockSpec((HALO, W), lambda i: (jnp.maximum(i * rb - 1, 0), P_PIN)),
                  pl.BlockSpec((tm, W), lambda i: (i, P_PGATE)),
                  pl.BlockSpec((4, HD, HD), lambda i: (0, 0, 0)), pl.BlockSpec((1, W), lambda i: (0, 0)), ANY],
        out_specs=pl.BlockSpec((None, tm, W), lambda i: (1, i, 0)), out_shape=_sds((4, S, W), BF16),
        input_output_aliases={5: 0}, compiler_params=_cp("parallel"), name="pool_fwd")(proj, proj, proj, pw, sc, y_all)


def _pool_bwd(proj, dy_all, pw, sc, dproj, tm=256):
    ni = S // tm
    rb = tm // HALO
    last_rb = S // HALO - 1
    rx = tm + HALO

    def body(p_ref, h_ref, gt_ref, gh_ref, dy_ref, dyh_ref, pw_ref, sc_ref, dp_in, dp_ref, dpw_ref, dsc_ref, obuf, osem):
        i = pl.program_id(0)

        @pl.when(i == 0)
        def _():
            dpw_ref[...] = jnp.zeros_like(dpw_ref)
            dsc_ref[...] = jnp.zeros_like(dsc_ref)

        halo = jnp.where(i > 0, h_ref[...].astype(F32), 0.0)
        ds = _pool_diff(p_ref[...].astype(F32), halo, i * tm, tm)
        nxt = i < ni - 1
        gx = jnp.concatenate([gt_ref[...], gh_ref[...]], axis=0).astype(F32)
        dyx = jnp.concatenate([dy_ref[...].astype(F32), jnp.where(nxt, dyh_ref[...].astype(F32), 0.0)], axis=0)
        sgx = _sigmoid(gx)
        slx = gx * sgx
        scv = sc_ref[...]
        dlinx = dyx * slx * scv
        t1 = (i * tm + 1 + lax.broadcasted_iota(jnp.int32, (rx, 1), 0)).astype(F32)
        gt, sg, sl, dyv = gx[:tm], sgx[:tm], slx[:tm], dyx[:tm]
        dsl = sg * (1.0 + gt * (1.0 - sg))

        def fill(slot):
            for g, win in enumerate(POOL_WINDOWS):
                cs = slice(g * HD, (g + 1) * HD)
                wv = pw_ref[g].astype(BF16)
                dlb = dlinx[:, cs].astype(BF16)
                ddx = _dot_nt(dlb, wv)
                f = ddx / jnp.minimum(t1, float(win))
                sh = 1
                while sh < win:
                    f = f + pltpu.roll(f, rx - sh, 0)
                    sh *= 2
                slot[:, cs] = (f[:tm] - ddx[:tm]).astype(BF16)
                db = ds[g].astype(BF16)
                lin = _dot(db, wv)
                slot[:, W + g * HD:W + (g + 1) * HD] = (dyv[:, cs] * lin * scv[:, cs] * dsl[:, cs]).astype(BF16)
                dsc_ref[:, cs] += jnp.sum(dyv[:, cs] * sl[:, cs] * lin, axis=0, keepdims=True)
                dpw_ref[g] += _dot_tn(db, dlb[:tm])

        _tile_put(obuf, osem, lambda st: dp_ref.at[pl.ds(st * tm, tm), pl.ds(P_PIN * W, 2 * W)], i, ni, fill)

    mats = pl.BlockSpec((4, HD, HD), lambda i: (0, 0, 0))
    vec = pl.BlockSpec((1, W), lambda i: (0, 0))
    return pl.pallas_call(
        body, grid=(ni,),
        in_specs=[pl.BlockSpec((tm, W), lambda i: (i, P_PIN)),
                  pl.BlockSpec((HALO, W), lambda i: (jnp.maximum(i * rb - 1, 0), P_PIN)),
                  pl.BlockSpec((tm, W), lambda i: (i, P_PGATE)),
                  pl.BlockSpec((HALO, W), lambda i: (jnp.minimum((i + 1) * rb, last_rb), P_PGATE)),
                  pl.BlockSpec((None, tm, W), lambda i: (1, i, 0)),
                  pl.BlockSpec((None, HALO, W), lambda i: (1, jnp.minimum((i + 1) * rb, last_rb), 0)),
                  mats, vec, ANY],
        out_specs=[ANY, mats, vec],
        out_shape=[_sds((S, DIN), BF16), _sds((4, HD, HD), F32), _sds((1, W), F32)],
        scratch_shapes=[pltpu.VMEM((2, tm, 2 * W), BF16), pltpu.SemaphoreType.DMA((2,))],
        input_output_aliases={8: 0}, compiler_params=_cp("arbitrary"), name="pool_bwd")(proj, proj, proj, proj, dy_all, dy_all, pw, sc, dproj)


ATT_STEP = ((1, 4), (4, 1), (4, 1))
ATT_GROUP = 16
ATT_GROUP_BWD = 8


def _att_band():
    qi = lax.broadcasted_iota(jnp.int32, (HD, 2 * HD), 0)
    kj = lax.broadcasted_iota(jnp.int32, (HD, 2 * HD), 1)
    return jnp.logical_and(kj >= qi, kj <= qi + HD), kj < HD


def _att_keys(kp_ref, ko_ref, vp_ref, vo_ref, a, jj):
    if jj == 0:
        return (jnp.concatenate([kp_ref[a], ko_ref[a, :HD, :]], axis=0), jnp.concatenate([vp_ref[a], vo_ref[a, :HD, :]], axis=0))
    return ko_ref[a, (jj - 1) * HD:(jj + 1) * HD, :], vo_ref[a, (jj - 1) * HD:(jj + 1) * HD, :]


def _dilate(src, dst, d, rows, cast=None):
    for r in range(d):
        for h in range(4):
            v = src.at[h][pl.ds(r, rows // d, stride=d), :] if d > 1 else src[h]
            dst[r * 4 + h] = v if cast is None else v.astype(cast)


def _undilate(src, dst, d, rows):
    for r in range(d):
        for h in range(4):
            if d > 1:
                dst.at[h][pl.ds(r, rows // d, stride=d), :] = src[r * 4 + h].astype(F32)
            else:
                dst[h] = src[h].astype(F32)


def _dil_spec(d, tm):
    return pl.BlockSpec((4 * d, tm // d, HD), lambda i: (0, i, 0))


def _att_prep(proj, tm=512):
    def body(q0, q1, q2, k_ref, v_ref, *rest):
        outs, scr = rest[:9], rest[9]
        for j, (src, dsts) in enumerate(((q0, ((0, outs[0]),)), (q1, ((1, outs[1]),)), (q2, ((2, outs[2]),)),
                                         (k_ref, tuple((g, outs[3 + g]) for g in range(3))),
                                         (v_ref, tuple((g, outs[6 + g]) for g in range(3))))):
            for h in range(4):
                scr[j, h] = src[:, h * HD:(h + 1) * HD].astype(F32)
            for g, dst in dsts:
                _dilate(scr.at[j], dst, DILATIONS[g], tm, BF16)

    def piece(p):
        return pl.BlockSpec((tm, W), lambda i: (i, p))

    shapes = [_sds((4 * d, S // d, HD), BF16) for d in DILATIONS]
    res = pl.pallas_call(
        body, grid=(S // tm,),
        in_specs=[piece(P_CQ), piece(P_CQ + 1), piece(P_CQ + 2), piece(P_CK), piece(P_CV)],
        out_specs=[_dil_spec(d, tm) for d in DILATIONS] * 3, out_shape=shapes * 3,
        scratch_shapes=[pltpu.VMEM((5, 4, tm, HD), F32)],
        compiler_params=_cp("parallel"), name="att_prep")(proj, proj, proj, proj, proj)
    return res[0:3], res[3:6], res[6:9]


def _att_specs(g):
    d = DILATIONS[g]
    nres, njb = ATT_STEP[g]
    nb = S // d // HD
    own = pl.BlockSpec((4 * nres, njb * HD, HD), lambda r, j: (r, j, 0))
    prev = pl.BlockSpec((4 * nres, HD, HD), lambda r, j: (r, jnp.maximum(j * njb - 1, 0), 0))
    nxt = pl.BlockSpec((4 * nres, HD, HD), lambda r, j: (r, jnp.minimum((j + 1) * njb, nb - 1), 0))
    return (d // nres, nb // njb), own, prev, nxt


def _att_fwd(q, k, v, g):
    d = DILATIONS[g]
    nres, njb = ATT_STEP[g]
    grid, own, prev, _ = _att_specs(g)

    def body(q_ref, kp_ref, ko_ref, vp_ref, vo_ref, o_ref, l_ref):
        jb = pl.program_id(1)
        band, is_prev = _att_band()
        no_prev = jnp.where(is_prev, jnp.where(jb > 0, 0.0, NEG), 0.0)
        blocks = [(a, jj) for jj in range(njb) for a in range(4 * nres)]
        for g0 in range(0, len(blocks), ATT_GROUP):
            grp = blocks[g0:g0 + ATT_GROUP]
            s, v2 = [], []
            for a, jj in grp:
                k2_, v2_ = _att_keys(kp_ref, ko_ref, vp_ref, vo_ref, a, jj)
                s_ = jnp.where(band, _dot_nt(q_ref[a, jj * HD:(jj + 1) * HD, :], k2_) * SCALE, NEG)
                s.append(s_ + no_prev if jj == 0 else s_)
                v2.append(v2_)
            m = [jnp.max(s_, axis=-1, keepdims=True) for s_ in s]
            e = [jnp.exp(s_ - m_) for s_, m_ in zip(s, m)]
            den = [jnp.sum(e_, axis=-1, keepdims=True) for e_ in e]
            inv = [1.0 / d_ for d_ in den]
            for i, (a, jj) in enumerate(grp):
                rs_ = slice(jj * HD, (jj + 1) * HD)
                o_ref[a, rs_, :] = _dot((e[i] * inv[i]).astype(BF16), v2[i]).astype(BF16)
                l_ref[a, rs_, :] = jnp.broadcast_to(m[i] + jnp.log(den[i]), (HD, HD))

    return pl.pallas_call(
        body, grid=grid, in_specs=[own, prev, own, prev, own], out_specs=[own, own],
        out_shape=[_sds((4 * d, S // d, HD), BF16), _sds((4 * d, S // d, HD), F32)],
        compiler_params=_cp("parallel", "parallel"), name="att_fwd")(q, k, k, v, v)


def _att_mix(os_, ls_, proj, y_all, tm=512):
    def body(o0, o1, o2, l0, l1, l2, gt_ref, y_in, y_ref, om_ref, lt_ref, so1, so2, sl1, sl2):
        _undilate(o1, so1, DILATIONS[1], tm)
        _undilate(o2, so2, DILATIONS[2], tm)
        _undilate(l1, sl1, DILATIONS[1], tm)
        _undilate(l2, sl2, DILATIONS[2], tm)
        for h in range(4):
            a, b, c = l0[h], sl1[h], sl2[h]
            m = jnp.maximum(jnp.maximum(a, b), c)
            ea, eb, ec = jnp.exp(a - m), jnp.exp(b - m), jnp.exp(c - m)
            z = ea + eb + ec
            inv = 1.0 / z
            o = (ea * inv) * o0[h] + (eb * inv) * so1[h] + (ec * inv) * so2[h]
            gt = gt_ref[:, h * HD:(h + 1) * HD].astype(F32)
            om_ref[h] = o
            lt_ref[h] = m + jnp.log(z)
            y_ref[:, h * HD:(h + 1) * HD] = (o * (gt * _sigmoid(gt))).astype(BF16)

    dil = [_dil_spec(d, tm) for d in DILATIONS]
    return pl.pallas_call(
        body, grid=(S // tm,),
        in_specs=dil * 2 + [pl.BlockSpec((tm, W), lambda i: (i, P_CGATE)), ANY],
        out_specs=[pl.BlockSpec((None, tm, W), lambda i: (2, i, 0)), dil[0], dil[0]],
        out_shape=[_sds((4, S, W), BF16), _sds((4, S, HD), F32), _sds((4, S, HD), F32)],
        scratch_shapes=[pltpu.VMEM((4, tm, HD), F32)] * 4,
        input_output_aliases={7: 0}, compiler_params=_cp("parallel"), name="att_mix")(*os_, *ls_, proj, y_all)


def _att_bwd_pre(dy_all, proj, om, lse, dproj, tm=512):
    def body(dy_ref, gt_ref, om_ref, ls_ref, dp_in, *rest):
        dos, dls, lss, dp_ref, sdo, sdl = rest[0:3], rest[3:6], rest[6:8], rest[8], rest[9], rest[10]
        for h in range(4):
            cs = slice(h * HD, (h + 1) * HD)
            gt = gt_ref[:, cs].astype(F32)
            sg = _sigmoid(gt)
            dyv = dy_ref[:, cs].astype(F32)
            o = om_ref[h]
            do = dyv * (gt * sg)
            dp_ref[:, cs] = (dyv * o * (sg * (1.0 + gt * (1.0 - sg)))).astype(BF16)
            sdo[h] = do
            sdl[h] = jnp.broadcast_to(jnp.sum(do * o, axis=-1, keepdims=True), (tm, HD))
        for g, d in enumerate(DILATIONS):
            _dilate(sdo, dos[g], d, tm, BF16)
            _dilate(sdl, dls[g], d, tm)
            if g > 0:
                _dilate(ls_ref, lss[g - 1], d, tm)

    dil = [_dil_spec(d, tm) for d in DILATIONS]
    gcol = pl.BlockSpec((tm, W), lambda i: (i, P_CGATE))
    res = pl.pallas_call(
        body, grid=(S // tm,),
        in_specs=[pl.BlockSpec((None, tm, W), lambda i: (2, i, 0)), gcol, dil[0], dil[0], ANY],
        out_specs=dil + dil + dil[1:] + [gcol],
        out_shape=([_sds((4 * d, S // d, HD), BF16) for d in DILATIONS] + [_sds((4 * d, S // d, HD), F32) for d in DILATIONS]
                   + [_sds((4 * d, S // d, HD), F32) for d in DILATIONS[1:]] + [_sds((S, DIN), BF16)]),
        scratch_shapes=[pltpu.VMEM((4, tm, HD), F32)] * 2,
        input_output_aliases={4: 8}, compiler_params=_cp("parallel"), name="att_bwd_pre")(dy_all, proj, om, lse, dproj)
    return res[0:3], res[3:6], [lse] + list(res[6:8]), res[8]


def _att_bwd(q, k, v, do, lse, delta, g):
    d = DILATIONS[g]
    nres, njb = ATT_STEP[g]
    grid, own, prev, nxt = _att_specs(g)

    def body(qa_ref, qn_ref, kp_ref, ko_ref, vp_ref, vo_ref, doa_ref, don_ref, la_ref, ln_ref, da_ref, dn_ref,
             dq_ref, dk_ref, dv_ref):
        jb = pl.program_id(1)
        band, is_prev = _att_band()
        m_next = lax.broadcasted_iota(jnp.int32, (HD, HD), 1) >= lax.broadcasted_iota(jnp.int32, (HD, HD), 0)
        has_prev = jnp.where(is_prev, jnp.where(jb > 0, 1.0, 0.0), 1.0)
        has_next = jnp.where(jb < grid[1] - 1, 1.0, 0.0)

        def wide(t):
            return jnp.concatenate([t, t], axis=1)

        blocks = [(a, jj) for jj in range(njb) for a in range(4 * nres)]
        for g0 in range(0, len(blocks), ATT_GROUP_BWD):
            grp = blocks[g0:g0 + ATT_GROUP_BWD]
            ops = []
            for a, jj in grp:
                rs_ = slice(jj * HD, (jj + 1) * HD)
                k2, v2 = _att_keys(kp_ref, ko_ref, vp_ref, vo_ref, a, jj)
                if jj == njb - 1:
                    qn, don, lsn, dln, fn = qn_ref[a], don_ref[a], ln_ref[a], dn_ref[a], has_next
                else:
                    ns = slice((jj + 1) * HD, (jj + 2) * HD)
                    qn, don, lsn, dln, fn = qa_ref[a, ns, :], doa_ref[a, ns, :], la_ref[a, ns, :], da_ref[a, ns, :], None
                ops.append(dict(qa=qa_ref[a, rs_, :], doa=doa_ref[a, rs_, :], lsa=wide(la_ref[a, rs_, :]),
                                dla=wide(da_ref[a, rs_, :]), k2=k2, v2=v2, ko=ko_ref[a, rs_, :], vo=vo_ref[a, rs_, :],
                                qn=qn, don=don, lsn=lsn, dln=dln, fn=fn, first=jj == 0))
            sa = [_dot_nt(o["qa"], o["k2"]) for o in ops]
            dpa = [_dot_nt(o["doa"], o["v2"]) for o in ops]
            sn = [_dot_nt(o["qn"], o["ko"]) for o in ops]
            dpn = [_dot_nt(o["don"], o["vo"]) for o in ops]
            pa, pn = [], []
            for o, sa_, sn_ in zip(ops, sa, sn):
                p_ = jnp.where(band, jnp.exp(sa_ * SCALE - o["lsa"]), 0.0)
                pa.append(p_ * has_prev if o["first"] else p_)
                p_ = jnp.where(m_next, jnp.exp(sn_ * SCALE - o["lsn"]), 0.0)
                pn.append(p_ if o["fn"] is None else p_ * o["fn"])
            dsa = [(p_ * (dp_ - o["dla"]) * SCALE).astype(BF16) for p_, dp_, o in zip(pa, dpa, ops)]
            dsn = [(p_ * (dp_ - o["dln"]) * SCALE).astype(BF16) for p_, dp_, o in zip(pn, dpn, ops)]
            for i, (a, jj) in enumerate(grp):
                rs_ = slice(jj * HD, (jj + 1) * HD)
                o = ops[i]
                dq_ref[a, rs_, :] = _dot(dsa[i], o["k2"]).astype(BF16)
                q2 = jnp.concatenate([o["qa"], o["qn"]], axis=0)
                do2 = jnp.concatenate([o["doa"], o["don"]], axis=0)
                dk_ref[a, rs_, :] = _dot_tn(jnp.concatenate([dsa[i][:, HD:], dsn[i]], axis=0), q2).astype(BF16)
                dv_ref[a, rs_, :] = _dot_tn(jnp.concatenate([pa[i][:, HD:].astype(BF16), pn[i].astype(BF16)], axis=0),
                                            do2).astype(BF16)

    return pl.pallas_call(
        body, grid=grid, in_specs=[own, nxt, prev, own, prev, own, own, nxt, own, nxt, own, nxt],
        out_specs=[own, own, own], out_shape=[_sds((4 * d, S // d, HD), BF16)] * 3,
        compiler_params=_cp("parallel", "parallel"), name="att_bwd")(q, q, k, k, v, v, do, do, lse, lse, delta, delta)


def _att_bwd_post(dqs, dks, dvs, dproj, tm=512):
    def body(*refs):
        dq, dk, dv, dp_ref, scr = refs[0:3], refs[3:6], refs[6:9], refs[10], refs[11]
        for g in range(3):
            _undilate(dq[g], scr, DILATIONS[g], tm)
            for h in range(4):
                dp_ref[:, g * W + h * HD:g * W + (h + 1) * HD] = scr[h].astype(BF16)
        for j, parts in enumerate((dk, dv)):
            acc = None
            for g in range(3):
                _undilate(parts[g], scr, DILATIONS[g], tm)
                vals = [scr[h] for h in range(4)]
                acc = vals if acc is None else [x + y for x, y in zip(acc, vals)]
            for h in range(4):
                dp_ref[:, (3 + j) * W + h * HD:(3 + j) * W + (h + 1) * HD] = acc[h].astype(BF16)

    dil = [_dil_spec(d, tm) for d in DILATIONS]
    return pl.pallas_call(
        body, grid=(S // tm,), in_specs=dil * 3 + [ANY],
        out_specs=pl.BlockSpec((tm, 5 * W), lambda i: (i, 1)), out_shape=_sds((S, DIN), BF16),
        scratch_shapes=[pltpu.VMEM((4, tm, HD), F32)],
        input_output_aliases={9: 0}, compiler_params=_cp("parallel"), name="att_bwd_post")(*dqs, *dks, *dvs, dproj)


def _mem_kv_fwd(mem_n, wkv):
    m = mem_n.shape[0]

    def body(a_ref, w_ref, o_ref):
        o_ref[...] = _dot(a_ref[...], w_ref[...])

    return pl.pallas_call(body, out_shape=_sds((m, 2 * W), F32), compiler_params=_cp(), name="mem_kv_fwd")(mem_n, wkv)


def _mem_fwd(proj, kv, y_all, tm=512):
    m = kv.shape[0]

    def body(q_ref, gt_ref, kv_ref, y_in, y_ref):
        gt = gt_ref[...].astype(F32)
        sl = gt * _sigmoid(gt)
        hs = [slice(h * HD, (h + 1) * HD) for h in range(4)]
        s = [_dot_nt(q_ref[:, cs].astype(BF16), kv_ref[:, cs].astype(BF16)) * SCALE for cs in hs]
        e = [jnp.exp(s_ - jnp.max(s_, axis=-1, keepdims=True)) for s_ in s]
        p = [(e_ * (1.0 / jnp.sum(e_, axis=-1, keepdims=True))).astype(BF16) for e_ in e]
        for h, cs in enumerate(hs):
            o = _dot(p[h], kv_ref[:, W + h * HD:W + (h + 1) * HD].astype(BF16))
            y_ref[:, cs] = (o * sl[:, cs]).astype(BF16)

    return pl.pallas_call(
        body, grid=(S // tm,),
        in_specs=[pl.BlockSpec((tm, W), lambda i: (i, P_MQ)), pl.BlockSpec((tm, W), lambda i: (i, P_MGATE)),
                  pl.BlockSpec((m, 2 * W), lambda i: (0, 0)), ANY],
        out_specs=pl.BlockSpec((None, tm, W), lambda i: (3, i, 0)), out_shape=_sds((4, S, W), BF16),
        input_output_aliases={3: 0}, compiler_params=_cp("parallel"), name="mem_fwd")(proj, proj, kv, y_all)


def _mem_bwd(proj, kv, dy_all, dproj, tm=512):
    m = kv.shape[0]
    ni = S // tm

    def body(q_ref, gt_ref, kv_ref, dy_ref, dp_in, dp_ref, dkv_ref, obuf, osem):
        i = pl.program_id(0)

        @pl.when(i == 0)
        def _():
            dkv_ref[...] = jnp.zeros_like(dkv_ref)

        gt = gt_ref[...].astype(F32)
        sg = _sigmoid(gt)
        sl = gt * sg
        dsl = sg * (1.0 + gt * (1.0 - sg))
        dyv = dy_ref[...].astype(F32)

        def fill(slot):
            hs = [slice(h * HD, (h + 1) * HD) for h in range(4)]
            vss = [slice(W + h * HD, W + (h + 1) * HD) for h in range(4)]
            q = [q_ref[:, cs].astype(BF16) for cs in hs]
            k = [kv_ref[:, cs].astype(BF16) for cs in hs]
            v = [kv_ref[:, vs].astype(BF16) for vs in vss]
            dob = [(dyv[:, cs] * sl[:, cs]).astype(BF16) for cs in hs]
            s = [_dot_nt(q_, k_) * SCALE for q_, k_ in zip(q, k)]
            dp = [_dot_nt(d_, v_) for d_, v_ in zip(dob, v)]
            e = [jnp.exp(s_ - jnp.max(s_, axis=-1, keepdims=True)) for s_ in s]
            p = [e_ * (1.0 / jnp.sum(e_, axis=-1, keepdims=True)) for e_ in e]
            pb = [p_.astype(BF16) for p_ in p]
            dsb = [(p_ * (dp_ - jnp.sum(dp_ * p_, axis=-1, keepdims=True)) * SCALE).astype(BF16) for p_, dp_ in zip(p, dp)]
            for h, (cs, vs) in enumerate(zip(hs, vss)):
                o = _dot(pb[h], v[h])
                slot[:, cs] = _dot(dsb[h], k[h]).astype(BF16)
                slot[:, vs] = (dyv[:, cs] * o * dsl[:, cs]).astype(BF16)
                dkv_ref[:, cs] += _dot_tn(dsb[h], q[h])
                dkv_ref[:, vs] += _dot_tn(pb[h], dob[h])

        _tile_put(obuf, osem, lambda st: dp_ref.at[pl.ds(st * tm, tm), pl.ds(P_MQ * W, 2 * W)], i, ni, fill)

    return pl.pallas_call(
        body, grid=(ni,),
        in_specs=[pl.BlockSpec((tm, W), lambda i: (i, P_MQ)), pl.BlockSpec((tm, W), lambda i: (i, P_MGATE)),
                  pl.BlockSpec((m, 2 * W), lambda i: (0, 0)), pl.BlockSpec((None, tm, W), lambda i: (3, i, 0)), ANY],
        out_specs=[ANY, pl.BlockSpec((m, 2 * W), lambda i: (0, 0))],
        out_shape=[_sds((S, DIN), BF16), _sds((m, 2 * W), F32)],
        scratch_shapes=[pltpu.VMEM((2, tm, 2 * W), BF16), pltpu.SemaphoreType.DMA((2,))],
        input_output_aliases={4: 0}, compiler_params=_cp("arbitrary"), name="mem_bwd")(proj, proj, kv, dy_all, dproj)


def _mem_kv_bwd(mem, g, mem_n, wkv, dkv):
    m = mem.shape[0]

    def body(x_ref, g_ref, a_ref, w_ref, d_ref, dw_ref, dg_ref):
        db = d_ref[...].astype(BF16)
        dw_ref[...] = _dot_tn(a_ref[...], db).astype(BF16)
        dn = _dot_nt(db, w_ref[...])
        xv = x_ref[...]
        xh = xv * lax.rsqrt(jnp.mean(xv * xv, axis=-1, keepdims=True) + EPS)
        dg_ref[...] = jnp.sum(dn * xh, axis=0, keepdims=True)

    return pl.pallas_call(
        body, out_shape=[_sds((D, 2 * W), BF16), _sds((1, D), F32)], compiler_params=_cp(), name="mem_kv_bwd")(mem, g, mem_n, wkv, dkv)


def _layer_fwd(x, h, ht, mem_n, p, wg):
    win, wkv, wbr, wo = wg
    proj = _proj_fwd(h, win)
    y_all = lax.empty((4, S, W), BF16)
    y_all = _gmlp_fwd(proj, p["gm_ln_g"], p["gm_ln_b"], p["gm_ws"], p["gm_bias"], y_all)
    y_all = _pool_fwd(proj, p["pool_w"], p["pool_scale"], y_all)
    qs, ks, vs = _att_prep(proj)
    os_, ls_ = zip(*[_att_fwd(qs[g], ks[g], vs[g], g) for g in range(3)])
    y_all, om, lse = _att_mix(os_, ls_, proj, y_all)
    kv = _mem_kv_fwd(mem_n, wkv)
    y_all = _mem_fwd(proj, kv, y_all)
    z = _merge_fwd(y_all, wbr, proj)
    x_new = _out_fwd(z, wo, x)
    return x_new, dict(x=x, ht=ht, proj=proj, y_all=y_all, om=om, lse=lse, mem_n=mem_n, kv=kv, z=z, qkv=(qs, ks, vs))


GRAD_PARTS = ((2, D // 2, CW), (2, D // 8, 2 * W), (2, 2 * W, D // NCHIP), (2, D // 8, D))
SUM_TILE = (64, 128, 256, 128)
ADAM_TILE = (128, 256, 2048, 256)
PLACE_TILE = (256, 256, 512, 256)


def _layer_bwd(dx, mem, p, wg, sv, exchange):
    win, wkv, wbr, wo = wg
    proj = sv["proj"]
    dz, d_wo = _out_bwd(dx, sv["z"], wo)
    dproj = lax.empty((S, DIN), BF16)
    dproj, dy_all, d_wbr = _merge_bwd(dz, sv["y_all"], wbr, proj, dproj)
    dproj, d_ws, d_bs, d_lg, d_lb = _gmlp_bwd(proj, dy_all, p["gm_ln_g"], p["gm_ln_b"], p["gm_ws"], p["gm_bias"], dproj)
    dproj, d_pw, d_sc = _pool_bwd(proj, dy_all, p["pool_w"], p["pool_scale"], dproj)
    dos, dls, lss, dproj = _att_bwd_pre(dy_all, proj, sv["om"], sv["lse"], dproj)
    qs, ks, vs = sv["qkv"]
    dqs, dks, dvs = zip(*[_att_bwd(qs[g], ks[g], vs[g], dos[g], lss[g], dls[g], g) for g in range(3)])
    dproj = _att_bwd_post(dqs, dks, dvs, dproj)
    dproj, dkv = _mem_bwd(proj, sv["kv"], dy_all, dproj)
    d_wkv, d_mg = _mem_kv_bwd(mem, p["mem_norm_g"], sv["mem_n"], wkv, dkv)
    d_win = _proj_bwd_w(sv["ht"], dproj)
    big = tuple(t.reshape((NCHIP,) + s) for t, s in zip((d_win, d_wkv, d_wbr, d_wo), GRAD_PARTS))
    inflight = exchange(big)
    dh = _proj_bwd_x(dproj, win, inflight[-1])
    dx_in, d_ng = _rms_bwd(dh, sv["x"], p["norm_g"], dx)
    small = dict(norm_g=d_ng, gm_ln_g=d_lg, gm_ln_b=d_lb, gm_ws=d_ws, gm_bs=d_bs[:, :, 0], pool_w=d_pw, pool_scale=d_sc, mem_norm_g=d_mg)
    return dx_in, (big,) + inflight, small


_SMALL = ("norm_g", "gm_ln_g", "gm_ln_b", "gm_ws", "gm_bs", "pool_w", "pool_scale", "mem_norm_g")


def _layer_params(l, norm_g, gm_ln_g, gm_ln_b, gm_ws, gm_bs, pool_w, pool_scale, mem_norm_g):
    return dict(norm_g=norm_g[l][None], gm_ln_g=gm_ln_g[l][None], gm_ln_b=gm_ln_b[l][None], gm_ws=gm_ws[l],
                gm_bias=jnp.broadcast_to(gm_bs[l][:, :, None], (4, HD, HD)), pool_w=pool_w[l],
                pool_scale=pool_scale[l][None], mem_norm_g=mem_norm_g[l][None])


def kernel(x, mem, norm_g, w_in, gm_ln_g, gm_ln_b, gm_ws, gm_bs, pool_w, pool_scale, mem_norm_g, w_mem_kv, w_branch, w_out, final_norm_g, loss_target, m_norm_g, m_w_in, m_gm_ln_g, m_gm_ln_b, m_gm_ws, m_gm_bs, m_pool_w, m_pool_scale, m_mem_norm_g, m_w_mem_kv, m_w_branch, m_w_out, m_final_norm_g, v_norm_g, v_w_in, v_gm_ln_g, v_gm_ln_b, v_gm_ws, v_gm_bs, v_pool_w, v_pool_scale, v_mem_norm_g, v_w_mem_kv, v_w_branch, v_w_out, v_final_norm_g):
    xs, memv, tgt = x[0], mem[0], loss_target[0]
    params = [_layer_params(l, norm_g, gm_ln_g, gm_ln_b, gm_ws, gm_bs, pool_w, pool_scale, mem_norm_g) for l in range(NL)]

    large = (w_in, w_mem_kv, w_branch, w_out)
    first = tuple(t[:1].astype(BF16) for t in large)
    lands, gsems, after = _gather_start(first, True)
    rest = tuple((t[1:] + after[0, 0]).astype(BF16) for t in large)
    lands_r, gsems_r, after_r = _gather_start(rest, False)
    lands, gsems = lands + lands_r, gsems + gsems_r

    saved, wgs = [], []
    mem_ns = [_rms_fwd(memv, params[l]["mem_norm_g"], memv.shape[0]) for l in range(NL)]
    for l in range(NL):
        h, ht = _rms_fwd_t(xs, params[l]["norm_g"])
        if l == 0:
            got, relay_sems = _gather_relay(first, lands[0], gsems[0], [after, after_r, h] + mem_ns)
            got = _gather_wait_relay(got, relay_sems)
            got = [_place_own(got[k], first[k], 0, PLACE_TILE[k]) for k in range(4)]
        else:
            got = _gather_wait(f"gather_wait_{l}", l - 1, rest, lands[l], gsems[l], after)
            got = [_place_own(got[k], rest[k], l - 1, PLACE_TILE[k]) for k in range(4)]
        wgs.append((got[0], got[1].reshape(D, 2 * W), got[2], got[3].reshape(D, D)))
        xs, sv = _layer_fwd(xs, h, ht, mem_ns[l], params[l], wgs[l])
        saved.append(sv)
        after = xs
    dx, d_fg, ls = _loss_head(xs, tgt, final_norm_g[None])
    loss = lax.psum(ls[0, 0], ("x", "y", "c"))

    flight, small = [None] * NL, [None] * NL
    for l in reversed(range(NL)):
        dx, flight[l], small[l] = _layer_bwd(dx, memv, params[l], wgs[l], saved[l], functools.partial(_exch_start, l))
    grad_x = dx[None]

    leaves = [jnp.stack([small[l][n] for l in range(NL)]) for n in _SMALL] + [d_fg]
    sizes = [t.size for t in leaves]
    packed = jnp.concatenate([t.reshape(-1, 128) for t in leaves], axis=0)
    rows = packed.shape[0]
    small_zone, small_sems, after = _small_start(packed)

    ws = dict(norm_g=norm_g, w_in=w_in, gm_ln_g=gm_ln_g, gm_ln_b=gm_ln_b, gm_ws=gm_ws, gm_bs=gm_bs, pool_w=pool_w,
              pool_scale=pool_scale, mem_norm_g=mem_norm_g, w_mem_kv=w_mem_kv, w_branch=w_branch, w_out=w_out,
              final_norm_g=final_norm_g)
    ms = dict(norm_g=m_norm_g, w_in=m_w_in, gm_ln_g=m_gm_ln_g, gm_ln_b=m_gm_ln_b, gm_ws=m_gm_ws, gm_bs=m_gm_bs,
              pool_w=m_pool_w, pool_scale=m_pool_scale, mem_norm_g=m_mem_norm_g, w_mem_kv=m_w_mem_kv,
              w_branch=m_w_branch, w_out=m_w_out, final_norm_g=m_final_norm_g)
    vs = dict(norm_g=v_norm_g, w_in=v_w_in, gm_ln_g=v_gm_ln_g, gm_ln_b=v_gm_ln_b, gm_ws=v_gm_ws, gm_bs=v_gm_bs,
              pool_w=v_pool_w, pool_scale=v_pool_scale, mem_norm_g=v_mem_norm_g, w_mem_kv=v_w_mem_kv,
              w_branch=v_w_branch, w_out=v_w_out, final_norm_g=v_final_norm_g)

    big = ("w_in", "w_mem_kv", "w_branch", "w_out")
    acc = {n: [lax.empty((ws[n].size // ws[n].shape[-1], ws[n].shape[-1]), F32) for _ in range(4)] for n in big}
    for l in reversed(range(NL)):
        parts, zones, sems, _ = flight[l]
        zones = _exch_wait(l, parts, zones, sems, after)
        full = _share_full([_sum_half(parts[k], zones[k], SUM_TILE[k]) for k in range(4)])
        for k, n in enumerate(big):
            acc[n] = _adamw_layer(l, ws[n], full[k], ms[n], vs[n], acc[n], ADAM_TILE[k])
        after = acc["w_in"][0]
    grads, upd = {}, {}
    for n in big:
        d_, m_, v_, g_ = (t.reshape(ws[n].shape) for t in acc[n])
        grads[n], upd[n] = g_, (d_, m_, v_)

    small_zone = _small_wait(packed, small_zone, small_sems, after)
    tot = _sum_small(packed, small_zone, max(t for t in range(8, 513, 8) if rows % t == 0))
    offs = [0]
    for sz in sizes:
        offs.append(offs[-1] + sz // 128)
    for i, n in enumerate(_SMALL + ("final_norm_g",)):
        grads[n] = tot[offs[i]:offs[i + 1]].reshape(ws[n].shape)
        upd[n] = _adamw(ws[n], grads[n], ms[n], vs[n])
    order = ("norm_g", "w_in", "gm_ln_g", "gm_ln_b", "gm_ws", "gm_bs", "pool_w", "pool_scale", "mem_norm_g", "w_mem_kv",
             "w_branch", "w_out", "final_norm_g")
    return (loss, grad_x, *[grads[n] for n in order], *[upd[n][0] for n in order], *[upd[n][1] for n in order],
            *[upd[n][2] for n in order])
```

```python
import functools
import math

import jax
import jax.numpy as jnp
from jax import lax
from jax.experimental import pallas as pl
from jax.experimental.pallas import tpu as pltpu

F32 = jnp.float32
BF16 = jnp.bfloat16

S = 4096
D = 1024
W = 512
DIN = 10752
NL = 4
NCHIP = 4
NDEV = 8
CW = DIN // NCHIP
TN_IN = 896
NJ = CW // TN_IN
HD = 128
EPS = 1e-6
NEG = -1e30
SCALE = HD ** -0.5
INV_SQRT2 = 1.0 / math.sqrt(2.0)
INV_SQRT2PI = 1.0 / math.sqrt(2.0 * math.pi)
POOL_WINDOWS = (2, 4, 8, 16)
DILATIONS = (1, 4, 16)
HALO = 16
NPIECE = DIN // W
P_AGATE, P_PIN, P_PGATE, P_CQ, P_CK, P_CV, P_CGATE, P_MQ, P_MGATE, P_GM = 2, 3, 4, 5, 8, 9, 10, 11, 12, 13
VMEM_LIMIT = 56 * 1024 * 1024

ADAM_LR, ADAM_B1, ADAM_B2, ADAM_EPS, ADAM_WD, ADAM_STEP = 0.001, 0.9, 0.999, 1e-08, 0.01, 10

MESH = pl.DeviceIdType.MESH
ANY = pl.BlockSpec(memory_space=pl.ANY)


def _cp(*sem):
    return pltpu.CompilerParams(dimension_semantics=sem or None, vmem_limit_bytes=VMEM_LIMIT)


def _sds(shape, dtype):
    return jax.ShapeDtypeStruct(shape, dtype)


def _sigmoid(v):
    return 1.0 / (1.0 + jnp.exp(-v))


def _dot(a, b):
    return jnp.dot(a, b, preferred_element_type=F32)


def _dot_nt(a, b):
    return lax.dot_general(a, b, (((1,), (1,)), ((), ())), preferred_element_type=F32)


def _dot_tn(a, b):
    return lax.dot_general(a, b, (((0,), (0,)), ((), ())), preferred_element_type=F32)


def _tile_put(buf, sem, dst_of, step, nsteps, fill):
    slot = step % 2

    def copy(s, st):
        return pltpu.make_async_copy(buf.at[s], dst_of(st), sem.at[s])

    @pl.when(step >= 2)
    def _():
        copy(slot, step).wait()

    fill(buf.at[slot])
    copy(slot, step).start()

    @pl.when(step == nsteps - 1)
    def _():
        if nsteps >= 2:
            copy(1 - slot, step).wait()
        copy(slot, step).wait()


def _my_pos():
    return lax.axis_index("x"), lax.axis_index("y"), lax.axis_index("c")


_CHIP_REL = ((1, 0), (0, 1), (1, 1))
_DEV_REL = tuple((dx, dy, dc) for dx in (0, 1) for dy in (0, 1) for dc in (0, 1))[1:]


HBM = pl.BlockSpec(memory_space=pltpu.HBM)
SEM = pl.BlockSpec(memory_space=pltpu.SEMAPHORE)
EFFECT = pltpu.SideEffectType.DATAFLOW_SIDE_EFFECTING
N_GATHER = 3 * 4
N_EXCH = 7 * 4


def _in_hbm(t):
    return pltpu.with_memory_space_constraint(t, pltpu.HBM)


def _gather_start(shards, first):
    nk = len(shards)
    nl = shards[0].shape[0]
    lands = [pltpu.HBM((NCHIP,) + s.shape[1:], BF16) for s in shards for _ in range(nl)]

    def body(*refs):
        ins, outs = refs[:nk], refs[nk:nk + nk * nl]
        sems = refs[nk + nk * nl:nk + nk * nl + 2 * nl]
        token = refs[-1]
        x, y, c = _my_pos()
        me = 2 * x + y
        for l in range(nl):
            for r, (dx, dy) in enumerate(_CHIP_REL):
                for k in range(nk):
                    src, dst = ins[k].at[l], outs[k * nl + l].at[me]
                    if first:
                        hf = pl.ds(c * (shards[k].shape[1] // 2), shards[k].shape[1] // 2)
                        src, dst = src.at[hf], dst.at[hf]
                    pltpu.make_async_remote_copy(
                        src_ref=src, dst_ref=dst, send_sem=sems[2 * l].at[r * nk + k],
                        recv_sem=sems[2 * l + 1].at[r * nk + k], device_id=(x ^ dx, y ^ dy, c), device_id_type=MESH).start()
        token[...] = jnp.zeros_like(token)

    res = pl.pallas_call(
        body, name="gather_start_first" if first else "gather_start_rest",
        out_shape=lands + [pltpu.SemaphoreType.DMA((N_GATHER,))] * (2 * nl) + [_sds((8, 128), F32)],
        in_specs=[HBM] * nk, out_specs=[HBM] * (nk * nl) + [SEM] * (2 * nl) + [pl.BlockSpec(memory_space=pltpu.VMEM)],
        compiler_params=pltpu.CompilerParams(has_side_effects=EFFECT))(*[_in_hbm(s) for s in shards])
    lands = [[res[k * nl + l] for k in range(nk)] for l in range(nl)]
    sems = [(res[nk * nl + 2 * l], res[nk * nl + 2 * l + 1]) for l in range(nl)]
    return lands, sems, res[-1]


def _gather_relay(shards, lands, sems, after):
    nk = len(shards)
    half = [s.shape[1] // 2 for s in shards]

    na = len(after)

    def body(*refs):
        ins, land = refs[:nk], refs[nk:2 * nk]
        send, recv = refs[2 * nk], refs[2 * nk + 1]
        send2, recv2 = refs[3 * nk + 2 + na], refs[3 * nk + 3 + na]
        x, y, c = _my_pos()
        for r, (dx, dy) in enumerate(_CHIP_REL):
            cx, cy = x ^ dx, y ^ dy
            for k in range(nk):
                hf = pl.ds(c * half[k], half[k])
                got = land[k].at[2 * cx + cy].at[hf]
                cp = pltpu.make_async_remote_copy(
                    src_ref=ins[k].at[0].at[hf], dst_ref=got, send_sem=send.at[r * nk + k],
                    recv_sem=recv.at[r * nk + k], device_id=(cx, cy, c), device_id_type=MESH)
                cp.wait_send()
                cp.wait_recv()
                pltpu.make_async_remote_copy(
                    src_ref=got, dst_ref=got, send_sem=send2.at[r * nk + k], recv_sem=recv2.at[r * nk + k],
                    device_id=(x, y, 1 - c), device_id_type=MESH).start()

    res = pl.pallas_call(
        body, name="gather_relay",
        out_shape=[pltpu.HBM(t.shape, t.dtype) for t in lands] + [pltpu.SemaphoreType.DMA((N_GATHER,))] * 2,
        in_specs=[ANY] * nk + [HBM] * nk + [SEM, SEM] + [ANY] * na, out_specs=[HBM] * nk + [SEM, SEM],
        input_output_aliases={nk + k: k for k in range(nk)},
        compiler_params=pltpu.CompilerParams(has_side_effects=EFFECT))(*shards, *lands, *sems, *after)
    return res[:nk], (res[nk], res[nk + 1])


def _gather_wait_relay(lands, sems):
    nk = len(lands)
    half = [t.shape[1] // 2 for t in lands]

    def body(*refs):
        land = refs[:nk]
        send, recv = refs[nk], refs[nk + 1]
        x, y, c = _my_pos()
        for r, (dx, dy) in enumerate(_CHIP_REL):
            chip = 2 * (x ^ dx) + (y ^ dy)
            for k in range(nk):
                mine = land[k].at[chip].at[pl.ds(c * half[k], half[k])]
                theirs = land[k].at[chip].at[pl.ds((1 - c) * half[k], half[k])]
                cp = pltpu.make_async_remote_copy(
                    src_ref=mine, dst_ref=theirs, send_sem=send.at[r * nk + k], recv_sem=recv.at[r * nk + k],
                    device_id=(x, y, 1 - c), device_id_type=MESH)
                cp.wait_send()
                cp.wait_recv()

    return pl.pallas_call(
        body, name="gather_wait_relay", out_shape=[pltpu.HBM(t.shape, t.dtype) for t in lands],
        in_specs=[HBM] * nk + [SEM, SEM], out_specs=[HBM] * nk, input_output_aliases={k: k for k in range(nk)},
        compiler_params=pltpu.CompilerParams(has_side_effects=EFFECT))(*lands, *sems)


def _gather_wait(name, l, shards, lands, sems, after):
    nk = len(shards)

    def body(*refs):
        ins, land = refs[:nk], refs[nk:2 * nk]
        send, recv = refs[2 * nk], refs[2 * nk + 1]
        x, y, c = _my_pos()
        for r, (dx, dy) in enumerate(_CHIP_REL):
            cx, cy = x ^ dx, y ^ dy
            for k in range(nk):
                cp = pltpu.make_async_remote_copy(
                    src_ref=ins[k].at[l], dst_ref=land[k].at[2 * cx + cy], send_sem=send.at[r * nk + k],
                    recv_sem=recv.at[r * nk + k], device_id=(cx, cy, c), device_id_type=MESH)
                cp.wait_send()
                cp.wait_recv()

    return pl.pallas_call(
        body, name=name, out_shape=[pltpu.HBM(t.shape, t.dtype) for t in lands],
        in_specs=[ANY] * nk + [HBM] * nk + [SEM, SEM, ANY], out_specs=[HBM] * nk,
        input_output_aliases={nk + k: k for k in range(nk)},
        compiler_params=pltpu.CompilerParams(has_side_effects=EFFECT))(*shards, *lands, *sems, after)


def _place_own(land, shard, l, tr):
    _, rows, cols = shard.shape[0], shard.shape[-2], shard.shape[-1]
    lead = shard.shape[1:-2]
    nlead = math.prod(lead)
    sh = shard.reshape((shard.shape[0], nlead, rows, cols))
    ld = land.reshape((NCHIP, nlead, rows, cols))
    me = (2 * lax.axis_index("x") + lax.axis_index("y")).astype(jnp.int32).reshape(1)

    def body(me_ref, s_ref, l_in, o_ref):
        o_ref[...] = s_ref[...]

    out = pl.pallas_call(
        body,
        grid_spec=pltpu.PrefetchScalarGridSpec(
            num_scalar_prefetch=1, grid=(nlead, rows // tr),
            in_specs=[pl.BlockSpec((None, None, tr, cols), lambda b, i, me_ref: (l, b, i, 0)), ANY],
            out_specs=pl.BlockSpec((None, None, tr, cols), lambda b, i, me_ref: (me_ref[0], b, i, 0))),
        out_shape=_sds(ld.shape, BF16), input_output_aliases={2: 0},
        compiler_params=_cp("parallel", "parallel"), name="place_own")(me, sh, ld)
    return out.reshape(land.shape)


def _exch_start(l, parts):
    nk = len(parts)

    def body(*refs):
        ins, outs = refs[:nk], refs[nk:2 * nk]
        send, recv, token = refs[2 * nk:]
        x, y, c = _my_pos()
        for r, (dx, dy, dc) in enumerate(_DEV_REL):
            px, py, pc = x ^ dx, y ^ dy, c ^ dc
            for k in range(nk):
                pltpu.make_async_remote_copy(
                    src_ref=ins[k].at[2 * px + py, pc], dst_ref=outs[k].at[r], send_sem=send.at[r * nk + k],
                    recv_sem=recv.at[r * nk + k], device_id=(px, py, pc), device_id_type=MESH).start()
        token[...] = jnp.zeros_like(token)

    res = pl.pallas_call(
        body, name=f"exch_start_{l}",
        out_shape=[pltpu.HBM((7,) + p.shape[2:], BF16) for p in parts] + [pltpu.SemaphoreType.DMA((N_EXCH,))] * 2 + [_sds((8, 128), F32)],
        in_specs=[HBM] * nk, out_specs=[HBM] * nk + [SEM, SEM, pl.BlockSpec(memory_space=pltpu.VMEM)],
        compiler_params=pltpu.CompilerParams(has_side_effects=EFFECT))(*[_in_hbm(p) for p in parts])
    return res[:nk], (res[nk], res[nk + 1]), res[-1]


def _exch_wait(l, parts, lands, sems, after):
    nk = len(parts)

    def body(*refs):
        ins, land = refs[:nk], refs[nk:2 * nk]
        send, recv = refs[2 * nk], refs[2 * nk + 1]
        x, y, c = _my_pos()
        for r, (dx, dy, dc) in enumerate(_DEV_REL):
            px, py, pc = x ^ dx, y ^ dy, c ^ dc
            for k in range(nk):
                cp = pltpu.make_async_remote_copy(
                    src_ref=ins[k].at[2 * px + py, pc], dst_ref=land[k].at[r], send_sem=send.at[r * nk + k],
                    recv_sem=recv.at[r * nk + k], device_id=(px, py, pc), device_id_type=MESH)
                cp.wait_send()
                cp.wait_recv()

    return pl.pallas_call(
        body, name=f"exch_wait_{l}", out_shape=[pltpu.HBM(t.shape, t.dtype) for t in lands],
        in_specs=[ANY] * nk + [HBM] * nk + [SEM, SEM, ANY], out_specs=[HBM] * nk,
        input_output_aliases={nk + k: k for k in range(nk)},
        compiler_params=pltpu.CompilerParams(has_side_effects=EFFECT))(*parts, *lands, *sems, after)


def _chip_half():
    x, y, c = _my_pos()
    return jnp.stack([2 * x + y, c]).astype(jnp.int32)


def _sum_half(part, land, tr):
    _, _, r2, cols = part.shape

    def body(pos_ref, p_ref, r_ref, o_ref):
        acc = p_ref[...].astype(F32)
        for r in range(7):
            acc = acc + r_ref[r].astype(F32)
        o_ref[...] = acc

    return pl.pallas_call(
        body,
        grid_spec=pltpu.PrefetchScalarGridSpec(
            num_scalar_prefetch=1, grid=(r2 // tr,),
            in_specs=[pl.BlockSpec((None, None, tr, cols), lambda i, pos: (pos[0], pos[1], i, 0)),
                      pl.BlockSpec((7, tr, cols), lambda i, pos: (0, i, 0))],
            out_specs=pl.BlockSpec((None, tr, cols), lambda i, pos: (pos[1], i, 0))),
        out_shape=_sds((2, r2, cols), F32), compiler_params=_cp("parallel"), name="sum_half")(_chip_half(), part, land)


def _share_full(fulls):
    nk = len(fulls)

    def body(*refs):
        ins, outs = refs[:nk], refs[nk:2 * nk]
        send, recv = refs[2 * nk:]
        x, y, c = _my_pos()

        def copy(k, hf):
            return pltpu.make_async_remote_copy(
                src_ref=ins[k].at[hf], dst_ref=outs[k].at[hf], send_sem=send.at[k], recv_sem=recv.at[k],
                device_id=(x, y, 1 - c), device_id_type=MESH)

        for k in range(nk):
            copy(k, c).start()
        for k in range(nk):
            copy(k, 1 - c).wait_recv()
        for k in range(nk):
            copy(k, c).wait_send()

    return pl.pallas_call(
        body, out_shape=[_sds(f.shape, F32) for f in fulls], in_specs=[ANY] * nk, out_specs=[ANY] * nk,
        scratch_shapes=[pltpu.SemaphoreType.DMA((nk,))] * 2, input_output_aliases={k: k for k in range(nk)},
        name="share_full")(*fulls)


def _small_start(packed, name):
    def body(in_ref, out_ref, send, recv, token):
        x, y, c = _my_pos()
        for r, (dx, dy, dc) in enumerate(_DEV_REL):
            pltpu.make_async_remote_copy(
                src_ref=in_ref, dst_ref=out_ref.at[r], send_sem=send.at[r], recv_sem=recv.at[r],
                device_id=(x ^ dx, y ^ dy, c ^ dc), device_id_type=MESH).start()
        token[...] = jnp.zeros_like(token)

    res = pl.pallas_call(
        body, name=name,
        out_shape=[pltpu.HBM((7,) + packed.shape, F32)] + [pltpu.SemaphoreType.DMA((7,))] * 2 + [_sds((8, 128), F32)],
        in_specs=[HBM], out_specs=[HBM, SEM, SEM, pl.BlockSpec(memory_space=pltpu.VMEM)],
        compiler_params=pltpu.CompilerParams(has_side_effects=EFFECT))(_in_hbm(packed))
    return res[0], (res[1], res[2]), res[3]


def _small_wait(packed, land, sems, after, name):
    def body(in_ref, land_ref, send, recv, after_ref, out_ref):
        x, y, c = _my_pos()
        for r, (dx, dy, dc) in enumerate(_DEV_REL):
            cp = pltpu.make_async_remote_copy(
                src_ref=in_ref, dst_ref=land_ref.at[r], send_sem=send.at[r], recv_sem=recv.at[r],
                device_id=(x ^ dx, y ^ dy, c ^ dc), device_id_type=MESH)
            cp.wait_send()
            cp.wait_recv()

    return pl.pallas_call(
        body, name=name, out_shape=pltpu.HBM(land.shape, land.dtype),
        in_specs=[ANY, HBM, SEM, SEM, ANY], out_specs=HBM, input_output_aliases={1: 0},
        compiler_params=pltpu.CompilerParams(has_side_effects=EFFECT))(packed, land, *sems, after)


def _sum_small(packed, land, tr):
    rows = packed.shape[0]
    x, y, c = _my_pos()
    me = (4 * x + 2 * y + c).astype(jnp.int32).reshape(1)

    def sbody(me_ref, p_ref, r_ref, o_ref):
        me_dev = me_ref[0]
        own = p_ref[...]
        acc = None
        for s in range(NDEV):
            rel = s ^ me_dev
            v = jnp.where(rel == 0, own, r_ref[jnp.maximum(rel - 1, 0)])
            acc = v if acc is None else acc + v
        o_ref[...] = acc

    return pl.pallas_call(
        sbody,
        grid_spec=pltpu.PrefetchScalarGridSpec(
            num_scalar_prefetch=1, grid=(rows // tr,),
            in_specs=[pl.BlockSpec((tr, 128), lambda i, me_ref: (i, 0)), pl.BlockSpec((7, tr, 128), lambda i, me_ref: (0, i, 0))],
            out_specs=pl.BlockSpec((tr, 128), lambda i, me_ref: (i, 0))),
        out_shape=_sds((rows, 128), F32), compiler_params=_cp("parallel"), name="sum_small")(me, packed, land)


def _rms_fwd(x, g, tm):
    n = x.shape[0]

    def body(x_ref, g_ref, h_ref):
        xv = x_ref[...]
        r = lax.rsqrt(jnp.mean(xv * xv, axis=-1, keepdims=True) + EPS)
        h_ref[...] = (xv * r * g_ref[...]).astype(BF16)

    return pl.pallas_call(
        body, grid=(n // tm,),
        in_specs=[pl.BlockSpec((tm, D), lambda i: (i, 0)), pl.BlockSpec((1, D), lambda i: (0, 0))],
        out_specs=pl.BlockSpec((tm, D), lambda i: (i, 0)), out_shape=_sds((n, D), BF16),
        compiler_params=_cp("parallel"), name="rms_fwd")(x, g)


def _rms_fwd_t(x, g, tm=512):
    n = x.shape[0]

    def body(x_ref, g_ref, h_ref, ht_ref):
        xv = x_ref[...]
        r = lax.rsqrt(jnp.mean(xv * xv, axis=-1, keepdims=True) + EPS)
        h = xv * r * g_ref[...]
        h_ref[...] = h.astype(BF16)
        ht_ref[...] = h.T.astype(BF16)

    return pl.pallas_call(
        body, grid=(n // tm,),
        in_specs=[pl.BlockSpec((tm, D), lambda i: (i, 0)), pl.BlockSpec((1, D), lambda i: (0, 0))],
        out_specs=[pl.BlockSpec((tm, D), lambda i: (i, 0)), pl.BlockSpec((D, tm), lambda i: (0, i))],
        out_shape=[_sds((n, D), BF16), _sds((D, n), BF16)], compiler_params=_cp("parallel"), name="rms_fwd_t")(x, g)


def _rms_bwd(dh, x, g, dres, tm=512):
    n = x.shape[0]

    def body(dh_ref, x_ref, g_ref, dr_ref, dx_ref, dg_ref):
        i = pl.program_id(0)
        xv = x_ref[...]
        r = lax.rsqrt(jnp.mean(xv * xv, axis=-1, keepdims=True) + EPS)
        xh = xv * r
        dhv = dh_ref[...]
        dxh = dhv * g_ref[...]
        dx_ref[...] = dr_ref[...] + r * (dxh - xh * jnp.mean(dxh * xh, axis=-1, keepdims=True))
        part = jnp.sum(dhv * xh, axis=0, keepdims=True)

        @pl.when(i == 0)
        def _():
            dg_ref[...] = part

        @pl.when(i > 0)
        def _():
            dg_ref[...] += part

    row = pl.BlockSpec((tm, D), lambda i: (i, 0))
    vec = pl.BlockSpec((1, D), lambda i: (0, 0))
    return pl.pallas_call(
        body, grid=(n // tm,), in_specs=[row, row, vec, row], out_specs=[row, vec],
        out_shape=[_sds((n, D), F32), _sds((1, D), F32)], compiler_params=_cp("arbitrary"), name="rms_bwd")(dh, x, g, dres)


def _loss_head(x, tgt, g, tm=512):
    def body(x_ref, t_ref, g_ref, dx_ref, dg_ref, ls_ref):
        i = pl.program_id(0)
        xv = x_ref[...]
        r = lax.rsqrt(jnp.mean(xv * xv, axis=-1, keepdims=True) + EPS)
        xh = xv * r
        gv = g_ref[...]
        diff = xh * gv - t_ref[...]
        dy = diff * (1.0 / D)
        dxh = dy * gv
        dx_ref[...] = r * (dxh - xh * jnp.mean(dxh * xh, axis=-1, keepdims=True))
        part_g = jnp.sum(dy * xh, axis=0, keepdims=True)
        part_l = jnp.sum(diff * diff, axis=0, keepdims=True)

        @pl.when(i == 0)
        def _():
            dg_ref[...] = part_g
            ls_ref[...] = part_l

        @pl.when(i > 0)
        def _():
            dg_ref[...] += part_g
            ls_ref[...] += part_l

        @pl.when(i == pl.num_programs(0) - 1)
        def _():
            tot = jnp.sum(ls_ref[...], axis=-1, keepdims=True) * (0.5 / D)
            ls_ref[...] = jnp.broadcast_to(tot, (1, D))

    row = pl.BlockSpec((tm, D), lambda i: (i, 0))
    vec = pl.BlockSpec((1, D), lambda i: (0, 0))
    return pl.pallas_call(
        body, grid=(S // tm,), in_specs=[row, row, vec], out_specs=[row, vec, vec],
        out_shape=[_sds((S, D), F32), _sds((1, D), F32), _sds((1, D), F32)],
        compiler_params=_cp("arbitrary"), name="loss_head")(x, tgt, g)


def _adamw(w, g, m, v):
    shape = w.shape
    cols = shape[-1] if w.ndim > 1 else shape[0]
    rows = w.size // cols
    w2, g2, m2, v2 = (t.reshape(rows, cols) for t in (w, g, m, v))
    tr = rows
    while tr * cols * 4 > (1 << 20) and tr % 16 == 0:
        tr //= 2
    c1 = 1.0 - ADAM_B1 ** ADAM_STEP
    c2 = 1.0 - ADAM_B2 ** ADAM_STEP

    def body(w_ref, g_ref, m_ref, v_ref, d_ref, nm_ref, nv_ref):
        gv = g_ref[...]
        mn = ADAM_B1 * m_ref[...] + (1.0 - ADAM_B1) * gv
        vn = ADAM_B2 * v_ref[...] + (1.0 - ADAM_B2) * (gv * gv)
        d_ref[...] = -ADAM_LR * ((mn / c1) / (jnp.sqrt(vn / c2) + ADAM_EPS) + ADAM_WD * w_ref[...])
        nm_ref[...] = mn
        nv_ref[...] = vn

    blk = pl.BlockSpec((tr, cols), lambda i: (i, 0))
    outs = pl.pallas_call(
        body, grid=(rows // tr,), in_specs=[blk] * 4, out_specs=[blk] * 3,
        out_shape=[_sds((rows, cols), F32)] * 3, compiler_params=_cp("parallel"), name="adamw")(w2, g2, m2, v2)
    return tuple(o.reshape(shape) for o in outs)


def _adamw_layer(l, w, g, m, v, outs, tr):
    cols = w.shape[-1]
    rows = w.size // (NL * cols)
    nb = rows // tr
    w2, m2, v2 = (t.reshape(NL * rows, cols) for t in (w, m, v))
    g2 = g.reshape(rows, cols)
    c1 = 1.0 - ADAM_B1 ** ADAM_STEP
    c2 = 1.0 - ADAM_B2 ** ADAM_STEP

    def body(w_ref, g_ref, m_ref, v_ref, d_in, nm_in, nv_in, go_in, d_ref, nm_ref, nv_ref, go_ref):
        gv = g_ref[...]
        mn = ADAM_B1 * m_ref[...] + (1.0 - ADAM_B1) * gv
        vn = ADAM_B2 * v_ref[...] + (1.0 - ADAM_B2) * (gv * gv)
        d_ref[...] = -ADAM_LR * ((mn / c1) / (jnp.sqrt(vn / c2) + ADAM_EPS) + ADAM_WD * w_ref[...])
        nm_ref[...] = mn
        nv_ref[...] = vn
        go_ref[...] = gv

    lay = pl.BlockSpec((tr, cols), lambda i: (l * nb + i, 0))
    return pl.pallas_call(
        body, grid=(nb,), in_specs=[lay, pl.BlockSpec((tr, cols), lambda i: (i, 0)), lay, lay] + [ANY] * 4,
        out_specs=[lay] * 4, out_shape=[_sds((NL * rows, cols), F32)] * 4,
        input_output_aliases={4: 0, 5: 1, 6: 2, 7: 3}, compiler_params=_cp("parallel"), name="adamw_layer")(w2, g2, m2, v2, *outs)


def _proj_fwd(h, wg, tm=512):
    def body(h_ref, w_ref, o_ref):
        o_ref[...] = _dot(h_ref[...], w_ref[...]).astype(BF16)

    return pl.pallas_call(
        body, grid=(NCHIP, S // tm),
        in_specs=[pl.BlockSpec((tm, D), lambda c, i: (i, 0)), pl.BlockSpec((None, D, CW), lambda c, i: (c, 0, 0))],
        out_specs=pl.BlockSpec((tm, CW), lambda c, i: (i, c)), out_shape=_sds((S, DIN), BF16),
        compiler_params=_cp("parallel", "parallel"), name="proj_fwd")(h, wg)


def _proj_bwd_x(dproj, wg, dep, tm=1024):
    def body(d_ref, w_ref, dep_ref, o_ref):
        k = pl.program_id(1)
        part = _dot_nt(d_ref[...], w_ref[...])

        @pl.when(k == 0)
        def _():
            o_ref[...] = part

        @pl.when(k > 0)
        def _():
            o_ref[...] += part

    return pl.pallas_call(
        body, grid=(S // tm, NCHIP),
        in_specs=[pl.BlockSpec((tm, CW), lambda i, k: (i, k)), pl.BlockSpec((None, D, CW), lambda i, k: (k, 0, 0)), ANY],
        out_specs=pl.BlockSpec((tm, D), lambda i, k: (i, 0)), out_shape=_sds((S, D), F32),
        compiler_params=_cp("parallel", "arbitrary"), name="proj_bwd_x")(dproj, wg, dep)


def _proj_bwd_w(ht, dproj):
    def body(h_ref, d_ref, o_ref):
        o_ref[...] = _dot(h_ref[...], d_ref[...]).astype(BF16)

    return pl.pallas_call(
        body, grid=(NCHIP, NJ),
        in_specs=[pl.BlockSpec((D, S), lambda c, j: (0, 0)), pl.BlockSpec((S, TN_IN), lambda c, j: (0, c * NJ + j))],
        out_specs=pl.BlockSpec((None, D, TN_IN), lambda c, j: (c, 0, j)), out_shape=_sds((NCHIP, D, CW), BF16),
        compiler_params=_cp("parallel", "parallel"), name="proj_bwd_w")(ht, dproj)


def _merge_fwd(y_all, wbr, proj, tm=256):
    cb = D // NCHIP

    def body(y_ref, w_ref, *rest):
        g_refs, z_ref = rest[:8], rest[8]
        for c in range(NCHIP):
            acc = None
            for b in range(4):
                g = g_refs[2 * b + c // 2][:, (c % 2) * cb:(c % 2 + 1) * cb].astype(F32)
                t = _dot(y_ref[b], w_ref[c, b]) * _sigmoid(g)
                acc = t if acc is None else acc + t
            z_ref[:, c * cb:(c + 1) * cb] = acc.astype(BF16)

    g_specs = [pl.BlockSpec((tm, W), functools.partial(lambda j, i: (i, P_GM + j), j)) for j in range(8)]
    return pl.pallas_call(
        body, grid=(S // tm,),
        in_specs=[pl.BlockSpec((4, tm, W), lambda i: (0, i, 0)), pl.BlockSpec((NCHIP, 4, W, cb), lambda i: (0, 0, 0, 0))] + g_specs,
        out_specs=pl.BlockSpec((tm, D), lambda i: (i, 0)), out_shape=_sds((S, D), BF16),
        compiler_params=_cp("parallel"), name="merge_fwd")(y_all, wbr, *([proj] * 8))


def _merge_bwd(dz, y_all, wbr, proj, dproj, tm=512):
    cb = D // NCHIP
    ni = S // tm

    def body(dz_ref, y_ref, w_ref, ga_ref, gb_ref, dp_in, dp_ref, dy_ref, dw_ref, acc, obuf, osem):
        b = pl.program_id(0)
        i = pl.program_id(1)

        @pl.when(i == 0)
        def _():
            acc[...] = jnp.zeros_like(acc)

        yv = y_ref[...]
        dys = []

        def fill(slot):
            ws_ = [w_ref[c] for c in range(NCHIP)]
            t = [_dot(yv, wv) for wv in ws_]
            dts = []
            for c in range(NCHIP):
                g_ref = ga_ref if c < 2 else gb_ref
                g = _sigmoid(g_ref[:, (c % 2) * cb:(c % 2 + 1) * cb].astype(F32))
                dzc = dz_ref[:, c * cb:(c + 1) * cb].astype(F32)
                slot[:, c * cb:(c + 1) * cb] = (dzc * t[c] * g * (1.0 - g)).astype(BF16)
                dts.append((dzc * g).astype(BF16))
            dy = None
            for c in range(NCHIP):
                part = _dot_nt(dts[c], ws_[c])
                dy = part if dy is None else dy + part
            for c in range(NCHIP):
                acc[c] += _dot_tn(yv, dts[c])
            dys.append(dy)

        _tile_put(obuf, osem, lambda st: dp_ref.at[pl.ds((st % ni) * tm, tm), pl.ds(P_GM * W + (st // ni) * D, D)],
                  b * ni + i, 4 * ni, fill)
        dy_ref[...] = dys[0].astype(BF16)

        @pl.when(i == ni - 1)
        def _():
            dw_ref[...] = acc[...].astype(BF16)

    return pl.pallas_call(
        body, grid=(4, ni),
        in_specs=[pl.BlockSpec((tm, D), lambda b, i: (i, 0)), pl.BlockSpec((None, tm, W), lambda b, i: (b, i, 0)),
                  pl.BlockSpec((NCHIP, None, W, cb), lambda b, i: (0, b, 0, 0)),
                  pl.BlockSpec((tm, W), lambda b, i: (i, P_GM + 2 * b)), pl.BlockSpec((tm, W), lambda b, i: (i, P_GM + 2 * b + 1)), ANY],
        out_specs=[ANY, pl.BlockSpec((None, tm, W), lambda b, i: (b, i, 0)), pl.BlockSpec((NCHIP, None, W, cb), lambda b, i: (0, b, 0, 0))],
        out_shape=[_sds((S, DIN), BF16), _sds((4, S, W), BF16), _sds((NCHIP, 4, W, cb), BF16)],
        scratch_shapes=[pltpu.VMEM((NCHIP, W, cb), F32), pltpu.VMEM((2, tm, D), BF16), pltpu.SemaphoreType.DMA((2,))],
        input_output_aliases={5: 0}, compiler_params=_cp("arbitrary", "arbitrary"), name="merge_bwd")(dz, y_all, wbr, proj, proj, dproj)


def _out_fwd(z, wo, x, tm=512):
    def body(z_ref, w_ref, x_ref, o_ref):
        o_ref[...] = x_ref[...] + _dot(z_ref[...], w_ref[...])

    row = pl.BlockSpec((tm, D), lambda i: (i, 0))
    return pl.pallas_call(
        body, grid=(S // tm,), in_specs=[row, pl.BlockSpec((D, D), lambda i: (0, 0)), row], out_specs=row,
        out_shape=_sds((S, D), F32), compiler_params=_cp("parallel"), name="out_fwd")(z, wo, x)


def _out_bwd(dx, z, wo, dep, tm=512):
    ni = S // tm

    def body(dx_ref, z_ref, w_ref, dep_ref, dz_ref, dw_ref, acc):
        i = pl.program_id(0)
        dxb = dx_ref[...].astype(BF16)
        dz_ref[...] = _dot_nt(dxb, w_ref[...]).astype(BF16)
        part = _dot_tn(z_ref[...], dxb)

        @pl.when(i == 0)
        def _():
            acc[...] = part

        @pl.when(i > 0)
        def _():
            acc[...] += part

        @pl.when(i == ni - 1)
        def _():
            dw_ref[...] = acc[...].astype(BF16)

    row = pl.BlockSpec((tm, D), lambda i: (i, 0))
    full = pl.BlockSpec((D, D), lambda i: (0, 0))
    return pl.pallas_call(
        body, grid=(ni,), in_specs=[row, row, full, ANY], out_specs=[row, full],
        out_shape=[_sds((S, D), BF16), _sds((D, D), BF16)], scratch_shapes=[pltpu.VMEM((D, D), F32)],
        compiler_params=_cp("arbitrary"), name="out_bwd")(dx, z, wo, dep)


def _gelu_parts(a):
    cdf = 0.5 * (1.0 + lax.erf(a * INV_SQRT2))
    return a * cdf, cdf


def _ln_parts(v):
    mu = jnp.mean(v, axis=-1, keepdims=True)
    vc = v - mu
    rs = lax.rsqrt(jnp.mean(vc * vc, axis=-1, keepdims=True) + EPS)
    return vc * rs, rs


def _causal_mask():
    return lax.broadcasted_iota(jnp.int32, (HD, HD), 0) >= lax.broadcasted_iota(jnp.int32, (HD, HD), 1)


def _gmlp_fwd(proj, lg, lb, ws, bias, y_all, tm=512):
    def body(uv_ref, gt_ref, lg_ref, lb_ref, ws_ref, b_ref, y_in, y_ref):
        act, _ = _gelu_parts(uv_ref[...].astype(F32))
        u = act[:, :W]
        xh, _ = _ln_parts(act[:, W:])
        vn = (xh * lg_ref[...] + lb_ref[...]).astype(BF16)
        gt = gt_ref[...].astype(F32)
        us = u * (gt * _sigmoid(gt))
        mask = _causal_mask()
        for h in range(4):
            wm = jnp.where(mask, ws_ref[h], 0.0).astype(BF16)
            cs = slice(h * HD, (h + 1) * HD)
            for c in range(tm // HD):
                rs_ = slice(c * HD, (c + 1) * HD)
                mixed = _dot(wm, vn[rs_, cs]) + b_ref[h]
                y_ref[rs_, cs] = (us[rs_, cs] * mixed).astype(BF16)

    vec = pl.BlockSpec((1, W), lambda i: (0, 0))
    mats = pl.BlockSpec((4, HD, HD), lambda i: (0, 0, 0))
    return pl.pallas_call(
        body, grid=(S // tm,),
        in_specs=[pl.BlockSpec((tm, 2 * W), lambda i: (i, 0)), pl.BlockSpec((tm, W), lambda i: (i, P_AGATE)), vec, vec, mats, mats, ANY],
        out_specs=pl.BlockSpec((None, tm, W), lambda i: (0, i, 0)), out_shape=_sds((4, S, W), BF16),
        input_output_aliases={6: 0}, compiler_params=_cp("parallel"), name="gmlp_fwd")(proj, proj, lg, lb, ws, bias, y_all)


def _gmlp_bwd(proj, dy_all, lg, lb, ws, bias, dproj, tm=256):
    ni = S // tm

    def body(uv_ref, gt_ref, dy_ref, lg_ref, lb_ref, ws_ref, b_ref, dp_in, dp_ref, dws_ref, dbs_ref, dlg_ref, dlb_ref, mix_s, dvn_s):
        i = pl.program_id(0)

        @pl.when(i == 0)
        def _():
            dws_ref[...] = jnp.zeros_like(dws_ref)
            dbs_ref[...] = jnp.zeros_like(dbs_ref)
            dlg_ref[...] = jnp.zeros_like(dlg_ref)
            dlb_ref[...] = jnp.zeros_like(dlb_ref)

        a0 = uv_ref[...].astype(F32)
        act, cdf = _gelu_parts(a0)
        u = act[:, :W]
        xh, rs = _ln_parts(act[:, W:])
        lgv = lg_ref[...]
        vn = (xh * lgv + lb_ref[...]).astype(BF16)
        mask = _causal_mask()
        wms = [jnp.where(mask, ws_ref[h], 0.0).astype(BF16) for h in range(4)]
        blocks = [(slice(c * HD, (c + 1) * HD), slice(h * HD, (h + 1) * HD), h) for h in range(4) for c in range(tm // HD)]
        for rs_, cs, h in blocks:
            mix_s[rs_, cs] = _dot(wms[h], vn[rs_, cs]) + b_ref[h]
        mixed = mix_s[...]
        gt = gt_ref[...].astype(F32)
        sg = _sigmoid(gt)
        sl = gt * sg
        dyv = dy_ref[...].astype(F32)
        dum = dyv * sl
        dgate = dyv * (u * mixed) * (sg * (1.0 + gt * (1.0 - sg)))
        du = dum * mixed
        dmix = dum * u
        dmb = dmix.astype(BF16)
        for rs_, cs, h in blocks:
            dvn_s[rs_, cs] = _dot_tn(wms[h], dmb[rs_, cs])
        for rs_, cs, h in blocks:
            dws_ref[h] += _dot_nt(dmb[rs_, cs], vn[rs_, cs])
            dbs_ref[h] += dmix[rs_, cs]
        dvn = dvn_s[...]
        dlg_ref[...] += jnp.sum(dvn * xh, axis=0, keepdims=True)
        dlb_ref[...] += jnp.sum(dvn, axis=0, keepdims=True)
        dxh = dvn * lgv
        dv = rs * (dxh - jnp.mean(dxh, axis=-1, keepdims=True) - xh * jnp.mean(dxh * xh, axis=-1, keepdims=True))
        gp = cdf + a0 * (jnp.exp(-0.5 * a0 * a0) * INV_SQRT2PI)
        dp_ref[:, :W] = (du * gp[:, :W]).astype(BF16)
        dp_ref[:, W:2 * W] = (dv * gp[:, W:]).astype(BF16)
        dp_ref[:, 2 * W:] = dgate.astype(BF16)

        @pl.when(i == ni - 1)
        def _():
            for h in range(4):
                dws_ref[h] = jnp.where(mask, dws_ref[h], 0.0)
                dbs_ref[h] = jnp.broadcast_to(jnp.sum(dbs_ref[h], axis=1, keepdims=True), (HD, HD))

    vec = pl.BlockSpec((1, W), lambda i: (0, 0))
    mats = pl.BlockSpec((4, HD, HD), lambda i: (0, 0, 0))
    return pl.pallas_call(
        body, grid=(ni,),
        in_specs=[pl.BlockSpec((tm, 2 * W), lambda i: (i, 0)), pl.BlockSpec((tm, W), lambda i: (i, P_AGATE)),
                  pl.BlockSpec((None, tm, W), lambda i: (0, i, 0)), vec, vec, mats, mats, ANY],
        out_specs=[pl.BlockSpec((tm, 3 * W), lambda i: (i, 0)), mats, mats, vec, vec],
        out_shape=[_sds((S, DIN), BF16), _sds((4, HD, HD), F32), _sds((4, HD, HD), F32), _sds((1, W), F32), _sds((1, W), F32)],
        scratch_shapes=[pltpu.VMEM((tm, W), F32), pltpu.VMEM((tm, W), F32)],
        input_output_aliases={7: 0}, compiler_params=_cp("arbitrary"), name="gmlp_bwd")(proj, proj, dy_all, lg, lb, ws, bias, dproj)


def _pool_diff(p, halo, row0, tm):
    xx = jnp.concatenate([halo, p], axis=0)
    t1 = (row0 + 1 + lax.broadcasted_iota(jnp.int32, (tm, 1), 0)).astype(F32)
    out = []
    for g, win in enumerate(POOL_WINDOWS):
        s = xx[:, g * HD:(g + 1) * HD]
        sh = 1
        while sh < win:
            s = s + pltpu.roll(s, sh, 0)
            sh *= 2
        out.append(s[HALO:] / jnp.minimum(t1, float(win)) - p[:, g * HD:(g + 1) * HD])
    return out


def _pool_fwd(proj, pw, sc, y_all, tm=512):
    rb = tm // HALO

    def body(p_ref, h_ref, gt_ref, pw_ref, sc_ref, y_in, y_ref):
        i = pl.program_id(0)
        halo = jnp.where(i > 0, h_ref[...].astype(F32), 0.0)
        ds = _pool_diff(p_ref[...].astype(F32), halo, i * tm, tm)
        gt = gt_ref[...].astype(F32)
        sl = gt * _sigmoid(gt)
        for g in range(4):
            cs = slice(g * HD, (g + 1) * HD)
            lin = _dot(ds[g].astype(BF16), pw_ref[g].astype(BF16))
            y_ref[:, cs] = (lin * sc_ref[:, cs] * sl[:, cs]).astype(BF16)

    return pl.pallas_call(
        body, grid=(S // tm,),
        in_specs=[pl.BlockSpec((tm, W), lambda i: (i, P_PIN)),
                  pl.BlockSpec((HALO, W), lambda i: (jnp.maximum(i * rb - 1, 0), P_PIN)),
                  pl.BlockSpec((tm, W), lambda i: (i, P_PGATE)),
                  pl.BlockSpec((4, HD, HD), lambda i: (0, 0, 0)), pl.BlockSpec((1, W), lambda i: (0, 0)), ANY],
        out_specs=pl.BlockSpec((None, tm, W), lambda i: (1, i, 0)), out_shape=_sds((4, S, W), BF16),
        input_output_aliases={5: 0}, compiler_params=_cp("parallel"), name="pool_fwd")(proj, proj, proj, pw, sc, y_all)


def _pool_bwd(proj, dy_all, pw, sc, dproj, tm=256):
    ni = S // tm
    rb = tm // HALO
    last_rb = S // HALO - 1
    rx = tm + HALO

    def body(p_ref, h_ref, gt_ref, gh_ref, dy_ref, dyh_ref, pw_ref, sc_ref, dp_in, dp_ref, dpw_ref, dsc_ref, obuf, osem):
        i = pl.program_id(0)

        @pl.when(i == 0)
        def _():
            dpw_ref[...] = jnp.zeros_like(dpw_ref)
            dsc_ref[...] = jnp.zeros_like(dsc_ref)

        halo = jnp.where(i > 0, h_ref[...].astype(F32), 0.0)
        ds = _pool_diff(p_ref[...].astype(F32), halo, i * tm, tm)
        nxt = i < ni - 1
        gx = jnp.concatenate([gt_ref[...], gh_ref[...]], axis=0).astype(F32)
        dyx = jnp.concatenate([dy_ref[...].astype(F32), jnp.where(nxt, dyh_ref[...].astype(F32), 0.0)], axis=0)
        sgx = _sigmoid(gx)
        slx = gx * sgx
        scv = sc_ref[...]
        dlinx = dyx * slx * scv
        t1 = (i * tm + 1 + lax.broadcasted_iota(jnp.int32, (rx, 1), 0)).astype(F32)
        gt, sg, sl, dyv = gx[:tm], sgx[:tm], slx[:tm], dyx[:tm]
        dsl = sg * (1.0 + gt * (1.0 - sg))

        def fill(slot):
            for g, win in enumerate(POOL_WINDOWS):
                cs = slice(g * HD, (g + 1) * HD)
                wv = pw_ref[g].astype(BF16)
                dlb = dlinx[:, cs].astype(BF16)
                ddx = _dot_nt(dlb, wv)
                f = ddx / jnp.minimum(t1, float(win))
                sh = 1
                while sh < win:
                    f = f + pltpu.roll(f, rx - sh, 0)
                    sh *= 2
                slot[:, cs] = (f[:tm] - ddx[:tm]).astype(BF16)
                db = ds[g].astype(BF16)
                lin = _dot(db, wv)
                slot[:, W + g * HD:W + (g + 1) * HD] = (dyv[:, cs] * lin * scv[:, cs] * dsl[:, cs]).astype(BF16)
                dsc_ref[:, cs] += jnp.sum(dyv[:, cs] * sl[:, cs] * lin, axis=0, keepdims=True)
                dpw_ref[g] += _dot_tn(db, dlb[:tm])

        _tile_put(obuf, osem, lambda st: dp_ref.at[pl.ds(st * tm, tm), pl.ds(P_PIN * W, 2 * W)], i, ni, fill)

    mats = pl.BlockSpec((4, HD, HD), lambda i: (0, 0, 0))
    vec = pl.BlockSpec((1, W), lambda i: (0, 0))
    return pl.pallas_call(
        body, grid=(ni,),
        in_specs=[pl.BlockSpec((tm, W), lambda i: (i, P_PIN)),
                  pl.BlockSpec((HALO, W), lambda i: (jnp.maximum(i * rb - 1, 0), P_PIN)),
                  pl.BlockSpec((tm, W), lambda i: (i, P_PGATE)),
                  pl.BlockSpec((HALO, W), lambda i: (jnp.minimum((i + 1) * rb, last_rb), P_PGATE)),
                  pl.BlockSpec((None, tm, W), lambda i: (1, i, 0)),
                  pl.BlockSpec((None, HALO, W), lambda i: (1, jnp.minimum((i + 1) * rb, last_rb), 0)),
                  mats, vec, ANY],
        out_specs=[ANY, mats, vec],
        out_shape=[_sds((S, DIN), BF16), _sds((4, HD, HD), F32), _sds((1, W), F32)],
        scratch_shapes=[pltpu.VMEM((2, tm, 2 * W), BF16), pltpu.SemaphoreType.DMA((2,))],
        input_output_aliases={8: 0}, compiler_params=_cp("arbitrary"), name="pool_bwd")(proj, proj, proj, proj, dy_all, dy_all, pw, sc, dproj)


ATT_STEP = ((1, 4), (4, 1), (4, 1))
ATT_GROUP = 16
ATT_GROUP_BWD = 8


def _att_band():
    qi = lax.broadcasted_iota(jnp.int32, (HD, 2 * HD), 0)
    kj = lax.broadcasted_iota(jnp.int32, (HD, 2 * HD), 1)
    return jnp.logical_and(kj >= qi, kj <= qi + HD), kj < HD


def _att_keys(kp_ref, ko_ref, vp_ref, vo_ref, a, jj):
    if jj == 0:
        return (jnp.concatenate([kp_ref[a], ko_ref[a, :HD, :]], axis=0), jnp.concatenate([vp_ref[a], vo_ref[a, :HD, :]], axis=0))
    return ko_ref[a, (jj - 1) * HD:(jj + 1) * HD, :], vo_ref[a, (jj - 1) * HD:(jj + 1) * HD, :]


def _dilate(src, dst, d, rows, cast=None):
    for r in range(d):
        for h in range(4):
            v = src.at[h][pl.ds(r, rows // d, stride=d), :] if d > 1 else src[h]
            dst[r * 4 + h] = v if cast is None else v.astype(cast)


def _undilate(src, dst, d, rows):
    for r in range(d):
        for h in range(4):
            if d > 1:
                dst.at[h][pl.ds(r, rows // d, stride=d), :] = src[r * 4 + h].astype(F32)
            else:
                dst[h] = src[h].astype(F32)


def _dil_spec(d, tm):
    return pl.BlockSpec((4 * d, tm // d, HD), lambda i: (0, i, 0))


def _att_prep(proj, tm=512):
    def body(q0, q1, q2, k_ref, v_ref, *rest):
        outs, scr = rest[:9], rest[9]
        for j, (src, dsts) in enumerate(((q0, ((0, outs[0]),)), (q1, ((1, outs[1]),)), (q2, ((2, outs[2]),)),
                                         (k_ref, tuple((g, outs[3 + g]) for g in range(3))),
                                         (v_ref, tuple((g, outs[6 + g]) for g in range(3))))):
            for h in range(4):
                scr[j, h] = src[:, h * HD:(h + 1) * HD].astype(F32)
            for g, dst in dsts:
                _dilate(scr.at[j], dst, DILATIONS[g], tm, BF16)

    def piece(p):
        return pl.BlockSpec((tm, W), lambda i: (i, p))

    shapes = [_sds((4 * d, S // d, HD), BF16) for d in DILATIONS]
    res = pl.pallas_call(
        body, grid=(S // tm,),
        in_specs=[piece(P_CQ), piece(P_CQ + 1), piece(P_CQ + 2), piece(P_CK), piece(P_CV)],
        out_specs=[_dil_spec(d, tm) for d in DILATIONS] * 3, out_shape=shapes * 3,
        scratch_shapes=[pltpu.VMEM((5, 4, tm, HD), F32)],
        compiler_params=_cp("parallel"), name="att_prep")(proj, proj, proj, proj, proj)
    return res[0:3], res[3:6], res[6:9]


def _att_specs(g):
    d = DILATIONS[g]
    nres, njb = ATT_STEP[g]
    nb = S // d // HD
    own = pl.BlockSpec((4 * nres, njb * HD, HD), lambda r, j: (r, j, 0))
    prev = pl.BlockSpec((4 * nres, HD, HD), lambda r, j: (r, jnp.maximum(j * njb - 1, 0), 0))
    nxt = pl.BlockSpec((4 * nres, HD, HD), lambda r, j: (r, jnp.minimum((j + 1) * njb, nb - 1), 0))
    return (d // nres, nb // njb), own, prev, nxt


def _att_fwd(q, k, v, g):
    d = DILATIONS[g]
    nres, njb = ATT_STEP[g]
    grid, own, prev, _ = _att_specs(g)

    def body(q_ref, kp_ref, ko_ref, vp_ref, vo_ref, o_ref, l_ref):
        jb = pl.program_id(1)
        band, is_prev = _att_band()
        no_prev = jnp.where(is_prev, jnp.where(jb > 0, 0.0, NEG), 0.0)
        blocks = [(a, jj) for jj in range(njb) for a in range(4 * nres)]
        for g0 in range(0, len(blocks), ATT_GROUP):
            grp = blocks[g0:g0 + ATT_GROUP]
            s, v2 = [], []
            for a, jj in grp:
                k2_, v2_ = _att_keys(kp_ref, ko_ref, vp_ref, vo_ref, a, jj)
                s_ = jnp.where(band, _dot_nt(q_ref[a, jj * HD:(jj + 1) * HD, :], k2_) * SCALE, NEG)
                s.append(s_ + no_prev if jj == 0 else s_)
                v2.append(v2_)
            m = [jnp.max(s_, axis=-1, keepdims=True) for s_ in s]
            e = [jnp.exp(s_ - m_) for s_, m_ in zip(s, m)]
            den = [jnp.sum(e_, axis=-1, keepdims=True) for e_ in e]
            inv = [1.0 / d_ for d_ in den]
            for i, (a, jj) in enumerate(grp):
                rs_ = slice(jj * HD, (jj + 1) * HD)
                o_ref[a, rs_, :] = _dot((e[i] * inv[i]).astype(BF16), v2[i]).astype(BF16)
                l_ref[a, rs_, :] = jnp.broadcast_to(m[i] + jnp.log(den[i]), (HD, HD))

    return pl.pallas_call(
        body, grid=grid, in_specs=[own, prev, own, prev, own], out_specs=[own, own],
        out_shape=[_sds((4 * d, S // d, HD), BF16), _sds((4 * d, S // d, HD), F32)],
        compiler_params=_cp("parallel", "parallel"), name="att_fwd")(q, k, k, v, v)


def _att_mix(os_, ls_, proj, y_all, tm=512):
    def body(o0, o1, o2, l0, l1, l2, gt_ref, y_in, y_ref, om_ref, lt_ref, so1, so2, sl1, sl2):
        _undilate(o1, so1, DILATIONS[1], tm)
        _undilate(o2, so2, DILATIONS[2], tm)
        _undilate(l1, sl1, DILATIONS[1], tm)
        _undilate(l2, sl2, DILATIONS[2], tm)
        for h in range(4):
            a, b, c = l0[h], sl1[h], sl2[h]
            m = jnp.maximum(jnp.maximum(a, b), c)
            ea, eb, ec = jnp.exp(a - m), jnp.exp(b - m), jnp.exp(c - m)
            z = ea + eb + ec
            inv = 1.0 / z
            o = (ea * inv) * o0[h] + (eb * inv) * so1[h] + (ec * inv) * so2[h]
            gt = gt_ref[:, h * HD:(h + 1) * HD].astype(F32)
            om_ref[h] = o
            lt_ref[h] = m + jnp.log(z)
            y_ref[:, h * HD:(h + 1) * HD] = (o * (gt * _sigmoid(gt))).astype(BF16)

    dil = [_dil_spec(d, tm) for d in DILATIONS]
    return pl.pallas_call(
        body, grid=(S // tm,),
        in_specs=dil * 2 + [pl.BlockSpec((tm, W), lambda i: (i, P_CGATE)), ANY],
        out_specs=[pl.BlockSpec((None, tm, W), lambda i: (2, i, 0)), dil[0], dil[0]],
        out_shape=[_sds((4, S, W), BF16), _sds((4, S, HD), F32), _sds((4, S, HD), F32)],
        scratch_shapes=[pltpu.VMEM((4, tm, HD), F32)] * 4,
        input_output_aliases={7: 0}, compiler_params=_cp("parallel"), name="att_mix")(*os_, *ls_, proj, y_all)


def _att_bwd_pre(dy_all, proj, om, lse, dproj, tm=512):
    def body(dy_ref, gt_ref, om_ref, ls_ref, dp_in, *rest):
        dos, dls, lss, dp_ref, sdo, sdl = rest[0:3], rest[3:6], rest[6:8], rest[8], rest[9], rest[10]
        for h in range(4):
            cs = slice(h * HD, (h + 1) * HD)
            gt = gt_ref[:, cs].astype(F32)
            sg = _sigmoid(gt)
            dyv = dy_ref[:, cs].astype(F32)
            o = om_ref[h]
            do = dyv * (gt * sg)
            dp_ref[:, cs] = (dyv * o * (sg * (1.0 + gt * (1.0 - sg)))).astype(BF16)
            sdo[h] = do
            sdl[h] = jnp.broadcast_to(jnp.sum(do * o, axis=-1, keepdims=True), (tm, HD))
        for g, d in enumerate(DILATIONS):
            _dilate(sdo, dos[g], d, tm, BF16)
            _dilate(sdl, dls[g], d, tm)
            if g > 0:
                _dilate(ls_ref, lss[g - 1], d, tm)

    dil = [_dil_spec(d, tm) for d in DILATIONS]
    gcol = pl.BlockSpec((tm, W), lambda i: (i, P_CGATE))
    res = pl.pallas_call(
        body, grid=(S // tm,),
        in_specs=[pl.BlockSpec((None, tm, W), lambda i: (2, i, 0)), gcol, dil[0], dil[0], ANY],
        out_specs=dil + dil + dil[1:] + [gcol],
        out_shape=([_sds((4 * d, S // d, HD), BF16) for d in DILATIONS] + [_sds((4 * d, S // d, HD), F32) for d in DILATIONS]
                   + [_sds((4 * d, S // d, HD), F32) for d in DILATIONS[1:]] + [_sds((S, DIN), BF16)]),
        scratch_shapes=[pltpu.VMEM((4, tm, HD), F32)] * 2,
        input_output_aliases={4: 8}, compiler_params=_cp("parallel"), name="att_bwd_pre")(dy_all, proj, om, lse, dproj)
    return res[0:3], res[3:6], [lse] + list(res[6:8]), res[8]


def _att_bwd(q, k, v, do, lse, delta, g):
    d = DILATIONS[g]
    nres, njb = ATT_STEP[g]
    grid, own, prev, nxt = _att_specs(g)

    def body(qa_ref, qn_ref, kp_ref, ko_ref, vp_ref, vo_ref, doa_ref, don_ref, la_ref, ln_ref, da_ref, dn_ref,
             dq_ref, dk_ref, dv_ref):
        jb = pl.program_id(1)
        band, is_prev = _att_band()
        m_next = lax.broadcasted_iota(jnp.int32, (HD, HD), 1) >= lax.broadcasted_iota(jnp.int32, (HD, HD), 0)
        has_prev = jnp.where(is_prev, jnp.where(jb > 0, 1.0, 0.0), 1.0)
        has_next = jnp.where(jb < grid[1] - 1, 1.0, 0.0)

        def wide(t):
            return jnp.concatenate([t, t], axis=1)

        blocks = [(a, jj) for jj in range(njb) for a in range(4 * nres)]
        for g0 in range(0, len(blocks), ATT_GROUP_BWD):
            grp = blocks[g0:g0 + ATT_GROUP_BWD]
            ops = []
            for a, jj in grp:
                rs_ = slice(jj * HD, (jj + 1) * HD)
                k2, v2 = _att_keys(kp_ref, ko_ref, vp_ref, vo_ref, a, jj)
                if jj == njb - 1:
                    qn, don, lsn, dln, fn = qn_ref[a], don_ref[a], ln_ref[a], dn_ref[a], has_next
                else:
                    ns = slice((jj + 1) * HD, (jj + 2) * HD)
                    qn, don, lsn, dln, fn = qa_ref[a, ns, :], doa_ref[a, ns, :], la_ref[a, ns, :], da_ref[a, ns, :], None
                ops.append(dict(qa=qa_ref[a, rs_, :], doa=doa_ref[a, rs_, :], lsa=wide(la_ref[a, rs_, :]),
                                dla=wide(da_ref[a, rs_, :]), k2=k2, v2=v2, ko=ko_ref[a, rs_, :], vo=vo_ref[a, rs_, :],
                                qn=qn, don=don, lsn=lsn, dln=dln, fn=fn, first=jj == 0))
            sa = [_dot_nt(o["qa"], o["k2"]) for o in ops]
            dpa = [_dot_nt(o["doa"], o["v2"]) for o in ops]
            sn = [_dot_nt(o["qn"], o["ko"]) for o in ops]
            dpn = [_dot_nt(o["don"], o["vo"]) for o in ops]
            pa, pn = [], []
            for o, sa_, sn_ in zip(ops, sa, sn):
                p_ = jnp.where(band, jnp.exp(sa_ * SCALE - o["lsa"]), 0.0)
                pa.append(p_ * has_prev if o["first"] else p_)
                p_ = jnp.where(m_next, jnp.exp(sn_ * SCALE - o["lsn"]), 0.0)
                pn.append(p_ if o["fn"] is None else p_ * o["fn"])
            dsa = [(p_ * (dp_ - o["dla"]) * SCALE).astype(BF16) for p_, dp_, o in zip(pa, dpa, ops)]
            dsn = [(p_ * (dp_ - o["dln"]) * SCALE).astype(BF16) for p_, dp_, o in zip(pn, dpn, ops)]
            for i, (a, jj) in enumerate(grp):
                rs_ = slice(jj * HD, (jj + 1) * HD)
                o = ops[i]
                dq_ref[a, rs_, :] = _dot(dsa[i], o["k2"]).astype(BF16)
                q2 = jnp.concatenate([o["qa"], o["qn"]], axis=0)
                do2 = jnp.concatenate([o["doa"], o["don"]], axis=0)
                dk_ref[a, rs_, :] = _dot_tn(jnp.concatenate([dsa[i][:, HD:], dsn[i]], axis=0), q2).astype(BF16)
                dv_ref[a, rs_, :] = _dot_tn(jnp.concatenate([pa[i][:, HD:].astype(BF16), pn[i].astype(BF16)], axis=0),
                                            do2).astype(BF16)

    return pl.pallas_call(
        body, grid=grid, in_specs=[own, nxt, prev, own, prev, own, own, nxt, own, nxt, own, nxt],
        out_specs=[own, own, own], out_shape=[_sds((4 * d, S // d, HD), BF16)] * 3,
        compiler_params=_cp("parallel", "parallel"), name="att_bwd")(q, q, k, k, v, v, do, do, lse, lse, delta, delta)


def _att_bwd_post(dqs, dks, dvs, dproj, tm=512):
    def body(*refs):
        dq, dk, dv, dp_ref, scr = refs[0:3], refs[3:6], refs[6:9], refs[10], refs[11]
        for g in range(3):
            _undilate(dq[g], scr, DILATIONS[g], tm)
            for h in range(4):
                dp_ref[:, g * W + h * HD:g * W + (h + 1) * HD] = scr[h].astype(BF16)
        for j, parts in enumerate((dk, dv)):
            acc = None
            for g in range(3):
                _undilate(parts[g], scr, DILATIONS[g], tm)
                vals = [scr[h] for h in range(4)]
                acc = vals if acc is None else [x + y for x, y in zip(acc, vals)]
            for h in range(4):
                dp_ref[:, (3 + j) * W + h * HD:(3 + j) * W + (h + 1) * HD] = acc[h].astype(BF16)

    dil = [_dil_spec(d, tm) for d in DILATIONS]
    return pl.pallas_call(
        body, grid=(S // tm,), in_specs=dil * 3 + [ANY],
        out_specs=pl.BlockSpec((tm, 5 * W), lambda i: (i, 1)), out_shape=_sds((S, DIN), BF16),
        scratch_shapes=[pltpu.VMEM((4, tm, HD), F32)],
        input_output_aliases={9: 0}, compiler_params=_cp("parallel"), name="att_bwd_post")(*dqs, *dks, *dvs, dproj)


def _mem_kv_fwd(mem_n, wkv):
    m = mem_n.shape[0]

    def body(a_ref, w_ref, o_ref):
        o_ref[...] = _dot(a_ref[...], w_ref[...])

    return pl.pallas_call(body, out_shape=_sds((m, 2 * W), F32), compiler_params=_cp(), name="mem_kv_fwd")(mem_n, wkv)


def _mem_fwd(proj, kv, y_all, tm=512):
    m = kv.shape[0]

    def body(q_ref, gt_ref, kv_ref, y_in, y_ref):
        gt = gt_ref[...].astype(F32)
        sl = gt * _sigmoid(gt)
        hs = [slice(h * HD, (h + 1) * HD) for h in range(4)]
        s = [_dot_nt(q_ref[:, cs].astype(BF16), kv_ref[:, cs].astype(BF16)) * SCALE for cs in hs]
        e = [jnp.exp(s_ - jnp.max(s_, axis=-1, keepdims=True)) for s_ in s]
        p = [(e_ * (1.0 / jnp.sum(e_, axis=-1, keepdims=True))).astype(BF16) for e_ in e]
        for h, cs in enumerate(hs):
            o = _dot(p[h], kv_ref[:, W + h * HD:W + (h + 1) * HD].astype(BF16))
            y_ref[:, cs] = (o * sl[:, cs]).astype(BF16)

    return pl.pallas_call(
        body, grid=(S // tm,),
        in_specs=[pl.BlockSpec((tm, W), lambda i: (i, P_MQ)), pl.BlockSpec((tm, W), lambda i: (i, P_MGATE)),
                  pl.BlockSpec((m, 2 * W), lambda i: (0, 0)), ANY],
        out_specs=pl.BlockSpec((None, tm, W), lambda i: (3, i, 0)), out_shape=_sds((4, S, W), BF16),
        input_output_aliases={3: 0}, compiler_params=_cp("parallel"), name="mem_fwd")(proj, proj, kv, y_all)


def _mem_bwd(proj, kv, dy_all, dproj, tm=512):
    m = kv.shape[0]
    ni = S // tm

    def body(q_ref, gt_ref, kv_ref, dy_ref, dp_in, dp_ref, dkv_ref, obuf, osem):
        i = pl.program_id(0)

        @pl.when(i == 0)
        def _():
            dkv_ref[...] = jnp.zeros_like(dkv_ref)

        gt = gt_ref[...].astype(F32)
        sg = _sigmoid(gt)
        sl = gt * sg
        dsl = sg * (1.0 + gt * (1.0 - sg))
        dyv = dy_ref[...].astype(F32)

        def fill(slot):
            hs = [slice(h * HD, (h + 1) * HD) for h in range(4)]
            vss = [slice(W + h * HD, W + (h + 1) * HD) for h in range(4)]
            q = [q_ref[:, cs].astype(BF16) for cs in hs]
            k = [kv_ref[:, cs].astype(BF16) for cs in hs]
            v = [kv_ref[:, vs].astype(BF16) for vs in vss]
            dob = [(dyv[:, cs] * sl[:, cs]).astype(BF16) for cs in hs]
            s = [_dot_nt(q_, k_) * SCALE for q_, k_ in zip(q, k)]
            dp = [_dot_nt(d_, v_) for d_, v_ in zip(dob, v)]
            e = [jnp.exp(s_ - jnp.max(s_, axis=-1, keepdims=True)) for s_ in s]
            p = [e_ * (1.0 / jnp.sum(e_, axis=-1, keepdims=True)) for e_ in e]
            pb = [p_.astype(BF16) for p_ in p]
            dsb = [(p_ * (dp_ - jnp.sum(dp_ * p_, axis=-1, keepdims=True)) * SCALE).astype(BF16) for p_, dp_ in zip(p, dp)]
            for h, (cs, vs) in enumerate(zip(hs, vss)):
                o = _dot(pb[h], v[h])
                slot[:, cs] = _dot(dsb[h], k[h]).astype(BF16)
                slot[:, vs] = (dyv[:, cs] * o * dsl[:, cs]).astype(BF16)
                dkv_ref[:, cs] += _dot_tn(dsb[h], q[h])
                dkv_ref[:, vs] += _dot_tn(pb[h], dob[h])

        _tile_put(obuf, osem, lambda st: dp_ref.at[pl.ds(st * tm, tm), pl.ds(P_MQ * W, 2 * W)], i, ni, fill)

    return pl.pallas_call(
        body, grid=(ni,),
        in_specs=[pl.BlockSpec((tm, W), lambda i: (i, P_MQ)), pl.BlockSpec((tm, W), lambda i: (i, P_MGATE)),
                  pl.BlockSpec((m, 2 * W), lambda i: (0, 0)), pl.BlockSpec((None, tm, W), lambda i: (3, i, 0)), ANY],
        out_specs=[ANY, pl.BlockSpec((m, 2 * W), lambda i: (0, 0))],
        out_shape=[_sds((S, DIN), BF16), _sds((m, 2 * W), F32)],
        scratch_shapes=[pltpu.VMEM((2, tm, 2 * W), BF16), pltpu.SemaphoreType.DMA((2,))],
        input_output_aliases={4: 0}, compiler_params=_cp("arbitrary"), name="mem_bwd")(proj, proj, kv, dy_all, dproj)


def _mem_kv_bwd(mem, g, mem_n, wkv, dkv):
    m = mem.shape[0]

    def body(x_ref, g_ref, a_ref, w_ref, d_ref, dw_ref, dg_ref):
        db = d_ref[...].astype(BF16)
        dw_ref[...] = _dot_tn(a_ref[...], db).astype(BF16)
        dn = _dot_nt(db, w_ref[...])
        xv = x_ref[...]
        xh = xv * lax.rsqrt(jnp.mean(xv * xv, axis=-1, keepdims=True) + EPS)
        dg_ref[...] = jnp.sum(dn * xh, axis=0, keepdims=True)

    return pl.pallas_call(
        body, out_shape=[_sds((D, 2 * W), BF16), _sds((1, D), F32)], compiler_params=_cp(), name="mem_kv_bwd")(mem, g, mem_n, wkv, dkv)


def _layer_fwd(x, h, ht, mem_n, p, wg):
    win, wkv, wbr, wo = wg
    proj = _proj_fwd(h, win)
    y_all = lax.empty((4, S, W), BF16)
    y_all = _gmlp_fwd(proj, p["gm_ln_g"], p["gm_ln_b"], p["gm_ws"], p["gm_bias"], y_all)
    y_all = _pool_fwd(proj, p["pool_w"], p["pool_scale"], y_all)
    qs, ks, vs = _att_prep(proj)
    os_, ls_ = zip(*[_att_fwd(qs[g], ks[g], vs[g], g) for g in range(3)])
    y_all, om, lse = _att_mix(os_, ls_, proj, y_all)
    kv = _mem_kv_fwd(mem_n, wkv)
    y_all = _mem_fwd(proj, kv, y_all)
    z = _merge_fwd(y_all, wbr, proj)
    x_new = _out_fwd(z, wo, x)
    return x_new, dict(x=x, ht=ht, proj=proj, y_all=y_all, om=om, lse=lse, mem_n=mem_n, kv=kv, z=z, qkv=(qs, ks, vs))


GRAD_PARTS = ((2, D // 2, CW), (2, D // 8, 2 * W), (2, 2 * W, D // NCHIP), (2, D // 8, D))
SUM_TILE = (64, 128, 256, 128)
ADAM_TILE = (128, 256, 2048, 256)
PLACE_TILE = (256, 256, 512, 256)


def _layer_bwd(dx, mem, p, wg, sv, exchange, dep):
    win, wkv, wbr, wo = wg
    proj = sv["proj"]
    dz, d_wo = _out_bwd(dx, sv["z"], wo, dep)
    dproj = lax.empty((S, DIN), BF16)
    dproj, dy_all, d_wbr = _merge_bwd(dz, sv["y_all"], wbr, proj, dproj)
    dproj, d_ws, d_bs, d_lg, d_lb = _gmlp_bwd(proj, dy_all, p["gm_ln_g"], p["gm_ln_b"], p["gm_ws"], p["gm_bias"], dproj)
    dproj, d_pw, d_sc = _pool_bwd(proj, dy_all, p["pool_w"], p["pool_scale"], dproj)
    dos, dls, lss, dproj = _att_bwd_pre(dy_all, proj, sv["om"], sv["lse"], dproj)
    qs, ks, vs = sv["qkv"]
    dqs, dks, dvs = zip(*[_att_bwd(qs[g], ks[g], vs[g], dos[g], lss[g], dls[g], g) for g in range(3)])
    dproj = _att_bwd_post(dqs, dks, dvs, dproj)
    dproj, dkv = _mem_bwd(proj, sv["kv"], dy_all, dproj)
    d_wkv, d_mg = _mem_kv_bwd(mem, p["mem_norm_g"], sv["mem_n"], wkv, dkv)
    d_win = _proj_bwd_w(sv["ht"], dproj)
    big = tuple(t.reshape((NCHIP,) + s) for t, s in zip((d_win, d_wkv, d_wbr, d_wo), GRAD_PARTS))
    inflight = exchange(big)
    dh = _proj_bwd_x(dproj, win, inflight[-1])
    dx_in, d_ng = _rms_bwd(dh, sv["x"], p["norm_g"], dx)
    small = dict(norm_g=d_ng, gm_ln_g=d_lg, gm_ln_b=d_lb, gm_ws=d_ws, gm_bs=d_bs[:, :, 0], pool_w=d_pw, pool_scale=d_sc, mem_norm_g=d_mg)
    return dx_in, (big,) + inflight, small


_SMALL = ("norm_g", "gm_ln_g", "gm_ln_b", "gm_ws", "gm_bs", "pool_w", "pool_scale", "mem_norm_g")


def _layer_params(l, norm_g, gm_ln_g, gm_ln_b, gm_ws, gm_bs, pool_w, pool_scale, mem_norm_g):
    return dict(norm_g=norm_g[l][None], gm_ln_g=gm_ln_g[l][None], gm_ln_b=gm_ln_b[l][None], gm_ws=gm_ws[l],
                gm_bias=jnp.broadcast_to(gm_bs[l][:, :, None], (4, HD, HD)), pool_w=pool_w[l],
                pool_scale=pool_scale[l][None], mem_norm_g=mem_norm_g[l][None])


def kernel(x, mem, norm_g, w_in, gm_ln_g, gm_ln_b, gm_ws, gm_bs, pool_w, pool_scale, mem_norm_g, w_mem_kv, w_branch, w_out, final_norm_g, loss_target, m_norm_g, m_w_in, m_gm_ln_g, m_gm_ln_b, m_gm_ws, m_gm_bs, m_pool_w, m_pool_scale, m_mem_norm_g, m_w_mem_kv, m_w_branch, m_w_out, m_final_norm_g, v_norm_g, v_w_in, v_gm_ln_g, v_gm_ln_b, v_gm_ws, v_gm_bs, v_pool_w, v_pool_scale, v_mem_norm_g, v_w_mem_kv, v_w_branch, v_w_out, v_final_norm_g):
    xs, memv, tgt = x[0], mem[0], loss_target[0]
    params = [_layer_params(l, norm_g, gm_ln_g, gm_ln_b, gm_ws, gm_bs, pool_w, pool_scale, mem_norm_g) for l in range(NL)]

    large = (w_in, w_mem_kv, w_branch, w_out)
    first = tuple(t[:1].astype(BF16) for t in large)
    lands, gsems, after = _gather_start(first, True)
    rest = tuple((t[1:] + after[0, 0]).astype(BF16) for t in large)
    lands_r, gsems_r, after_r = _gather_start(rest, False)
    lands, gsems = lands + lands_r, gsems + gsems_r

    saved, wgs = [], []
    mem_ns = [_rms_fwd(memv, params[l]["mem_norm_g"], memv.shape[0]) for l in range(NL)]
    for l in range(NL):
        h, ht = _rms_fwd_t(xs, params[l]["norm_g"])
        if l == 0:
            got, relay_sems = _gather_relay(first, lands[0], gsems[0], [after, after_r, h] + mem_ns)
            got = _gather_wait_relay(got, relay_sems)
            got = [_place_own(got[k], first[k], 0, PLACE_TILE[k]) for k in range(4)]
        else:
            got = _gather_wait(f"gather_wait_{l}", l - 1, rest, lands[l], gsems[l], after)
            got = [_place_own(got[k], rest[k], l - 1, PLACE_TILE[k]) for k in range(4)]
        wgs.append((got[0], got[1].reshape(D, 2 * W), got[2], got[3].reshape(D, D)))
        xs, sv = _layer_fwd(xs, h, ht, mem_ns[l], params[l], wgs[l])
        saved.append(sv)
        after = xs
    dx, d_fg, ls = _loss_head(xs, tgt, final_norm_g[None])
    loss = lax.psum(ls[0, 0], ("x", "y", "c"))

    def pack(leaves):
        parts, spans, at = [], [], 0
        for t in leaves:
            r = t.reshape(-1, 128)
            pad = -r.shape[0] % 8
            parts.append(jnp.pad(r, ((0, pad), (0, 0))) if pad else r)
            spans.append((at, at + r.shape[0]))
            at += r.shape[0] + pad
        return jnp.concatenate(parts, axis=0), spans

    flight, small = [None] * NL, [None] * NL
    dep = dx
    for l in reversed(range(NL)):
        dx, flight[l], small[l] = _layer_bwd(dx, memv, params[l], wgs[l], saved[l], functools.partial(_exch_start, l), dep)
        dep = dx
        if l == 1:
            packed_a, spans_a = pack([jnp.stack([small[j][n] for j in range(1, NL)]) for n in _SMALL] + [d_fg])
            zone_a, sems_a, dep = _small_start(packed_a, "small_start_early")
    grad_x = dx[None]
    packed_b, spans_b = pack([small[0][n] for n in _SMALL])
    zone_b, sems_b, after = _small_start(packed_b, "small_start_last")

    ws = dict(norm_g=norm_g, w_in=w_in, gm_ln_g=gm_ln_g, gm_ln_b=gm_ln_b, gm_ws=gm_ws, gm_bs=gm_bs, pool_w=pool_w,
              pool_scale=pool_scale, mem_norm_g=mem_norm_g, w_mem_kv=w_mem_kv, w_branch=w_branch, w_out=w_out,
              final_norm_g=final_norm_g)
    ms = dict(norm_g=m_norm_g, w_in=m_w_in, gm_ln_g=m_gm_ln_g, gm_ln_b=m_gm_ln_b, gm_ws=m_gm_ws, gm_bs=m_gm_bs,
              pool_w=m_pool_w, pool_scale=m_pool_scale, mem_norm_g=m_mem_norm_g, w_mem_kv=m_w_mem_kv,
              w_branch=m_w_branch, w_out=m_w_out, final_norm_g=m_final_norm_g)
    vs = dict(norm_g=v_norm_g, w_in=v_w_in, gm_ln_g=v_gm_ln_g, gm_ln_b=v_gm_ln_b, gm_ws=v_gm_ws, gm_bs=v_gm_bs,
              pool_w=v_pool_w, pool_scale=v_pool_scale, mem_norm_g=v_mem_norm_g, w_mem_kv=v_w_mem_kv,
              w_branch=v_w_branch, w_out=v_w_out, final_norm_g=v_final_norm_g)

    big = ("w_in", "w_mem_kv", "w_branch", "w_out")
    acc = {n: [lax.empty((ws[n].size // ws[n].shape[-1], ws[n].shape[-1]), F32) for _ in range(4)] for n in big}
    for l in reversed(range(NL)):
        parts, zones, sems, _ = flight[l]
        zones = _exch_wait(l, parts, zones, sems, after)
        full = _share_full([_sum_half(parts[k], zones[k], SUM_TILE[k]) for k in range(4)])
        for k, n in enumerate(big):
            acc[n] = _adamw_layer(l, ws[n], full[k], ms[n], vs[n], acc[n], ADAM_TILE[k])
        after = acc["w_in"][0]
    grads, upd = {}, {}
    for n in big:
        d_, m_, v_, g_ = (t.reshape(ws[n].shape) for t in acc[n])
        grads[n], upd[n] = g_, (d_, m_, v_)

    def row_tile(rows):
        return max(t for t in range(8, 513, 8) if rows % t == 0)

    zone_a = _small_wait(packed_a, zone_a, sems_a, after, "small_wait_early")
    zone_b = _small_wait(packed_b, zone_b, sems_b, zone_a, "small_wait_last")
    tot_a = _sum_small(packed_a, zone_a, row_tile(packed_a.shape[0]))
    tot_b = _sum_small(packed_b, zone_b, row_tile(packed_b.shape[0]))
    for i, n in enumerate(_SMALL):
        later = tot_a[spans_a[i][0]:spans_a[i][1]].reshape((NL - 1,) + ws[n].shape[1:])
        first_l = tot_b[spans_b[i][0]:spans_b[i][1]].reshape(ws[n].shape[1:])
        grads[n] = jnp.stack([first_l] + [later[j] for j in range(NL - 1)])
    grads["final_norm_g"] = tot_a[spans_a[-1][0]:spans_a[-1][1]].reshape(ws["final_norm_g"].shape)
    for n in _SMALL + ("final_norm_g",):
        upd[n] = _adamw(ws[n], grads[n], ms[n], vs[n])
    order = ("norm_g", "w_in", "gm_ln_g", "gm_ln_b", "gm_ws", "gm_bs", "pool_w", "pool_scale", "mem_norm_g", "w_mem_kv",
             "w_branch", "w_out", "final_norm_g")
    return (loss, grad_x, *[grads[n] for n in order], *[upd[n][0] for n in order], *[upd[n][1] for n in order],
            *[upd[n][2] for n in order])
```

```python
import functools
import math

import jax
import jax.numpy as jnp
from jax import lax
from jax.experimental import pallas as pl
from jax.experimental.pallas import tpu as pltpu

F32 = jnp.float32
BF16 = jnp.bfloat16

S = 4096
D = 1024
W = 512
DIN = 10752
NL = 4
NCHIP = 4
NDEV = 8
CW = DIN // NCHIP
TN_IN = 896
NJ = CW // TN_IN
HD = 128
EPS = 1e-6
NEG = -1e30
SCALE = HD ** -0.5
INV_SQRT2 = 1.0 / math.sqrt(2.0)
INV_SQRT2PI = 1.0 / math.sqrt(2.0 * math.pi)
POOL_WINDOWS = (2, 4, 8, 16)
DILATIONS = (1, 4, 16)
HALO = 16
NPIECE = DIN // W
P_AGATE, P_PIN, P_PGATE, P_CQ, P_CK, P_CV, P_CGATE, P_MQ, P_MGATE, P_GM = 2, 3, 4, 5, 8, 9, 10, 11, 12, 13
VMEM_LIMIT = 56 * 1024 * 1024

ADAM_LR, ADAM_B1, ADAM_B2, ADAM_EPS, ADAM_WD, ADAM_STEP = 0.001, 0.9, 0.999, 1e-08, 0.01, 10

MESH = pl.DeviceIdType.MESH
ANY = pl.BlockSpec(memory_space=pl.ANY)


def _cp(*sem):
    return pltpu.CompilerParams(dimension_semantics=sem or None, vmem_limit_bytes=VMEM_LIMIT)


def _sds(shape, dtype):
    return jax.ShapeDtypeStruct(shape, dtype)


def _sigmoid(v):
    return 1.0 / (1.0 + jnp.exp(-v))


def _dot(a, b):
    return jnp.dot(a, b, preferred_element_type=F32)


def _dot_nt(a, b):
    return lax.dot_general(a, b, (((1,), (1,)), ((), ())), preferred_element_type=F32)


def _dot_tn(a, b):
    return lax.dot_general(a, b, (((0,), (0,)), ((), ())), preferred_element_type=F32)


def _tile_put(buf, sem, dst_of, step, nsteps, fill):
    slot = step % 2

    def copy(s, st):
        return pltpu.make_async_copy(buf.at[s], dst_of(st), sem.at[s])

    @pl.when(step >= 2)
    def _():
        copy(slot, step).wait()

    fill(buf.at[slot])
    copy(slot, step).start()

    @pl.when(step == nsteps - 1)
    def _():
        if nsteps >= 2:
            copy(1 - slot, step).wait()
        copy(slot, step).wait()


def _my_pos():
    return lax.axis_index("x"), lax.axis_index("y"), lax.axis_index("c")


_CHIP_REL = ((1, 0), (0, 1), (1, 1))
_DEV_REL = tuple((dx, dy, dc) for dx in (0, 1) for dy in (0, 1) for dc in (0, 1))[1:]


HBM = pl.BlockSpec(memory_space=pltpu.HBM)
SEM = pl.BlockSpec(memory_space=pltpu.SEMAPHORE)
EFFECT = pltpu.SideEffectType.DATAFLOW_SIDE_EFFECTING
N_GATHER = 3 * 4
N_EXCH = 7 * 4


def _in_hbm(t):
    return pltpu.with_memory_space_constraint(t, pltpu.HBM)


def _gather_start(shards, first):
    nk = len(shards)
    nl = shards[0].shape[0]
    lands = [pltpu.HBM((NCHIP,) + s.shape[1:], BF16) for s in shards for _ in range(nl)]

    def body(*refs):
        ins, outs = refs[:nk], refs[nk:nk + nk * nl]
        sems = refs[nk + nk * nl:nk + nk * nl + 2 * nl]
        token = refs[-1]
        x, y, c = _my_pos()
        me = 2 * x + y
        for l in range(nl):
            for r, (dx, dy) in enumerate(_CHIP_REL):
                for k in range(nk):
                    src, dst = ins[k].at[l], outs[k * nl + l].at[me]
                    if first:
                        hf = pl.ds(c * (shards[k].shape[1] // 2), shards[k].shape[1] // 2)
                        src, dst = src.at[hf], dst.at[hf]
                    pltpu.make_async_remote_copy(
                        src_ref=src, dst_ref=dst, send_sem=sems[2 * l].at[r * nk + k],
                        recv_sem=sems[2 * l + 1].at[r * nk + k], device_id=(x ^ dx, y ^ dy, c), device_id_type=MESH).start()
        token[...] = jnp.zeros_like(token)

    res = pl.pallas_call(
        body, name="gather_start_first" if first else "gather_start_rest",
        out_shape=lands + [pltpu.SemaphoreType.DMA((N_GATHER,))] * (2 * nl) + [_sds((8, 128), F32)],
        in_specs=[HBM] * nk, out_specs=[HBM] * (nk * nl) + [SEM] * (2 * nl) + [pl.BlockSpec(memory_space=pltpu.VMEM)],
        compiler_params=pltpu.CompilerParams(has_side_effects=EFFECT))(*[_in_hbm(s) for s in shards])
    lands = [[res[k * nl + l] for k in range(nk)] for l in range(nl)]
    sems = [(res[nk * nl + 2 * l], res[nk * nl + 2 * l + 1]) for l in range(nl)]
    return lands, sems, res[-1]


def _gather_relay(shards, lands, sems, after):
    nk = len(shards)
    half = [s.shape[1] // 2 for s in shards]

    na = len(after)

    def body(*refs):
        ins, land = refs[:nk], refs[nk:2 * nk]
        send, recv = refs[2 * nk], refs[2 * nk + 1]
        send2, recv2 = refs[3 * nk + 2 + na], refs[3 * nk + 3 + na]
        x, y, c = _my_pos()
        for r, (dx, dy) in enumerate(_CHIP_REL):
            cx, cy = x ^ dx, y ^ dy
            for k in range(nk):
                hf = pl.ds(c * half[k], half[k])
                got = land[k].at[2 * cx + cy].at[hf]
                cp = pltpu.make_async_remote_copy(
                    src_ref=ins[k].at[0].at[hf], dst_ref=got, send_sem=send.at[r * nk + k],
                    recv_sem=recv.at[r * nk + k], device_id=(cx, cy, c), device_id_type=MESH)
                cp.wait_send()
                cp.wait_recv()
                pltpu.make_async_remote_copy(
                    src_ref=got, dst_ref=got, send_sem=send2.at[r * nk + k], recv_sem=recv2.at[r * nk + k],
                    device_id=(x, y, 1 - c), device_id_type=MESH).start()

    res = pl.pallas_call(
        body, name="gather_relay",
        out_shape=[pltpu.HBM(t.shape, t.dtype) for t in lands] + [pltpu.SemaphoreType.DMA((N_GATHER,))] * 2,
        in_specs=[ANY] * nk + [HBM] * nk + [SEM, SEM] + [ANY] * na, out_specs=[HBM] * nk + [SEM, SEM],
        input_output_aliases={nk + k: k for k in range(nk)},
        compiler_params=pltpu.CompilerParams(has_side_effects=EFFECT))(*shards, *lands, *sems, *after)
    return res[:nk], (res[nk], res[nk + 1])


def _gather_wait_relay(lands, sems):
    nk = len(lands)
    half = [t.shape[1] // 2 for t in lands]

    def body(*refs):
        land = refs[:nk]
        send, recv = refs[nk], refs[nk + 1]
        x, y, c = _my_pos()
        for r, (dx, dy) in enumerate(_CHIP_REL):
            chip = 2 * (x ^ dx) + (y ^ dy)
            for k in range(nk):
                mine = land[k].at[chip].at[pl.ds(c * half[k], half[k])]
                theirs = land[k].at[chip].at[pl.ds((1 - c) * half[k], half[k])]
                cp = pltpu.make_async_remote_copy(
                    src_ref=mine, dst_ref=theirs, send_sem=send.at[r * nk + k], recv_sem=recv.at[r * nk + k],
                    device_id=(x, y, 1 - c), device_id_type=MESH)
                cp.wait_send()
                cp.wait_recv()

    return pl.pallas_call(
        body, name="gather_wait_relay", out_shape=[pltpu.HBM(t.shape, t.dtype) for t in lands],
        in_specs=[HBM] * nk + [SEM, SEM], out_specs=[HBM] * nk, input_output_aliases={k: k for k in range(nk)},
        compiler_params=pltpu.CompilerParams(has_side_effects=EFFECT))(*lands, *sems)


def _gather_wait(name, l, shards, lands, sems, after):
    nk = len(shards)

    def body(*refs):
        ins, land = refs[:nk], refs[nk:2 * nk]
        send, recv = refs[2 * nk], refs[2 * nk + 1]
        x, y, c = _my_pos()
        for r, (dx, dy) in enumerate(_CHIP_REL):
            cx, cy = x ^ dx, y ^ dy
            for k in range(nk):
                cp = pltpu.make_async_remote_copy(
                    src_ref=ins[k].at[l], dst_ref=land[k].at[2 * cx + cy], send_sem=send.at[r * nk + k],
                    recv_sem=recv.at[r * nk + k], device_id=(cx, cy, c), device_id_type=MESH)
                cp.wait_send()
                cp.wait_recv()

    return pl.pallas_call(
        body, name=name, out_shape=[pltpu.HBM(t.shape, t.dtype) for t in lands],
        in_specs=[ANY] * nk + [HBM] * nk + [SEM, SEM, ANY], out_specs=[HBM] * nk,
        input_output_aliases={nk + k: k for k in range(nk)},
        compiler_params=pltpu.CompilerParams(has_side_effects=EFFECT))(*shards, *lands, *sems, after)


def _place_own(land, shard, l, tr):
    _, rows, cols = shard.shape[0], shard.shape[-2], shard.shape[-1]
    lead = shard.shape[1:-2]
    nlead = math.prod(lead)
    sh = shard.reshape((shard.shape[0], nlead, rows, cols))
    ld = land.reshape((NCHIP, nlead, rows, cols))
    me = (2 * lax.axis_index("x") + lax.axis_index("y")).astype(jnp.int32).reshape(1)

    def body(me_ref, s_ref, l_in, o_ref):
        o_ref[...] = s_ref[...]

    out = pl.pallas_call(
        body,
        grid_spec=pltpu.PrefetchScalarGridSpec(
            num_scalar_prefetch=1, grid=(nlead, rows // tr),
            in_specs=[pl.BlockSpec((None, None, tr, cols), lambda b, i, me_ref: (l, b, i, 0)), ANY],
            out_specs=pl.BlockSpec((None, None, tr, cols), lambda b, i, me_ref: (me_ref[0], b, i, 0))),
        out_shape=_sds(ld.shape, BF16), input_output_aliases={2: 0},
        compiler_params=_cp("parallel", "parallel"), name="place_own")(me, sh, ld)
    return out.reshape(land.shape)


def _exch_start(l, parts):
    nk = len(parts)

    def body(*refs):
        ins, outs = refs[:nk], refs[nk:2 * nk]
        send, recv, token = refs[2 * nk:]
        x, y, c = _my_pos()
        for r, (dx, dy, dc) in enumerate(_DEV_REL):
            px, py, pc = x ^ dx, y ^ dy, c ^ dc
            for k in range(nk):
                pltpu.make_async_remote_copy(
                    src_ref=ins[k].at[2 * px + py, pc], dst_ref=outs[k].at[r], send_sem=send.at[r * nk + k],
                    recv_sem=recv.at[r * nk + k], device_id=(px, py, pc), device_id_type=MESH).start()
        token[...] = jnp.zeros_like(token)

    res = pl.pallas_call(
        body, name=f"exch_start_{l}",
        out_shape=[pltpu.HBM((7,) + p.shape[2:], BF16) for p in parts] + [pltpu.SemaphoreType.DMA((N_EXCH,))] * 2 + [_sds((8, 128), F32)],
        in_specs=[HBM] * nk, out_specs=[HBM] * nk + [SEM, SEM, pl.BlockSpec(memory_space=pltpu.VMEM)],
        compiler_params=pltpu.CompilerParams(has_side_effects=EFFECT))(*[_in_hbm(p) for p in parts])
    return res[:nk], (res[nk], res[nk + 1]), res[-1]


def _exch_wait(l, parts, lands, sems, after):
    nk = len(parts)

    def body(*refs):
        ins, land = refs[:nk], refs[nk:2 * nk]
        send, recv = refs[2 * nk], refs[2 * nk + 1]
        x, y, c = _my_pos()
        for r, (dx, dy, dc) in enumerate(_DEV_REL):
            px, py, pc = x ^ dx, y ^ dy, c ^ dc
            for k in range(nk):
                cp = pltpu.make_async_remote_copy(
                    src_ref=ins[k].at[2 * px + py, pc], dst_ref=land[k].at[r], send_sem=send.at[r * nk + k],
                    recv_sem=recv.at[r * nk + k], device_id=(px, py, pc), device_id_type=MESH)
                cp.wait_send()
                cp.wait_recv()

    return pl.pallas_call(
        body, name=f"exch_wait_{l}", out_shape=[pltpu.HBM(t.shape, t.dtype) for t in lands],
        in_specs=[ANY] * nk + [HBM] * nk + [SEM, SEM, ANY], out_specs=[HBM] * nk,
        input_output_aliases={nk + k: k for k in range(nk)},
        compiler_params=pltpu.CompilerParams(has_side_effects=EFFECT))(*parts, *lands, *sems, after)


def _chip_half():
    x, y, c = _my_pos()
    return jnp.stack([2 * x + y, c]).astype(jnp.int32)


def _sum_half(part, land, tr):
    _, _, r2, cols = part.shape

    def body(pos_ref, p_ref, r_ref, o_ref):
        acc = p_ref[...].astype(F32)
        for r in range(7):
            acc = acc + r_ref[r].astype(F32)
        o_ref[...] = acc

    return pl.pallas_call(
        body,
        grid_spec=pltpu.PrefetchScalarGridSpec(
            num_scalar_prefetch=1, grid=(r2 // tr,),
            in_specs=[pl.BlockSpec((None, None, tr, cols), lambda i, pos: (pos[0], pos[1], i, 0)),
                      pl.BlockSpec((7, tr, cols), lambda i, pos: (0, i, 0))],
            out_specs=pl.BlockSpec((None, tr, cols), lambda i, pos: (pos[1], i, 0))),
        out_shape=_sds((2, r2, cols), F32), compiler_params=_cp("parallel"), name="sum_half")(_chip_half(), part, land)


def _share_full(fulls):
    nk = len(fulls)

    def body(*refs):
        ins, outs = refs[:nk], refs[nk:2 * nk]
        send, recv = refs[2 * nk:]
        x, y, c = _my_pos()

        def copy(k, hf):
            return pltpu.make_async_remote_copy(
                src_ref=ins[k].at[hf], dst_ref=outs[k].at[hf], send_sem=send.at[k], recv_sem=recv.at[k],
                device_id=(x, y, 1 - c), device_id_type=MESH)

        for k in range(nk):
            copy(k, c).start()
        for k in range(nk):
            copy(k, 1 - c).wait_recv()
        for k in range(nk):
            copy(k, c).wait_send()

    return pl.pallas_call(
        body, out_shape=[_sds(f.shape, F32) for f in fulls], in_specs=[ANY] * nk, out_specs=[ANY] * nk,
        scratch_shapes=[pltpu.SemaphoreType.DMA((nk,))] * 2, input_output_aliases={k: k for k in range(nk)},
        name="share_full")(*fulls)


def _small_start(packed, name):
    def body(in_ref, out_ref, send, recv, token):
        x, y, c = _my_pos()
        for r, (dx, dy, dc) in enumerate(_DEV_REL):
            pltpu.make_async_remote_copy(
                src_ref=in_ref, dst_ref=out_ref.at[r], send_sem=send.at[r], recv_sem=recv.at[r],
                device_id=(x ^ dx, y ^ dy, c ^ dc), device_id_type=MESH).start()
        token[...] = jnp.zeros_like(token)

    res = pl.pallas_call(
        body, name=name,
        out_shape=[pltpu.HBM((7,) + packed.shape, F32)] + [pltpu.SemaphoreType.DMA((7,))] * 2 + [_sds((8, 128), F32)],
        in_specs=[HBM], out_specs=[HBM, SEM, SEM, pl.BlockSpec(memory_space=pltpu.VMEM)],
        compiler_params=pltpu.CompilerParams(has_side_effects=EFFECT))(_in_hbm(packed))
    return res[0], (res[1], res[2]), res[3]


def _small_wait(packed, land, sems, after, name):
    def body(in_ref, land_ref, send, recv, after_ref, out_ref):
        x, y, c = _my_pos()
        for r, (dx, dy, dc) in enumerate(_DEV_REL):
            cp = pltpu.make_async_remote_copy(
                src_ref=in_ref, dst_ref=land_ref.at[r], send_sem=send.at[r], recv_sem=recv.at[r],
                device_id=(x ^ dx, y ^ dy, c ^ dc), device_id_type=MESH)
            cp.wait_send()
            cp.wait_recv()

    return pl.pallas_call(
        body, name=name, out_shape=pltpu.HBM(land.shape, land.dtype),
        in_specs=[ANY, HBM, SEM, SEM, ANY], out_specs=HBM, input_output_aliases={1: 0},
        compiler_params=pltpu.CompilerParams(has_side_effects=EFFECT))(packed, land, *sems, after)


def _sum_small(packed, land, tr):
    rows = packed.shape[0]
    x, y, c = _my_pos()
    me = (4 * x + 2 * y + c).astype(jnp.int32).reshape(1)

    def sbody(me_ref, p_ref, r_ref, o_ref):
        me_dev = me_ref[0]
        own = p_ref[...]
        acc = None
        for s in range(NDEV):
            rel = s ^ me_dev
            v = jnp.where(rel == 0, own, r_ref[jnp.maximum(rel - 1, 0)])
            acc = v if acc is None else acc + v
        o_ref[...] = acc

    return pl.pallas_call(
        sbody,
        grid_spec=pltpu.PrefetchScalarGridSpec(
            num_scalar_prefetch=1, grid=(rows // tr,),
            in_specs=[pl.BlockSpec((tr, 128), lambda i, me_ref: (i, 0)), pl.BlockSpec((7, tr, 128), lambda i, me_ref: (0, i, 0))],
            out_specs=pl.BlockSpec((tr, 128), lambda i, me_ref: (i, 0))),
        out_shape=_sds((rows, 128), F32), compiler_params=_cp("parallel"), name="sum_small")(me, packed, land)


def _rms_fwd(x, g, tm):
    n = x.shape[0]

    def body(x_ref, g_ref, h_ref):
        xv = x_ref[...]
        r = lax.rsqrt(jnp.mean(xv * xv, axis=-1, keepdims=True) + EPS)
        h_ref[...] = (xv * r * g_ref[...]).astype(BF16)

    return pl.pallas_call(
        body, grid=(n // tm,),
        in_specs=[pl.BlockSpec((tm, D), lambda i: (i, 0)), pl.BlockSpec((1, D), lambda i: (0, 0))],
        out_specs=pl.BlockSpec((tm, D), lambda i: (i, 0)), out_shape=_sds((n, D), BF16),
        compiler_params=_cp("parallel"), name="rms_fwd")(x, g)


def _rms_fwd_t(x, g, tm=512):
    n = x.shape[0]

    def body(x_ref, g_ref, h_ref, ht_ref):
        xv = x_ref[...]
        r = lax.rsqrt(jnp.mean(xv * xv, axis=-1, keepdims=True) + EPS)
        h = xv * r * g_ref[...]
        h_ref[...] = h.astype(BF16)
        ht_ref[...] = h.T.astype(BF16)

    return pl.pallas_call(
        body, grid=(n // tm,),
        in_specs=[pl.BlockSpec((tm, D), lambda i: (i, 0)), pl.BlockSpec((1, D), lambda i: (0, 0))],
        out_specs=[pl.BlockSpec((tm, D), lambda i: (i, 0)), pl.BlockSpec((D, tm), lambda i: (0, i))],
        out_shape=[_sds((n, D), BF16), _sds((D, n), BF16)], compiler_params=_cp("parallel"), name="rms_fwd_t")(x, g)


def _rms_bwd(dh, x, g, dres, tm=512):
    n = x.shape[0]

    def body(dh_ref, x_ref, g_ref, dr_ref, dx_ref, dg_ref):
        i = pl.program_id(0)
        xv = x_ref[...]
        r = lax.rsqrt(jnp.mean(xv * xv, axis=-1, keepdims=True) + EPS)
        xh = xv * r
        dhv = dh_ref[...]
        dxh = dhv * g_ref[...]
        dx_ref[...] = dr_ref[...] + r * (dxh - xh * jnp.mean(dxh * xh, axis=-1, keepdims=True))
        part = jnp.sum(dhv * xh, axis=0, keepdims=True)

        @pl.when(i == 0)
        def _():
            dg_ref[...] = part

        @pl.when(i > 0)
        def _():
            dg_ref[...] += part

    row = pl.BlockSpec((tm, D), lambda i: (i, 0))
    vec = pl.BlockSpec((1, D), lambda i: (0, 0))
    return pl.pallas_call(
        body, grid=(n // tm,), in_specs=[row, row, vec, row], out_specs=[row, vec],
        out_shape=[_sds((n, D), F32), _sds((1, D), F32)], compiler_params=_cp("arbitrary"), name="rms_bwd")(dh, x, g, dres)


def _loss_head(x, tgt, g, tm=512):
    def body(x_ref, t_ref, g_ref, dx_ref, dg_ref, ls_ref):
        i = pl.program_id(0)
        xv = x_ref[...]
        r = lax.rsqrt(jnp.mean(xv * xv, axis=-1, keepdims=True) + EPS)
        xh = xv * r
        gv = g_ref[...]
        diff = xh * gv - t_ref[...]
        dy = diff * (1.0 / D)
        dxh = dy * gv
        dx_ref[...] = r * (dxh - xh * jnp.mean(dxh * xh, axis=-1, keepdims=True))
        part_g = jnp.sum(dy * xh, axis=0, keepdims=True)
        part_l = jnp.sum(diff * diff, axis=0, keepdims=True)

        @pl.when(i == 0)
        def _():
            dg_ref[...] = part_g
            ls_ref[...] = part_l

        @pl.when(i > 0)
        def _():
            dg_ref[...] += part_g
            ls_ref[...] += part_l

        @pl.when(i == pl.num_programs(0) - 1)
        def _():
            tot = jnp.sum(ls_ref[...], axis=-1, keepdims=True) * (0.5 / D)
            ls_ref[...] = jnp.broadcast_to(tot, (1, D))

    row = pl.BlockSpec((tm, D), lambda i: (i, 0))
    vec = pl.BlockSpec((1, D), lambda i: (0, 0))
    return pl.pallas_call(
        body, grid=(S // tm,), in_specs=[row, row, vec], out_specs=[row, vec, vec],
        out_shape=[_sds((S, D), F32), _sds((1, D), F32), _sds((1, D), F32)],
        compiler_params=_cp("arbitrary"), name="loss_head")(x, tgt, g)


def _adamw(w, g, m, v):
    shape = w.shape
    cols = shape[-1] if w.ndim > 1 else shape[0]
    rows = w.size // cols
    w2, g2, m2, v2 = (t.reshape(rows, cols) for t in (w, g, m, v))
    tr = rows
    while tr * cols * 4 > (1 << 20) and tr % 16 == 0:
        tr //= 2
    c1 = 1.0 - ADAM_B1 ** ADAM_STEP
    c2 = 1.0 - ADAM_B2 ** ADAM_STEP

    def body(w_ref, g_ref, m_ref, v_ref, d_ref, nm_ref, nv_ref):
        gv = g_ref[...]
        mn = ADAM_B1 * m_ref[...] + (1.0 - ADAM_B1) * gv
        vn = ADAM_B2 * v_ref[...] + (1.0 - ADAM_B2) * (gv * gv)
        d_ref[...] = -ADAM_LR * ((mn / c1) / (jnp.sqrt(vn / c2) + ADAM_EPS) + ADAM_WD * w_ref[...])
        nm_ref[...] = mn
        nv_ref[...] = vn

    blk = pl.BlockSpec((tr, cols), lambda i: (i, 0))
    outs = pl.pallas_call(
        body, grid=(rows // tr,), in_specs=[blk] * 4, out_specs=[blk] * 3,
        out_shape=[_sds((rows, cols), F32)] * 3, compiler_params=_cp("parallel"), name="adamw")(w2, g2, m2, v2)
    return tuple(o.reshape(shape) for o in outs)


def _adamw_layer(l, w, g, m, v, outs, tr):
    cols = w.shape[-1]
    rows = w.size // (NL * cols)
    nb = rows // tr
    w2, m2, v2 = (t.reshape(NL * rows, cols) for t in (w, m, v))
    g2 = g.reshape(rows, cols)
    c1 = 1.0 - ADAM_B1 ** ADAM_STEP
    c2 = 1.0 - ADAM_B2 ** ADAM_STEP

    def body(w_ref, g_ref, m_ref, v_ref, d_in, nm_in, nv_in, go_in, d_ref, nm_ref, nv_ref, go_ref):
        gv = g_ref[...]
        mn = ADAM_B1 * m_ref[...] + (1.0 - ADAM_B1) * gv
        vn = ADAM_B2 * v_ref[...] + (1.0 - ADAM_B2) * (gv * gv)
        d_ref[...] = -ADAM_LR * ((mn / c1) / (jnp.sqrt(vn / c2) + ADAM_EPS) + ADAM_WD * w_ref[...])
        nm_ref[...] = mn
        nv_ref[...] = vn
        go_ref[...] = gv

    lay = pl.BlockSpec((tr, cols), lambda i: (l * nb + i, 0))
    return pl.pallas_call(
        body, grid=(nb,), in_specs=[lay, pl.BlockSpec((tr, cols), lambda i: (i, 0)), lay, lay] + [ANY] * 4,
        out_specs=[lay] * 4, out_shape=[_sds((NL * rows, cols), F32)] * 4,
        input_output_aliases={4: 0, 5: 1, 6: 2, 7: 3}, compiler_params=_cp("parallel"), name="adamw_layer")(w2, g2, m2, v2, *outs)


def _proj_fwd(h, wg, tm=512):
    def body(h_ref, w_ref, o_ref):
        o_ref[...] = _dot(h_ref[...], w_ref[...]).astype(BF16)

    return pl.pallas_call(
        body, grid=(NCHIP, S // tm),
        in_specs=[pl.BlockSpec((tm, D), lambda c, i: (i, 0)), pl.BlockSpec((None, D, CW), lambda c, i: (c, 0, 0))],
        out_specs=pl.BlockSpec((tm, CW), lambda c, i: (i, c)), out_shape=_sds((S, DIN), BF16),
        compiler_params=_cp("parallel", "parallel"), name="proj_fwd")(h, wg)


def _proj_bwd_x(dproj, wg, dep, tm=1024):
    def body(d_ref, w_ref, dep_ref, o_ref):
        k = pl.program_id(1)
        part = _dot_nt(d_ref[...], w_ref[...])

        @pl.when(k == 0)
        def _():
            o_ref[...] = part

        @pl.when(k > 0)
        def _():
            o_ref[...] += part

    return pl.pallas_call(
        body, grid=(S // tm, NCHIP),
        in_specs=[pl.BlockSpec((tm, CW), lambda i, k: (i, k)), pl.BlockSpec((None, D, CW), lambda i, k: (k, 0, 0)), ANY],
        out_specs=pl.BlockSpec((tm, D), lambda i, k: (i, 0)), out_shape=_sds((S, D), F32),
        compiler_params=_cp("parallel", "arbitrary"), name="proj_bwd_x")(dproj, wg, dep)


def _proj_bwd_w(ht, dproj):
    def body(h_ref, d_ref, o_ref):
        o_ref[...] = _dot(h_ref[...], d_ref[...]).astype(BF16)

    return pl.pallas_call(
        body, grid=(NCHIP, NJ),
        in_specs=[pl.BlockSpec((D, S), lambda c, j: (0, 0)), pl.BlockSpec((S, TN_IN), lambda c, j: (0, c * NJ + j))],
        out_specs=pl.BlockSpec((None, D, TN_IN), lambda c, j: (c, 0, j)), out_shape=_sds((NCHIP, D, CW), BF16),
        compiler_params=_cp("parallel", "parallel"), name="proj_bwd_w")(ht, dproj)


def _merge_fwd(y_all, wbr, proj, tm=256):
    cb = D // NCHIP

    def body(y_ref, w_ref, *rest):
        g_refs, z_ref = rest[:8], rest[8]
        for c in range(NCHIP):
            acc = None
            for b in range(4):
                g = g_refs[2 * b + c // 2][:, (c % 2) * cb:(c % 2 + 1) * cb].astype(F32)
                t = _dot(y_ref[b], w_ref[c, b]) * _sigmoid(g)
                acc = t if acc is None else acc + t
            z_ref[:, c * cb:(c + 1) * cb] = acc.astype(BF16)

    g_specs = [pl.BlockSpec((tm, W), functools.partial(lambda j, i: (i, P_GM + j), j)) for j in range(8)]
    return pl.pallas_call(
        body, grid=(S // tm,),
        in_specs=[pl.BlockSpec((4, tm, W), lambda i: (0, i, 0)), pl.BlockSpec((NCHIP, 4, W, cb), lambda i: (0, 0, 0, 0))] + g_specs,
        out_specs=pl.BlockSpec((tm, D), lambda i: (i, 0)), out_shape=_sds((S, D), BF16),
        compiler_params=_cp("parallel"), name="merge_fwd")(y_all, wbr, *([proj] * 8))


def _merge_bwd(dz, y_all, wbr, proj, dproj, tm=512):
    cb = D // NCHIP
    ni = S // tm

    def body(dz_ref, y_ref, w_ref, ga_ref, gb_ref, dp_in, dp_ref, dy_ref, dw_ref, acc, obuf, osem):
        b = pl.program_id(0)
        i = pl.program_id(1)

        @pl.when(i == 0)
        def _():
            acc[...] = jnp.zeros_like(acc)

        yv = y_ref[...]
        dys = []

        def fill(slot):
            ws_ = [w_ref[c] for c in range(NCHIP)]
            t = [_dot(yv, wv) for wv in ws_]
            dts = []
            for c in range(NCHIP):
                g_ref = ga_ref if c < 2 else gb_ref
                g = _sigmoid(g_ref[:, (c % 2) * cb:(c % 2 + 1) * cb].astype(F32))
                dzc = dz_ref[:, c * cb:(c + 1) * cb].astype(F32)
                slot[:, c * cb:(c + 1) * cb] = (dzc * t[c] * g * (1.0 - g)).astype(BF16)
                dts.append((dzc * g).astype(BF16))
            dy = None
            for c in range(NCHIP):
                part = _dot_nt(dts[c], ws_[c])
                dy = part if dy is None else dy + part
            for c in range(NCHIP):
                acc[c] += _dot_tn(yv, dts[c])
            dys.append(dy)

        _tile_put(obuf, osem, lambda st: dp_ref.at[pl.ds((st % ni) * tm, tm), pl.ds(P_GM * W + (st // ni) * D, D)],
                  b * ni + i, 4 * ni, fill)
        dy_ref[...] = dys[0].astype(BF16)

        @pl.when(i == ni - 1)
        def _():
            dw_ref[...] = acc[...].astype(BF16)

    return pl.pallas_call(
        body, grid=(4, ni),
        in_specs=[pl.BlockSpec((tm, D), lambda b, i: (i, 0)), pl.BlockSpec((None, tm, W), lambda b, i: (b, i, 0)),
                  pl.BlockSpec((NCHIP, None, W, cb), lambda b, i: (0, b, 0, 0)),
                  pl.BlockSpec((tm, W), lambda b, i: (i, P_GM + 2 * b)), pl.BlockSpec((tm, W), lambda b, i: (i, P_GM + 2 * b + 1)), ANY],
        out_specs=[ANY, pl.BlockSpec((None, tm, W), lambda b, i: (b, i, 0)), pl.BlockSpec((NCHIP, None, W, cb), lambda b, i: (0, b, 0, 0))],
        out_shape=[_sds((S, DIN), BF16), _sds((4, S, W), BF16), _sds((NCHIP, 4, W, cb), BF16)],
        scratch_shapes=[pltpu.VMEM((NCHIP, W, cb), F32), pltpu.VMEM((2, tm, D), BF16), pltpu.SemaphoreType.DMA((2,))],
        input_output_aliases={5: 0}, compiler_params=_cp("arbitrary", "arbitrary"), name="merge_bwd")(dz, y_all, wbr, proj, proj, dproj)


def _out_fwd(z, wo, x, tm=512):
    def body(z_ref, w_ref, x_ref, o_ref):
        o_ref[...] = x_ref[...] + _dot(z_ref[...], w_ref[...])

    row = pl.BlockSpec((tm, D), lambda i: (i, 0))
    return pl.pallas_call(
        body, grid=(S // tm,), in_specs=[row, pl.BlockSpec((D, D), lambda i: (0, 0)), row], out_specs=row,
        out_shape=_sds((S, D), F32), compiler_params=_cp("parallel"), name="out_fwd")(z, wo, x)


def _out_bwd(dx, z, wo, dep, tm=512):
    ni = S // tm

    def body(dx_ref, z_ref, w_ref, dep_ref, dz_ref, dw_ref, acc):
        i = pl.program_id(0)
        dxb = dx_ref[...].astype(BF16)
        dz_ref[...] = _dot_nt(dxb, w_ref[...]).astype(BF16)
        part = _dot_tn(z_ref[...], dxb)

        @pl.when(i == 0)
        def _():
            acc[...] = part

        @pl.when(i > 0)
        def _():
            acc[...] += part

        @pl.when(i == ni - 1)
        def _():
            dw_ref[...] = acc[...].astype(BF16)

    row = pl.BlockSpec((tm, D), lambda i: (i, 0))
    full = pl.BlockSpec((D, D), lambda i: (0, 0))
    return pl.pallas_call(
        body, grid=(ni,), in_specs=[row, row, full, ANY], out_specs=[row, full],
        out_shape=[_sds((S, D), BF16), _sds((D, D), BF16)], scratch_shapes=[pltpu.VMEM((D, D), F32)],
        compiler_params=_cp("arbitrary"), name="out_bwd")(dx, z, wo, dep)


def _gelu_parts(a):
    cdf = 0.5 * (1.0 + lax.erf(a * INV_SQRT2))
    return a * cdf, cdf


def _ln_parts(v):
    mu = jnp.mean(v, axis=-1, keepdims=True)
    vc = v - mu
    rs = lax.rsqrt(jnp.mean(vc * vc, axis=-1, keepdims=True) + EPS)
    return vc * rs, rs


def _causal_mask():
    return lax.broadcasted_iota(jnp.int32, (HD, HD), 0) >= lax.broadcasted_iota(jnp.int32, (HD, HD), 1)


def _gmlp_fwd(proj, lg, lb, ws, bias, y_all, tm=512):
    def body(uv_ref, gt_ref, lg_ref, lb_ref, ws_ref, b_ref, y_in, y_ref):
        act, _ = _gelu_parts(uv_ref[...].astype(F32))
        u = act[:, :W]
        xh, _ = _ln_parts(act[:, W:])
        vn = (xh * lg_ref[...] + lb_ref[...]).astype(BF16)
        gt = gt_ref[...].astype(F32)
        us = u * (gt * _sigmoid(gt))
        mask = _causal_mask()
        for h in range(4):
            wm = jnp.where(mask, ws_ref[h], 0.0).astype(BF16)
            cs = slice(h * HD, (h + 1) * HD)
            for c in range(tm // HD):
                rs_ = slice(c * HD, (c + 1) * HD)
                mixed = _dot(wm, vn[rs_, cs]) + b_ref[h]
                y_ref[rs_, cs] = (us[rs_, cs] * mixed).astype(BF16)

    vec = pl.BlockSpec((1, W), lambda i: (0, 0))
    mats = pl.BlockSpec((4, HD, HD), lambda i: (0, 0, 0))
    return pl.pallas_call(
        body, grid=(S // tm,),
        in_specs=[pl.BlockSpec((tm, 2 * W), lambda i: (i, 0)), pl.BlockSpec((tm, W), lambda i: (i, P_AGATE)), vec, vec, mats, mats, ANY],
        out_specs=pl.BlockSpec((None, tm, W), lambda i: (0, i, 0)), out_shape=_sds((4, S, W), BF16),
        input_output_aliases={6: 0}, compiler_params=_cp("parallel"), name="gmlp_fwd")(proj, proj, lg, lb, ws, bias, y_all)


def _gmlp_bwd(proj, dy_all, lg, lb, ws, bias, dproj, tm=256):
    ni = S // tm

    def body(uv_ref, gt_ref, dy_ref, lg_ref, lb_ref, ws_ref, b_ref, dp_in, dp_ref, dws_ref, dbs_ref, dlg_ref, dlb_ref, mix_s, dvn_s):
        i = pl.program_id(0)

        @pl.when(i == 0)
        def _():
            dws_ref[...] = jnp.zeros_like(dws_ref)
            dbs_ref[...] = jnp.zeros_like(dbs_ref)
            dlg_ref[...] = jnp.zeros_like(dlg_ref)
            dlb_ref[...] = jnp.zeros_like(dlb_ref)

        a0 = uv_ref[...].astype(F32)
        act, cdf = _gelu_parts(a0)
        u = act[:, :W]
        xh, rs = _ln_parts(act[:, W:])
        lgv = lg_ref[...]
        vn = (xh * lgv + lb_ref[...]).astype(BF16)
        mask = _causal_mask()
        wms = [jnp.where(mask, ws_ref[h], 0.0).astype(BF16) for h in range(4)]
        blocks = [(slice(c * HD, (c + 1) * HD), slice(h * HD, (h + 1) * HD), h) for h in range(4) for c in range(tm // HD)]
        for rs_, cs, h in blocks:
            mix_s[rs_, cs] = _dot(wms[h], vn[rs_, cs]) + b_ref[h]
        mixed = mix_s[...]
        gt = gt_ref[...].astype(F32)
        sg = _sigmoid(gt)
        sl = gt * sg
        dyv = dy_ref[...].astype(F32)
        dum = dyv * sl
        dgate = dyv * (u * mixed) * (sg * (1.0 + gt * (1.0 - sg)))
        du = dum * mixed
        dmix = dum * u
        dmb = dmix.astype(BF16)
        for rs_, cs, h in blocks:
            dvn_s[rs_, cs] = _dot_tn(wms[h], dmb[rs_, cs])
        for rs_, cs, h in blocks:
            dws_ref[h] += _dot_nt(dmb[rs_, cs], vn[rs_, cs])
            dbs_ref[h] += dmix[rs_, cs]
        dvn = dvn_s[...]
        dlg_ref[...] += jnp.sum(dvn * xh, axis=0, keepdims=True)
        dlb_ref[...] += jnp.sum(dvn, axis=0, keepdims=True)
        dxh = dvn * lgv
        dv = rs * (dxh - jnp.mean(dxh, axis=-1, keepdims=True) - xh * jnp.mean(dxh * xh, axis=-1, keepdims=True))
        gp = cdf + a0 * (jnp.exp(-0.5 * a0 * a0) * INV_SQRT2PI)
        dp_ref[:, :W] = (du * gp[:, :W]).astype(BF16)
        dp_ref[:, W:2 * W] = (dv * gp[:, W:]).astype(BF16)
        dp_ref[:, 2 * W:] = dgate.astype(BF16)

        @pl.when(i == ni - 1)
        def _():
            for h in range(4):
                dws_ref[h] = jnp.where(mask, dws_ref[h], 0.0)
                dbs_ref[h] = jnp.broadcast_to(jnp.sum(dbs_ref[h], axis=1, keepdims=True), (HD, HD))

    vec = pl.BlockSpec((1, W), lambda i: (0, 0))
    mats = pl.BlockSpec((4, HD, HD), lambda i: (0, 0, 0))
    return pl.pallas_call(
        body, grid=(ni,),
        in_specs=[pl.BlockSpec((tm, 2 * W), lambda i: (i, 0)), pl.BlockSpec((tm, W), lambda i: (i, P_AGATE)),
                  pl.BlockSpec((None, tm, W), lambda i: (0, i, 0)), vec, vec, mats, mats, ANY],
        out_specs=[pl.BlockSpec((tm, 3 * W), lambda i: (i, 0)), mats, mats, vec, vec],
        out_shape=[_sds((S, DIN), BF16), _sds((4, HD, HD), F32), _sds((4, HD, HD), F32), _sds((1, W), F32), _sds((1, W), F32)],
        scratch_shapes=[pltpu.VMEM((tm, W), F32), pltpu.VMEM((tm, W), F32)],
        input_output_aliases={7: 0}, compiler_params=_cp("arbitrary"), name="gmlp_bwd")(proj, proj, dy_all, lg, lb, ws, bias, dproj)


def _pool_diff(p, halo, row0, tm):
    xx = jnp.concatenate([halo, p], axis=0)
    t1 = (row0 + 1 + lax.broadcasted_iota(jnp.int32, (tm, 1), 0)).astype(F32)
    out = []
    for g, win in enumerate(POOL_WINDOWS):
        s = xx[:, g * HD:(g + 1) * HD]
        sh = 1
        while sh < win:
            s = s + pltpu.roll(s, sh, 0)
            sh *= 2
        out.append(s[HALO:] / jnp.minimum(t1, float(win)) - p[:, g * HD:(g + 1) * HD])
    return out


def _pool_fwd(proj, pw, sc, y_all, tm=512):
    rb = tm // HALO

    def body(p_ref, h_ref, gt_ref, pw_ref, sc_ref, y_in, y_ref):
        i = pl.program_id(0)
        halo = jnp.where(i > 0, h_ref[...].astype(F32), 0.0)
        ds = _pool_diff(p_ref[...].astype(F32), halo, i * tm, tm)
        gt = gt_ref[...].astype(F32)
        sl = gt * _sigmoid(gt)
        for g in range(4):
            cs = slice(g * HD, (g + 1) * HD)
            lin = _dot(ds[g].astype(BF16), pw_ref[g].astype(BF16))
            y_ref[:, cs] = (lin * sc_ref[:, cs] * sl[:, cs]).astype(BF16)

    return pl.pallas_call(
        body, grid=(S // tm,),
        in_specs=[pl.BlockSpec((tm, W), lambda i: (i, P_PIN)),
                  pl.BlockSpec((HALO, W), lambda i: (jnp.maximum(i * rb - 1, 0), P_PIN)),
                  pl.BlockSpec((tm, W), lambda i: (i, P_PGATE)),
                  pl.BlockSpec((4, HD, HD), lambda i: (0, 0, 0)), pl.BlockSpec((1, W), lambda i: (0, 0)), ANY],
        out_specs=pl.BlockSpec((None, tm, W), lambda i: (1, i, 0)), out_shape=_sds((4, S, W), BF16),
        input_output_aliases={5: 0}, compiler_params=_cp("parallel"), name="pool_fwd")(proj, proj, proj, pw, sc, y_all)


def _pool_bwd(proj, dy_all, pw, sc, dproj, tm=256):
    ni = S // tm
    rb = tm // HALO
    last_rb = S // HALO - 1
    rx = tm + HALO

    def body(p_ref, h_ref, gt_ref, gh_ref, dy_ref, dyh_ref, pw_ref, sc_ref, dp_in, dp_ref, dpw_ref, dsc_ref, obuf, osem):
        i = pl.program_id(0)

        @pl.when(i == 0)
        def _():
            dpw_ref[...] = jnp.zeros_like(dpw_ref)
            dsc_ref[...] = jnp.zeros_like(dsc_ref)

        halo = jnp.where(i > 0, h_ref[...].astype(F32), 0.0)
        ds = _pool_diff(p_ref[...].astype(F32), halo, i * tm, tm)
        nxt = i < ni - 1
        gx = jnp.concatenate([gt_ref[...], gh_ref[...]], axis=0).astype(F32)
        dyx = jnp.concatenate([dy_ref[...].astype(F32), jnp.where(nxt, dyh_ref[...].astype(F32), 0.0)], axis=0)
        sgx = _sigmoid(gx)
        slx = gx * sgx
        scv = sc_ref[...]
        dlinx = dyx * slx * scv
        t1 = (i * tm + 1 + lax.broadcasted_iota(jnp.int32, (rx, 1), 0)).astype(F32)
        gt, sg, sl, dyv = gx[:tm], sgx[:tm], slx[:tm], dyx[:tm]
        dsl = sg * (1.0 + gt * (1.0 - sg))

        def fill(slot):
            for g, win in enumerate(POOL_WINDOWS):
                cs = slice(g * HD, (g + 1) * HD)
                wv = pw_ref[g].astype(BF16)
                dlb = dlinx[:, cs].astype(BF16)
                ddx = _dot_nt(dlb, wv)
                f = ddx / jnp.minimum(t1, float(win))
                sh = 1
                while sh < win:
                    f = f + pltpu.roll(f, rx - sh, 0)
                    sh *= 2
                slot[:, cs] = (f[:tm] - ddx[:tm]).astype(BF16)
                db = ds[g].astype(BF16)
                lin = _dot(db, wv)
                slot[:, W + g * HD:W + (g + 1) * HD] = (dyv[:, cs] * lin * scv[:, cs] * dsl[:, cs]).astype(BF16)
                dsc_ref[:, cs] += jnp.sum(dyv[:, cs] * sl[:, cs] * lin, axis=0, keepdims=True)
                dpw_ref[g] += _dot_tn(db, dlb[:tm])

        _tile_put(obuf, osem, lambda st: dp_ref.at[pl.ds(st * tm, tm), pl.ds(P_PIN * W, 2 * W)], i, ni, fill)

    mats = pl.BlockSpec((4, HD, HD), lambda i: (0, 0, 0))
    vec = pl.BlockSpec((1, W), lambda i: (0, 0))
    return pl.pallas_call(
        body, grid=(ni,),
        in_specs=[pl.BlockSpec((tm, W), lambda i: (i, P_PIN)),
                  pl.BlockSpec((HALO, W), lambda i: (jnp.maximum(i * rb - 1, 0), P_PIN)),
                  pl.BlockSpec((tm, W), lambda i: (i, P_PGATE)),
                  pl.BlockSpec((HALO, W), lambda i: (jnp.minimum((i + 1) * rb, last_rb), P_PGATE)),
                  pl.BlockSpec((None, tm, W), lambda i: (1, i, 0)),
                  pl.BlockSpec((None, HALO, W), lambda i: (1, jnp.minimum((i + 1) * rb, last_rb), 0)),
                  mats, vec, ANY],
        out_specs=[ANY, mats, vec],
        out_shape=[_sds((S, DIN), BF16), _sds((4, HD, HD), F32), _sds((1, W), F32)],
        scratch_shapes=[pltpu.VMEM((2, tm, 2 * W), BF16), pltpu.SemaphoreType.DMA((2,))],
        input_output_aliases={8: 0}, compiler_params=_cp("arbitrary"), name="pool_bwd")(proj, proj, proj, proj, dy_all, dy_all, pw, sc, dproj)


ATT_STEP = ((1, 4), (4, 1), (4, 1))
ATT_GROUP = 16
ATT_GROUP_BWD = 8


def _att_band():
    qi = lax.broadcasted_iota(jnp.int32, (HD, 2 * HD), 0)
    kj = lax.broadcasted_iota(jnp.int32, (HD, 2 * HD), 1)
    return jnp.logical_and(kj >= qi, kj <= qi + HD), kj < HD


def _att_keys(kp_ref, ko_ref, vp_ref, vo_ref, a, jj):
    if jj == 0:
        return (jnp.concatenate([kp_ref[a], ko_ref[a, :HD, :]], axis=0), jnp.concatenate([vp_ref[a], vo_ref[a, :HD, :]], axis=0))
    return ko_ref[a, (jj - 1) * HD:(jj + 1) * HD, :], vo_ref[a, (jj - 1) * HD:(jj + 1) * HD, :]


def _dilate(src, dst, d, rows, cast=None):
    for r in range(d):
        for h in range(4):
            v = src.at[h][pl.ds(r, rows // d, stride=d), :] if d > 1 else src[h]
            dst[r * 4 + h] = v if cast is None else v.astype(cast)


def _undilate(src, dst, d, rows):
    for r in range(d):
        for h in range(4):
            if d > 1:
                dst.at[h][pl.ds(r, rows // d, stride=d), :] = src[r * 4 + h].astype(F32)
            else:
                dst[h] = src[h].astype(F32)


def _dil_spec(d, tm):
    return pl.BlockSpec((4 * d, tm // d, HD), lambda i: (0, i, 0))


def _att_prep(proj, tm=512):
    def body(q0, q1, q2, k_ref, v_ref, *rest):
        outs, scr = rest[:9], rest[9]
        for j, (src, dsts) in enumerate(((q0, ((0, outs[0]),)), (q1, ((1, outs[1]),)), (q2, ((2, outs[2]),)),
                                         (k_ref, tuple((g, outs[3 + g]) for g in range(3))),
                                         (v_ref, tuple((g, outs[6 + g]) for g in range(3))))):
            for h in range(4):
                scr[j, h] = src[:, h * HD:(h + 1) * HD].astype(F32)
            for g, dst in dsts:
                _dilate(scr.at[j], dst, DILATIONS[g], tm, BF16)

    def piece(p):
        return pl.BlockSpec((tm, W), lambda i: (i, p))

    shapes = [_sds((4 * d, S // d, HD), BF16) for d in DILATIONS]
    res = pl.pallas_call(
        body, grid=(S // tm,),
        in_specs=[piece(P_CQ), piece(P_CQ + 1), piece(P_CQ + 2), piece(P_CK), piece(P_CV)],
        out_specs=[_dil_spec(d, tm) for d in DILATIONS] * 3, out_shape=shapes * 3,
        scratch_shapes=[pltpu.VMEM((5, 4, tm, HD), F32)],
        compiler_params=_cp("parallel"), name="att_prep")(proj, proj, proj, proj, proj)
    return res[0:3], res[3:6], res[6:9]


def _att_specs(g):
    d = DILATIONS[g]
    nres, njb = ATT_STEP[g]
    nb = S // d // HD
    own = pl.BlockSpec((4 * nres, njb * HD, HD), lambda r, j: (r, j, 0))
    prev = pl.BlockSpec((4 * nres, HD, HD), lambda r, j: (r, jnp.maximum(j * njb - 1, 0), 0))
    nxt = pl.BlockSpec((4 * nres, HD, HD), lambda r, j: (r, jnp.minimum((j + 1) * njb, nb - 1), 0))
    return (d // nres, nb // njb), own, prev, nxt


def _att_fwd(q, k, v, g):
    d = DILATIONS[g]
    nres, njb = ATT_STEP[g]
    grid, own, prev, _ = _att_specs(g)

    def body(q_ref, kp_ref, ko_ref, vp_ref, vo_ref, o_ref, l_ref):
        jb = pl.program_id(1)
        band, is_prev = _att_band()
        no_prev = jnp.where(is_prev, jnp.where(jb > 0, 0.0, NEG), 0.0)
        blocks = [(a, jj) for jj in range(njb) for a in range(4 * nres)]
        for g0 in range(0, len(blocks), ATT_GROUP):
            grp = blocks[g0:g0 + ATT_GROUP]
            s, v2 = [], []
            for a, jj in grp:
                k2_, v2_ = _att_keys(kp_ref, ko_ref, vp_ref, vo_ref, a, jj)
                s_ = jnp.where(band, _dot_nt(q_ref[a, jj * HD:(jj + 1) * HD, :], k2_) * SCALE, NEG)
                s.append(s_ + no_prev if jj == 0 else s_)
                v2.append(v2_)
            m = [jnp.max(s_, axis=-1, keepdims=True) for s_ in s]
            e = [jnp.exp(s_ - m_) for s_, m_ in zip(s, m)]
            den = [jnp.sum(e_, axis=-1, keepdims=True) for e_ in e]
            inv = [1.0 / d_ for d_ in den]
            for i, (a, jj) in enumerate(grp):
                rs_ = slice(jj * HD, (jj + 1) * HD)
                o_ref[a, rs_, :] = _dot((e[i] * inv[i]).astype(BF16), v2[i]).astype(BF16)
                l_ref[a, rs_, :] = jnp.broadcast_to(m[i] + jnp.log(den[i]), (HD, HD))

    return pl.pallas_call(
        body, grid=grid, in_specs=[own, prev, own, prev, own], out_specs=[own, own],
        out_shape=[_sds((4 * d, S // d, HD), BF16), _sds((4 * d, S // d, HD), F32)],
        compiler_params=_cp("parallel", "parallel"), name="att_fwd")(q, k, k, v, v)


def _att_mix(os_, ls_, proj, y_all, tm=512):
    def body(o0, o1, o2, l0, l1, l2, gt_ref, y_in, y_ref, om_ref, lt_ref, so1, so2, sl1, sl2):
        _undilate(o1, so1, DILATIONS[1], tm)
        _undilate(o2, so2, DILATIONS[2], tm)
        _undilate(l1, sl1, DILATIONS[1], tm)
        _undilate(l2, sl2, DILATIONS[2], tm)
        for h in range(4):
            a, b, c = l0[h], sl1[h], sl2[h]
            m = jnp.maximum(jnp.maximum(a, b), c)
            ea, eb, ec = jnp.exp(a - m), jnp.exp(b - m), jnp.exp(c - m)
            z = ea + eb + ec
            inv = 1.0 / z
            o = (ea * inv) * o0[h] + (eb * inv) * so1[h] + (ec * inv) * so2[h]
            gt = gt_ref[:, h * HD:(h + 1) * HD].astype(F32)
            om_ref[h] = o
            lt_ref[h] = m + jnp.log(z)
            y_ref[:, h * HD:(h + 1) * HD] = (o * (gt * _sigmoid(gt))).astype(BF16)

    dil = [_dil_spec(d, tm) for d in DILATIONS]
    return pl.pallas_call(
        body, grid=(S // tm,),
        in_specs=dil * 2 + [pl.BlockSpec((tm, W), lambda i: (i, P_CGATE)), ANY],
        out_specs=[pl.BlockSpec((None, tm, W), lambda i: (2, i, 0)), dil[0], dil[0]],
        out_shape=[_sds((4, S, W), BF16), _sds((4, S, HD), F32), _sds((4, S, HD), F32)],
        scratch_shapes=[pltpu.VMEM((4, tm, HD), F32)] * 4,
        input_output_aliases={7: 0}, compiler_params=_cp("parallel"), name="att_mix")(*os_, *ls_, proj, y_all)


def _att_bwd_pre(dy_all, proj, om, lse, dproj, tm=512):
    def body(dy_ref, gt_ref, om_ref, ls_ref, dp_in, *rest):
        dos, dls, lss, dp_ref, sdo, sdl = rest[0:3], rest[3:6], rest[6:8], rest[8], rest[9], rest[10]
        for h in range(4):
            cs = slice(h * HD, (h + 1) * HD)
            gt = gt_ref[:, cs].astype(F32)
            sg = _sigmoid(gt)
            dyv = dy_ref[:, cs].astype(F32)
            o = om_ref[h]
            do = dyv * (gt * sg)
            dp_ref[:, cs] = (dyv * o * (sg * (1.0 + gt * (1.0 - sg)))).astype(BF16)
            sdo[h] = do
            sdl[h] = jnp.broadcast_to(jnp.sum(do * o, axis=-1, keepdims=True), (tm, HD))
        for g, d in enumerate(DILATIONS):
            _dilate(sdo, dos[g], d, tm, BF16)
            _dilate(sdl, dls[g], d, tm)
            if g > 0:
                _dilate(ls_ref, lss[g - 1], d, tm)

    dil = [_dil_spec(d, tm) for d in DILATIONS]
    gcol = pl.BlockSpec((tm, W), lambda i: (i, P_CGATE))
    res = pl.pallas_call(
        body, grid=(S // tm,),
        in_specs=[pl.BlockSpec((None, tm, W), lambda i: (2, i, 0)), gcol, dil[0], dil[0], ANY],
        out_specs=dil + dil + dil[1:] + [gcol],
        out_shape=([_sds((4 * d, S // d, HD), BF16) for d in DILATIONS] + [_sds((4 * d, S // d, HD), F32) for d in DILATIONS]
                   + [_sds((4 * d, S // d, HD), F32) for d in DILATIONS[1:]] + [_sds((S, DIN), BF16)]),
        scratch_shapes=[pltpu.VMEM((4, tm, HD), F32)] * 2,
        input_output_aliases={4: 8}, compiler_params=_cp("parallel"), name="att_bwd_pre")(dy_all, proj, om, lse, dproj)
    return res[0:3], res[3:6], [lse] + list(res[6:8]), res[8]


def _att_bwd(q, k, v, do, lse, delta, g):
    d = DILATIONS[g]
    nres, njb = ATT_STEP[g]
    grid, own, prev, nxt = _att_specs(g)

    def body(qa_ref, qn_ref, kp_ref, ko_ref, vp_ref, vo_ref, doa_ref, don_ref, la_ref, ln_ref, da_ref, dn_ref,
             dq_ref, dk_ref, dv_ref):
        jb = pl.program_id(1)
        band, is_prev = _att_band()
        m_next = lax.broadcasted_iota(jnp.int32, (HD, HD), 1) >= lax.broadcasted_iota(jnp.int32, (HD, HD), 0)
        has_prev = jnp.where(is_prev, jnp.where(jb > 0, 1.0, 0.0), 1.0)
        has_next = jnp.where(jb < grid[1] - 1, 1.0, 0.0)

        def wide(t):
            return jnp.concatenate([t, t], axis=1)

        blocks = [(a, jj) for jj in range(njb) for a in range(4 * nres)]
        for g0 in range(0, len(blocks), ATT_GROUP_BWD):
            grp = blocks[g0:g0 + ATT_GROUP_BWD]
            ops = []
            for a, jj in grp:
                rs_ = slice(jj * HD, (jj + 1) * HD)
                k2, v2 = _att_keys(kp_ref, ko_ref, vp_ref, vo_ref, a, jj)
                if jj == njb - 1:
                    qn, don, lsn, dln, fn = qn_ref[a], don_ref[a], ln_ref[a], dn_ref[a], has_next
                else:
                    ns = slice((jj + 1) * HD, (jj + 2) * HD)
                    qn, don, lsn, dln, fn = qa_ref[a, ns, :], doa_ref[a, ns, :], la_ref[a, ns, :], da_ref[a, ns, :], None
                ops.append(dict(qa=qa_ref[a, rs_, :], doa=doa_ref[a, rs_, :], lsa=wide(la_ref[a, rs_, :]),
                                dla=wide(da_ref[a, rs_, :]), k2=k2, v2=v2, ko=ko_ref[a, rs_, :], vo=vo_ref[a, rs_, :],
                                qn=qn, don=don, lsn=lsn, dln=dln, fn=fn, first=jj == 0))
            sa = [_dot_nt(o["qa"], o["k2"]) for o in ops]
            dpa = [_dot_nt(o["doa"], o["v2"]) for o in ops]
            sn = [_dot_nt(o["qn"], o["ko"]) for o in ops]
            dpn = [_dot_nt(o["don"], o["vo"]) for o in ops]
            pa, pn = [], []
            for o, sa_, sn_ in zip(ops, sa, sn):
                p_ = jnp.where(band, jnp.exp(sa_ * SCALE - o["lsa"]), 0.0)
                pa.append(p_ * has_prev if o["first"] else p_)
                p_ = jnp.where(m_next, jnp.exp(sn_ * SCALE - o["lsn"]), 0.0)
                pn.append(p_ if o["fn"] is None else p_ * o["fn"])
            dsa = [(p_ * (dp_ - o["dla"]) * SCALE).astype(BF16) for p_, dp_, o in zip(pa, dpa, ops)]
            dsn = [(p_ * (dp_ - o["dln"]) * SCALE).astype(BF16) for p_, dp_, o in zip(pn, dpn, ops)]
            for i, (a, jj) in enumerate(grp):
                rs_ = slice(jj * HD, (jj + 1) * HD)
                o = ops[i]
                dq_ref[a, rs_, :] = _dot(dsa[i], o["k2"]).astype(BF16)
                q2 = jnp.concatenate([o["qa"], o["qn"]], axis=0)
                do2 = jnp.concatenate([o["doa"], o["don"]], axis=0)
                dk_ref[a, rs_, :] = _dot_tn(jnp.concatenate([dsa[i][:, HD:], dsn[i]], axis=0), q2).astype(BF16)
                dv_ref[a, rs_, :] = _dot_tn(jnp.concatenate([pa[i][:, HD:].astype(BF16), pn[i].astype(BF16)], axis=0),
                                            do2).astype(BF16)

    return pl.pallas_call(
        body, grid=grid, in_specs=[own, nxt, prev, own, prev, own, own, nxt, own, nxt, own, nxt],
        out_specs=[own, own, own], out_shape=[_sds((4 * d, S // d, HD), BF16)] * 3,
        compiler_params=_cp("parallel", "parallel"), name="att_bwd")(q, q, k, k, v, v, do, do, lse, lse, delta, delta)


def _att_bwd_post(dqs, dks, dvs, dproj, tm=512):
    def body(*refs):
        dq, dk, dv, dp_ref, scr = refs[0:3], refs[3:6], refs[6:9], refs[10], refs[11]
        for g in range(3):
            _undilate(dq[g], scr, DILATIONS[g], tm)
            for h in range(4):
                dp_ref[:, g * W + h * HD:g * W + (h + 1) * HD] = scr[h].astype(BF16)
        for j, parts in enumerate((dk, dv)):
            acc = None
            for g in range(3):
                _undilate(parts[g], scr, DILATIONS[g], tm)
                vals = [scr[h] for h in range(4)]
                acc = vals if acc is None else [x + y for x, y in zip(acc, vals)]
            for h in range(4):
                dp_ref[:, (3 + j) * W + h * HD:(3 + j) * W + (h + 1) * HD] = acc[h].astype(BF16)

    dil = [_dil_spec(d, tm) for d in DILATIONS]
    return pl.pallas_call(
        body, grid=(S // tm,), in_specs=dil * 3 + [ANY],
        out_specs=pl.BlockSpec((tm, 5 * W), lambda i: (i, 1)), out_shape=_sds((S, DIN), BF16),
        scratch_shapes=[pltpu.VMEM((4, tm, HD), F32)],
        input_output_aliases={9: 0}, compiler_params=_cp("parallel"), name="att_bwd_post")(*dqs, *dks, *dvs, dproj)


def _mem_kv_fwd(mem_n, wkv):
    m = mem_n.shape[0]

    def body(a_ref, w_ref, o_ref):
        o_ref[...] = _dot(a_ref[...], w_ref[...])

    return pl.pallas_call(body, out_shape=_sds((m, 2 * W), F32), compiler_params=_cp(), name="mem_kv_fwd")(mem_n, wkv)


def _mem_fwd(proj, kv, y_all, tm=512):
    m = kv.shape[0]

    def body(q_ref, gt_ref, kv_ref, y_in, y_ref):
        gt = gt_ref[...].astype(F32)
        sl = gt * _sigmoid(gt)
        hs = [slice(h * HD, (h + 1) * HD) for h in range(4)]
        s = [_dot_nt(q_ref[:, cs].astype(BF16), kv_ref[:, cs].astype(BF16)) * SCALE for cs in hs]
        e = [jnp.exp(s_ - jnp.max(s_, axis=-1, keepdims=True)) for s_ in s]
        p = [(e_ * (1.0 / jnp.sum(e_, axis=-1, keepdims=True))).astype(BF16) for e_ in e]
        for h, cs in enumerate(hs):
            o = _dot(p[h], kv_ref[:, W + h * HD:W + (h + 1) * HD].astype(BF16))
            y_ref[:, cs] = (o * sl[:, cs]).astype(BF16)

    return pl.pallas_call(
        body, grid=(S // tm,),
        in_specs=[pl.BlockSpec((tm, W), lambda i: (i, P_MQ)), pl.BlockSpec((tm, W), lambda i: (i, P_MGATE)),
                  pl.BlockSpec((m, 2 * W), lambda i: (0, 0)), ANY],
        out_specs=pl.BlockSpec((None, tm, W), lambda i: (3, i, 0)), out_shape=_sds((4, S, W), BF16),
        input_output_aliases={3: 0}, compiler_params=_cp("parallel"), name="mem_fwd")(proj, proj, kv, y_all)


def _mem_bwd(proj, kv, dy_all, dproj, tm=512):
    m = kv.shape[0]
    ni = S // tm

    def body(q_ref, gt_ref, kv_ref, dy_ref, dp_in, dp_ref, dkv_ref, obuf, osem):
        i = pl.program_id(0)

        @pl.when(i == 0)
        def _():
            dkv_ref[...] = jnp.zeros_like(dkv_ref)

        gt = gt_ref[...].astype(F32)
        sg = _sigmoid(gt)
        sl = gt * sg
        dsl = sg * (1.0 + gt * (1.0 - sg))
        dyv = dy_ref[...].astype(F32)

        def fill(slot):
            hs = [slice(h * HD, (h + 1) * HD) for h in range(4)]
            vss = [slice(W + h * HD, W + (h + 1) * HD) for h in range(4)]
            q = [q_ref[:, cs].astype(BF16) for cs in hs]
            k = [kv_ref[:, cs].astype(BF16) for cs in hs]
            v = [kv_ref[:, vs].astype(BF16) for vs in vss]
            dob = [(dyv[:, cs] * sl[:, cs]).astype(BF16) for cs in hs]
            s = [_dot_nt(q_, k_) * SCALE for q_, k_ in zip(q, k)]
            dp = [_dot_nt(d_, v_) for d_, v_ in zip(dob, v)]
            e = [jnp.exp(s_ - jnp.max(s_, axis=-1, keepdims=True)) for s_ in s]
            p = [e_ * (1.0 / jnp.sum(e_, axis=-1, keepdims=True)) for e_ in e]
            pb = [p_.astype(BF16) for p_ in p]
            dsb = [(p_ * (dp_ - jnp.sum(dp_ * p_, axis=-1, keepdims=True)) * SCALE).astype(BF16) for p_, dp_ in zip(p, dp)]
            for h, (cs, vs) in enumerate(zip(hs, vss)):
                o = _dot(pb[h], v[h])
                slot[:, cs] = _dot(dsb[h], k[h]).astype(BF16)
                slot[:, vs] = (dyv[:, cs] * o * dsl[:, cs]).astype(BF16)
                dkv_ref[:, cs] += _dot_tn(dsb[h], q[h])
                dkv_ref[:, vs] += _dot_tn(pb[h], dob[h])

        _tile_put(obuf, osem, lambda st: dp_ref.at[pl.ds(st * tm, tm), pl.ds(P_MQ * W, 2 * W)], i, ni, fill)

    return pl.pallas_call(
        body, grid=(ni,),
        in_specs=[pl.BlockSpec((tm, W), lambda i: (i, P_MQ)), pl.BlockSpec((tm, W), lambda i: (i, P_MGATE)),
                  pl.BlockSpec((m, 2 * W), lambda i: (0, 0)), pl.BlockSpec((None, tm, W), lambda i: (3, i, 0)), ANY],
        out_specs=[ANY, pl.BlockSpec((m, 2 * W), lambda i: (0, 0))],
        out_shape=[_sds((S, DIN), BF16), _sds((m, 2 * W), F32)],
        scratch_shapes=[pltpu.VMEM((2, tm, 2 * W), BF16), pltpu.SemaphoreType.DMA((2,))],
        input_output_aliases={4: 0}, compiler_params=_cp("arbitrary"), name="mem_bwd")(proj, proj, kv, dy_all, dproj)


def _mem_kv_bwd(mem, g, mem_n, wkv, dkv):
    m = mem.shape[0]

    def body(x_ref, g_ref, a_ref, w_ref, d_ref, dw_ref, dg_ref):
        db = d_ref[...].astype(BF16)
        dw_ref[...] = _dot_tn(a_ref[...], db).astype(BF16)
        dn = _dot_nt(db, w_ref[...])
        xv = x_ref[...]
        xh = xv * lax.rsqrt(jnp.mean(xv * xv, axis=-1, keepdims=True) + EPS)
        dg_ref[...] = jnp.sum(dn * xh, axis=0, keepdims=True)

    return pl.pallas_call(
        body, out_shape=[_sds((D, 2 * W), BF16), _sds((1, D), F32)], compiler_params=_cp(), name="mem_kv_bwd")(mem, g, mem_n, wkv, dkv)


def _layer_fwd(x, h, ht, mem_n, p, wg):
    win, wkv, wbr, wo = wg
    proj = _proj_fwd(h, win)
    y_all = lax.empty((4, S, W), BF16)
    y_all = _gmlp_fwd(proj, p["gm_ln_g"], p["gm_ln_b"], p["gm_ws"], p["gm_bias"], y_all)
    y_all = _pool_fwd(proj, p["pool_w"], p["pool_scale"], y_all)
    qs, ks, vs = _att_prep(proj)
    os_, ls_ = zip(*[_att_fwd(qs[g], ks[g], vs[g], g) for g in range(3)])
    y_all, om, lse = _att_mix(os_, ls_, proj, y_all)
    kv = _mem_kv_fwd(mem_n, wkv)
    y_all = _mem_fwd(proj, kv, y_all)
    z = _merge_fwd(y_all, wbr, proj)
    x_new = _out_fwd(z, wo, x)
    return x_new, dict(x=x, ht=ht, proj=proj, y_all=y_all, om=om, lse=lse, mem_n=mem_n, kv=kv, z=z, qkv=(qs, ks, vs))


GRAD_PARTS = ((2, D // 2, CW), (2, D // 8, 2 * W), (2, 2 * W, D // NCHIP), (2, D // 8, D))
SUM_TILE = (64, 128, 256, 128)
ADAM_TILE = (128, 256, 2048, 256)
PLACE_TILE = (256, 256, 512, 256)


def _layer_bwd(dx, mem, p, wg, sv, exchange, dep):
    win, wkv, wbr, wo = wg
    proj = sv["proj"]
    dz, d_wo = _out_bwd(dx, sv["z"], wo, dep)
    dproj = lax.empty((S, DIN), BF16)
    dproj, dy_all, d_wbr = _merge_bwd(dz, sv["y_all"], wbr, proj, dproj)
    dproj, d_ws, d_bs, d_lg, d_lb = _gmlp_bwd(proj, dy_all, p["gm_ln_g"], p["gm_ln_b"], p["gm_ws"], p["gm_bias"], dproj)
    dproj, d_pw, d_sc = _pool_bwd(proj, dy_all, p["pool_w"], p["pool_scale"], dproj)
    dos, dls, lss, dproj = _att_bwd_pre(dy_all, proj, sv["om"], sv["lse"], dproj)
    qs, ks, vs = sv["qkv"]
    dqs, dks, dvs = zip(*[_att_bwd(qs[g], ks[g], vs[g], dos[g], lss[g], dls[g], g) for g in range(3)])
    dproj = _att_bwd_post(dqs, dks, dvs, dproj)
    dproj, dkv = _mem_bwd(proj, sv["kv"], dy_all, dproj)
    d_wkv, d_mg = _mem_kv_bwd(mem, p["mem_norm_g"], sv["mem_n"], wkv, dkv)
    d_win = _proj_bwd_w(sv["ht"], dproj)
    big = tuple(t.reshape((NCHIP,) + s) for t, s in zip((d_win, d_wkv, d_wbr, d_wo), GRAD_PARTS))
    inflight = exchange(big)
    dh = _proj_bwd_x(dproj, win, inflight[-1])
    dx_in, d_ng = _rms_bwd(dh, sv["x"], p["norm_g"], dx)
    small = dict(norm_g=d_ng, gm_ln_g=d_lg, gm_ln_b=d_lb, gm_ws=d_ws, gm_bs=d_bs[:, :, 0], pool_w=d_pw, pool_scale=d_sc, mem_norm_g=d_mg)
    return dx_in, (big,) + inflight, small


_SMALL = ("norm_g", "gm_ln_g", "gm_ln_b", "gm_ws", "gm_bs", "pool_w", "pool_scale", "mem_norm_g")


def _layer_params(l, norm_g, gm_ln_g, gm_ln_b, gm_ws, gm_bs, pool_w, pool_scale, mem_norm_g):
    return dict(norm_g=norm_g[l][None], gm_ln_g=gm_ln_g[l][None], gm_ln_b=gm_ln_b[l][None], gm_ws=gm_ws[l],
                gm_bias=jnp.broadcast_to(gm_bs[l][:, :, None], (4, HD, HD)), pool_w=pool_w[l],
                pool_scale=pool_scale[l][None], mem_norm_g=mem_norm_g[l][None])


def kernel(x, mem, norm_g, w_in, gm_ln_g, gm_ln_b, gm_ws, gm_bs, pool_w, pool_scale, mem_norm_g, w_mem_kv, w_branch, w_out, final_norm_g, loss_target, m_norm_g, m_w_in, m_gm_ln_g, m_gm_ln_b, m_gm_ws, m_gm_bs, m_pool_w, m_pool_scale, m_mem_norm_g, m_w_mem_kv, m_w_branch, m_w_out, m_final_norm_g, v_norm_g, v_w_in, v_gm_ln_g, v_gm_ln_b, v_gm_ws, v_gm_bs, v_pool_w, v_pool_scale, v_mem_norm_g, v_w_mem_kv, v_w_branch, v_w_out, v_final_norm_g):
    xs, memv, tgt = x[0], mem[0], loss_target[0]
    params = [_layer_params(l, norm_g, gm_ln_g, gm_ln_b, gm_ws, gm_bs, pool_w, pool_scale, mem_norm_g) for l in range(NL)]

    large = (w_in, w_mem_kv, w_branch, w_out)
    first = tuple(t[:1].astype(BF16) for t in large)
    lands, gsems, after = _gather_start(first, True)
    rest = tuple((t[1:] + after[0, 0]).astype(BF16) for t in large)
    lands_r, gsems_r, after_r = _gather_start(rest, False)
    lands, gsems = lands + lands_r, gsems + gsems_r

    saved, wgs = [], []
    mem_ns = [_rms_fwd(memv, params[l]["mem_norm_g"], memv.shape[0]) for l in range(NL)]
    for l in range(NL):
        h, ht = _rms_fwd_t(xs, params[l]["norm_g"])
        if l == 0:
            got, relay_sems = _gather_relay(first, lands[0], gsems[0], [after, after_r, h] + mem_ns)
            got = _gather_wait_relay(got, relay_sems)
            got = [_place_own(got[k], first[k], 0, PLACE_TILE[k]) for k in range(4)]
        else:
            got = _gather_wait(f"gather_wait_{l}", l - 1, rest, lands[l], gsems[l], after)
            got = [_place_own(got[k], rest[k], l - 1, PLACE_TILE[k]) for k in range(4)]
        wgs.append((got[0], got[1].reshape(D, 2 * W), got[2], got[3].reshape(D, D)))
        xs, sv = _layer_fwd(xs, h, ht, mem_ns[l], params[l], wgs[l])
        saved.append(sv)
        after = xs
    dx, d_fg, ls = _loss_head(xs, tgt, final_norm_g[None])
    loss = lax.psum(ls[0, 0], ("x", "y", "c"))

    def pack(leaves):
        parts, spans, at = [], [], 0
        for t in leaves:
            r = t.reshape(-1, 128)
            pad = -r.shape[0] % 8
            parts.append(jnp.pad(r, ((0, pad), (0, 0))) if pad else r)
            spans.append((at, at + r.shape[0]))
            at += r.shape[0] + pad
        return jnp.concatenate(parts, axis=0), spans

    flight, small = [None] * NL, [None] * NL
    dep = dx
    for l in reversed(range(NL)):
        dx, flight[l], small[l] = _layer_bwd(dx, memv, params[l], wgs[l], saved[l], functools.partial(_exch_start, l), dep)
        dep = dx
        if l == 1:
            packed_a, spans_a = pack([jnp.stack([small[j][n] for j in range(1, NL)]) for n in _SMALL] + [d_fg])
            zone_a, sems_a, dep = _small_start(packed_a, "small_start_early")
    grad_x = dx[None]
    packed_b, spans_b = pack([small[0][n] for n in _SMALL])
    zone_b, sems_b, after = _small_start(packed_b, "small_start_last")

    ws = dict(norm_g=norm_g, w_in=w_in, gm_ln_g=gm_ln_g, gm_ln_b=gm_ln_b, gm_ws=gm_ws, gm_bs=gm_bs, pool_w=pool_w,
              pool_scale=pool_scale, mem_norm_g=mem_norm_g, w_mem_kv=w_mem_kv, w_branch=w_branch, w_out=w_out,
              final_norm_g=final_norm_g)
    ms = dict(norm_g=m_norm_g, w_in=m_w_in, gm_ln_g=m_gm_ln_g, gm_ln_b=m_gm_ln_b, gm_ws=m_gm_ws, gm_bs=m_gm_bs,
              pool_w=m_pool_w, pool_scale=m_pool_scale, mem_norm_g=m_mem_norm_g, w_mem_kv=m_w_mem_kv,
              w_branch=m_w_branch, w_out=m_w_out, final_norm_g=m_final_norm_g)
    vs = dict(norm_g=v_norm_g, w_in=v_w_in, gm_ln_g=v_gm_ln_g, gm_ln_b=v_gm_ln_b, gm_ws=v_gm_ws, gm_bs=v_gm_bs,
              pool_w=v_pool_w, pool_scale=v_pool_scale, mem_norm_g=v_mem_norm_g, w_mem_kv=v_w_mem_kv,
              w_branch=v_w_branch, w_out=v_w_out, final_norm_g=v_final_norm_g)

    big = ("w_in", "w_mem_kv", "w_branch", "w_out")
    acc = {n: [lax.empty((ws[n].size // ws[n].shape[-1], ws[n].shape[-1]), F32) for _ in range(4)] for n in big}
    for l in reversed(range(NL)):
        parts, zones, sems, _ = flight[l]
        zones = _exch_wait(l, parts, zones, sems, after)
        full = _share_full([_sum_half(parts[k], zones[k], SUM_TILE[k]) for k in range(4)])
        for k, n in enumerate(big):
            acc[n] = _adamw_layer(l, ws[n], full[k], ms[n], vs[n], acc[n], ADAM_TILE[k])
        after = acc["w_in"][0]
    grads, upd = {}, {}
    for n in big:
        d_, m_, v_, g_ = (t.reshape(ws[n].shape) for t in acc[n])
        grads[n], upd[n] = g_, (d_, m_, v_)

    def row_tile(rows):
        return rows if rows <= 1536 else max(t for t in range(8, 513, 8) if rows % t == 0)

    zone_a = _small_wait(packed_a, zone_a, sems_a, after, "small_wait_early")
    zone_b = _small_wait(packed_b, zone_b, sems_b, zone_a, "small_wait_last")
    tot_a = _sum_small(packed_a, zone_a, row_tile(packed_a.shape[0]))
    tot_b = _sum_small(packed_b, zone_b, row_tile(packed_b.shape[0]))
    for i, n in enumerate(_SMALL):
        later = tot_a[spans_a[i][0]:spans_a[i][1]].reshape((NL - 1,) + ws[n].shape[1:])
        first_l = tot_b[spans_b[i][0]:spans_b[i][1]].reshape(ws[n].shape[1:])
        grads[n] = jnp.stack([first_l] + [later[j] for j in range(NL - 1)])
    grads["final_norm_g"] = tot_a[spans_a[-1][0]:spans_a[-1][1]].reshape(ws["final_norm_g"].shape)
    for n in _SMALL + ("final_norm_g",):
        upd[n] = _adamw(ws[n], grads[n], ms[n], vs[n])
    order = ("norm_g", "w_in", "gm_ln_g", "gm_ln_b", "gm_ws", "gm_bs", "pool_w", "pool_scale", "mem_norm_g", "w_mem_kv",
             "w_branch", "w_out", "final_norm_g")
    return (loss, grad_x, *[grads[n] for n in order], *[upd[n][0] for n in order], *[upd[n][1] for n in order],
            *[upd[n][2] for n in order])
```

```python
import functools
import math

import jax
import jax.numpy as jnp
from jax import lax
from jax.experimental import pallas as pl
from jax.experimental.pallas import tpu as pltpu

F32 = jnp.float32
BF16 = jnp.bfloat16

S = 4096
D = 1024
W = 512
DIN = 10752
NL = 4
NCHIP = 4
NDEV = 8
CW = DIN // NCHIP
TN_IN = 896
NJ = CW // TN_IN
HD = 128
EPS = 1e-6
NEG = -1e30
SCALE = HD ** -0.5
INV_SQRT2 = 1.0 / math.sqrt(2.0)
INV_SQRT2PI = 1.0 / math.sqrt(2.0 * math.pi)
POOL_WINDOWS = (2, 4, 8, 16)
DILATIONS = (1, 4, 16)
HALO = 16
NPIECE = DIN // W
P_AGATE, P_PIN, P_PGATE, P_CQ, P_CK, P_CV, P_CGATE, P_MQ, P_MGATE, P_GM = 2, 3, 4, 5, 8, 9, 10, 11, 12, 13
VMEM_LIMIT = 56 * 1024 * 1024

ADAM_LR, ADAM_B1, ADAM_B2, ADAM_EPS, ADAM_WD, ADAM_STEP = 0.001, 0.9, 0.999, 1e-08, 0.01, 10

MESH = pl.DeviceIdType.MESH
ANY = pl.BlockSpec(memory_space=pl.ANY)


def _cp(*sem):
    return pltpu.CompilerParams(dimension_semantics=sem or None, vmem_limit_bytes=VMEM_LIMIT)


def _sds(shape, dtype):
    return jax.ShapeDtypeStruct(shape, dtype)


def _sigmoid(v):
    return 1.0 / (1.0 + jnp.exp(-v))


def _dot(a, b):
    return jnp.dot(a, b, preferred_element_type=F32)


def _dot_nt(a, b):
    return lax.dot_general(a, b, (((1,), (1,)), ((), ())), preferred_element_type=F32)


def _dot_tn(a, b):
    return lax.dot_general(a, b, (((0,), (0,)), ((), ())), preferred_element_type=F32)


def _tile_put(buf, sem, dst_of, step, nsteps, fill):
    slot = step % 2

    def copy(s, st):
        return pltpu.make_async_copy(buf.at[s], dst_of(st), sem.at[s])

    @pl.when(step >= 2)
    def _():
        copy(slot, step).wait()

    fill(buf.at[slot])
    copy(slot, step).start()

    @pl.when(step == nsteps - 1)
    def _():
        if nsteps >= 2:
            copy(1 - slot, step).wait()
        copy(slot, step).wait()


def _my_pos():
    return lax.axis_index("x"), lax.axis_index("y"), lax.axis_index("c")


_CHIP_REL = ((1, 0), (0, 1), (1, 1))
_DEV_REL = tuple((dx, dy, dc) for dx in (0, 1) for dy in (0, 1) for dc in (0, 1))[1:]


HBM = pl.BlockSpec(memory_space=pltpu.HBM)
SEM = pl.BlockSpec(memory_space=pltpu.SEMAPHORE)
EFFECT = pltpu.SideEffectType.DATAFLOW_SIDE_EFFECTING
N_GATHER = 3 * 4
N_EXCH = 7 * 4


def _in_hbm(t):
    return pltpu.with_memory_space_constraint(t, pltpu.HBM)


def _gather_start(shards, first):
    nk = len(shards)
    nl = shards[0].shape[0]
    lands = [pltpu.HBM((NCHIP,) + s.shape[1:], BF16) for s in shards for _ in range(nl)]

    def body(*refs):
        ins, outs = refs[:nk], refs[nk:nk + nk * nl]
        sems = refs[nk + nk * nl:nk + nk * nl + 2 * nl]
        token = refs[-1]
        x, y, c = _my_pos()
        me = 2 * x + y
        for l in range(nl):
            for r, (dx, dy) in enumerate(_CHIP_REL):
                for k in range(nk):
                    src, dst = ins[k].at[l], outs[k * nl + l].at[me]
                    if first:
                        hf = pl.ds(c * (shards[k].shape[1] // 2), shards[k].shape[1] // 2)
                        src, dst = src.at[hf], dst.at[hf]
                    pltpu.make_async_remote_copy(
                        src_ref=src, dst_ref=dst, send_sem=sems[2 * l].at[r * nk + k],
                        recv_sem=sems[2 * l + 1].at[r * nk + k], device_id=(x ^ dx, y ^ dy, c), device_id_type=MESH).start()
        token[...] = jnp.zeros_like(token)

    res = pl.pallas_call(
        body, name="gather_start_first" if first else "gather_start_rest",
        out_shape=lands + [pltpu.SemaphoreType.DMA((N_GATHER,))] * (2 * nl) + [_sds((8, 128), F32)],
        in_specs=[HBM] * nk, out_specs=[HBM] * (nk * nl) + [SEM] * (2 * nl) + [pl.BlockSpec(memory_space=pltpu.VMEM)],
        compiler_params=pltpu.CompilerParams(has_side_effects=EFFECT))(*[_in_hbm(s) for s in shards])
    lands = [[res[k * nl + l] for k in range(nk)] for l in range(nl)]
    sems = [(res[nk * nl + 2 * l], res[nk * nl + 2 * l + 1]) for l in range(nl)]
    return lands, sems, res[-1]


def _gather_relay(shards, lands, sems, after):
    nk = len(shards)
    half = [s.shape[1] // 2 for s in shards]

    na = len(after)

    def body(*refs):
        ins, land = refs[:nk], refs[nk:2 * nk]
        send, recv = refs[2 * nk], refs[2 * nk + 1]
        send2, recv2 = refs[3 * nk + 2 + na], refs[3 * nk + 3 + na]
        x, y, c = _my_pos()
        for r, (dx, dy) in enumerate(_CHIP_REL):
            cx, cy = x ^ dx, y ^ dy
            for k in range(nk):
                hf = pl.ds(c * half[k], half[k])
                got = land[k].at[2 * cx + cy].at[hf]
                cp = pltpu.make_async_remote_copy(
                    src_ref=ins[k].at[0].at[hf], dst_ref=got, send_sem=send.at[r * nk + k],
                    recv_sem=recv.at[r * nk + k], device_id=(cx, cy, c), device_id_type=MESH)
                cp.wait_send()
                cp.wait_recv()
                pltpu.make_async_remote_copy(
                    src_ref=got, dst_ref=got, send_sem=send2.at[r * nk + k], recv_sem=recv2.at[r * nk + k],
                    device_id=(x, y, 1 - c), device_id_type=MESH).start()

    res = pl.pallas_call(
        body, name="gather_relay",
        out_shape=[pltpu.HBM(t.shape, t.dtype) for t in lands] + [pltpu.SemaphoreType.DMA((N_GATHER,))] * 2,
        in_specs=[ANY] * nk + [HBM] * nk + [SEM, SEM] + [ANY] * na, out_specs=[HBM] * nk + [SEM, SEM],
        input_output_aliases={nk + k: k for k in range(nk)},
        compiler_params=pltpu.CompilerParams(has_side_effects=EFFECT))(*shards, *lands, *sems, *after)
    return res[:nk], (res[nk], res[nk + 1])


def _gather_wait_relay(lands, sems):
    nk = len(lands)
    half = [t.shape[1] // 2 for t in lands]

    def body(*refs):
        land = refs[:nk]
        send, recv = refs[nk], refs[nk + 1]
        x, y, c = _my_pos()
        for r, (dx, dy) in enumerate(_CHIP_REL):
            chip = 2 * (x ^ dx) + (y ^ dy)
            for k in range(nk):
                mine = land[k].at[chip].at[pl.ds(c * half[k], half[k])]
                theirs = land[k].at[chip].at[pl.ds((1 - c) * half[k], half[k])]
                cp = pltpu.make_async_remote_copy(
                    src_ref=mine, dst_ref=theirs, send_sem=send.at[r * nk + k], recv_sem=recv.at[r * nk + k],
                    device_id=(x, y, 1 - c), device_id_type=MESH)
                cp.wait_send()
                cp.wait_recv()

    return pl.pallas_call(
        body, name="gather_wait_relay", out_shape=[pltpu.HBM(t.shape, t.dtype) for t in lands],
        in_specs=[HBM] * nk + [SEM, SEM], out_specs=[HBM] * nk, input_output_aliases={k: k for k in range(nk)},
        compiler_params=pltpu.CompilerParams(has_side_effects=EFFECT))(*lands, *sems)


def _gather_wait(name, l, shards, lands, sems, after):
    nk = len(shards)

    def body(*refs):
        ins, land = refs[:nk], refs[nk:2 * nk]
        send, recv = refs[2 * nk], refs[2 * nk + 1]
        x, y, c = _my_pos()
        for r, (dx, dy) in enumerate(_CHIP_REL):
            cx, cy = x ^ dx, y ^ dy
            for k in range(nk):
                cp = pltpu.make_async_remote_copy(
                    src_ref=ins[k].at[l], dst_ref=land[k].at[2 * cx + cy], send_sem=send.at[r * nk + k],
                    recv_sem=recv.at[r * nk + k], device_id=(cx, cy, c), device_id_type=MESH)
                cp.wait_send()
                cp.wait_recv()

    return pl.pallas_call(
        body, name=name, out_shape=[pltpu.HBM(t.shape, t.dtype) for t in lands],
        in_specs=[ANY] * nk + [HBM] * nk + [SEM, SEM, ANY], out_specs=[HBM] * nk,
        input_output_aliases={nk + k: k for k in range(nk)},
        compiler_params=pltpu.CompilerParams(has_side_effects=EFFECT))(*shards, *lands, *sems, after)


def _place_own(land, shard, l, tr):
    _, rows, cols = shard.shape[0], shard.shape[-2], shard.shape[-1]
    lead = shard.shape[1:-2]
    nlead = math.prod(lead)
    sh = shard.reshape((shard.shape[0], nlead, rows, cols))
    ld = land.reshape((NCHIP, nlead, rows, cols))
    me = (2 * lax.axis_index("x") + lax.axis_index("y")).astype(jnp.int32).reshape(1)

    def body(me_ref, s_ref, l_in, o_ref):
        o_ref[...] = s_ref[...]

    out = pl.pallas_call(
        body,
        grid_spec=pltpu.PrefetchScalarGridSpec(
            num_scalar_prefetch=1, grid=(nlead, rows // tr),
            in_specs=[pl.BlockSpec((None, None, tr, cols), lambda b, i, me_ref: (l, b, i, 0)), ANY],
            out_specs=pl.BlockSpec((None, None, tr, cols), lambda b, i, me_ref: (me_ref[0], b, i, 0))),
        out_shape=_sds(ld.shape, BF16), input_output_aliases={2: 0},
        compiler_params=_cp("parallel", "parallel"), name="place_own")(me, sh, ld)
    return out.reshape(land.shape)


def _exch_start(l, parts):
    nk = len(parts)

    def body(*refs):
        ins, outs = refs[:nk], refs[nk:2 * nk]
        send, recv, token = refs[2 * nk:]
        x, y, c = _my_pos()
        for r, (dx, dy, dc) in enumerate(_DEV_REL):
            px, py, pc = x ^ dx, y ^ dy, c ^ dc
            for k in range(nk):
                pltpu.make_async_remote_copy(
                    src_ref=ins[k].at[2 * px + py, pc], dst_ref=outs[k].at[r], send_sem=send.at[r * nk + k],
                    recv_sem=recv.at[r * nk + k], device_id=(px, py, pc), device_id_type=MESH).start()
        token[...] = jnp.zeros_like(token)

    res = pl.pallas_call(
        body, name=f"exch_start_{l}",
        out_shape=[pltpu.HBM((7,) + p.shape[2:], BF16) for p in parts] + [pltpu.SemaphoreType.DMA((N_EXCH,))] * 2 + [_sds((8, 128), F32)],
        in_specs=[HBM] * nk, out_specs=[HBM] * nk + [SEM, SEM, pl.BlockSpec(memory_space=pltpu.VMEM)],
        compiler_params=pltpu.CompilerParams(has_side_effects=EFFECT))(*[_in_hbm(p) for p in parts])
    return res[:nk], (res[nk], res[nk + 1]), res[-1]


def _exch_wait(l, parts, lands, sems, after):
    nk = len(parts)

    def body(*refs):
        ins, land = refs[:nk], refs[nk:2 * nk]
        send, recv = refs[2 * nk], refs[2 * nk + 1]
        x, y, c = _my_pos()
        for r, (dx, dy, dc) in enumerate(_DEV_REL):
            px, py, pc = x ^ dx, y ^ dy, c ^ dc
            for k in range(nk):
                cp = pltpu.make_async_remote_copy(
                    src_ref=ins[k].at[2 * px + py, pc], dst_ref=land[k].at[r], send_sem=send.at[r * nk + k],
                    recv_sem=recv.at[r * nk + k], device_id=(px, py, pc), device_id_type=MESH)
                cp.wait_send()
                cp.wait_recv()

    return pl.pallas_call(
        body, name=f"exch_wait_{l}", out_shape=[pltpu.HBM(t.shape, t.dtype) for t in lands],
        in_specs=[ANY] * nk + [HBM] * nk + [SEM, SEM, ANY], out_specs=[HBM] * nk,
        input_output_aliases={nk + k: k for k in range(nk)},
        compiler_params=pltpu.CompilerParams(has_side_effects=EFFECT))(*parts, *lands, *sems, after)


def _chip_half():
    x, y, c = _my_pos()
    return jnp.stack([2 * x + y, c]).astype(jnp.int32)


def _sum_half(part, land, tr):
    _, _, r2, cols = part.shape

    def body(pos_ref, p_ref, r_ref, o_ref):
        acc = p_ref[...].astype(F32)
        for r in range(7):
            acc = acc + r_ref[r].astype(F32)
        o_ref[...] = acc

    return pl.pallas_call(
        body,
        grid_spec=pltpu.PrefetchScalarGridSpec(
            num_scalar_prefetch=1, grid=(r2 // tr,),
            in_specs=[pl.BlockSpec((None, None, tr, cols), lambda i, pos: (pos[0], pos[1], i, 0)),
                      pl.BlockSpec((7, tr, cols), lambda i, pos: (0, i, 0))],
            out_specs=pl.BlockSpec((None, tr, cols), lambda i, pos: (pos[1], i, 0))),
        out_shape=_sds((2, r2, cols), F32), compiler_params=_cp("parallel"), name="sum_half")(_chip_half(), part, land)


def _share_full(fulls):
    nk = len(fulls)

    def body(*refs):
        ins, outs = refs[:nk], refs[nk:2 * nk]
        send, recv = refs[2 * nk:]
        x, y, c = _my_pos()

        def copy(k, hf):
            return pltpu.make_async_remote_copy(
                src_ref=ins[k].at[hf], dst_ref=outs[k].at[hf], send_sem=send.at[k], recv_sem=recv.at[k],
                device_id=(x, y, 1 - c), device_id_type=MESH)

        for k in range(nk):
            copy(k, c).start()
        for k in range(nk):
            copy(k, 1 - c).wait_recv()
        for k in range(nk):
            copy(k, c).wait_send()

    return pl.pallas_call(
        body, out_shape=[_sds(f.shape, F32) for f in fulls], in_specs=[ANY] * nk, out_specs=[ANY] * nk,
        scratch_shapes=[pltpu.SemaphoreType.DMA((nk,))] * 2, input_output_aliases={k: k for k in range(nk)},
        name="share_full")(*fulls)


def _small_start(packed, name):
    def body(in_ref, out_ref, send, recv, token):
        x, y, c = _my_pos()
        for r, (dx, dy, dc) in enumerate(_DEV_REL):
            pltpu.make_async_remote_copy(
                src_ref=in_ref, dst_ref=out_ref.at[r], send_sem=send.at[r], recv_sem=recv.at[r],
                device_id=(x ^ dx, y ^ dy, c ^ dc), device_id_type=MESH).start()
        token[...] = jnp.zeros_like(token)

    res = pl.pallas_call(
        body, name=name,
        out_shape=[pltpu.HBM((7,) + packed.shape, F32)] + [pltpu.SemaphoreType.DMA((7,))] * 2 + [_sds((8, 128), F32)],
        in_specs=[HBM], out_specs=[HBM, SEM, SEM, pl.BlockSpec(memory_space=pltpu.VMEM)],
        compiler_params=pltpu.CompilerParams(has_side_effects=EFFECT))(_in_hbm(packed))
    return res[0], (res[1], res[2]), res[3]


def _small_wait(packed, land, sems, after, name):
    def body(in_ref, land_ref, send, recv, after_ref, out_ref):
        x, y, c = _my_pos()
        for r, (dx, dy, dc) in enumerate(_DEV_REL):
            cp = pltpu.make_async_remote_copy(
                src_ref=in_ref, dst_ref=land_ref.at[r], send_sem=send.at[r], recv_sem=recv.at[r],
                device_id=(x ^ dx, y ^ dy, c ^ dc), device_id_type=MESH)
            cp.wait_send()
            cp.wait_recv()

    return pl.pallas_call(
        body, name=name, out_shape=pltpu.HBM(land.shape, land.dtype),
        in_specs=[ANY, HBM, SEM, SEM, ANY], out_specs=HBM, input_output_aliases={1: 0},
        compiler_params=pltpu.CompilerParams(has_side_effects=EFFECT))(packed, land, *sems, after)


def _sum_small(packed, land, tr):
    rows = packed.shape[0]
    x, y, c = _my_pos()
    me = (4 * x + 2 * y + c).astype(jnp.int32).reshape(1)

    def sbody(me_ref, p_ref, r_ref, o_ref):
        me_dev = me_ref[0]
        own = p_ref[...]
        acc = None
        for s in range(NDEV):
            rel = s ^ me_dev
            v = jnp.where(rel == 0, own, r_ref[jnp.maximum(rel - 1, 0)])
            acc = v if acc is None else acc + v
        o_ref[...] = acc

    return pl.pallas_call(
        sbody,
        grid_spec=pltpu.PrefetchScalarGridSpec(
            num_scalar_prefetch=1, grid=(rows // tr,),
            in_specs=[pl.BlockSpec((tr, 128), lambda i, me_ref: (i, 0)), pl.BlockSpec((7, tr, 128), lambda i, me_ref: (0, i, 0))],
            out_specs=pl.BlockSpec((tr, 128), lambda i, me_ref: (i, 0))),
        out_shape=_sds((rows, 128), F32), compiler_params=_cp("parallel"), name="sum_small")(me, packed, land)


def _rms_fwd(x, g, tm):
    n = x.shape[0]

    def body(x_ref, g_ref, h_ref):
        xv = x_ref[...]
        r = lax.rsqrt(jnp.mean(xv * xv, axis=-1, keepdims=True) + EPS)
        h_ref[...] = (xv * r * g_ref[...]).astype(BF16)

    return pl.pallas_call(
        body, grid=(n // tm,),
        in_specs=[pl.BlockSpec((tm, D), lambda i: (i, 0)), pl.BlockSpec((1, D), lambda i: (0, 0))],
        out_specs=pl.BlockSpec((tm, D), lambda i: (i, 0)), out_shape=_sds((n, D), BF16),
        compiler_params=_cp("parallel"), name="rms_fwd")(x, g)


def _rms_fwd_t(x, g, tm=512):
    n = x.shape[0]

    def body(x_ref, g_ref, h_ref, ht_ref):
        xv = x_ref[...]
        r = lax.rsqrt(jnp.mean(xv * xv, axis=-1, keepdims=True) + EPS)
        h = xv * r * g_ref[...]
        h_ref[...] = h.astype(BF16)
        ht_ref[...] = h.T.astype(BF16)

    return pl.pallas_call(
        body, grid=(n // tm,),
        in_specs=[pl.BlockSpec((tm, D), lambda i: (i, 0)), pl.BlockSpec((1, D), lambda i: (0, 0))],
        out_specs=[pl.BlockSpec((tm, D), lambda i: (i, 0)), pl.BlockSpec((D, tm), lambda i: (0, i))],
        out_shape=[_sds((n, D), BF16), _sds((D, n), BF16)], compiler_params=_cp("parallel"), name="rms_fwd_t")(x, g)


def _rms_bwd(dh, x, g, dres, tm=512):
    n = x.shape[0]

    def body(dh_ref, x_ref, g_ref, dr_ref, dx_ref, dg_ref):
        i = pl.program_id(0)
        xv = x_ref[...]
        r = lax.rsqrt(jnp.mean(xv * xv, axis=-1, keepdims=True) + EPS)
        xh = xv * r
        dhv = dh_ref[...]
        dxh = dhv * g_ref[...]
        dx_ref[...] = dr_ref[...] + r * (dxh - xh * jnp.mean(dxh * xh, axis=-1, keepdims=True))
        part = jnp.sum(dhv * xh, axis=0, keepdims=True)

        @pl.when(i == 0)
        def _():
            dg_ref[...] = part

        @pl.when(i > 0)
        def _():
            dg_ref[...] += part

    row = pl.BlockSpec((tm, D), lambda i: (i, 0))
    vec = pl.BlockSpec((1, D), lambda i: (0, 0))
    return pl.pallas_call(
        body, grid=(n // tm,), in_specs=[row, row, vec, row], out_specs=[row, vec],
        out_shape=[_sds((n, D), F32), _sds((1, D), F32)], compiler_params=_cp("arbitrary"), name="rms_bwd")(dh, x, g, dres)


def _loss_head(x, tgt, g, tm=512):
    def body(x_ref, t_ref, g_ref, dx_ref, dg_ref, ls_ref):
        i = pl.program_id(0)
        xv = x_ref[...]
        r = lax.rsqrt(jnp.mean(xv * xv, axis=-1, keepdims=True) + EPS)
        xh = xv * r
        gv = g_ref[...]
        diff = xh * gv - t_ref[...]
        dy = diff * (1.0 / D)
        dxh = dy * gv
        dx_ref[...] = r * (dxh - xh * jnp.mean(dxh * xh, axis=-1, keepdims=True))
        part_g = jnp.sum(dy * xh, axis=0, keepdims=True)
        part_l = jnp.sum(diff * diff, axis=0, keepdims=True)

        @pl.when(i == 0)
        def _():
            dg_ref[...] = part_g
            ls_ref[...] = part_l

        @pl.when(i > 0)
        def _():
            dg_ref[...] += part_g
            ls_ref[...] += part_l

        @pl.when(i == pl.num_programs(0) - 1)
        def _():
            tot = jnp.sum(ls_ref[...], axis=-1, keepdims=True) * (0.5 / D)
            ls_ref[...] = jnp.broadcast_to(tot, (1, D))

    row = pl.BlockSpec((tm, D), lambda i: (i, 0))
    vec = pl.BlockSpec((1, D), lambda i: (0, 0))
    return pl.pallas_call(
        body, grid=(S // tm,), in_specs=[row, row, vec], out_specs=[row, vec, vec],
        out_shape=[_sds((S, D), F32), _sds((1, D), F32), _sds((1, D), F32)],
        compiler_params=_cp("arbitrary"), name="loss_head")(x, tgt, g)


def _adamw(w, g, m, v):
    shape = w.shape
    cols = shape[-1] if w.ndim > 1 else shape[0]
    rows = w.size // cols
    w2, g2, m2, v2 = (t.reshape(rows, cols) for t in (w, g, m, v))
    tr = rows
    while tr * cols * 4 > (1 << 20) and tr % 16 == 0:
        tr //= 2
    c1 = 1.0 - ADAM_B1 ** ADAM_STEP
    c2 = 1.0 - ADAM_B2 ** ADAM_STEP

    def body(w_ref, g_ref, m_ref, v_ref, d_ref, nm_ref, nv_ref):
        gv = g_ref[...]
        mn = ADAM_B1 * m_ref[...] + (1.0 - ADAM_B1) * gv
        vn = ADAM_B2 * v_ref[...] + (1.0 - ADAM_B2) * (gv * gv)
        d_ref[...] = -ADAM_LR * ((mn / c1) / (jnp.sqrt(vn / c2) + ADAM_EPS) + ADAM_WD * w_ref[...])
        nm_ref[...] = mn
        nv_ref[...] = vn

    blk = pl.BlockSpec((tr, cols), lambda i: (i, 0))
    outs = pl.pallas_call(
        body, grid=(rows // tr,), in_specs=[blk] * 4, out_specs=[blk] * 3,
        out_shape=[_sds((rows, cols), F32)] * 3, compiler_params=_cp("parallel"), name="adamw")(w2, g2, m2, v2)
    return tuple(o.reshape(shape) for o in outs)


def _adamw_layer(l, w, g, m, v, outs, tr):
    cols = w.shape[-1]
    rows = w.size // (NL * cols)
    nb = rows // tr
    w2, m2, v2 = (t.reshape(NL * rows, cols) for t in (w, m, v))
    g2 = g.reshape(rows, cols)
    c1 = 1.0 - ADAM_B1 ** ADAM_STEP
    c2 = 1.0 - ADAM_B2 ** ADAM_STEP

    def body(w_ref, g_ref, m_ref, v_ref, d_in, nm_in, nv_in, go_in, d_ref, nm_ref, nv_ref, go_ref):
        gv = g_ref[...]
        mn = ADAM_B1 * m_ref[...] + (1.0 - ADAM_B1) * gv
        vn = ADAM_B2 * v_ref[...] + (1.0 - ADAM_B2) * (gv * gv)
        d_ref[...] = -ADAM_LR * ((mn / c1) / (jnp.sqrt(vn / c2) + ADAM_EPS) + ADAM_WD * w_ref[...])
        nm_ref[...] = mn
        nv_ref[...] = vn
        go_ref[...] = gv

    lay = pl.BlockSpec((tr, cols), lambda i: (l * nb + i, 0))
    return pl.pallas_call(
        body, grid=(nb,), in_specs=[lay, pl.BlockSpec((tr, cols), lambda i: (i, 0)), lay, lay] + [ANY] * 4,
        out_specs=[lay] * 4, out_shape=[_sds((NL * rows, cols), F32)] * 4,
        input_output_aliases={4: 0, 5: 1, 6: 2, 7: 3}, compiler_params=_cp("parallel"), name="adamw_layer")(w2, g2, m2, v2, *outs)


def _proj_fwd(h, wg, tm=512):
    def body(h_ref, w_ref, o_ref):
        o_ref[...] = _dot(h_ref[...], w_ref[...]).astype(BF16)

    return pl.pallas_call(
        body, grid=(NCHIP, S // tm),
        in_specs=[pl.BlockSpec((tm, D), lambda c, i: (i, 0)), pl.BlockSpec((None, D, CW), lambda c, i: (c, 0, 0))],
        out_specs=pl.BlockSpec((tm, CW), lambda c, i: (i, c)), out_shape=_sds((S, DIN), BF16),
        compiler_params=_cp("parallel", "parallel"), name="proj_fwd")(h, wg)


def _proj_bwd_x(dproj, wg, dep, tm=1024):
    def body(d_ref, w_ref, dep_ref, o_ref):
        k = pl.program_id(1)
        part = _dot_nt(d_ref[...], w_ref[...])

        @pl.when(k == 0)
        def _():
            o_ref[...] = part

        @pl.when(k > 0)
        def _():
            o_ref[...] += part

    return pl.pallas_call(
        body, grid=(S // tm, NCHIP),
        in_specs=[pl.BlockSpec((tm, CW), lambda i, k: (i, k)), pl.BlockSpec((None, D, CW), lambda i, k: (k, 0, 0)), ANY],
        out_specs=pl.BlockSpec((tm, D), lambda i, k: (i, 0)), out_shape=_sds((S, D), F32),
        compiler_params=_cp("parallel", "arbitrary"), name="proj_bwd_x")(dproj, wg, dep)


def _proj_bwd_w(ht, dproj):
    def body(h_ref, d_ref, o_ref):
        o_ref[...] = _dot(h_ref[...], d_ref[...]).astype(BF16)

    return pl.pallas_call(
        body, grid=(NCHIP, NJ),
        in_specs=[pl.BlockSpec((D, S), lambda c, j: (0, 0)), pl.BlockSpec((S, TN_IN), lambda c, j: (0, c * NJ + j))],
        out_specs=pl.BlockSpec((None, D, TN_IN), lambda c, j: (c, 0, j)), out_shape=_sds((NCHIP, D, CW), BF16),
        compiler_params=_cp("parallel", "parallel"), name="proj_bwd_w")(ht, dproj)


def _merge_fwd(y_all, wbr, proj, tm=256):
    cb = D // NCHIP

    def body(y_ref, w_ref, *rest):
        g_refs, z_ref = rest[:8], rest[8]
        for c in range(NCHIP):
            acc = None
            for b in range(4):
                g = g_refs[2 * b + c // 2][:, (c % 2) * cb:(c % 2 + 1) * cb].astype(F32)
                t = _dot(y_ref[b], w_ref[c, b]) * _sigmoid(g)
                acc = t if acc is None else acc + t
            z_ref[:, c * cb:(c + 1) * cb] = acc.astype(BF16)

    g_specs = [pl.BlockSpec((tm, W), functools.partial(lambda j, i: (i, P_GM + j), j)) for j in range(8)]
    return pl.pallas_call(
        body, grid=(S // tm,),
        in_specs=[pl.BlockSpec((4, tm, W), lambda i: (0, i, 0)), pl.BlockSpec((NCHIP, 4, W, cb), lambda i: (0, 0, 0, 0))] + g_specs,
        out_specs=pl.BlockSpec((tm, D), lambda i: (i, 0)), out_shape=_sds((S, D), BF16),
        compiler_params=_cp("parallel"), name="merge_fwd")(y_all, wbr, *([proj] * 8))


def _merge_bwd(dz, y_all, wbr, proj, dproj, tm=512):
    cb = D // NCHIP
    ni = S // tm

    def body(dz_ref, y_ref, w_ref, ga_ref, gb_ref, dp_in, dp_ref, dy_ref, dw_ref, acc, obuf, osem):
        b = pl.program_id(0)
        i = pl.program_id(1)

        @pl.when(i == 0)
        def _():
            acc[...] = jnp.zeros_like(acc)

        yv = y_ref[...]
        dys = []

        def fill(slot):
            ws_ = [w_ref[c] for c in range(NCHIP)]
            t = [_dot(yv, wv) for wv in ws_]
            dts = []
            for c in range(NCHIP):
                g_ref = ga_ref if c < 2 else gb_ref
                g = _sigmoid(g_ref[:, (c % 2) * cb:(c % 2 + 1) * cb].astype(F32))
                dzc = dz_ref[:, c * cb:(c + 1) * cb].astype(F32)
                slot[:, c * cb:(c + 1) * cb] = (dzc * t[c] * g * (1.0 - g)).astype(BF16)
                dts.append((dzc * g).astype(BF16))
            dy = None
            for c in range(NCHIP):
                part = _dot_nt(dts[c], ws_[c])
                dy = part if dy is None else dy + part
            for c in range(NCHIP):
                acc[c] += _dot_tn(yv, dts[c])
            dys.append(dy)

        _tile_put(obuf, osem, lambda st: dp_ref.at[pl.ds((st % ni) * tm, tm), pl.ds(P_GM * W + (st // ni) * D, D)],
                  b * ni + i, 4 * ni, fill)
        dy_ref[...] = dys[0].astype(BF16)

        @pl.when(i == ni - 1)
        def _():
            dw_ref[...] = acc[...].astype(BF16)

    return pl.pallas_call(
        body, grid=(4, ni),
        in_specs=[pl.BlockSpec((tm, D), lambda b, i: (i, 0)), pl.BlockSpec((None, tm, W), lambda b, i: (b, i, 0)),
                  pl.BlockSpec((NCHIP, None, W, cb), lambda b, i: (0, b, 0, 0)),
                  pl.BlockSpec((tm, W), lambda b, i: (i, P_GM + 2 * b)), pl.BlockSpec((tm, W), lambda b, i: (i, P_GM + 2 * b + 1)), ANY],
        out_specs=[ANY, pl.BlockSpec((None, tm, W), lambda b, i: (b, i, 0)), pl.BlockSpec((NCHIP, None, W, cb), lambda b, i: (0, b, 0, 0))],
        out_shape=[_sds((S, DIN), BF16), _sds((4, S, W), BF16), _sds((NCHIP, 4, W, cb), BF16)],
        scratch_shapes=[pltpu.VMEM((NCHIP, W, cb), F32), pltpu.VMEM((2, tm, D), BF16), pltpu.SemaphoreType.DMA((2,))],
        input_output_aliases={5: 0}, compiler_params=_cp("arbitrary", "arbitrary"), name="merge_bwd")(dz, y_all, wbr, proj, proj, dproj)


def _out_fwd_norm(z, wo, x, g_next, tm=512):
    def body(z_ref, w_ref, x_ref, g_ref, o_ref, h_ref, ht_ref):
        xn = x_ref[...] + _dot(z_ref[...], w_ref[...])
        o_ref[...] = xn
        r = lax.rsqrt(jnp.mean(xn * xn, axis=-1, keepdims=True) + EPS)
        h = xn * r * g_ref[...]
        h_ref[...] = h.astype(BF16)
        ht_ref[...] = h.T.astype(BF16)

    row = pl.BlockSpec((tm, D), lambda i: (i, 0))
    return pl.pallas_call(
        body, grid=(S // tm,),
        in_specs=[row, pl.BlockSpec((D, D), lambda i: (0, 0)), row, pl.BlockSpec((1, D), lambda i: (0, 0))],
        out_specs=[row, row, pl.BlockSpec((D, tm), lambda i: (0, i))],
        out_shape=[_sds((S, D), F32), _sds((S, D), BF16), _sds((D, S), BF16)],
        compiler_params=_cp("parallel"), name="out_fwd_norm")(z, wo, x, g_next)


def _out_fwd(z, wo, x, tm=512):
    def body(z_ref, w_ref, x_ref, o_ref):
        o_ref[...] = x_ref[...] + _dot(z_ref[...], w_ref[...])

    row = pl.BlockSpec((tm, D), lambda i: (i, 0))
    return pl.pallas_call(
        body, grid=(S // tm,), in_specs=[row, pl.BlockSpec((D, D), lambda i: (0, 0)), row], out_specs=row,
        out_shape=_sds((S, D), F32), compiler_params=_cp("parallel"), name="out_fwd")(z, wo, x)


def _out_bwd(dx, z, wo, dep, tm=512):
    ni = S // tm

    def body(dx_ref, z_ref, w_ref, dep_ref, dz_ref, dw_ref, acc):
        i = pl.program_id(0)
        dxb = dx_ref[...].astype(BF16)
        dz_ref[...] = _dot_nt(dxb, w_ref[...]).astype(BF16)
        part = _dot_tn(z_ref[...], dxb)

        @pl.when(i == 0)
        def _():
            acc[...] = part

        @pl.when(i > 0)
        def _():
            acc[...] += part

        @pl.when(i == ni - 1)
        def _():
            dw_ref[...] = acc[...].astype(BF16)

    row = pl.BlockSpec((tm, D), lambda i: (i, 0))
    full = pl.BlockSpec((D, D), lambda i: (0, 0))
    return pl.pallas_call(
        body, grid=(ni,), in_specs=[row, row, full, ANY], out_specs=[row, full],
        out_shape=[_sds((S, D), BF16), _sds((D, D), BF16)], scratch_shapes=[pltpu.VMEM((D, D), F32)],
        compiler_params=_cp("arbitrary"), name="out_bwd")(dx, z, wo, dep)


def _gelu_parts(a):
    cdf = 0.5 * (1.0 + lax.erf(a * INV_SQRT2))
    return a * cdf, cdf


def _ln_parts(v):
    mu = jnp.mean(v, axis=-1, keepdims=True)
    vc = v - mu
    rs = lax.rsqrt(jnp.mean(vc * vc, axis=-1, keepdims=True) + EPS)
    return vc * rs, rs


def _causal_mask():
    return lax.broadcasted_iota(jnp.int32, (HD, HD), 0) >= lax.broadcasted_iota(jnp.int32, (HD, HD), 1)


def _gmlp_fwd(proj, lg, lb, ws, bias, y_all, tm=512):
    def body(uv_ref, gt_ref, lg_ref, lb_ref, ws_ref, b_ref, y_in, y_ref):
        act, _ = _gelu_parts(uv_ref[...].astype(F32))
        u = act[:, :W]
        xh, _ = _ln_parts(act[:, W:])
        vn = (xh * lg_ref[...] + lb_ref[...]).astype(BF16)
        gt = gt_ref[...].astype(F32)
        us = u * (gt * _sigmoid(gt))
        mask = _causal_mask()
        for h in range(4):
            wm = jnp.where(mask, ws_ref[h], 0.0).astype(BF16)
            cs = slice(h * HD, (h + 1) * HD)
            for c in range(tm // HD):
                rs_ = slice(c * HD, (c + 1) * HD)
                mixed = _dot(wm, vn[rs_, cs]) + b_ref[h]
                y_ref[rs_, cs] = (us[rs_, cs] * mixed).astype(BF16)

    vec = pl.BlockSpec((1, W), lambda i: (0, 0))
    mats = pl.BlockSpec((4, HD, HD), lambda i: (0, 0, 0))
    return pl.pallas_call(
        body, grid=(S // tm,),
        in_specs=[pl.BlockSpec((tm, 2 * W), lambda i: (i, 0)), pl.BlockSpec((tm, W), lambda i: (i, P_AGATE)), vec, vec, mats, mats, ANY],
        out_specs=pl.BlockSpec((None, tm, W), lambda i: (0, i, 0)), out_shape=_sds((4, S, W), BF16),
        input_output_aliases={6: 0}, compiler_params=_cp("parallel"), name="gmlp_fwd")(proj, proj, lg, lb, ws, bias, y_all)


def _gmlp_bwd(proj, dy_all, lg, lb, ws, bias, dproj, tm=256):
    ni = S // tm

    def body(uv_ref, gt_ref, dy_ref, lg_ref, lb_ref, ws_ref, b_ref, dp_in, dp_ref, dws_ref, dbs_ref, dlg_ref, dlb_ref, mix_s, dvn_s):
        i = pl.program_id(0)

        @pl.when(i == 0)
        def _():
            dws_ref[...] = jnp.zeros_like(dws_ref)
            dbs_ref[...] = jnp.zeros_like(dbs_ref)
            dlg_ref[...] = jnp.zeros_like(dlg_ref)
            dlb_ref[...] = jnp.zeros_like(dlb_ref)

        a0 = uv_ref[...].astype(F32)
        act, cdf = _gelu_parts(a0)
        u = act[:, :W]
        xh, rs = _ln_parts(act[:, W:])
        lgv = lg_ref[...]
        vn = (xh * lgv + lb_ref[...]).astype(BF16)
        mask = _causal_mask()
        wms = [jnp.where(mask, ws_ref[h], 0.0).astype(BF16) for h in range(4)]
        blocks = [(slice(c * HD, (c + 1) * HD), slice(h * HD, (h + 1) * HD), h) for h in range(4) for c in range(tm // HD)]
        for rs_, cs, h in blocks:
            mix_s[rs_, cs] = _dot(wms[h], vn[rs_, cs]) + b_ref[h]
        mixed = mix_s[...]
        gt = gt_ref[...].astype(F32)
        sg = _sigmoid(gt)
        sl = gt * sg
        dyv = dy_ref[...].astype(F32)
        dum = dyv * sl
        dgate = dyv * (u * mixed) * (sg * (1.0 + gt * (1.0 - sg)))
        du = dum * mixed
        dmix = dum * u
        dmb = dmix.astype(BF16)
        for rs_, cs, h in blocks:
            dvn_s[rs_, cs] = _dot_tn(wms[h], dmb[rs_, cs])
        for rs_, cs, h in blocks:
            dws_ref[h] += _dot_nt(dmb[rs_, cs], vn[rs_, cs])
            dbs_ref[h] += dmix[rs_, cs]
        dvn = dvn_s[...]
        dlg_ref[...] += jnp.sum(dvn * xh, axis=0, keepdims=True)
        dlb_ref[...] += jnp.sum(dvn, axis=0, keepdims=True)
        dxh = dvn * lgv
        dv = rs * (dxh - jnp.mean(dxh, axis=-1, keepdims=True) - xh * jnp.mean(dxh * xh, axis=-1, keepdims=True))
        gp = cdf + a0 * (jnp.exp(-0.5 * a0 * a0) * INV_SQRT2PI)
        dp_ref[:, :W] = (du * gp[:, :W]).astype(BF16)
        dp_ref[:, W:2 * W] = (dv * gp[:, W:]).astype(BF16)
        dp_ref[:, 2 * W:] = dgate.astype(BF16)

        @pl.when(i == ni - 1)
        def _():
            for h in range(4):
                dws_ref[h] = jnp.where(mask, dws_ref[h], 0.0)
                dbs_ref[h] = jnp.broadcast_to(jnp.sum(dbs_ref[h], axis=1, keepdims=True), (HD, HD))

    vec = pl.BlockSpec((1, W), lambda i: (0, 0))
    mats = pl.BlockSpec((4, HD, HD), lambda i: (0, 0, 0))
    return pl.pallas_call(
        body, grid=(ni,),
        in_specs=[pl.BlockSpec((tm, 2 * W), lambda i: (i, 0)), pl.BlockSpec((tm, W), lambda i: (i, P_AGATE)),
                  pl.BlockSpec((None, tm, W), lambda i: (0, i, 0)), vec, vec, mats, mats, ANY],
        out_specs=[pl.BlockSpec((tm, 3 * W), lambda i: (i, 0)), mats, mats, vec, vec],
        out_shape=[_sds((S, DIN), BF16), _sds((4, HD, HD), F32), _sds((4, HD, HD), F32), _sds((1, W), F32), _sds((1, W), F32)],
        scratch_shapes=[pltpu.VMEM((tm, W), F32), pltpu.VMEM((tm, W), F32)],
        input_output_aliases={7: 0}, compiler_params=_cp("arbitrary"), name="gmlp_bwd")(proj, proj, dy_all, lg, lb, ws, bias, dproj)


def _pool_diff(p, halo, row0, tm):
    xx = jnp.concatenate([halo, p], axis=0)
    t1 = (row0 + 1 + lax.broadcasted_iota(jnp.int32, (tm, 1), 0)).astype(F32)
    out = []
    for g, win in enumerate(POOL_WINDOWS):
        s = xx[:, g * HD:(g + 1) * HD]
        sh = 1
        while sh < win:
            s = s + pltpu.roll(s, sh, 0)
            sh *= 2
        out.append(s[HALO:] / jnp.minimum(t1, float(win)) - p[:, g * HD:(g + 1) * HD])
    return out


def _pool_fwd(proj, pw, sc, y_all, tm=512):
    rb = tm // HALO

    def body(p_ref, h_ref, gt_ref, pw_ref, sc_ref, y_in, y_ref):
        i = pl.program_id(0)
        halo = jnp.where(i > 0, h_ref[...].astype(F32), 0.0)
        ds = _pool_diff(p_ref[...].astype(F32), halo, i * tm, tm)
        gt = gt_ref[...].astype(F32)
        sl = gt * _sigmoid(gt)
        for g in range(4):
            cs = slice(g * HD, (g + 1) * HD)
            lin = _dot(ds[g].astype(BF16), pw_ref[g].astype(BF16))
            y_ref[:, cs] = (lin * sc_ref[:, cs] * sl[:, cs]).astype(BF16)

    return pl.pallas_call(
        body, grid=(S // tm,),
        in_specs=[pl.BlockSpec((tm, W), lambda i: (i, P_PIN)),
                  pl.BlockSpec((HALO, W), lambda i: (jnp.maximum(i * rb - 1, 0), P_PIN)),
                  pl.BlockSpec((tm, W), lambda i: (i, P_PGATE)),
                  pl.BlockSpec((4, HD, HD), lambda i: (0, 0, 0)), pl.BlockSpec((1, W), lambda i: (0, 0)), ANY],
        out_specs=pl.BlockSpec((None, tm, W), lambda i: (1, i, 0)), out_shape=_sds((4, S, W), BF16),
        input_output_aliases={5: 0}, compiler_params=_cp("parallel"), name="pool_fwd")(proj, proj, proj, pw, sc, y_all)


def _pool_bwd(proj, dy_all, pw, sc, dproj, tm=256):
    ni = S // tm
    rb = tm // HALO
    last_rb = S // HALO - 1
    rx = tm + HALO

    def body(p_ref, h_ref, gt_ref, gh_ref, dy_ref, dyh_ref, pw_ref, sc_ref, dp_in, dp_ref, dpw_ref, dsc_ref, obuf, osem):
        i = pl.program_id(0)

        @pl.when(i == 0)
        def _():
            dpw_ref[...] = jnp.zeros_like(dpw_ref)
            dsc_ref[...] = jnp.zeros_like(dsc_ref)

        halo = jnp.where(i > 0, h_ref[...].astype(F32), 0.0)
        ds = _pool_diff(p_ref[...].astype(F32), halo, i * tm, tm)
        nxt = i < ni - 1
        gx = jnp.concatenate([gt_ref[...], gh_ref[...]], axis=0).astype(F32)
        dyx = jnp.concatenate([dy_ref[...].astype(F32), jnp.where(nxt, dyh_ref[...].astype(F32), 0.0)], axis=0)
        sgx = _sigmoid(gx)
        slx = gx * sgx
        scv = sc_ref[...]
        dlinx = dyx * slx * scv
        t1 = (i * tm + 1 + lax.broadcasted_iota(jnp.int32, (rx, 1), 0)).astype(F32)
        gt, sg, sl, dyv = gx[:tm], sgx[:tm], slx[:tm], dyx[:tm]
        dsl = sg * (1.0 + gt * (1.0 - sg))

        def fill(slot):
            for g, win in enumerate(POOL_WINDOWS):
                cs = slice(g * HD, (g + 1) * HD)
                wv = pw_ref[g].astype(BF16)
                dlb = dlinx[:, cs].astype(BF16)
                ddx = _dot_nt(dlb, wv)
                f = ddx / jnp.minimum(t1, float(win))
                sh = 1
                while sh < win:
                    f = f + pltpu.roll(f, rx - sh, 0)
                    sh *= 2
                slot[:, cs] = (f[:tm] - ddx[:tm]).astype(BF16)
                db = ds[g].astype(BF16)
                lin = _dot(db, wv)
                slot[:, W + g * HD:W + (g + 1) * HD] = (dyv[:, cs] * lin * scv[:, cs] * dsl[:, cs]).astype(BF16)
                dsc_ref[:, cs] += jnp.sum(dyv[:, cs] * sl[:, cs] * lin, axis=0, keepdims=True)
                dpw_ref[g] += _dot_tn(db, dlb[:tm])

        _tile_put(obuf, osem, lambda st: dp_ref.at[pl.ds(st * tm, tm), pl.ds(P_PIN * W, 2 * W)], i, ni, fill)

    mats = pl.BlockSpec((4, HD, HD), lambda i: (0, 0, 0))
    vec = pl.BlockSpec((1, W), lambda i: (0, 0))
    return pl.pallas_call(
        body, grid=(ni,),
        in_specs=[pl.BlockSpec((tm, W), lambda i: (i, P_PIN)),
                  pl.BlockSpec((HALO, W), lambda i: (jnp.maximum(i * rb - 1, 0), P_PIN)),
                  pl.BlockSpec((tm, W), lambda i: (i, P_PGATE)),
                  pl.BlockSpec((HALO, W), lambda i: (jnp.minimum((i + 1) * rb, last_rb), P_PGATE)),
                  pl.BlockSpec((None, tm, W), lambda i: (1, i, 0)),
                  pl.BlockSpec((None, HALO, W), lambda i: (1, jnp.minimum((i + 1) * rb, last_rb), 0)),
                  mats, vec, ANY],
        out_specs=[ANY, mats, vec],
        out_shape=[_sds((S, DIN), BF16), _sds((4, HD, HD), F32), _sds((1, W), F32)],
        scratch_shapes=[pltpu.VMEM((2, tm, 2 * W), BF16), pltpu.SemaphoreType.DMA((2,))],
        input_output_aliases={8: 0}, compiler_params=_cp("arbitrary"), name="pool_bwd")(proj, proj, proj, proj, dy_all, dy_all, pw, sc, dproj)


ATT_STEP = ((1, 4), (4, 1), (4, 1))
ATT_GROUP = 16
ATT_GROUP_BWD = 8


def _att_band():
    qi = lax.broadcasted_iota(jnp.int32, (HD, 2 * HD), 0)
    kj = lax.broadcasted_iota(jnp.int32, (HD, 2 * HD), 1)
    return jnp.logical_and(kj >= qi, kj <= qi + HD), kj < HD


def _att_keys(kp_ref, ko_ref, vp_ref, vo_ref, a, jj):
    if jj == 0:
        return (jnp.concatenate([kp_ref[a], ko_ref[a, :HD, :]], axis=0), jnp.concatenate([vp_ref[a], vo_ref[a, :HD, :]], axis=0))
    return ko_ref[a, (jj - 1) * HD:(jj + 1) * HD, :], vo_ref[a, (jj - 1) * HD:(jj + 1) * HD, :]


def _dilate(src, dst, d, rows, cast=None):
    for r in range(d):
        for h in range(4):
            v = src.at[h][pl.ds(r, rows // d, stride=d), :] if d > 1 else src[h]
            dst[r * 4 + h] = v if cast is None else v.astype(cast)


def _undilate(src, dst, d, rows):
    for r in range(d):
        for h in range(4):
            if d > 1:
                dst.at[h][pl.ds(r, rows // d, stride=d), :] = src[r * 4 + h].astype(F32)
            else:
                dst[h] = src[h].astype(F32)


def _dil_spec(d, tm):
    return pl.BlockSpec((4 * d, tm // d, HD), lambda i: (0, i, 0))


def _att_prep(proj, tm=512):
    def body(q0, q1, q2, k_ref, v_ref, *rest):
        outs, scr = rest[:9], rest[9]
        for j, (src, dsts) in enumerate(((q0, ((0, outs[0]),)), (q1, ((1, outs[1]),)), (q2, ((2, outs[2]),)),
                                         (k_ref, tuple((g, outs[3 + g]) for g in range(3))),
                                         (v_ref, tuple((g, outs[6 + g]) for g in range(3))))):
            for h in range(4):
                scr[j, h] = src[:, h * HD:(h + 1) * HD].astype(F32)
            for g, dst in dsts:
                _dilate(scr.at[j], dst, DILATIONS[g], tm, BF16)

    def piece(p):
        return pl.BlockSpec((tm, W), lambda i: (i, p))

    shapes = [_sds((4 * d, S // d, HD), BF16) for d in DILATIONS]
    res = pl.pallas_call(
        body, grid=(S // tm,),
        in_specs=[piece(P_CQ), piece(P_CQ + 1), piece(P_CQ + 2), piece(P_CK), piece(P_CV)],
        out_specs=[_dil_spec(d, tm) for d in DILATIONS] * 3, out_shape=shapes * 3,
        scratch_shapes=[pltpu.VMEM((5, 4, tm, HD), F32)],
        compiler_params=_cp("parallel"), name="att_prep")(proj, proj, proj, proj, proj)
    return res[0:3], res[3:6], res[6:9]


def _att_specs(g):
    d = DILATIONS[g]
    nres, njb = ATT_STEP[g]
    nb = S // d // HD
    own = pl.BlockSpec((4 * nres, njb * HD, HD), lambda r, j: (r, j, 0))
    prev = pl.BlockSpec((4 * nres, HD, HD), lambda r, j: (r, jnp.maximum(j * njb - 1, 0), 0))
    nxt = pl.BlockSpec((4 * nres, HD, HD), lambda r, j: (r, jnp.minimum((j + 1) * njb, nb - 1), 0))
    return (d // nres, nb // njb), own, prev, nxt


def _att_fwd(q, k, v, g):
    d = DILATIONS[g]
    nres, njb = ATT_STEP[g]
    grid, own, prev, _ = _att_specs(g)

    def body(q_ref, kp_ref, ko_ref, vp_ref, vo_ref, o_ref, l_ref):
        jb = pl.program_id(1)
        band, is_prev = _att_band()
        no_prev = jnp.where(is_prev, jnp.where(jb > 0, 0.0, NEG), 0.0)
        blocks = [(a, jj) for jj in range(njb) for a in range(4 * nres)]
        for g0 in range(0, len(blocks), ATT_GROUP):
            grp = blocks[g0:g0 + ATT_GROUP]
            s, v2 = [], []
            for a, jj in grp:
                k2_, v2_ = _att_keys(kp_ref, ko_ref, vp_ref, vo_ref, a, jj)
                s_ = jnp.where(band, _dot_nt(q_ref[a, jj * HD:(jj + 1) * HD, :], k2_) * SCALE, NEG)
                s.append(s_ + no_prev if jj == 0 else s_)
                v2.append(v2_)
            m = [jnp.max(s_, axis=-1, keepdims=True) for s_ in s]
            e = [jnp.exp(s_ - m_) for s_, m_ in zip(s, m)]
            den = [jnp.sum(e_, axis=-1, keepdims=True) for e_ in e]
            inv = [1.0 / d_ for d_ in den]
            for i, (a, jj) in enumerate(grp):
                rs_ = slice(jj * HD, (jj + 1) * HD)
                o_ref[a, rs_, :] = _dot((e[i] * inv[i]).astype(BF16), v2[i]).astype(BF16)
                l_ref[a, rs_, :] = jnp.broadcast_to(m[i] + jnp.log(den[i]), (HD, HD))

    return pl.pallas_call(
        body, grid=grid, in_specs=[own, prev, own, prev, own], out_specs=[own, own],
        out_shape=[_sds((4 * d, S // d, HD), BF16), _sds((4 * d, S // d, HD), F32)],
        compiler_params=_cp("parallel", "parallel"), name="att_fwd")(q, k, k, v, v)


def _att_mix(os_, ls_, proj, y_all, tm=512):
    def body(o0, o1, o2, l0, l1, l2, gt_ref, y_in, y_ref, om_ref, lt_ref, so1, so2, sl1, sl2):
        _undilate(o1, so1, DILATIONS[1], tm)
        _undilate(o2, so2, DILATIONS[2], tm)
        _undilate(l1, sl1, DILATIONS[1], tm)
        _undilate(l2, sl2, DILATIONS[2], tm)
        for h in range(4):
            a, b, c = l0[h], sl1[h], sl2[h]
            m = jnp.maximum(jnp.maximum(a, b), c)
            ea, eb, ec = jnp.exp(a - m), jnp.exp(b - m), jnp.exp(c - m)
            z = ea + eb + ec
            inv = 1.0 / z
            o = (ea * inv) * o0[h] + (eb * inv) * so1[h] + (ec * inv) * so2[h]
            gt = gt_ref[:, h * HD:(h + 1) * HD].astype(F32)
            om_ref[h] = o
            lt_ref[h] = m + jnp.log(z)
            y_ref[:, h * HD:(h + 1) * HD] = (o * (gt * _sigmoid(gt))).astype(BF16)

    dil = [_dil_spec(d, tm) for d in DILATIONS]
    return pl.pallas_call(
        body, grid=(S // tm,),
        in_specs=dil * 2 + [pl.BlockSpec((tm, W), lambda i: (i, P_CGATE)), ANY],
        out_specs=[pl.BlockSpec((None, tm, W), lambda i: (2, i, 0)), dil[0], dil[0]],
        out_shape=[_sds((4, S, W), BF16), _sds((4, S, HD), F32), _sds((4, S, HD), F32)],
        scratch_shapes=[pltpu.VMEM((4, tm, HD), F32)] * 4,
        input_output_aliases={7: 0}, compiler_params=_cp("parallel"), name="att_mix")(*os_, *ls_, proj, y_all)


def _att_bwd_pre(dy_all, proj, om, lse, dproj, tm=512):
    def body(dy_ref, gt_ref, om_ref, ls_ref, dp_in, *rest):
        dos, dls, lss, dp_ref, sdo, sdl = rest[0:3], rest[3:6], rest[6:8], rest[8], rest[9], rest[10]
        for h in range(4):
            cs = slice(h * HD, (h + 1) * HD)
            gt = gt_ref[:, cs].astype(F32)
            sg = _sigmoid(gt)
            dyv = dy_ref[:, cs].astype(F32)
            o = om_ref[h]
            do = dyv * (gt * sg)
            dp_ref[:, cs] = (dyv * o * (sg * (1.0 + gt * (1.0 - sg)))).astype(BF16)
            sdo[h] = do
            sdl[h] = jnp.broadcast_to(jnp.sum(do * o, axis=-1, keepdims=True), (tm, HD))
        for g, d in enumerate(DILATIONS):
            _dilate(sdo, dos[g], d, tm, BF16)
            _dilate(sdl, dls[g], d, tm)
            if g > 0:
                _dilate(ls_ref, lss[g - 1], d, tm)

    dil = [_dil_spec(d, tm) for d in DILATIONS]
    gcol = pl.BlockSpec((tm, W), lambda i: (i, P_CGATE))
    res = pl.pallas_call(
        body, grid=(S // tm,),
        in_specs=[pl.BlockSpec((None, tm, W), lambda i: (2, i, 0)), gcol, dil[0], dil[0], ANY],
        out_specs=dil + dil + dil[1:] + [gcol],
        out_shape=([_sds((4 * d, S // d, HD), BF16) for d in DILATIONS] + [_sds((4 * d, S // d, HD), F32) for d in DILATIONS]
                   + [_sds((4 * d, S // d, HD), F32) for d in DILATIONS[1:]] + [_sds((S, DIN), BF16)]),
        scratch_shapes=[pltpu.VMEM((4, tm, HD), F32)] * 2,
        input_output_aliases={4: 8}, compiler_params=_cp("parallel"), name="att_bwd_pre")(dy_all, proj, om, lse, dproj)
    return res[0:3], res[3:6], [lse] + list(res[6:8]), res[8]


def _att_bwd(q, k, v, do, lse, delta, g):
    d = DILATIONS[g]
    nres, njb = ATT_STEP[g]
    grid, own, prev, nxt = _att_specs(g)

    def body(qa_ref, qn_ref, kp_ref, ko_ref, vp_ref, vo_ref, doa_ref, don_ref, la_ref, ln_ref, da_ref, dn_ref,
             dq_ref, dk_ref, dv_ref):
        jb = pl.program_id(1)
        band, is_prev = _att_band()
        m_next = lax.broadcasted_iota(jnp.int32, (HD, HD), 1) >= lax.broadcasted_iota(jnp.int32, (HD, HD), 0)
        has_prev = jnp.where(is_prev, jnp.where(jb > 0, 1.0, 0.0), 1.0)
        has_next = jnp.where(jb < grid[1] - 1, 1.0, 0.0)

        def wide(t):
            return jnp.concatenate([t, t], axis=1)

        blocks = [(a, jj) for jj in range(njb) for a in range(4 * nres)]
        for g0 in range(0, len(blocks), ATT_GROUP_BWD):
            grp = blocks[g0:g0 + ATT_GROUP_BWD]
            ops = []
            for a, jj in grp:
                rs_ = slice(jj * HD, (jj + 1) * HD)
                k2, v2 = _att_keys(kp_ref, ko_ref, vp_ref, vo_ref, a, jj)
                if jj == njb - 1:
                    qn, don, lsn, dln, fn = qn_ref[a], don_ref[a], ln_ref[a], dn_ref[a], has_next
                else:
                    ns = slice((jj + 1) * HD, (jj + 2) * HD)
                    qn, don, lsn, dln, fn = qa_ref[a, ns, :], doa_ref[a, ns, :], la_ref[a, ns, :], da_ref[a, ns, :], None
                ops.append(dict(qa=qa_ref[a, rs_, :], doa=doa_ref[a, rs_, :], lsa=wide(la_ref[a, rs_, :]),
                                dla=wide(da_ref[a, rs_, :]), k2=k2, v2=v2, ko=ko_ref[a, rs_, :], vo=vo_ref[a, rs_, :],
                                qn=qn, don=don, lsn=lsn, dln=dln, fn=fn, first=jj == 0))
            sa = [_dot_nt(o["qa"], o["k2"]) for o in ops]
            dpa = [_dot_nt(o["doa"], o["v2"]) for o in ops]
            sn = [_dot_nt(o["qn"], o["ko"]) for o in ops]
            dpn = [_dot_nt(o["don"], o["vo"]) for o in ops]
            pa, pn = [], []
            for o, sa_, sn_ in zip(ops, sa, sn):
                p_ = jnp.where(band, jnp.exp(sa_ * SCALE - o["lsa"]), 0.0)
                pa.append(p_ * has_prev if o["first"] else p_)
                p_ = jnp.where(m_next, jnp.exp(sn_ * SCALE - o["lsn"]), 0.0)
                pn.append(p_ if o["fn"] is None else p_ * o["fn"])
            dsa = [(p_ * (dp_ - o["dla"]) * SCALE).astype(BF16) for p_, dp_, o in zip(pa, dpa, ops)]
            dsn = [(p_ * (dp_ - o["dln"]) * SCALE).astype(BF16) for p_, dp_, o in zip(pn, dpn, ops)]
            for i, (a, jj) in enumerate(grp):
                rs_ = slice(jj * HD, (jj + 1) * HD)
                o = ops[i]
                dq_ref[a, rs_, :] = _dot(dsa[i], o["k2"]).astype(BF16)
                q2 = jnp.concatenate([o["qa"], o["qn"]], axis=0)
                do2 = jnp.concatenate([o["doa"], o["don"]], axis=0)
                dk_ref[a, rs_, :] = _dot_tn(jnp.concatenate([dsa[i][:, HD:], dsn[i]], axis=0), q2).astype(BF16)
                dv_ref[a, rs_, :] = _dot_tn(jnp.concatenate([pa[i][:, HD:].astype(BF16), pn[i].astype(BF16)], axis=0),
                                            do2).astype(BF16)

    return pl.pallas_call(
        body, grid=grid, in_specs=[own, nxt, prev, own, prev, own, own, nxt, own, nxt, own, nxt],
        out_specs=[own, own, own], out_shape=[_sds((4 * d, S // d, HD), BF16)] * 3,
        compiler_params=_cp("parallel", "parallel"), name="att_bwd")(q, q, k, k, v, v, do, do, lse, lse, delta, delta)


def _att_bwd_post(dqs, dks, dvs, dproj, tm=512):
    def body(*refs):
        dq, dk, dv, dp_ref, scr = refs[0:3], refs[3:6], refs[6:9], refs[10], refs[11]
        for g in range(3):
            _undilate(dq[g], scr, DILATIONS[g], tm)
            for h in range(4):
                dp_ref[:, g * W + h * HD:g * W + (h + 1) * HD] = scr[h].astype(BF16)
        for j, parts in enumerate((dk, dv)):
            acc = None
            for g in range(3):
                _undilate(parts[g], scr, DILATIONS[g], tm)
                vals = [scr[h] for h in range(4)]
                acc = vals if acc is None else [x + y for x, y in zip(acc, vals)]
            for h in range(4):
                dp_ref[:, (3 + j) * W + h * HD:(3 + j) * W + (h + 1) * HD] = acc[h].astype(BF16)

    dil = [_dil_spec(d, tm) for d in DILATIONS]
    return pl.pallas_call(
        body, grid=(S // tm,), in_specs=dil * 3 + [ANY],
        out_specs=pl.BlockSpec((tm, 5 * W), lambda i: (i, 1)), out_shape=_sds((S, DIN), BF16),
        scratch_shapes=[pltpu.VMEM((4, tm, HD), F32)],
        input_output_aliases={9: 0}, compiler_params=_cp("parallel"), name="att_bwd_post")(*dqs, *dks, *dvs, dproj)


def _mem_kv_fwd(mem_n, wkv):
    m = mem_n.shape[0]

    def body(a_ref, w_ref, o_ref):
        o_ref[...] = _dot(a_ref[...], w_ref[...])

    return pl.pallas_call(body, out_shape=_sds((m, 2 * W), F32), compiler_params=_cp(), name="mem_kv_fwd")(mem_n, wkv)


def _mem_fwd(proj, kv, y_all, tm=512):
    m = kv.shape[0]

    def body(q_ref, gt_ref, kv_ref, y_in, y_ref):
        gt = gt_ref[...].astype(F32)
        sl = gt * _sigmoid(gt)
        hs = [slice(h * HD, (h + 1) * HD) for h in range(4)]
        s = [_dot_nt(q_ref[:, cs].astype(BF16), kv_ref[:, cs].astype(BF16)) * SCALE for cs in hs]
        e = [jnp.exp(s_ - jnp.max(s_, axis=-1, keepdims=True)) for s_ in s]
        p = [(e_ * (1.0 / jnp.sum(e_, axis=-1, keepdims=True))).astype(BF16) for e_ in e]
        for h, cs in enumerate(hs):
            o = _dot(p[h], kv_ref[:, W + h * HD:W + (h + 1) * HD].astype(BF16))
            y_ref[:, cs] = (o * sl[:, cs]).astype(BF16)

    return pl.pallas_call(
        body, grid=(S // tm,),
        in_specs=[pl.BlockSpec((tm, W), lambda i: (i, P_MQ)), pl.BlockSpec((tm, W), lambda i: (i, P_MGATE)),
                  pl.BlockSpec((m, 2 * W), lambda i: (0, 0)), ANY],
        out_specs=pl.BlockSpec((None, tm, W), lambda i: (3, i, 0)), out_shape=_sds((4, S, W), BF16),
        input_output_aliases={3: 0}, compiler_params=_cp("parallel"), name="mem_fwd")(proj, proj, kv, y_all)


def _mem_bwd(proj, kv, dy_all, dproj, tm=512):
    m = kv.shape[0]
    ni = S // tm

    def body(q_ref, gt_ref, kv_ref, dy_ref, dp_in, dp_ref, dkv_ref, obuf, osem):
        i = pl.program_id(0)

        @pl.when(i == 0)
        def _():
            dkv_ref[...] = jnp.zeros_like(dkv_ref)

        gt = gt_ref[...].astype(F32)
        sg = _sigmoid(gt)
        sl = gt * sg
        dsl = sg * (1.0 + gt * (1.0 - sg))
        dyv = dy_ref[...].astype(F32)

        def fill(slot):
            hs = [slice(h * HD, (h + 1) * HD) for h in range(4)]
            vss = [slice(W + h * HD, W + (h + 1) * HD) for h in range(4)]
            q = [q_ref[:, cs].astype(BF16) for cs in hs]
            k = [kv_ref[:, cs].astype(BF16) for cs in hs]
            v = [kv_ref[:, vs].astype(BF16) for vs in vss]
            dob = [(dyv[:, cs] * sl[:, cs]).astype(BF16) for cs in hs]
            s = [_dot_nt(q_, k_) * SCALE for q_, k_ in zip(q, k)]
            dp = [_dot_nt(d_, v_) for d_, v_ in zip(dob, v)]
            e = [jnp.exp(s_ - jnp.max(s_, axis=-1, keepdims=True)) for s_ in s]
            p = [e_ * (1.0 / jnp.sum(e_, axis=-1, keepdims=True)) for e_ in e]
            pb = [p_.astype(BF16) for p_ in p]
            dsb = [(p_ * (dp_ - jnp.sum(dp_ * p_, axis=-1, keepdims=True)) * SCALE).astype(BF16) for p_, dp_ in zip(p, dp)]
            for h, (cs, vs) in enumerate(zip(hs, vss)):
                o = _dot(pb[h], v[h])
                slot[:, cs] = _dot(dsb[h], k[h]).astype(BF16)
                slot[:, vs] = (dyv[:, cs] * o * dsl[:, cs]).astype(BF16)
                dkv_ref[:, cs] += _dot_tn(dsb[h], q[h])
                dkv_ref[:, vs] += _dot_tn(pb[h], dob[h])

        _tile_put(obuf, osem, lambda st: dp_ref.at[pl.ds(st * tm, tm), pl.ds(P_MQ * W, 2 * W)], i, ni, fill)

    return pl.pallas_call(
        body, grid=(ni,),
        in_specs=[pl.BlockSpec((tm, W), lambda i: (i, P_MQ)), pl.BlockSpec((tm, W), lambda i: (i, P_MGATE)),
                  pl.BlockSpec((m, 2 * W), lambda i: (0, 0)), pl.BlockSpec((None, tm, W), lambda i: (3, i, 0)), ANY],
        out_specs=[ANY, pl.BlockSpec((m, 2 * W), lambda i: (0, 0))],
        out_shape=[_sds((S, DIN), BF16), _sds((m, 2 * W), F32)],
        scratch_shapes=[pltpu.VMEM((2, tm, 2 * W), BF16), pltpu.SemaphoreType.DMA((2,))],
        input_output_aliases={4: 0}, compiler_params=_cp("arbitrary"), name="mem_bwd")(proj, proj, kv, dy_all, dproj)


def _mem_kv_bwd(mem, g, mem_n, wkv, dkv):
    m = mem.shape[0]

    def body(x_ref, g_ref, a_ref, w_ref, d_ref, dw_ref, dg_ref):
        db = d_ref[...].astype(BF16)
        dw_ref[...] = _dot_tn(a_ref[...], db).astype(BF16)
        dn = _dot_nt(db, w_ref[...])
        xv = x_ref[...]
        xh = xv * lax.rsqrt(jnp.mean(xv * xv, axis=-1, keepdims=True) + EPS)
        dg_ref[...] = jnp.sum(dn * xh, axis=0, keepdims=True)

    return pl.pallas_call(
        body, out_shape=[_sds((D, 2 * W), BF16), _sds((1, D), F32)], compiler_params=_cp(), name="mem_kv_bwd")(mem, g, mem_n, wkv, dkv)


def _layer_fwd(x, h, ht, mem_n, p, wg, g_next=None):
    win, wkv, wbr, wo = wg
    proj = _proj_fwd(h, win)
    y_all = lax.empty((4, S, W), BF16)
    y_all = _gmlp_fwd(proj, p["gm_ln_g"], p["gm_ln_b"], p["gm_ws"], p["gm_bias"], y_all)
    y_all = _pool_fwd(proj, p["pool_w"], p["pool_scale"], y_all)
    qs, ks, vs = _att_prep(proj)
    os_, ls_ = zip(*[_att_fwd(qs[g], ks[g], vs[g], g) for g in range(3)])
    y_all, om, lse = _att_mix(os_, ls_, proj, y_all)
    kv = _mem_kv_fwd(mem_n, wkv)
    y_all = _mem_fwd(proj, kv, y_all)
    z = _merge_fwd(y_all, wbr, proj)
    x_new = _out_fwd(z, wo, x) if g_next is None else _out_fwd_norm(z, wo, x, g_next)
    return x_new, dict(x=x, ht=ht, proj=proj, y_all=y_all, om=om, lse=lse, mem_n=mem_n, kv=kv, z=z, qkv=(qs, ks, vs))


GRAD_PARTS = ((2, D // 2, CW), (2, D // 8, 2 * W), (2, 2 * W, D // NCHIP), (2, D // 8, D))
SUM_TILE = (64, 128, 256, 128)
ADAM_TILE = (128, 256, 2048, 256)
PLACE_TILE = (256, 256, 512, 256)


def _layer_bwd(dx, mem, p, wg, sv, exchange, dep):
    win, wkv, wbr, wo = wg
    proj = sv["proj"]
    dz, d_wo = _out_bwd(dx, sv["z"], wo, dep)
    dproj = lax.empty((S, DIN), BF16)
    dproj, dy_all, d_wbr = _merge_bwd(dz, sv["y_all"], wbr, proj, dproj)
    dproj, d_ws, d_bs, d_lg, d_lb = _gmlp_bwd(proj, dy_all, p["gm_ln_g"], p["gm_ln_b"], p["gm_ws"], p["gm_bias"], dproj)
    dproj, d_pw, d_sc = _pool_bwd(proj, dy_all, p["pool_w"], p["pool_scale"], dproj)
    dos, dls, lss, dproj = _att_bwd_pre(dy_all, proj, sv["om"], sv["lse"], dproj)
    qs, ks, vs = sv["qkv"]
    dqs, dks, dvs = zip(*[_att_bwd(qs[g], ks[g], vs[g], dos[g], lss[g], dls[g], g) for g in range(3)])
    dproj = _att_bwd_post(dqs, dks, dvs, dproj)
    dproj, dkv = _mem_bwd(proj, sv["kv"], dy_all, dproj)
    d_wkv, d_mg = _mem_kv_bwd(mem, p["mem_norm_g"], sv["mem_n"], wkv, dkv)
    d_win = _proj_bwd_w(sv["ht"], dproj)
    big = tuple(t.reshape((NCHIP,) + s) for t, s in zip((d_win, d_wkv, d_wbr, d_wo), GRAD_PARTS))
    inflight = exchange(big)
    dh = _proj_bwd_x(dproj, win, inflight[-1])
    dx_in, d_ng = _rms_bwd(dh, sv["x"], p["norm_g"], dx)
    small = dict(norm_g=d_ng, gm_ln_g=d_lg, gm_ln_b=d_lb, gm_ws=d_ws, gm_bs=d_bs[:, :, 0], pool_w=d_pw, pool_scale=d_sc, mem_norm_g=d_mg)
    return dx_in, (big,) + inflight, small


_SMALL = ("norm_g", "gm_ln_g", "gm_ln_b", "gm_ws", "gm_bs", "pool_w", "pool_scale", "mem_norm_g")


def _layer_params(l, norm_g, gm_ln_g, gm_ln_b, gm_ws, gm_bs, pool_w, pool_scale, mem_norm_g):
    return dict(norm_g=norm_g[l][None], gm_ln_g=gm_ln_g[l][None], gm_ln_b=gm_ln_b[l][None], gm_ws=gm_ws[l],
                gm_bias=jnp.broadcast_to(gm_bs[l][:, :, None], (4, HD, HD)), pool_w=pool_w[l],
                pool_scale=pool_scale[l][None], mem_norm_g=mem_norm_g[l][None])


def kernel(x, mem, norm_g, w_in, gm_ln_g, gm_ln_b, gm_ws, gm_bs, pool_w, pool_scale, mem_norm_g, w_mem_kv, w_branch, w_out, final_norm_g, loss_target, m_norm_g, m_w_in, m_gm_ln_g, m_gm_ln_b, m_gm_ws, m_gm_bs, m_pool_w, m_pool_scale, m_mem_norm_g, m_w_mem_kv, m_w_branch, m_w_out, m_final_norm_g, v_norm_g, v_w_in, v_gm_ln_g, v_gm_ln_b, v_gm_ws, v_gm_bs, v_pool_w, v_pool_scale, v_mem_norm_g, v_w_mem_kv, v_w_branch, v_w_out, v_final_norm_g):
    xs, memv, tgt = x[0], mem[0], loss_target[0]
    params = [_layer_params(l, norm_g, gm_ln_g, gm_ln_b, gm_ws, gm_bs, pool_w, pool_scale, mem_norm_g) for l in range(NL)]

    large = (w_in, w_mem_kv, w_branch, w_out)
    first = tuple(t[:1].astype(BF16) for t in large)
    lands, gsems, after = _gather_start(first, True)
    rest = tuple((t[1:] + after[0, 0]).astype(BF16) for t in large)
    lands_r, gsems_r, after_r = _gather_start(rest, False)
    lands, gsems = lands + lands_r, gsems + gsems_r

    saved, wgs = [], []
    mem_ns = [_rms_fwd(memv, params[l]["mem_norm_g"], memv.shape[0]) for l in range(NL)]
    for l in range(NL):
        if l == 0:
            h, ht = _rms_fwd_t(xs, params[0]["norm_g"])
        if l == 0:
            got, relay_sems = _gather_relay(first, lands[0], gsems[0], [after, after_r, h] + mem_ns)
            got = _gather_wait_relay(got, relay_sems)
            got = [_place_own(got[k], first[k], 0, PLACE_TILE[k]) for k in range(4)]
        else:
            got = _gather_wait(f"gather_wait_{l}", l - 1, rest, lands[l], gsems[l], after)
            got = [_place_own(got[k], rest[k], l - 1, PLACE_TILE[k]) for k in range(4)]
        wgs.append((got[0], got[1].reshape(D, 2 * W), got[2], got[3].reshape(D, D)))
        g_next = params[l + 1]["norm_g"] if l + 1 < NL else None
        xs, sv = _layer_fwd(xs, h, ht, mem_ns[l], params[l], wgs[l], g_next)
        if g_next is not None:
            xs, h, ht = xs
        saved.append(sv)
        after = xs
    dx, d_fg, ls = _loss_head(xs, tgt, final_norm_g[None])
    loss = lax.psum(ls[0, 0], ("x", "y", "c"))

    def pack(leaves):
        parts, spans, at = [], [], 0
        for t in leaves:
            r = t.reshape(-1, 128)
            pad = -r.shape[0] % 8
            parts.append(jnp.pad(r, ((0, pad), (0, 0))) if pad else r)
            spans.append((at, at + r.shape[0]))
            at += r.shape[0] + pad
        return jnp.concatenate(parts, axis=0), spans

    flight, small = [None] * NL, [None] * NL
    dep = dx
    for l in reversed(range(NL)):
        dx, flight[l], small[l] = _layer_bwd(dx, memv, params[l], wgs[l], saved[l], functools.partial(_exch_start, l), dep)
        dep = dx
        if l == 1:
            packed_a, spans_a = pack([jnp.stack([small[j][n] for j in range(1, NL)]) for n in _SMALL] + [d_fg])
            zone_a, sems_a, dep = _small_start(packed_a, "small_start_early")
    grad_x = dx[None]
    packed_b, spans_b = pack([small[0][n] for n in _SMALL])
    zone_b, sems_b, after = _small_start(packed_b, "small_start_last")

    ws = dict(norm_g=norm_g, w_in=w_in, gm_ln_g=gm_ln_g, gm_ln_b=gm_ln_b, gm_ws=gm_ws, gm_bs=gm_bs, pool_w=pool_w,
              pool_scale=pool_scale, mem_norm_g=mem_norm_g, w_mem_kv=w_mem_kv, w_branch=w_branch, w_out=w_out,
              final_norm_g=final_norm_g)
    ms = dict(norm_g=m_norm_g, w_in=m_w_in, gm_ln_g=m_gm_ln_g, gm_ln_b=m_gm_ln_b, gm_ws=m_gm_ws, gm_bs=m_gm_bs,
              pool_w=m_pool_w, pool_scale=m_pool_scale, mem_norm_g=m_mem_norm_g, w_mem_kv=m_w_mem_kv,
              w_branch=m_w_branch, w_out=m_w_out, final_norm_g=m_final_norm_g)
    vs = dict(norm_g=v_norm_g, w_in=v_w_in, gm_ln_g=v_gm_ln_g, gm_ln_b=v_gm_ln_b, gm_ws=v_gm_ws, gm_bs=v_gm_bs,
              pool_w=v_pool_w, pool_scale=v_pool_scale, mem_norm_g=v_mem_norm_g, w_mem_kv=v_w_mem_kv,
              w_branch=v_w_branch, w_out=v_w_out, final_norm_g=v_final_norm_g)

    big = ("w_in", "w_mem_kv", "w_branch", "w_out")
    acc = {n: [lax.empty((ws[n].size // ws[n].shape[-1], ws[n].shape[-1]), F32) for _ in range(4)] for n in big}
    for l in reversed(range(NL)):
        parts, zones, sems, _ = flight[l]
        zones = _exch_wait(l, parts, zones, sems, after)
        full = _share_full([_sum_half(parts[k], zones[k], SUM_TILE[k]) for k in range(4)])
        for k, n in enumerate(big):
            acc[n] = _adamw_layer(l, ws[n], full[k], ms[n], vs[n], acc[n], ADAM_TILE[k])
        after = acc["w_in"][0]
    grads, upd = {}, {}
    for n in big:
        d_, m_, v_, g_ = (t.reshape(ws[n].shape) for t in acc[n])
        grads[n], upd[n] = g_, (d_, m_, v_)

    def row_tile(rows):
        return rows if rows <= 1536 else max(t for t in range(8, 513, 8) if rows % t == 0)

    zone_a = _small_wait(packed_a, zone_a, sems_a, after, "small_wait_early")
    zone_b = _small_wait(packed_b, zone_b, sems_b, zone_a, "small_wait_last")
    tot_a = _sum_small(packed_a, zone_a, row_tile(packed_a.shape[0]))
    tot_b = _sum_small(packed_b, zone_b, row_tile(packed_b.shape[0]))
    for i, n in enumerate(_SMALL):
        later = tot_a[spans_a[i][0]:spans_a[i][1]].reshape((NL - 1,) + ws[n].shape[1:])
        first_l = tot_b[spans_b[i][0]:spans_b[i][1]].reshape(ws[n].shape[1:])
        grads[n] = jnp.stack([first_l] + [later[j] for j in range(NL - 1)])
    grads["final_norm_g"] = tot_a[spans_a[-1][0]:spans_a[-1][1]].reshape(ws["final_norm_g"].shape)
    for n in _SMALL + ("final_norm_g",):
        upd[n] = _adamw(ws[n], grads[n], ms[n], vs[n])
    order = ("norm_g", "w_in", "gm_ln_g", "gm_ln_b", "gm_ws", "gm_bs", "pool_w", "pool_scale", "mem_norm_g", "w_mem_kv",
             "w_branch", "w_out", "final_norm_g")
    return (loss, grad_x, *[grads[n] for n in order], *[upd[n][0] for n in order], *[upd[n][1] for n in order],
            *[upd[n][2] for n in order])
```
